```python
import jax, jax.numpy as jnp
from jax import lax
import numpy as np

D_MODEL = 4096
BATCH = 1
SEQ = 8192
DEPTH = 2
DEC_BATCH = 8
DEC_SEQ = 64
PAST_LEN = 4096

CHUNK = 64
Q_BLOCK = 128
H_SB = 16
DH_SB = 128
SB_W = H_SB * DH_SB
H_MLA = 16
QK_NOPE = 128
QK_ROPE = 64
V_DIM = 128
Q_LORA = 1024
KV_LORA = 512
ROPE_THETA = 10000.0
N_GROUPS = 4
EXPERTS_PER_GROUP = 8
N_EXPERTS = N_GROUPS * EXPERTS_PER_GROUP
TOP_K_IN_GROUP = 2
D_EXPERT = 256
N_MOD = 6
EPS = 1e-6
N_IN = 3 * SB_W + Q_LORA + KV_LORA + QK_ROPE + 2 * D_MODEL

kernel_name = "stickbreak_mla_hiermoe_adaln_stream"


def _split_points(sizes):
    pts, acc = [], 0
    for s in sizes[:-1]:
        acc += s
        pts.append(acc)
    return pts


def rmsnorm(x, g):
    xf = x.astype(jnp.float32)
    y = xf * lax.rsqrt(jnp.mean(xf * xf, axis=-1, keepdims=True) + EPS)
    return (y * g.astype(jnp.float32)).astype(x.dtype)


def rope_cos_sin(pos):
    inv = ROPE_THETA ** (-jnp.arange(0, QK_ROPE, 2, dtype=jnp.float32) / QK_ROPE)
    ang = pos.astype(jnp.float32)[:, None] * inv[None, :]
    return jnp.cos(ang), jnp.sin(ang)


def apply_rope(x, cos, sin):
    xf = x.astype(jnp.float32)
    x1, x2 = jnp.split(xf, 2, axis=-1)
    return jnp.concatenate([x1 * cos - x2 * sin, x2 * cos + x1 * sin], axis=-1).astype(x.dtype)


def sweep_query_blocks(attend, qs, q_pos):
    t = q_pos.shape[0]
    if t <= Q_BLOCK:
        return attend(qs, q_pos)
    nb = t // Q_BLOCK

    def to_blocks(a):
        a = a.reshape(a.shape[0], nb, Q_BLOCK, *a.shape[2:])
        return jnp.moveaxis(a, 1, 0)

    out = lax.map(lambda args: attend(args[0], args[1]),
                  (tuple(to_blocks(a) for a in qs), q_pos.reshape(nb, Q_BLOCK)))
    out = jnp.moveaxis(out, 0, 1)
    return out.reshape(out.shape[0], t, *out.shape[3:])


def stick_breaking_attend(q, k, v, q_pos, k_pos):
    z = jnp.einsum('bqhd,bshd->bhqs', q, k).astype(jnp.float32) * (DH_SB ** -0.5)
    valid = k_pos[None, :] < q_pos[:, None]
    log_not = jnp.where(valid, jax.nn.log_sigmoid(-z), 0.0)
    after = lax.cumsum(log_not, axis=3, reverse=True) - log_not
    w = jnp.where(valid, jnp.exp(jax.nn.log_sigmoid(z) + after), 0.0)
    return jnp.einsum('bhqs,bshd->bqhd', w.astype(v.dtype), v)


def mla_attend(q_nope, q_rope, k_nope, k_rope, v, q_pos, k_pos):
    s = (jnp.einsum('bqhd,bshd->bhqs', q_nope, k_nope)
         + jnp.einsum('bqhr,bsr->bhqs', q_rope, k_rope)).astype(jnp.float32)
    s = s * ((QK_NOPE + QK_ROPE) ** -0.5)
    visible = (k_pos // CHUNK)[None, :] <= (q_pos // CHUNK)[:, None]
    p = jax.nn.softmax(jnp.where(visible, s, -jnp.inf), axis=-1)
    return jnp.einsum('bhqs,bshd->bqhd', p.astype(v.dtype), v)


def parallel_mixer(h, pos, past, p, l):
    b, t, _ = h.shape
    proj = h @ p['w_in'][l]
    sizes = (SB_W, SB_W, SB_W, Q_LORA, KV_LORA, QK_ROPE, D_MODEL, D_MODEL)
    sb_q, sb_k, sb_v, c_q, c_kv, k_r, gate_sb, gate_mla = jnp.split(proj, _split_points(sizes), axis=-1)
    sb_q = sb_q.reshape(b, t, H_SB, DH_SB)
    sb_k = sb_k.reshape(b, t, H_SB, DH_SB)
    sb_v = sb_v.reshape(b, t, H_SB, DH_SB)
    c_q = rmsnorm(c_q, p['g_q_lat'][l])
    c_kv = rmsnorm(c_kv, p['g_kv_lat'][l])
    cos, sin = rope_cos_sin(pos)
    q = (c_q @ p['w_uq'][l]).reshape(b, t, H_MLA, QK_NOPE + QK_ROPE)
    q_nope = q[..., :QK_NOPE]
    q_rope = apply_rope(q[..., QK_NOPE:], cos[:, None, :], sin[:, None, :])
    k_rope = apply_rope(k_r, cos, sin)
    new_state = (sb_k, sb_v, c_kv, k_rope)

    if past is None:
        k_all, v_all, ckv_all, kr_all, k_pos = sb_k, sb_v, c_kv, k_rope, pos
    else:
        pk, pv, pc, pr = past
        k_all = jnp.concatenate([pk.astype(sb_k.dtype), sb_k], axis=1)
        v_all = jnp.concatenate([pv.astype(sb_v.dtype), sb_v], axis=1)
        ckv_all = jnp.concatenate([pc.astype(c_kv.dtype), c_kv], axis=1)
        kr_all = jnp.concatenate([pr.astype(k_rope.dtype), k_rope], axis=1)
        k_pos = jnp.concatenate([jnp.arange(pk.shape[1], dtype=jnp.int32), pos])
    s_len = ckv_all.shape[1]
    kv = (ckv_all @ p['w_ukv'][l]).reshape(b, s_len, H_MLA, QK_NOPE + V_DIM)
    k_nope, v_mla = kv[..., :QK_NOPE], kv[..., QK_NOPE:]

    o_sb = sweep_query_blocks(
        lambda qs, qp: stick_breaking_attend(qs[0], k_all, v_all, qp, k_pos), (sb_q,), pos)
    o_mla = sweep_query_blocks(
        lambda qs, qp: mla_attend(qs[0], qs[1], k_nope, kr_all, v_mla, qp, k_pos), (q_nope, q_rope), pos)

    y_sb = o_sb.reshape(b, t, SB_W) @ p['w_branch_sb'][l]
    y_mla = o_mla.reshape(b, t, H_MLA * V_DIM) @ p['w_branch_mla'][l]
    merged = jax.nn.sigmoid(gate_sb) * y_sb + jax.nn.sigmoid(gate_mla) * y_mla
    return merged @ p['w_out'][l], new_state


def hier_moe(h, p, l):
    b, t, d = h.shape
    xt = h.reshape(b * t, d)
    g_logit = (xt @ p['w_router_group'][l]).astype(jnp.float32) + p['b_router_group'][l].astype(jnp.float32)
    g_prob = jax.nn.softmax(g_logit, axis=-1)
    _, g_idx = lax.top_k(g_logit, 1)
    p_group = jnp.take_along_axis(g_prob, g_idx, axis=-1)
    e_logit = ((xt @ p['w_router_expert'][l]).astype(jnp.float32)
               + p['b_router_expert'][l].astype(jnp.float32)).reshape(-1, N_GROUPS, EXPERTS_PER_GROUP)
    e_in_group = jnp.take_along_axis(e_logit, g_idx[:, :, None], axis=1)[:, 0]
    e_top, e_idx = lax.top_k(e_in_group, TOP_K_IN_GROUP)
    weights = jax.nn.softmax(e_top, axis=-1) * p_group
    expert = g_idx * EXPERTS_PER_GROUP + e_idx
    combine = jnp.sum(jax.nn.one_hot(expert, N_EXPERTS, dtype=jnp.float32) * weights[..., None], axis=1)
    a = jnp.einsum('nd,edf->nef', xt, p['w_exp_gate'][l])
    u = jnp.einsum('nd,edf->nef', xt, p['w_exp_up'][l])
    hid = jax.nn.silu(a) * u * combine.astype(h.dtype)[:, :, None]
    return jnp.einsum('nef,efd->nd', hid, p['w_exp_down'][l]).reshape(b, t, d)


def trunk(x, c, pos, caches, p):
    new = ([], [], [], [])
    for l in range(DEPTH):
        mod = (jax.nn.silu(c) @ p['w_ada'][l] + p['b_ada'][l])[:, None, :]
        sh1, sc1, g1, sh2, sc2, g2 = jnp.split(mod, N_MOD, axis=-1)
        h = rmsnorm(x, p['g_norm_mix'][l]) * (1 + sc1) + sh1
        past = None if caches is None else tuple(cc[l] for cc in caches)
        y, st = parallel_mixer(h, pos, past, p, l)
        x = x + g1 * y
        h = rmsnorm(x, p['g_norm_ffn'][l]) * (1 + sc2) + sh2
        x = x + g2 * hier_moe(h, p, l)
        for lst, s in zip(new, st):
            lst.append(s)
    y = rmsnorm(x, p['g_final'])
    return y, tuple(jnp.stack(s) for s in new)


def setup_inputs(seed: int = 0) -> dict:
    key = jax.random.key(seed)
    ks = jax.random.split(key, 32)
    f32 = jnp.float32

    def nrm(k, shape, scale):
        return jax.random.normal(k, shape, f32) * scale

    def gain(k, shape):
        return 1.0 + 0.02 * jax.random.normal(k, shape, f32)

    return {
        "x_prompt": nrm(ks[0], (BATCH, SEQ, D_MODEL), 1.0),
        "x_sample": nrm(ks[1], (DEC_BATCH, DEC_SEQ, D_MODEL), 1.0),
        "c_prompt": nrm(ks[2], (BATCH, D_MODEL), 1.0),
        "c_sample": nrm(ks[3], (DEC_BATCH, D_MODEL), 1.0),
        "cache_sb_k": nrm(ks[4], (DEPTH, DEC_BATCH, PAST_LEN, H_SB, DH_SB), 1.0),
        "cache_sb_v": nrm(ks[5], (DEPTH, DEC_BATCH, PAST_LEN, H_SB, DH_SB), 1.0),
        "cache_mla_ckv": nrm(ks[6], (DEPTH, DEC_BATCH, PAST_LEN, KV_LORA), 1.0),
        "cache_mla_krope": nrm(ks[7], (DEPTH, DEC_BATCH, PAST_LEN, QK_ROPE), 1.0),
        "w_ada": nrm(ks[8], (DEPTH, D_MODEL, N_MOD * D_MODEL), 0.5 * D_MODEL ** -0.5),
        "b_ada": nrm(ks[9], (DEPTH, N_MOD * D_MODEL), 0.02),
        "g_norm_mix": gain(ks[10], (DEPTH, D_MODEL)),
        "g_norm_ffn": gain(ks[11], (DEPTH, D_MODEL)),
        "w_in": nrm(ks[12], (DEPTH, D_MODEL, N_IN), D_MODEL ** -0.5),
        "g_q_lat": gain(ks[13], (DEPTH, Q_LORA)),
        "g_kv_lat": gain(ks[14], (DEPTH, KV_LORA)),
        "w_uq": nrm(ks[15], (DEPTH, Q_LORA, H_MLA * (QK_NOPE + QK_ROPE)), Q_LORA ** -0.5),
        "w_ukv": nrm(ks[16], (DEPTH, KV_LORA, H_MLA * (QK_NOPE + V_DIM)), KV_LORA ** -0.5),
        "w_branch_sb": nrm(ks[17], (DEPTH, SB_W, D_MODEL), SB_W ** -0.5),
        "w_branch_mla": nrm(ks[18], (DEPTH, H_MLA * V_DIM, D_MODEL), (H_MLA * V_DIM) ** -0.5),
        "w_out": nrm(ks[19], (DEPTH, D_MODEL, D_MODEL), D_MODEL ** -0.5),
        "w_router_group": nrm(ks[20], (DEPTH, D_MODEL, N_GROUPS), D_MODEL ** -0.5),
        "b_router_group": nrm(ks[21], (DEPTH, N_GROUPS), 0.01),
        "w_router_expert": nrm(ks[22], (DEPTH, D_MODEL, N_EXPERTS), D_MODEL ** -0.5),
        "b_router_expert": nrm(ks[23], (DEPTH, N_EXPERTS), 0.01),
        "w_exp_gate": nrm(ks[24], (DEPTH, N_EXPERTS, D_MODEL, D_EXPERT), D_MODEL ** -0.5),
        "w_exp_up": nrm(ks[25], (DEPTH, N_EXPERTS, D_MODEL, D_EXPERT), D_MODEL ** -0.5),
        "w_exp_down": nrm(ks[26], (DEPTH, N_EXPERTS, D_EXPERT, D_MODEL), D_EXPERT ** -0.5),
        "g_final": gain(ks[27], (D_MODEL,)),
    }


def reference(x_prompt, x_sample, c_prompt, c_sample, cache_sb_k, cache_sb_v, cache_mla_ckv,
              cache_mla_krope, w_ada, b_ada, g_norm_mix, g_norm_ffn, w_in, g_q_lat, g_kv_lat,
              w_uq, w_ukv, w_branch_sb, w_branch_mla, w_out, w_router_group, b_router_group,
              w_router_expert, b_router_expert, w_exp_gate, w_exp_up, w_exp_down, g_final):
    p = dict(w_ada=w_ada, b_ada=b_ada, g_norm_mix=g_norm_mix, g_norm_ffn=g_norm_ffn, w_in=w_in,
             g_q_lat=g_q_lat, g_kv_lat=g_kv_lat, w_uq=w_uq, w_ukv=w_ukv, w_branch_sb=w_branch_sb,
             w_branch_mla=w_branch_mla, w_out=w_out, w_router_group=w_router_group,
             b_router_group=b_router_group, w_router_expert=w_router_expert,
             b_router_expert=b_router_expert, w_exp_gate=w_exp_gate, w_exp_up=w_exp_up,
             w_exp_down=w_exp_down, g_final=g_final)
    pos_p = jnp.arange(x_prompt.shape[1], dtype=jnp.int32)
    y_prompt, (pk, pv, pc, pr) = trunk(x_prompt, c_prompt, pos_p, None, p)
    past_len = cache_sb_k.shape[2]
    pos_s = past_len + jnp.arange(x_sample.shape[1], dtype=jnp.int32)
    caches = (cache_sb_k, cache_sb_v, cache_mla_ckv, cache_mla_krope)
    y_sample, (sk, sv, sc, sr) = trunk(x_sample, c_sample, pos_s, caches, p)
    return (y_prompt, y_sample, pk, pv, pc, pr, sk, sv, sc, sr)
```

```python
import functools

import numpy as np
import jax
import jax.numpy as jnp
from jax import lax
from jax.experimental import pallas as pl
from jax.experimental.pallas import tpu as pltpu

F32 = jnp.float32
BF16 = jnp.bfloat16

CHUNK = 64
H_SB = 16
DH_SB = 128
H_MLA = 16
QK_NOPE = 128
QK_ROPE = 64
V_DIM = 128
ROPE_THETA = 10000.0
N_GROUPS = 4
EXPERTS_PER_GROUP = 8
N_EXPERTS = N_GROUPS * EXPERTS_PER_GROUP
N_MOD = 6
EPS = 1e-6

LANES = 128
ATTN_TILE = 256
MOE_TILE = 256
SB_DEAD = 104.0
VMEM_LIMIT = 56 * 1024 * 1024


def _cparams(n_axes, vmem=VMEM_LIMIT):
    return pltpu.CompilerParams(dimension_semantics=("arbitrary",) * n_axes,
                                vmem_limit_bytes=vmem)


def _row_tile(n, cap=512):
    t = cap
    while n % t:
        t //= 2
    return t


def _matmul(pairs, *, m, n_out, tm, tn, epilogue, out_shape, out_specs,
            extras=(), extra_specs=(), prologue=None):
    n_pairs = len(pairs)
    n_ex = len(extras)
    n_outs = len(out_shape)
    cast = [p[3] for p in pairs]

    def kern(*refs):
        a_refs = refs[0:2 * n_pairs:2]
        b_refs = refs[1:2 * n_pairs:2]
        ex = refs[2 * n_pairs:2 * n_pairs + n_ex]
        outs = refs[2 * n_pairs + n_ex:2 * n_pairs + n_ex + n_outs]
        scr = refs[2 * n_pairs + n_ex + n_outs:]
        i = pl.program_id(1)
        accs = []
        si = 0
        for p in range(n_pairs):
            if cast[p]:
                bsc = scr[si]
                si += 1

                @pl.when(i == 0)
                def _(bsc=bsc, b_ref=b_refs[p]):
                    bsc[...] = b_ref[...].astype(BF16)

                bv = bsc[...]
            else:
                bv = b_refs[p][...]
            a = a_refs[p][...]
            if prologue is not None:
                a = prologue(a, ex)
            accs.append(jnp.dot(a.astype(BF16), bv, preferred_element_type=F32))
        epilogue(accs, ex, outs)

    in_specs, args, scratch = [], [], []
    for (a, b, b_spec, cb) in pairs:
        k = a.shape[1]
        in_specs += [pl.BlockSpec((tm, k), lambda j, i: (i, 0)), b_spec]
        args += [a, b]
        if cb:
            scratch.append(pltpu.VMEM((k, tn), BF16))
    in_specs += list(extra_specs)
    args += list(extras)
    return pl.pallas_call(
        kern,
        grid=(n_out // tn, m // tm),
        in_specs=in_specs,
        out_specs=out_specs,
        out_shape=out_shape,
        scratch_shapes=scratch,
        compiler_params=_cparams(2),
    )(*args)


def _wspec(l, k, tn, col_off):
    cb = col_off // tn
    assert cb * tn == col_off
    return pl.BlockSpec((None, k, tn), lambda j, i: (l, 0, cb + j))


def _spec2(tm, tn):
    return pl.BlockSpec((tm, tn), lambda j, i: (i, j))


def _hm_spec(tm, tn):
    return pl.BlockSpec((tn // LANES, tm, LANES), lambda j, i: (j, i, 0))


def _store_heads(o_ref, val):
    for c in range(val.shape[1] // LANES):
        o_ref[c] = val[:, c * LANES:(c + 1) * LANES].astype(o_ref.dtype)


def _rms(x, g):
    return x * lax.rsqrt(jnp.mean(x * x, axis=-1, keepdims=True) + EPS) * g


def _norm_mod_kernel(x_ref, g_ref, sc_ref, sh_ref, o_ref):
    x = x_ref[...]
    y = _rms(x, g_ref[...])
    o_ref[...] = (y * (1.0 + sc_ref[...]) + sh_ref[...]).astype(o_ref.dtype)


def _split3(x):
    hi = x.astype(BF16)
    r = x - hi.astype(F32)
    mid = r.astype(BF16)
    lo = (r - mid.astype(F32)).astype(BF16)
    return hi, mid, lo


def _dot_f32(a, b3):
    a_hi, a_mid, a_lo = _split3(a)
    b_hi, b_mid, b_lo = b3
    d = functools.partial(jnp.dot, preferred_element_type=F32)
    small = d(a_hi, b_lo) + d(a_lo, b_hi) + d(a_mid, b_mid)
    return (d(a_hi, b_hi) + (d(a_hi, b_mid) + d(a_mid, b_hi))) + small


def _route(logits):
    lane = lax.broadcasted_iota(jnp.int32, logits.shape, 1)
    lanef = lane.astype(F32)
    big = jnp.float32(1e9)
    ninf = jnp.float32(-jnp.inf)
    is_g = (lane >= N_EXPERTS) & (lane < N_EXPERTS + N_GROUPS)
    gl = jnp.where(is_g, logits, ninf)
    gmax = jnp.max(gl, axis=1, keepdims=True)
    g_idx = jnp.min(jnp.where(gl == gmax, lanef - N_EXPERTS, big), axis=1, keepdims=True)
    p_group = 1.0 / jnp.sum(jnp.where(is_g, jnp.exp(gl - gmax), 0.0), axis=1, keepdims=True)
    grp = jnp.floor(lanef * (1.0 / EXPERTS_PER_GROUP))
    in_g = (lane < N_EXPERTS) & (grp == g_idx)
    el = jnp.where(in_g, logits, ninf)
    e1 = jnp.max(el, axis=1, keepdims=True)
    i1 = jnp.min(jnp.where(el == e1, lanef, big), axis=1, keepdims=True)
    el2 = jnp.where(lanef == i1, ninf, el)
    e2 = jnp.max(el2, axis=1, keepdims=True)
    i2 = jnp.min(jnp.where(el2 == e2, lanef, big), axis=1, keepdims=True)
    t = jnp.exp(e2 - e1)
    den = 1.0 + t
    w1 = (1.0 / den) * p_group
    w2 = (t / den) * p_group
    out = jnp.where(lane == 0, i1, jnp.where(lane == 1, i2,
          jnp.where(lane == 2, w1, jnp.where(lane == 3, w2, 0.0))))
    return out


def _norm_route_kernel(x_ref, g_ref, sc_ref, sh_ref, wr_ref, br_ref, h_ref, r_ref):
    x = x_ref[...]
    y = _rms(x, g_ref[...])
    h = y * (1.0 + sc_ref[...]) + sh_ref[...]
    h_ref[...] = h
    b3 = (wr_ref[0], wr_ref[1], wr_ref[2])
    for g in range(x.shape[0]):
        logits = _dot_f32(h[g], b3) + br_ref[...]
        r_ref[g] = _route(logits)


def _mod_spec(idx, gb, d):
    return pl.BlockSpec((None, gb, 1, d), lambda i: (idx, i, 0, 0))


def _norm_mod(x3, g, modg, l, sc_i, sh_i):
    ng, grp, d = x3.shape
    gb = _row_tile(ng, 4)
    return pl.pallas_call(
        _norm_mod_kernel,
        grid=(ng // gb,),
        in_specs=[pl.BlockSpec((gb, grp, d), lambda i: (i, 0, 0)),
                  pl.BlockSpec((None, 1, d), lambda i: (l, 0, 0)),
                  _mod_spec(sc_i, gb, d), _mod_spec(sh_i, gb, d)],
        out_specs=pl.BlockSpec((gb, grp, d), lambda i: (i, 0, 0)),
        out_shape=jax.ShapeDtypeStruct(x3.shape, BF16),
        compiler_params=_cparams(1),
    )(x3, g, modg, modg)


def _norm_route(x3, g, modg, l, sc_i, sh_i, wr3, br):
    ng, grp, d = x3.shape
    gb = _row_tile(ng, 4)
    return pl.pallas_call(
        _norm_route_kernel,
        grid=(ng // gb,),
        in_specs=[pl.BlockSpec((gb, grp, d), lambda i: (i, 0, 0)),
                  pl.BlockSpec((None, 1, d), lambda i: (l, 0, 0)),
                  _mod_spec(sc_i, gb, d), _mod_spec(sh_i, gb, d),
                  pl.BlockSpec((None, 3, d, LANES), lambda i: (l, 0, 0, 0)),
                  pl.BlockSpec((None, 1, LANES), lambda i: (l, 0, 0))],
        out_specs=[pl.BlockSpec((gb, grp, d), lambda i: (i, 0, 0)),
                   pl.BlockSpec((gb, grp, LANES), lambda i: (i, 0, 0))],
        out_shape=[jax.ShapeDtypeStruct(x3.shape, F32),
                   jax.ShapeDtypeStruct((ng, grp, LANES), F32)],
        compiler_params=_cparams(1),
    )(x3, g, modg, modg, wr3, br)


def _final_norm_kernel(x_ref, g_ref, o_ref):
    o_ref[...] = _rms(x_ref[...], g_ref[...])


def _final_norm(x3, g):
    ng, grp, d = x3.shape
    gb = _row_tile(ng, 4)
    return pl.pallas_call(
        _final_norm_kernel,
        grid=(ng // gb,),
        in_specs=[pl.BlockSpec((gb, grp, d), lambda i: (i, 0, 0)),
                  pl.BlockSpec((1, d), lambda i: (0, 0))],
        out_specs=pl.BlockSpec((gb, grp, d), lambda i: (i, 0, 0)),
        out_shape=jax.ShapeDtypeStruct(x3.shape, F32),
        compiler_params=_cparams(1),
    )(x3, g)


def _lanes(c, w):
    if w % LANES == 0:
        return c if w == LANES else jnp.tile(c, (1, w // LANES))
    return c[:, :w]


def _dot_nt(a, b):
    return lax.dot_general(a, b, (((1,), (1,)), ((), ())), preferred_element_type=F32)


def _sb_block(qh, kh, vh, c, u, scale, masked):
    w = kh.shape[0]
    z = _dot_nt(qh, kh) * scale
    sp = jnp.maximum(z, 0.0) + jnp.log1p(jnp.exp(-jnp.abs(z)))
    if masked:
        row = lax.broadcasted_iota(jnp.int32, z.shape, 0)
        col = lax.broadcasted_iota(jnp.int32, z.shape, 1)
        valid = col < row
        sp = jnp.where(valid, sp, 0.0)
    hi = sp.astype(BF16)
    lo = (sp - hi.astype(F32)).astype(BF16)
    cs = jnp.dot(hi, u, preferred_element_type=F32) + jnp.dot(lo, u, preferred_element_type=F32)
    wgt = jnp.exp(z - sp - cs - _lanes(c, w))
    if masked:
        wgt = jnp.where(valid, wgt, 0.0)
    o = jnp.dot(wgt.astype(BF16), vh, preferred_element_type=F32)
    c_new = c + jnp.sum(sp, axis=1, keepdims=True)
    return o, c_new


def _sb_kernel(qb_ref, kb_ref, hb_ref, fl_ref, q_ref, kn_ref, vn_ref, kp_ref, vp_ref,
               ud_ref, up_ref, o_ref, acc, carry, done, *, heads, scale, sub):
    s = pl.program_id(0)
    fl = fl_ref[s]
    is_first = (fl & 1) != 0
    is_last = (fl & 2) != 0

    def head_ref(ref, h):
        return ref[h] if len(ref.shape) == 3 else ref[...]

    @pl.when(is_first)
    def _():
        def body(h, _):
            o, c = _sb_block(q_ref[h], kn_ref[h], vn_ref[h], jnp.zeros(carry.shape[1:], F32),
                             ud_ref[...], scale, True)
            acc[h] = o
            carry[h] = c
            done[h] = 0
            return 0
        lax.fori_loop(0, heads, body, 0)

    @pl.when(jnp.logical_not(is_first))
    def _():
        def body(h, _):
            @pl.when(done[h] == 0)
            def _():
                kp = head_ref(kp_ref, h)
                vp = head_ref(vp_ref, h)
                qh = q_ref[h]
                c = carry[h]
                o_tot = acc[h]
                n_sub = kp.shape[0] // sub
                for j in reversed(range(n_sub)):
                    kh = kp[j * sub:(j + 1) * sub].astype(BF16)
                    vh = vp[j * sub:(j + 1) * sub].astype(BF16)
                    o, c = _sb_block(qh, kh, vh, c, up_ref[...], scale, False)
                    o_tot = o_tot + o
                acc[h] = o_tot
                carry[h] = c
                done[h] = (jnp.min(c) >= SB_DEAD).astype(jnp.int32)
            return 0
        lax.fori_loop(0, heads, body, 0)

    @pl.when(is_last)
    def _():
        for h in range(heads):
            o_ref[:, h * DH_SB:(h + 1) * DH_SB] = acc[h].astype(o_ref.dtype)


def _suffix_matrix(w):
    j = np.arange(w)[:, None]
    s = np.arange(w)[None, :]
    return jnp.asarray((j > s).astype(np.float32), dtype=BF16)


def _sb_attention(q, kn, vn, kp, vp, tables, *, heads, tq, tk, sub, out_rows, out_block_of):
    qb, kb, hb, fl = tables
    n_steps = qb.shape[0]
    hblk = heads if heads > 1 else 1

    def qmap(s, qb, kb, hb, fl):
        return (hb[s], qb[s], 0)

    if kp.ndim == 3:
        past_spec = pl.BlockSpec((hblk, tk, DH_SB), lambda s, qb, kb, hb, fl: (hb[s], kb[s], 0))
    else:
        past_spec = pl.BlockSpec((tk, DH_SB), lambda s, qb, kb, hb, fl: (kb[s], hb[s]))
    new_spec = pl.BlockSpec((hblk, tq, DH_SB), qmap)
    const = lambda s, qb, kb, hb, fl: (0, 0)
    grid_spec = pltpu.PrefetchScalarGridSpec(
        num_scalar_prefetch=4,
        grid=(n_steps,),
        in_specs=[new_spec, new_spec, new_spec, past_spec, past_spec,
                  pl.BlockSpec((tq, tq), const), pl.BlockSpec((sub, sub), const)],
        out_specs=pl.BlockSpec((tq, hblk * DH_SB), out_block_of),
        scratch_shapes=[pltpu.VMEM((hblk, tq, DH_SB), F32),
                        pltpu.VMEM((hblk, tq, LANES), F32),
                        pltpu.SMEM((hblk,), jnp.int32)],
    )
    kern = functools.partial(_sb_kernel, heads=hblk, scale=DH_SB ** -0.5, sub=sub)
    return pl.pallas_call(
        kern, grid_spec=grid_spec,
        out_shape=jax.ShapeDtypeStruct((out_rows, H_SB * DH_SB), BF16),
        compiler_params=_cparams(1),
    )(qb, kb, hb, fl, q, kn, vn, kp, vp, _suffix_matrix(tq), _suffix_matrix(sub))


def _mla_block(qn, qr, kn, kr, vh, m, l, acc, scale, mask):
    s = (_dot_nt(qn, kn) + _dot_nt(qr, kr)) * scale
    if mask is not None:
        s = jnp.where(mask, s, -jnp.inf)
    w = s.shape[1]
    m_new = jnp.maximum(m, jnp.max(s, axis=1, keepdims=True))
    alpha = jnp.exp(m - m_new)
    p = jnp.exp(s - _lanes(m_new, w))
    l_new = alpha * l + jnp.sum(p, axis=1, keepdims=True)
    acc_new = alpha * acc + jnp.dot(p.astype(BF16), vh, preferred_element_type=F32)
    return m_new, l_new, acc_new


def _mla_kernel(qb_ref, kb_ref, fl_ref, qn_ref, qr_ref, knn_ref, krn_ref, vn_ref,
                knp_ref, krp_ref, vp_ref, o_ref, acc, m_sc, l_sc, *, heads, scale, pos0):
    s = pl.program_id(0)
    fl = fl_ref[s]
    is_first = (fl & 1) != 0
    is_last = (fl & 2) != 0
    tq = qn_ref.shape[1]

    @pl.when(is_first)
    def _():
        row = lax.broadcasted_iota(jnp.int32, (tq, tq), 0) + pos0
        col = lax.broadcasted_iota(jnp.int32, (tq, tq), 1) + pos0
        mask = (col // CHUNK) <= (row // CHUNK)
        kr = krn_ref[...]

        def body(h, _):
            m0 = jnp.full((tq, LANES), -jnp.inf, F32)
            l0 = jnp.zeros((tq, LANES), F32)
            a0 = jnp.zeros((tq, V_DIM), F32)
            m, l, a = _mla_block(qn_ref[h], qr_ref[h], knn_ref[h], kr, vn_ref[h],
                                 m0, l0, a0, scale, mask)
            m_sc[h] = m
            l_sc[h] = l
            acc[h] = a
            return 0
        lax.fori_loop(0, heads, body, 0)

    @pl.when(jnp.logical_not(is_first))
    def _():
        kr = krp_ref[...]

        def body(h, _):
            m, l, a = _mla_block(qn_ref[h], qr_ref[h], knp_ref[h], kr, vp_ref[h],
                                 m_sc[h], l_sc[h], acc[h], scale, None)
            m_sc[h] = m
            l_sc[h] = l
            acc[h] = a
            return 0
        lax.fori_loop(0, heads, body, 0)

    @pl.when(is_last)
    def _():
        for h in range(heads):
            o_ref[:, h * V_DIM:(h + 1) * V_DIM] = (acc[h] / l_sc[h]).astype(o_ref.dtype)


def _mla_attention(qn, qr, knn, krn, vn, knp, krp, vp, tables, *, tq, tk, pos0, out_rows,
                   out_block_of):
    qb, kb, fl = tables
    n_steps = qb.shape[0]
    h = H_MLA
    new3 = pl.BlockSpec((h, tq, LANES), lambda s, qb, kb, fl: (0, qb[s], 0))
    new2 = pl.BlockSpec((tq, LANES), lambda s, qb, kb, fl: (qb[s], 0))
    past3 = pl.BlockSpec((h, tk, LANES), lambda s, qb, kb, fl: (0, kb[s], 0))
    past2 = pl.BlockSpec((tk, LANES), lambda s, qb, kb, fl: (kb[s], 0))
    grid_spec = pltpu.PrefetchScalarGridSpec(
        num_scalar_prefetch=3,
        grid=(n_steps,),
        in_specs=[new3, new3, new3, new2, new3, past3, past2, past3],
        out_specs=pl.BlockSpec((tq, h * V_DIM), out_block_of),
        scratch_shapes=[pltpu.VMEM((h, tq, V_DIM), F32),
                        pltpu.VMEM((h, tq, LANES), F32),
                        pltpu.VMEM((h, tq, LANES), F32)],
    )
    kern = functools.partial(_mla_kernel, heads=h, scale=(QK_NOPE + QK_ROPE) ** -0.5, pos0=pos0)
    return pl.pallas_call(
        kern, grid_spec=grid_spec,
        out_shape=jax.ShapeDtypeStruct((out_rows, h * V_DIM), BF16),
        compiler_params=_cparams(1),
    )(qb, kb, fl, qn, qr, knn, krn, vn, knp, krp, vp)


def _causal_tables(nq, q_off, per_head=False, heads=1, n_batch=1, past_tiles=None,
                   past_stride=0):
    qb, kb, hb, fl = [], [], [], []
    if past_tiles is None:
        for i in range(nq):
            n = i + 1
            for j in range(n):
                qb.append(q_off + i)
                kb.append(max(i - j, 1) - 1 if j == 0 else i - j)
                hb.append(0)
                fl.append((1 if j == 0 else 0) | (2 if j == n - 1 else 0))
    else:
        for b in range(n_batch):
            for h in range(heads if per_head else 1):
                n = 1 + past_tiles
                for j in range(n):
                    qb.append(q_off + b)
                    jj = past_tiles - 1 if j == 0 else past_tiles - j
                    kb.append(past_stride + b * past_tiles + jj)
                    hb.append(h)
                    fl.append((1 if j == 0 else 0) | (2 if j == n - 1 else 0))
    arr = lambda v: jnp.asarray(np.asarray(v, dtype=np.int32))
    return arr(qb), arr(kb), arr(hb), arr(fl)


GATHER_DEPTH = 16


def _gather_kernel(idx_ref, src_ref, dst_ref, sem, *, n_rows):
    def copy(r):
        return pltpu.make_async_copy(src_ref.at[pl.ds(idx_ref[r], 1)],
                                     dst_ref.at[pl.ds(r, 1)], sem)

    def body(r, _):
        copy(r).start()

        @pl.when(r >= GATHER_DEPTH)
        def _():
            copy(r - GATHER_DEPTH).wait()
        return 0
    lax.fori_loop(0, n_rows, body, 0)

    def drain(r, _):
        copy(r).wait()
        return 0
    lax.fori_loop(max(n_rows - GATHER_DEPTH, 0), n_rows, drain, 0)


def _gather_rows(src, idx):
    n_rows = idx.shape[0]
    grid_spec = pltpu.PrefetchScalarGridSpec(
        num_scalar_prefetch=1, grid=(1,),
        in_specs=[pl.BlockSpec(memory_space=pl.ANY)],
        out_specs=pl.BlockSpec(memory_space=pl.ANY),
        scratch_shapes=[pltpu.SemaphoreType.DMA(())],
    )
    return pl.pallas_call(
        functools.partial(_gather_kernel, n_rows=n_rows), grid_spec=grid_spec,
        out_shape=jax.ShapeDtypeStruct((n_rows, src.shape[1]), src.dtype),
        compiler_params=_cparams(1),
    )(idx, src)


def _moe_kernel(te_ref, nv_ref, x_ref, w_ref, wg_ref, wu_ref, wd_ref, o_ref, wg_s, wu_s, wd_s):
    t = pl.program_id(0)
    prev = te_ref[jnp.maximum(t - 1, 0)]
    fresh = (t == 0) | (te_ref[t] != prev)
    live = t < nv_ref[0]

    @pl.when(live & fresh)
    def _():
        wg_s[...] = wg_ref[...].astype(BF16)
        wu_s[...] = wu_ref[...].astype(BF16)
        wd_s[...] = wd_ref[...].astype(BF16)

    @pl.when(live)
    def _():
        x = x_ref[...].astype(BF16)
        a = jnp.dot(x, wg_s[...], preferred_element_type=F32)
        u = jnp.dot(x, wu_s[...], preferred_element_type=F32)
        hid = (a * jax.nn.sigmoid(a)) * u * w_ref[...]
        o_ref[...] = jnp.dot(hid.astype(BF16), wd_s[...], preferred_element_type=F32)

    @pl.when(jnp.logical_not(live))
    def _():
        o_ref[...] = jnp.zeros(o_ref.shape, o_ref.dtype)


def _moe_experts(xs, row_w, tile_expert, n_valid, w_gate, w_up, w_down, l):
    r, d = xs.shape
    f = w_gate.shape[-1]
    tm = MOE_TILE
    grid_spec = pltpu.PrefetchScalarGridSpec(
        num_scalar_prefetch=2, grid=(r // tm,),
        in_specs=[pl.BlockSpec((tm, d), lambda t, te, nv: (t, 0)),
                  pl.BlockSpec((tm, 1), lambda t, te, nv: (t, 0)),
                  pl.BlockSpec((None, None, d, f), lambda t, te, nv: (l, te[t], 0, 0)),
                  pl.BlockSpec((None, None, d, f), lambda t, te, nv: (l, te[t], 0, 0)),
                  pl.BlockSpec((None, None, f, d), lambda t, te, nv: (l, te[t], 0, 0))],
        out_specs=pl.BlockSpec((tm, d), lambda t, te, nv: (t, 0)),
        scratch_shapes=[pltpu.VMEM((d, f), BF16), pltpu.VMEM((d, f), BF16),
                        pltpu.VMEM((f, d), BF16)],
    )
    return pl.pallas_call(
        _moe_kernel, grid_spec=grid_spec,
        out_shape=jax.ShapeDtypeStruct((r, d), F32),
        compiler_params=_cparams(1),
    )(tile_expert, n_valid, xs, row_w, w_gate, w_up, w_down)


def _combine_kernel(x_ref, y0_ref, y1_ref, g_ref, o_ref):
    o_ref[...] = x_ref[...] + g_ref[...] * (y0_ref[...] + y1_ref[...])


def _combine(x3, yg3, modg, l, g_i):
    ng, grp, d = x3.shape
    gb = _row_tile(ng, 2)
    blk = lambda off: pl.BlockSpec((gb, grp, d), lambda i: (i + off, 0, 0))
    return pl.pallas_call(
        _combine_kernel,
        grid=(ng // gb,),
        in_specs=[blk(0), blk(0), blk(ng // gb), _mod_spec(g_i, gb, d)],
        out_specs=blk(0),
        out_shape=jax.ShapeDtypeStruct(x3.shape, F32),
        compiler_params=_cparams(1),
    )(x3, yg3, yg3, modg)


def _dispatch(route, n):
    tm = MOE_TILE
    r_cap = (2 * n + N_EXPERTS * (tm - 1) + tm - 1) // tm * tm
    e = route[:, :2].astype(jnp.int32)
    w = route[:, 2:4]
    flat_e = e.T.reshape(-1)
    flat_w = w.T.reshape(-1)
    flat_tok = jnp.tile(jnp.arange(n, dtype=jnp.int32), 2)
    order = jnp.argsort(flat_e, stable=True)
    sorted_e = flat_e[order]
    counts = jnp.bincount(flat_e, length=N_EXPERTS).astype(jnp.int32)
    padded = (counts + tm - 1) // tm * tm
    ends_p = jnp.cumsum(padded)
    starts_p = ends_p - padded
    starts = jnp.cumsum(counts) - counts
    rank = jnp.arange(2 * n, dtype=jnp.int32) - starts[sorted_e]
    dest_sorted = starts_p[sorted_e] + rank
    row_tok = jnp.zeros((r_cap,), jnp.int32).at[dest_sorted].set(flat_tok[order])
    row_w = jnp.zeros((r_cap,), F32).at[dest_sorted].set(flat_w[order])
    pos = jnp.zeros((2 * n,), jnp.int32).at[order].set(dest_sorted)
    tile_start = jnp.arange(r_cap // tm, dtype=jnp.int32) * tm
    tile_expert = jnp.minimum(jnp.searchsorted(ends_p, tile_start, side="right"),
                              N_EXPERTS - 1).astype(jnp.int32)
    n_valid = (ends_p[-1:] // tm).astype(jnp.int32)
    return row_tok, row_w.reshape(r_cap, 1), pos, tile_expert, n_valid


def _rope_tables(pos):
    inv = ROPE_THETA ** (-jnp.arange(0, QK_ROPE, 2, dtype=F32) / QK_ROPE)
    ang = pos.astype(F32)[:, None] * inv[None, :]
    cos, sin = jnp.cos(ang), jnp.sin(ang)
    pad = jnp.zeros((pos.shape[0], LANES - QK_ROPE), F32)
    return (jnp.concatenate([cos, cos, pad], axis=1),
            jnp.concatenate([-sin, sin, pad], axis=1))


def _rope_lanes(acc, c, s):
    return acc * c + pltpu.roll(acc, LANES - QK_ROPE, axis=1) * s


def _swap_halves(w):
    half = w.shape[-1] // 2
    return jnp.concatenate([w[..., half:], w[..., :half]], axis=-1)


def kernel(x_prompt, x_sample, c_prompt, c_sample, cache_sb_k, cache_sb_v, cache_mla_ckv,
           cache_mla_krope, w_ada, b_ada, g_norm_mix, g_norm_ffn, w_in, g_q_lat, g_kv_lat,
           w_uq, w_ukv, w_branch_sb, w_branch_mla, w_out, w_router_group, b_router_group,
           w_router_expert, b_router_expert, w_exp_gate, w_exp_up, w_exp_down, g_final):
    bp, t_p, d = x_prompt.shape
    bs, t_s, _ = x_sample.shape
    depth = w_in.shape[0]
    past = cache_sb_k.shape[2]
    grp = t_s
    n_p, n_s = bp * t_p, bs * t_s
    n = n_p + n_s
    ng = n // grp
    sb_w = H_SB * DH_SB
    q_lora = g_q_lat.shape[1]
    kv_lora = g_kv_lat.shape[1]
    tm = _row_tile(n)
    gpt = tm // grp
    tn = 512
    assert bp == 1 and t_p % ATTN_TILE == 0 and t_p % grp == 0 and past % 512 == 0

    x3 = jnp.concatenate([x_prompt.reshape(n_p // grp, grp, d), x_sample], axis=0)

    n_c = bp + bs
    c_rows = 16
    c_all = jnp.zeros((c_rows, d), F32).at[:n_c].set(jnp.concatenate([c_prompt, c_sample], 0))
    n_modc = N_MOD * d

    def ada_epi(accs, ex, outs):
        outs[0][...] = accs[0] + ex[0][...]

    mods = []
    for l in range(depth):
        mods.append(_matmul(
            [(c_all, w_ada, _wspec(l, d, tn, 0), True)], m=c_rows, n_out=n_modc, tm=c_rows, tn=tn,
            prologue=lambda a, ex: a * jax.nn.sigmoid(a),
            epilogue=ada_epi,
            extras=[b_ada.reshape(depth, 1, n_modc)],
            extra_specs=[pl.BlockSpec((None, 1, tn), lambda j, i, l=l: (l, 0, j))],
            out_shape=[jax.ShapeDtypeStruct((c_rows, n_modc), F32)],
            out_specs=[_spec2(c_rows, tn)])[0])
    mod = jnp.stack(mods)
    modg = jnp.concatenate(
        [jnp.broadcast_to(mod[:, :bp], (depth, n_p // grp, n_modc)), mod[:, bp:n_c]], axis=1)
    modg = modg.reshape(depth, ng, N_MOD, d).transpose(0, 2, 1, 3)
    modg = modg.reshape(depth * N_MOD, ng, 1, d)

    def mod_idx(l, k):
        return l * N_MOD + k

    def mspec_rows(l, k):
        return pl.BlockSpec((None, gpt, 1, tn), lambda j, i: (mod_idx(l, k), i, 0, j))

    pos = jnp.concatenate([jnp.arange(t_p, dtype=jnp.int32),
                           jnp.tile(past + jnp.arange(t_s, dtype=jnp.int32), bs)])
    rope_c, rope_s = _rope_tables(pos)
    tq = ATTN_TILE
    nq_p = t_p // tq
    tab_p = _causal_tables(nq_p, 0)
    tk_s = 512
    pt = past // tk_s
    tab_sb_s = lambda l: _causal_tables(0, n_p // t_s, per_head=True, heads=H_SB, n_batch=bs,
                                        past_tiles=pt, past_stride=l * bs * pt)
    tab_mla_s = _causal_tables(0, n_p // t_s, n_batch=bs, past_tiles=pt)

    off_q, off_k, off_v = 0, sb_w, 2 * sb_w
    off_cq = 3 * sb_w
    off_ckv = off_cq + q_lora
    off_kr = off_ckv + kv_lora
    off_g = off_kr + QK_ROPE
    w_kr = w_in[:, :, off_kr:off_kr + QK_ROPE]
    w_kr_aug = jnp.concatenate([w_kr, _swap_halves(w_kr)], axis=-1).astype(BF16)
    w_gates = w_in[:, :, off_g:].astype(BF16)
    uq = w_uq.reshape(depth, q_lora, H_MLA, QK_NOPE + QK_ROPE)
    w_uq_nope = uq[..., :QK_NOPE].reshape(depth, q_lora, H_MLA * QK_NOPE)
    uq_r = uq[..., QK_NOPE:]
    w_uq_rope = jnp.concatenate([uq_r, _swap_halves(uq_r)], axis=-1).reshape(depth, q_lora, H_MLA * LANES)
    ukv = w_ukv.reshape(depth, kv_lora, H_MLA, QK_NOPE + V_DIM)
    w_ukn = ukv[..., :QK_NOPE].reshape(depth, kv_lora, H_MLA * QK_NOPE)
    w_uv = ukv[..., QK_NOPE:].reshape(depth, kv_lora, H_MLA * V_DIM)
    w_router = jnp.concatenate(
        [w_router_expert, w_router_group,
         jnp.zeros((depth, d, LANES - N_EXPERTS - N_GROUPS), F32)], axis=-1)
    r_hi = w_router.astype(BF16)
    r_res = w_router - r_hi.astype(F32)
    r_mid = r_res.astype(BF16)
    r_lo = (r_res - r_mid.astype(F32)).astype(BF16)
    w_router3 = jnp.stack([r_hi, r_mid, r_lo], axis=1)
    b_router = jnp.concatenate(
        [b_router_expert, b_router_group,
         jnp.zeros((depth, LANES - N_EXPERTS - N_GROUPS), F32)], axis=-1).reshape(depth, 1, LANES)

    hm = lambda rows: jax.ShapeDtypeStruct((H_SB, rows, LANES), BF16)
    new_k, new_v, new_c, new_r = [], [], [], []

    for l in range(depth):
        h = _norm_mod(x3, g_norm_mix.reshape(depth, 1, d), modg, l, mod_idx(l, 1), mod_idx(l, 0))
        h = h.reshape(n, d)

        def plain_hm(accs, ex, outs):
            _store_heads(outs[0], accs[0])

        def f32_and_hm(accs, ex, outs):
            outs[0][...] = accs[0]
            _store_heads(outs[1], accs[0])

        sb_q = _matmul([(h, w_in, _wspec(l, d, tn, off_q), True)], m=n, n_out=sb_w, tm=tm, tn=tn,
                       epilogue=plain_hm, out_shape=[hm(n)], out_specs=[_hm_spec(tm, tn)])[0]
        k_f32, sb_k = _matmul([(h, w_in, _wspec(l, d, tn, off_k), True)], m=n, n_out=sb_w, tm=tm,
                              tn=tn, epilogue=f32_and_hm,
                              out_shape=[jax.ShapeDtypeStruct((n, sb_w), F32), hm(n)],
                              out_specs=[_spec2(tm, tn), _hm_spec(tm, tn)])
        v_f32, sb_v = _matmul([(h, w_in, _wspec(l, d, tn, off_v), True)], m=n, n_out=sb_w, tm=tm,
                              tn=tn, epilogue=f32_and_hm,
                              out_shape=[jax.ShapeDtypeStruct((n, sb_w), F32), hm(n)],
                              out_specs=[_spec2(tm, tn), _hm_spec(tm, tn)])

        def plain_f32(accs, ex, outs):
            outs[0][...] = accs[0]

        c_q = _matmul([(h, w_in, _wspec(l, d, tn, off_cq), True)], m=n, n_out=q_lora, tm=tm, tn=tn,
                      epilogue=plain_f32, out_shape=[jax.ShapeDtypeStruct((n, q_lora), F32)],
                      out_specs=[_spec2(tm, tn)])[0]

        def ckv_epi(accs, ex, outs):
            y = _rms(accs[0], ex[0][...])
            outs[0][...] = y
            outs[1][...] = y.astype(BF16)

        c_kv, c_kv_b = _matmul(
            [(h, w_in, _wspec(l, d, kv_lora, off_ckv), True)], m=n, n_out=kv_lora, tm=tm, tn=kv_lora,
            epilogue=ckv_epi, extras=[g_kv_lat.reshape(depth, 1, kv_lora)],
            extra_specs=[pl.BlockSpec((None, 1, kv_lora), lambda j, i: (l, 0, 0))],
            out_shape=[jax.ShapeDtypeStruct((n, kv_lora), F32),
                       jax.ShapeDtypeStruct((n, kv_lora), BF16)],
            out_specs=[_spec2(tm, kv_lora), _spec2(tm, kv_lora)])

        def kr_epi(accs, ex, outs):
            r = _rope_lanes(accs[0], ex[0][...], ex[1][...])
            outs[0][...] = r
            outs[1][...] = r.astype(BF16)

        rope_specs = [pl.BlockSpec((tm, LANES), lambda j, i: (i, 0))] * 2
        k_r, k_r_b = _matmul(
            [(h, w_kr_aug, pl.BlockSpec((None, d, LANES), lambda j, i: (l, 0, 0)), False)],
            m=n, n_out=LANES, tm=tm, tn=LANES, epilogue=kr_epi,
            extras=[rope_c, rope_s], extra_specs=rope_specs,
            out_shape=[jax.ShapeDtypeStruct((n, LANES), F32), jax.ShapeDtypeStruct((n, LANES), BF16)],
            out_specs=[_spec2(tm, LANES), _spec2(tm, LANES)])

        def gate_epi(accs, ex, outs):
            outs[0][...] = jax.nn.sigmoid(accs[0])

        gates = _matmul(
            [(h, w_gates, pl.BlockSpec((None, d, tn), lambda j, i: (l, 0, j)), False)],
            m=n, n_out=2 * d, tm=tm, tn=tn, epilogue=gate_epi,
            out_shape=[jax.ShapeDtypeStruct((n, 2 * d), F32)], out_specs=[_spec2(tm, tn)])[0]

        def cq_prologue(a, ex):
            return _rms(a, ex[0][...])

        gq_spec = pl.BlockSpec((None, 1, q_lora), lambda j, i: (l, 0, 0))
        gq = g_q_lat.reshape(depth, 1, q_lora)
        q_nope = _matmul(
            [(c_q, w_uq_nope, pl.BlockSpec((None, q_lora, tn), lambda j, i: (l, 0, j)), True)],
            m=n, n_out=H_MLA * QK_NOPE, tm=tm, tn=tn, prologue=cq_prologue,
            epilogue=lambda accs, ex, outs: _store_heads(outs[0], accs[0]),
            extras=[gq], extra_specs=[gq_spec],
            out_shape=[hm(n)], out_specs=[_hm_spec(tm, tn)])[0]

        def qr_epi(accs, ex, outs):
            c, s = ex[1][...], ex[2][...]
            for cc in range(accs[0].shape[1] // LANES):
                outs[0][cc] = _rope_lanes(accs[0][:, cc * LANES:(cc + 1) * LANES], c, s).astype(BF16)

        q_rope = _matmul(
            [(c_q, w_uq_rope, pl.BlockSpec((None, q_lora, tn), lambda j, i: (l, 0, j)), True)],
            m=n, n_out=H_MLA * LANES, tm=tm, tn=tn, prologue=cq_prologue, epilogue=qr_epi,
            extras=[gq, rope_c, rope_s], extra_specs=[gq_spec] + rope_specs,
            out_shape=[hm(n)], out_specs=[_hm_spec(tm, tn)])[0]

        def up_kv(a, rows, tmr):
            outs = []
            for wmat in (w_ukn, w_uv):
                outs.append(_matmul(
                    [(a, wmat, pl.BlockSpec((None, kv_lora, tn), lambda j, i: (l, 0, j)), True)],
                    m=rows, n_out=H_MLA * LANES, tm=tmr, tn=tn,
                    epilogue=lambda accs, ex, outs: _store_heads(outs[0], accs[0]),
                    out_shape=[hm(rows)], out_specs=[_hm_spec(tmr, tn)])[0])
            return outs

        kn_new, v_new = up_kv(c_kv_b, n, tm)
        rows_past = bs * past
        kn_past, v_past = up_kv(cache_mla_ckv[l].reshape(rows_past, kv_lora), rows_past, 512)
        kr_past = jnp.pad(cache_mla_krope[l].reshape(rows_past, QK_ROPE),
                          ((0, 0), (0, LANES - QK_ROPE))).astype(BF16)

        o_sb_p = _sb_attention(sb_q, sb_k, sb_v, sb_k, sb_v, tab_p, heads=H_SB, tq=tq, tk=tq,
                               sub=tq, out_rows=n_p,
                               out_block_of=lambda s, qb, kb, hb, fl: (qb[s], 0))
        cache_k2 = cache_sb_k.reshape(depth * bs * past, sb_w)
        cache_v2 = cache_sb_v.reshape(depth * bs * past, sb_w)
        q_off_s = n_p // t_s
        o_sb_s = _sb_attention(sb_q, sb_k, sb_v, cache_k2, cache_v2, tab_sb_s(l), heads=1, tq=t_s,
                               tk=tk_s, sub=ATTN_TILE, out_rows=n_s,
                               out_block_of=lambda s, qb, kb, hb, fl: (qb[s] - q_off_s, hb[s]))
        o_sb = jnp.concatenate([o_sb_p, o_sb_s], axis=0)

        tp3 = (tab_p[0], tab_p[1], tab_p[3])
        o_mla_p = _mla_attention(q_nope, q_rope, kn_new, k_r_b, v_new, kn_new, k_r_b, v_new, tp3,
                                 tq=tq, tk=tq, pos0=0, out_rows=n_p,
                                 out_block_of=lambda s, qb, kb, fl: (qb[s], 0))
        ts3 = (tab_mla_s[0], tab_mla_s[1], tab_mla_s[3])
        o_mla_s = _mla_attention(q_nope, q_rope, kn_new, k_r_b, v_new, kn_past, kr_past, v_past, ts3,
                                 tq=t_s, tk=tk_s, pos0=past, out_rows=n_s,
                                 out_block_of=lambda s, qb, kb, fl: (qb[s] - q_off_s, 0))
        o_mla = jnp.concatenate([o_mla_p, o_mla_s], axis=0)

        def merge_epi(accs, ex, outs):
            outs[0][...] = (ex[0][...] * accs[0] + ex[1][...] * accs[1]).astype(BF16)

        merged = _matmul(
            [(o_sb, w_branch_sb, _wspec(l, sb_w, tn, 0), True),
             (o_mla, w_branch_mla, _wspec(l, H_MLA * V_DIM, tn, 0), True)],
            m=n, n_out=d, tm=tm, tn=tn, epilogue=merge_epi,
            extras=[gates, gates],
            extra_specs=[pl.BlockSpec((tm, tn), lambda j, i: (i, j)),
                         pl.BlockSpec((tm, tn), lambda j, i: (i, j + d // tn))],
            out_shape=[jax.ShapeDtypeStruct((n, d), BF16)], out_specs=[_spec2(tm, tn)])[0]

        def resid_epi(accs, ex, outs):
            for g in range(gpt):
                outs[0][g] = ex[0][g] + ex[1][g] * accs[0][g * grp:(g + 1) * grp, :]

        x_spec = pl.BlockSpec((gpt, grp, tn), lambda j, i: (i, 0, j))
        x3 = _matmul(
            [(merged, w_out, _wspec(l, d, tn, 0), True)], m=n, n_out=d, tm=tm, tn=tn,
            epilogue=resid_epi, extras=[x3, modg], extra_specs=[x_spec, mspec_rows(l, 2)],
            out_shape=[jax.ShapeDtypeStruct((ng, grp, d), F32)], out_specs=[x_spec])[0]

        h2, route = _norm_route(x3, g_norm_ffn.reshape(depth, 1, d), modg, l, mod_idx(l, 4),
                                mod_idx(l, 3), w_router3, b_router)
        row_tok, row_w, pos_rows, tile_expert, n_valid = _dispatch(route.reshape(n, LANES), n)
        xs = _gather_rows(h2.reshape(n, d), row_tok)
        ys = _moe_experts(xs, row_w, tile_expert, n_valid, w_exp_gate, w_exp_up, w_exp_down, l)
        yg = _gather_rows(ys, pos_rows)
        x3 = _combine(x3, yg.reshape(2 * ng, grp, d), modg, l, mod_idx(l, 5))

        new_k.append(k_f32)
        new_v.append(v_f32)
        new_c.append(c_kv)
        new_r.append(k_r[:, :QK_ROPE])

    y = _final_norm(x3, g_final.reshape(1, d)).reshape(n, d)

    def split(parts, tail):
        a = jnp.stack(parts)
        return (a[:, :n_p].reshape((depth, bp, t_p) + tail),
                a[:, n_p:].reshape((depth, bs, t_s) + tail))

    pk, sk = split(new_k, (H_SB, DH_SB))
    pv, sv = split(new_v, (H_SB, DH_SB))
    pc, sc = split(new_c, (kv_lora,))
    pr, sr = split(new_r, (QK_ROPE,))
    return (y[:n_p].reshape(bp, t_p, d), y[n_p:].reshape(bs, t_s, d),
            pk, pv, pc, pr, sk, sv, sc, sr)
```

```python
import functools

import numpy as np
import jax
import jax.numpy as jnp
from jax import lax
from jax.experimental import pallas as pl
from jax.experimental.pallas import tpu as pltpu

F32 = jnp.float32
BF16 = jnp.bfloat16

CHUNK = 64
H_SB = 16
DH_SB = 128
H_MLA = 16
QK_NOPE = 128
QK_ROPE = 64
V_DIM = 128
ROPE_THETA = 10000.0
N_GROUPS = 4
EXPERTS_PER_GROUP = 8
N_EXPERTS = N_GROUPS * EXPERTS_PER_GROUP
N_MOD = 6
EPS = 1e-6

LANES = 128
ATTN_TILE = 256
MOE_TILE = 256
HEAD_UNROLL = 4
SB_DEAD = 104.0
VMEM_LIMIT = 56 * 1024 * 1024


def _cparams(n_axes, vmem=VMEM_LIMIT):
    return pltpu.CompilerParams(dimension_semantics=("arbitrary",) * n_axes,
                                vmem_limit_bytes=vmem)


def _row_tile(n, cap=512):
    t = cap
    while n % t:
        t //= 2
    return t


def _matmul(pairs, *, m, n_out, tm, tn, epilogue, out_shape, out_specs,
            extras=(), extra_specs=(), prologue=None):
    n_pairs = len(pairs)
    n_ex = len(extras)
    n_outs = len(out_shape)
    cast = [p[3] for p in pairs]

    def kern(*refs):
        a_refs = refs[0:2 * n_pairs:2]
        b_refs = refs[1:2 * n_pairs:2]
        ex = refs[2 * n_pairs:2 * n_pairs + n_ex]
        outs = refs[2 * n_pairs + n_ex:2 * n_pairs + n_ex + n_outs]
        scr = refs[2 * n_pairs + n_ex + n_outs:]
        i = pl.program_id(1)
        accs = []
        si = 0
        for p in range(n_pairs):
            if cast[p]:
                bsc = scr[si]
                si += 1

                @pl.when(i == 0)
                def _(bsc=bsc, b_ref=b_refs[p]):
                    bsc[...] = b_ref[...].astype(BF16)

                bv = bsc[...]
            else:
                bv = b_refs[p][...]
            a = a_refs[p][...]
            if prologue is not None:
                a = prologue(a, ex)
            accs.append(jnp.dot(a.astype(BF16), bv, preferred_element_type=F32))
        epilogue(accs, ex, outs)

    in_specs, args, scratch = [], [], []
    for (a, b, b_spec, cb) in pairs:
        k = a.shape[1]
        in_specs += [pl.BlockSpec((tm, k), lambda j, i: (i, 0)), b_spec]
        args += [a, b]
        if cb:
            scratch.append(pltpu.VMEM((k, tn), BF16))
    in_specs += list(extra_specs)
    args += list(extras)
    return pl.pallas_call(
        kern,
        grid=(n_out // tn, m // tm),
        in_specs=in_specs,
        out_specs=out_specs,
        out_shape=out_shape,
        scratch_shapes=scratch,
        compiler_params=_cparams(2),
    )(*args)


def _wspec(l, k, tn, col_off):
    cb = col_off // tn
    assert cb * tn == col_off
    return pl.BlockSpec((None, k, tn), lambda j, i: (l, 0, cb + j))


def _spec2(tm, tn):
    return pl.BlockSpec((tm, tn), lambda j, i: (i, j))


def _hm_spec(tm, tn):
    return pl.BlockSpec((tn // LANES, tm, LANES), lambda j, i: (j, i, 0))


def _store_heads(o_ref, val):
    for c in range(val.shape[1] // LANES):
        o_ref[c] = val[:, c * LANES:(c + 1) * LANES].astype(o_ref.dtype)


def _rms(x, g):
    return x * lax.rsqrt(jnp.mean(x * x, axis=-1, keepdims=True) + EPS) * g


def _norm_mod_kernel(x_ref, g_ref, sc_ref, sh_ref, o_ref):
    x = x_ref[...]
    y = _rms(x, g_ref[...])
    o_ref[...] = (y * (1.0 + sc_ref[...]) + sh_ref[...]).astype(o_ref.dtype)


def _split3(x):
    hi = x.astype(BF16)
    r = x - hi.astype(F32)
    mid = r.astype(BF16)
    lo = (r - mid.astype(F32)).astype(BF16)
    return hi, mid, lo


def _dot_f32(a, b3):
    a_hi, a_mid, a_lo = _split3(a)
    b_hi, b_mid, b_lo = b3
    d = functools.partial(jnp.dot, preferred_element_type=F32)
    small = d(a_hi, b_lo) + d(a_lo, b_hi) + d(a_mid, b_mid)
    return (d(a_hi, b_hi) + (d(a_hi, b_mid) + d(a_mid, b_hi))) + small


def _route(logits):
    lane = lax.broadcasted_iota(jnp.int32, logits.shape, 1)
    lanef = lane.astype(F32)
    big = jnp.float32(1e9)
    ninf = jnp.float32(-jnp.inf)
    is_g = (lane >= N_EXPERTS) & (lane < N_EXPERTS + N_GROUPS)
    gl = jnp.where(is_g, logits, ninf)
    gmax = jnp.max(gl, axis=1, keepdims=True)
    g_idx = jnp.min(jnp.where(gl == gmax, lanef - N_EXPERTS, big), axis=1, keepdims=True)
    p_group = 1.0 / jnp.sum(jnp.where(is_g, jnp.exp(gl - gmax), 0.0), axis=1, keepdims=True)
    grp = jnp.floor(lanef * (1.0 / EXPERTS_PER_GROUP))
    in_g = (lane < N_EXPERTS) & (grp == g_idx)
    el = jnp.where(in_g, logits, ninf)
    e1 = jnp.max(el, axis=1, keepdims=True)
    i1 = jnp.min(jnp.where(el == e1, lanef, big), axis=1, keepdims=True)
    el2 = jnp.where(lanef == i1, ninf, el)
    e2 = jnp.max(el2, axis=1, keepdims=True)
    i2 = jnp.min(jnp.where(el2 == e2, lanef, big), axis=1, keepdims=True)
    t = jnp.exp(e2 - e1)
    den = 1.0 + t
    w1 = (1.0 / den) * p_group
    w2 = (t / den) * p_group
    out = jnp.where(lane == 0, i1, jnp.where(lane == 1, i2,
          jnp.where(lane == 2, w1, jnp.where(lane == 3, w2, 0.0))))
    return out


def _slab_store(ref, row0, val):
    rows, d = val.shape
    s = d // LANES
    for c in range(s):
        ref[pl.ds(row0 * s + c, rows, stride=s), :] = val[:, c * LANES:(c + 1) * LANES]


def _slab_load(ref, row0, rows, s):
    return jnp.concatenate(
        [ref[pl.ds(row0 * s + c, rows, stride=s), :] for c in range(s)], axis=1)


def _norm_route_kernel(x_ref, g_ref, sc_ref, sh_ref, wr_ref, br_ref, h_ref, r_ref):
    x = x_ref[...]
    y = _rms(x, g_ref[...])
    h = y * (1.0 + sc_ref[...]) + sh_ref[...]
    b3 = (wr_ref[0], wr_ref[1], wr_ref[2])
    grp = x.shape[1]
    for g in range(x.shape[0]):
        _slab_store(h_ref, g * grp, h[g])
        logits = _dot_f32(h[g], b3) + br_ref[...]
        r_ref[g] = _route(logits)


def _mod_spec(idx, gb, d):
    return pl.BlockSpec((None, gb, 1, d), lambda i: (idx, i, 0, 0))


def _norm_mod(x3, g, modg, l, sc_i, sh_i):
    ng, grp, d = x3.shape
    gb = _row_tile(ng, 4)
    return pl.pallas_call(
        _norm_mod_kernel,
        grid=(ng // gb,),
        in_specs=[pl.BlockSpec((gb, grp, d), lambda i: (i, 0, 0)),
                  pl.BlockSpec((None, 1, d), lambda i: (l, 0, 0)),
                  _mod_spec(sc_i, gb, d), _mod_spec(sh_i, gb, d)],
        out_specs=pl.BlockSpec((gb, grp, d), lambda i: (i, 0, 0)),
        out_shape=jax.ShapeDtypeStruct(x3.shape, BF16),
        compiler_params=_cparams(1),
    )(x3, g, modg, modg)


def _norm_route(x3, g, modg, l, sc_i, sh_i, wr3, br):
    ng, grp, d = x3.shape
    gb = _row_tile(ng, 4)
    return pl.pallas_call(
        _norm_route_kernel,
        grid=(ng // gb,),
        in_specs=[pl.BlockSpec((gb, grp, d), lambda i: (i, 0, 0)),
                  pl.BlockSpec((None, 1, d), lambda i: (l, 0, 0)),
                  _mod_spec(sc_i, gb, d), _mod_spec(sh_i, gb, d),
                  pl.BlockSpec((None, 3, d, LANES), lambda i: (l, 0, 0, 0)),
                  pl.BlockSpec((None, 1, LANES), lambda i: (l, 0, 0))],
        out_specs=[pl.BlockSpec((gb * grp * (d // LANES), LANES), lambda i: (i, 0)),
                   pl.BlockSpec((gb, grp, LANES), lambda i: (i, 0, 0))],
        out_shape=[jax.ShapeDtypeStruct((ng * grp * (d // LANES), LANES), F32),
                   jax.ShapeDtypeStruct((ng, grp, LANES), F32)],
        compiler_params=_cparams(1),
    )(x3, g, modg, modg, wr3, br)


def _final_norm_kernel(x_ref, g_ref, o_ref):
    o_ref[...] = _rms(x_ref[...], g_ref[...])


def _final_norm(x3, g):
    ng, grp, d = x3.shape
    gb = _row_tile(ng, 4)
    return pl.pallas_call(
        _final_norm_kernel,
        grid=(ng // gb,),
        in_specs=[pl.BlockSpec((gb, grp, d), lambda i: (i, 0, 0)),
                  pl.BlockSpec((1, d), lambda i: (0, 0))],
        out_specs=pl.BlockSpec((gb, grp, d), lambda i: (i, 0, 0)),
        out_shape=jax.ShapeDtypeStruct(x3.shape, F32),
        compiler_params=_cparams(1),
    )(x3, g)


def _lanes(c, w):
    if w % LANES == 0:
        return c if w == LANES else jnp.tile(c, (1, w // LANES))
    return c[:, :w]


def _dot_nt(a, b):
    return lax.dot_general(a, b, (((1,), (1,)), ((), ())), preferred_element_type=F32)


def _sb_block(qh, kh, vh, c, u, scale, masked):
    w = kh.shape[0]
    z = _dot_nt(qh, kh) * scale
    sp = jnp.maximum(z, 0.0) + jnp.log1p(jnp.exp(-jnp.abs(z)))
    if masked:
        row = lax.broadcasted_iota(jnp.int32, z.shape, 0)
        col = lax.broadcasted_iota(jnp.int32, z.shape, 1)
        valid = col < row
        sp = jnp.where(valid, sp, 0.0)
    hi = sp.astype(BF16)
    lo = (sp - hi.astype(F32)).astype(BF16)
    cs = jnp.dot(hi, u, preferred_element_type=F32) + jnp.dot(lo, u, preferred_element_type=F32)
    wgt = jnp.exp(z - sp - cs - _lanes(c, w))
    if masked:
        wgt = jnp.where(valid, wgt, 0.0)
    o = jnp.dot(wgt.astype(BF16), vh, preferred_element_type=F32)
    c_new = c + jnp.sum(sp, axis=1, keepdims=True)
    return o, c_new


def _sb_kernel(qb_ref, kb_ref, hb_ref, fl_ref, q_ref, kn_ref, vn_ref, kp_ref, vp_ref,
               ud_ref, up_ref, o_ref, acc, carry, done, *, heads, scale, sub, n_sub):
    s = pl.program_id(0)
    fl = fl_ref[s]
    is_first = (fl & 1) != 0
    is_last = (fl & 2) != 0

    def past_block(ref, h, j):
        if len(ref.shape) == 3:
            return ref[h, pl.ds(j * sub, sub), :].astype(BF16)
        return ref[pl.ds(j * sub * heads + h, sub, stride=heads), :].astype(BF16)

    @pl.when(is_first)
    def _():
        def body(h, _):
            o, c = _sb_block(q_ref[h], kn_ref[h], vn_ref[h], jnp.zeros(carry.shape[1:], F32),
                             ud_ref[...], scale, True)
            acc[h] = o
            carry[h] = c
            done[h] = 0
            return 0
        lax.fori_loop(0, heads, body, 0, unroll=min(heads, HEAD_UNROLL))

    @pl.when(jnp.logical_not(is_first))
    def _():
        def body(h, _):
            @pl.when(done[h] == 0)
            def _():
                qh = q_ref[h]
                c = carry[h]
                o_tot = acc[h]
                for j in reversed(range(n_sub)):
                    o, c = _sb_block(qh, past_block(kp_ref, h, j), past_block(vp_ref, h, j), c,
                                     up_ref[...], scale, False)
                    o_tot = o_tot + o
                acc[h] = o_tot
                carry[h] = c
                done[h] = (jnp.min(c) >= SB_DEAD).astype(jnp.int32)
            return 0
        lax.fori_loop(0, heads, body, 0, unroll=min(heads, HEAD_UNROLL))

    @pl.when(is_last)
    def _():
        for h in range(heads):
            o_ref[:, h * DH_SB:(h + 1) * DH_SB] = acc[h].astype(o_ref.dtype)


def _suffix_matrix(w):
    j = np.arange(w)[:, None]
    s = np.arange(w)[None, :]
    return jnp.asarray((j > s).astype(np.float32), dtype=BF16)


def _sb_attention(q, kn, vn, kp, vp, tables, *, heads, tq, tk, sub, out_rows, out_block_of):
    qb, kb, hb, fl = tables
    n_steps = qb.shape[0]
    hblk = heads

    def qmap(s, qb, kb, hb, fl):
        return (0, qb[s], 0)

    if kp.ndim == 3:
        past_spec = pl.BlockSpec((hblk, tk, DH_SB), lambda s, qb, kb, hb, fl: (0, kb[s], 0))
    else:
        past_spec = pl.BlockSpec((tk * hblk, DH_SB), lambda s, qb, kb, hb, fl: (kb[s], 0))
    new_spec = pl.BlockSpec((hblk, tq, DH_SB), qmap)
    const = lambda s, qb, kb, hb, fl: (0, 0)
    grid_spec = pltpu.PrefetchScalarGridSpec(
        num_scalar_prefetch=4,
        grid=(n_steps,),
        in_specs=[new_spec, new_spec, new_spec, past_spec, past_spec,
                  pl.BlockSpec((tq, tq), const), pl.BlockSpec((sub, sub), const)],
        out_specs=pl.BlockSpec((tq, hblk * DH_SB), out_block_of),
        scratch_shapes=[pltpu.VMEM((hblk, tq, DH_SB), F32),
                        pltpu.VMEM((hblk, tq, LANES), F32),
                        pltpu.SMEM((hblk,), jnp.int32)],
    )
    kern = functools.partial(_sb_kernel, heads=hblk, scale=DH_SB ** -0.5, sub=sub,
                             n_sub=tk // sub)
    return pl.pallas_call(
        kern, grid_spec=grid_spec,
        out_shape=jax.ShapeDtypeStruct((out_rows, H_SB * DH_SB), BF16),
        compiler_params=_cparams(1),
    )(qb, kb, hb, fl, q, kn, vn, kp, vp, _suffix_matrix(tq), _suffix_matrix(sub))


def _mla_block(qc, kc, vh, m, l, acc, scale, mask):
    s = _dot_nt(qc, kc) * scale
    if mask is not None:
        s = jnp.where(mask, s, -jnp.inf)
    w = s.shape[1]
    m_new = jnp.maximum(m, jnp.max(s, axis=1, keepdims=True))
    alpha = jnp.exp(m - m_new)
    p = jnp.exp(s - _lanes(m_new, w))
    l_new = alpha * l + jnp.sum(p, axis=1, keepdims=True)
    acc_new = alpha * acc + jnp.dot(p.astype(BF16), vh, preferred_element_type=F32)
    return m_new, l_new, acc_new


def _mla_kernel(qb_ref, kb_ref, fl_ref, q_ref, kn_ref, vn_ref, kp_ref, vp_ref, o_ref,
                acc, m_sc, l_sc, *, heads, scale, pos0):
    s = pl.program_id(0)
    fl = fl_ref[s]
    is_first = (fl & 1) != 0
    is_last = (fl & 2) != 0
    tq = q_ref.shape[1]

    @pl.when(is_first)
    def _():
        row = lax.broadcasted_iota(jnp.int32, (tq, tq), 0) + pos0
        col = lax.broadcasted_iota(jnp.int32, (tq, tq), 1) + pos0
        mask = (col // CHUNK) <= (row // CHUNK)

        def body(h, _):
            m0 = jnp.full((tq, LANES), -jnp.inf, F32)
            l0 = jnp.zeros((tq, LANES), F32)
            a0 = jnp.zeros((tq, V_DIM), F32)
            m, l, a = _mla_block(q_ref[h], kn_ref[h], vn_ref[h], m0, l0, a0, scale, mask)
            m_sc[h] = m
            l_sc[h] = l
            acc[h] = a
            return 0
        lax.fori_loop(0, heads, body, 0, unroll=min(heads, HEAD_UNROLL))

    @pl.when(jnp.logical_not(is_first))
    def _():
        def body(h, _):
            m, l, a = _mla_block(q_ref[h], kp_ref[h], vp_ref[h], m_sc[h], l_sc[h], acc[h],
                                 scale, None)
            m_sc[h] = m
            l_sc[h] = l
            acc[h] = a
            return 0
        lax.fori_loop(0, heads, body, 0, unroll=min(heads, HEAD_UNROLL))

    @pl.when(is_last)
    def _():
        for h in range(heads):
            o_ref[:, h * V_DIM:(h + 1) * V_DIM] = (acc[h] / l_sc[h]).astype(o_ref.dtype)


def _mla_attention(qc, kcn, vn, kcp, vp, tables, *, tq, tk, pos0, out_rows, out_block_of):
    qb, kb, fl = tables
    n_steps = qb.shape[0]
    h = H_MLA
    wqk = 2 * LANES
    newq = pl.BlockSpec((h, tq, wqk), lambda s, qb, kb, fl: (0, qb[s], 0))
    newv = pl.BlockSpec((h, tq, V_DIM), lambda s, qb, kb, fl: (0, qb[s], 0))
    pastk = pl.BlockSpec((h, tk, wqk), lambda s, qb, kb, fl: (0, kb[s], 0))
    pastv = pl.BlockSpec((h, tk, V_DIM), lambda s, qb, kb, fl: (0, kb[s], 0))
    grid_spec = pltpu.PrefetchScalarGridSpec(
        num_scalar_prefetch=3,
        grid=(n_steps,),
        in_specs=[newq, newq, newv, pastk, pastv],
        out_specs=pl.BlockSpec((tq, h * V_DIM), out_block_of),
        scratch_shapes=[pltpu.VMEM((h, tq, V_DIM), F32),
                        pltpu.VMEM((h, tq, LANES), F32),
                        pltpu.VMEM((h, tq, LANES), F32)],
    )
    kern = functools.partial(_mla_kernel, heads=h, scale=(QK_NOPE + QK_ROPE) ** -0.5, pos0=pos0)
    return pl.pallas_call(
        kern, grid_spec=grid_spec,
        out_shape=jax.ShapeDtypeStruct((out_rows, h * V_DIM), BF16),
        compiler_params=_cparams(1),
    )(qb, kb, fl, qc, kcn, vn, kcp, vp)


def _causal_tables(nq, q_off, per_head=False, heads=1, n_batch=1, past_tiles=None,
                   past_stride=0):
    qb, kb, hb, fl = [], [], [], []
    if past_tiles is None:
        for i in range(nq):
            n = i + 1
            for j in range(n):
                qb.append(q_off + i)
                kb.append(max(i - j, 1) - 1 if j == 0 else i - j)
                hb.append(0)
                fl.append((1 if j == 0 else 0) | (2 if j == n - 1 else 0))
    else:
        for b in range(n_batch):
            for h in range(heads if per_head else 1):
                n = 1 + past_tiles
                for j in range(n):
                    qb.append(q_off + b)
                    jj = past_tiles - 1 if j == 0 else past_tiles - j
                    kb.append(past_stride + b * past_tiles + jj)
                    hb.append(h)
                    fl.append((1 if j == 0 else 0) | (2 if j == n - 1 else 0))
    arr = lambda v: jnp.asarray(np.asarray(v, dtype=np.int32))
    return arr(qb), arr(kb), arr(hb), arr(fl)


GATHER_DEPTH = 32


def _gather_kernel(idx_ref, src_ref, dst_ref, sem, *, n_rows, s):
    def copy(r):
        src0 = pl.multiple_of(idx_ref[r] * s, s)
        dst0 = pl.multiple_of(r * s, s)
        return pltpu.make_async_copy(src_ref.at[pl.ds(src0, s)], dst_ref.at[pl.ds(dst0, s)], sem)

    def body(r, _):
        copy(r).start()

        @pl.when(r >= GATHER_DEPTH)
        def _():
            copy(r - GATHER_DEPTH).wait()
        return 0
    lax.fori_loop(0, n_rows, body, 0)

    def drain(r, _):
        copy(r).wait()
        return 0
    lax.fori_loop(max(n_rows - GATHER_DEPTH, 0), n_rows, drain, 0)


def _gather_rows(src, idx, s):
    n_rows = idx.shape[0]
    grid_spec = pltpu.PrefetchScalarGridSpec(
        num_scalar_prefetch=1, grid=(1,),
        in_specs=[pl.BlockSpec(memory_space=pl.ANY)],
        out_specs=pl.BlockSpec(memory_space=pl.ANY),
        scratch_shapes=[pltpu.SemaphoreType.DMA(())],
    )
    return pl.pallas_call(
        functools.partial(_gather_kernel, n_rows=n_rows, s=s), grid_spec=grid_spec,
        out_shape=jax.ShapeDtypeStruct((n_rows * s, src.shape[1]), src.dtype),
        compiler_params=_cparams(1),
    )(idx, src)


def _moe_kernel(te_ref, nv_ref, x_ref, w_ref, wg_ref, wu_ref, wd_ref, o_ref, wg_s, wu_s, wd_s):
    t = pl.program_id(0)
    prev = te_ref[jnp.maximum(t - 1, 0)]
    fresh = (t == 0) | (te_ref[t] != prev)
    live = t < nv_ref[0]

    @pl.when(live & fresh)
    def _():
        wg_s[...] = wg_ref[...].astype(BF16)
        wu_s[...] = wu_ref[...].astype(BF16)
        wd_s[...] = wd_ref[...].astype(BF16)

    @pl.when(live)
    def _():
        tm = w_ref.shape[0]
        s = x_ref.shape[0] // tm
        x = _slab_load(x_ref, 0, tm, s).astype(BF16)
        a = jnp.dot(x, wg_s[...], preferred_element_type=F32)
        u = jnp.dot(x, wu_s[...], preferred_element_type=F32)
        hid = (a * jax.nn.sigmoid(a)) * u * w_ref[...]
        _slab_store(o_ref, 0, jnp.dot(hid.astype(BF16), wd_s[...], preferred_element_type=F32))

    @pl.when(jnp.logical_not(live))
    def _():
        o_ref[...] = jnp.zeros(o_ref.shape, o_ref.dtype)


def _moe_experts(xs, row_w, tile_expert, n_valid, w_gate, w_up, w_down, l):
    d, f = w_gate.shape[-2:]
    s = d // LANES
    r = xs.shape[0] // s
    tm = MOE_TILE
    grid_spec = pltpu.PrefetchScalarGridSpec(
        num_scalar_prefetch=2, grid=(r // tm,),
        in_specs=[pl.BlockSpec((tm * s, LANES), lambda t, te, nv: (t, 0)),
                  pl.BlockSpec((tm, 1), lambda t, te, nv: (t, 0)),
                  pl.BlockSpec((None, None, d, f), lambda t, te, nv: (l, te[t], 0, 0)),
                  pl.BlockSpec((None, None, d, f), lambda t, te, nv: (l, te[t], 0, 0)),
                  pl.BlockSpec((None, None, f, d), lambda t, te, nv: (l, te[t], 0, 0))],
        out_specs=pl.BlockSpec((tm * s, LANES), lambda t, te, nv: (t, 0)),
        scratch_shapes=[pltpu.VMEM((d, f), BF16), pltpu.VMEM((d, f), BF16),
                        pltpu.VMEM((f, d), BF16)],
    )
    return pl.pallas_call(
        _moe_kernel, grid_spec=grid_spec,
        out_shape=jax.ShapeDtypeStruct((r * s, LANES), F32),
        compiler_params=_cparams(1),
    )(tile_expert, n_valid, xs, row_w, w_gate, w_up, w_down)


def _combine_kernel(x_ref, y0_ref, y1_ref, g_ref, o_ref):
    gb, grp, d = x_ref.shape
    s = d // LANES
    for g in range(gb):
        y = _slab_load(y0_ref, g * grp, grp, s) + _slab_load(y1_ref, g * grp, grp, s)
        o_ref[g] = x_ref[g] + g_ref[g] * y


def _combine(x3, yg, modg, l, g_i):
    ng, grp, d = x3.shape
    s = d // LANES
    gb = _row_tile(ng, 2)
    blk = pl.BlockSpec((gb, grp, d), lambda i: (i, 0, 0))
    yblk = lambda off: pl.BlockSpec((gb * grp * s, LANES), lambda i: (i + off, 0))
    return pl.pallas_call(
        _combine_kernel,
        grid=(ng // gb,),
        in_specs=[blk, yblk(0), yblk(ng // gb), _mod_spec(g_i, gb, d)],
        out_specs=blk,
        out_shape=jax.ShapeDtypeStruct(x3.shape, F32),
        compiler_params=_cparams(1),
    )(x3, yg, yg, modg)


def _dispatch(route, n):
    tm = MOE_TILE
    r_cap = (2 * n + N_EXPERTS * (tm - 1) + tm - 1) // tm * tm
    e = route[:, :2].astype(jnp.int32)
    w = route[:, 2:4]
    flat_e = e.T.reshape(-1)
    flat_w = w.T.reshape(-1)
    flat_tok = jnp.tile(jnp.arange(n, dtype=jnp.int32), 2)
    order = jnp.argsort(flat_e, stable=True)
    sorted_e = flat_e[order]
    counts = jnp.bincount(flat_e, length=N_EXPERTS).astype(jnp.int32)
    padded = (counts + tm - 1) // tm * tm
    ends_p = jnp.cumsum(padded)
    starts_p = ends_p - padded
    starts = jnp.cumsum(counts) - counts
    rank = jnp.arange(2 * n, dtype=jnp.int32) - starts[sorted_e]
    dest_sorted = starts_p[sorted_e] + rank
    row_tok = jnp.zeros((r_cap,), jnp.int32).at[dest_sorted].set(flat_tok[order])
    row_w = jnp.zeros((r_cap,), F32).at[dest_sorted].set(flat_w[order])
    pos = jnp.zeros((2 * n,), jnp.int32).at[order].set(dest_sorted)
    tile_start = jnp.arange(r_cap // tm, dtype=jnp.int32) * tm
    tile_expert = jnp.minimum(jnp.searchsorted(ends_p, tile_start, side="right"),
                              N_EXPERTS - 1).astype(jnp.int32)
    n_valid = (ends_p[-1:] // tm).astype(jnp.int32)
    return row_tok, row_w.reshape(r_cap, 1), pos, tile_expert, n_valid


def _rope_tables(pos):
    inv = ROPE_THETA ** (-jnp.arange(0, QK_ROPE, 2, dtype=F32) / QK_ROPE)
    ang = pos.astype(F32)[:, None] * inv[None, :]
    cos, sin = jnp.cos(ang), jnp.sin(ang)
    pad = jnp.zeros((pos.shape[0], LANES - QK_ROPE), F32)
    return (jnp.concatenate([cos, cos, pad], axis=1),
            jnp.concatenate([-sin, sin, pad], axis=1))


def _rope_lanes(acc, c, s):
    return acc * c + pltpu.roll(acc, LANES - QK_ROPE, axis=1) * s


def _swap_halves(w):
    half = w.shape[-1] // 2
    return jnp.concatenate([w[..., half:], w[..., :half]], axis=-1)


def kernel(x_prompt, x_sample, c_prompt, c_sample, cache_sb_k, cache_sb_v, cache_mla_ckv,
           cache_mla_krope, w_ada, b_ada, g_norm_mix, g_norm_ffn, w_in, g_q_lat, g_kv_lat,
           w_uq, w_ukv, w_branch_sb, w_branch_mla, w_out, w_router_group, b_router_group,
           w_router_expert, b_router_expert, w_exp_gate, w_exp_up, w_exp_down, g_final):
    bp, t_p, d = x_prompt.shape
    bs, t_s, _ = x_sample.shape
    depth = w_in.shape[0]
    past = cache_sb_k.shape[2]
    grp = t_s
    n_p, n_s = bp * t_p, bs * t_s
    n = n_p + n_s
    ng = n // grp
    sb_w = H_SB * DH_SB
    q_lora = g_q_lat.shape[1]
    kv_lora = g_kv_lat.shape[1]
    tm = _row_tile(n)
    gpt = tm // grp
    tn = 512
    assert bp == 1 and t_p % ATTN_TILE == 0 and t_p % grp == 0 and past % 512 == 0

    x3 = jnp.concatenate([x_prompt.reshape(n_p // grp, grp, d), x_sample], axis=0)

    n_c = bp + bs
    c_rows = 16
    c_all = jnp.zeros((c_rows, d), F32).at[:n_c].set(jnp.concatenate([c_prompt, c_sample], 0))
    n_modc = N_MOD * d

    def ada_epi(accs, ex, outs):
        outs[0][...] = accs[0] + ex[0][...]

    mods = []
    for l in range(depth):
        mods.append(_matmul(
            [(c_all, w_ada, _wspec(l, d, tn, 0), True)], m=c_rows, n_out=n_modc, tm=c_rows, tn=tn,
            prologue=lambda a, ex: a * jax.nn.sigmoid(a),
            epilogue=ada_epi,
            extras=[b_ada.reshape(depth, 1, n_modc)],
            extra_specs=[pl.BlockSpec((None, 1, tn), lambda j, i, l=l: (l, 0, j))],
            out_shape=[jax.ShapeDtypeStruct((c_rows, n_modc), F32)],
            out_specs=[_spec2(c_rows, tn)])[0])
    mod = jnp.stack(mods)
    modg = jnp.concatenate(
        [jnp.broadcast_to(mod[:, :bp], (depth, n_p // grp, n_modc)), mod[:, bp:n_c]], axis=1)
    modg = modg.reshape(depth, ng, N_MOD, d).transpose(0, 2, 1, 3)
    modg = modg.reshape(depth * N_MOD, ng, 1, d)

    def mod_idx(l, k):
        return l * N_MOD + k

    def mspec_rows(l, k):
        return pl.BlockSpec((None, gpt, 1, tn), lambda j, i: (mod_idx(l, k), i, 0, j))

    pos = jnp.concatenate([jnp.arange(t_p, dtype=jnp.int32),
                           jnp.tile(past + jnp.arange(t_s, dtype=jnp.int32), bs)])
    rope_c, rope_s = _rope_tables(pos)
    tq = ATTN_TILE
    nq_p = t_p // tq
    tab_p = _causal_tables(nq_p, 0)
    tk_s = 512
    pt = past // tk_s
    tab_sb_s = lambda l: _causal_tables(0, n_p // t_s, n_batch=bs, past_tiles=pt,
                                        past_stride=l * bs * pt)
    tab_mla_s = _causal_tables(0, n_p // t_s, n_batch=bs, past_tiles=pt)

    off_q, off_k, off_v = 0, sb_w, 2 * sb_w
    off_cq = 3 * sb_w
    off_ckv = off_cq + q_lora
    off_kr = off_ckv + kv_lora
    off_g = off_kr + QK_ROPE
    w_kr = w_in[:, :, off_kr:off_kr + QK_ROPE]
    w_kr_aug = jnp.concatenate([w_kr, _swap_halves(w_kr)], axis=-1).astype(BF16)
    w_gates = w_in[:, :, off_g:].astype(BF16)
    uq = w_uq.reshape(depth, q_lora, H_MLA, QK_NOPE + QK_ROPE)
    uq_r = uq[..., QK_NOPE:]
    w_uq_cat = jnp.concatenate([uq[..., :QK_NOPE], uq_r, _swap_halves(uq_r)], axis=-1)
    w_uq_cat = w_uq_cat.reshape(depth, q_lora, H_MLA * 2 * LANES)
    ukv = w_ukv.reshape(depth, kv_lora, H_MLA, QK_NOPE + V_DIM)
    w_ukn = ukv[..., :QK_NOPE].reshape(depth, kv_lora, H_MLA * QK_NOPE)
    w_uv = ukv[..., QK_NOPE:].reshape(depth, kv_lora, H_MLA * V_DIM)
    w_router = jnp.concatenate(
        [w_router_expert, w_router_group,
         jnp.zeros((depth, d, LANES - N_EXPERTS - N_GROUPS), F32)], axis=-1)
    r_hi = w_router.astype(BF16)
    r_res = w_router - r_hi.astype(F32)
    r_mid = r_res.astype(BF16)
    r_lo = (r_res - r_mid.astype(F32)).astype(BF16)
    w_router3 = jnp.stack([r_hi, r_mid, r_lo], axis=1)
    b_router = jnp.concatenate(
        [b_router_expert, b_router_group,
         jnp.zeros((depth, LANES - N_EXPERTS - N_GROUPS), F32)], axis=-1).reshape(depth, 1, LANES)

    hm = lambda rows: jax.ShapeDtypeStruct((H_SB, rows, LANES), BF16)
    new_k, new_v, new_c, new_r = [], [], [], []

    for l in range(depth):
        h = _norm_mod(x3, g_norm_mix.reshape(depth, 1, d), modg, l, mod_idx(l, 1), mod_idx(l, 0))
        h = h.reshape(n, d)

        def plain_hm(accs, ex, outs):
            _store_heads(outs[0], accs[0])

        def f32_and_hm(accs, ex, outs):
            outs[0][...] = accs[0]
            _store_heads(outs[1], accs[0])

        sb_q = _matmul([(h, w_in, _wspec(l, d, tn, off_q), True)], m=n, n_out=sb_w, tm=tm, tn=tn,
                       epilogue=plain_hm, out_shape=[hm(n)], out_specs=[_hm_spec(tm, tn)])[0]
        k_f32, sb_k = _matmul([(h, w_in, _wspec(l, d, tn, off_k), True)], m=n, n_out=sb_w, tm=tm,
                              tn=tn, epilogue=f32_and_hm,
                              out_shape=[jax.ShapeDtypeStruct((n, sb_w), F32), hm(n)],
                              out_specs=[_spec2(tm, tn), _hm_spec(tm, tn)])
        v_f32, sb_v = _matmul([(h, w_in, _wspec(l, d, tn, off_v), True)], m=n, n_out=sb_w, tm=tm,
                              tn=tn, epilogue=f32_and_hm,
                              out_shape=[jax.ShapeDtypeStruct((n, sb_w), F32), hm(n)],
                              out_specs=[_spec2(tm, tn), _hm_spec(tm, tn)])

        def plain_f32(accs, ex, outs):
            outs[0][...] = accs[0]

        c_q = _matmul([(h, w_in, _wspec(l, d, tn, off_cq), True)], m=n, n_out=q_lora, tm=tm, tn=tn,
                      epilogue=plain_f32, out_shape=[jax.ShapeDtypeStruct((n, q_lora), F32)],
                      out_specs=[_spec2(tm, tn)])[0]

        def ckv_epi(accs, ex, outs):
            y = _rms(accs[0], ex[0][...])
            outs[0][...] = y
            outs[1][...] = y.astype(BF16)

        c_kv, c_kv_b = _matmul(
            [(h, w_in, _wspec(l, d, kv_lora, off_ckv), True)], m=n, n_out=kv_lora, tm=tm, tn=kv_lora,
            epilogue=ckv_epi, extras=[g_kv_lat.reshape(depth, 1, kv_lora)],
            extra_specs=[pl.BlockSpec((None, 1, kv_lora), lambda j, i: (l, 0, 0))],
            out_shape=[jax.ShapeDtypeStruct((n, kv_lora), F32),
                       jax.ShapeDtypeStruct((n, kv_lora), BF16)],
            out_specs=[_spec2(tm, kv_lora), _spec2(tm, kv_lora)])

        def kr_epi(accs, ex, outs):
            r = _rope_lanes(accs[0], ex[0][...], ex[1][...])
            outs[0][...] = r
            outs[1][...] = r.astype(BF16)

        rope_specs = [pl.BlockSpec((tm, LANES), lambda j, i: (i, 0))] * 2
        k_r, k_r_b = _matmul(
            [(h, w_kr_aug, pl.BlockSpec((None, d, LANES), lambda j, i: (l, 0, 0)), False)],
            m=n, n_out=LANES, tm=tm, tn=LANES, epilogue=kr_epi,
            extras=[rope_c, rope_s], extra_specs=rope_specs,
            out_shape=[jax.ShapeDtypeStruct((n, LANES), F32), jax.ShapeDtypeStruct((n, LANES), BF16)],
            out_specs=[_spec2(tm, LANES), _spec2(tm, LANES)])

        def gate_epi(accs, ex, outs):
            outs[0][...] = jax.nn.sigmoid(accs[0])

        gates = _matmul(
            [(h, w_gates, pl.BlockSpec((None, d, tn), lambda j, i: (l, 0, j)), False)],
            m=n, n_out=2 * d, tm=tm, tn=tn, epilogue=gate_epi,
            out_shape=[jax.ShapeDtypeStruct((n, 2 * d), F32)], out_specs=[_spec2(tm, tn)])[0]

        def cq_prologue(a, ex):
            return _rms(a, ex[0][...])

        gq_spec = pl.BlockSpec((None, 1, q_lora), lambda j, i: (l, 0, 0))
        gq = g_q_lat.reshape(depth, 1, q_lora)
        wqk = 2 * LANES
        hm_qk = lambda rows: jax.ShapeDtypeStruct((H_MLA, rows, wqk), BF16)

        def qcat_epi(accs, ex, outs):
            c, s = ex[1][...], ex[2][...]
            for hh in range(tn // wqk):
                blk = accs[0][:, hh * wqk:(hh + 1) * wqk]
                outs[0][hh, :, :LANES] = blk[:, :LANES].astype(BF16)
                outs[0][hh, :, LANES:] = _rope_lanes(blk[:, LANES:], c, s).astype(BF16)

        q_cat = _matmul(
            [(c_q, w_uq_cat, pl.BlockSpec((None, q_lora, tn), lambda j, i: (l, 0, j)), True)],
            m=n, n_out=H_MLA * wqk, tm=tm, tn=tn, prologue=cq_prologue, epilogue=qcat_epi,
            extras=[gq, rope_c, rope_s], extra_specs=[gq_spec] + rope_specs,
            out_shape=[hm_qk(n)],
            out_specs=[pl.BlockSpec((tn // wqk, tm, wqk), lambda j, i: (j, i, 0))])[0]

        def kcat_epi(accs, ex, outs):
            kr = ex[0][...]
            for hh in range(tn // LANES):
                outs[0][hh, :, :LANES] = accs[0][:, hh * LANES:(hh + 1) * LANES].astype(BF16)
                outs[0][hh, :, LANES:] = kr

        def up_kv(a, kr, rows, tmr):
            wspec = pl.BlockSpec((None, kv_lora, tn), lambda j, i: (l, 0, j))
            k_cat = _matmul(
                [(a, w_ukn, wspec, True)], m=rows, n_out=H_MLA * QK_NOPE, tm=tmr, tn=tn,
                epilogue=kcat_epi, extras=[kr],
                extra_specs=[pl.BlockSpec((tmr, LANES), lambda j, i: (i, 0))],
                out_shape=[hm_qk(rows)],
                out_specs=[pl.BlockSpec((tn // LANES, tmr, wqk), lambda j, i: (j, i, 0))])[0]
            v = _matmul(
                [(a, w_uv, wspec, True)], m=rows, n_out=H_MLA * V_DIM, tm=tmr, tn=tn,
                epilogue=lambda accs, ex, outs: _store_heads(outs[0], accs[0]),
                out_shape=[hm(rows)], out_specs=[_hm_spec(tmr, tn)])[0]
            return k_cat, v

        kc_new, v_new = up_kv(c_kv_b, k_r_b, n, tm)
        rows_past = bs * past
        kr_past = jnp.pad(cache_mla_krope[l].reshape(rows_past, QK_ROPE),
                          ((0, 0), (0, LANES - QK_ROPE))).astype(BF16)
        kc_past, v_past = up_kv(cache_mla_ckv[l].reshape(rows_past, kv_lora), kr_past, rows_past,
                                512)

        o_sb_p = _sb_attention(sb_q, sb_k, sb_v, sb_k, sb_v, tab_p, heads=H_SB, tq=tq, tk=tq,
                               sub=tq, out_rows=n_p,
                               out_block_of=lambda s, qb, kb, hb, fl: (qb[s], 0))
        cache_k2 = cache_sb_k.reshape(depth * bs * past * H_SB, DH_SB)
        cache_v2 = cache_sb_v.reshape(depth * bs * past * H_SB, DH_SB)
        q_off_s = n_p // t_s
        o_sb_s = _sb_attention(sb_q, sb_k, sb_v, cache_k2, cache_v2, tab_sb_s(l), heads=H_SB,
                               tq=t_s, tk=tk_s, sub=ATTN_TILE, out_rows=n_s,
                               out_block_of=lambda s, qb, kb, hb, fl: (qb[s] - q_off_s, 0))
        o_sb = jnp.concatenate([o_sb_p, o_sb_s], axis=0)

        tp3 = (tab_p[0], tab_p[1], tab_p[3])
        o_mla_p = _mla_attention(q_cat, kc_new, v_new, kc_new, v_new, tp3,
                                 tq=tq, tk=tq, pos0=0, out_rows=n_p,
                                 out_block_of=lambda s, qb, kb, fl: (qb[s], 0))
        ts3 = (tab_mla_s[0], tab_mla_s[1], tab_mla_s[3])
        o_mla_s = _mla_attention(q_cat, kc_new, v_new, kc_past, v_past, ts3,
                                 tq=t_s, tk=tk_s, pos0=past, out_rows=n_s,
                                 out_block_of=lambda s, qb, kb, fl: (qb[s] - q_off_s, 0))
        o_mla = jnp.concatenate([o_mla_p, o_mla_s], axis=0)

        def merge_epi(accs, ex, outs):
            outs[0][...] = (ex[0][...] * accs[0] + ex[1][...] * accs[1]).astype(BF16)

        merged = _matmul(
            [(o_sb, w_branch_sb, _wspec(l, sb_w, tn, 0), True),
             (o_mla, w_branch_mla, _wspec(l, H_MLA * V_DIM, tn, 0), True)],
            m=n, n_out=d, tm=tm, tn=tn, epilogue=merge_epi,
            extras=[gates, gates],
            extra_specs=[pl.BlockSpec((tm, tn), lambda j, i: (i, j)),
                         pl.BlockSpec((tm, tn), lambda j, i: (i, j + d // tn))],
            out_shape=[jax.ShapeDtypeStruct((n, d), BF16)], out_specs=[_spec2(tm, tn)])[0]

        def resid_epi(accs, ex, outs):
            for g in range(gpt):
                outs[0][g] = ex[0][g] + ex[1][g] * accs[0][g * grp:(g + 1) * grp, :]

        x_spec = pl.BlockSpec((gpt, grp, tn), lambda j, i: (i, 0, j))
        x3 = _matmul(
            [(merged, w_out, _wspec(l, d, tn, 0), True)], m=n, n_out=d, tm=tm, tn=tn,
            epilogue=resid_epi, extras=[x3, modg], extra_specs=[x_spec, mspec_rows(l, 2)],
            out_shape=[jax.ShapeDtypeStruct((ng, grp, d), F32)], out_specs=[x_spec])[0]

        h2, route = _norm_route(x3, g_norm_ffn.reshape(depth, 1, d), modg, l, mod_idx(l, 4),
                                mod_idx(l, 3), w_router3, b_router)
        row_tok, row_w, pos_rows, tile_expert, n_valid = _dispatch(route.reshape(n, LANES), n)
        xs = _gather_rows(h2, row_tok, d // LANES)
        ys = _moe_experts(xs, row_w, tile_expert, n_valid, w_exp_gate, w_exp_up, w_exp_down, l)
        yg = _gather_rows(ys, pos_rows, d // LANES)
        x3 = _combine(x3, yg, modg, l, mod_idx(l, 5))

        new_k.append(k_f32)
        new_v.append(v_f32)
        new_c.append(c_kv)
        new_r.append(k_r[:, :QK_ROPE])

    y = _final_norm(x3, g_final.reshape(1, d)).reshape(n, d)

    def split(parts, tail):
        a = jnp.stack(parts)
        return (a[:, :n_p].reshape((depth, bp, t_p) + tail),
                a[:, n_p:].reshape((depth, bs, t_s) + tail))

    pk, sk = split(new_k, (H_SB, DH_SB))
    pv, sv = split(new_v, (H_SB, DH_SB))
    pc, sc = split(new_c, (kv_lora,))
    pr, sr = split(new_r, (QK_ROPE,))
    return (y[:n_p].reshape(bp, t_p, d), y[n_p:].reshape(bs, t_s, d),
            pk, pv, pc, pr, sk, sv, sc, sr)
```

```python
import functools

import numpy as np
import jax
import jax.numpy as jnp
from jax import lax
from jax.experimental import pallas as pl
from jax.experimental.pallas import tpu as pltpu

F32 = jnp.float32
BF16 = jnp.bfloat16

CHUNK = 64
H_SB = 16
DH_SB = 128
H_MLA = 16
QK_NOPE = 128
QK_ROPE = 64
V_DIM = 128
ROPE_THETA = 10000.0
N_GROUPS = 4
EXPERTS_PER_GROUP = 8
N_EXPERTS = N_GROUPS * EXPERTS_PER_GROUP
N_MOD = 6
EPS = 1e-6

LANES = 128
ATTN_TILE = 256
MOE_TILE = 256
HEAD_UNROLL = 4
SB_DEAD = 104.0
VMEM_LIMIT = 56 * 1024 * 1024


def _cparams(n_axes, vmem=VMEM_LIMIT):
    return pltpu.CompilerParams(dimension_semantics=("arbitrary",) * n_axes,
                                vmem_limit_bytes=vmem)


def _row_tile(n, cap=512):
    t = cap
    while n % t:
        t //= 2
    return t


def _matmul(pairs, *, m, n_out, tm, tn, epilogue, out_shape, out_specs,
            extras=(), extra_specs=(), prologue=None):
    n_pairs = len(pairs)
    n_ex = len(extras)
    n_outs = len(out_shape)
    cast = [p[3] for p in pairs]

    def kern(*refs):
        a_refs = refs[0:2 * n_pairs:2]
        b_refs = refs[1:2 * n_pairs:2]
        ex = refs[2 * n_pairs:2 * n_pairs + n_ex]
        outs = refs[2 * n_pairs + n_ex:2 * n_pairs + n_ex + n_outs]
        scr = refs[2 * n_pairs + n_ex + n_outs:]
        i = pl.program_id(1)
        accs = []
        si = 0
        for p in range(n_pairs):
            if cast[p]:
                bsc = scr[si]
                si += 1

                @pl.when(i == 0)
                def _(bsc=bsc, b_ref=b_refs[p]):
                    bsc[...] = b_ref[...].astype(BF16)

                bv = bsc[...]
            else:
                bv = b_refs[p][...]
            a = a_refs[p][...]
            if prologue is not None:
                a = prologue(a, ex)
            accs.append(jnp.dot(a.astype(BF16), bv, preferred_element_type=F32))
        epilogue(accs, ex, outs)

    in_specs, args, scratch = [], [], []
    for (a, b, b_spec, cb) in pairs:
        k = a.shape[1]
        in_specs += [pl.BlockSpec((tm, k), lambda j, i: (i, 0)), b_spec]
        args += [a, b]
        if cb:
            scratch.append(pltpu.VMEM((k, tn), BF16))
    in_specs += list(extra_specs)
    args += list(extras)
    return pl.pallas_call(
        kern,
        grid=(n_out // tn, m // tm),
        in_specs=in_specs,
        out_specs=out_specs,
        out_shape=out_shape,
        scratch_shapes=scratch,
        compiler_params=_cparams(2),
    )(*args)


def _wspec(l, k, tn, col_off):
    cb = col_off // tn
    assert cb * tn == col_off
    return pl.BlockSpec((None, k, tn), lambda j, i: (l, 0, cb + j))


def _spec2(tm, tn):
    return pl.BlockSpec((tm, tn), lambda j, i: (i, j))


def _hm_spec(tm, tn):
    return pl.BlockSpec((tn // LANES, tm, LANES), lambda j, i: (j, i, 0))


def _store_heads(o_ref, val):
    for c in range(val.shape[1] // LANES):
        o_ref[c] = val[:, c * LANES:(c + 1) * LANES].astype(o_ref.dtype)


def _rms(x, g):
    return x * lax.rsqrt(jnp.mean(x * x, axis=-1, keepdims=True) + EPS) * g


def _norm_mod_kernel(x_ref, g_ref, sc_ref, sh_ref, o_ref):
    x = x_ref[...]
    y = _rms(x, g_ref[...])
    o_ref[...] = (y * (1.0 + sc_ref[...]) + sh_ref[...]).astype(o_ref.dtype)


def _split3(x):
    hi = x.astype(BF16)
    r = x - hi.astype(F32)
    mid = r.astype(BF16)
    lo = (r - mid.astype(F32)).astype(BF16)
    return hi, mid, lo


def _dot_f32(a, b3):
    a_hi, a_mid, a_lo = _split3(a)
    b_hi, b_mid, b_lo = b3
    d = functools.partial(jnp.dot, preferred_element_type=F32)
    small = d(a_hi, b_lo) + d(a_lo, b_hi) + d(a_mid, b_mid)
    return (d(a_hi, b_hi) + (d(a_hi, b_mid) + d(a_mid, b_hi))) + small


def _route(logits):
    lane = lax.broadcasted_iota(jnp.int32, logits.shape, 1)
    lanef = lane.astype(F32)
    big = jnp.float32(1e9)
    ninf = jnp.float32(-jnp.inf)
    is_g = (lane >= N_EXPERTS) & (lane < N_EXPERTS + N_GROUPS)
    gl = jnp.where(is_g, logits, ninf)
    gmax = jnp.max(gl, axis=1, keepdims=True)
    g_idx = jnp.min(jnp.where(gl == gmax, lanef - N_EXPERTS, big), axis=1, keepdims=True)
    p_group = 1.0 / jnp.sum(jnp.where(is_g, jnp.exp(gl - gmax), 0.0), axis=1, keepdims=True)
    grp = jnp.floor(lanef * (1.0 / EXPERTS_PER_GROUP))
    in_g = (lane < N_EXPERTS) & (grp == g_idx)
    el = jnp.where(in_g, logits, ninf)
    e1 = jnp.max(el, axis=1, keepdims=True)
    i1 = jnp.min(jnp.where(el == e1, lanef, big), axis=1, keepdims=True)
    el2 = jnp.where(lanef == i1, ninf, el)
    e2 = jnp.max(el2, axis=1, keepdims=True)
    i2 = jnp.min(jnp.where(el2 == e2, lanef, big), axis=1, keepdims=True)
    t = jnp.exp(e2 - e1)
    den = 1.0 + t
    w1 = (1.0 / den) * p_group
    w2 = (t / den) * p_group
    out = jnp.where(lane == 0, i1, jnp.where(lane == 1, i2,
          jnp.where(lane == 2, w1, jnp.where(lane == 3, w2, 0.0))))
    return out


def _slab_store(ref, row0, val):
    rows, d = val.shape
    s = d // LANES
    for c in range(s):
        ref[pl.ds(row0 * s + c, rows, stride=s), :] = val[:, c * LANES:(c + 1) * LANES]


def _slab_load(ref, row0, rows, s, lead=None):
    pieces = []
    for c in range(s):
        rs = pl.ds(row0 * s + c, rows, stride=s)
        pieces.append(ref[rs, :] if lead is None else ref[lead, rs, :])
    return jnp.concatenate(pieces, axis=1)


def _norm_route_kernel(x_ref, g_ref, sc_ref, sh_ref, wr_ref, br_ref, h_ref, r_ref):
    x = x_ref[...]
    y = _rms(x, g_ref[...])
    h = y * (1.0 + sc_ref[...]) + sh_ref[...]
    b3 = (wr_ref[0], wr_ref[1], wr_ref[2])
    grp = x.shape[1]
    for g in range(x.shape[0]):
        _slab_store(h_ref, g * grp, h[g])
        logits = _dot_f32(h[g], b3) + br_ref[...]
        r_ref[g] = _route(logits)


def _mod_spec(idx, gb, d):
    return pl.BlockSpec((None, gb, 1, d), lambda i: (idx, i, 0, 0))


def _norm_mod(x3, g, modg, l, sc_i, sh_i):
    ng, grp, d = x3.shape
    gb = _row_tile(ng, 4)
    return pl.pallas_call(
        _norm_mod_kernel,
        grid=(ng // gb,),
        in_specs=[pl.BlockSpec((gb, grp, d), lambda i: (i, 0, 0)),
                  pl.BlockSpec((None, 1, d), lambda i: (l, 0, 0)),
                  _mod_spec(sc_i, gb, d), _mod_spec(sh_i, gb, d)],
        out_specs=pl.BlockSpec((gb, grp, d), lambda i: (i, 0, 0)),
        out_shape=jax.ShapeDtypeStruct(x3.shape, BF16),
        compiler_params=_cparams(1),
    )(x3, g, modg, modg)


def _norm_route(x3, g, modg, l, sc_i, sh_i, wr3, br):
    ng, grp, d = x3.shape
    gb = _row_tile(ng, 4)
    return pl.pallas_call(
        _norm_route_kernel,
        grid=(ng // gb,),
        in_specs=[pl.BlockSpec((gb, grp, d), lambda i: (i, 0, 0)),
                  pl.BlockSpec((None, 1, d), lambda i: (l, 0, 0)),
                  _mod_spec(sc_i, gb, d), _mod_spec(sh_i, gb, d),
                  pl.BlockSpec((None, 3, d, LANES), lambda i: (l, 0, 0, 0)),
                  pl.BlockSpec((None, 1, LANES), lambda i: (l, 0, 0))],
        out_specs=[pl.BlockSpec((gb * grp * (d // LANES), LANES), lambda i: (i, 0)),
                   pl.BlockSpec((gb, grp, LANES), lambda i: (i, 0, 0))],
        out_shape=[jax.ShapeDtypeStruct((ng * grp * (d // LANES), LANES), F32),
                   jax.ShapeDtypeStruct((ng, grp, LANES), F32)],
        compiler_params=_cparams(1),
    )(x3, g, modg, modg, wr3, br)


def _final_norm_kernel(x_ref, g_ref, o_ref):
    o_ref[...] = _rms(x_ref[...], g_ref[...])


def _final_norm(x3, g):
    ng, grp, d = x3.shape
    gb = _row_tile(ng, 4)
    return pl.pallas_call(
        _final_norm_kernel,
        grid=(ng // gb,),
        in_specs=[pl.BlockSpec((gb, grp, d), lambda i: (i, 0, 0)),
                  pl.BlockSpec((1, d), lambda i: (0, 0))],
        out_specs=pl.BlockSpec((gb, grp, d), lambda i: (i, 0, 0)),
        out_shape=jax.ShapeDtypeStruct(x3.shape, F32),
        compiler_params=_cparams(1),
    )(x3, g)


def _lanes(c, w):
    if w % LANES == 0:
        return c if w == LANES else jnp.tile(c, (1, w // LANES))
    return c[:, :w]


def _dot_nt(a, b):
    return lax.dot_general(a, b, (((1,), (1,)), ((), ())), preferred_element_type=F32)


def _sb_block(qh, kh, vh, c, u, scale, masked):
    w = kh.shape[0]
    z = _dot_nt(qh, kh) * scale
    sp = jnp.maximum(z, 0.0) + jnp.log1p(jnp.exp(-jnp.abs(z)))
    if masked:
        row = lax.broadcasted_iota(jnp.int32, z.shape, 0)
        col = lax.broadcasted_iota(jnp.int32, z.shape, 1)
        valid = col < row
        sp = jnp.where(valid, sp, 0.0)
    hi = sp.astype(BF16)
    lo = (sp - hi.astype(F32)).astype(BF16)
    cs = jnp.dot(hi, u, preferred_element_type=F32) + jnp.dot(lo, u, preferred_element_type=F32)
    wgt = jnp.exp(z - sp - cs - _lanes(c, w))
    if masked:
        wgt = jnp.where(valid, wgt, 0.0)
    o = jnp.dot(wgt.astype(BF16), vh, preferred_element_type=F32)
    c_new = c + jnp.sum(sp, axis=1, keepdims=True)
    return o, c_new


def _sb_kernel(qb_ref, kb_ref, hb_ref, fl_ref, q_ref, kn_ref, vn_ref, kp_ref, vp_ref,
               ud_ref, up_ref, o_ref, acc, carry, done, *, heads, scale, sub, n_sub):
    s = pl.program_id(0)
    fl = fl_ref[s]
    is_first = (fl & 1) != 0
    is_last = (fl & 2) != 0

    def past_block(ref, h, j):
        if len(ref.shape) == 3:
            return ref[h, pl.ds(j * sub, sub), :].astype(BF16)
        return ref[pl.ds(j * sub * heads + h, sub, stride=heads), :].astype(BF16)

    @pl.when(is_first)
    def _():
        def body(h, _):
            o, c = _sb_block(q_ref[h], kn_ref[h], vn_ref[h], jnp.zeros(carry.shape[1:], F32),
                             ud_ref[...], scale, True)
            acc[h] = o
            carry[h] = c
            done[h] = 0
            return 0
        lax.fori_loop(0, heads, body, 0, unroll=min(heads, HEAD_UNROLL))

    @pl.when(jnp.logical_not(is_first))
    def _():
        def body(h, _):
            @pl.when(done[h] == 0)
            def _():
                qh = q_ref[h]
                c = carry[h]
                o_tot = acc[h]
                for j in reversed(range(n_sub)):
                    o, c = _sb_block(qh, past_block(kp_ref, h, j), past_block(vp_ref, h, j), c,
                                     up_ref[...], scale, False)
                    o_tot = o_tot + o
                acc[h] = o_tot
                carry[h] = c
                done[h] = (jnp.min(c) >= SB_DEAD).astype(jnp.int32)
            return 0
        lax.fori_loop(0, heads, body, 0, unroll=min(heads, HEAD_UNROLL))

    @pl.when(is_last)
    def _():
        for h in range(heads):
            o_ref[:, h * DH_SB:(h + 1) * DH_SB] = acc[h].astype(o_ref.dtype)


def _suffix_matrix(w):
    j = np.arange(w)[:, None]
    s = np.arange(w)[None, :]
    return jnp.asarray((j > s).astype(np.float32), dtype=BF16)


def _sb_attention(q, kn, vn, kp, vp, tables, *, heads, tq, tk, sub, out_rows, out_block_of):
    qb, kb, hb, fl = tables
    n_steps = qb.shape[0]
    hblk = heads

    def qmap(s, qb, kb, hb, fl):
        return (0, qb[s], 0)

    if kp.ndim == 3:
        past_spec = pl.BlockSpec((hblk, tk, DH_SB), lambda s, qb, kb, hb, fl: (0, kb[s], 0))
    else:
        past_spec = pl.BlockSpec((tk * hblk, DH_SB), lambda s, qb, kb, hb, fl: (kb[s], 0))
    new_spec = pl.BlockSpec((hblk, tq, DH_SB), qmap)
    const = lambda s, qb, kb, hb, fl: (0, 0)
    grid_spec = pltpu.PrefetchScalarGridSpec(
        num_scalar_prefetch=4,
        grid=(n_steps,),
        in_specs=[new_spec, new_spec, new_spec, past_spec, past_spec,
                  pl.BlockSpec((tq, tq), const), pl.BlockSpec((sub, sub), const)],
        out_specs=pl.BlockSpec((tq, hblk * DH_SB), out_block_of),
        scratch_shapes=[pltpu.VMEM((hblk, tq, DH_SB), F32),
                        pltpu.VMEM((hblk, tq, LANES), F32),
                        pltpu.SMEM((hblk,), jnp.int32)],
    )
    kern = functools.partial(_sb_kernel, heads=hblk, scale=DH_SB ** -0.5, sub=sub,
                             n_sub=tk // sub)
    return pl.pallas_call(
        kern, grid_spec=grid_spec,
        out_shape=jax.ShapeDtypeStruct((out_rows, H_SB * DH_SB), BF16),
        compiler_params=_cparams(1),
    )(qb, kb, hb, fl, q, kn, vn, kp, vp, _suffix_matrix(tq), _suffix_matrix(sub))


def _mla_block(qc, kc, vh, m, l, acc, scale, mask):
    s = _dot_nt(qc, kc) * scale
    if mask is not None:
        s = jnp.where(mask, s, -jnp.inf)
    w = s.shape[1]
    m_new = jnp.maximum(m, jnp.max(s, axis=1, keepdims=True))
    alpha = jnp.exp(m - m_new)
    p = jnp.exp(s - _lanes(m_new, w))
    l_new = alpha * l + jnp.sum(p, axis=1, keepdims=True)
    acc_new = alpha * acc + jnp.dot(p.astype(BF16), vh, preferred_element_type=F32)
    return m_new, l_new, acc_new


def _mla_kernel(qb_ref, kb_ref, fl_ref, q_ref, kn_ref, vn_ref, kp_ref, vp_ref, o_ref,
                acc, m_sc, l_sc, *, heads, scale, pos0):
    s = pl.program_id(0)
    fl = fl_ref[s]
    is_first = (fl & 1) != 0
    is_last = (fl & 2) != 0
    tq = q_ref.shape[1]

    @pl.when(is_first)
    def _():
        row = lax.broadcasted_iota(jnp.int32, (tq, tq), 0) + pos0
        col = lax.broadcasted_iota(jnp.int32, (tq, tq), 1) + pos0
        mask = (col // CHUNK) <= (row // CHUNK)

        def body(h, _):
            m0 = jnp.full((tq, LANES), -jnp.inf, F32)
            l0 = jnp.zeros((tq, LANES), F32)
            a0 = jnp.zeros((tq, V_DIM), F32)
            m, l, a = _mla_block(q_ref[h], kn_ref[h], vn_ref[h], m0, l0, a0, scale, mask)
            m_sc[h] = m
            l_sc[h] = l
            acc[h] = a
            return 0
        lax.fori_loop(0, heads, body, 0, unroll=min(heads, HEAD_UNROLL))

    @pl.when(jnp.logical_not(is_first))
    def _():
        def body(h, _):
            m, l, a = _mla_block(q_ref[h], kp_ref[h], vp_ref[h], m_sc[h], l_sc[h], acc[h],
                                 scale, None)
            m_sc[h] = m
            l_sc[h] = l
            acc[h] = a
            return 0
        lax.fori_loop(0, heads, body, 0, unroll=min(heads, HEAD_UNROLL))

    @pl.when(is_last)
    def _():
        for h in range(heads):
            o_ref[:, h * V_DIM:(h + 1) * V_DIM] = (acc[h] / l_sc[h]).astype(o_ref.dtype)


def _mla_attention(qc, kcn, vn, kcp, vp, tables, *, tq, tk, pos0, out_rows, out_block_of):
    qb, kb, fl = tables
    n_steps = qb.shape[0]
    h = H_MLA
    wqk = 2 * LANES
    newq = pl.BlockSpec((h, tq, wqk), lambda s, qb, kb, fl: (0, qb[s], 0))
    newv = pl.BlockSpec((h, tq, V_DIM), lambda s, qb, kb, fl: (0, qb[s], 0))
    pastk = pl.BlockSpec((h, tk, wqk), lambda s, qb, kb, fl: (0, kb[s], 0))
    pastv = pl.BlockSpec((h, tk, V_DIM), lambda s, qb, kb, fl: (0, kb[s], 0))
    grid_spec = pltpu.PrefetchScalarGridSpec(
        num_scalar_prefetch=3,
        grid=(n_steps,),
        in_specs=[newq, newq, newv, pastk, pastv],
        out_specs=pl.BlockSpec((tq, h * V_DIM), out_block_of),
        scratch_shapes=[pltpu.VMEM((h, tq, V_DIM), F32),
                        pltpu.VMEM((h, tq, LANES), F32),
                        pltpu.VMEM((h, tq, LANES), F32)],
    )
    kern = functools.partial(_mla_kernel, heads=h, scale=(QK_NOPE + QK_ROPE) ** -0.5, pos0=pos0)
    return pl.pallas_call(
        kern, grid_spec=grid_spec,
        out_shape=jax.ShapeDtypeStruct((out_rows, h * V_DIM), BF16),
        compiler_params=_cparams(1),
    )(qb, kb, fl, qc, kcn, vn, kcp, vp)


def _causal_tables(nq, q_off, per_head=False, heads=1, n_batch=1, past_tiles=None,
                   past_stride=0):
    qb, kb, hb, fl = [], [], [], []
    if past_tiles is None:
        for i in range(nq):
            n = i + 1
            for j in range(n):
                qb.append(q_off + i)
                kb.append(max(i - j, 1) - 1 if j == 0 else i - j)
                hb.append(0)
                fl.append((1 if j == 0 else 0) | (2 if j == n - 1 else 0))
    else:
        for b in range(n_batch):
            for h in range(heads if per_head else 1):
                n = 1 + past_tiles
                for j in range(n):
                    qb.append(q_off + b)
                    jj = past_tiles - 1 if j == 0 else past_tiles - j
                    kb.append(past_stride + b * past_tiles + jj)
                    hb.append(h)
                    fl.append((1 if j == 0 else 0) | (2 if j == n - 1 else 0))
    arr = lambda v: jnp.asarray(np.asarray(v, dtype=np.int32))
    return arr(qb), arr(kb), arr(hb), arr(fl)


ROW_DMA_UNROLL = 8


def _moe_kernel(te_ref, nv_ref, tok_ref, dst_ref, h_hbm, w_ref, wg_ref, wu_ref, wd_ref, y_hbm,
                xbuf, obuf, wg_s, wu_s, wd_s, sem_in, sem_out, *, s, n_dest):
    t = pl.program_id(0)
    tm = w_ref.shape[0]
    nv = nv_ref[0]
    live = t < nv
    slot = lax.rem(t, 2)

    def in_copy(tile, r, sl):
        src0 = pl.multiple_of(tok_ref[tile * tm + r] * s, s)
        return pltpu.make_async_copy(h_hbm.at[pl.ds(src0, s)],
                                     xbuf.at[sl, pl.ds(pl.multiple_of(r * s, s), s)],
                                     sem_in.at[sl])

    def gather_start(tile, sl):
        def body(r, _):
            in_copy(tile, r, sl).start()
            return 0
        lax.fori_loop(0, tm, body, 0, unroll=ROW_DMA_UNROLL)

    def gather_wait(tile, sl):
        def body(r, _):
            in_copy(tile, r, sl).wait()
            return 0
        lax.fori_loop(0, tm, body, 0, unroll=ROW_DMA_UNROLL)

    @pl.when(live & (t == 0))
    def _():
        gather_start(0, 0)

    @pl.when(t + 1 < nv)
    def _():
        gather_start(t + 1, 1 - slot)

    prev = te_ref[jnp.maximum(t - 1, 0)]
    fresh = (t == 0) | (te_ref[t] != prev)

    @pl.when(live & fresh)
    def _():
        wg_s[...] = wg_ref[...].astype(BF16)
        wu_s[...] = wu_ref[...].astype(BF16)
        wd_s[...] = wd_ref[...].astype(BF16)

    @pl.when(live)
    def _():
        gather_wait(t, slot)
        x = _slab_load(xbuf, 0, tm, s, lead=slot).astype(BF16)
        a = jnp.dot(x, wg_s[...], preferred_element_type=F32)
        u = jnp.dot(x, wu_s[...], preferred_element_type=F32)
        hid = (a * jax.nn.sigmoid(a)) * u * w_ref[...]
        _slab_store(obuf, 0, jnp.dot(hid.astype(BF16), wd_s[...], preferred_element_type=F32))

        def out_copy(r):
            dst0 = pl.multiple_of(dst_ref[t * tm + r] * s, s)
            return pltpu.make_async_copy(obuf.at[pl.ds(pl.multiple_of(r * s, s), s)],
                                         y_hbm.at[pl.ds(dst0, s)], sem_out)

        def start(r, _):
            @pl.when(dst_ref[t * tm + r] < n_dest)
            def _():
                out_copy(r).start()
            return 0
        lax.fori_loop(0, tm, start, 0, unroll=ROW_DMA_UNROLL)

        def wait(r, _):
            @pl.when(dst_ref[t * tm + r] < n_dest)
            def _():
                out_copy(r).wait()
            return 0
        lax.fori_loop(0, tm, wait, 0, unroll=ROW_DMA_UNROLL)


def _moe_experts(h_slab, row_tok, row_dst, row_w, tile_expert, n_valid, w_gate, w_up, w_down, l,
                 n_dest):
    d, f = w_gate.shape[-2:]
    s = d // LANES
    tm = MOE_TILE
    n_tiles = row_tok.shape[0] // tm
    wmap = lambda t, te, nv, tok, dst: (l, te[t], 0, 0)
    grid_spec = pltpu.PrefetchScalarGridSpec(
        num_scalar_prefetch=4, grid=(n_tiles,),
        in_specs=[pl.BlockSpec(memory_space=pl.ANY),
                  pl.BlockSpec((tm, 1), lambda t, te, nv, tok, dst: (t, 0)),
                  pl.BlockSpec((None, None, d, f), wmap),
                  pl.BlockSpec((None, None, d, f), wmap),
                  pl.BlockSpec((None, None, f, d), wmap)],
        out_specs=pl.BlockSpec(memory_space=pl.ANY),
        scratch_shapes=[pltpu.VMEM((2, tm * s, LANES), F32), pltpu.VMEM((tm * s, LANES), F32),
                        pltpu.VMEM((d, f), BF16), pltpu.VMEM((d, f), BF16),
                        pltpu.VMEM((f, d), BF16),
                        pltpu.SemaphoreType.DMA((2,)), pltpu.SemaphoreType.DMA(())],
    )
    return pl.pallas_call(
        functools.partial(_moe_kernel, s=s, n_dest=n_dest), grid_spec=grid_spec,
        out_shape=jax.ShapeDtypeStruct((n_dest * s, LANES), F32),
        compiler_params=_cparams(1),
    )(tile_expert, n_valid, row_tok, row_dst, h_slab, row_w, w_gate, w_up, w_down)


def _combine_kernel(x_ref, y0_ref, y1_ref, g_ref, o_ref):
    gb, grp, d = x_ref.shape
    s = d // LANES
    for g in range(gb):
        y = _slab_load(y0_ref, g * grp, grp, s) + _slab_load(y1_ref, g * grp, grp, s)
        o_ref[g] = x_ref[g] + g_ref[g] * y


def _combine(x3, yg, modg, l, g_i):
    ng, grp, d = x3.shape
    s = d // LANES
    gb = _row_tile(ng, 2)
    blk = pl.BlockSpec((gb, grp, d), lambda i: (i, 0, 0))
    yblk = lambda off: pl.BlockSpec((gb * grp * s, LANES), lambda i: (i + off, 0))
    return pl.pallas_call(
        _combine_kernel,
        grid=(ng // gb,),
        in_specs=[blk, yblk(0), yblk(ng // gb), _mod_spec(g_i, gb, d)],
        out_specs=blk,
        out_shape=jax.ShapeDtypeStruct(x3.shape, F32),
        compiler_params=_cparams(1),
    )(x3, yg, yg, modg)


def _dispatch(route, n):
    tm = MOE_TILE
    r_cap = (2 * n + N_EXPERTS * (tm - 1) + tm - 1) // tm * tm
    e = route[:, :2].astype(jnp.int32)
    w = route[:, 2:4]
    flat_e = e.T.reshape(-1)
    flat_w = w.T.reshape(-1)
    flat_tok = jnp.tile(jnp.arange(n, dtype=jnp.int32), 2)
    order = jnp.argsort(flat_e, stable=True)
    sorted_e = flat_e[order]
    counts = jnp.bincount(flat_e, length=N_EXPERTS).astype(jnp.int32)
    padded = (counts + tm - 1) // tm * tm
    ends_p = jnp.cumsum(padded)
    starts_p = ends_p - padded
    starts = jnp.cumsum(counts) - counts
    rank = jnp.arange(2 * n, dtype=jnp.int32) - starts[sorted_e]
    dest_sorted = starts_p[sorted_e] + rank
    row_tok = jnp.zeros((r_cap,), jnp.int32).at[dest_sorted].set(flat_tok[order])
    row_w = jnp.zeros((r_cap,), F32).at[dest_sorted].set(flat_w[order])
    row_dst = jnp.full((r_cap,), 2 * n, jnp.int32).at[dest_sorted].set(order.astype(jnp.int32))
    tile_start = jnp.arange(r_cap // tm, dtype=jnp.int32) * tm
    tile_expert = jnp.minimum(jnp.searchsorted(ends_p, tile_start, side="right"),
                              N_EXPERTS - 1).astype(jnp.int32)
    n_valid = (ends_p[-1:] // tm).astype(jnp.int32)
    return row_tok, row_dst, row_w.reshape(r_cap, 1), tile_expert, n_valid


def _rope_tables(pos):
    inv = ROPE_THETA ** (-jnp.arange(0, QK_ROPE, 2, dtype=F32) / QK_ROPE)
    ang = pos.astype(F32)[:, None] * inv[None, :]
    cos, sin = jnp.cos(ang), jnp.sin(ang)
    pad = jnp.zeros((pos.shape[0], LANES - QK_ROPE), F32)
    return (jnp.concatenate([cos, cos, pad], axis=1),
            jnp.concatenate([-sin, sin, pad], axis=1))


def _rope_lanes(acc, c, s):
    return acc * c + pltpu.roll(acc, LANES - QK_ROPE, axis=1) * s


def _swap_halves(w):
    half = w.shape[-1] // 2
    return jnp.concatenate([w[..., half:], w[..., :half]], axis=-1)


def kernel(x_prompt, x_sample, c_prompt, c_sample, cache_sb_k, cache_sb_v, cache_mla_ckv,
           cache_mla_krope, w_ada, b_ada, g_norm_mix, g_norm_ffn, w_in, g_q_lat, g_kv_lat,
           w_uq, w_ukv, w_branch_sb, w_branch_mla, w_out, w_router_group, b_router_group,
           w_router_expert, b_router_expert, w_exp_gate, w_exp_up, w_exp_down, g_final):
    bp, t_p, d = x_prompt.shape
    bs, t_s, _ = x_sample.shape
    depth = w_in.shape[0]
    past = cache_sb_k.shape[2]
    grp = t_s
    n_p, n_s = bp * t_p, bs * t_s
    n = n_p + n_s
    ng = n // grp
    sb_w = H_SB * DH_SB
    q_lora = g_q_lat.shape[1]
    kv_lora = g_kv_lat.shape[1]
    tm = _row_tile(n)
    gpt = tm // grp
    tn = 512
    assert bp == 1 and t_p % ATTN_TILE == 0 and t_p % grp == 0 and past % 512 == 0

    x3 = jnp.concatenate([x_prompt.reshape(n_p // grp, grp, d), x_sample], axis=0)

    n_c = bp + bs
    c_rows = 16
    c_all = jnp.zeros((c_rows, d), F32).at[:n_c].set(jnp.concatenate([c_prompt, c_sample], 0))
    n_modc = N_MOD * d

    def ada_epi(accs, ex, outs):
        outs[0][...] = accs[0] + ex[0][...]

    mods = []
    for l in range(depth):
        mods.append(_matmul(
            [(c_all, w_ada, _wspec(l, d, tn, 0), True)], m=c_rows, n_out=n_modc, tm=c_rows, tn=tn,
            prologue=lambda a, ex: a * jax.nn.sigmoid(a),
            epilogue=ada_epi,
            extras=[b_ada.reshape(depth, 1, n_modc)],
            extra_specs=[pl.BlockSpec((None, 1, tn), lambda j, i, l=l: (l, 0, j))],
            out_shape=[jax.ShapeDtypeStruct((c_rows, n_modc), F32)],
            out_specs=[_spec2(c_rows, tn)])[0])
    mod = jnp.stack(mods)
    modg = jnp.concatenate(
        [jnp.broadcast_to(mod[:, :bp], (depth, n_p // grp, n_modc)), mod[:, bp:n_c]], axis=1)
    modg = modg.reshape(depth, ng, N_MOD, d).transpose(0, 2, 1, 3)
    modg = modg.reshape(depth * N_MOD, ng, 1, d)

    def mod_idx(l, k):
        return l * N_MOD + k

    def mspec_rows(l, k):
        return pl.BlockSpec((None, gpt, 1, tn), lambda j, i: (mod_idx(l, k), i, 0, j))

    pos = jnp.concatenate([jnp.arange(t_p, dtype=jnp.int32),
                           jnp.tile(past + jnp.arange(t_s, dtype=jnp.int32), bs)])
    rope_c, rope_s = _rope_tables(pos)
    tq = ATTN_TILE
    nq_p = t_p // tq
    tab_p = _causal_tables(nq_p, 0)
    tk_s = 512
    pt = past // tk_s
    tab_sb_s = lambda l: _causal_tables(0, n_p // t_s, n_batch=bs, past_tiles=pt,
                                        past_stride=l * bs * pt)
    tab_mla_s = _causal_tables(0, n_p // t_s, n_batch=bs, past_tiles=pt)

    off_q, off_k, off_v = 0, sb_w, 2 * sb_w
    off_cq = 3 * sb_w
    off_ckv = off_cq + q_lora
    off_kr = off_ckv + kv_lora
    off_g = off_kr + QK_ROPE
    w_kr = w_in[:, :, off_kr:off_kr + QK_ROPE]
    w_kr_aug = jnp.concatenate([w_kr, _swap_halves(w_kr)], axis=-1).astype(BF16)
    w_gates = w_in[:, :, off_g:].astype(BF16)
    uq = w_uq.reshape(depth, q_lora, H_MLA, QK_NOPE + QK_ROPE)
    uq_r = uq[..., QK_NOPE:]
    w_uq_cat = jnp.concatenate([uq[..., :QK_NOPE], uq_r, _swap_halves(uq_r)], axis=-1)
    w_uq_cat = w_uq_cat.reshape(depth, q_lora, H_MLA * 2 * LANES)
    ukv = w_ukv.reshape(depth, kv_lora, H_MLA, QK_NOPE + V_DIM)
    w_ukn = ukv[..., :QK_NOPE].reshape(depth, kv_lora, H_MLA * QK_NOPE)
    w_uv = ukv[..., QK_NOPE:].reshape(depth, kv_lora, H_MLA * V_DIM)
    w_router = jnp.concatenate(
        [w_router_expert, w_router_group,
         jnp.zeros((depth, d, LANES - N_EXPERTS - N_GROUPS), F32)], axis=-1)
    r_hi = w_router.astype(BF16)
    r_res = w_router - r_hi.astype(F32)
    r_mid = r_res.astype(BF16)
    r_lo = (r_res - r_mid.astype(F32)).astype(BF16)
    w_router3 = jnp.stack([r_hi, r_mid, r_lo], axis=1)
    b_router = jnp.concatenate(
        [b_router_expert, b_router_group,
         jnp.zeros((depth, LANES - N_EXPERTS - N_GROUPS), F32)], axis=-1).reshape(depth, 1, LANES)

    hm = lambda rows: jax.ShapeDtypeStruct((H_SB, rows, LANES), BF16)
    new_k, new_v, new_c, new_r = [], [], [], []

    for l in range(depth):
        h = _norm_mod(x3, g_norm_mix.reshape(depth, 1, d), modg, l, mod_idx(l, 1), mod_idx(l, 0))
        h = h.reshape(n, d)

        def plain_hm(accs, ex, outs):
            _store_heads(outs[0], accs[0])

        def f32_and_hm(accs, ex, outs):
            outs[0][...] = accs[0]
            _store_heads(outs[1], accs[0])

        sb_q = _matmul([(h, w_in, _wspec(l, d, tn, off_q), True)], m=n, n_out=sb_w, tm=tm, tn=tn,
                       epilogue=plain_hm, out_shape=[hm(n)], out_specs=[_hm_spec(tm, tn)])[0]
        k_f32, sb_k = _matmul([(h, w_in, _wspec(l, d, tn, off_k), True)], m=n, n_out=sb_w, tm=tm,
                              tn=tn, epilogue=f32_and_hm,
                              out_shape=[jax.ShapeDtypeStruct((n, sb_w), F32), hm(n)],
                              out_specs=[_spec2(tm, tn), _hm_spec(tm, tn)])
        v_f32, sb_v = _matmul([(h, w_in, _wspec(l, d, tn, off_v), True)], m=n, n_out=sb_w, tm=tm,
                              tn=tn, epilogue=f32_and_hm,
                              out_shape=[jax.ShapeDtypeStruct((n, sb_w), F32), hm(n)],
                              out_specs=[_spec2(tm, tn), _hm_spec(tm, tn)])

        def plain_f32(accs, ex, outs):
            outs[0][...] = accs[0]

        c_q = _matmul([(h, w_in, _wspec(l, d, tn, off_cq), True)], m=n, n_out=q_lora, tm=tm, tn=tn,
                      epilogue=plain_f32, out_shape=[jax.ShapeDtypeStruct((n, q_lora), F32)],
                      out_specs=[_spec2(tm, tn)])[0]

        def ckv_epi(accs, ex, outs):
            y = _rms(accs[0], ex[0][...])
            outs[0][...] = y
            outs[1][...] = y.astype(BF16)

        c_kv, c_kv_b = _matmul(
            [(h, w_in, _wspec(l, d, kv_lora, off_ckv), True)], m=n, n_out=kv_lora, tm=tm, tn=kv_lora,
            epilogue=ckv_epi, extras=[g_kv_lat.reshape(depth, 1, kv_lora)],
            extra_specs=[pl.BlockSpec((None, 1, kv_lora), lambda j, i: (l, 0, 0))],
            out_shape=[jax.ShapeDtypeStruct((n, kv_lora), F32),
                       jax.ShapeDtypeStruct((n, kv_lora), BF16)],
            out_specs=[_spec2(tm, kv_lora), _spec2(tm, kv_lora)])

        def kr_epi(accs, ex, outs):
            r = _rope_lanes(accs[0], ex[0][...], ex[1][...])
            outs[0][...] = r
            outs[1][...] = r.astype(BF16)

        rope_specs = [pl.BlockSpec((tm, LANES), lambda j, i: (i, 0))] * 2
        k_r, k_r_b = _matmul(
            [(h, w_kr_aug, pl.BlockSpec((None, d, LANES), lambda j, i: (l, 0, 0)), False)],
            m=n, n_out=LANES, tm=tm, tn=LANES, epilogue=kr_epi,
            extras=[rope_c, rope_s], extra_specs=rope_specs,
            out_shape=[jax.ShapeDtypeStruct((n, LANES), F32), jax.ShapeDtypeStruct((n, LANES), BF16)],
            out_specs=[_spec2(tm, LANES), _spec2(tm, LANES)])

        def gate_epi(accs, ex, outs):
            outs[0][...] = jax.nn.sigmoid(accs[0])

        gates = _matmul(
            [(h, w_gates, pl.BlockSpec((None, d, tn), lambda j, i: (l, 0, j)), False)],
            m=n, n_out=2 * d, tm=tm, tn=tn, epilogue=gate_epi,
            out_shape=[jax.ShapeDtypeStruct((n, 2 * d), F32)], out_specs=[_spec2(tm, tn)])[0]

        def cq_prologue(a, ex):
            return _rms(a, ex[0][...])

        gq_spec = pl.BlockSpec((None, 1, q_lora), lambda j, i: (l, 0, 0))
        gq = g_q_lat.reshape(depth, 1, q_lora)
        wqk = 2 * LANES
        hm_qk = lambda rows: jax.ShapeDtypeStruct((H_MLA, rows, wqk), BF16)

        def qcat_epi(accs, ex, outs):
            c, s = ex[1][...], ex[2][...]
            for hh in range(tn // wqk):
                blk = accs[0][:, hh * wqk:(hh + 1) * wqk]
                outs[0][hh, :, :LANES] = blk[:, :LANES].astype(BF16)
                outs[0][hh, :, LANES:] = _rope_lanes(blk[:, LANES:], c, s).astype(BF16)

        q_cat = _matmul(
            [(c_q, w_uq_cat, pl.BlockSpec((None, q_lora, tn), lambda j, i: (l, 0, j)), True)],
            m=n, n_out=H_MLA * wqk, tm=tm, tn=tn, prologue=cq_prologue, epilogue=qcat_epi,
            extras=[gq, rope_c, rope_s], extra_specs=[gq_spec] + rope_specs,
            out_shape=[hm_qk(n)],
            out_specs=[pl.BlockSpec((tn // wqk, tm, wqk), lambda j, i: (j, i, 0))])[0]

        def kcat_epi(accs, ex, outs):
            kr = ex[0][...]
            for hh in range(tn // LANES):
                outs[0][hh, :, :LANES] = accs[0][:, hh * LANES:(hh + 1) * LANES].astype(BF16)
                outs[0][hh, :, LANES:] = kr

        def up_kv(a, kr, rows, tmr):
            wspec = pl.BlockSpec((None, kv_lora, tn), lambda j, i: (l, 0, j))
            k_cat = _matmul(
                [(a, w_ukn, wspec, True)], m=rows, n_out=H_MLA * QK_NOPE, tm=tmr, tn=tn,
                epilogue=kcat_epi, extras=[kr],
                extra_specs=[pl.BlockSpec((tmr, LANES), lambda j, i: (i, 0))],
                out_shape=[hm_qk(rows)],
                out_specs=[pl.BlockSpec((tn // LANES, tmr, wqk), lambda j, i: (j, i, 0))])[0]
            v = _matmul(
                [(a, w_uv, wspec, True)], m=rows, n_out=H_MLA * V_DIM, tm=tmr, tn=tn,
                epilogue=lambda accs, ex, outs: _store_heads(outs[0], accs[0]),
                out_shape=[hm(rows)], out_specs=[_hm_spec(tmr, tn)])[0]
            return k_cat, v

        kc_new, v_new = up_kv(c_kv_b, k_r_b, n, tm)
        rows_past = bs * past
        kr_past = jnp.pad(cache_mla_krope[l].reshape(rows_past, QK_ROPE),
                          ((0, 0), (0, LANES - QK_ROPE))).astype(BF16)
        kc_past, v_past = up_kv(cache_mla_ckv[l].reshape(rows_past, kv_lora), kr_past, rows_past,
                                512)

        o_sb_p = _sb_attention(sb_q, sb_k, sb_v, sb_k, sb_v, tab_p, heads=H_SB, tq=tq, tk=tq,
                               sub=tq, out_rows=n_p,
                               out_block_of=lambda s, qb, kb, hb, fl: (qb[s], 0))
        cache_k2 = cache_sb_k.reshape(depth * bs * past * H_SB, DH_SB)
        cache_v2 = cache_sb_v.reshape(depth * bs * past * H_SB, DH_SB)
        q_off_s = n_p // t_s
        o_sb_s = _sb_attention(sb_q, sb_k, sb_v, cache_k2, cache_v2, tab_sb_s(l), heads=H_SB,
                               tq=t_s, tk=tk_s, sub=ATTN_TILE, out_rows=n_s,
                               out_block_of=lambda s, qb, kb, hb, fl: (qb[s] - q_off_s, 0))
        o_sb = jnp.concatenate([o_sb_p, o_sb_s], axis=0)

        tp3 = (tab_p[0], tab_p[1], tab_p[3])
        o_mla_p = _mla_attention(q_cat, kc_new, v_new, kc_new, v_new, tp3,
                                 tq=tq, tk=tq, pos0=0, out_rows=n_p,
                                 out_block_of=lambda s, qb, kb, fl: (qb[s], 0))
        ts3 = (tab_mla_s[0], tab_mla_s[1], tab_mla_s[3])
        o_mla_s = _mla_attention(q_cat, kc_new, v_new, kc_past, v_past, ts3,
                                 tq=t_s, tk=tk_s, pos0=past, out_rows=n_s,
                                 out_block_of=lambda s, qb, kb, fl: (qb[s] - q_off_s, 0))
        o_mla = jnp.concatenate([o_mla_p, o_mla_s], axis=0)

        def merge_epi(accs, ex, outs):
            outs[0][...] = (ex[0][...] * accs[0] + ex[1][...] * accs[1]).astype(BF16)

        merged = _matmul(
            [(o_sb, w_branch_sb, _wspec(l, sb_w, tn, 0), True),
             (o_mla, w_branch_mla, _wspec(l, H_MLA * V_DIM, tn, 0), True)],
            m=n, n_out=d, tm=tm, tn=tn, epilogue=merge_epi,
            extras=[gates, gates],
            extra_specs=[pl.BlockSpec((tm, tn), lambda j, i: (i, j)),
                         pl.BlockSpec((tm, tn), lambda j, i: (i, j + d // tn))],
            out_shape=[jax.ShapeDtypeStruct((n, d), BF16)], out_specs=[_spec2(tm, tn)])[0]

        def resid_epi(accs, ex, outs):
            for g in range(gpt):
                outs[0][g] = ex[0][g] + ex[1][g] * accs[0][g * grp:(g + 1) * grp, :]

        x_spec = pl.BlockSpec((gpt, grp, tn), lambda j, i: (i, 0, j))
        x3 = _matmul(
            [(merged, w_out, _wspec(l, d, tn, 0), True)], m=n, n_out=d, tm=tm, tn=tn,
            epilogue=resid_epi, extras=[x3, modg], extra_specs=[x_spec, mspec_rows(l, 2)],
            out_shape=[jax.ShapeDtypeStruct((ng, grp, d), F32)], out_specs=[x_spec])[0]

        h2, route = _norm_route(x3, g_norm_ffn.reshape(depth, 1, d), modg, l, mod_idx(l, 4),
                                mod_idx(l, 3), w_router3, b_router)
        row_tok, row_dst, row_w, tile_expert, n_valid = _dispatch(route.reshape(n, LANES), n)
        yg = _moe_experts(h2, row_tok, row_dst, row_w, tile_expert, n_valid,
                          w_exp_gate, w_exp_up, w_exp_down, l, 2 * n)
        x3 = _combine(x3, yg, modg, l, mod_idx(l, 5))

        new_k.append(k_f32)
        new_v.append(v_f32)
        new_c.append(c_kv)
        new_r.append(k_r[:, :QK_ROPE])

    y = _final_norm(x3, g_final.reshape(1, d)).reshape(n, d)

    def split(parts, tail):
        a = jnp.stack(parts)
        return (a[:, :n_p].reshape((depth, bp, t_p) + tail),
                a[:, n_p:].reshape((depth, bs, t_s) + tail))

    pk, sk = split(new_k, (H_SB, DH_SB))
    pv, sv = split(new_v, (H_SB, DH_SB))
    pc, sc = split(new_c, (kv_lora,))
    pr, sr = split(new_r, (QK_ROPE,))
    return (y[:n_p].reshape(bp, t_p, d), y[n_p:].reshape(bs, t_s, d),
            pk, pv, pc, pr, sk, sv, sc, sr)
```

```python
import functools

import numpy as np
import jax
import jax.numpy as jnp
from jax import lax
from jax.experimental import pallas as pl
from jax.experimental.pallas import tpu as pltpu

F32 = jnp.float32
BF16 = jnp.bfloat16

CHUNK = 64
H_SB = 16
DH_SB = 128
H_MLA = 16
QK_NOPE = 128
QK_ROPE = 64
V_DIM = 128
ROPE_THETA = 10000.0
N_GROUPS = 4
EXPERTS_PER_GROUP = 8
N_EXPERTS = N_GROUPS * EXPERTS_PER_GROUP
N_MOD = 6
EPS = 1e-6

LANES = 128
ATTN_TILE = 256
MOE_TILE = 256
HEAD_UNROLL = 4
MLA_UNROLL = 8
SB_DEAD = 104.0
VMEM_LIMIT = 56 * 1024 * 1024


def _cparams(n_axes, vmem=VMEM_LIMIT):
    return pltpu.CompilerParams(dimension_semantics=("arbitrary",) * n_axes,
                                vmem_limit_bytes=vmem)


def _row_tile(n, cap=512):
    t = cap
    while n % t:
        t //= 2
    return t


def _matmul(pairs, *, m, n_out, tm, tn, epilogue, out_shape, out_specs,
            extras=(), extra_specs=(), prologue=None):
    n_pairs = len(pairs)
    n_ex = len(extras)
    n_outs = len(out_shape)
    cast = [p[3] for p in pairs]

    def kern(*refs):
        a_refs = refs[0:2 * n_pairs:2]
        b_refs = refs[1:2 * n_pairs:2]
        ex = refs[2 * n_pairs:2 * n_pairs + n_ex]
        outs = refs[2 * n_pairs + n_ex:2 * n_pairs + n_ex + n_outs]
        scr = refs[2 * n_pairs + n_ex + n_outs:]
        i = pl.program_id(1)
        accs = []
        si = 0
        for p in range(n_pairs):
            if cast[p]:
                bsc = scr[si]
                si += 1

                @pl.when(i == 0)
                def _(bsc=bsc, b_ref=b_refs[p]):
                    bsc[...] = b_ref[...].astype(BF16)

                bv = bsc[...]
            else:
                bv = b_refs[p][...]
            a = a_refs[p][...]
            if prologue is not None:
                a = prologue(a, ex)
            accs.append(jnp.dot(a.astype(BF16), bv, preferred_element_type=F32))
        epilogue(accs, ex, outs)

    in_specs, args, scratch = [], [], []
    for (a, b, b_spec, cb) in pairs:
        k = a.shape[1]
        in_specs += [pl.BlockSpec((tm, k), lambda j, i: (i, 0)), b_spec]
        args += [a, b]
        if cb:
            scratch.append(pltpu.VMEM((k, tn), BF16))
    in_specs += list(extra_specs)
    args += list(extras)
    return pl.pallas_call(
        kern,
        grid=(n_out // tn, m // tm),
        in_specs=in_specs,
        out_specs=out_specs,
        out_shape=out_shape,
        scratch_shapes=scratch,
        compiler_params=_cparams(2),
    )(*args)


def _wspec(l, k, tn, col_off):
    cb = col_off // tn
    assert cb * tn == col_off
    return pl.BlockSpec((None, k, tn), lambda j, i: (l, 0, cb + j))


def _spec2(tm, tn):
    return pl.BlockSpec((tm, tn), lambda j, i: (i, j))


def _hm_spec(tm, tn):
    return pl.BlockSpec((tn // LANES, tm, LANES), lambda j, i: (j, i, 0))


def _store_heads(o_ref, val):
    for c in range(val.shape[1] // LANES):
        o_ref[c] = val[:, c * LANES:(c + 1) * LANES].astype(o_ref.dtype)


def _rms(x, g):
    return x * lax.rsqrt(jnp.mean(x * x, axis=-1, keepdims=True) + EPS) * g


def _norm_mod_kernel(x_ref, g_ref, sc_ref, sh_ref, o_ref):
    x = x_ref[...]
    y = _rms(x, g_ref[...])
    o_ref[...] = (y * (1.0 + sc_ref[...]) + sh_ref[...]).astype(o_ref.dtype)


def _split3(x):
    hi = x.astype(BF16)
    r = x - hi.astype(F32)
    mid = r.astype(BF16)
    lo = (r - mid.astype(F32)).astype(BF16)
    return hi, mid, lo


def _dot_f32(a, b3):
    a_hi, a_mid, a_lo = _split3(a)
    b_hi, b_mid, b_lo = b3
    d = functools.partial(jnp.dot, preferred_element_type=F32)
    small = d(a_hi, b_lo) + d(a_lo, b_hi) + d(a_mid, b_mid)
    return (d(a_hi, b_hi) + (d(a_hi, b_mid) + d(a_mid, b_hi))) + small


def _route(logits):
    lane = lax.broadcasted_iota(jnp.int32, logits.shape, 1)
    lanef = lane.astype(F32)
    big = jnp.float32(1e9)
    ninf = jnp.float32(-jnp.inf)
    is_g = (lane >= N_EXPERTS) & (lane < N_EXPERTS + N_GROUPS)
    gl = jnp.where(is_g, logits, ninf)
    gmax = jnp.max(gl, axis=1, keepdims=True)
    g_idx = jnp.min(jnp.where(gl == gmax, lanef - N_EXPERTS, big), axis=1, keepdims=True)
    p_group = 1.0 / jnp.sum(jnp.where(is_g, jnp.exp(gl - gmax), 0.0), axis=1, keepdims=True)
    grp = jnp.floor(lanef * (1.0 / EXPERTS_PER_GROUP))
    in_g = (lane < N_EXPERTS) & (grp == g_idx)
    el = jnp.where(in_g, logits, ninf)
    e1 = jnp.max(el, axis=1, keepdims=True)
    i1 = jnp.min(jnp.where(el == e1, lanef, big), axis=1, keepdims=True)
    el2 = jnp.where(lanef == i1, ninf, el)
    e2 = jnp.max(el2, axis=1, keepdims=True)
    i2 = jnp.min(jnp.where(el2 == e2, lanef, big), axis=1, keepdims=True)
    t = jnp.exp(e2 - e1)
    den = 1.0 + t
    w1 = (1.0 / den) * p_group
    w2 = (t / den) * p_group
    out = jnp.where(lane == 0, i1, jnp.where(lane == 1, i2,
          jnp.where(lane == 2, w1, jnp.where(lane == 3, w2, 0.0))))
    return out


def _slab_pitch(s):
    return s + 8


def _slab_store(ref, row0, val):
    rows, d = val.shape
    s = d // LANES
    p = _slab_pitch(s)
    for c in range(s):
        ref[pl.ds(row0 * p + c, rows, stride=p), :] = val[:, c * LANES:(c + 1) * LANES]


def _slab_load(ref, row0, rows, s, lead=None):
    pieces = []
    p = _slab_pitch(s)
    for c in range(s):
        rs = pl.ds(row0 * p + c, rows, stride=p)
        pieces.append(ref[rs, :] if lead is None else ref[lead, rs, :])
    return jnp.concatenate(pieces, axis=1)


def _norm_route_kernel(x_ref, g_ref, sc_ref, sh_ref, wr_ref, br_ref, h_ref, r_ref):
    x = x_ref[...]
    y = _rms(x, g_ref[...])
    h = y * (1.0 + sc_ref[...]) + sh_ref[...]
    b3 = (wr_ref[0], wr_ref[1], wr_ref[2])
    grp = x.shape[1]
    for g in range(x.shape[0]):
        _slab_store(h_ref, g * grp, h[g])
        logits = _dot_f32(h[g], b3) + br_ref[...]
        r_ref[g] = _route(logits)


def _mod_spec(idx, gb, d):
    return pl.BlockSpec((None, gb, 1, d), lambda i: (idx, i, 0, 0))


def _norm_mod(x3, g, modg, l, sc_i, sh_i):
    ng, grp, d = x3.shape
    gb = _row_tile(ng, 4)
    return pl.pallas_call(
        _norm_mod_kernel,
        grid=(ng // gb,),
        in_specs=[pl.BlockSpec((gb, grp, d), lambda i: (i, 0, 0)),
                  pl.BlockSpec((None, 1, d), lambda i: (l, 0, 0)),
                  _mod_spec(sc_i, gb, d), _mod_spec(sh_i, gb, d)],
        out_specs=pl.BlockSpec((gb, grp, d), lambda i: (i, 0, 0)),
        out_shape=jax.ShapeDtypeStruct(x3.shape, BF16),
        compiler_params=_cparams(1),
    )(x3, g, modg, modg)


def _norm_route(x3, g, modg, l, sc_i, sh_i, wr3, br):
    ng, grp, d = x3.shape
    gb = _row_tile(ng, 4)
    return pl.pallas_call(
        _norm_route_kernel,
        grid=(ng // gb,),
        in_specs=[pl.BlockSpec((gb, grp, d), lambda i: (i, 0, 0)),
                  pl.BlockSpec((None, 1, d), lambda i: (l, 0, 0)),
                  _mod_spec(sc_i, gb, d), _mod_spec(sh_i, gb, d),
                  pl.BlockSpec((None, 3, d, LANES), lambda i: (l, 0, 0, 0)),
                  pl.BlockSpec((None, 1, LANES), lambda i: (l, 0, 0))],
        out_specs=[pl.BlockSpec((gb * grp * _slab_pitch(d // LANES), LANES), lambda i: (i, 0)),
                   pl.BlockSpec((gb, grp, LANES), lambda i: (i, 0, 0))],
        out_shape=[jax.ShapeDtypeStruct((ng * grp * _slab_pitch(d // LANES), LANES), F32),
                   jax.ShapeDtypeStruct((ng, grp, LANES), F32)],
        compiler_params=_cparams(1),
    )(x3, g, modg, modg, wr3, br)


def _final_norm_kernel(x_ref, g_ref, o_ref):
    o_ref[...] = _rms(x_ref[...], g_ref[...])


def _final_norm(x3, g):
    ng, grp, d = x3.shape
    gb = _row_tile(ng, 4)
    return pl.pallas_call(
        _final_norm_kernel,
        grid=(ng // gb,),
        in_specs=[pl.BlockSpec((gb, grp, d), lambda i: (i, 0, 0)),
                  pl.BlockSpec((1, d), lambda i: (0, 0))],
        out_specs=pl.BlockSpec((gb, grp, d), lambda i: (i, 0, 0)),
        out_shape=jax.ShapeDtypeStruct(x3.shape, F32),
        compiler_params=_cparams(1),
    )(x3, g)


def _lanes(c, w):
    if w % LANES == 0:
        return c if w == LANES else jnp.tile(c, (1, w // LANES))
    return c[:, :w]


def _dot_nt(a, b):
    return lax.dot_general(a, b, (((1,), (1,)), ((), ())), preferred_element_type=F32)


def _sb_block(qh, kh, vh, c, u, scale, masked):
    w = kh.shape[0]
    z = _dot_nt(qh, kh) * scale
    sp = jnp.maximum(z, 0.0) + jnp.log1p(jnp.exp(-jnp.abs(z)))
    if masked:
        row = lax.broadcasted_iota(jnp.int32, z.shape, 0)
        col = lax.broadcasted_iota(jnp.int32, z.shape, 1)
        valid = col < row
        sp = jnp.where(valid, sp, 0.0)
    hi = sp.astype(BF16)
    lo = (sp - hi.astype(F32)).astype(BF16)
    cs = jnp.dot(hi, u, preferred_element_type=F32) + jnp.dot(lo, u, preferred_element_type=F32)
    wgt = jnp.exp(z - sp - cs - _lanes(c, w))
    if masked:
        wgt = jnp.where(valid, wgt, 0.0)
    o = jnp.dot(wgt.astype(BF16), vh, preferred_element_type=F32)
    c_new = c + jnp.sum(sp, axis=1, keepdims=True)
    return o, c_new


def _sb_kernel(qb_ref, kb_ref, hb_ref, fl_ref, q_ref, kn_ref, vn_ref, kp_ref, vp_ref,
               ud_ref, up_ref, o_ref, acc, carry, done, *, heads, scale, sub, n_sub):
    s = pl.program_id(0)
    fl = fl_ref[s]
    is_first = (fl & 1) != 0
    is_last = (fl & 2) != 0

    def past_block(ref, h, j):
        if len(ref.shape) == 3:
            return ref[h, pl.ds(j * sub, sub), :].astype(BF16)
        return ref[pl.ds(j * sub * heads + h, sub, stride=heads), :].astype(BF16)

    @pl.when(is_first)
    def _():
        def body(h, _):
            o, c = _sb_block(q_ref[h], kn_ref[h], vn_ref[h], jnp.zeros(carry.shape[1:], F32),
                             ud_ref[...], scale, True)
            acc[h] = o
            carry[h] = c
            done[h] = 0
            return 0
        lax.fori_loop(0, heads, body, 0, unroll=min(heads, HEAD_UNROLL))

    @pl.when(jnp.logical_not(is_first))
    def _():
        def body(h, _):
            @pl.when(done[h] == 0)
            def _():
                qh = q_ref[h]
                c = carry[h]
                o_tot = acc[h]
                for j in reversed(range(n_sub)):
                    o, c = _sb_block(qh, past_block(kp_ref, h, j), past_block(vp_ref, h, j), c,
                                     up_ref[...], scale, False)
                    o_tot = o_tot + o
                acc[h] = o_tot
                carry[h] = c
                done[h] = (jnp.min(c) >= SB_DEAD).astype(jnp.int32)
            return 0
        lax.fori_loop(0, heads, body, 0, unroll=min(heads, HEAD_UNROLL))

    @pl.when(is_last)
    def _():
        for h in range(heads):
            o_ref[:, h * DH_SB:(h + 1) * DH_SB] = acc[h].astype(o_ref.dtype)


def _suffix_matrix(w):
    j = np.arange(w)[:, None]
    s = np.arange(w)[None, :]
    return jnp.asarray((j > s).astype(np.float32), dtype=BF16)


def _sb_attention(q, kn, vn, kp, vp, tables, *, heads, tq, tk, sub, out_rows, out_block_of):
    qb, kb, hb, fl = tables
    n_steps = qb.shape[0]
    hblk = heads

    def qmap(s, qb, kb, hb, fl):
        return (0, qb[s], 0)

    if kp.ndim == 3:
        past_spec = pl.BlockSpec((hblk, tk, DH_SB), lambda s, qb, kb, hb, fl: (0, kb[s], 0))
    else:
        past_spec = pl.BlockSpec((tk * hblk, DH_SB), lambda s, qb, kb, hb, fl: (kb[s], 0))
    new_spec = pl.BlockSpec((hblk, tq, DH_SB), qmap)
    const = lambda s, qb, kb, hb, fl: (0, 0)
    grid_spec = pltpu.PrefetchScalarGridSpec(
        num_scalar_prefetch=4,
        grid=(n_steps,),
        in_specs=[new_spec, new_spec, new_spec, past_spec, past_spec,
                  pl.BlockSpec((tq, tq), const), pl.BlockSpec((sub, sub), const)],
        out_specs=pl.BlockSpec((tq, hblk * DH_SB), out_block_of),
        scratch_shapes=[pltpu.VMEM((hblk, tq, DH_SB), F32),
                        pltpu.VMEM((hblk, tq, LANES), F32),
                        pltpu.SMEM((hblk,), jnp.int32)],
    )
    kern = functools.partial(_sb_kernel, heads=hblk, scale=DH_SB ** -0.5, sub=sub,
                             n_sub=tk // sub)
    return pl.pallas_call(
        kern, grid_spec=grid_spec,
        out_shape=jax.ShapeDtypeStruct((out_rows, H_SB * DH_SB), BF16),
        compiler_params=_cparams(1),
    )(qb, kb, hb, fl, q, kn, vn, kp, vp, _suffix_matrix(tq), _suffix_matrix(sub))


def _mla_block(qc, kc, va, m, acc, mask):
    s = _dot_nt(qc, kc)
    if mask is not None:
        s = jnp.where(mask, s, -jnp.inf)
    m_new = jnp.maximum(m, jnp.max(s, axis=1, keepdims=True))
    alpha = jnp.exp2(m - m_new)
    p = jnp.exp2(s - _lanes(m_new, s.shape[1]))
    acc_new = _lanes(alpha, acc.shape[1]) * acc + jnp.dot(p.astype(BF16), va,
                                                          preferred_element_type=F32)
    return m_new, acc_new


def _mla_kernel(qb_ref, kb_ref, fl_ref, q_ref, kn_ref, vn_ref, kp_ref, vp_ref, o_ref,
                acc, m_sc, *, heads, pos0):
    s = pl.program_id(0)
    fl = fl_ref[s]
    is_first = (fl & 1) != 0
    is_last = (fl & 2) != 0
    tq = q_ref.shape[1]

    @pl.when(is_first)
    def _():
        row = lax.broadcasted_iota(jnp.int32, (tq, tq), 0) + pos0
        col = lax.broadcasted_iota(jnp.int32, (tq, tq), 1) + pos0
        mask = (col // CHUNK) <= (row // CHUNK)

        def body(h, _):
            m0 = jnp.full((tq, LANES), -jnp.inf, F32)
            a0 = jnp.zeros((tq, 2 * V_DIM), F32)
            m, a = _mla_block(q_ref[h], kn_ref[h], vn_ref[h], m0, a0, mask)
            m_sc[h] = m
            acc[h] = a
            return 0
        lax.fori_loop(0, heads, body, 0, unroll=min(heads, MLA_UNROLL))

    @pl.when(jnp.logical_not(is_first))
    def _():
        def body(h, _):
            m, a = _mla_block(q_ref[h], kp_ref[h], vp_ref[h], m_sc[h], acc[h], None)
            m_sc[h] = m
            acc[h] = a
            return 0
        lax.fori_loop(0, heads, body, 0, unroll=min(heads, MLA_UNROLL))

    @pl.when(is_last)
    def _():
        for h in range(heads):
            a = acc[h]
            o_ref[:, h * V_DIM:(h + 1) * V_DIM] = (a[:, :V_DIM] / a[:, V_DIM:]).astype(o_ref.dtype)


def _mla_attention(qc, kcn, vn, kcp, vp, tables, *, tq, tk, pos0, out_rows, out_block_of):
    qb, kb, fl = tables
    n_steps = qb.shape[0]
    h = H_MLA
    wqk = 2 * LANES
    new = pl.BlockSpec((h, tq, wqk), lambda s, qb, kb, fl: (0, qb[s], 0))
    past = pl.BlockSpec((h, tk, wqk), lambda s, qb, kb, fl: (0, kb[s], 0))
    grid_spec = pltpu.PrefetchScalarGridSpec(
        num_scalar_prefetch=3,
        grid=(n_steps,),
        in_specs=[new, new, new, past, past],
        out_specs=pl.BlockSpec((tq, h * V_DIM), out_block_of),
        scratch_shapes=[pltpu.VMEM((h, tq, 2 * V_DIM), F32),
                        pltpu.VMEM((h, tq, LANES), F32)],
    )
    kern = functools.partial(_mla_kernel, heads=h, pos0=pos0)
    return pl.pallas_call(
        kern, grid_spec=grid_spec,
        out_shape=jax.ShapeDtypeStruct((out_rows, h * V_DIM), BF16),
        compiler_params=_cparams(1),
    )(qb, kb, fl, qc, kcn, vn, kcp, vp)


def _causal_tables(nq, q_off, per_head=False, heads=1, n_batch=1, past_tiles=None,
                   past_stride=0):
    qb, kb, hb, fl = [], [], [], []
    if past_tiles is None:
        for i in range(nq):
            n = i + 1
            for j in range(n):
                qb.append(q_off + i)
                kb.append(max(i - j, 1) - 1 if j == 0 else i - j)
                hb.append(0)
                fl.append((1 if j == 0 else 0) | (2 if j == n - 1 else 0))
    else:
        for b in range(n_batch):
            for h in range(heads if per_head else 1):
                n = 1 + past_tiles
                for j in range(n):
                    qb.append(q_off + b)
                    jj = past_tiles - 1 if j == 0 else past_tiles - j
                    kb.append(past_stride + b * past_tiles + jj)
                    hb.append(h)
                    fl.append((1 if j == 0 else 0) | (2 if j == n - 1 else 0))
    arr = lambda v: jnp.asarray(np.asarray(v, dtype=np.int32))
    return arr(qb), arr(kb), arr(hb), arr(fl)


ROW_DMA_UNROLL = 8


def _moe_kernel(te_ref, nv_ref, tok_ref, dst_ref, h_hbm, w_ref, wg_ref, wu_ref, wd_ref, y_hbm,
                xbuf, obuf, wg_s, wu_s, wd_s, sem_in, sem_out, *, s, n_dest):
    t = pl.program_id(0)
    tm = w_ref.shape[0]
    nv = nv_ref[0]
    live = t < nv
    slot = lax.rem(t, 2)

    p = _slab_pitch(s)

    def in_copy(tile, r, sl):
        src0 = pl.multiple_of(tok_ref[tile * tm + r] * p, 8)
        return pltpu.make_async_copy(h_hbm.at[pl.ds(src0, s)],
                                     xbuf.at[sl, pl.ds(pl.multiple_of(r * p, 8), s)],
                                     sem_in.at[sl])

    def gather_start(tile, sl):
        def body(r, _):
            in_copy(tile, r, sl).start()
            return 0
        lax.fori_loop(0, tm, body, 0, unroll=ROW_DMA_UNROLL)

    def gather_wait(tile, sl):
        def body(r, _):
            in_copy(tile, r, sl).wait()
            return 0
        lax.fori_loop(0, tm, body, 0, unroll=ROW_DMA_UNROLL)

    @pl.when(live & (t == 0))
    def _():
        gather_start(0, 0)

    @pl.when(t + 1 < nv)
    def _():
        gather_start(t + 1, 1 - slot)

    prev = te_ref[jnp.maximum(t - 1, 0)]
    fresh = (t == 0) | (te_ref[t] != prev)

    @pl.when(live & fresh)
    def _():
        wg_s[...] = wg_ref[...].astype(BF16)
        wu_s[...] = wu_ref[...].astype(BF16)
        wd_s[...] = wd_ref[...].astype(BF16)

    @pl.when(live)
    def _():
        gather_wait(t, slot)
        x = _slab_load(xbuf, 0, tm, s, lead=slot).astype(BF16)
        a = jnp.dot(x, wg_s[...], preferred_element_type=F32)
        u = jnp.dot(x, wu_s[...], preferred_element_type=F32)
        hid = (a * jax.nn.sigmoid(a)) * u * w_ref[...]
        _slab_store(obuf, 0, jnp.dot(hid.astype(BF16), wd_s[...], preferred_element_type=F32))

        def out_copy(r):
            dst0 = pl.multiple_of(dst_ref[t * tm + r] * p, 8)
            return pltpu.make_async_copy(obuf.at[pl.ds(pl.multiple_of(r * p, 8), s)],
                                         y_hbm.at[pl.ds(dst0, s)], sem_out)

        def start(r, _):
            @pl.when(dst_ref[t * tm + r] < n_dest)
            def _():
                out_copy(r).start()
            return 0
        lax.fori_loop(0, tm, start, 0, unroll=ROW_DMA_UNROLL)

        def wait(r, _):
            @pl.when(dst_ref[t * tm + r] < n_dest)
            def _():
                out_copy(r).wait()
            return 0
        lax.fori_loop(0, tm, wait, 0, unroll=ROW_DMA_UNROLL)


def _moe_experts(h_slab, row_tok, row_dst, row_w, tile_expert, n_valid, w_gate, w_up, w_down, l,
                 n_dest):
    d, f = w_gate.shape[-2:]
    s = d // LANES
    p = _slab_pitch(s)
    tm = MOE_TILE
    n_tiles = row_tok.shape[0] // tm
    wmap = lambda t, te, nv, tok, dst: (l, te[t], 0, 0)
    grid_spec = pltpu.PrefetchScalarGridSpec(
        num_scalar_prefetch=4, grid=(n_tiles,),
        in_specs=[pl.BlockSpec(memory_space=pl.ANY),
                  pl.BlockSpec((tm, 1), lambda t, te, nv, tok, dst: (t, 0)),
                  pl.BlockSpec((None, None, d, f), wmap),
                  pl.BlockSpec((None, None, d, f), wmap),
                  pl.BlockSpec((None, None, f, d), wmap)],
        out_specs=pl.BlockSpec(memory_space=pl.ANY),
        scratch_shapes=[pltpu.VMEM((2, tm * p, LANES), F32), pltpu.VMEM((tm * p, LANES), F32),
                        pltpu.VMEM((d, f), BF16), pltpu.VMEM((d, f), BF16),
                        pltpu.VMEM((f, d), BF16),
                        pltpu.SemaphoreType.DMA((2,)), pltpu.SemaphoreType.DMA(())],
    )
    return pl.pallas_call(
        functools.partial(_moe_kernel, s=s, n_dest=n_dest), grid_spec=grid_spec,
        out_shape=jax.ShapeDtypeStruct((n_dest * p, LANES), F32),
        compiler_params=_cparams(1),
    )(tile_expert, n_valid, row_tok, row_dst, h_slab, row_w, w_gate, w_up, w_down)


def _combine_kernel(x_ref, y0_ref, y1_ref, g_ref, o_ref):
    gb, grp, d = x_ref.shape
    s = d // LANES
    for g in range(gb):
        y = _slab_load(y0_ref, g * grp, grp, s) + _slab_load(y1_ref, g * grp, grp, s)
        o_ref[g] = x_ref[g] + g_ref[g] * y


def _combine(x3, yg, modg, l, g_i):
    ng, grp, d = x3.shape
    s = d // LANES
    gb = _row_tile(ng, 2)
    blk = pl.BlockSpec((gb, grp, d), lambda i: (i, 0, 0))
    yblk = lambda off: pl.BlockSpec((gb * grp * _slab_pitch(s), LANES), lambda i: (i + off, 0))
    return pl.pallas_call(
        _combine_kernel,
        grid=(ng // gb,),
        in_specs=[blk, yblk(0), yblk(ng // gb), _mod_spec(g_i, gb, d)],
        out_specs=blk,
        out_shape=jax.ShapeDtypeStruct(x3.shape, F32),
        compiler_params=_cparams(1),
    )(x3, yg, yg, modg)


def _dispatch(route, n):
    tm = MOE_TILE
    r_cap = (2 * n + N_EXPERTS * (tm - 1) + tm - 1) // tm * tm
    e = route[:, :2].astype(jnp.int32)
    w = route[:, 2:4]
    flat_e = e.T.reshape(-1)
    flat_w = w.T.reshape(-1)
    order = jnp.argsort(flat_e, stable=True).astype(jnp.int32)
    experts = jnp.arange(N_EXPERTS, dtype=jnp.int32)
    counts = jnp.sum((flat_e[:, None] == experts[None, :]).astype(jnp.int32), axis=0)
    padded = (counts + tm - 1) // tm * tm
    ends_p = jnp.cumsum(padded)
    starts_p = ends_p - padded
    starts = jnp.cumsum(counts) - counts
    tile_start = jnp.arange(r_cap // tm, dtype=jnp.int32) * tm
    tile_expert = jnp.minimum(jnp.searchsorted(ends_p, tile_start, side="right"),
                              N_EXPERTS - 1).astype(jnp.int32)
    n_valid = (ends_p[-1:] // tm).astype(jnp.int32)
    row_e = jnp.repeat(tile_expert, tm)
    local = jnp.arange(r_cap, dtype=jnp.int32) - starts_p[row_e]
    valid = local < counts[row_e]
    slot = order[jnp.clip(starts[row_e] + local, 0, 2 * n - 1)]
    row_dst = jnp.where(valid, slot, 2 * n)
    row_tok = jnp.where(valid, jnp.where(slot >= n, slot - n, slot), 0)
    row_w = jnp.where(valid, flat_w[slot], 0.0)
    return row_tok, row_dst, row_w.reshape(r_cap, 1), tile_expert, n_valid


def _rope_tables(pos):
    inv = ROPE_THETA ** (-jnp.arange(0, QK_ROPE, 2, dtype=F32) / QK_ROPE)
    ang = pos.astype(F32)[:, None] * inv[None, :]
    cos, sin = jnp.cos(ang), jnp.sin(ang)
    pad = jnp.zeros((pos.shape[0], LANES - QK_ROPE), F32)
    return (jnp.concatenate([cos, cos, pad], axis=1),
            jnp.concatenate([-sin, sin, pad], axis=1))


def _rope_lanes(acc, c, s):
    return acc * c + pltpu.roll(acc, LANES - QK_ROPE, axis=1) * s


def _swap_halves(w):
    half = w.shape[-1] // 2
    return jnp.concatenate([w[..., half:], w[..., :half]], axis=-1)


def kernel(x_prompt, x_sample, c_prompt, c_sample, cache_sb_k, cache_sb_v, cache_mla_ckv,
           cache_mla_krope, w_ada, b_ada, g_norm_mix, g_norm_ffn, w_in, g_q_lat, g_kv_lat,
           w_uq, w_ukv, w_branch_sb, w_branch_mla, w_out, w_router_group, b_router_group,
           w_router_expert, b_router_expert, w_exp_gate, w_exp_up, w_exp_down, g_final):
    bp, t_p, d = x_prompt.shape
    bs, t_s, _ = x_sample.shape
    depth = w_in.shape[0]
    past = cache_sb_k.shape[2]
    grp = t_s
    n_p, n_s = bp * t_p, bs * t_s
    n = n_p + n_s
    ng = n // grp
    sb_w = H_SB * DH_SB
    q_lora = g_q_lat.shape[1]
    kv_lora = g_kv_lat.shape[1]
    tm = _row_tile(n)
    gpt = tm // grp
    tn = 512
    assert bp == 1 and t_p % ATTN_TILE == 0 and t_p % grp == 0 and past % 512 == 0

    x3 = jnp.concatenate([x_prompt.reshape(n_p // grp, grp, d), x_sample], axis=0)

    n_c = bp + bs
    c_rows = 16
    c_all = jnp.zeros((c_rows, d), F32).at[:n_c].set(jnp.concatenate([c_prompt, c_sample], 0))
    n_modc = N_MOD * d

    def ada_epi(accs, ex, outs):
        outs[0][...] = accs[0] + ex[0][...]

    mods = []
    for l in range(depth):
        mods.append(_matmul(
            [(c_all, w_ada, _wspec(l, d, tn, 0), True)], m=c_rows, n_out=n_modc, tm=c_rows, tn=tn,
            prologue=lambda a, ex: a * jax.nn.sigmoid(a),
            epilogue=ada_epi,
            extras=[b_ada.reshape(depth, 1, n_modc)],
            extra_specs=[pl.BlockSpec((None, 1, tn), lambda j, i, l=l: (l, 0, j))],
            out_shape=[jax.ShapeDtypeStruct((c_rows, n_modc), F32)],
            out_specs=[_spec2(c_rows, tn)])[0])
    mod = jnp.stack(mods)
    modg = jnp.concatenate(
        [jnp.broadcast_to(mod[:, :bp], (depth, n_p // grp, n_modc)), mod[:, bp:n_c]], axis=1)
    modg = modg.reshape(depth, ng, N_MOD, d).transpose(0, 2, 1, 3)
    modg = modg.reshape(depth * N_MOD, ng, 1, d)

    def mod_idx(l, k):
        return l * N_MOD + k

    def mspec_rows(l, k):
        return pl.BlockSpec((None, gpt, 1, tn), lambda j, i: (mod_idx(l, k), i, 0, j))

    pos = jnp.concatenate([jnp.arange(t_p, dtype=jnp.int32),
                           jnp.tile(past + jnp.arange(t_s, dtype=jnp.int32), bs)])
    rope_c, rope_s = _rope_tables(pos)
    tq = ATTN_TILE
    nq_p = t_p // tq
    tab_p = _causal_tables(nq_p, 0)
    tk_s = 512
    pt = past // tk_s
    tab_sb_s = lambda l: _causal_tables(0, n_p // t_s, n_batch=bs, past_tiles=pt,
                                        past_stride=l * bs * pt)
    tab_mla_s = _causal_tables(0, n_p // t_s, n_batch=bs, past_tiles=pt)

    off_q, off_k, off_v = 0, sb_w, 2 * sb_w
    off_cq = 3 * sb_w
    off_ckv = off_cq + q_lora
    off_kr = off_ckv + kv_lora
    off_g = off_kr + QK_ROPE
    w_kr = w_in[:, :, off_kr:off_kr + QK_ROPE]
    w_kr_aug = jnp.concatenate([w_kr, _swap_halves(w_kr)], axis=-1).astype(BF16)
    w_gates = w_in[:, :, off_g:].astype(BF16)
    uq = w_uq.reshape(depth, q_lora, H_MLA, QK_NOPE + QK_ROPE)
    uq_r = uq[..., QK_NOPE:]
    w_uq_cat = jnp.concatenate([uq[..., :QK_NOPE], uq_r, _swap_halves(uq_r)], axis=-1)
    w_uq_cat = w_uq_cat.reshape(depth, q_lora, H_MLA * 2 * LANES)
    w_router = jnp.concatenate(
        [w_router_expert, w_router_group,
         jnp.zeros((depth, d, LANES - N_EXPERTS - N_GROUPS), F32)], axis=-1)
    r_hi = w_router.astype(BF16)
    r_res = w_router - r_hi.astype(F32)
    r_mid = r_res.astype(BF16)
    r_lo = (r_res - r_mid.astype(F32)).astype(BF16)
    w_router3 = jnp.stack([r_hi, r_mid, r_lo], axis=1)
    b_router = jnp.concatenate(
        [b_router_expert, b_router_group,
         jnp.zeros((depth, LANES - N_EXPERTS - N_GROUPS), F32)], axis=-1).reshape(depth, 1, LANES)

    hm = lambda rows: jax.ShapeDtypeStruct((H_SB, rows, LANES), BF16)
    new_k, new_v, new_c, new_r = [], [], [], []

    for l in range(depth):
        h = _norm_mod(x3, g_norm_mix.reshape(depth, 1, d), modg, l, mod_idx(l, 1), mod_idx(l, 0))
        h = h.reshape(n, d)

        def plain_hm(accs, ex, outs):
            _store_heads(outs[0], accs[0])

        def f32_and_hm(accs, ex, outs):
            outs[0][...] = accs[0]
            _store_heads(outs[1], accs[0])

        sb_q = _matmul([(h, w_in, _wspec(l, d, tn, off_q), True)], m=n, n_out=sb_w, tm=tm, tn=tn,
                       epilogue=plain_hm, out_shape=[hm(n)], out_specs=[_hm_spec(tm, tn)])[0]
        k_f32, sb_k = _matmul([(h, w_in, _wspec(l, d, tn, off_k), True)], m=n, n_out=sb_w, tm=tm,
                              tn=tn, epilogue=f32_and_hm,
                              out_shape=[jax.ShapeDtypeStruct((n, sb_w), F32), hm(n)],
                              out_specs=[_spec2(tm, tn), _hm_spec(tm, tn)])
        v_f32, sb_v = _matmul([(h, w_in, _wspec(l, d, tn, off_v), True)], m=n, n_out=sb_w, tm=tm,
                              tn=tn, epilogue=f32_and_hm,
                              out_shape=[jax.ShapeDtypeStruct((n, sb_w), F32), hm(n)],
                              out_specs=[_spec2(tm, tn), _hm_spec(tm, tn)])

        def plain_f32(accs, ex, outs):
            outs[0][...] = accs[0]

        c_q = _matmul([(h, w_in, _wspec(l, d, tn, off_cq), True)], m=n, n_out=q_lora, tm=tm, tn=tn,
                      epilogue=plain_f32, out_shape=[jax.ShapeDtypeStruct((n, q_lora), F32)],
                      out_specs=[_spec2(tm, tn)])[0]

        def ckv_epi(accs, ex, outs):
            y = _rms(accs[0], ex[0][...])
            outs[0][...] = y
            outs[1][...] = y.astype(BF16)

        c_kv, c_kv_b = _matmul(
            [(h, w_in, _wspec(l, d, kv_lora, off_ckv), True)], m=n, n_out=kv_lora, tm=tm, tn=kv_lora,
            epilogue=ckv_epi, extras=[g_kv_lat.reshape(depth, 1, kv_lora)],
            extra_specs=[pl.BlockSpec((None, 1, kv_lora), lambda j, i: (l, 0, 0))],
            out_shape=[jax.ShapeDtypeStruct((n, kv_lora), F32),
                       jax.ShapeDtypeStruct((n, kv_lora), BF16)],
            out_specs=[_spec2(tm, kv_lora), _spec2(tm, kv_lora)])

        def kr_epi(accs, ex, outs):
            r = _rope_lanes(accs[0], ex[0][...], ex[1][...])
            outs[0][...] = r
            outs[1][...] = r.astype(BF16)

        rope_specs = [pl.BlockSpec((tm, LANES), lambda j, i: (i, 0))] * 2
        k_r, k_r_b = _matmul(
            [(h, w_kr_aug, pl.BlockSpec((None, d, LANES), lambda j, i: (l, 0, 0)), False)],
            m=n, n_out=LANES, tm=tm, tn=LANES, epilogue=kr_epi,
            extras=[rope_c, rope_s], extra_specs=rope_specs,
            out_shape=[jax.ShapeDtypeStruct((n, LANES), F32), jax.ShapeDtypeStruct((n, LANES), BF16)],
            out_specs=[_spec2(tm, LANES), _spec2(tm, LANES)])

        def gate_epi(accs, ex, outs):
            outs[0][...] = jax.nn.sigmoid(accs[0])

        gates = _matmul(
            [(h, w_gates, pl.BlockSpec((None, d, tn), lambda j, i: (l, 0, j)), False)],
            m=n, n_out=2 * d, tm=tm, tn=tn, epilogue=gate_epi,
            out_shape=[jax.ShapeDtypeStruct((n, 2 * d), F32)], out_specs=[_spec2(tm, tn)])[0]

        def cq_prologue(a, ex):
            return _rms(a, ex[0][...])

        gq_spec = pl.BlockSpec((None, 1, q_lora), lambda j, i: (l, 0, 0))
        gq = g_q_lat.reshape(depth, 1, q_lora)
        wqk = 2 * LANES
        hm_qk = lambda rows: jax.ShapeDtypeStruct((H_MLA, rows, wqk), BF16)

        q_scale = (QK_NOPE + QK_ROPE) ** -0.5 * float(np.log2(np.e))

        def qcat_epi(accs, ex, outs):
            c, s = ex[1][...], ex[2][...]
            for hh in range(tn // wqk):
                blk = accs[0][:, hh * wqk:(hh + 1) * wqk]
                outs[0][hh, :, :LANES] = (blk[:, :LANES] * q_scale).astype(BF16)
                outs[0][hh, :, LANES:] = (_rope_lanes(blk[:, LANES:], c, s) * q_scale).astype(BF16)

        q_cat = _matmul(
            [(c_q, w_uq_cat, pl.BlockSpec((None, q_lora, tn), lambda j, i: (l, 0, j)), True)],
            m=n, n_out=H_MLA * wqk, tm=tm, tn=tn, prologue=cq_prologue, epilogue=qcat_epi,
            extras=[gq, rope_c, rope_s], extra_specs=[gq_spec] + rope_specs,
            out_shape=[hm_qk(n)],
            out_specs=[pl.BlockSpec((tn // wqk, tm, wqk), lambda j, i: (j, i, 0))])[0]

        tn_kv = min(2048, H_MLA * wqk)
        hpt = tn_kv // wqk

        def kv_epi(accs, ex, outs):
            kr = ex[0][...]
            ones = jnp.ones(kr.shape, BF16)
            for hh in range(hpt):
                outs[0][hh, :, :LANES] = accs[0][:, hh * wqk:hh * wqk + LANES].astype(BF16)
                outs[0][hh, :, LANES:] = kr
                outs[1][hh, :, :LANES] = accs[0][:, hh * wqk + LANES:(hh + 1) * wqk].astype(BF16)
                outs[1][hh, :, LANES:] = ones

        def up_kv(a, kr, rows, tmr):
            hspec = pl.BlockSpec((hpt, tmr, wqk), lambda j, i: (j, i, 0))
            return _matmul(
                [(a, w_ukv, pl.BlockSpec((None, kv_lora, tn_kv), lambda j, i: (l, 0, j)), True)],
                m=rows, n_out=H_MLA * wqk, tm=tmr, tn=tn_kv, epilogue=kv_epi, extras=[kr],
                extra_specs=[pl.BlockSpec((tmr, LANES), lambda j, i: (i, 0))],
                out_shape=[hm_qk(rows), hm_qk(rows)], out_specs=[hspec, hspec])

        kc_new, v_new = up_kv(c_kv_b, k_r_b, n, tm)
        rows_past = bs * past
        kr_past = jnp.pad(cache_mla_krope[l].reshape(rows_past, QK_ROPE),
                          ((0, 0), (0, LANES - QK_ROPE))).astype(BF16)
        kc_past, v_past = up_kv(cache_mla_ckv[l].reshape(rows_past, kv_lora), kr_past, rows_past,
                                512)

        o_sb_p = _sb_attention(sb_q, sb_k, sb_v, sb_k, sb_v, tab_p, heads=H_SB, tq=tq, tk=tq,
                               sub=tq, out_rows=n_p,
                               out_block_of=lambda s, qb, kb, hb, fl: (qb[s], 0))
        cache_k2 = cache_sb_k.reshape(depth * bs * past * H_SB, DH_SB)
        cache_v2 = cache_sb_v.reshape(depth * bs * past * H_SB, DH_SB)
        q_off_s = n_p // t_s
        o_sb_s = _sb_attention(sb_q, sb_k, sb_v, cache_k2, cache_v2, tab_sb_s(l), heads=H_SB,
                               tq=t_s, tk=tk_s, sub=ATTN_TILE, out_rows=n_s,
                               out_block_of=lambda s, qb, kb, hb, fl: (qb[s] - q_off_s, 0))
        o_sb = jnp.concatenate([o_sb_p, o_sb_s], axis=0)

        tp3 = (tab_p[0], tab_p[1], tab_p[3])
        o_mla_p = _mla_attention(q_cat, kc_new, v_new, kc_new, v_new, tp3,
                                 tq=tq, tk=tq, pos0=0, out_rows=n_p,
                                 out_block_of=lambda s, qb, kb, fl: (qb[s], 0))
        ts3 = (tab_mla_s[0], tab_mla_s[1], tab_mla_s[3])
        o_mla_s = _mla_attention(q_cat, kc_new, v_new, kc_past, v_past, ts3,
                                 tq=t_s, tk=tk_s, pos0=past, out_rows=n_s,
                                 out_block_of=lambda s, qb, kb, fl: (qb[s] - q_off_s, 0))
        o_mla = jnp.concatenate([o_mla_p, o_mla_s], axis=0)

        def merge_epi(accs, ex, outs):
            outs[0][...] = (ex[0][...] * accs[0] + ex[1][...] * accs[1]).astype(BF16)

        merged = _matmul(
            [(o_sb, w_branch_sb, _wspec(l, sb_w, tn, 0), True),
             (o_mla, w_branch_mla, _wspec(l, H_MLA * V_DIM, tn, 0), True)],
            m=n, n_out=d, tm=tm, tn=tn, epilogue=merge_epi,
            extras=[gates, gates],
            extra_specs=[pl.BlockSpec((tm, tn), lambda j, i: (i, j)),
                         pl.BlockSpec((tm, tn), lambda j, i: (i, j + d // tn))],
            out_shape=[jax.ShapeDtypeStruct((n, d), BF16)], out_specs=[_spec2(tm, tn)])[0]

        def resid_epi(accs, ex, outs):
            for g in range(gpt):
                outs[0][g] = ex[0][g] + ex[1][g] * accs[0][g * grp:(g + 1) * grp, :]

        x_spec = pl.BlockSpec((gpt, grp, tn), lambda j, i: (i, 0, j))
        x3 = _matmul(
            [(merged, w_out, _wspec(l, d, tn, 0), True)], m=n, n_out=d, tm=tm, tn=tn,
            epilogue=resid_epi, extras=[x3, modg], extra_specs=[x_spec, mspec_rows(l, 2)],
            out_shape=[jax.ShapeDtypeStruct((ng, grp, d), F32)], out_specs=[x_spec])[0]

        h2, route = _norm_route(x3, g_norm_ffn.reshape(depth, 1, d), modg, l, mod_idx(l, 4),
                                mod_idx(l, 3), w_router3, b_router)
        row_tok, row_dst, row_w, tile_expert, n_valid = _dispatch(route.reshape(n, LANES), n)
        yg = _moe_experts(h2, row_tok, row_dst, row_w, tile_expert, n_valid,
                          w_exp_gate, w_exp_up, w_exp_down, l, 2 * n)
        x3 = _combine(x3, yg, modg, l, mod_idx(l, 5))

        new_k.append(k_f32)
        new_v.append(v_f32)
        new_c.append(c_kv)
        new_r.append(k_r[:, :QK_ROPE])

    y = _final_norm(x3, g_final.reshape(1, d)).reshape(n, d)

    def split(parts, tail):
        a = jnp.stack(parts)
        return (a[:, :n_p].reshape((depth, bp, t_p) + tail),
                a[:, n_p:].reshape((depth, bs, t_s) + tail))

    pk, sk = split(new_k, (H_SB, DH_SB))
    pv, sv = split(new_v, (H_SB, DH_SB))
    pc, sc = split(new_c, (kv_lora,))
    pr, sr = split(new_r, (QK_ROPE,))
    return (y[:n_p].reshape(bp, t_p, d), y[n_p:].reshape(bs, t_s, d),
            pk, pv, pc, pr, sk, sv, sc, sr)
```

```python
import functools

import numpy as np
import jax
import jax.numpy as jnp
from jax import lax
from jax.experimental import pallas as pl
from jax.experimental.pallas import tpu as pltpu

F32 = jnp.float32
BF16 = jnp.bfloat16

CHUNK = 64
H_SB = 16
DH_SB = 128
H_MLA = 16
QK_NOPE = 128
QK_ROPE = 64
V_DIM = 128
ROPE_THETA = 10000.0
N_GROUPS = 4
EXPERTS_PER_GROUP = 8
N_EXPERTS = N_GROUPS * EXPERTS_PER_GROUP
N_MOD = 6
EPS = 1e-6

LANES = 128
ATTN_TILE = 256
MOE_TILE = 256
HEAD_UNROLL = 4
MLA_UNROLL = 16
SB_DEAD = 104.0
VMEM_LIMIT = 56 * 1024 * 1024


def _cparams(n_axes, vmem=VMEM_LIMIT):
    return pltpu.CompilerParams(dimension_semantics=("arbitrary",) * n_axes,
                                vmem_limit_bytes=vmem)


def _row_tile(n, cap=512):
    t = cap
    while n % t:
        t //= 2
    return t


def _token_tile(n, grp, cap=1152):
    return max(t for t in range(grp, cap + 1, grp) if n % t == 0)


def _matmul(pairs, *, m, n_out, tm, tn, epilogue, out_shape, out_specs,
            extras=(), extra_specs=(), prologue=None):
    n_pairs = len(pairs)
    n_ex = len(extras)
    n_outs = len(out_shape)
    cast = [p[3] for p in pairs]

    def kern(*refs):
        a_refs = refs[0:2 * n_pairs:2]
        b_refs = refs[1:2 * n_pairs:2]
        ex = refs[2 * n_pairs:2 * n_pairs + n_ex]
        outs = refs[2 * n_pairs + n_ex:2 * n_pairs + n_ex + n_outs]
        scr = refs[2 * n_pairs + n_ex + n_outs:]
        i = pl.program_id(1)
        accs = []
        si = 0
        for p in range(n_pairs):
            if cast[p]:
                bsc = scr[si]
                si += 1

                @pl.when(i == 0)
                def _(bsc=bsc, b_ref=b_refs[p]):
                    bsc[...] = b_ref[...].astype(BF16)

                bv = bsc[...]
            else:
                bv = b_refs[p][...]
            a = a_refs[p][...]
            if prologue is not None:
                a = prologue(a, ex)
            accs.append(jnp.dot(a.astype(BF16), bv, preferred_element_type=F32))
        epilogue(accs, ex, outs)

    in_specs, args, scratch = [], [], []
    for (a, b, b_spec, cb) in pairs:
        k = a.shape[1]
        in_specs += [pl.BlockSpec((tm, k), lambda j, i: (i, 0)), b_spec]
        args += [a, b]
        if cb:
            scratch.append(pltpu.VMEM((k, tn), BF16))
    in_specs += list(extra_specs)
    args += list(extras)
    return pl.pallas_call(
        kern,
        grid=(n_out // tn, m // tm),
        in_specs=in_specs,
        out_specs=out_specs,
        out_shape=out_shape,
        scratch_shapes=scratch,
        compiler_params=_cparams(2),
    )(*args)


def _wspec(l, k, tn, col_off):
    cb = col_off // tn
    assert cb * tn == col_off
    return pl.BlockSpec((None, k, tn), lambda j, i: (l, 0, cb + j))


def _spec2(tm, tn):
    return pl.BlockSpec((tm, tn), lambda j, i: (i, j))


def _hm_spec(tm, tn):
    return pl.BlockSpec((tn // LANES, tm, LANES), lambda j, i: (j, i, 0))


def _store_heads(o_ref, val):
    for c in range(val.shape[1] // LANES):
        o_ref[c] = val[:, c * LANES:(c + 1) * LANES].astype(o_ref.dtype)


def _rms(x, g):
    return x * lax.rsqrt(jnp.mean(x * x, axis=-1, keepdims=True) + EPS) * g


def _norm_mod_kernel(x_ref, g_ref, sc_ref, sh_ref, o_ref):
    x = x_ref[...]
    y = _rms(x, g_ref[...])
    o_ref[...] = (y * (1.0 + sc_ref[...]) + sh_ref[...]).astype(o_ref.dtype)


def _split3(x):
    hi = x.astype(BF16)
    r = x - hi.astype(F32)
    mid = r.astype(BF16)
    lo = (r - mid.astype(F32)).astype(BF16)
    return hi, mid, lo


def _dot_f32(a, b3):
    a_hi, a_mid, a_lo = _split3(a)
    b_hi, b_mid, b_lo = b3
    d = functools.partial(jnp.dot, preferred_element_type=F32)
    small = d(a_hi, b_lo) + d(a_lo, b_hi) + d(a_mid, b_mid)
    return (d(a_hi, b_hi) + (d(a_hi, b_mid) + d(a_mid, b_hi))) + small


def _route(logits):
    lane = lax.broadcasted_iota(jnp.int32, logits.shape, 1)
    lanef = lane.astype(F32)
    big = jnp.float32(1e9)
    ninf = jnp.float32(-jnp.inf)
    is_g = (lane >= N_EXPERTS) & (lane < N_EXPERTS + N_GROUPS)
    gl = jnp.where(is_g, logits, ninf)
    gmax = jnp.max(gl, axis=1, keepdims=True)
    g_idx = jnp.min(jnp.where(gl == gmax, lanef - N_EXPERTS, big), axis=1, keepdims=True)
    p_group = 1.0 / jnp.sum(jnp.where(is_g, jnp.exp(gl - gmax), 0.0), axis=1, keepdims=True)
    grp = jnp.floor(lanef * (1.0 / EXPERTS_PER_GROUP))
    in_g = (lane < N_EXPERTS) & (grp == g_idx)
    el = jnp.where(in_g, logits, ninf)
    e1 = jnp.max(el, axis=1, keepdims=True)
    i1 = jnp.min(jnp.where(el == e1, lanef, big), axis=1, keepdims=True)
    el2 = jnp.where(lanef == i1, ninf, el)
    e2 = jnp.max(el2, axis=1, keepdims=True)
    i2 = jnp.min(jnp.where(el2 == e2, lanef, big), axis=1, keepdims=True)
    t = jnp.exp(e2 - e1)
    den = 1.0 + t
    w1 = (1.0 / den) * p_group
    w2 = (t / den) * p_group
    out = jnp.where(lane == 0, i1, jnp.where(lane == 1, i2,
          jnp.where(lane == 2, w1, jnp.where(lane == 3, w2, 0.0))))
    return out


def _slab_pitch(s):
    return s + 8


def _slab_store(ref, row0, val):
    rows, d = val.shape
    s = d // LANES
    p = _slab_pitch(s)
    for c in range(s):
        ref[pl.ds(row0 * p + c, rows, stride=p), :] = val[:, c * LANES:(c + 1) * LANES]


def _slab_load(ref, row0, rows, s, lead=None):
    pieces = []
    p = _slab_pitch(s)
    for c in range(s):
        rs = pl.ds(row0 * p + c, rows, stride=p)
        pieces.append(ref[rs, :] if lead is None else ref[lead, rs, :])
    return jnp.concatenate(pieces, axis=1)


def _norm_route_kernel(x_ref, g_ref, sc_ref, sh_ref, wr_ref, br_ref, h_ref, r_ref):
    x = x_ref[...]
    y = _rms(x, g_ref[...])
    h = y * (1.0 + sc_ref[...]) + sh_ref[...]
    b3 = (wr_ref[0], wr_ref[1], wr_ref[2])
    grp = x.shape[1]
    for g in range(x.shape[0]):
        _slab_store(h_ref, g * grp, h[g])
        logits = _dot_f32(h[g], b3) + br_ref[...]
        r_ref[g] = _route(logits)


def _mod_spec(idx, gb, d):
    return pl.BlockSpec((None, gb, 1, d), lambda i: (idx, i, 0, 0))


def _norm_mod(x3, g, modg, l, sc_i, sh_i):
    ng, grp, d = x3.shape
    gb = _row_tile(ng, 4)
    return pl.pallas_call(
        _norm_mod_kernel,
        grid=(ng // gb,),
        in_specs=[pl.BlockSpec((gb, grp, d), lambda i: (i, 0, 0)),
                  pl.BlockSpec((None, 1, d), lambda i: (l, 0, 0)),
                  _mod_spec(sc_i, gb, d), _mod_spec(sh_i, gb, d)],
        out_specs=pl.BlockSpec((gb, grp, d), lambda i: (i, 0, 0)),
        out_shape=jax.ShapeDtypeStruct(x3.shape, BF16),
        compiler_params=_cparams(1),
    )(x3, g, modg, modg)


def _norm_route(x3, g, modg, l, sc_i, sh_i, wr3, br):
    ng, grp, d = x3.shape
    gb = _row_tile(ng, 4)
    return pl.pallas_call(
        _norm_route_kernel,
        grid=(ng // gb,),
        in_specs=[pl.BlockSpec((gb, grp, d), lambda i: (i, 0, 0)),
                  pl.BlockSpec((None, 1, d), lambda i: (l, 0, 0)),
                  _mod_spec(sc_i, gb, d), _mod_spec(sh_i, gb, d),
                  pl.BlockSpec((None, 3, d, LANES), lambda i: (l, 0, 0, 0)),
                  pl.BlockSpec((None, 1, LANES), lambda i: (l, 0, 0))],
        out_specs=[pl.BlockSpec((gb * grp * _slab_pitch(d // LANES), LANES), lambda i: (i, 0)),
                   pl.BlockSpec((gb, grp, LANES), lambda i: (i, 0, 0))],
        out_shape=[jax.ShapeDtypeStruct((ng * grp * _slab_pitch(d // LANES), LANES), F32),
                   jax.ShapeDtypeStruct((ng, grp, LANES), F32)],
        compiler_params=_cparams(1),
    )(x3, g, modg, modg, wr3, br)


def _final_norm_kernel(x_ref, g_ref, o_ref):
    o_ref[...] = _rms(x_ref[...], g_ref[...])


def _final_norm(x3, g):
    ng, grp, d = x3.shape
    gb = _row_tile(ng, 4)
    return pl.pallas_call(
        _final_norm_kernel,
        grid=(ng // gb,),
        in_specs=[pl.BlockSpec((gb, grp, d), lambda i: (i, 0, 0)),
                  pl.BlockSpec((1, d), lambda i: (0, 0))],
        out_specs=pl.BlockSpec((gb, grp, d), lambda i: (i, 0, 0)),
        out_shape=jax.ShapeDtypeStruct(x3.shape, F32),
        compiler_params=_cparams(1),
    )(x3, g)


def _lanes(c, w):
    if w % LANES == 0:
        return c if w == LANES else jnp.tile(c, (1, w // LANES))
    return c[:, :w]


def _dot_nt(a, b):
    return lax.dot_general(a, b, (((1,), (1,)), ((), ())), preferred_element_type=F32)


def _sb_block(qh, kh, vh, c, u, scale, masked):
    w = kh.shape[0]
    z = _dot_nt(qh, kh) * scale
    sp = jnp.maximum(z, 0.0) + jnp.log1p(jnp.exp(-jnp.abs(z)))
    if masked:
        row = lax.broadcasted_iota(jnp.int32, z.shape, 0)
        col = lax.broadcasted_iota(jnp.int32, z.shape, 1)
        valid = col < row
        sp = jnp.where(valid, sp, 0.0)
    hi = sp.astype(BF16)
    lo = (sp - hi.astype(F32)).astype(BF16)
    cs = jnp.dot(hi, u, preferred_element_type=F32) + jnp.dot(lo, u, preferred_element_type=F32)
    wgt = jnp.exp(z - sp - cs - _lanes(c, w))
    if masked:
        wgt = jnp.where(valid, wgt, 0.0)
    o = jnp.dot(wgt.astype(BF16), vh, preferred_element_type=F32)
    c_new = c + jnp.sum(sp, axis=1, keepdims=True)
    return o, c_new


def _sb_kernel(qb_ref, kb_ref, hb_ref, fl_ref, q_ref, kn_ref, vn_ref, kp_ref, vp_ref,
               ud_ref, up_ref, o_ref, acc, carry, done, *, heads, scale, sub, n_sub):
    s = pl.program_id(0)
    fl = fl_ref[s]
    is_first = (fl & 1) != 0
    is_last = (fl & 2) != 0

    def past_block(ref, h, j):
        if len(ref.shape) == 3:
            return ref[h, pl.ds(j * sub, sub), :].astype(BF16)
        return ref[pl.ds(j * sub * heads + h, sub, stride=heads), :].astype(BF16)

    @pl.when(is_first)
    def _():
        def body(h, _):
            o, c = _sb_block(q_ref[h], kn_ref[h], vn_ref[h], jnp.zeros(carry.shape[1:], F32),
                             ud_ref[...], scale, True)
            acc[h] = o
            carry[h] = c
            done[h] = 0
            return 0
        lax.fori_loop(0, heads, body, 0, unroll=min(heads, HEAD_UNROLL))

    @pl.when(jnp.logical_not(is_first))
    def _():
        def body(h, _):
            @pl.when(done[h] == 0)
            def _():
                qh = q_ref[h]
                c = carry[h]
                o_tot = acc[h]
                for j in reversed(range(n_sub)):
                    o, c = _sb_block(qh, past_block(kp_ref, h, j), past_block(vp_ref, h, j), c,
                                     up_ref[...], scale, False)
                    o_tot = o_tot + o
                acc[h] = o_tot
                carry[h] = c
                done[h] = (jnp.min(c) >= SB_DEAD).astype(jnp.int32)
            return 0
        lax.fori_loop(0, heads, body, 0, unroll=min(heads, HEAD_UNROLL))

    @pl.when(is_last)
    def _():
        for h in range(heads):
            o_ref[:, h * DH_SB:(h + 1) * DH_SB] = acc[h].astype(o_ref.dtype)


def _suffix_matrix(w):
    j = np.arange(w)[:, None]
    s = np.arange(w)[None, :]
    return jnp.asarray((j > s).astype(np.float32), dtype=BF16)


def _sb_self_kernel(q_ref, kn_ref, vn_ref, k_hbm, v_hbm, ud_ref, up_ref, o_ref,
                    kbuf, vbuf, acc, carry, done, sem, *, heads, scale, tk):
    i = pl.program_id(0)

    def fetch(j, slot):
        rows = pl.ds(pl.multiple_of(j * tk, tk), tk)
        return (pltpu.make_async_copy(k_hbm.at[:, rows, :], kbuf.at[slot], sem.at[0, slot]),
                pltpu.make_async_copy(v_hbm.at[:, rows, :], vbuf.at[slot], sem.at[1, slot]))

    @pl.when(i > 0)
    def _():
        for c in fetch(i - 1, 0):
            c.start()

    def diag(h, _):
        o, c = _sb_block(q_ref[h], kn_ref[h], vn_ref[h], jnp.zeros(carry.shape[1:], F32),
                         ud_ref[...], scale, True)
        acc[h] = o
        carry[h] = c
        done[h] = 0
        return 0
    lax.fori_loop(0, heads, diag, 0, unroll=min(heads, HEAD_UNROLL))

    def cond(state):
        j, alive = state
        return (j >= 0) & (alive > 0)

    def body(state):
        j, _ = state
        slot = lax.rem(i - 1 - j, 2)
        for c in fetch(j, slot):
            c.wait()

        @pl.when(j > 0)
        def _():
            for c in fetch(j - 1, 1 - slot):
                c.start()

        def head(h, alive):
            @pl.when(done[h] == 0)
            def _():
                o, c = _sb_block(q_ref[h], kbuf[slot, h], vbuf[slot, h], carry[h], up_ref[...],
                                 scale, False)
                acc[h] = acc[h] + o
                carry[h] = c
                done[h] = (jnp.min(c) >= SB_DEAD).astype(jnp.int32)
            return alive + 1 - done[h]
        return j - 1, lax.fori_loop(0, heads, head, 0)

    j_end, _ = lax.while_loop(cond, body, (i - 1, jnp.int32(heads)))

    @pl.when(j_end >= 0)
    def _():
        for c in fetch(j_end, lax.rem(i - 1 - j_end, 2)):
            c.wait()

    for h in range(heads):
        o_ref[:, h * DH_SB:(h + 1) * DH_SB] = acc[h].astype(o_ref.dtype)


def _sb_self_attention(q, k, v, *, tq, n_rows):
    heads = q.shape[0]
    blk = pl.BlockSpec((heads, tq, DH_SB), lambda i: (0, i, 0))
    umat = pl.BlockSpec((tq, tq), lambda i: (0, 0))
    anyspec = pl.BlockSpec(memory_space=pl.ANY)
    kern = functools.partial(_sb_self_kernel, heads=heads, scale=DH_SB ** -0.5, tk=tq)
    return pl.pallas_call(
        kern, grid=(n_rows // tq,),
        in_specs=[blk, blk, blk, anyspec, anyspec, umat, umat],
        out_specs=pl.BlockSpec((tq, heads * DH_SB), lambda i: (i, 0)),
        out_shape=jax.ShapeDtypeStruct((n_rows, heads * DH_SB), BF16),
        scratch_shapes=[pltpu.VMEM((2, heads, tq, DH_SB), BF16),
                        pltpu.VMEM((2, heads, tq, DH_SB), BF16),
                        pltpu.VMEM((heads, tq, DH_SB), F32),
                        pltpu.VMEM((heads, tq, LANES), F32),
                        pltpu.SMEM((heads,), jnp.int32),
                        pltpu.SemaphoreType.DMA((2, 2))],
        compiler_params=_cparams(1),
    )(q, k, v, k, v, _suffix_matrix(tq), _suffix_matrix(tq))


def _sb_attention(q, kn, vn, kp, vp, tables, *, heads, tq, tk, sub, out_rows, out_block_of):
    qb, kb, hb, fl = tables
    n_steps = qb.shape[0]
    hblk = heads

    def qmap(s, qb, kb, hb, fl):
        return (0, qb[s], 0)

    if kp.ndim == 3:
        past_spec = pl.BlockSpec((hblk, tk, DH_SB), lambda s, qb, kb, hb, fl: (0, kb[s], 0))
    else:
        past_spec = pl.BlockSpec((tk * hblk, DH_SB), lambda s, qb, kb, hb, fl: (kb[s], 0))
    new_spec = pl.BlockSpec((hblk, tq, DH_SB), qmap)
    const = lambda s, qb, kb, hb, fl: (0, 0)
    grid_spec = pltpu.PrefetchScalarGridSpec(
        num_scalar_prefetch=4,
        grid=(n_steps,),
        in_specs=[new_spec, new_spec, new_spec, past_spec, past_spec,
                  pl.BlockSpec((tq, tq), const), pl.BlockSpec((sub, sub), const)],
        out_specs=pl.BlockSpec((tq, hblk * DH_SB), out_block_of),
        scratch_shapes=[pltpu.VMEM((hblk, tq, DH_SB), F32),
                        pltpu.VMEM((hblk, tq, LANES), F32),
                        pltpu.SMEM((hblk,), jnp.int32)],
    )
    kern = functools.partial(_sb_kernel, heads=hblk, scale=DH_SB ** -0.5, sub=sub,
                             n_sub=tk // sub)
    return pl.pallas_call(
        kern, grid_spec=grid_spec,
        out_shape=jax.ShapeDtypeStruct((out_rows, H_SB * DH_SB), BF16),
        compiler_params=_cparams(1),
    )(qb, kb, hb, fl, q, kn, vn, kp, vp, _suffix_matrix(tq), _suffix_matrix(sub))


def _mla_block(qc, kc, va, m, acc, mask):
    s = _dot_nt(qc, kc)
    if mask is not None:
        s = jnp.where(mask, s, -jnp.inf)
    m_new = jnp.maximum(m, jnp.max(s, axis=1, keepdims=True))
    alpha = jnp.exp2(m - m_new)
    p = jnp.exp2(s - _lanes(m_new, s.shape[1]))
    acc_new = _lanes(alpha, acc.shape[1]) * acc + jnp.dot(p.astype(BF16), va,
                                                          preferred_element_type=F32)
    return m_new, acc_new


def _mla_kernel(qb_ref, kb_ref, fl_ref, q_ref, kn_ref, vn_ref, kp_ref, vp_ref, o_ref,
                acc, m_sc, *, heads, pos0):
    s = pl.program_id(0)
    fl = fl_ref[s]
    is_first = (fl & 1) != 0
    is_last = (fl & 2) != 0
    tq = q_ref.shape[1]

    @pl.when(is_first)
    def _():
        row = lax.broadcasted_iota(jnp.int32, (tq, tq), 0) + pos0
        col = lax.broadcasted_iota(jnp.int32, (tq, tq), 1) + pos0
        mask = (col // CHUNK) <= (row // CHUNK)

        def body(h, _):
            m0 = jnp.full((tq, LANES), -jnp.inf, F32)
            a0 = jnp.zeros((tq, 2 * V_DIM), F32)
            m, a = _mla_block(q_ref[h], kn_ref[h], vn_ref[h], m0, a0, mask)
            m_sc[h] = m
            acc[h] = a
            return 0
        lax.fori_loop(0, heads, body, 0, unroll=min(heads, MLA_UNROLL))

    @pl.when(jnp.logical_not(is_first))
    def _():
        def body(h, _):
            m, a = _mla_block(q_ref[h], kp_ref[h], vp_ref[h], m_sc[h], acc[h], None)
            m_sc[h] = m
            acc[h] = a
            return 0
        lax.fori_loop(0, heads, body, 0, unroll=min(heads, MLA_UNROLL))

    @pl.when(is_last)
    def _():
        for h in range(heads):
            a = acc[h]
            o_ref[:, h * V_DIM:(h + 1) * V_DIM] = (a[:, :V_DIM] / a[:, V_DIM:]).astype(o_ref.dtype)


def _mla_attention(qc, kcn, vn, kcp, vp, tables, *, tq, tk, pos0, out_rows, out_block_of):
    qb, kb, fl = tables
    n_steps = qb.shape[0]
    h = H_MLA
    wqk = 2 * LANES
    new = pl.BlockSpec((h, tq, wqk), lambda s, qb, kb, fl: (0, qb[s], 0))
    past = pl.BlockSpec((h, tk, wqk), lambda s, qb, kb, fl: (0, kb[s], 0))
    grid_spec = pltpu.PrefetchScalarGridSpec(
        num_scalar_prefetch=3,
        grid=(n_steps,),
        in_specs=[new, new, new, past, past],
        out_specs=pl.BlockSpec((tq, h * V_DIM), out_block_of),
        scratch_shapes=[pltpu.VMEM((h, tq, 2 * V_DIM), F32),
                        pltpu.VMEM((h, tq, LANES), F32)],
    )
    kern = functools.partial(_mla_kernel, heads=h, pos0=pos0)
    return pl.pallas_call(
        kern, grid_spec=grid_spec,
        out_shape=jax.ShapeDtypeStruct((out_rows, h * V_DIM), BF16),
        compiler_params=_cparams(1),
    )(qb, kb, fl, qc, kcn, vn, kcp, vp)


def _causal_tables(nq, q_off, per_head=False, heads=1, n_batch=1, past_tiles=None,
                   past_stride=0):
    qb, kb, hb, fl = [], [], [], []
    if past_tiles is None:
        for i in range(nq):
            n = i + 1
            for j in range(n):
                qb.append(q_off + i)
                kb.append(max(i - j, 1) - 1 if j == 0 else i - j)
                hb.append(0)
                fl.append((1 if j == 0 else 0) | (2 if j == n - 1 else 0))
    else:
        for b in range(n_batch):
            for h in range(heads if per_head else 1):
                n = 1 + past_tiles
                for j in range(n):
                    qb.append(q_off + b)
                    jj = past_tiles - 1 if j == 0 else past_tiles - j
                    kb.append(past_stride + b * past_tiles + jj)
                    hb.append(h)
                    fl.append((1 if j == 0 else 0) | (2 if j == n - 1 else 0))
    arr = lambda v: jnp.asarray(np.asarray(v, dtype=np.int32))
    return arr(qb), arr(kb), arr(hb), arr(fl)


ROW_DMA_UNROLL = 8


def _moe_kernel(te_ref, nv_ref, tok_ref, dst_ref, h_hbm, w_ref, wg_ref, wu_ref, wd_ref, y_hbm,
                xbuf, obuf, wg_s, wu_s, wd_s, sem_in, sem_out, *, s):
    t = pl.program_id(0)
    tm = w_ref.shape[0]
    nv = nv_ref[0]
    live = t < nv
    slot = lax.rem(t, 2)

    p = _slab_pitch(s)

    def in_copy(tile, r, sl):
        src0 = pl.multiple_of(tok_ref[tile * tm + r] * p, 8)
        return pltpu.make_async_copy(h_hbm.at[pl.ds(src0, s)],
                                     xbuf.at[sl, pl.ds(pl.multiple_of(r * p, 8), s)],
                                     sem_in.at[sl])

    def gather_start(tile, sl):
        def body(r, _):
            in_copy(tile, r, sl).start()
            return 0
        lax.fori_loop(0, tm, body, 0, unroll=ROW_DMA_UNROLL)

    def gather_wait(tile, sl):
        def body(r, _):
            in_copy(tile, r, sl).wait()
            return 0
        lax.fori_loop(0, tm, body, 0, unroll=ROW_DMA_UNROLL)

    @pl.when(live & (t == 0))
    def _():
        gather_start(0, 0)

    @pl.when(t + 1 < nv)
    def _():
        gather_start(t + 1, 1 - slot)

    prev = te_ref[jnp.maximum(t - 1, 0)]
    fresh = (t == 0) | (te_ref[t] != prev)

    @pl.when(live & fresh)
    def _():
        wg_s[...] = wg_ref[...].astype(BF16)
        wu_s[...] = wu_ref[...].astype(BF16)
        wd_s[...] = wd_ref[...].astype(BF16)

    @pl.when(live)
    def _():
        gather_wait(t, slot)
        x = _slab_load(xbuf, 0, tm, s, lead=slot).astype(BF16)
        a = jnp.dot(x, wg_s[...], preferred_element_type=F32)
        u = jnp.dot(x, wu_s[...], preferred_element_type=F32)
        hid = (a * jax.nn.sigmoid(a)) * u * w_ref[...]
        _slab_store(obuf, 0, jnp.dot(hid.astype(BF16), wd_s[...], preferred_element_type=F32))

        def out_copy(r):
            dst0 = pl.multiple_of(dst_ref[t * tm + r] * p, 8)
            return pltpu.make_async_copy(obuf.at[pl.ds(pl.multiple_of(r * p, 8), s)],
                                         y_hbm.at[pl.ds(dst0, s)], sem_out)

        def start(r, _):
            out_copy(r).start()
            return 0
        lax.fori_loop(0, tm, start, 0, unroll=ROW_DMA_UNROLL)

        def wait(r, _):
            out_copy(r).wait()
            return 0
        lax.fori_loop(0, tm, wait, 0, unroll=ROW_DMA_UNROLL)


def _moe_experts(h_slab, row_tok, row_dst, row_w, tile_expert, n_valid, w_gate, w_up, w_down, l,
                 n_dest):
    d, f = w_gate.shape[-2:]
    s = d // LANES
    p = _slab_pitch(s)
    tm = MOE_TILE
    n_tiles = row_tok.shape[0] // tm
    wmap = lambda t, te, nv, tok, dst: (l, te[t], 0, 0)
    grid_spec = pltpu.PrefetchScalarGridSpec(
        num_scalar_prefetch=4, grid=(n_tiles,),
        in_specs=[pl.BlockSpec(memory_space=pl.ANY),
                  pl.BlockSpec((tm, 1), lambda t, te, nv, tok, dst: (t, 0)),
                  pl.BlockSpec((None, None, d, f), wmap),
                  pl.BlockSpec((None, None, d, f), wmap),
                  pl.BlockSpec((None, None, f, d), wmap)],
        out_specs=pl.BlockSpec(memory_space=pl.ANY),
        scratch_shapes=[pltpu.VMEM((2, tm * p, LANES), F32), pltpu.VMEM((tm * p, LANES), F32),
                        pltpu.VMEM((d, f), BF16), pltpu.VMEM((d, f), BF16),
                        pltpu.VMEM((f, d), BF16),
                        pltpu.SemaphoreType.DMA((2,)), pltpu.SemaphoreType.DMA(())],
    )
    return pl.pallas_call(
        functools.partial(_moe_kernel, s=s), grid_spec=grid_spec,
        out_shape=jax.ShapeDtypeStruct(((n_dest + tm) * p, LANES), F32),
        compiler_params=_cparams(1),
    )(tile_expert, n_valid, row_tok, row_dst, h_slab, row_w, w_gate, w_up, w_down)


def _combine_kernel(x_ref, y0_ref, y1_ref, g_ref, o_ref):
    gb, grp, d = x_ref.shape
    s = d // LANES
    for g in range(gb):
        y = _slab_load(y0_ref, g * grp, grp, s) + _slab_load(y1_ref, g * grp, grp, s)
        o_ref[g] = x_ref[g] + g_ref[g] * y


def _combine(x3, yg, modg, l, g_i):
    ng, grp, d = x3.shape
    s = d // LANES
    gb = _row_tile(ng, 2)
    blk = pl.BlockSpec((gb, grp, d), lambda i: (i, 0, 0))
    yblk = lambda off: pl.BlockSpec((gb * grp * _slab_pitch(s), LANES), lambda i: (i + off, 0))
    return pl.pallas_call(
        _combine_kernel,
        grid=(ng // gb,),
        in_specs=[blk, yblk(0), yblk(ng // gb), _mod_spec(g_i, gb, d)],
        out_specs=blk,
        out_shape=jax.ShapeDtypeStruct(x3.shape, F32),
        compiler_params=_cparams(1),
    )(x3, yg, yg, modg)


def _dispatch(route, n):
    tm = MOE_TILE
    r_cap = (2 * n + N_EXPERTS * (tm - 1) + tm - 1) // tm * tm
    e = route[:, :2].astype(jnp.int32)
    w = route[:, 2:4]
    flat_e = e.T.reshape(-1)
    flat_w = w.T.reshape(-1)
    order = jnp.argsort(flat_e, stable=True).astype(jnp.int32)
    experts = jnp.arange(N_EXPERTS, dtype=jnp.int32)
    counts = jnp.sum((flat_e[:, None] == experts[None, :]).astype(jnp.int32), axis=0)
    padded = (counts + tm - 1) // tm * tm
    ends_p = jnp.cumsum(padded)
    starts_p = ends_p - padded
    starts = jnp.cumsum(counts) - counts
    tile_start = jnp.arange(r_cap // tm, dtype=jnp.int32) * tm
    tile_expert = jnp.minimum(jnp.searchsorted(ends_p, tile_start, side="right"),
                              N_EXPERTS - 1).astype(jnp.int32)
    n_valid = (ends_p[-1:] // tm).astype(jnp.int32)
    row_e = jnp.repeat(tile_expert, tm)
    local = jnp.arange(r_cap, dtype=jnp.int32) - starts_p[row_e]
    valid = local < counts[row_e]
    slot = order[jnp.clip(starts[row_e] + local, 0, 2 * n - 1)]
    row_dst = jnp.where(valid, slot, 2 * n + jnp.arange(r_cap, dtype=jnp.int32) % tm)
    row_tok = jnp.where(valid, jnp.where(slot >= n, slot - n, slot), 0)
    row_w = jnp.where(valid, flat_w[slot], 0.0)
    return row_tok, row_dst, row_w.reshape(r_cap, 1), tile_expert, n_valid


def _rope_tables(pos):
    inv = ROPE_THETA ** (-jnp.arange(0, QK_ROPE, 2, dtype=F32) / QK_ROPE)
    ang = pos.astype(F32)[:, None] * inv[None, :]
    cos, sin = jnp.cos(ang), jnp.sin(ang)
    pad = jnp.zeros((pos.shape[0], LANES - QK_ROPE), F32)
    return (jnp.concatenate([cos, cos, pad], axis=1),
            jnp.concatenate([-sin, sin, pad], axis=1))


def _rope_lanes(acc, c, s):
    return acc * c + pltpu.roll(acc, LANES - QK_ROPE, axis=1) * s


def _swap_halves(w):
    half = w.shape[-1] // 2
    return jnp.concatenate([w[..., half:], w[..., :half]], axis=-1)


def kernel(x_prompt, x_sample, c_prompt, c_sample, cache_sb_k, cache_sb_v, cache_mla_ckv,
           cache_mla_krope, w_ada, b_ada, g_norm_mix, g_norm_ffn, w_in, g_q_lat, g_kv_lat,
           w_uq, w_ukv, w_branch_sb, w_branch_mla, w_out, w_router_group, b_router_group,
           w_router_expert, b_router_expert, w_exp_gate, w_exp_up, w_exp_down, g_final):
    bp, t_p, d = x_prompt.shape
    bs, t_s, _ = x_sample.shape
    depth = w_in.shape[0]
    past = cache_sb_k.shape[2]
    grp = t_s
    n_p, n_s = bp * t_p, bs * t_s
    n = n_p + n_s
    ng = n // grp
    sb_w = H_SB * DH_SB
    q_lora = g_q_lat.shape[1]
    kv_lora = g_kv_lat.shape[1]
    tm = _token_tile(n, grp)
    gpt = tm // grp
    tn = 512
    assert bp == 1 and t_p % ATTN_TILE == 0 and t_p % grp == 0 and past % 512 == 0

    x3 = jnp.concatenate([x_prompt.reshape(n_p // grp, grp, d), x_sample], axis=0)

    n_c = bp + bs
    c_rows = 16
    c_all = jnp.zeros((c_rows, d), F32).at[:n_c].set(jnp.concatenate([c_prompt, c_sample], 0))
    n_modc = N_MOD * d

    def ada_epi(accs, ex, outs):
        outs[0][...] = accs[0] + ex[0][...]

    mods = []
    for l in range(depth):
        mods.append(_matmul(
            [(c_all, w_ada, _wspec(l, d, tn, 0), True)], m=c_rows, n_out=n_modc, tm=c_rows, tn=tn,
            prologue=lambda a, ex: a * jax.nn.sigmoid(a),
            epilogue=ada_epi,
            extras=[b_ada.reshape(depth, 1, n_modc)],
            extra_specs=[pl.BlockSpec((None, 1, tn), lambda j, i, l=l: (l, 0, j))],
            out_shape=[jax.ShapeDtypeStruct((c_rows, n_modc), F32)],
            out_specs=[_spec2(c_rows, tn)])[0])
    mod = jnp.stack(mods)
    modg = jnp.concatenate(
        [jnp.broadcast_to(mod[:, :bp], (depth, n_p // grp, n_modc)), mod[:, bp:n_c]], axis=1)
    modg = modg.reshape(depth, ng, N_MOD, d).transpose(0, 2, 1, 3)
    modg = modg.reshape(depth * N_MOD, ng, 1, d)

    def mod_idx(l, k):
        return l * N_MOD + k

    def mspec_rows(l, k):
        return pl.BlockSpec((None, gpt, 1, tn), lambda j, i: (mod_idx(l, k), i, 0, j))

    pos = jnp.concatenate([jnp.arange(t_p, dtype=jnp.int32),
                           jnp.tile(past + jnp.arange(t_s, dtype=jnp.int32), bs)])
    rope_c, rope_s = _rope_tables(pos)
    tq = ATTN_TILE
    nq_p = t_p // tq
    tab_p = _causal_tables(nq_p, 0)
    tk_s = 512
    pt = past // tk_s
    tab_sb_s = lambda l: _causal_tables(0, n_p // t_s, n_batch=bs, past_tiles=pt,
                                        past_stride=l * bs * pt)
    tab_mla_s = _causal_tables(0, n_p // t_s, n_batch=bs, past_tiles=pt)

    off_q, off_k, off_v = 0, sb_w, 2 * sb_w
    off_cq = 3 * sb_w
    off_ckv = off_cq + q_lora
    off_kr = off_ckv + kv_lora
    off_g = off_kr + QK_ROPE
    w_kr = w_in[:, :, off_kr:off_kr + QK_ROPE]
    w_kr_aug = jnp.concatenate([w_kr, _swap_halves(w_kr)], axis=-1).astype(BF16)
    w_gates = w_in[:, :, off_g:].astype(BF16)
    uq = w_uq.reshape(depth, q_lora, H_MLA, QK_NOPE + QK_ROPE)
    uq_r = uq[..., QK_NOPE:]
    w_uq_cat = jnp.concatenate([uq[..., :QK_NOPE], uq_r, _swap_halves(uq_r)], axis=-1)
    w_uq_cat = w_uq_cat.reshape(depth, q_lora, H_MLA * 2 * LANES)
    w_router = jnp.concatenate(
        [w_router_expert, w_router_group,
         jnp.zeros((depth, d, LANES - N_EXPERTS - N_GROUPS), F32)], axis=-1)
    r_hi = w_router.astype(BF16)
    r_res = w_router - r_hi.astype(F32)
    r_mid = r_res.astype(BF16)
    r_lo = (r_res - r_mid.astype(F32)).astype(BF16)
    w_router3 = jnp.stack([r_hi, r_mid, r_lo], axis=1)
    b_router = jnp.concatenate(
        [b_router_expert, b_router_group,
         jnp.zeros((depth, LANES - N_EXPERTS - N_GROUPS), F32)], axis=-1).reshape(depth, 1, LANES)

    hm = lambda rows: jax.ShapeDtypeStruct((H_SB, rows, LANES), BF16)
    new_k, new_v, new_c, new_r = [], [], [], []

    for l in range(depth):
        h = _norm_mod(x3, g_norm_mix.reshape(depth, 1, d), modg, l, mod_idx(l, 1), mod_idx(l, 0))
        h = h.reshape(n, d)

        def plain_hm(accs, ex, outs):
            _store_heads(outs[0], accs[0])

        def f32_and_hm(accs, ex, outs):
            outs[0][...] = accs[0]
            _store_heads(outs[1], accs[0])

        sb_q = _matmul([(h, w_in, _wspec(l, d, tn, off_q), True)], m=n, n_out=sb_w, tm=tm, tn=tn,
                       epilogue=plain_hm, out_shape=[hm(n)], out_specs=[_hm_spec(tm, tn)])[0]
        k_f32, sb_k = _matmul([(h, w_in, _wspec(l, d, tn, off_k), True)], m=n, n_out=sb_w, tm=tm,
                              tn=tn, epilogue=f32_and_hm,
                              out_shape=[jax.ShapeDtypeStruct((n, sb_w), F32), hm(n)],
                              out_specs=[_spec2(tm, tn), _hm_spec(tm, tn)])
        v_f32, sb_v = _matmul([(h, w_in, _wspec(l, d, tn, off_v), True)], m=n, n_out=sb_w, tm=tm,
                              tn=tn, epilogue=f32_and_hm,
                              out_shape=[jax.ShapeDtypeStruct((n, sb_w), F32), hm(n)],
                              out_specs=[_spec2(tm, tn), _hm_spec(tm, tn)])

        def plain_f32(accs, ex, outs):
            outs[0][...] = accs[0]

        c_q = _matmul([(h, w_in, _wspec(l, d, tn, off_cq), True)], m=n, n_out=q_lora, tm=tm, tn=tn,
                      epilogue=plain_f32, out_shape=[jax.ShapeDtypeStruct((n, q_lora), F32)],
                      out_specs=[_spec2(tm, tn)])[0]

        def ckv_epi(accs, ex, outs):
            y = _rms(accs[0], ex[0][...])
            outs[0][...] = y
            outs[1][...] = y.astype(BF16)

        c_kv, c_kv_b = _matmul(
            [(h, w_in, _wspec(l, d, kv_lora, off_ckv), True)], m=n, n_out=kv_lora, tm=tm, tn=kv_lora,
            epilogue=ckv_epi, extras=[g_kv_lat.reshape(depth, 1, kv_lora)],
            extra_specs=[pl.BlockSpec((None, 1, kv_lora), lambda j, i: (l, 0, 0))],
            out_shape=[jax.ShapeDtypeStruct((n, kv_lora), F32),
                       jax.ShapeDtypeStruct((n, kv_lora), BF16)],
            out_specs=[_spec2(tm, kv_lora), _spec2(tm, kv_lora)])

        def kr_epi(accs, ex, outs):
            r = _rope_lanes(accs[0], ex[0][...], ex[1][...])
            outs[0][...] = r
            outs[1][...] = r.astype(BF16)

        rope_specs = [pl.BlockSpec((tm, LANES), lambda j, i: (i, 0))] * 2
        k_r, k_r_b = _matmul(
            [(h, w_kr_aug, pl.BlockSpec((None, d, LANES), lambda j, i: (l, 0, 0)), False)],
            m=n, n_out=LANES, tm=tm, tn=LANES, epilogue=kr_epi,
            extras=[rope_c, rope_s], extra_specs=rope_specs,
            out_shape=[jax.ShapeDtypeStruct((n, LANES), F32), jax.ShapeDtypeStruct((n, LANES), BF16)],
            out_specs=[_spec2(tm, LANES), _spec2(tm, LANES)])

        def gate_epi(accs, ex, outs):
            outs[0][...] = jax.nn.sigmoid(accs[0])

        gates = _matmul(
            [(h, w_gates, pl.BlockSpec((None, d, tn), lambda j, i: (l, 0, j)), False)],
            m=n, n_out=2 * d, tm=tm, tn=tn, epilogue=gate_epi,
            out_shape=[jax.ShapeDtypeStruct((n, 2 * d), F32)], out_specs=[_spec2(tm, tn)])[0]

        def cq_prologue(a, ex):
            return _rms(a, ex[0][...])

        gq_spec = pl.BlockSpec((None, 1, q_lora), lambda j, i: (l, 0, 0))
        gq = g_q_lat.reshape(depth, 1, q_lora)
        wqk = 2 * LANES
        hm_qk = lambda rows: jax.ShapeDtypeStruct((H_MLA, rows, wqk), BF16)

        q_scale = (QK_NOPE + QK_ROPE) ** -0.5 * float(np.log2(np.e))

        def qcat_epi(accs, ex, outs):
            c, s = ex[1][...], ex[2][...]
            for hh in range(tn // wqk):
                blk = accs[0][:, hh * wqk:(hh + 1) * wqk]
                outs[0][hh, :, :LANES] = (blk[:, :LANES] * q_scale).astype(BF16)
                outs[0][hh, :, LANES:] = (_rope_lanes(blk[:, LANES:], c, s) * q_scale).astype(BF16)

        q_cat = _matmul(
            [(c_q, w_uq_cat, pl.BlockSpec((None, q_lora, tn), lambda j, i: (l, 0, j)), True)],
            m=n, n_out=H_MLA * wqk, tm=tm, tn=tn, prologue=cq_prologue, epilogue=qcat_epi,
            extras=[gq, rope_c, rope_s], extra_specs=[gq_spec] + rope_specs,
            out_shape=[hm_qk(n)],
            out_specs=[pl.BlockSpec((tn // wqk, tm, wqk), lambda j, i: (j, i, 0))])[0]

        tn_kv = min(2048, H_MLA * wqk)
        hpt = tn_kv // wqk

        def kv_epi(accs, ex, outs):
            kr = ex[0][...]
            ones = jnp.ones(kr.shape, BF16)
            for hh in range(hpt):
                outs[0][hh, :, :LANES] = accs[0][:, hh * wqk:hh * wqk + LANES].astype(BF16)
                outs[0][hh, :, LANES:] = kr
                outs[1][hh, :, :LANES] = accs[0][:, hh * wqk + LANES:(hh + 1) * wqk].astype(BF16)
                outs[1][hh, :, LANES:] = ones

        def up_kv(a, kr, rows, tmr):
            hspec = pl.BlockSpec((hpt, tmr, wqk), lambda j, i: (j, i, 0))
            return _matmul(
                [(a, w_ukv, pl.BlockSpec((None, kv_lora, tn_kv), lambda j, i: (l, 0, j)), True)],
                m=rows, n_out=H_MLA * wqk, tm=tmr, tn=tn_kv, epilogue=kv_epi, extras=[kr],
                extra_specs=[pl.BlockSpec((tmr, LANES), lambda j, i: (i, 0))],
                out_shape=[hm_qk(rows), hm_qk(rows)], out_specs=[hspec, hspec])

        kc_new, v_new = up_kv(c_kv_b, k_r_b, n, tm)
        rows_past = bs * past
        kr_past = jnp.pad(cache_mla_krope[l].reshape(rows_past, QK_ROPE),
                          ((0, 0), (0, LANES - QK_ROPE))).astype(BF16)
        kc_past, v_past = up_kv(cache_mla_ckv[l].reshape(rows_past, kv_lora), kr_past, rows_past,
                                512)

        o_sb_p = _sb_self_attention(sb_q, sb_k, sb_v, tq=tq, n_rows=n_p)
        cache_k2 = cache_sb_k.reshape(depth * bs * past * H_SB, DH_SB)
        cache_v2 = cache_sb_v.reshape(depth * bs * past * H_SB, DH_SB)
        q_off_s = n_p // t_s
        o_sb_s = _sb_attention(sb_q, sb_k, sb_v, cache_k2, cache_v2, tab_sb_s(l), heads=H_SB,
                               tq=t_s, tk=tk_s, sub=ATTN_TILE, out_rows=n_s,
                               out_block_of=lambda s, qb, kb, hb, fl: (qb[s] - q_off_s, 0))
        o_sb = jnp.concatenate([o_sb_p, o_sb_s], axis=0)

        tp3 = (tab_p[0], tab_p[1], tab_p[3])
        o_mla_p = _mla_attention(q_cat, kc_new, v_new, kc_new, v_new, tp3,
                                 tq=tq, tk=tq, pos0=0, out_rows=n_p,
                                 out_block_of=lambda s, qb, kb, fl: (qb[s], 0))
        ts3 = (tab_mla_s[0], tab_mla_s[1], tab_mla_s[3])
        o_mla_s = _mla_attention(q_cat, kc_new, v_new, kc_past, v_past, ts3,
                                 tq=t_s, tk=tk_s, pos0=past, out_rows=n_s,
                                 out_block_of=lambda s, qb, kb, fl: (qb[s] - q_off_s, 0))
        o_mla = jnp.concatenate([o_mla_p, o_mla_s], axis=0)

        def merge_epi(accs, ex, outs):
            outs[0][...] = (ex[0][...] * accs[0] + ex[1][...] * accs[1]).astype(BF16)

        merged = _matmul(
            [(o_sb, w_branch_sb, _wspec(l, sb_w, tn, 0), True),
             (o_mla, w_branch_mla, _wspec(l, H_MLA * V_DIM, tn, 0), True)],
            m=n, n_out=d, tm=tm, tn=tn, epilogue=merge_epi,
            extras=[gates, gates],
            extra_specs=[pl.BlockSpec((tm, tn), lambda j, i: (i, j)),
                         pl.BlockSpec((tm, tn), lambda j, i: (i, j + d // tn))],
            out_shape=[jax.ShapeDtypeStruct((n, d), BF16)], out_specs=[_spec2(tm, tn)])[0]

        def resid_epi(accs, ex, outs):
            for g in range(gpt):
                outs[0][g] = ex[0][g] + ex[1][g] * accs[0][g * grp:(g + 1) * grp, :]

        x_spec = pl.BlockSpec((gpt, grp, tn), lambda j, i: (i, 0, j))
        x3 = _matmul(
            [(merged, w_out, _wspec(l, d, tn, 0), True)], m=n, n_out=d, tm=tm, tn=tn,
            epilogue=resid_epi, extras=[x3, modg], extra_specs=[x_spec, mspec_rows(l, 2)],
            out_shape=[jax.ShapeDtypeStruct((ng, grp, d), F32)], out_specs=[x_spec])[0]

        h2, route = _norm_route(x3, g_norm_ffn.reshape(depth, 1, d), modg, l, mod_idx(l, 4),
                                mod_idx(l, 3), w_router3, b_router)
        row_tok, row_dst, row_w, tile_expert, n_valid = _dispatch(route.reshape(n, LANES), n)
        yg = _moe_experts(h2, row_tok, row_dst, row_w, tile_expert, n_valid,
                          w_exp_gate, w_exp_up, w_exp_down, l, 2 * n)
        x3 = _combine(x3, yg, modg, l, mod_idx(l, 5))

        new_k.append(k_f32)
        new_v.append(v_f32)
        new_c.append(c_kv)
        new_r.append(k_r[:, :QK_ROPE])

    y = _final_norm(x3, g_final.reshape(1, d)).reshape(n, d)

    def split(parts, tail):
        a = jnp.stack(parts)
        return (a[:, :n_p].reshape((depth, bp, t_p) + tail),
                a[:, n_p:].reshape((depth, bs, t_s) + tail))

    pk, sk = split(new_k, (H_SB, DH_SB))
    pv, sv = split(new_v, (H_SB, DH_SB))
    pc, sc = split(new_c, (kv_lora,))
    pr, sr = split(new_r, (QK_ROPE,))
    return (y[:n_p].reshape(bp, t_p, d), y[n_p:].reshape(bs, t_s, d),
            pk, pv, pc, pr, sk, sv, sc, sr)
```

```python
import functools

import numpy as np
import jax
import jax.numpy as jnp
from jax import lax
from jax.experimental import pallas as pl
from jax.experimental.pallas import tpu as pltpu

F32 = jnp.float32
BF16 = jnp.bfloat16

CHUNK = 64
H_SB = 16
DH_SB = 128
H_MLA = 16
QK_NOPE = 128
QK_ROPE = 64
V_DIM = 128
ROPE_THETA = 10000.0
N_GROUPS = 4
EXPERTS_PER_GROUP = 8
N_EXPERTS = N_GROUPS * EXPERTS_PER_GROUP
N_MOD = 6
EPS = 1e-6

LANES = 128
ATTN_TILE = 256
MOE_TILE = 256
HEAD_UNROLL = 4
MLA_UNROLL = 16
SB_DEAD = 104.0
VMEM_LIMIT = 56 * 1024 * 1024


def _cparams(n_axes, vmem=VMEM_LIMIT):
    return pltpu.CompilerParams(dimension_semantics=("arbitrary",) * n_axes,
                                vmem_limit_bytes=vmem)


def _row_tile(n, cap=512):
    t = cap
    while n % t:
        t //= 2
    return t


def _token_tile(n, grp, cap=1152):
    return max(t for t in range(grp, cap + 1, grp) if n % t == 0)


def _matmul(pairs, *, m, n_out, tm, tn, epilogue, out_shape, out_specs,
            extras=(), extra_specs=(), prologue=None):
    n_pairs = len(pairs)
    n_ex = len(extras)
    n_outs = len(out_shape)
    cast = [p[3] for p in pairs]

    def kern(*refs):
        a_refs = refs[0:2 * n_pairs:2]
        b_refs = refs[1:2 * n_pairs:2]
        ex = refs[2 * n_pairs:2 * n_pairs + n_ex]
        outs = refs[2 * n_pairs + n_ex:2 * n_pairs + n_ex + n_outs]
        scr = refs[2 * n_pairs + n_ex + n_outs:]
        i = pl.program_id(1)
        accs = []
        si = 0
        for p in range(n_pairs):
            if cast[p]:
                bsc = scr[si]
                si += 1

                @pl.when(i == 0)
                def _(bsc=bsc, b_ref=b_refs[p]):
                    bsc[...] = b_ref[...].astype(BF16)

                bv = bsc[...]
            else:
                bv = b_refs[p][...]
            a = a_refs[p][...]
            if prologue is not None:
                a = prologue(a, ex)
            accs.append(jnp.dot(a.astype(BF16), bv, preferred_element_type=F32))
        epilogue(accs, ex, outs)

    in_specs, args, scratch = [], [], []
    for (a, b, b_spec, cb) in pairs:
        k = a.shape[1]
        in_specs += [pl.BlockSpec((tm, k), lambda j, i: (i, 0)), b_spec]
        args += [a, b]
        if cb:
            scratch.append(pltpu.VMEM((k, tn), BF16))
    in_specs += list(extra_specs)
    args += list(extras)
    return pl.pallas_call(
        kern,
        grid=(n_out // tn, m // tm),
        in_specs=in_specs,
        out_specs=out_specs,
        out_shape=out_shape,
        scratch_shapes=scratch,
        compiler_params=_cparams(2),
    )(*args)


def _wspec(l, k, tn, col_off):
    cb = col_off // tn
    assert cb * tn == col_off
    return pl.BlockSpec((None, k, tn), lambda j, i: (l, 0, cb + j))


def _spec2(tm, tn):
    return pl.BlockSpec((tm, tn), lambda j, i: (i, j))


def _hm_spec(tm, tn):
    return pl.BlockSpec((tn // LANES, tm, LANES), lambda j, i: (j, i, 0))


def _store_heads(o_ref, val):
    for c in range(val.shape[1] // LANES):
        o_ref[c] = val[:, c * LANES:(c + 1) * LANES].astype(o_ref.dtype)


def _rms(x, g):
    return x * lax.rsqrt(jnp.mean(x * x, axis=-1, keepdims=True) + EPS) * g


def _norm_mod_kernel(x_ref, g_ref, sc_ref, sh_ref, o_ref):
    x = x_ref[...]
    y = _rms(x, g_ref[...])
    o_ref[...] = (y * (1.0 + sc_ref[...]) + sh_ref[...]).astype(o_ref.dtype)


def _split3(x):
    hi = x.astype(BF16)
    r = x - hi.astype(F32)
    mid = r.astype(BF16)
    lo = (r - mid.astype(F32)).astype(BF16)
    return hi, mid, lo


def _dot_f32(a, b3):
    a_hi, a_mid, a_lo = _split3(a)
    b_hi, b_mid, b_lo = b3
    d = functools.partial(jnp.dot, preferred_element_type=F32)
    small = d(a_hi, b_lo) + d(a_lo, b_hi) + d(a_mid, b_mid)
    return (d(a_hi, b_hi) + (d(a_hi, b_mid) + d(a_mid, b_hi))) + small


def _route(logits):
    lane = lax.broadcasted_iota(jnp.int32, logits.shape, 1)
    lanef = lane.astype(F32)
    big = jnp.float32(1e9)
    ninf = jnp.float32(-jnp.inf)
    is_g = (lane >= N_EXPERTS) & (lane < N_EXPERTS + N_GROUPS)
    gl = jnp.where(is_g, logits, ninf)
    gmax = jnp.max(gl, axis=1, keepdims=True)
    g_idx = jnp.min(jnp.where(gl == gmax, lanef - N_EXPERTS, big), axis=1, keepdims=True)
    p_group = 1.0 / jnp.sum(jnp.where(is_g, jnp.exp(gl - gmax), 0.0), axis=1, keepdims=True)
    grp = jnp.floor(lanef * (1.0 / EXPERTS_PER_GROUP))
    in_g = (lane < N_EXPERTS) & (grp == g_idx)
    el = jnp.where(in_g, logits, ninf)
    e1 = jnp.max(el, axis=1, keepdims=True)
    i1 = jnp.min(jnp.where(el == e1, lanef, big), axis=1, keepdims=True)
    el2 = jnp.where(lanef == i1, ninf, el)
    e2 = jnp.max(el2, axis=1, keepdims=True)
    i2 = jnp.min(jnp.where(el2 == e2, lanef, big), axis=1, keepdims=True)
    t = jnp.exp(e2 - e1)
    den = 1.0 + t
    w1 = (1.0 / den) * p_group
    w2 = (t / den) * p_group
    out = jnp.where(lane == 0, i1, jnp.where(lane == 1, i2,
          jnp.where(lane == 2, w1, jnp.where(lane == 3, w2, 0.0))))
    return out


def _slab_pitch(s):
    return s + 8


def _slab_store(ref, row0, val):
    rows, d = val.shape
    s = d // LANES
    p = _slab_pitch(s)
    for c in range(s):
        ref[pl.ds(row0 * p + c, rows, stride=p), :] = val[:, c * LANES:(c + 1) * LANES]


def _slab_load(ref, row0, rows, s, lead=None):
    pieces = []
    p = _slab_pitch(s)
    for c in range(s):
        rs = pl.ds(row0 * p + c, rows, stride=p)
        pieces.append(ref[rs, :] if lead is None else ref[lead, rs, :])
    return jnp.concatenate(pieces, axis=1)


def _norm_route_kernel(x_ref, g_ref, sc_ref, sh_ref, wr_ref, br_ref, h_ref, r_ref):
    x = x_ref[...]
    y = _rms(x, g_ref[...])
    h = y * (1.0 + sc_ref[...]) + sh_ref[...]
    b3 = (wr_ref[0], wr_ref[1], wr_ref[2])
    grp = x.shape[1]
    for g in range(x.shape[0]):
        _slab_store(h_ref, g * grp, h[g])
        logits = _dot_f32(h[g], b3) + br_ref[...]
        r_ref[g] = _route(logits)


def _mod_spec(idx, gb, d):
    return pl.BlockSpec((None, gb, 1, d), lambda i: (idx, i, 0, 0))


def _norm_mod(x3, g, modg, l, sc_i, sh_i):
    ng, grp, d = x3.shape
    gb = _row_tile(ng, 4)
    return pl.pallas_call(
        _norm_mod_kernel,
        grid=(ng // gb,),
        in_specs=[pl.BlockSpec((gb, grp, d), lambda i: (i, 0, 0)),
                  pl.BlockSpec((None, 1, d), lambda i: (l, 0, 0)),
                  _mod_spec(sc_i, gb, d), _mod_spec(sh_i, gb, d)],
        out_specs=pl.BlockSpec((gb, grp, d), lambda i: (i, 0, 0)),
        out_shape=jax.ShapeDtypeStruct(x3.shape, BF16),
        compiler_params=_cparams(1),
    )(x3, g, modg, modg)


def _norm_route(x3, g, modg, l, sc_i, sh_i, wr3, br):
    ng, grp, d = x3.shape
    gb = _row_tile(ng, 4)
    return pl.pallas_call(
        _norm_route_kernel,
        grid=(ng // gb,),
        in_specs=[pl.BlockSpec((gb, grp, d), lambda i: (i, 0, 0)),
                  pl.BlockSpec((None, 1, d), lambda i: (l, 0, 0)),
                  _mod_spec(sc_i, gb, d), _mod_spec(sh_i, gb, d),
                  pl.BlockSpec((None, 3, d, LANES), lambda i: (l, 0, 0, 0)),
                  pl.BlockSpec((None, 1, LANES), lambda i: (l, 0, 0))],
        out_specs=[pl.BlockSpec((gb * grp * _slab_pitch(d // LANES), LANES), lambda i: (i, 0)),
                   pl.BlockSpec((gb, grp, LANES), lambda i: (i, 0, 0))],
        out_shape=[jax.ShapeDtypeStruct((ng * grp * _slab_pitch(d // LANES), LANES), F32),
                   jax.ShapeDtypeStruct((ng, grp, LANES), F32)],
        compiler_params=_cparams(1),
    )(x3, g, modg, modg, wr3, br)


def _final_norm_kernel(x_ref, g_ref, op_ref, os_ref, *, n_prompt_blocks):
    i = pl.program_id(0)
    y = _rms(x_ref[...], g_ref[...])

    @pl.when(i < n_prompt_blocks)
    def _():
        op_ref[...] = y

    @pl.when(i >= n_prompt_blocks)
    def _():
        os_ref[...] = y


def _final_norm(x3, g, ng_prompt):
    ng, grp, d = x3.shape
    gb = _row_tile(np.gcd(ng_prompt, ng - ng_prompt), 4)
    npb = ng_prompt // gb
    blk = lambda f: pl.BlockSpec((gb, grp, d), f)
    return pl.pallas_call(
        functools.partial(_final_norm_kernel, n_prompt_blocks=npb),
        grid=(ng // gb,),
        in_specs=[blk(lambda i: (i, 0, 0)), pl.BlockSpec((1, d), lambda i: (0, 0))],
        out_specs=[blk(lambda i: (jnp.minimum(i, npb - 1), 0, 0)),
                   blk(lambda i: (jnp.maximum(i - npb, 0), 0, 0))],
        out_shape=[jax.ShapeDtypeStruct((ng_prompt, grp, d), F32),
                   jax.ShapeDtypeStruct((ng - ng_prompt, grp, d), F32)],
        compiler_params=_cparams(1),
    )(x3, g)


def _lanes(c, w):
    if w % LANES == 0:
        return c if w == LANES else jnp.tile(c, (1, w // LANES))
    return c[:, :w]


def _dot_nt(a, b):
    return lax.dot_general(a, b, (((1,), (1,)), ((), ())), preferred_element_type=F32)


def _sb_block(qh, kh, vh, c, u, scale, masked):
    w = kh.shape[0]
    z = _dot_nt(qh, kh) * scale
    sp = jnp.maximum(z, 0.0) + jnp.log1p(jnp.exp(-jnp.abs(z)))
    if masked:
        row = lax.broadcasted_iota(jnp.int32, z.shape, 0)
        col = lax.broadcasted_iota(jnp.int32, z.shape, 1)
        valid = col < row
        sp = jnp.where(valid, sp, 0.0)
    hi = sp.astype(BF16)
    lo = (sp - hi.astype(F32)).astype(BF16)
    cs = jnp.dot(hi, u, preferred_element_type=F32) + jnp.dot(lo, u, preferred_element_type=F32)
    wgt = jnp.exp(z - sp - cs - _lanes(c, w))
    if masked:
        wgt = jnp.where(valid, wgt, 0.0)
    o = jnp.dot(wgt.astype(BF16), vh, preferred_element_type=F32)
    c_new = c + jnp.sum(sp, axis=1, keepdims=True)
    return o, c_new


def _sb_kernel(qb_ref, kb_ref, hb_ref, fl_ref, q_ref, kn_ref, vn_ref, kp_ref, vp_ref,
               ud_ref, up_ref, o_ref, acc, carry, done, *, heads, scale, sub, n_sub):
    s = pl.program_id(0)
    fl = fl_ref[s]
    is_first = (fl & 1) != 0
    is_last = (fl & 2) != 0

    def past_block(ref, h, j):
        if len(ref.shape) == 3:
            return ref[h, pl.ds(j * sub, sub), :].astype(BF16)
        return ref[pl.ds(j * sub * heads + h, sub, stride=heads), :].astype(BF16)

    @pl.when(is_first)
    def _():
        def body(h, _):
            o, c = _sb_block(q_ref[h], kn_ref[h], vn_ref[h], jnp.zeros(carry.shape[1:], F32),
                             ud_ref[...], scale, True)
            acc[h] = o
            carry[h] = c
            done[h] = 0
            return 0
        lax.fori_loop(0, heads, body, 0, unroll=min(heads, HEAD_UNROLL))

    @pl.when(jnp.logical_not(is_first))
    def _():
        def body(h, _):
            @pl.when(done[h] == 0)
            def _():
                qh = q_ref[h]
                c = carry[h]
                o_tot = acc[h]
                for j in reversed(range(n_sub)):
                    o, c = _sb_block(qh, past_block(kp_ref, h, j), past_block(vp_ref, h, j), c,
                                     up_ref[...], scale, False)
                    o_tot = o_tot + o
                acc[h] = o_tot
                carry[h] = c
                done[h] = (jnp.min(c) >= SB_DEAD).astype(jnp.int32)
            return 0
        lax.fori_loop(0, heads, body, 0, unroll=min(heads, HEAD_UNROLL))

    @pl.when(is_last)
    def _():
        for h in range(heads):
            o_ref[:, h * DH_SB:(h + 1) * DH_SB] = acc[h].astype(o_ref.dtype)


def _suffix_matrix(w):
    j = np.arange(w)[:, None]
    s = np.arange(w)[None, :]
    return jnp.asarray((j > s).astype(np.float32), dtype=BF16)


def _sb_self_kernel(q_ref, kn_ref, vn_ref, k_hbm, v_hbm, ud_ref, up_ref, o_ref,
                    kbuf, vbuf, acc, carry, done, alive_ref, sem, *, heads, scale, tk):
    i = pl.program_id(0)

    def fetch(j, slot):
        rows = pl.ds(pl.multiple_of(j * tk, tk), tk)
        return (pltpu.make_async_copy(k_hbm.at[:, rows, :], kbuf.at[slot], sem.at[0, slot]),
                pltpu.make_async_copy(v_hbm.at[:, rows, :], vbuf.at[slot], sem.at[1, slot]))

    @pl.when(i > 0)
    def _():
        for c in fetch(i - 1, 0):
            c.start()

    def diag(h, _):
        o, c = _sb_block(q_ref[h], kn_ref[h], vn_ref[h], jnp.zeros(carry.shape[1:], F32),
                         ud_ref[...], scale, True)
        acc[h] = o
        carry[h] = c
        done[h] = 0
        return 0
    lax.fori_loop(0, heads, diag, 0, unroll=min(heads, HEAD_UNROLL))

    alive_ref[0] = heads

    @pl.when(i > 0)
    def _():
        for c in fetch(i - 1, 0):
            c.wait()

        @pl.when(i > 1)
        def _():
            for c in fetch(i - 2, 1):
                c.start()

        def first(h, alive):
            o, c = _sb_block(q_ref[h], kbuf[0, h], vbuf[0, h], carry[h], up_ref[...], scale, False)
            acc[h] = acc[h] + o
            carry[h] = c
            dead = (jnp.min(c) >= SB_DEAD).astype(jnp.int32)
            done[h] = dead
            return alive + 1 - dead
        alive_ref[0] = lax.fori_loop(0, heads, first, 0, unroll=min(heads, HEAD_UNROLL))

    def cond(state):
        j, alive = state
        return (j >= 0) & (alive > 0)

    def body(state):
        j, _ = state
        slot = lax.rem(i - 1 - j, 2)
        for c in fetch(j, slot):
            c.wait()

        @pl.when(j > 0)
        def _():
            for c in fetch(j - 1, 1 - slot):
                c.start()

        def head(h, alive):
            @pl.when(done[h] == 0)
            def _():
                o, c = _sb_block(q_ref[h], kbuf[slot, h], vbuf[slot, h], carry[h], up_ref[...],
                                 scale, False)
                acc[h] = acc[h] + o
                carry[h] = c
                done[h] = (jnp.min(c) >= SB_DEAD).astype(jnp.int32)
            return alive + 1 - done[h]
        return j - 1, lax.fori_loop(0, heads, head, 0)

    j_end, _ = lax.while_loop(cond, body, (i - 2, alive_ref[0]))

    @pl.when(j_end >= 0)
    def _():
        for c in fetch(j_end, lax.rem(i - 1 - j_end, 2)):
            c.wait()

    for h in range(heads):
        o_ref[:, h * DH_SB:(h + 1) * DH_SB] = acc[h].astype(o_ref.dtype)


def _sb_self_attention(q, k, v, *, tq, n_rows):
    heads = q.shape[0]
    blk = pl.BlockSpec((heads, tq, DH_SB), lambda i: (0, i, 0))
    umat = pl.BlockSpec((tq, tq), lambda i: (0, 0))
    anyspec = pl.BlockSpec(memory_space=pl.ANY)
    kern = functools.partial(_sb_self_kernel, heads=heads, scale=DH_SB ** -0.5, tk=tq)
    return pl.pallas_call(
        kern, grid=(n_rows // tq,),
        in_specs=[blk, blk, blk, anyspec, anyspec, umat, umat],
        out_specs=pl.BlockSpec((tq, heads * DH_SB), lambda i: (i, 0)),
        out_shape=jax.ShapeDtypeStruct((n_rows, heads * DH_SB), BF16),
        scratch_shapes=[pltpu.VMEM((2, heads, tq, DH_SB), BF16),
                        pltpu.VMEM((2, heads, tq, DH_SB), BF16),
                        pltpu.VMEM((heads, tq, DH_SB), F32),
                        pltpu.VMEM((heads, tq, LANES), F32),
                        pltpu.SMEM((heads,), jnp.int32),
                        pltpu.SMEM((1,), jnp.int32),
                        pltpu.SemaphoreType.DMA((2, 2))],
        compiler_params=_cparams(1),
    )(q, k, v, k, v, _suffix_matrix(tq), _suffix_matrix(tq))


def _sb_attention(q, kn, vn, kp, vp, tables, *, heads, tq, tk, sub, out_rows, out_block_of):
    qb, kb, hb, fl = tables
    n_steps = qb.shape[0]
    hblk = heads

    def qmap(s, qb, kb, hb, fl):
        return (0, qb[s], 0)

    if kp.ndim == 3:
        past_spec = pl.BlockSpec((hblk, tk, DH_SB), lambda s, qb, kb, hb, fl: (0, kb[s], 0))
    else:
        past_spec = pl.BlockSpec((tk * hblk, DH_SB), lambda s, qb, kb, hb, fl: (kb[s], 0))
    new_spec = pl.BlockSpec((hblk, tq, DH_SB), qmap)
    const = lambda s, qb, kb, hb, fl: (0, 0)
    grid_spec = pltpu.PrefetchScalarGridSpec(
        num_scalar_prefetch=4,
        grid=(n_steps,),
        in_specs=[new_spec, new_spec, new_spec, past_spec, past_spec,
                  pl.BlockSpec((tq, tq), const), pl.BlockSpec((sub, sub), const)],
        out_specs=pl.BlockSpec((tq, hblk * DH_SB), out_block_of),
        scratch_shapes=[pltpu.VMEM((hblk, tq, DH_SB), F32),
                        pltpu.VMEM((hblk, tq, LANES), F32),
                        pltpu.SMEM((hblk,), jnp.int32)],
    )
    kern = functools.partial(_sb_kernel, heads=hblk, scale=DH_SB ** -0.5, sub=sub,
                             n_sub=tk // sub)
    return pl.pallas_call(
        kern, grid_spec=grid_spec,
        out_shape=jax.ShapeDtypeStruct((out_rows, H_SB * DH_SB), BF16),
        compiler_params=_cparams(1),
    )(qb, kb, hb, fl, q, kn, vn, kp, vp, _suffix_matrix(tq), _suffix_matrix(sub))


def _mla_block(qc, kc, va, m, acc, mask):
    s = _dot_nt(qc, kc)
    if mask is not None:
        s = jnp.where(mask, s, -jnp.inf)
    m_new = jnp.maximum(m, jnp.max(s, axis=1, keepdims=True))
    alpha = jnp.exp2(m - m_new)
    p = jnp.exp2(s - _lanes(m_new, s.shape[1]))
    acc_new = _lanes(alpha, acc.shape[1]) * acc + jnp.dot(p.astype(BF16), va,
                                                          preferred_element_type=F32)
    return m_new, acc_new


def _mla_kernel(qb_ref, kb_ref, fl_ref, q_ref, kn_ref, vn_ref, kp_ref, vp_ref, o_ref,
                acc, m_sc, *, heads, pos0):
    s = pl.program_id(0)
    fl = fl_ref[s]
    is_first = (fl & 1) != 0
    is_last = (fl & 2) != 0
    tq = q_ref.shape[1]

    @pl.when(is_first)
    def _():
        row = lax.broadcasted_iota(jnp.int32, (tq, tq), 0) + pos0
        col = lax.broadcasted_iota(jnp.int32, (tq, tq), 1) + pos0
        mask = (col // CHUNK) <= (row // CHUNK)

        def body(h, _):
            m0 = jnp.full((tq, LANES), -jnp.inf, F32)
            a0 = jnp.zeros((tq, 2 * V_DIM), F32)
            m, a = _mla_block(q_ref[h], kn_ref[h], vn_ref[h], m0, a0, mask)
            m_sc[h] = m
            acc[h] = a
            return 0
        lax.fori_loop(0, heads, body, 0, unroll=min(heads, MLA_UNROLL))

    @pl.when(jnp.logical_not(is_first))
    def _():
        def body(h, _):
            m, a = _mla_block(q_ref[h], kp_ref[h], vp_ref[h], m_sc[h], acc[h], None)
            m_sc[h] = m
            acc[h] = a
            return 0
        lax.fori_loop(0, heads, body, 0, unroll=min(heads, MLA_UNROLL))

    @pl.when(is_last)
    def _():
        for h in range(heads):
            a = acc[h]
            o_ref[:, h * V_DIM:(h + 1) * V_DIM] = (a[:, :V_DIM] / a[:, V_DIM:]).astype(o_ref.dtype)


def _mla_attention(qc, kcn, vn, kcp, vp, tables, *, tq, tk, pos0, out_rows, out_block_of):
    qb, kb, fl = tables
    n_steps = qb.shape[0]
    h = H_MLA
    wqk = 2 * LANES
    new = pl.BlockSpec((h, tq, wqk), lambda s, qb, kb, fl: (0, qb[s], 0))
    past = pl.BlockSpec((h, tk, wqk), lambda s, qb, kb, fl: (0, kb[s], 0))
    grid_spec = pltpu.PrefetchScalarGridSpec(
        num_scalar_prefetch=3,
        grid=(n_steps,),
        in_specs=[new, new, new, past, past],
        out_specs=pl.BlockSpec((tq, h * V_DIM), out_block_of),
        scratch_shapes=[pltpu.VMEM((h, tq, 2 * V_DIM), F32),
                        pltpu.VMEM((h, tq, LANES), F32)],
    )
    kern = functools.partial(_mla_kernel, heads=h, pos0=pos0)
    return pl.pallas_call(
        kern, grid_spec=grid_spec,
        out_shape=jax.ShapeDtypeStruct((out_rows, h * V_DIM), BF16),
        compiler_params=_cparams(1),
    )(qb, kb, fl, qc, kcn, vn, kcp, vp)


def _causal_tables(nq, q_off, per_head=False, heads=1, n_batch=1, past_tiles=None,
                   past_stride=0):
    qb, kb, hb, fl = [], [], [], []
    if past_tiles is None:
        for i in range(nq):
            n = i + 1
            for j in range(n):
                qb.append(q_off + i)
                kb.append(max(i - j, 1) - 1 if j == 0 else i - j)
                hb.append(0)
                fl.append((1 if j == 0 else 0) | (2 if j == n - 1 else 0))
    else:
        for b in range(n_batch):
            for h in range(heads if per_head else 1):
                n = 1 + past_tiles
                for j in range(n):
                    qb.append(q_off + b)
                    jj = past_tiles - 1 if j == 0 else past_tiles - j
                    kb.append(past_stride + b * past_tiles + jj)
                    hb.append(h)
                    fl.append((1 if j == 0 else 0) | (2 if j == n - 1 else 0))
    arr = lambda v: jnp.asarray(np.asarray(v, dtype=np.int32))
    return arr(qb), arr(kb), arr(hb), arr(fl)


ROW_DMA_UNROLL = 8


def _moe_kernel(te_ref, nv_ref, tok_ref, dst_ref, h_hbm, w_ref, wg_ref, wu_ref, wd_ref, y_hbm,
                xbuf, obuf, wg_s, wu_s, wd_s, sem_in, sem_out, *, s):
    t = pl.program_id(0)
    tm = w_ref.shape[0]
    nv = nv_ref[0]
    live = t < nv
    slot = lax.rem(t, 2)

    p = _slab_pitch(s)

    def in_copy(tile, r, sl):
        src0 = pl.multiple_of(tok_ref[tile * tm + r] * p, 8)
        return pltpu.make_async_copy(h_hbm.at[pl.ds(src0, s)],
                                     xbuf.at[sl, pl.ds(pl.multiple_of(r * p, 8), s)],
                                     sem_in.at[sl])

    def gather_start(tile, sl):
        def body(r, _):
            in_copy(tile, r, sl).start()
            return 0
        lax.fori_loop(0, tm, body, 0, unroll=ROW_DMA_UNROLL)

    def gather_wait(tile, sl):
        def body(r, _):
            in_copy(tile, r, sl).wait()
            return 0
        lax.fori_loop(0, tm, body, 0, unroll=ROW_DMA_UNROLL)

    @pl.when(live & (t == 0))
    def _():
        gather_start(0, 0)

    @pl.when(t + 1 < nv)
    def _():
        gather_start(t + 1, 1 - slot)

    prev = te_ref[jnp.maximum(t - 1, 0)]
    fresh = (t == 0) | (te_ref[t] != prev)

    @pl.when(live & fresh)
    def _():
        wg_s[...] = wg_ref[...].astype(BF16)
        wu_s[...] = wu_ref[...].astype(BF16)
        wd_s[...] = wd_ref[...].astype(BF16)

    @pl.when(live)
    def _():
        gather_wait(t, slot)
        x = _slab_load(xbuf, 0, tm, s, lead=slot).astype(BF16)
        a = jnp.dot(x, wg_s[...], preferred_element_type=F32)
        u = jnp.dot(x, wu_s[...], preferred_element_type=F32)
        hid = (a * jax.nn.sigmoid(a)) * u * w_ref[...]
        y = jnp.dot(hid.astype(BF16), wd_s[...], preferred_element_type=F32)

        def out_copy(tile, r):
            dst0 = pl.multiple_of(dst_ref[tile * tm + r] * p, 8)
            return pltpu.make_async_copy(obuf.at[pl.ds(pl.multiple_of(r * p, 8), s)],
                                         y_hbm.at[pl.ds(dst0, s)], sem_out)

        def scatter_wait(tile):
            def wait(r, _):
                out_copy(tile, r).wait()
                return 0
            lax.fori_loop(0, tm, wait, 0, unroll=ROW_DMA_UNROLL)

        @pl.when(t > 0)
        def _():
            scatter_wait(t - 1)

        _slab_store(obuf, 0, y)

        def start(r, _):
            out_copy(t, r).start()
            return 0
        lax.fori_loop(0, tm, start, 0, unroll=ROW_DMA_UNROLL)

        @pl.when(t == nv - 1)
        def _():
            scatter_wait(t)


def _moe_experts(h_slab, row_tok, row_dst, row_w, tile_expert, n_valid, w_gate, w_up, w_down, l,
                 n_dest):
    d, f = w_gate.shape[-2:]
    s = d // LANES
    p = _slab_pitch(s)
    tm = MOE_TILE
    n_tiles = row_tok.shape[0] // tm
    wmap = lambda t, te, nv, tok, dst: (l, te[t], 0, 0)
    grid_spec = pltpu.PrefetchScalarGridSpec(
        num_scalar_prefetch=4, grid=(n_tiles,),
        in_specs=[pl.BlockSpec(memory_space=pl.ANY),
                  pl.BlockSpec((tm, 1), lambda t, te, nv, tok, dst: (t, 0)),
                  pl.BlockSpec((None, None, d, f), wmap),
                  pl.BlockSpec((None, None, d, f), wmap),
                  pl.BlockSpec((None, None, f, d), wmap)],
        out_specs=pl.BlockSpec(memory_space=pl.ANY),
        scratch_shapes=[pltpu.VMEM((2, tm * p, LANES), F32), pltpu.VMEM((tm * p, LANES), F32),
                        pltpu.VMEM((d, f), BF16), pltpu.VMEM((d, f), BF16),
                        pltpu.VMEM((f, d), BF16),
                        pltpu.SemaphoreType.DMA((2,)), pltpu.SemaphoreType.DMA(())],
    )
    return pl.pallas_call(
        functools.partial(_moe_kernel, s=s), grid_spec=grid_spec,
        out_shape=jax.ShapeDtypeStruct(((n_dest + tm) * p, LANES), F32),
        compiler_params=_cparams(1),
    )(tile_expert, n_valid, row_tok, row_dst, h_slab, row_w, w_gate, w_up, w_down)


def _combine_kernel(x_ref, y0_ref, y1_ref, g_ref, o_ref):
    gb, grp, d = x_ref.shape
    s = d // LANES
    for g in range(gb):
        y = _slab_load(y0_ref, g * grp, grp, s) + _slab_load(y1_ref, g * grp, grp, s)
        o_ref[g] = x_ref[g] + g_ref[g] * y


def _combine(x3, yg, modg, l, g_i):
    ng, grp, d = x3.shape
    s = d // LANES
    gb = _row_tile(ng, 2)
    blk = pl.BlockSpec((gb, grp, d), lambda i: (i, 0, 0))
    yblk = lambda off: pl.BlockSpec((gb * grp * _slab_pitch(s), LANES), lambda i: (i + off, 0))
    return pl.pallas_call(
        _combine_kernel,
        grid=(ng // gb,),
        in_specs=[blk, yblk(0), yblk(ng // gb), _mod_spec(g_i, gb, d)],
        out_specs=blk,
        out_shape=jax.ShapeDtypeStruct(x3.shape, F32),
        compiler_params=_cparams(1),
    )(x3, yg, yg, modg)


def _dispatch(route, n):
    tm = MOE_TILE
    r_cap = (2 * n + N_EXPERTS * (tm - 1) + tm - 1) // tm * tm
    e = route[:, :2].astype(jnp.int32)
    w = route[:, 2:4]
    flat_e = e.T.reshape(-1)
    flat_w = w.T.reshape(-1)
    order = jnp.argsort(flat_e, stable=True).astype(jnp.int32)
    experts = jnp.arange(N_EXPERTS, dtype=jnp.int32)
    counts = jnp.sum((flat_e[:, None] == experts[None, :]).astype(jnp.int32), axis=0)
    padded = (counts + tm - 1) // tm * tm
    ends_p = jnp.cumsum(padded)
    starts_p = ends_p - padded
    starts = jnp.cumsum(counts) - counts
    tile_start = jnp.arange(r_cap // tm, dtype=jnp.int32) * tm
    tile_expert = jnp.minimum(jnp.searchsorted(ends_p, tile_start, side="right"),
                              N_EXPERTS - 1).astype(jnp.int32)
    n_valid = (ends_p[-1:] // tm).astype(jnp.int32)
    row_e = jnp.repeat(tile_expert, tm)
    local = jnp.arange(r_cap, dtype=jnp.int32) - starts_p[row_e]
    valid = local < counts[row_e]
    slot = order[jnp.clip(starts[row_e] + local, 0, 2 * n - 1)]
    row_dst = jnp.where(valid, slot, 2 * n + jnp.arange(r_cap, dtype=jnp.int32) % tm)
    row_tok = jnp.where(valid, jnp.where(slot >= n, slot - n, slot), 0)
    row_w = jnp.where(valid, flat_w[slot], 0.0)
    return row_tok, row_dst, row_w.reshape(r_cap, 1), tile_expert, n_valid


def _rope_tables(pos):
    inv = ROPE_THETA ** (-jnp.arange(0, QK_ROPE, 2, dtype=F32) / QK_ROPE)
    ang = pos.astype(F32)[:, None] * inv[None, :]
    cos, sin = jnp.cos(ang), jnp.sin(ang)
    pad = jnp.zeros((pos.shape[0], LANES - QK_ROPE), F32)
    return (jnp.concatenate([cos, cos, pad], axis=1),
            jnp.concatenate([-sin, sin, pad], axis=1))


def _rope_lanes(acc, c, s):
    return acc * c + pltpu.roll(acc, LANES - QK_ROPE, axis=1) * s


def _swap_halves(w):
    half = w.shape[-1] // 2
    return jnp.concatenate([w[..., half:], w[..., :half]], axis=-1)


def kernel(x_prompt, x_sample, c_prompt, c_sample, cache_sb_k, cache_sb_v, cache_mla_ckv,
           cache_mla_krope, w_ada, b_ada, g_norm_mix, g_norm_ffn, w_in, g_q_lat, g_kv_lat,
           w_uq, w_ukv, w_branch_sb, w_branch_mla, w_out, w_router_group, b_router_group,
           w_router_expert, b_router_expert, w_exp_gate, w_exp_up, w_exp_down, g_final):
    bp, t_p, d = x_prompt.shape
    bs, t_s, _ = x_sample.shape
    depth = w_in.shape[0]
    past = cache_sb_k.shape[2]
    grp = t_s
    n_p, n_s = bp * t_p, bs * t_s
    n = n_p + n_s
    ng = n // grp
    sb_w = H_SB * DH_SB
    q_lora = g_q_lat.shape[1]
    kv_lora = g_kv_lat.shape[1]
    tm = _token_tile(n, grp)
    gpt = tm // grp
    tn = 512
    assert bp == 1 and t_p % ATTN_TILE == 0 and t_p % grp == 0 and past % 512 == 0

    x3 = jnp.concatenate([x_prompt.reshape(n_p // grp, grp, d), x_sample], axis=0)

    n_c = bp + bs
    c_rows = 16
    c_all = jnp.zeros((c_rows, d), F32).at[:n_c].set(jnp.concatenate([c_prompt, c_sample], 0))
    n_modc = N_MOD * d

    def ada_epi(accs, ex, outs):
        outs[0][...] = accs[0] + ex[0][...]

    mods = []
    for l in range(depth):
        mods.append(_matmul(
            [(c_all, w_ada, _wspec(l, d, tn, 0), True)], m=c_rows, n_out=n_modc, tm=c_rows, tn=tn,
            prologue=lambda a, ex: a * jax.nn.sigmoid(a),
            epilogue=ada_epi,
            extras=[b_ada.reshape(depth, 1, n_modc)],
            extra_specs=[pl.BlockSpec((None, 1, tn), lambda j, i, l=l: (l, 0, j))],
            out_shape=[jax.ShapeDtypeStruct((c_rows, n_modc), F32)],
            out_specs=[_spec2(c_rows, tn)])[0])
    mod = jnp.stack(mods)
    modg = jnp.concatenate(
        [jnp.broadcast_to(mod[:, :bp], (depth, n_p // grp, n_modc)), mod[:, bp:n_c]], axis=1)
    modg = modg.reshape(depth, ng, N_MOD, d).transpose(0, 2, 1, 3)
    modg = modg.reshape(depth * N_MOD, ng, 1, d)

    def mod_idx(l, k):
        return l * N_MOD + k

    def mspec_rows(l, k):
        return pl.BlockSpec((None, gpt, 1, tn), lambda j, i: (mod_idx(l, k), i, 0, j))

    pos = jnp.concatenate([jnp.arange(t_p, dtype=jnp.int32),
                           jnp.tile(past + jnp.arange(t_s, dtype=jnp.int32), bs)])
    rope_c, rope_s = _rope_tables(pos)
    tq = ATTN_TILE
    nq_p = t_p // tq
    tab_p = _causal_tables(nq_p, 0)
    tk_s = 512
    pt = past // tk_s
    tab_sb_s = lambda l: _causal_tables(0, n_p // t_s, n_batch=bs, past_tiles=pt,
                                        past_stride=l * bs * pt)
    tab_mla_s = _causal_tables(0, n_p // t_s, n_batch=bs, past_tiles=pt)

    off_q, off_k, off_v = 0, sb_w, 2 * sb_w
    off_cq = 3 * sb_w
    off_ckv = off_cq + q_lora
    off_kr = off_ckv + kv_lora
    off_g = off_kr + QK_ROPE
    w_kr = w_in[:, :, off_kr:off_kr + QK_ROPE]
    w_kr_aug = jnp.concatenate([w_kr, _swap_halves(w_kr)], axis=-1).astype(BF16)
    w_gates = w_in[:, :, off_g:].astype(BF16)
    uq = w_uq.reshape(depth, q_lora, H_MLA, QK_NOPE + QK_ROPE)
    uq_r = uq[..., QK_NOPE:]
    w_uq_cat = jnp.concatenate([uq[..., :QK_NOPE], uq_r, _swap_halves(uq_r)], axis=-1)
    w_uq_cat = w_uq_cat.reshape(depth, q_lora, H_MLA * 2 * LANES)
    w_router = jnp.concatenate(
        [w_router_expert, w_router_group,
         jnp.zeros((depth, d, LANES - N_EXPERTS - N_GROUPS), F32)], axis=-1)
    r_hi = w_router.astype(BF16)
    r_res = w_router - r_hi.astype(F32)
    r_mid = r_res.astype(BF16)
    r_lo = (r_res - r_mid.astype(F32)).astype(BF16)
    w_router3 = jnp.stack([r_hi, r_mid, r_lo], axis=1)
    b_router = jnp.concatenate(
        [b_router_expert, b_router_group,
         jnp.zeros((depth, LANES - N_EXPERTS - N_GROUPS), F32)], axis=-1).reshape(depth, 1, LANES)

    hm = lambda rows: jax.ShapeDtypeStruct((H_SB, rows, LANES), BF16)
    new_k, new_v, new_c, new_r = [], [], [], []

    for l in range(depth):
        h = _norm_mod(x3, g_norm_mix.reshape(depth, 1, d), modg, l, mod_idx(l, 1), mod_idx(l, 0))
        h = h.reshape(n, d)

        def plain_hm(accs, ex, outs):
            _store_heads(outs[0], accs[0])

        def f32_and_hm(accs, ex, outs):
            outs[0][...] = accs[0]
            _store_heads(outs[1], accs[0])

        sb_q = _matmul([(h, w_in, _wspec(l, d, tn, off_q), True)], m=n, n_out=sb_w, tm=tm, tn=tn,
                       epilogue=plain_hm, out_shape=[hm(n)], out_specs=[_hm_spec(tm, tn)])[0]
        k_f32, sb_k = _matmul([(h, w_in, _wspec(l, d, tn, off_k), True)], m=n, n_out=sb_w, tm=tm,
                              tn=tn, epilogue=f32_and_hm,
                              out_shape=[jax.ShapeDtypeStruct((n, sb_w), F32), hm(n)],
                              out_specs=[_spec2(tm, tn), _hm_spec(tm, tn)])
        v_f32, sb_v = _matmul([(h, w_in, _wspec(l, d, tn, off_v), True)], m=n, n_out=sb_w, tm=tm,
                              tn=tn, epilogue=f32_and_hm,
                              out_shape=[jax.ShapeDtypeStruct((n, sb_w), F32), hm(n)],
                              out_specs=[_spec2(tm, tn), _hm_spec(tm, tn)])

        def plain_f32(accs, ex, outs):
            outs[0][...] = accs[0]

        c_q = _matmul([(h, w_in, _wspec(l, d, tn, off_cq), True)], m=n, n_out=q_lora, tm=tm, tn=tn,
                      epilogue=plain_f32, out_shape=[jax.ShapeDtypeStruct((n, q_lora), F32)],
                      out_specs=[_spec2(tm, tn)])[0]

        def ckv_epi(accs, ex, outs):
            y = _rms(accs[0], ex[0][...])
            outs[0][...] = y
            outs[1][...] = y.astype(BF16)

        c_kv, c_kv_b = _matmul(
            [(h, w_in, _wspec(l, d, kv_lora, off_ckv), True)], m=n, n_out=kv_lora, tm=tm, tn=kv_lora,
            epilogue=ckv_epi, extras=[g_kv_lat.reshape(depth, 1, kv_lora)],
            extra_specs=[pl.BlockSpec((None, 1, kv_lora), lambda j, i: (l, 0, 0))],
            out_shape=[jax.ShapeDtypeStruct((n, kv_lora), F32),
                       jax.ShapeDtypeStruct((n, kv_lora), BF16)],
            out_specs=[_spec2(tm, kv_lora), _spec2(tm, kv_lora)])

        def kr_epi(accs, ex, outs):
            r = _rope_lanes(accs[0], ex[0][...], ex[1][...])
            outs[0][...] = r
            outs[1][...] = r.astype(BF16)

        rope_specs = [pl.BlockSpec((tm, LANES), lambda j, i: (i, 0))] * 2
        k_r, k_r_b = _matmul(
            [(h, w_kr_aug, pl.BlockSpec((None, d, LANES), lambda j, i: (l, 0, 0)), False)],
            m=n, n_out=LANES, tm=tm, tn=LANES, epilogue=kr_epi,
            extras=[rope_c, rope_s], extra_specs=rope_specs,
            out_shape=[jax.ShapeDtypeStruct((n, LANES), F32), jax.ShapeDtypeStruct((n, LANES), BF16)],
            out_specs=[_spec2(tm, LANES), _spec2(tm, LANES)])

        def gate_epi(accs, ex, outs):
            outs[0][...] = jax.nn.sigmoid(accs[0])

        gates = _matmul(
            [(h, w_gates, pl.BlockSpec((None, d, tn), lambda j, i: (l, 0, j)), False)],
            m=n, n_out=2 * d, tm=tm, tn=tn, epilogue=gate_epi,
            out_shape=[jax.ShapeDtypeStruct((n, 2 * d), F32)], out_specs=[_spec2(tm, tn)])[0]

        def cq_prologue(a, ex):
            return _rms(a, ex[0][...])

        gq_spec = pl.BlockSpec((None, 1, q_lora), lambda j, i: (l, 0, 0))
        gq = g_q_lat.reshape(depth, 1, q_lora)
        wqk = 2 * LANES
        hm_qk = lambda rows: jax.ShapeDtypeStruct((H_MLA, rows, wqk), BF16)

        q_scale = (QK_NOPE + QK_ROPE) ** -0.5 * float(np.log2(np.e))

        def qcat_epi(accs, ex, outs):
            c, s = ex[1][...], ex[2][...]
            for hh in range(tn // wqk):
                blk = accs[0][:, hh * wqk:(hh + 1) * wqk]
                outs[0][hh, :, :LANES] = (blk[:, :LANES] * q_scale).astype(BF16)
                outs[0][hh, :, LANES:] = (_rope_lanes(blk[:, LANES:], c, s) * q_scale).astype(BF16)

        q_cat = _matmul(
            [(c_q, w_uq_cat, pl.BlockSpec((None, q_lora, tn), lambda j, i: (l, 0, j)), True)],
            m=n, n_out=H_MLA * wqk, tm=tm, tn=tn, prologue=cq_prologue, epilogue=qcat_epi,
            extras=[gq, rope_c, rope_s], extra_specs=[gq_spec] + rope_specs,
            out_shape=[hm_qk(n)],
            out_specs=[pl.BlockSpec((tn // wqk, tm, wqk), lambda j, i: (j, i, 0))])[0]

        tn_kv = min(2048, H_MLA * wqk)
        hpt = tn_kv // wqk

        def kv_epi(accs, ex, outs):
            kr = ex[0][...]
            ones = jnp.ones(kr.shape, BF16)
            for hh in range(hpt):
                outs[0][hh, :, :LANES] = accs[0][:, hh * wqk:hh * wqk + LANES].astype(BF16)
                outs[0][hh, :, LANES:] = kr
                outs[1][hh, :, :LANES] = accs[0][:, hh * wqk + LANES:(hh + 1) * wqk].astype(BF16)
                outs[1][hh, :, LANES:] = ones

        def up_kv(a, kr, rows, tmr):
            hspec = pl.BlockSpec((hpt, tmr, wqk), lambda j, i: (j, i, 0))
            return _matmul(
                [(a, w_ukv, pl.BlockSpec((None, kv_lora, tn_kv), lambda j, i: (l, 0, j)), True)],
                m=rows, n_out=H_MLA * wqk, tm=tmr, tn=tn_kv, epilogue=kv_epi, extras=[kr],
                extra_specs=[pl.BlockSpec((tmr, LANES), lambda j, i: (i, 0))],
                out_shape=[hm_qk(rows), hm_qk(rows)], out_specs=[hspec, hspec])

        kc_new, v_new = up_kv(c_kv_b, k_r_b, n, tm)
        rows_past = bs * past
        kr_past = jnp.pad(cache_mla_krope[l].reshape(rows_past, QK_ROPE),
                          ((0, 0), (0, LANES - QK_ROPE))).astype(BF16)
        kc_past, v_past = up_kv(cache_mla_ckv[l].reshape(rows_past, kv_lora), kr_past, rows_past,
                                512)

        o_sb_p = _sb_self_attention(sb_q, sb_k, sb_v, tq=tq, n_rows=n_p)
        cache_k2 = cache_sb_k.reshape(depth * bs * past * H_SB, DH_SB)
        cache_v2 = cache_sb_v.reshape(depth * bs * past * H_SB, DH_SB)
        q_off_s = n_p // t_s
        o_sb_s = _sb_attention(sb_q, sb_k, sb_v, cache_k2, cache_v2, tab_sb_s(l), heads=H_SB,
                               tq=t_s, tk=tk_s, sub=ATTN_TILE, out_rows=n_s,
                               out_block_of=lambda s, qb, kb, hb, fl: (qb[s] - q_off_s, 0))
        o_sb = jnp.concatenate([o_sb_p, o_sb_s], axis=0)

        tp3 = (tab_p[0], tab_p[1], tab_p[3])
        o_mla_p = _mla_attention(q_cat, kc_new, v_new, kc_new, v_new, tp3,
                                 tq=tq, tk=tq, pos0=0, out_rows=n_p,
                                 out_block_of=lambda s, qb, kb, fl: (qb[s], 0))
        ts3 = (tab_mla_s[0], tab_mla_s[1], tab_mla_s[3])
        o_mla_s = _mla_attention(q_cat, kc_new, v_new, kc_past, v_past, ts3,
                                 tq=t_s, tk=tk_s, pos0=past, out_rows=n_s,
                                 out_block_of=lambda s, qb, kb, fl: (qb[s] - q_off_s, 0))
        o_mla = jnp.concatenate([o_mla_p, o_mla_s], axis=0)

        def merge_epi(accs, ex, outs):
            outs[0][...] = (ex[0][...] * accs[0] + ex[1][...] * accs[1]).astype(BF16)

        merged = _matmul(
            [(o_sb, w_branch_sb, _wspec(l, sb_w, tn, 0), True),
             (o_mla, w_branch_mla, _wspec(l, H_MLA * V_DIM, tn, 0), True)],
            m=n, n_out=d, tm=tm, tn=tn, epilogue=merge_epi,
            extras=[gates, gates],
            extra_specs=[pl.BlockSpec((tm, tn), lambda j, i: (i, j)),
                         pl.BlockSpec((tm, tn), lambda j, i: (i, j + d // tn))],
            out_shape=[jax.ShapeDtypeStruct((n, d), BF16)], out_specs=[_spec2(tm, tn)])[0]

        def resid_epi(accs, ex, outs):
            for g in range(gpt):
                outs[0][g] = ex[0][g] + ex[1][g] * accs[0][g * grp:(g + 1) * grp, :]

        x_spec = pl.BlockSpec((gpt, grp, tn), lambda j, i: (i, 0, j))
        x3 = _matmul(
            [(merged, w_out, _wspec(l, d, tn, 0), True)], m=n, n_out=d, tm=tm, tn=tn,
            epilogue=resid_epi, extras=[x3, modg], extra_specs=[x_spec, mspec_rows(l, 2)],
            out_shape=[jax.ShapeDtypeStruct((ng, grp, d), F32)], out_specs=[x_spec])[0]

        h2, route = _norm_route(x3, g_norm_ffn.reshape(depth, 1, d), modg, l, mod_idx(l, 4),
                                mod_idx(l, 3), w_router3, b_router)
        row_tok, row_dst, row_w, tile_expert, n_valid = _dispatch(route.reshape(n, LANES), n)
        yg = _moe_experts(h2, row_tok, row_dst, row_w, tile_expert, n_valid,
                          w_exp_gate, w_exp_up, w_exp_down, l, 2 * n)
        x3 = _combine(x3, yg, modg, l, mod_idx(l, 5))

        new_k.append(k_f32)
        new_v.append(v_f32)
        new_c.append(c_kv)
        new_r.append(k_r[:, :QK_ROPE])

    y_p, y_s = _final_norm(x3, g_final.reshape(1, d), n_p // grp)

    def split(parts, tail):
        a = jnp.stack(parts)
        return (a[:, :n_p].reshape((depth, bp, t_p) + tail),
                a[:, n_p:].reshape((depth, bs, t_s) + tail))

    pk, sk = split(new_k, (H_SB, DH_SB))
    pv, sv = split(new_v, (H_SB, DH_SB))
    pc, sc = split(new_c, (kv_lora,))
    pr, sr = split(new_r, (QK_ROPE,))
    return (y_p.reshape(bp, t_p, d), y_s.reshape(bs, t_s, d), pk, pv, pc, pr, sk, sv, sc, sr)
```

```python
import functools

import numpy as np
import jax
import jax.numpy as jnp
from jax import lax
from jax.experimental import pallas as pl
from jax.experimental.pallas import tpu as pltpu

F32 = jnp.float32
BF16 = jnp.bfloat16

CHUNK = 64
H_SB = 16
DH_SB = 128
H_MLA = 16
QK_NOPE = 128
QK_ROPE = 64
V_DIM = 128
ROPE_THETA = 10000.0
N_GROUPS = 4
EXPERTS_PER_GROUP = 8
N_EXPERTS = N_GROUPS * EXPERTS_PER_GROUP
N_MOD = 6
EPS = 1e-6

LANES = 128
ATTN_TILE = 256
MOE_TILE = 256
HEAD_UNROLL = 4
MLA_UNROLL = 16
SB_DEAD = 104.0
VMEM_LIMIT = 56 * 1024 * 1024


def _cparams(n_axes, vmem=VMEM_LIMIT):
    return pltpu.CompilerParams(dimension_semantics=("arbitrary",) * n_axes,
                                vmem_limit_bytes=vmem)


def _row_tile(n, cap=512):
    t = cap
    while n % t:
        t //= 2
    return t


def _token_tile(n, grp, cap=1152):
    return max(t for t in range(grp, cap + 1, grp) if n % t == 0)


def _matmul(pairs, *, m, n_out, tm, tn, epilogue, out_shape, out_specs,
            extras=(), extra_specs=(), prologue=None):
    n_pairs = len(pairs)
    n_ex = len(extras)
    n_outs = len(out_shape)
    cast = [p[3] for p in pairs]

    def kern(*refs):
        a_refs = refs[0:2 * n_pairs:2]
        b_refs = refs[1:2 * n_pairs:2]
        ex = refs[2 * n_pairs:2 * n_pairs + n_ex]
        outs = refs[2 * n_pairs + n_ex:2 * n_pairs + n_ex + n_outs]
        scr = refs[2 * n_pairs + n_ex + n_outs:]
        i = pl.program_id(1)
        accs = []
        si = 0
        for p in range(n_pairs):
            if cast[p]:
                bsc = scr[si]
                si += 1

                @pl.when(i == 0)
                def _(bsc=bsc, b_ref=b_refs[p]):
                    bsc[...] = b_ref[...].astype(BF16)

                bv = bsc[...]
            else:
                bv = b_refs[p][...]
            a = a_refs[p][...]
            if prologue is not None:
                a = prologue(a, ex)
            accs.append(jnp.dot(a.astype(BF16), bv, preferred_element_type=F32))
        epilogue(accs, ex, outs)

    in_specs, args, scratch = [], [], []
    for (a, b, b_spec, cb) in pairs:
        k = a.shape[1]
        in_specs += [pl.BlockSpec((tm, k), lambda j, i: (i, 0)), b_spec]
        args += [a, b]
        if cb:
            scratch.append(pltpu.VMEM((k, tn), BF16))
    in_specs += list(extra_specs)
    args += list(extras)
    return pl.pallas_call(
        kern,
        grid=(n_out // tn, m // tm),
        in_specs=in_specs,
        out_specs=out_specs,
        out_shape=out_shape,
        scratch_shapes=scratch,
        compiler_params=_cparams(2),
    )(*args)


def _wspec(l, k, tn, col_off):
    cb = col_off // tn
    assert cb * tn == col_off
    return pl.BlockSpec((None, k, tn), lambda j, i: (l, 0, cb + j))


def _spec2(tm, tn):
    return pl.BlockSpec((tm, tn), lambda j, i: (i, j))


def _hm_spec(tm, tn):
    return pl.BlockSpec((tn // LANES, tm, LANES), lambda j, i: (j, i, 0))


def _store_heads(o_ref, val):
    for c in range(val.shape[1] // LANES):
        o_ref[c] = val[:, c * LANES:(c + 1) * LANES].astype(o_ref.dtype)


def _rms(x, g):
    return x * lax.rsqrt(jnp.mean(x * x, axis=-1, keepdims=True) + EPS) * g


def _norm_mod_kernel(x_ref, g_ref, sc_ref, sh_ref, o_ref):
    x = x_ref[...]
    y = _rms(x, g_ref[...])
    o_ref[...] = (y * (1.0 + sc_ref[...]) + sh_ref[...]).astype(o_ref.dtype)


def _split3(x):
    hi = x.astype(BF16)
    r = x - hi.astype(F32)
    mid = r.astype(BF16)
    lo = (r - mid.astype(F32)).astype(BF16)
    return hi, mid, lo


def _dot_f32(a, b3):
    a_hi, a_mid, a_lo = _split3(a)
    b_hi, b_mid, b_lo = b3
    d = functools.partial(jnp.dot, preferred_element_type=F32)
    small = d(a_hi, b_lo) + d(a_lo, b_hi) + d(a_mid, b_mid)
    return (d(a_hi, b_hi) + (d(a_hi, b_mid) + d(a_mid, b_hi))) + small


def _route(logits):
    lane = lax.broadcasted_iota(jnp.int32, logits.shape, 1)
    lanef = lane.astype(F32)
    big = jnp.float32(1e9)
    ninf = jnp.float32(-jnp.inf)
    is_g = (lane >= N_EXPERTS) & (lane < N_EXPERTS + N_GROUPS)
    gl = jnp.where(is_g, logits, ninf)
    gmax = jnp.max(gl, axis=1, keepdims=True)
    g_idx = jnp.min(jnp.where(gl == gmax, lanef - N_EXPERTS, big), axis=1, keepdims=True)
    p_group = 1.0 / jnp.sum(jnp.where(is_g, jnp.exp(gl - gmax), 0.0), axis=1, keepdims=True)
    grp = jnp.floor(lanef * (1.0 / EXPERTS_PER_GROUP))
    in_g = (lane < N_EXPERTS) & (grp == g_idx)
    el = jnp.where(in_g, logits, ninf)
    e1 = jnp.max(el, axis=1, keepdims=True)
    i1 = jnp.min(jnp.where(el == e1, lanef, big), axis=1, keepdims=True)
    el2 = jnp.where(lanef == i1, ninf, el)
    e2 = jnp.max(el2, axis=1, keepdims=True)
    i2 = jnp.min(jnp.where(el2 == e2, lanef, big), axis=1, keepdims=True)
    t = jnp.exp(e2 - e1)
    den = 1.0 + t
    w1 = (1.0 / den) * p_group
    w2 = (t / den) * p_group
    out = jnp.where(lane == 0, i1, jnp.where(lane == 1, i2,
          jnp.where(lane == 2, w1, jnp.where(lane == 3, w2, 0.0))))
    return out


def _slab_pitch(s):
    return s + 8


def _slab_store(ref, row0, val):
    rows, d = val.shape
    s = d // LANES
    p = _slab_pitch(s)
    for c in range(s):
        ref[pl.ds(row0 * p + c, rows, stride=p), :] = val[:, c * LANES:(c + 1) * LANES]


def _slab_load(ref, row0, rows, s, lead=None):
    pieces = []
    p = _slab_pitch(s)
    for c in range(s):
        rs = pl.ds(row0 * p + c, rows, stride=p)
        pieces.append(ref[rs, :] if lead is None else ref[lead, rs, :])
    return jnp.concatenate(pieces, axis=1)


def _norm_route_kernel(x_ref, g_ref, sc_ref, sh_ref, wr_ref, br_ref, h_ref, r_ref):
    x = x_ref[...]
    y = _rms(x, g_ref[...])
    h = y * (1.0 + sc_ref[...]) + sh_ref[...]
    b3 = (wr_ref[0], wr_ref[1], wr_ref[2])
    grp = x.shape[1]
    for g in range(x.shape[0]):
        _slab_store(h_ref, g * grp, h[g])
        logits = _dot_f32(h[g], b3) + br_ref[...]
        r_ref[g] = _route(logits)


def _mod_spec(idx, gb, d):
    return pl.BlockSpec((None, gb, 1, d), lambda i: (idx, i, 0, 0))


def _norm_mod(x3, g, modg, l, sc_i, sh_i):
    ng, grp, d = x3.shape
    gb = _row_tile(ng, 4)
    return pl.pallas_call(
        _norm_mod_kernel,
        grid=(ng // gb,),
        in_specs=[pl.BlockSpec((gb, grp, d), lambda i: (i, 0, 0)),
                  pl.BlockSpec((None, 1, d), lambda i: (l, 0, 0)),
                  _mod_spec(sc_i, gb, d), _mod_spec(sh_i, gb, d)],
        out_specs=pl.BlockSpec((gb, grp, d), lambda i: (i, 0, 0)),
        out_shape=jax.ShapeDtypeStruct(x3.shape, BF16),
        compiler_params=_cparams(1),
    )(x3, g, modg, modg)


def _norm_route(x3, g, modg, l, sc_i, sh_i, wr3, br):
    ng, grp, d = x3.shape
    gb = _row_tile(ng, 4)
    return pl.pallas_call(
        _norm_route_kernel,
        grid=(ng // gb,),
        in_specs=[pl.BlockSpec((gb, grp, d), lambda i: (i, 0, 0)),
                  pl.BlockSpec((None, 1, d), lambda i: (l, 0, 0)),
                  _mod_spec(sc_i, gb, d), _mod_spec(sh_i, gb, d),
                  pl.BlockSpec((None, 3, d, LANES), lambda i: (l, 0, 0, 0)),
                  pl.BlockSpec((None, 1, LANES), lambda i: (l, 0, 0))],
        out_specs=[pl.BlockSpec((gb * grp * _slab_pitch(d // LANES), LANES), lambda i: (i, 0)),
                   pl.BlockSpec((gb, grp, LANES), lambda i: (i, 0, 0))],
        out_shape=[jax.ShapeDtypeStruct((ng * grp * _slab_pitch(d // LANES), LANES), F32),
                   jax.ShapeDtypeStruct((ng, grp, LANES), F32)],
        compiler_params=_cparams(1),
    )(x3, g, modg, modg, wr3, br)


def _final_norm_kernel(x_ref, g_ref, op_ref, os_ref, *, n_prompt_blocks):
    i = pl.program_id(0)
    y = _rms(x_ref[...], g_ref[...])

    @pl.when(i < n_prompt_blocks)
    def _():
        op_ref[...] = y

    @pl.when(i >= n_prompt_blocks)
    def _():
        os_ref[...] = y


def _final_norm(x3, g, ng_prompt):
    ng, grp, d = x3.shape
    gb = _row_tile(np.gcd(ng_prompt, ng - ng_prompt), 4)
    npb = ng_prompt // gb
    blk = lambda f: pl.BlockSpec((gb, grp, d), f)
    return pl.pallas_call(
        functools.partial(_final_norm_kernel, n_prompt_blocks=npb),
        grid=(ng // gb,),
        in_specs=[blk(lambda i: (i, 0, 0)), pl.BlockSpec((1, d), lambda i: (0, 0))],
        out_specs=[blk(lambda i: (jnp.minimum(i, npb - 1), 0, 0)),
                   blk(lambda i: (jnp.maximum(i - npb, 0), 0, 0))],
        out_shape=[jax.ShapeDtypeStruct((ng_prompt, grp, d), F32),
                   jax.ShapeDtypeStruct((ng - ng_prompt, grp, d), F32)],
        compiler_params=_cparams(1),
    )(x3, g)


def _lanes(c, w):
    if w % LANES == 0:
        return c if w == LANES else jnp.tile(c, (1, w // LANES))
    return c[:, :w]


def _dot_nt(a, b):
    return lax.dot_general(a, b, (((1,), (1,)), ((), ())), preferred_element_type=F32)


def _sb_block(qh, kh, vh, c, u, scale, masked):
    w = kh.shape[0]
    z = _dot_nt(qh, kh) * scale
    sp = jnp.maximum(z, 0.0) + jnp.log1p(jnp.exp(-jnp.abs(z)))
    if masked:
        row = lax.broadcasted_iota(jnp.int32, z.shape, 0)
        col = lax.broadcasted_iota(jnp.int32, z.shape, 1)
        valid = col < row
        sp = jnp.where(valid, sp, 0.0)
    hi = sp.astype(BF16)
    lo = (sp - hi.astype(F32)).astype(BF16)
    cs = jnp.dot(hi, u, preferred_element_type=F32) + jnp.dot(lo, u, preferred_element_type=F32)
    wgt = jnp.exp(z - sp - cs - _lanes(c, w))
    if masked:
        wgt = jnp.where(valid, wgt, 0.0)
    o = jnp.dot(wgt.astype(BF16), vh, preferred_element_type=F32)
    c_new = c + jnp.sum(sp, axis=1, keepdims=True)
    return o, c_new


def _sb_kernel(qb_ref, kb_ref, hb_ref, fl_ref, q_ref, kn_ref, vn_ref, kp_ref, vp_ref,
               ud_ref, up_ref, o_ref, acc, carry, done, *, heads, scale, sub, n_sub):
    s = pl.program_id(0)
    fl = fl_ref[s]
    is_first = (fl & 1) != 0
    is_last = (fl & 2) != 0

    def past_block(ref, h, j):
        if len(ref.shape) == 3:
            return ref[h, pl.ds(j * sub, sub), :].astype(BF16)
        return ref[pl.ds(j * sub * heads + h, sub, stride=heads), :].astype(BF16)

    @pl.when(is_first)
    def _():
        def body(h, _):
            o, c = _sb_block(q_ref[h], kn_ref[h], vn_ref[h], jnp.zeros(carry.shape[1:], F32),
                             ud_ref[...], scale, True)
            acc[h] = o
            carry[h] = c
            done[h] = 0
            return 0
        lax.fori_loop(0, heads, body, 0, unroll=min(heads, HEAD_UNROLL))

    @pl.when(jnp.logical_not(is_first))
    def _():
        def body(h, _):
            @pl.when(done[h] == 0)
            def _():
                qh = q_ref[h]
                c = carry[h]
                o_tot = acc[h]
                for j in reversed(range(n_sub)):
                    o, c = _sb_block(qh, past_block(kp_ref, h, j), past_block(vp_ref, h, j), c,
                                     up_ref[...], scale, False)
                    o_tot = o_tot + o
                acc[h] = o_tot
                carry[h] = c
                done[h] = (jnp.min(c) >= SB_DEAD).astype(jnp.int32)
            return 0
        lax.fori_loop(0, heads, body, 0, unroll=min(heads, HEAD_UNROLL))

    @pl.when(is_last)
    def _():
        for h in range(heads):
            o_ref[:, h * DH_SB:(h + 1) * DH_SB] = acc[h].astype(o_ref.dtype)


def _suffix_matrix(w):
    j = np.arange(w)[:, None]
    s = np.arange(w)[None, :]
    return jnp.asarray((j > s).astype(np.float32), dtype=BF16)


def _sb_self_kernel(q_ref, kn_ref, vn_ref, k_hbm, v_hbm, ud_ref, up_ref, o_ref,
                    kbuf, vbuf, acc, carry, done, alive_ref, sem, *, heads, scale, tk):
    i = pl.program_id(0)

    def fetch(j, slot):
        rows = pl.ds(pl.multiple_of(j * tk, tk), tk)
        return (pltpu.make_async_copy(k_hbm.at[:, rows, :], kbuf.at[slot], sem.at[0, slot]),
                pltpu.make_async_copy(v_hbm.at[:, rows, :], vbuf.at[slot], sem.at[1, slot]))

    @pl.when(i > 0)
    def _():
        for c in fetch(i - 1, 0):
            c.start()

    def diag(h, _):
        o, c = _sb_block(q_ref[h], kn_ref[h], vn_ref[h], jnp.zeros(carry.shape[1:], F32),
                         ud_ref[...], scale, True)
        acc[h] = o
        carry[h] = c
        done[h] = 0
        return 0
    lax.fori_loop(0, heads, diag, 0, unroll=min(heads, HEAD_UNROLL))

    alive_ref[0] = heads

    @pl.when(i > 0)
    def _():
        for c in fetch(i - 1, 0):
            c.wait()

        @pl.when(i > 1)
        def _():
            for c in fetch(i - 2, 1):
                c.start()

        def first(h, alive):
            o, c = _sb_block(q_ref[h], kbuf[0, h], vbuf[0, h], carry[h], up_ref[...], scale, False)
            acc[h] = acc[h] + o
            carry[h] = c
            dead = (jnp.min(c) >= SB_DEAD).astype(jnp.int32)
            done[h] = dead
            return alive + 1 - dead
        alive_ref[0] = lax.fori_loop(0, heads, first, 0, unroll=min(heads, HEAD_UNROLL))

    def cond(state):
        j, alive = state
        return (j >= 0) & (alive > 0)

    def body(state):
        j, _ = state
        slot = lax.rem(i - 1 - j, 2)
        for c in fetch(j, slot):
            c.wait()

        @pl.when(j > 0)
        def _():
            for c in fetch(j - 1, 1 - slot):
                c.start()

        def head(h, alive):
            @pl.when(done[h] == 0)
            def _():
                o, c = _sb_block(q_ref[h], kbuf[slot, h], vbuf[slot, h], carry[h], up_ref[...],
                                 scale, False)
                acc[h] = acc[h] + o
                carry[h] = c
                done[h] = (jnp.min(c) >= SB_DEAD).astype(jnp.int32)
            return alive + 1 - done[h]
        return j - 1, lax.fori_loop(0, heads, head, 0)

    j_end, _ = lax.while_loop(cond, body, (i - 2, alive_ref[0]))

    @pl.when(j_end >= 0)
    def _():
        for c in fetch(j_end, lax.rem(i - 1 - j_end, 2)):
            c.wait()

    for h in range(heads):
        o_ref[:, h * DH_SB:(h + 1) * DH_SB] = acc[h].astype(o_ref.dtype)


def _sb_self_attention(q, k, v, *, tq, n_rows):
    heads = q.shape[0]
    blk = pl.BlockSpec((heads, tq, DH_SB), lambda i: (0, i, 0))
    umat = pl.BlockSpec((tq, tq), lambda i: (0, 0))
    anyspec = pl.BlockSpec(memory_space=pl.ANY)
    kern = functools.partial(_sb_self_kernel, heads=heads, scale=DH_SB ** -0.5, tk=tq)
    return pl.pallas_call(
        kern, grid=(n_rows // tq,),
        in_specs=[blk, blk, blk, anyspec, anyspec, umat, umat],
        out_specs=pl.BlockSpec((tq, heads * DH_SB), lambda i: (i, 0)),
        out_shape=jax.ShapeDtypeStruct((n_rows, heads * DH_SB), BF16),
        scratch_shapes=[pltpu.VMEM((2, heads, tq, DH_SB), BF16),
                        pltpu.VMEM((2, heads, tq, DH_SB), BF16),
                        pltpu.VMEM((heads, tq, DH_SB), F32),
                        pltpu.VMEM((heads, tq, LANES), F32),
                        pltpu.SMEM((heads,), jnp.int32),
                        pltpu.SMEM((1,), jnp.int32),
                        pltpu.SemaphoreType.DMA((2, 2))],
        compiler_params=_cparams(1),
    )(q, k, v, k, v, _suffix_matrix(tq), _suffix_matrix(tq))


def _sb_attention(q, kn, vn, kp, vp, tables, *, heads, tq, tk, sub, out_rows, out_block_of):
    qb, kb, hb, fl = tables
    n_steps = qb.shape[0]
    hblk = heads

    def qmap(s, qb, kb, hb, fl):
        return (0, qb[s], 0)

    if kp.ndim == 3:
        past_spec = pl.BlockSpec((hblk, tk, DH_SB), lambda s, qb, kb, hb, fl: (0, kb[s], 0))
    else:
        past_spec = pl.BlockSpec((tk * hblk, DH_SB), lambda s, qb, kb, hb, fl: (kb[s], 0))
    new_spec = pl.BlockSpec((hblk, tq, DH_SB), qmap)
    const = lambda s, qb, kb, hb, fl: (0, 0)
    grid_spec = pltpu.PrefetchScalarGridSpec(
        num_scalar_prefetch=4,
        grid=(n_steps,),
        in_specs=[new_spec, new_spec, new_spec, past_spec, past_spec,
                  pl.BlockSpec((tq, tq), const), pl.BlockSpec((sub, sub), const)],
        out_specs=pl.BlockSpec((tq, hblk * DH_SB), out_block_of),
        scratch_shapes=[pltpu.VMEM((hblk, tq, DH_SB), F32),
                        pltpu.VMEM((hblk, tq, LANES), F32),
                        pltpu.SMEM((hblk,), jnp.int32)],
    )
    kern = functools.partial(_sb_kernel, heads=hblk, scale=DH_SB ** -0.5, sub=sub,
                             n_sub=tk // sub)
    return pl.pallas_call(
        kern, grid_spec=grid_spec,
        out_shape=jax.ShapeDtypeStruct((out_rows, H_SB * DH_SB), BF16),
        compiler_params=_cparams(1),
    )(qb, kb, hb, fl, q, kn, vn, kp, vp, _suffix_matrix(tq), _suffix_matrix(sub))


def _mla_block(qc, kc, va, m, acc, mask):
    s = _dot_nt(qc, kc)
    if mask is not None:
        s = jnp.where(mask, s, -jnp.inf)
    m_new = jnp.maximum(m, jnp.max(s, axis=1, keepdims=True))
    alpha = jnp.exp2(m - m_new)
    p = jnp.exp2(s - _lanes(m_new, s.shape[1]))
    acc_new = _lanes(alpha, acc.shape[1]) * acc + jnp.dot(p.astype(BF16), va,
                                                          preferred_element_type=F32)
    return m_new, acc_new


def _mla_kernel(qb_ref, kb_ref, fl_ref, q_ref, kn_ref, vn_ref, kp_ref, vp_ref, o_ref,
                acc, m_sc, *, heads, pos0):
    s = pl.program_id(0)
    fl = fl_ref[s]
    is_first = (fl & 1) != 0
    is_last = (fl & 2) != 0
    tq = q_ref.shape[1]

    @pl.when(is_first)
    def _():
        row = lax.broadcasted_iota(jnp.int32, (tq, tq), 0) + pos0
        col = lax.broadcasted_iota(jnp.int32, (tq, tq), 1) + pos0
        mask = (col // CHUNK) <= (row // CHUNK)

        def body(h, _):
            m0 = jnp.full((tq, LANES), -jnp.inf, F32)
            a0 = jnp.zeros((tq, 2 * V_DIM), F32)
            m, a = _mla_block(q_ref[h], kn_ref[h], vn_ref[h], m0, a0, mask)
            m_sc[h] = m
            acc[h] = a
            return 0
        lax.fori_loop(0, heads, body, 0, unroll=min(heads, MLA_UNROLL))

    @pl.when(jnp.logical_not(is_first))
    def _():
        def body(h, _):
            m, a = _mla_block(q_ref[h], kp_ref[h], vp_ref[h], m_sc[h], acc[h], None)
            m_sc[h] = m
            acc[h] = a
            return 0
        lax.fori_loop(0, heads, body, 0, unroll=min(heads, MLA_UNROLL))

    @pl.when(is_last)
    def _():
        for h in range(heads):
            a = acc[h]
            o_ref[:, h * V_DIM:(h + 1) * V_DIM] = (a[:, :V_DIM] / a[:, V_DIM:]).astype(o_ref.dtype)


def _mla_attention(qc, kcn, vn, kcp, vp, tables, *, tq, tk, pos0, out_rows, out_block_of):
    qb, kb, fl = tables
    n_steps = qb.shape[0]
    h = H_MLA
    wqk = 2 * LANES
    new = pl.BlockSpec((h, tq, wqk), lambda s, qb, kb, fl: (0, qb[s], 0))
    past = pl.BlockSpec((h, tk, wqk), lambda s, qb, kb, fl: (0, kb[s], 0))
    grid_spec = pltpu.PrefetchScalarGridSpec(
        num_scalar_prefetch=3,
        grid=(n_steps,),
        in_specs=[new, new, new, past, past],
        out_specs=pl.BlockSpec((tq, h * V_DIM), out_block_of),
        scratch_shapes=[pltpu.VMEM((h, tq, 2 * V_DIM), F32),
                        pltpu.VMEM((h, tq, LANES), F32)],
    )
    kern = functools.partial(_mla_kernel, heads=h, pos0=pos0)
    return pl.pallas_call(
        kern, grid_spec=grid_spec,
        out_shape=jax.ShapeDtypeStruct((out_rows, h * V_DIM), BF16),
        compiler_params=_cparams(1),
    )(qb, kb, fl, qc, kcn, vn, kcp, vp)


def _mla_latent_kernel(qb_ref, kb_ref, fl_ref, q_ref, cn_ref, rn_ref, cp_ref, rp_ref, w_ref,
                       o_ref, qa, qr, acc, m_sc, l_sc, *, heads, pos0):
    s_id = pl.program_id(0)
    fl = fl_ref[s_id]
    is_first = (fl & 1) != 0
    is_last = (fl & 2) != 0
    tq = q_ref.shape[1]
    hw = QK_NOPE + V_DIM

    def step(ck, kr, mask):
        s = _dot_nt(qa[...], ck) + _dot_nt(qr[...], kr)
        if mask is not None:
            s = jnp.where(mask, s, -jnp.inf)
        m_old = m_sc[...]
        m_new = jnp.maximum(m_old, jnp.max(s, axis=1, keepdims=True))
        alpha = jnp.exp2(m_old - m_new)
        p = jnp.exp2(s - m_new[:, :1])
        l_sc[...] = alpha * l_sc[...] + jnp.sum(p, axis=1, keepdims=True)
        acc[...] = alpha[:, :1] * acc[...] + jnp.dot(p.astype(BF16), ck,
                                                     preferred_element_type=F32)
        m_sc[...] = m_new

    @pl.when(is_first)
    def _():
        for h in range(heads):
            qh = q_ref[h]
            w_uk = w_ref[:, h * hw:h * hw + QK_NOPE]
            qa[h * tq:(h + 1) * tq, :] = _dot_nt(qh[:, :QK_NOPE], w_uk).astype(BF16)
            qr[h * tq:(h + 1) * tq, :] = qh[:, QK_NOPE:]
        m_sc[...] = jnp.full(m_sc.shape, -jnp.inf, F32)
        l_sc[...] = jnp.zeros(l_sc.shape, F32)
        acc[...] = jnp.zeros(acc.shape, F32)
        row = lax.broadcasted_iota(jnp.int32, (heads * tq, tq), 0) % tq + pos0
        col = lax.broadcasted_iota(jnp.int32, (heads * tq, tq), 1) + pos0
        step(cn_ref[...], rn_ref[...], (col // CHUNK) <= (row // CHUNK))

    @pl.when(jnp.logical_not(is_first))
    def _():
        step(cp_ref[...].astype(BF16), rp_ref[...], None)

    @pl.when(is_last)
    def _():
        o_lat = (acc[...] / l_sc[...][:, :1]).astype(BF16)
        for h in range(heads):
            w_uv = w_ref[:, h * hw + QK_NOPE:(h + 1) * hw]
            o_ref[:, h * V_DIM:(h + 1) * V_DIM] = jnp.dot(
                o_lat[h * tq:(h + 1) * tq, :], w_uv, preferred_element_type=F32).astype(o_ref.dtype)


def _mla_latent_attention(qc, c_new, r_new, c_past, r_past, w_ukv_b, tables, l, *, tq, tk, pos0,
                          out_rows, out_block_of):
    qb, kb, fl = tables
    h = H_MLA
    kvl = c_new.shape[1]
    wqk = 2 * LANES
    grid_spec = pltpu.PrefetchScalarGridSpec(
        num_scalar_prefetch=3,
        grid=(qb.shape[0],),
        in_specs=[pl.BlockSpec((h, tq, wqk), lambda s, qb, kb, fl: (0, qb[s], 0)),
                  pl.BlockSpec((tq, kvl), lambda s, qb, kb, fl: (qb[s], 0)),
                  pl.BlockSpec((tq, LANES), lambda s, qb, kb, fl: (qb[s], 0)),
                  pl.BlockSpec((tk, kvl), lambda s, qb, kb, fl: (kb[s], 0)),
                  pl.BlockSpec((tk, LANES), lambda s, qb, kb, fl: (kb[s], 0)),
                  pl.BlockSpec((None, kvl, w_ukv_b.shape[2]), lambda s, qb, kb, fl: (l, 0, 0))],
        out_specs=pl.BlockSpec((tq, h * V_DIM), out_block_of),
        scratch_shapes=[pltpu.VMEM((h * tq, kvl), BF16),
                        pltpu.VMEM((h * tq, LANES), BF16),
                        pltpu.VMEM((h * tq, kvl), F32),
                        pltpu.VMEM((h * tq, LANES), F32),
                        pltpu.VMEM((h * tq, LANES), F32)],
    )
    kern = functools.partial(_mla_latent_kernel, heads=h, pos0=pos0)
    return pl.pallas_call(
        kern, grid_spec=grid_spec,
        out_shape=jax.ShapeDtypeStruct((out_rows, h * V_DIM), BF16),
        compiler_params=_cparams(1),
    )(qb, kb, fl, qc, c_new, r_new, c_past, r_past, w_ukv_b)


def _causal_tables(nq, q_off, per_head=False, heads=1, n_batch=1, past_tiles=None,
                   past_stride=0):
    qb, kb, hb, fl = [], [], [], []
    if past_tiles is None:
        for i in range(nq):
            n = i + 1
            for j in range(n):
                qb.append(q_off + i)
                kb.append(max(i - j, 1) - 1 if j == 0 else i - j)
                hb.append(0)
                fl.append((1 if j == 0 else 0) | (2 if j == n - 1 else 0))
    else:
        for b in range(n_batch):
            for h in range(heads if per_head else 1):
                n = 1 + past_tiles
                for j in range(n):
                    qb.append(q_off + b)
                    jj = past_tiles - 1 if j == 0 else past_tiles - j
                    kb.append(past_stride + b * past_tiles + jj)
                    hb.append(h)
                    fl.append((1 if j == 0 else 0) | (2 if j == n - 1 else 0))
    arr = lambda v: jnp.asarray(np.asarray(v, dtype=np.int32))
    return arr(qb), arr(kb), arr(hb), arr(fl)


ROW_DMA_UNROLL = 8


def _moe_kernel(te_ref, nv_ref, tok_ref, dst_ref, h_hbm, w_ref, wg_ref, wu_ref, wd_ref, y_hbm,
                xbuf, obuf, wg_s, wu_s, wd_s, sem_in, sem_out, *, s):
    t = pl.program_id(0)
    tm = w_ref.shape[0]
    nv = nv_ref[0]
    live = t < nv
    slot = lax.rem(t, 2)

    p = _slab_pitch(s)

    def in_copy(tile, r, sl):
        src0 = pl.multiple_of(tok_ref[tile * tm + r] * p, 8)
        return pltpu.make_async_copy(h_hbm.at[pl.ds(src0, s)],
                                     xbuf.at[sl, pl.ds(pl.multiple_of(r * p, 8), s)],
                                     sem_in.at[sl])

    def gather_start(tile, sl):
        def body(r, _):
            in_copy(tile, r, sl).start()
            return 0
        lax.fori_loop(0, tm, body, 0, unroll=ROW_DMA_UNROLL)

    def gather_wait(tile, sl):
        def body(r, _):
            in_copy(tile, r, sl).wait()
            return 0
        lax.fori_loop(0, tm, body, 0, unroll=ROW_DMA_UNROLL)

    @pl.when(live & (t == 0))
    def _():
        gather_start(0, 0)

    @pl.when(t + 1 < nv)
    def _():
        gather_start(t + 1, 1 - slot)

    prev = te_ref[jnp.maximum(t - 1, 0)]
    fresh = (t == 0) | (te_ref[t] != prev)

    @pl.when(live & fresh)
    def _():
        wg_s[...] = wg_ref[...].astype(BF16)
        wu_s[...] = wu_ref[...].astype(BF16)
        wd_s[...] = wd_ref[...].astype(BF16)

    @pl.when(live)
    def _():
        gather_wait(t, slot)
        x = _slab_load(xbuf, 0, tm, s, lead=slot).astype(BF16)
        a = jnp.dot(x, wg_s[...], preferred_element_type=F32)
        u = jnp.dot(x, wu_s[...], preferred_element_type=F32)
        hid = (a * jax.nn.sigmoid(a)) * u * w_ref[...]
        y = jnp.dot(hid.astype(BF16), wd_s[...], preferred_element_type=F32)

        def out_copy(tile, r):
            dst0 = pl.multiple_of(dst_ref[tile * tm + r] * p, 8)
            return pltpu.make_async_copy(obuf.at[pl.ds(pl.multiple_of(r * p, 8), s)],
                                         y_hbm.at[pl.ds(dst0, s)], sem_out)

        def scatter_wait(tile):
            def wait(r, _):
                out_copy(tile, r).wait()
                return 0
            lax.fori_loop(0, tm, wait, 0, unroll=ROW_DMA_UNROLL)

        @pl.when(t > 0)
        def _():
            scatter_wait(t - 1)

        _slab_store(obuf, 0, y)

        def start(r, _):
            out_copy(t, r).start()
            return 0
        lax.fori_loop(0, tm, start, 0, unroll=ROW_DMA_UNROLL)

        @pl.when(t == nv - 1)
        def _():
            scatter_wait(t)


def _moe_experts(h_slab, row_tok, row_dst, row_w, tile_expert, n_valid, w_gate, w_up, w_down, l,
                 n_dest):
    d, f = w_gate.shape[-2:]
    s = d // LANES
    p = _slab_pitch(s)
    tm = MOE_TILE
    n_tiles = row_tok.shape[0] // tm
    wmap = lambda t, te, nv, tok, dst: (l, te[t], 0, 0)
    grid_spec = pltpu.PrefetchScalarGridSpec(
        num_scalar_prefetch=4, grid=(n_tiles,),
        in_specs=[pl.BlockSpec(memory_space=pl.ANY),
                  pl.BlockSpec((tm, 1), lambda t, te, nv, tok, dst: (t, 0)),
                  pl.BlockSpec((None, None, d, f), wmap),
                  pl.BlockSpec((None, None, d, f), wmap),
                  pl.BlockSpec((None, None, f, d), wmap)],
        out_specs=pl.BlockSpec(memory_space=pl.ANY),
        scratch_shapes=[pltpu.VMEM((2, tm * p, LANES), F32), pltpu.VMEM((tm * p, LANES), F32),
                        pltpu.VMEM((d, f), BF16), pltpu.VMEM((d, f), BF16),
                        pltpu.VMEM((f, d), BF16),
                        pltpu.SemaphoreType.DMA((2,)), pltpu.SemaphoreType.DMA(())],
    )
    return pl.pallas_call(
        functools.partial(_moe_kernel, s=s), grid_spec=grid_spec,
        out_shape=jax.ShapeDtypeStruct(((n_dest + tm) * p, LANES), F32),
        compiler_params=_cparams(1),
    )(tile_expert, n_valid, row_tok, row_dst, h_slab, row_w, w_gate, w_up, w_down)


def _combine_kernel(x_ref, y0_ref, y1_ref, g_ref, o_ref):
    gb, grp, d = x_ref.shape
    s = d // LANES
    for g in range(gb):
        y = _slab_load(y0_ref, g * grp, grp, s) + _slab_load(y1_ref, g * grp, grp, s)
        o_ref[g] = x_ref[g] + g_ref[g] * y


def _combine(x3, yg, modg, l, g_i):
    ng, grp, d = x3.shape
    s = d // LANES
    gb = _row_tile(ng, 2)
    blk = pl.BlockSpec((gb, grp, d), lambda i: (i, 0, 0))
    yblk = lambda off: pl.BlockSpec((gb * grp * _slab_pitch(s), LANES), lambda i: (i + off, 0))
    return pl.pallas_call(
        _combine_kernel,
        grid=(ng // gb,),
        in_specs=[blk, yblk(0), yblk(ng // gb), _mod_spec(g_i, gb, d)],
        out_specs=blk,
        out_shape=jax.ShapeDtypeStruct(x3.shape, F32),
        compiler_params=_cparams(1),
    )(x3, yg, yg, modg)


def _dispatch(route, n):
    tm = MOE_TILE
    r_cap = (2 * n + N_EXPERTS * (tm - 1) + tm - 1) // tm * tm
    e = route[:, :2].astype(jnp.int32)
    w = route[:, 2:4]
    flat_e = e.T.reshape(-1)
    flat_w = w.T.reshape(-1)
    order = jnp.argsort(flat_e, stable=True).astype(jnp.int32)
    experts = jnp.arange(N_EXPERTS, dtype=jnp.int32)
    counts = jnp.sum((flat_e[:, None] == experts[None, :]).astype(jnp.int32), axis=0)
    padded = (counts + tm - 1) // tm * tm
    ends_p = jnp.cumsum(padded)
    starts_p = ends_p - padded
    starts = jnp.cumsum(counts) - counts
    tile_start = jnp.arange(r_cap // tm, dtype=jnp.int32) * tm
    tile_expert = jnp.minimum(jnp.searchsorted(ends_p, tile_start, side="right"),
                              N_EXPERTS - 1).astype(jnp.int32)
    n_valid = (ends_p[-1:] // tm).astype(jnp.int32)
    row_e = jnp.repeat(tile_expert, tm)
    local = jnp.arange(r_cap, dtype=jnp.int32) - starts_p[row_e]
    valid = local < counts[row_e]
    slot = order[jnp.clip(starts[row_e] + local, 0, 2 * n - 1)]
    row_dst = jnp.where(valid, slot, 2 * n + jnp.arange(r_cap, dtype=jnp.int32) % tm)
    row_tok = jnp.where(valid, jnp.where(slot >= n, slot - n, slot), 0)
    row_w = jnp.where(valid, flat_w[slot], 0.0)
    return row_tok, row_dst, row_w.reshape(r_cap, 1), tile_expert, n_valid


def _rope_tables(pos):
    inv = ROPE_THETA ** (-jnp.arange(0, QK_ROPE, 2, dtype=F32) / QK_ROPE)
    ang = pos.astype(F32)[:, None] * inv[None, :]
    cos, sin = jnp.cos(ang), jnp.sin(ang)
    pad = jnp.zeros((pos.shape[0], LANES - QK_ROPE), F32)
    return (jnp.concatenate([cos, cos, pad], axis=1),
            jnp.concatenate([-sin, sin, pad], axis=1))


def _rope_lanes(acc, c, s):
    return acc * c + pltpu.roll(acc, LANES - QK_ROPE, axis=1) * s


def _swap_halves(w):
    half = w.shape[-1] // 2
    return jnp.concatenate([w[..., half:], w[..., :half]], axis=-1)


def kernel(x_prompt, x_sample, c_prompt, c_sample, cache_sb_k, cache_sb_v, cache_mla_ckv,
           cache_mla_krope, w_ada, b_ada, g_norm_mix, g_norm_ffn, w_in, g_q_lat, g_kv_lat,
           w_uq, w_ukv, w_branch_sb, w_branch_mla, w_out, w_router_group, b_router_group,
           w_router_expert, b_router_expert, w_exp_gate, w_exp_up, w_exp_down, g_final):
    bp, t_p, d = x_prompt.shape
    bs, t_s, _ = x_sample.shape
    depth = w_in.shape[0]
    past = cache_sb_k.shape[2]
    grp = t_s
    n_p, n_s = bp * t_p, bs * t_s
    n = n_p + n_s
    ng = n // grp
    sb_w = H_SB * DH_SB
    q_lora = g_q_lat.shape[1]
    kv_lora = g_kv_lat.shape[1]
    tm = _token_tile(n, grp)
    gpt = tm // grp
    tn = 512
    assert bp == 1 and t_p % ATTN_TILE == 0 and t_p % grp == 0 and past % 512 == 0

    x3 = jnp.concatenate([x_prompt.reshape(n_p // grp, grp, d), x_sample], axis=0)

    n_c = bp + bs
    c_rows = 16
    c_all = jnp.zeros((c_rows, d), F32).at[:n_c].set(jnp.concatenate([c_prompt, c_sample], 0))
    n_modc = N_MOD * d

    def ada_epi(accs, ex, outs):
        outs[0][...] = accs[0] + ex[0][...]

    mods = []
    for l in range(depth):
        mods.append(_matmul(
            [(c_all, w_ada, _wspec(l, d, tn, 0), True)], m=c_rows, n_out=n_modc, tm=c_rows, tn=tn,
            prologue=lambda a, ex: a * jax.nn.sigmoid(a),
            epilogue=ada_epi,
            extras=[b_ada.reshape(depth, 1, n_modc)],
            extra_specs=[pl.BlockSpec((None, 1, tn), lambda j, i, l=l: (l, 0, j))],
            out_shape=[jax.ShapeDtypeStruct((c_rows, n_modc), F32)],
            out_specs=[_spec2(c_rows, tn)])[0])
    mod = jnp.stack(mods)
    modg = jnp.concatenate(
        [jnp.broadcast_to(mod[:, :bp], (depth, n_p // grp, n_modc)), mod[:, bp:n_c]], axis=1)
    modg = modg.reshape(depth, ng, N_MOD, d).transpose(0, 2, 1, 3)
    modg = modg.reshape(depth * N_MOD, ng, 1, d)

    def mod_idx(l, k):
        return l * N_MOD + k

    def mspec_rows(l, k):
        return pl.BlockSpec((None, gpt, 1, tn), lambda j, i: (mod_idx(l, k), i, 0, j))

    pos = jnp.concatenate([jnp.arange(t_p, dtype=jnp.int32),
                           jnp.tile(past + jnp.arange(t_s, dtype=jnp.int32), bs)])
    rope_c, rope_s = _rope_tables(pos)
    tq = ATTN_TILE
    nq_p = t_p // tq
    tab_p = _causal_tables(nq_p, 0)
    tk_s = 512
    pt = past // tk_s
    tab_sb_s = lambda l: _causal_tables(0, n_p // t_s, n_batch=bs, past_tiles=pt,
                                        past_stride=l * bs * pt)
    tab_mla_s = _causal_tables(0, n_p // t_s, n_batch=bs, past_tiles=pt)

    off_q, off_k, off_v = 0, sb_w, 2 * sb_w
    off_cq = 3 * sb_w
    off_ckv = off_cq + q_lora
    off_kr = off_ckv + kv_lora
    off_g = off_kr + QK_ROPE
    w_kr = w_in[:, :, off_kr:off_kr + QK_ROPE]
    w_kr_aug = jnp.concatenate([w_kr, _swap_halves(w_kr)], axis=-1).astype(BF16)
    w_gates = w_in[:, :, off_g:].astype(BF16)
    uq = w_uq.reshape(depth, q_lora, H_MLA, QK_NOPE + QK_ROPE)
    uq_r = uq[..., QK_NOPE:]
    w_uq_cat = jnp.concatenate([uq[..., :QK_NOPE], uq_r, _swap_halves(uq_r)], axis=-1)
    w_uq_cat = w_uq_cat.reshape(depth, q_lora, H_MLA * 2 * LANES)
    w_router = jnp.concatenate(
        [w_router_expert, w_router_group,
         jnp.zeros((depth, d, LANES - N_EXPERTS - N_GROUPS), F32)], axis=-1)
    r_hi = w_router.astype(BF16)
    r_res = w_router - r_hi.astype(F32)
    r_mid = r_res.astype(BF16)
    r_lo = (r_res - r_mid.astype(F32)).astype(BF16)
    w_router3 = jnp.stack([r_hi, r_mid, r_lo], axis=1)
    b_router = jnp.concatenate(
        [b_router_expert, b_router_group,
         jnp.zeros((depth, LANES - N_EXPERTS - N_GROUPS), F32)], axis=-1).reshape(depth, 1, LANES)

    rows_past = bs * past
    kr_past = jnp.pad(cache_mla_krope.reshape(depth * rows_past, QK_ROPE),
                      ((0, 0), (0, LANES - QK_ROPE))).astype(BF16)
    w_ukv_b = w_ukv.astype(BF16)
    hm = lambda rows: jax.ShapeDtypeStruct((H_SB, rows, LANES), BF16)
    new_k, new_v, new_c, new_r = [], [], [], []

    for l in range(depth):
        h = _norm_mod(x3, g_norm_mix.reshape(depth, 1, d), modg, l, mod_idx(l, 1), mod_idx(l, 0))
        h = h.reshape(n, d)

        def plain_hm(accs, ex, outs):
            _store_heads(outs[0], accs[0])

        def f32_and_hm(accs, ex, outs):
            outs[0][...] = accs[0]
            _store_heads(outs[1], accs[0])

        sb_q = _matmul([(h, w_in, _wspec(l, d, tn, off_q), True)], m=n, n_out=sb_w, tm=tm, tn=tn,
                       epilogue=plain_hm, out_shape=[hm(n)], out_specs=[_hm_spec(tm, tn)])[0]
        k_f32, sb_k = _matmul([(h, w_in, _wspec(l, d, tn, off_k), True)], m=n, n_out=sb_w, tm=tm,
                              tn=tn, epilogue=f32_and_hm,
                              out_shape=[jax.ShapeDtypeStruct((n, sb_w), F32), hm(n)],
                              out_specs=[_spec2(tm, tn), _hm_spec(tm, tn)])
        v_f32, sb_v = _matmul([(h, w_in, _wspec(l, d, tn, off_v), True)], m=n, n_out=sb_w, tm=tm,
                              tn=tn, epilogue=f32_and_hm,
                              out_shape=[jax.ShapeDtypeStruct((n, sb_w), F32), hm(n)],
                              out_specs=[_spec2(tm, tn), _hm_spec(tm, tn)])

        def plain_f32(accs, ex, outs):
            outs[0][...] = accs[0]

        c_q = _matmul([(h, w_in, _wspec(l, d, tn, off_cq), True)], m=n, n_out=q_lora, tm=tm, tn=tn,
                      epilogue=plain_f32, out_shape=[jax.ShapeDtypeStruct((n, q_lora), F32)],
                      out_specs=[_spec2(tm, tn)])[0]

        def ckv_epi(accs, ex, outs):
            y = _rms(accs[0], ex[0][...])
            outs[0][...] = y
            outs[1][...] = y.astype(BF16)

        c_kv, c_kv_b = _matmul(
            [(h, w_in, _wspec(l, d, kv_lora, off_ckv), True)], m=n, n_out=kv_lora, tm=tm, tn=kv_lora,
            epilogue=ckv_epi, extras=[g_kv_lat.reshape(depth, 1, kv_lora)],
            extra_specs=[pl.BlockSpec((None, 1, kv_lora), lambda j, i: (l, 0, 0))],
            out_shape=[jax.ShapeDtypeStruct((n, kv_lora), F32),
                       jax.ShapeDtypeStruct((n, kv_lora), BF16)],
            out_specs=[_spec2(tm, kv_lora), _spec2(tm, kv_lora)])

        def kr_epi(accs, ex, outs):
            r = _rope_lanes(accs[0], ex[0][...], ex[1][...])
            outs[0][...] = r
            outs[1][...] = r.astype(BF16)

        rope_specs = [pl.BlockSpec((tm, LANES), lambda j, i: (i, 0))] * 2
        k_r, k_r_b = _matmul(
            [(h, w_kr_aug, pl.BlockSpec((None, d, LANES), lambda j, i: (l, 0, 0)), False)],
            m=n, n_out=LANES, tm=tm, tn=LANES, epilogue=kr_epi,
            extras=[rope_c, rope_s], extra_specs=rope_specs,
            out_shape=[jax.ShapeDtypeStruct((n, LANES), F32), jax.ShapeDtypeStruct((n, LANES), BF16)],
            out_specs=[_spec2(tm, LANES), _spec2(tm, LANES)])

        def gate_epi(accs, ex, outs):
            outs[0][...] = jax.nn.sigmoid(accs[0])

        gates = _matmul(
            [(h, w_gates, pl.BlockSpec((None, d, tn), lambda j, i: (l, 0, j)), False)],
            m=n, n_out=2 * d, tm=tm, tn=tn, epilogue=gate_epi,
            out_shape=[jax.ShapeDtypeStruct((n, 2 * d), F32)], out_specs=[_spec2(tm, tn)])[0]

        def cq_prologue(a, ex):
            return _rms(a, ex[0][...])

        gq_spec = pl.BlockSpec((None, 1, q_lora), lambda j, i: (l, 0, 0))
        gq = g_q_lat.reshape(depth, 1, q_lora)
        wqk = 2 * LANES
        hm_qk = lambda rows: jax.ShapeDtypeStruct((H_MLA, rows, wqk), BF16)

        q_scale = (QK_NOPE + QK_ROPE) ** -0.5 * float(np.log2(np.e))

        def qcat_epi(accs, ex, outs):
            c, s = ex[1][...], ex[2][...]
            for hh in range(tn // wqk):
                blk = accs[0][:, hh * wqk:(hh + 1) * wqk]
                outs[0][hh, :, :LANES] = (blk[:, :LANES] * q_scale).astype(BF16)
                outs[0][hh, :, LANES:] = (_rope_lanes(blk[:, LANES:], c, s) * q_scale).astype(BF16)

        q_cat = _matmul(
            [(c_q, w_uq_cat, pl.BlockSpec((None, q_lora, tn), lambda j, i: (l, 0, j)), True)],
            m=n, n_out=H_MLA * wqk, tm=tm, tn=tn, prologue=cq_prologue, epilogue=qcat_epi,
            extras=[gq, rope_c, rope_s], extra_specs=[gq_spec] + rope_specs,
            out_shape=[hm_qk(n)],
            out_specs=[pl.BlockSpec((tn // wqk, tm, wqk), lambda j, i: (j, i, 0))])[0]

        tn_kv = min(2048, H_MLA * wqk)
        hpt = tn_kv // wqk

        def kv_epi(accs, ex, outs):
            kr = ex[0][...]
            ones = jnp.ones(kr.shape, BF16)
            for hh in range(hpt):
                outs[0][hh, :, :LANES] = accs[0][:, hh * wqk:hh * wqk + LANES].astype(BF16)
                outs[0][hh, :, LANES:] = kr
                outs[1][hh, :, :LANES] = accs[0][:, hh * wqk + LANES:(hh + 1) * wqk].astype(BF16)
                outs[1][hh, :, LANES:] = ones

        def up_kv(a, kr, rows, tmr):
            hspec = pl.BlockSpec((hpt, tmr, wqk), lambda j, i: (j, i, 0))
            return _matmul(
                [(a, w_ukv, pl.BlockSpec((None, kv_lora, tn_kv), lambda j, i: (l, 0, j)), True)],
                m=rows, n_out=H_MLA * wqk, tm=tmr, tn=tn_kv, epilogue=kv_epi, extras=[kr],
                extra_specs=[pl.BlockSpec((tmr, LANES), lambda j, i: (i, 0))],
                out_shape=[hm_qk(rows), hm_qk(rows)], out_specs=[hspec, hspec])

        kc_new, v_new = up_kv(c_kv_b, k_r_b, n, tm)

        o_sb_p = _sb_self_attention(sb_q, sb_k, sb_v, tq=tq, n_rows=n_p)
        cache_k2 = cache_sb_k.reshape(depth * bs * past * H_SB, DH_SB)
        cache_v2 = cache_sb_v.reshape(depth * bs * past * H_SB, DH_SB)
        q_off_s = n_p // t_s
        o_sb_s = _sb_attention(sb_q, sb_k, sb_v, cache_k2, cache_v2, tab_sb_s(l), heads=H_SB,
                               tq=t_s, tk=tk_s, sub=ATTN_TILE, out_rows=n_s,
                               out_block_of=lambda s, qb, kb, hb, fl: (qb[s] - q_off_s, 0))
        o_sb = jnp.concatenate([o_sb_p, o_sb_s], axis=0)

        tp3 = (tab_p[0], tab_p[1], tab_p[3])
        o_mla_p = _mla_attention(q_cat, kc_new, v_new, kc_new, v_new, tp3,
                                 tq=tq, tk=tq, pos0=0, out_rows=n_p,
                                 out_block_of=lambda s, qb, kb, fl: (qb[s], 0))
        tab_s = tab_sb_s(l)
        o_mla_s = _mla_latent_attention(
            q_cat, c_kv_b, k_r_b, cache_mla_ckv.reshape(depth * rows_past, kv_lora), kr_past,
            w_ukv_b, (tab_s[0], tab_s[1], tab_s[3]), l, tq=t_s, tk=tk_s, pos0=past, out_rows=n_s,
            out_block_of=lambda s, qb, kb, fl: (qb[s] - q_off_s, 0))
        o_mla = jnp.concatenate([o_mla_p, o_mla_s], axis=0)

        def merge_epi(accs, ex, outs):
            outs[0][...] = (ex[0][...] * accs[0] + ex[1][...] * accs[1]).astype(BF16)

        merged = _matmul(
            [(o_sb, w_branch_sb, _wspec(l, sb_w, tn, 0), True),
             (o_mla, w_branch_mla, _wspec(l, H_MLA * V_DIM, tn, 0), True)],
            m=n, n_out=d, tm=tm, tn=tn, epilogue=merge_epi,
            extras=[gates, gates],
            extra_specs=[pl.BlockSpec((tm, tn), lambda j, i: (i, j)),
                         pl.BlockSpec((tm, tn), lambda j, i: (i, j + d // tn))],
            out_shape=[jax.ShapeDtypeStruct((n, d), BF16)], out_specs=[_spec2(tm, tn)])[0]

        def resid_epi(accs, ex, outs):
            for g in range(gpt):
                outs[0][g] = ex[0][g] + ex[1][g] * accs[0][g * grp:(g + 1) * grp, :]

        x_spec = pl.BlockSpec((gpt, grp, tn), lambda j, i: (i, 0, j))
        x3 = _matmul(
            [(merged, w_out, _wspec(l, d, tn, 0), True)], m=n, n_out=d, tm=tm, tn=tn,
            epilogue=resid_epi, extras=[x3, modg], extra_specs=[x_spec, mspec_rows(l, 2)],
            out_shape=[jax.ShapeDtypeStruct((ng, grp, d), F32)], out_specs=[x_spec])[0]

        h2, route = _norm_route(x3, g_norm_ffn.reshape(depth, 1, d), modg, l, mod_idx(l, 4),
                                mod_idx(l, 3), w_router3, b_router)
        row_tok, row_dst, row_w, tile_expert, n_valid = _dispatch(route.reshape(n, LANES), n)
        yg = _moe_experts(h2, row_tok, row_dst, row_w, tile_expert, n_valid,
                          w_exp_gate, w_exp_up, w_exp_down, l, 2 * n)
        x3 = _combine(x3, yg, modg, l, mod_idx(l, 5))

        new_k.append(k_f32)
        new_v.append(v_f32)
        new_c.append(c_kv)
        new_r.append(k_r[:, :QK_ROPE])

    y_p, y_s = _final_norm(x3, g_final.reshape(1, d), n_p // grp)

    def split(parts, tail):
        a = jnp.stack(parts)
        return (a[:, :n_p].reshape((depth, bp, t_p) + tail),
                a[:, n_p:].reshape((depth, bs, t_s) + tail))

    pk, sk = split(new_k, (H_SB, DH_SB))
    pv, sv = split(new_v, (H_SB, DH_SB))
    pc, sc = split(new_c, (kv_lora,))
    pr, sr = split(new_r, (QK_ROPE,))
    return (y_p.reshape(bp, t_p, d), y_s.reshape(bs, t_s, d), pk, pv, pc, pr, sk, sv, sc, sr)
```

```python
import functools

import numpy as np
import jax
import jax.numpy as jnp
from jax import lax
from jax.experimental import pallas as pl
from jax.experimental.pallas import tpu as pltpu

F32 = jnp.float32
BF16 = jnp.bfloat16

CHUNK = 64
H_SB = 16
DH_SB = 128
H_MLA = 16
QK_NOPE = 128
QK_ROPE = 64
V_DIM = 128
ROPE_THETA = 10000.0
N_GROUPS = 4
EXPERTS_PER_GROUP = 8
N_EXPERTS = N_GROUPS * EXPERTS_PER_GROUP
N_MOD = 6
EPS = 1e-6

LANES = 128
ATTN_TILE = 256
MOE_TILE = 256
HEAD_UNROLL = 4
MLA_UNROLL = 16
SB_DEAD = 152.0
VMEM_LIMIT = 56 * 1024 * 1024


def _cparams(n_axes, vmem=VMEM_LIMIT):
    return pltpu.CompilerParams(dimension_semantics=("arbitrary",) * n_axes,
                                vmem_limit_bytes=vmem)


def _row_tile(n, cap=512):
    t = cap
    while n % t:
        t //= 2
    return t


def _token_tile(n, grp, cap=1152):
    return max(t for t in range(grp, cap + 1, grp) if n % t == 0)


def _matmul(pairs, *, m, n_out, tm, tn, epilogue, out_shape, out_specs,
            extras=(), extra_specs=(), prologue=None):
    n_pairs = len(pairs)
    n_ex = len(extras)
    n_outs = len(out_shape)
    cast = [p[3] for p in pairs]

    def kern(*refs):
        a_refs = refs[0:2 * n_pairs:2]
        b_refs = refs[1:2 * n_pairs:2]
        ex = refs[2 * n_pairs:2 * n_pairs + n_ex]
        outs = refs[2 * n_pairs + n_ex:2 * n_pairs + n_ex + n_outs]
        scr = refs[2 * n_pairs + n_ex + n_outs:]
        i = pl.program_id(1)
        accs = []
        si = 0
        for p in range(n_pairs):
            if cast[p]:
                bsc = scr[si]
                si += 1

                @pl.when(i == 0)
                def _(bsc=bsc, b_ref=b_refs[p]):
                    bsc[...] = b_ref[...].astype(BF16)

                bv = bsc[...]
            else:
                bv = b_refs[p][...]
            a = a_refs[p][...]
            if prologue is not None:
                a = prologue(a, ex)
            accs.append(jnp.dot(a.astype(BF16), bv, preferred_element_type=F32))
        epilogue(accs, ex, outs)

    in_specs, args, scratch = [], [], []
    for (a, b, b_spec, cb) in pairs:
        k = a.shape[1]
        in_specs += [pl.BlockSpec((tm, k), lambda j, i: (i, 0)), b_spec]
        args += [a, b]
        if cb:
            scratch.append(pltpu.VMEM((k, tn), BF16))
    in_specs += list(extra_specs)
    args += list(extras)
    return pl.pallas_call(
        kern,
        grid=(n_out // tn, m // tm),
        in_specs=in_specs,
        out_specs=out_specs,
        out_shape=out_shape,
        scratch_shapes=scratch,
        compiler_params=_cparams(2),
    )(*args)


def _wspec(l, k, tn, col_off):
    cb = col_off // tn
    assert cb * tn == col_off
    return pl.BlockSpec((None, k, tn), lambda j, i: (l, 0, cb + j))


def _spec2(tm, tn):
    return pl.BlockSpec((tm, tn), lambda j, i: (i, j))


def _hm_spec(tm, tn):
    return pl.BlockSpec((tn // LANES, tm, LANES), lambda j, i: (j, i, 0))


def _store_heads(o_ref, val):
    for c in range(val.shape[1] // LANES):
        o_ref[c] = val[:, c * LANES:(c + 1) * LANES].astype(o_ref.dtype)


def _rms(x, g):
    return x * lax.rsqrt(jnp.mean(x * x, axis=-1, keepdims=True) + EPS) * g


def _norm_mod_kernel(x_ref, g_ref, sc_ref, sh_ref, o_ref):
    x = x_ref[...]
    y = _rms(x, g_ref[...])
    o_ref[...] = (y * (1.0 + sc_ref[...]) + sh_ref[...]).astype(o_ref.dtype)


def _split3(x):
    hi = x.astype(BF16)
    r = x - hi.astype(F32)
    mid = r.astype(BF16)
    lo = (r - mid.astype(F32)).astype(BF16)
    return hi, mid, lo


def _dot_f32(a, b3):
    a_hi, a_mid, a_lo = _split3(a)
    b_hi, b_mid, b_lo = b3
    d = functools.partial(jnp.dot, preferred_element_type=F32)
    small = d(a_hi, b_lo) + d(a_lo, b_hi) + d(a_mid, b_mid)
    return (d(a_hi, b_hi) + (d(a_hi, b_mid) + d(a_mid, b_hi))) + small


def _route(logits):
    lane = lax.broadcasted_iota(jnp.int32, logits.shape, 1)
    lanef = lane.astype(F32)
    big = jnp.float32(1e9)
    ninf = jnp.float32(-jnp.inf)
    is_g = (lane >= N_EXPERTS) & (lane < N_EXPERTS + N_GROUPS)
    gl = jnp.where(is_g, logits, ninf)
    gmax = jnp.max(gl, axis=1, keepdims=True)
    g_idx = jnp.min(jnp.where(gl == gmax, lanef - N_EXPERTS, big), axis=1, keepdims=True)
    p_group = 1.0 / jnp.sum(jnp.where(is_g, jnp.exp(gl - gmax), 0.0), axis=1, keepdims=True)
    grp = jnp.floor(lanef * (1.0 / EXPERTS_PER_GROUP))
    in_g = (lane < N_EXPERTS) & (grp == g_idx)
    el = jnp.where(in_g, logits, ninf)
    e1 = jnp.max(el, axis=1, keepdims=True)
    i1 = jnp.min(jnp.where(el == e1, lanef, big), axis=1, keepdims=True)
    el2 = jnp.where(lanef == i1, ninf, el)
    e2 = jnp.max(el2, axis=1, keepdims=True)
    i2 = jnp.min(jnp.where(el2 == e2, lanef, big), axis=1, keepdims=True)
    t = jnp.exp(e2 - e1)
    den = 1.0 + t
    w1 = (1.0 / den) * p_group
    w2 = (t / den) * p_group
    out = jnp.where(lane == 0, i1, jnp.where(lane == 1, i2,
          jnp.where(lane == 2, w1, jnp.where(lane == 3, w2, 0.0))))
    return out


def _slab_pitch(s):
    return s + 8


def _slab_store(ref, row0, val):
    rows, d = val.shape
    s = d // LANES
    p = _slab_pitch(s)
    for c in range(s):
        ref[pl.ds(row0 * p + c, rows, stride=p), :] = val[:, c * LANES:(c + 1) * LANES]


def _slab_load(ref, row0, rows, s, lead=None):
    pieces = []
    p = _slab_pitch(s)
    for c in range(s):
        rs = pl.ds(row0 * p + c, rows, stride=p)
        pieces.append(ref[rs, :] if lead is None else ref[lead, rs, :])
    return jnp.concatenate(pieces, axis=1)


def _norm_route_kernel(x_ref, g_ref, sc_ref, sh_ref, wr_ref, br_ref, h_ref, r_ref):
    x = x_ref[...]
    y = _rms(x, g_ref[...])
    h = y * (1.0 + sc_ref[...]) + sh_ref[...]
    b3 = (wr_ref[0], wr_ref[1], wr_ref[2])
    grp = x.shape[1]
    for g in range(x.shape[0]):
        _slab_store(h_ref, g * grp, h[g])
        logits = _dot_f32(h[g], b3) + br_ref[...]
        r_ref[g] = _route(logits)


def _mod_spec(idx, gb, d):
    return pl.BlockSpec((None, gb, 1, d), lambda i: (idx, i, 0, 0))


def _norm_mod(x3, g, modg, l, sc_i, sh_i):
    ng, grp, d = x3.shape
    gb = _row_tile(ng, 4)
    return pl.pallas_call(
        _norm_mod_kernel,
        grid=(ng // gb,),
        in_specs=[pl.BlockSpec((gb, grp, d), lambda i: (i, 0, 0)),
                  pl.BlockSpec((None, 1, d), lambda i: (l, 0, 0)),
                  _mod_spec(sc_i, gb, d), _mod_spec(sh_i, gb, d)],
        out_specs=pl.BlockSpec((gb, grp, d), lambda i: (i, 0, 0)),
        out_shape=jax.ShapeDtypeStruct(x3.shape, BF16),
        compiler_params=_cparams(1),
    )(x3, g, modg, modg)


def _norm_route(x3, g, modg, l, sc_i, sh_i, wr3, br):
    ng, grp, d = x3.shape
    gb = _row_tile(ng, 4)
    return pl.pallas_call(
        _norm_route_kernel,
        grid=(ng // gb,),
        in_specs=[pl.BlockSpec((gb, grp, d), lambda i: (i, 0, 0)),
                  pl.BlockSpec((None, 1, d), lambda i: (l, 0, 0)),
                  _mod_spec(sc_i, gb, d), _mod_spec(sh_i, gb, d),
                  pl.BlockSpec((None, 3, d, LANES), lambda i: (l, 0, 0, 0)),
                  pl.BlockSpec((None, 1, LANES), lambda i: (l, 0, 0))],
        out_specs=[pl.BlockSpec((gb * grp * _slab_pitch(d // LANES), LANES), lambda i: (i, 0)),
                   pl.BlockSpec((gb, grp, LANES), lambda i: (i, 0, 0))],
        out_shape=[jax.ShapeDtypeStruct((ng * grp * _slab_pitch(d // LANES), LANES), F32),
                   jax.ShapeDtypeStruct((ng, grp, LANES), F32)],
        compiler_params=_cparams(1),
    )(x3, g, modg, modg, wr3, br)


def _final_norm_kernel(x_ref, g_ref, op_ref, os_ref, *, n_prompt_blocks):
    i = pl.program_id(0)
    y = _rms(x_ref[...], g_ref[...])

    @pl.when(i < n_prompt_blocks)
    def _():
        op_ref[...] = y

    @pl.when(i >= n_prompt_blocks)
    def _():
        os_ref[...] = y


def _final_norm(x3, g, ng_prompt):
    ng, grp, d = x3.shape
    gb = _row_tile(np.gcd(ng_prompt, ng - ng_prompt), 4)
    npb = ng_prompt // gb
    blk = lambda f: pl.BlockSpec((gb, grp, d), f)
    return pl.pallas_call(
        functools.partial(_final_norm_kernel, n_prompt_blocks=npb),
        grid=(ng // gb,),
        in_specs=[blk(lambda i: (i, 0, 0)), pl.BlockSpec((1, d), lambda i: (0, 0))],
        out_specs=[blk(lambda i: (jnp.minimum(i, npb - 1), 0, 0)),
                   blk(lambda i: (jnp.maximum(i - npb, 0), 0, 0))],
        out_shape=[jax.ShapeDtypeStruct((ng_prompt, grp, d), F32),
                   jax.ShapeDtypeStruct((ng - ng_prompt, grp, d), F32)],
        compiler_params=_cparams(1),
    )(x3, g)


def _lanes(c, w):
    if w % LANES == 0:
        return c if w == LANES else jnp.tile(c, (1, w // LANES))
    return c[:, :w]


def _dot_nt(a, b):
    return lax.dot_general(a, b, (((1,), (1,)), ((), ())), preferred_element_type=F32)


def _sb_block(qh, kh, vh, c, u, masked):
    w = kh.shape[0]
    z = _dot_nt(qh, kh)
    sp = jnp.maximum(z, 0.0) + jnp.log2(1.0 + jnp.exp2(-jnp.abs(z)))
    if masked:
        row = lax.broadcasted_iota(jnp.int32, z.shape, 0)
        col = lax.broadcasted_iota(jnp.int32, z.shape, 1)
        valid = col < row
        sp = jnp.where(valid, sp, 0.0)
    hi = sp.astype(BF16)
    lo = (sp - hi.astype(F32)).astype(BF16)
    cs = jnp.dot(hi, u, preferred_element_type=F32) + jnp.dot(lo, u, preferred_element_type=F32)
    wgt = jnp.exp2(z - sp - cs - _lanes(c, w))
    if masked:
        wgt = jnp.where(valid, wgt, 0.0)
    o = jnp.dot(wgt.astype(BF16), vh, preferred_element_type=F32)
    c_new = c + jnp.sum(sp, axis=1, keepdims=True)
    return o, c_new


def _suffix_matrix(w):
    j = np.arange(w)[:, None]
    s = np.arange(w)[None, :]
    return jnp.asarray((j > s).astype(np.float32), dtype=BF16)


def _sb_stream_kernel(q_ref, kn_ref, vn_ref, k_hbm, v_hbm, ud_ref, up_ref, o_ref,
                      kbuf, vbuf, acc, carry, done, alive_ref, sem, *, heads, tk, cache_tiles,
                      cache_base):
    i = pl.program_id(0)
    n_past = i if cache_tiles is None else cache_tiles

    def fetch(j, slot):
        if cache_tiles is None:
            rows = pl.ds(pl.multiple_of(j * tk, tk), tk)
            src_k, src_v = k_hbm.at[:, rows, :], v_hbm.at[:, rows, :]
        else:
            blk = tk * heads
            rows = pl.ds(pl.multiple_of((cache_base + i * cache_tiles + j) * blk, blk), blk)
            src_k, src_v = k_hbm.at[rows], v_hbm.at[rows]
        return (pltpu.make_async_copy(src_k, kbuf.at[slot], sem.at[0, slot]),
                pltpu.make_async_copy(src_v, vbuf.at[slot], sem.at[1, slot]))

    def tile_of(buf, slot, h):
        if cache_tiles is None:
            return buf[slot, h]
        return buf.at[slot][pl.ds(h, tk, stride=heads), :].astype(BF16)

    @pl.when(n_past > 0)
    def _():
        for c in fetch(n_past - 1, 0):
            c.start()

    def diag(h, _):
        o, c = _sb_block(q_ref[h], kn_ref[h], vn_ref[h], jnp.zeros(carry.shape[1:], F32),
                         ud_ref[...], True)
        acc[h] = o
        carry[h] = c
        done[h] = 0
        return 0
    lax.fori_loop(0, heads, diag, 0, unroll=min(heads, HEAD_UNROLL))

    alive_ref[0] = heads

    @pl.when(n_past > 0)
    def _():
        for c in fetch(n_past - 1, 0):
            c.wait()

        @pl.when(n_past > 1)
        def _():
            for c in fetch(n_past - 2, 1):
                c.start()

        def first(h, alive):
            o, c = _sb_block(q_ref[h], tile_of(kbuf, 0, h), tile_of(vbuf, 0, h), carry[h],
                             up_ref[...], False)
            acc[h] = acc[h] + o
            carry[h] = c
            dead = (jnp.min(c) >= SB_DEAD).astype(jnp.int32)
            done[h] = dead
            return alive + 1 - dead
        alive_ref[0] = lax.fori_loop(0, heads, first, 0, unroll=min(heads, HEAD_UNROLL))

    def cond(state):
        j, alive = state
        return (j >= 0) & (alive > 0)

    def body(state):
        j, _ = state
        slot = lax.rem(n_past - 1 - j, 2)
        for c in fetch(j, slot):
            c.wait()

        @pl.when(j > 0)
        def _():
            for c in fetch(j - 1, 1 - slot):
                c.start()

        def head(h, alive):
            @pl.when(done[h] == 0)
            def _():
                o, c = _sb_block(q_ref[h], tile_of(kbuf, slot, h), tile_of(vbuf, slot, h),
                                 carry[h], up_ref[...], False)
                acc[h] = acc[h] + o
                carry[h] = c
                done[h] = (jnp.min(c) >= SB_DEAD).astype(jnp.int32)
            return alive + 1 - done[h]
        return j - 1, lax.fori_loop(0, heads, head, 0)

    j_end, _ = lax.while_loop(cond, body, (jnp.int32(n_past - 2), alive_ref[0]))

    @pl.when(j_end >= 0)
    def _():
        for c in fetch(j_end, lax.rem(n_past - 1 - j_end, 2)):
            c.wait()

    for h in range(heads):
        o_ref[:, h * DH_SB:(h + 1) * DH_SB] = acc[h].astype(o_ref.dtype)


def _sb_stream_attention(q, kn, vn, k_src, v_src, *, tq, tk, q_block0, n_tiles, cache_tiles=None,
                         cache_base=0):
    heads = q.shape[0]
    blk = pl.BlockSpec((heads, tq, DH_SB), lambda i: (0, q_block0 + i, 0))
    anyspec = pl.BlockSpec(memory_space=pl.ANY)
    if cache_tiles is None:
        buf = pltpu.VMEM((2, heads, tk, DH_SB), BF16)
    else:
        buf = pltpu.VMEM((2, tk * heads, DH_SB), F32)
    kern = functools.partial(_sb_stream_kernel, heads=heads, tk=tk, cache_tiles=cache_tiles,
                             cache_base=cache_base)
    return pl.pallas_call(
        kern, grid=(n_tiles,),
        in_specs=[blk, blk, blk, anyspec, anyspec,
                  pl.BlockSpec((tq, tq), lambda i: (0, 0)), pl.BlockSpec((tk, tk), lambda i: (0, 0))],
        out_specs=pl.BlockSpec((tq, heads * DH_SB), lambda i: (i, 0)),
        out_shape=jax.ShapeDtypeStruct((n_tiles * tq, heads * DH_SB), BF16),
        scratch_shapes=[buf, buf,
                        pltpu.VMEM((heads, tq, DH_SB), F32),
                        pltpu.VMEM((heads, tq, LANES), F32),
                        pltpu.SMEM((heads,), jnp.int32),
                        pltpu.SMEM((1,), jnp.int32),
                        pltpu.SemaphoreType.DMA((2, 2))],
        compiler_params=_cparams(1),
    )(q, kn, vn, k_src, v_src, _suffix_matrix(tq), _suffix_matrix(tk))


def _mla_block(qc, kc, va, m, acc, mask):
    s = _dot_nt(qc, kc)
    if mask is not None:
        s = jnp.where(mask, s, -jnp.inf)
    m_new = jnp.maximum(m, jnp.max(s, axis=1, keepdims=True))
    alpha = jnp.exp2(m - m_new)
    p = jnp.exp2(s - _lanes(m_new, s.shape[1]))
    acc_new = _lanes(alpha, acc.shape[1]) * acc + jnp.dot(p.astype(BF16), va,
                                                          preferred_element_type=F32)
    return m_new, acc_new


def _mla_kernel(qb_ref, kb_ref, fl_ref, q_ref, kn_ref, vn_ref, kp_ref, vp_ref, o_ref,
                acc, m_sc, *, heads, pos0):
    s = pl.program_id(0)
    fl = fl_ref[s]
    is_first = (fl & 1) != 0
    is_last = (fl & 2) != 0
    tq = q_ref.shape[1]

    @pl.when(is_first)
    def _():
        row = lax.broadcasted_iota(jnp.int32, (tq, tq), 0) + pos0
        col = lax.broadcasted_iota(jnp.int32, (tq, tq), 1) + pos0
        mask = (col // CHUNK) <= (row // CHUNK)

        def body(h, _):
            m0 = jnp.full((tq, LANES), -jnp.inf, F32)
            a0 = jnp.zeros((tq, 2 * V_DIM), F32)
            m, a = _mla_block(q_ref[h], kn_ref[h], vn_ref[h], m0, a0, mask)
            m_sc[h] = m
            acc[h] = a
            return 0
        lax.fori_loop(0, heads, body, 0, unroll=min(heads, MLA_UNROLL))

    @pl.when(jnp.logical_not(is_first))
    def _():
        def body(h, _):
            m, a = _mla_block(q_ref[h], kp_ref[h], vp_ref[h], m_sc[h], acc[h], None)
            m_sc[h] = m
            acc[h] = a
            return 0
        lax.fori_loop(0, heads, body, 0, unroll=min(heads, MLA_UNROLL))

    @pl.when(is_last)
    def _():
        for h in range(heads):
            a = acc[h]
            o_ref[:, h * V_DIM:(h + 1) * V_DIM] = (a[:, :V_DIM] / a[:, V_DIM:]).astype(o_ref.dtype)


def _mla_attention(qc, kcn, vn, kcp, vp, tables, *, tq, tk, pos0, out_rows, out_block_of):
    qb, kb, fl = tables
    n_steps = qb.shape[0]
    h = H_MLA
    wqk = 2 * LANES
    new = pl.BlockSpec((h, tq, wqk), lambda s, qb, kb, fl: (0, qb[s], 0))
    past = pl.BlockSpec((h, tk, wqk), lambda s, qb, kb, fl: (0, kb[s], 0))
    grid_spec = pltpu.PrefetchScalarGridSpec(
        num_scalar_prefetch=3,
        grid=(n_steps,),
        in_specs=[new, new, new, past, past],
        out_specs=pl.BlockSpec((tq, h * V_DIM), out_block_of),
        scratch_shapes=[pltpu.VMEM((h, tq, 2 * V_DIM), F32),
                        pltpu.VMEM((h, tq, LANES), F32)],
    )
    kern = functools.partial(_mla_kernel, heads=h, pos0=pos0)
    return pl.pallas_call(
        kern, grid_spec=grid_spec,
        out_shape=jax.ShapeDtypeStruct((out_rows, h * V_DIM), BF16),
        compiler_params=_cparams(1),
    )(qb, kb, fl, qc, kcn, vn, kcp, vp)


def _mla_latent_kernel(qb_ref, kb_ref, fl_ref, q_ref, cn_ref, rn_ref, cp_ref, rp_ref, w_ref,
                       o_ref, qa, qr, acc, m_sc, l_sc, *, heads, pos0):
    s_id = pl.program_id(0)
    fl = fl_ref[s_id]
    is_first = (fl & 1) != 0
    is_last = (fl & 2) != 0
    tq = q_ref.shape[1]
    hw = QK_NOPE + V_DIM

    def step(ck, kr, mask):
        s = _dot_nt(qa[...], ck) + _dot_nt(qr[...], kr)
        if mask is not None:
            s = jnp.where(mask, s, -jnp.inf)
        m_old = m_sc[...]
        m_new = jnp.maximum(m_old, jnp.max(s, axis=1, keepdims=True))
        alpha = jnp.exp2(m_old - m_new)
        p = jnp.exp2(s - m_new[:, :1])
        l_sc[...] = alpha * l_sc[...] + jnp.sum(p, axis=1, keepdims=True)
        acc[...] = alpha[:, :1] * acc[...] + jnp.dot(p.astype(BF16), ck,
                                                     preferred_element_type=F32)
        m_sc[...] = m_new

    @pl.when(is_first)
    def _():
        for h in range(heads):
            qh = q_ref[h]
            w_uk = w_ref[:, h * hw:h * hw + QK_NOPE]
            qa[h * tq:(h + 1) * tq, :] = _dot_nt(qh[:, :QK_NOPE], w_uk).astype(BF16)
            qr[h * tq:(h + 1) * tq, :] = qh[:, QK_NOPE:]
        m_sc[...] = jnp.full(m_sc.shape, -jnp.inf, F32)
        l_sc[...] = jnp.zeros(l_sc.shape, F32)
        acc[...] = jnp.zeros(acc.shape, F32)
        row = lax.broadcasted_iota(jnp.int32, (heads * tq, tq), 0) % tq + pos0
        col = lax.broadcasted_iota(jnp.int32, (heads * tq, tq), 1) + pos0
        step(cn_ref[...], rn_ref[...], (col // CHUNK) <= (row // CHUNK))

    @pl.when(jnp.logical_not(is_first))
    def _():
        step(cp_ref[...].astype(BF16), rp_ref[...], None)

    @pl.when(is_last)
    def _():
        o_lat = (acc[...] / l_sc[...][:, :1]).astype(BF16)
        for h in range(heads):
            w_uv = w_ref[:, h * hw + QK_NOPE:(h + 1) * hw]
            o_ref[:, h * V_DIM:(h + 1) * V_DIM] = jnp.dot(
                o_lat[h * tq:(h + 1) * tq, :], w_uv, preferred_element_type=F32).astype(o_ref.dtype)


def _mla_latent_attention(qc, c_new, r_new, c_past, r_past, w_ukv_b, tables, l, *, tq, tk, pos0,
                          out_rows, out_block_of):
    qb, kb, fl = tables
    h = H_MLA
    kvl = c_new.shape[1]
    wqk = 2 * LANES
    grid_spec = pltpu.PrefetchScalarGridSpec(
        num_scalar_prefetch=3,
        grid=(qb.shape[0],),
        in_specs=[pl.BlockSpec((h, tq, wqk), lambda s, qb, kb, fl: (0, qb[s], 0)),
                  pl.BlockSpec((tq, kvl), lambda s, qb, kb, fl: (qb[s], 0)),
                  pl.BlockSpec((tq, LANES), lambda s, qb, kb, fl: (qb[s], 0)),
                  pl.BlockSpec((tk, kvl), lambda s, qb, kb, fl: (kb[s], 0)),
                  pl.BlockSpec((tk, LANES), lambda s, qb, kb, fl: (kb[s], 0)),
                  pl.BlockSpec((None, kvl, w_ukv_b.shape[2]), lambda s, qb, kb, fl: (l, 0, 0))],
        out_specs=pl.BlockSpec((tq, h * V_DIM), out_block_of),
        scratch_shapes=[pltpu.VMEM((h * tq, kvl), BF16),
                        pltpu.VMEM((h * tq, LANES), BF16),
                        pltpu.VMEM((h * tq, kvl), F32),
                        pltpu.VMEM((h * tq, LANES), F32),
                        pltpu.VMEM((h * tq, LANES), F32)],
    )
    kern = functools.partial(_mla_latent_kernel, heads=h, pos0=pos0)
    return pl.pallas_call(
        kern, grid_spec=grid_spec,
        out_shape=jax.ShapeDtypeStruct((out_rows, h * V_DIM), BF16),
        compiler_params=_cparams(1),
    )(qb, kb, fl, qc, c_new, r_new, c_past, r_past, w_ukv_b)


def _causal_tables(nq, q_off, per_head=False, heads=1, n_batch=1, past_tiles=None,
                   past_stride=0):
    qb, kb, hb, fl = [], [], [], []
    if past_tiles is None:
        for i in range(nq):
            n = i + 1
            for j in range(n):
                qb.append(q_off + i)
                kb.append(max(i - j, 1) - 1 if j == 0 else i - j)
                hb.append(0)
                fl.append((1 if j == 0 else 0) | (2 if j == n - 1 else 0))
    else:
        for b in range(n_batch):
            for h in range(heads if per_head else 1):
                n = 1 + past_tiles
                for j in range(n):
                    qb.append(q_off + b)
                    jj = past_tiles - 1 if j == 0 else past_tiles - j
                    kb.append(past_stride + b * past_tiles + jj)
                    hb.append(h)
                    fl.append((1 if j == 0 else 0) | (2 if j == n - 1 else 0))
    arr = lambda v: jnp.asarray(np.asarray(v, dtype=np.int32))
    return arr(qb), arr(kb), arr(hb), arr(fl)


ROW_DMA_UNROLL = 8


def _moe_kernel(te_ref, nv_ref, tok_ref, dst_ref, h_hbm, w_ref, wg_ref, wu_ref, wd_ref, y_hbm,
                xbuf, obuf, wg_s, wu_s, wd_s, sem_in, sem_out, *, s):
    t = pl.program_id(0)
    tm = w_ref.shape[0]
    nv = nv_ref[0]
    live = t < nv
    slot = lax.rem(t, 2)

    p = _slab_pitch(s)

    def in_copy(tile, r, sl):
        src0 = pl.multiple_of(tok_ref[tile * tm + r] * p, 8)
        return pltpu.make_async_copy(h_hbm.at[pl.ds(src0, s)],
                                     xbuf.at[sl, pl.ds(pl.multiple_of(r * p, 8), s)],
                                     sem_in.at[sl])

    def gather_start(tile, sl):
        def body(r, _):
            in_copy(tile, r, sl).start()
            return 0
        lax.fori_loop(0, tm, body, 0, unroll=ROW_DMA_UNROLL)

    def gather_wait(tile, sl):
        def body(r, _):
            in_copy(tile, r, sl).wait()
            return 0
        lax.fori_loop(0, tm, body, 0, unroll=ROW_DMA_UNROLL)

    @pl.when(live & (t == 0))
    def _():
        gather_start(0, 0)

    @pl.when(t + 1 < nv)
    def _():
        gather_start(t + 1, 1 - slot)

    prev = te_ref[jnp.maximum(t - 1, 0)]
    fresh = (t == 0) | (te_ref[t] != prev)

    @pl.when(live & fresh)
    def _():
        wg_s[...] = wg_ref[...].astype(BF16)
        wu_s[...] = wu_ref[...].astype(BF16)
        wd_s[...] = wd_ref[...].astype(BF16)

    @pl.when(live)
    def _():
        gather_wait(t, slot)
        x = _slab_load(xbuf, 0, tm, s, lead=slot).astype(BF16)
        a = jnp.dot(x, wg_s[...], preferred_element_type=F32)
        u = jnp.dot(x, wu_s[...], preferred_element_type=F32)
        hid = (a * jax.nn.sigmoid(a)) * u * w_ref[...]
        y = jnp.dot(hid.astype(BF16), wd_s[...], preferred_element_type=F32)

        def out_copy(tile, r):
            dst0 = pl.multiple_of(dst_ref[tile * tm + r] * p, 8)
            return pltpu.make_async_copy(obuf.at[pl.ds(pl.multiple_of(r * p, 8), s)],
                                         y_hbm.at[pl.ds(dst0, s)], sem_out)

        def scatter_wait(tile):
            def wait(r, _):
                out_copy(tile, r).wait()
                return 0
            lax.fori_loop(0, tm, wait, 0, unroll=ROW_DMA_UNROLL)

        @pl.when(t > 0)
        def _():
            scatter_wait(t - 1)

        _slab_store(obuf, 0, y)

        def start(r, _):
            out_copy(t, r).start()
            return 0
        lax.fori_loop(0, tm, start, 0, unroll=ROW_DMA_UNROLL)

        @pl.when(t == nv - 1)
        def _():
            scatter_wait(t)


def _moe_experts(h_slab, row_tok, row_dst, row_w, tile_expert, n_valid, w_gate, w_up, w_down, l,
                 n_dest):
    d, f = w_gate.shape[-2:]
    s = d // LANES
    p = _slab_pitch(s)
    tm = MOE_TILE
    n_tiles = row_tok.shape[0] // tm
    wmap = lambda t, te, nv, tok, dst: (l, te[t], 0, 0)
    grid_spec = pltpu.PrefetchScalarGridSpec(
        num_scalar_prefetch=4, grid=(n_tiles,),
        in_specs=[pl.BlockSpec(memory_space=pl.ANY),
                  pl.BlockSpec((tm, 1), lambda t, te, nv, tok, dst: (t, 0)),
                  pl.BlockSpec((None, None, d, f), wmap),
                  pl.BlockSpec((None, None, d, f), wmap),
                  pl.BlockSpec((None, None, f, d), wmap)],
        out_specs=pl.BlockSpec(memory_space=pl.ANY),
        scratch_shapes=[pltpu.VMEM((2, tm * p, LANES), F32), pltpu.VMEM((tm * p, LANES), F32),
                        pltpu.VMEM((d, f), BF16), pltpu.VMEM((d, f), BF16),
                        pltpu.VMEM((f, d), BF16),
                        pltpu.SemaphoreType.DMA((2,)), pltpu.SemaphoreType.DMA(())],
    )
    return pl.pallas_call(
        functools.partial(_moe_kernel, s=s), grid_spec=grid_spec,
        out_shape=jax.ShapeDtypeStruct(((n_dest + tm) * p, LANES), F32),
        compiler_params=_cparams(1),
    )(tile_expert, n_valid, row_tok, row_dst, h_slab, row_w, w_gate, w_up, w_down)


def _combine_kernel(x_ref, y0_ref, y1_ref, g_ref, o_ref):
    gb, grp, d = x_ref.shape
    s = d // LANES
    for g in range(gb):
        y = _slab_load(y0_ref, g * grp, grp, s) + _slab_load(y1_ref, g * grp, grp, s)
        o_ref[g] = x_ref[g] + g_ref[g] * y


def _combine(x3, yg, modg, l, g_i):
    ng, grp, d = x3.shape
    s = d // LANES
    gb = _row_tile(ng, 2)
    blk = pl.BlockSpec((gb, grp, d), lambda i: (i, 0, 0))
    yblk = lambda off: pl.BlockSpec((gb * grp * _slab_pitch(s), LANES), lambda i: (i + off, 0))
    return pl.pallas_call(
        _combine_kernel,
        grid=(ng // gb,),
        in_specs=[blk, yblk(0), yblk(ng // gb), _mod_spec(g_i, gb, d)],
        out_specs=blk,
        out_shape=jax.ShapeDtypeStruct(x3.shape, F32),
        compiler_params=_cparams(1),
    )(x3, yg, yg, modg)


def _dispatch(route, n):
    tm = MOE_TILE
    r_cap = (2 * n + N_EXPERTS * (tm - 1) + tm - 1) // tm * tm
    e = route[:, :2].astype(jnp.int32)
    w = route[:, 2:4]
    flat_e = e.T.reshape(-1)
    flat_w = w.T.reshape(-1)
    order = jnp.argsort(flat_e, stable=True).astype(jnp.int32)
    bounds = jnp.searchsorted(flat_e[order], jnp.arange(N_EXPERTS + 1, dtype=jnp.int32),
                              side="left").astype(jnp.int32)
    counts = bounds[1:] - bounds[:-1]
    padded = (counts + tm - 1) // tm * tm
    ends_p = jnp.cumsum(padded)
    starts_p = ends_p - padded
    starts = jnp.cumsum(counts) - counts
    tile_start = jnp.arange(r_cap // tm, dtype=jnp.int32) * tm
    tile_expert = jnp.minimum(jnp.searchsorted(ends_p, tile_start, side="right"),
                              N_EXPERTS - 1).astype(jnp.int32)
    n_valid = (ends_p[-1:] // tm).astype(jnp.int32)
    row_e = jnp.repeat(tile_expert, tm)
    local = jnp.arange(r_cap, dtype=jnp.int32) - starts_p[row_e]
    valid = local < counts[row_e]
    slot = order[jnp.clip(starts[row_e] + local, 0, 2 * n - 1)]
    row_dst = jnp.where(valid, slot, 2 * n + jnp.arange(r_cap, dtype=jnp.int32) % tm)
    row_tok = jnp.where(valid, jnp.where(slot >= n, slot - n, slot), 0)
    row_w = jnp.where(valid, flat_w[slot], 0.0)
    return row_tok, row_dst, row_w.reshape(r_cap, 1), tile_expert, n_valid


def _rope_tables(pos):
    inv = ROPE_THETA ** (-jnp.arange(0, QK_ROPE, 2, dtype=F32) / QK_ROPE)
    ang = pos.astype(F32)[:, None] * inv[None, :]
    cos, sin = jnp.cos(ang), jnp.sin(ang)
    pad = jnp.zeros((pos.shape[0], LANES - QK_ROPE), F32)
    return (jnp.concatenate([cos, cos, pad], axis=1),
            jnp.concatenate([-sin, sin, pad], axis=1))


def _rope_lanes(acc, c, s):
    return acc * c + pltpu.roll(acc, LANES - QK_ROPE, axis=1) * s


def _swap_halves(w):
    half = w.shape[-1] // 2
    return jnp.concatenate([w[..., half:], w[..., :half]], axis=-1)


def kernel(x_prompt, x_sample, c_prompt, c_sample, cache_sb_k, cache_sb_v, cache_mla_ckv,
           cache_mla_krope, w_ada, b_ada, g_norm_mix, g_norm_ffn, w_in, g_q_lat, g_kv_lat,
           w_uq, w_ukv, w_branch_sb, w_branch_mla, w_out, w_router_group, b_router_group,
           w_router_expert, b_router_expert, w_exp_gate, w_exp_up, w_exp_down, g_final):
    bp, t_p, d = x_prompt.shape
    bs, t_s, _ = x_sample.shape
    depth = w_in.shape[0]
    past = cache_sb_k.shape[2]
    grp = t_s
    n_p, n_s = bp * t_p, bs * t_s
    n = n_p + n_s
    ng = n // grp
    sb_w = H_SB * DH_SB
    q_lora = g_q_lat.shape[1]
    kv_lora = g_kv_lat.shape[1]
    tm = _token_tile(n, grp)
    gpt = tm // grp
    tn = 512
    assert bp == 1 and t_p % ATTN_TILE == 0 and t_p % grp == 0 and past % 512 == 0

    x3 = jnp.concatenate([x_prompt.reshape(n_p // grp, grp, d), x_sample], axis=0)

    n_c = bp + bs
    c_rows = 16
    c_all = jnp.zeros((c_rows, d), F32).at[:n_c].set(jnp.concatenate([c_prompt, c_sample], 0))
    n_modc = N_MOD * d

    def ada_epi(accs, ex, outs):
        outs[0][...] = accs[0] + ex[0][...]

    mods = []
    for l in range(depth):
        mods.append(_matmul(
            [(c_all, w_ada, _wspec(l, d, tn, 0), True)], m=c_rows, n_out=n_modc, tm=c_rows, tn=tn,
            prologue=lambda a, ex: a * jax.nn.sigmoid(a),
            epilogue=ada_epi,
            extras=[b_ada.reshape(depth, 1, n_modc)],
            extra_specs=[pl.BlockSpec((None, 1, tn), lambda j, i, l=l: (l, 0, j))],
            out_shape=[jax.ShapeDtypeStruct((c_rows, n_modc), F32)],
            out_specs=[_spec2(c_rows, tn)])[0])
    mod = jnp.stack(mods)
    modg = jnp.concatenate(
        [jnp.broadcast_to(mod[:, :bp], (depth, n_p // grp, n_modc)), mod[:, bp:n_c]], axis=1)
    modg = modg.reshape(depth, ng, N_MOD, d).transpose(0, 2, 1, 3)
    modg = modg.reshape(depth * N_MOD, ng, 1, d)

    def mod_idx(l, k):
        return l * N_MOD + k

    def mspec_rows(l, k):
        return pl.BlockSpec((None, gpt, 1, tn), lambda j, i: (mod_idx(l, k), i, 0, j))

    pos = jnp.concatenate([jnp.arange(t_p, dtype=jnp.int32),
                           jnp.tile(past + jnp.arange(t_s, dtype=jnp.int32), bs)])
    rope_c, rope_s = _rope_tables(pos)
    tq = ATTN_TILE
    nq_p = t_p // tq
    tab_p = _causal_tables(nq_p, 0)
    tk_s = 512
    pt = past // tk_s
    tab_sb_s = lambda l: _causal_tables(0, n_p // t_s, n_batch=bs, past_tiles=pt,
                                        past_stride=l * bs * pt)
    tab_mla_s = _causal_tables(0, n_p // t_s, n_batch=bs, past_tiles=pt)

    off_q, off_k, off_v = 0, sb_w, 2 * sb_w
    off_cq = 3 * sb_w
    off_ckv = off_cq + q_lora
    off_kr = off_ckv + kv_lora
    off_g = off_kr + QK_ROPE
    w_kr = w_in[:, :, off_kr:off_kr + QK_ROPE]
    w_kr_aug = jnp.concatenate([w_kr, _swap_halves(w_kr)], axis=-1).astype(BF16)
    w_gates = w_in[:, :, off_g:].astype(BF16)
    uq = w_uq.reshape(depth, q_lora, H_MLA, QK_NOPE + QK_ROPE)
    uq_r = uq[..., QK_NOPE:]
    w_uq_cat = jnp.concatenate([uq[..., :QK_NOPE], uq_r, _swap_halves(uq_r)], axis=-1)
    w_uq_cat = w_uq_cat.reshape(depth, q_lora, H_MLA * 2 * LANES)
    w_router = jnp.concatenate(
        [w_router_expert, w_router_group,
         jnp.zeros((depth, d, LANES - N_EXPERTS - N_GROUPS), F32)], axis=-1)
    r_hi = w_router.astype(BF16)
    r_res = w_router - r_hi.astype(F32)
    r_mid = r_res.astype(BF16)
    r_lo = (r_res - r_mid.astype(F32)).astype(BF16)
    w_router3 = jnp.stack([r_hi, r_mid, r_lo], axis=1)
    b_router = jnp.concatenate(
        [b_router_expert, b_router_group,
         jnp.zeros((depth, LANES - N_EXPERTS - N_GROUPS), F32)], axis=-1).reshape(depth, 1, LANES)

    rows_past = bs * past
    kr_past = jnp.pad(cache_mla_krope.reshape(depth * rows_past, QK_ROPE),
                      ((0, 0), (0, LANES - QK_ROPE))).astype(BF16)
    w_ukv_b = w_ukv.astype(BF16)
    hm = lambda rows: jax.ShapeDtypeStruct((H_SB, rows, LANES), BF16)
    new_k, new_v, new_c, new_r = [], [], [], []

    for l in range(depth):
        h = _norm_mod(x3, g_norm_mix.reshape(depth, 1, d), modg, l, mod_idx(l, 1), mod_idx(l, 0))
        h = h.reshape(n, d)

        sbq_scale = DH_SB ** -0.5 * float(np.log2(np.e))

        def plain_hm(accs, ex, outs):
            _store_heads(outs[0], accs[0] * sbq_scale)

        def f32_and_hm(accs, ex, outs):
            outs[0][...] = accs[0]
            _store_heads(outs[1], accs[0])

        sb_q = _matmul([(h, w_in, _wspec(l, d, tn, off_q), True)], m=n, n_out=sb_w, tm=tm, tn=tn,
                       epilogue=plain_hm, out_shape=[hm(n)], out_specs=[_hm_spec(tm, tn)])[0]
        k_f32, sb_k = _matmul([(h, w_in, _wspec(l, d, tn, off_k), True)], m=n, n_out=sb_w, tm=tm,
                              tn=tn, epilogue=f32_and_hm,
                              out_shape=[jax.ShapeDtypeStruct((n, sb_w), F32), hm(n)],
                              out_specs=[_spec2(tm, tn), _hm_spec(tm, tn)])
        v_f32, sb_v = _matmul([(h, w_in, _wspec(l, d, tn, off_v), True)], m=n, n_out=sb_w, tm=tm,
                              tn=tn, epilogue=f32_and_hm,
                              out_shape=[jax.ShapeDtypeStruct((n, sb_w), F32), hm(n)],
                              out_specs=[_spec2(tm, tn), _hm_spec(tm, tn)])

        def plain_f32(accs, ex, outs):
            outs[0][...] = accs[0]

        c_q = _matmul([(h, w_in, _wspec(l, d, tn, off_cq), True)], m=n, n_out=q_lora, tm=tm, tn=tn,
                      epilogue=plain_f32, out_shape=[jax.ShapeDtypeStruct((n, q_lora), F32)],
                      out_specs=[_spec2(tm, tn)])[0]

        def ckv_epi(accs, ex, outs):
            y = _rms(accs[0], ex[0][...])
            outs[0][...] = y
            outs[1][...] = y.astype(BF16)

        c_kv, c_kv_b = _matmul(
            [(h, w_in, _wspec(l, d, kv_lora, off_ckv), True)], m=n, n_out=kv_lora, tm=tm, tn=kv_lora,
            epilogue=ckv_epi, extras=[g_kv_lat.reshape(depth, 1, kv_lora)],
            extra_specs=[pl.BlockSpec((None, 1, kv_lora), lambda j, i: (l, 0, 0))],
            out_shape=[jax.ShapeDtypeStruct((n, kv_lora), F32),
                       jax.ShapeDtypeStruct((n, kv_lora), BF16)],
            out_specs=[_spec2(tm, kv_lora), _spec2(tm, kv_lora)])

        def kr_epi(accs, ex, outs):
            r = _rope_lanes(accs[0], ex[0][...], ex[1][...])
            outs[0][...] = r
            outs[1][...] = r.astype(BF16)

        rope_specs = [pl.BlockSpec((tm, LANES), lambda j, i: (i, 0))] * 2
        k_r, k_r_b = _matmul(
            [(h, w_kr_aug, pl.BlockSpec((None, d, LANES), lambda j, i: (l, 0, 0)), False)],
            m=n, n_out=LANES, tm=tm, tn=LANES, epilogue=kr_epi,
            extras=[rope_c, rope_s], extra_specs=rope_specs,
            out_shape=[jax.ShapeDtypeStruct((n, LANES), F32), jax.ShapeDtypeStruct((n, LANES), BF16)],
            out_specs=[_spec2(tm, LANES), _spec2(tm, LANES)])

        def gate_epi(accs, ex, outs):
            outs[0][...] = jax.nn.sigmoid(accs[0])

        gates = _matmul(
            [(h, w_gates, pl.BlockSpec((None, d, tn), lambda j, i: (l, 0, j)), False)],
            m=n, n_out=2 * d, tm=tm, tn=tn, epilogue=gate_epi,
            out_shape=[jax.ShapeDtypeStruct((n, 2 * d), F32)], out_specs=[_spec2(tm, tn)])[0]

        def cq_prologue(a, ex):
            return _rms(a, ex[0][...])

        gq_spec = pl.BlockSpec((None, 1, q_lora), lambda j, i: (l, 0, 0))
        gq = g_q_lat.reshape(depth, 1, q_lora)
        wqk = 2 * LANES
        hm_qk = lambda rows: jax.ShapeDtypeStruct((H_MLA, rows, wqk), BF16)

        q_scale = (QK_NOPE + QK_ROPE) ** -0.5 * float(np.log2(np.e))

        def qcat_epi(accs, ex, outs):
            c, s = ex[1][...], ex[2][...]
            for hh in range(tn // wqk):
                blk = accs[0][:, hh * wqk:(hh + 1) * wqk]
                outs[0][hh, :, :LANES] = (blk[:, :LANES] * q_scale).astype(BF16)
                outs[0][hh, :, LANES:] = (_rope_lanes(blk[:, LANES:], c, s) * q_scale).astype(BF16)

        q_cat = _matmul(
            [(c_q, w_uq_cat, pl.BlockSpec((None, q_lora, tn), lambda j, i: (l, 0, j)), True)],
            m=n, n_out=H_MLA * wqk, tm=tm, tn=tn, prologue=cq_prologue, epilogue=qcat_epi,
            extras=[gq, rope_c, rope_s], extra_specs=[gq_spec] + rope_specs,
            out_shape=[hm_qk(n)],
            out_specs=[pl.BlockSpec((tn // wqk, tm, wqk), lambda j, i: (j, i, 0))])[0]

        tn_kv = min(2048, H_MLA * wqk)
        hpt = tn_kv // wqk

        def kv_epi(accs, ex, outs):
            kr = ex[0][...]
            ones = jnp.ones(kr.shape, BF16)
            for hh in range(hpt):
                outs[0][hh, :, :LANES] = accs[0][:, hh * wqk:hh * wqk + LANES].astype(BF16)
                outs[0][hh, :, LANES:] = kr
                outs[1][hh, :, :LANES] = accs[0][:, hh * wqk + LANES:(hh + 1) * wqk].astype(BF16)
                outs[1][hh, :, LANES:] = ones

        def up_kv(a, kr, rows, tmr):
            hspec = pl.BlockSpec((hpt, tmr, wqk), lambda j, i: (j, i, 0))
            return _matmul(
                [(a, w_ukv, pl.BlockSpec((None, kv_lora, tn_kv), lambda j, i: (l, 0, j)), True)],
                m=rows, n_out=H_MLA * wqk, tm=tmr, tn=tn_kv, epilogue=kv_epi, extras=[kr],
                extra_specs=[pl.BlockSpec((tmr, LANES), lambda j, i: (i, 0))],
                out_shape=[hm_qk(rows), hm_qk(rows)], out_specs=[hspec, hspec])

        kc_new, v_new = up_kv(c_kv_b, k_r_b, n, tm)

        o_sb_p = _sb_stream_attention(sb_q, sb_k, sb_v, sb_k, sb_v, tq=tq, tk=tq, q_block0=0,
                                      n_tiles=n_p // tq)
        cache_k2 = cache_sb_k.reshape(depth * bs * past * H_SB, DH_SB)
        cache_v2 = cache_sb_v.reshape(depth * bs * past * H_SB, DH_SB)
        q_off_s = n_p // t_s
        o_sb_s = _sb_stream_attention(sb_q, sb_k, sb_v, cache_k2, cache_v2, tq=t_s, tk=ATTN_TILE,
                                      q_block0=q_off_s, n_tiles=bs,
                                      cache_tiles=past // ATTN_TILE,
                                      cache_base=l * bs * (past // ATTN_TILE))
        o_sb = jnp.concatenate([o_sb_p, o_sb_s], axis=0)

        tp3 = (tab_p[0], tab_p[1], tab_p[3])
        o_mla_p = _mla_attention(q_cat, kc_new, v_new, kc_new, v_new, tp3,
                                 tq=tq, tk=tq, pos0=0, out_rows=n_p,
                                 out_block_of=lambda s, qb, kb, fl: (qb[s], 0))
        tab_s = tab_sb_s(l)
        o_mla_s = _mla_latent_attention(
            q_cat, c_kv_b, k_r_b, cache_mla_ckv.reshape(depth * rows_past, kv_lora), kr_past,
            w_ukv_b, (tab_s[0], tab_s[1], tab_s[3]), l, tq=t_s, tk=tk_s, pos0=past, out_rows=n_s,
            out_block_of=lambda s, qb, kb, fl: (qb[s] - q_off_s, 0))
        o_mla = jnp.concatenate([o_mla_p, o_mla_s], axis=0)

        def merge_epi(accs, ex, outs):
            outs[0][...] = (ex[0][...] * accs[0] + ex[1][...] * accs[1]).astype(BF16)

        merged = _matmul(
            [(o_sb, w_branch_sb, _wspec(l, sb_w, tn, 0), True),
             (o_mla, w_branch_mla, _wspec(l, H_MLA * V_DIM, tn, 0), True)],
            m=n, n_out=d, tm=tm, tn=tn, epilogue=merge_epi,
            extras=[gates, gates],
            extra_specs=[pl.BlockSpec((tm, tn), lambda j, i: (i, j)),
                         pl.BlockSpec((tm, tn), lambda j, i: (i, j + d // tn))],
            out_shape=[jax.ShapeDtypeStruct((n, d), BF16)], out_specs=[_spec2(tm, tn)])[0]

        def resid_epi(accs, ex, outs):
            for g in range(gpt):
                outs[0][g] = ex[0][g] + ex[1][g] * accs[0][g * grp:(g + 1) * grp, :]

        x_spec = pl.BlockSpec((gpt, grp, tn), lambda j, i: (i, 0, j))
        x3 = _matmul(
            [(merged, w_out, _wspec(l, d, tn, 0), True)], m=n, n_out=d, tm=tm, tn=tn,
            epilogue=resid_epi, extras=[x3, modg], extra_specs=[x_spec, mspec_rows(l, 2)],
            out_shape=[jax.ShapeDtypeStruct((ng, grp, d), F32)], out_specs=[x_spec])[0]

        h2, route = _norm_route(x3, g_norm_ffn.reshape(depth, 1, d), modg, l, mod_idx(l, 4),
                                mod_idx(l, 3), w_router3, b_router)
        row_tok, row_dst, row_w, tile_expert, n_valid = _dispatch(route.reshape(n, LANES), n)
        yg = _moe_experts(h2, row_tok, row_dst, row_w, tile_expert, n_valid,
                          w_exp_gate, w_exp_up, w_exp_down, l, 2 * n)
        x3 = _combine(x3, yg, modg, l, mod_idx(l, 5))

        new_k.append(k_f32)
        new_v.append(v_f32)
        new_c.append(c_kv)
        new_r.append(k_r[:, :QK_ROPE])

    y_p, y_s = _final_norm(x3, g_final.reshape(1, d), n_p // grp)

    def split(parts, tail):
        a = jnp.stack(parts)
        return (a[:, :n_p].reshape((depth, bp, t_p) + tail),
                a[:, n_p:].reshape((depth, bs, t_s) + tail))

    pk, sk = split(new_k, (H_SB, DH_SB))
    pv, sv = split(new_v, (H_SB, DH_SB))
    pc, sc = split(new_c, (kv_lora,))
    pr, sr = split(new_r, (QK_ROPE,))
    return (y_p.reshape(bp, t_p, d), y_s.reshape(bs, t_s, d), pk, pv, pc, pr, sk, sv, sc, sr)
```

```python
import functools

import numpy as np
import jax
import jax.numpy as jnp
from jax import lax
from jax.experimental import pallas as pl
from jax.experimental.pallas import tpu as pltpu

F32 = jnp.float32
BF16 = jnp.bfloat16

CHUNK = 64
H_SB = 16
DH_SB = 128
H_MLA = 16
QK_NOPE = 128
QK_ROPE = 64
V_DIM = 128
ROPE_THETA = 10000.0
N_GROUPS = 4
EXPERTS_PER_GROUP = 8
N_EXPERTS = N_GROUPS * EXPERTS_PER_GROUP
N_MOD = 6
EPS = 1e-6

LANES = 128
ATTN_TILE = 256
MOE_TILE = 256
HEAD_UNROLL = 4
MLA_UNROLL = 16
SB_DEAD = 152.0
VMEM_LIMIT = 56 * 1024 * 1024


def _cparams(n_axes, vmem=VMEM_LIMIT):
    return pltpu.CompilerParams(dimension_semantics=("arbitrary",) * n_axes,
                                vmem_limit_bytes=vmem)


def _row_tile(n, cap=512):
    t = cap
    while n % t:
        t //= 2
    return t


def _token_tile(n, grp, cap=1152):
    return max(t for t in range(grp, cap + 1, grp) if n % t == 0)


def _matmul(pairs, *, m, n_out, tm, tn, epilogue, out_shape, out_specs,
            extras=(), extra_specs=(), prologue=None):
    n_pairs = len(pairs)
    n_ex = len(extras)
    n_outs = len(out_shape)
    pairs = [tuple(p) + (False,) * (5 - len(p)) for p in pairs]
    cast = [p[3] for p in pairs]
    b_nk = [p[4] for p in pairs]

    def kern(*refs):
        a_refs = refs[0:2 * n_pairs:2]
        b_refs = refs[1:2 * n_pairs:2]
        ex = refs[2 * n_pairs:2 * n_pairs + n_ex]
        outs = refs[2 * n_pairs + n_ex:2 * n_pairs + n_ex + n_outs]
        scr = refs[2 * n_pairs + n_ex + n_outs:]
        i = pl.program_id(1)
        accs = []
        si = 0
        for p in range(n_pairs):
            if cast[p]:
                bsc = scr[si]
                si += 1

                @pl.when(i == 0)
                def _(bsc=bsc, b_ref=b_refs[p]):
                    bsc[...] = b_ref[...].astype(BF16)

                bv = bsc[...]
            else:
                bv = b_refs[p][...]
            a = a_refs[p][...]
            if prologue is not None:
                a = prologue(a, ex)
            if b_nk[p]:
                accs.append(_dot_nt(a.astype(BF16), bv))
            else:
                accs.append(jnp.dot(a.astype(BF16), bv, preferred_element_type=F32))
        epilogue(accs, ex, outs)

    in_specs, args, scratch = [], [], []
    for (a, b, b_spec, cb, nk) in pairs:
        k = a.shape[1]
        in_specs += [pl.BlockSpec((tm, k), lambda j, i: (i, 0)), b_spec]
        args += [a, b]
        if cb:
            scratch.append(pltpu.VMEM((tn, k) if nk else (k, tn), BF16))
    in_specs += list(extra_specs)
    args += list(extras)
    return pl.pallas_call(
        kern,
        grid=(n_out // tn, m // tm),
        in_specs=in_specs,
        out_specs=out_specs,
        out_shape=out_shape,
        scratch_shapes=scratch,
        compiler_params=_cparams(2),
    )(*args)


def _wspec(l, k, tn, col_off):
    cb = col_off // tn
    assert cb * tn == col_off
    return pl.BlockSpec((None, k, tn), lambda j, i: (l, 0, cb + j))


def _wspec_nk(l, k, tn, row_off):
    rb = row_off // tn
    assert rb * tn == row_off
    return pl.BlockSpec((None, tn, k), lambda j, i: (l, rb + j, 0))


def _spec2(tm, tn):
    return pl.BlockSpec((tm, tn), lambda j, i: (i, j))


def _hm_spec(tm, tn):
    return pl.BlockSpec((tn // LANES, tm, LANES), lambda j, i: (j, i, 0))


def _store_heads(o_ref, val):
    for c in range(val.shape[1] // LANES):
        o_ref[c] = val[:, c * LANES:(c + 1) * LANES].astype(o_ref.dtype)


def _rms(x, g):
    return x * lax.rsqrt(jnp.mean(x * x, axis=-1, keepdims=True) + EPS) * g


def _norm_mod_kernel(x_ref, g_ref, sc_ref, sh_ref, o_ref):
    x = x_ref[...]
    y = _rms(x, g_ref[...])
    o_ref[...] = (y * (1.0 + sc_ref[...]) + sh_ref[...]).astype(o_ref.dtype)


def _split3(x):
    hi = x.astype(BF16)
    r = x - hi.astype(F32)
    mid = r.astype(BF16)
    lo = (r - mid.astype(F32)).astype(BF16)
    return hi, mid, lo


def _dot_f32(a, b3):
    a_hi, a_mid, a_lo = _split3(a)
    b_hi, b_mid, b_lo = b3
    d = functools.partial(jnp.dot, preferred_element_type=F32)
    small = d(a_hi, b_lo) + d(a_lo, b_hi) + d(a_mid, b_mid)
    return (d(a_hi, b_hi) + (d(a_hi, b_mid) + d(a_mid, b_hi))) + small


def _route(logits):
    lane = lax.broadcasted_iota(jnp.int32, logits.shape, 1)
    lanef = lane.astype(F32)
    big = jnp.float32(1e9)
    ninf = jnp.float32(-jnp.inf)
    is_g = (lane >= N_EXPERTS) & (lane < N_EXPERTS + N_GROUPS)
    gl = jnp.where(is_g, logits, ninf)
    gmax = jnp.max(gl, axis=1, keepdims=True)
    g_idx = jnp.min(jnp.where(gl == gmax, lanef - N_EXPERTS, big), axis=1, keepdims=True)
    p_group = 1.0 / jnp.sum(jnp.where(is_g, jnp.exp(gl - gmax), 0.0), axis=1, keepdims=True)
    grp = jnp.floor(lanef * (1.0 / EXPERTS_PER_GROUP))
    in_g = (lane < N_EXPERTS) & (grp == g_idx)
    el = jnp.where(in_g, logits, ninf)
    e1 = jnp.max(el, axis=1, keepdims=True)
    i1 = jnp.min(jnp.where(el == e1, lanef, big), axis=1, keepdims=True)
    el2 = jnp.where(lanef == i1, ninf, el)
    e2 = jnp.max(el2, axis=1, keepdims=True)
    i2 = jnp.min(jnp.where(el2 == e2, lanef, big), axis=1, keepdims=True)
    t = jnp.exp(e2 - e1)
    den = 1.0 + t
    w1 = (1.0 / den) * p_group
    w2 = (t / den) * p_group
    out = jnp.where(lane == 0, i1, jnp.where(lane == 1, i2,
          jnp.where(lane == 2, w1, jnp.where(lane == 3, w2, 0.0))))
    return out


def _slab_pitch(s):
    return s + 8


def _slab_store(ref, row0, val):
    rows, d = val.shape
    s = d // LANES
    p = _slab_pitch(s)
    for c in range(s):
        ref[pl.ds(row0 * p + c, rows, stride=p), :] = val[:, c * LANES:(c + 1) * LANES]


def _slab_load(ref, row0, rows, s, lead=None):
    pieces = []
    p = _slab_pitch(s)
    for c in range(s):
        rs = pl.ds(row0 * p + c, rows, stride=p)
        pieces.append(ref[rs, :] if lead is None else ref[lead, rs, :])
    return jnp.concatenate(pieces, axis=1)


def _norm_route_kernel(x_ref, g_ref, sc_ref, sh_ref, wr_ref, br_ref, h_ref, r_ref):
    x = x_ref[...]
    y = _rms(x, g_ref[...])
    h = y * (1.0 + sc_ref[...]) + sh_ref[...]
    b3 = (wr_ref[0], wr_ref[1], wr_ref[2])
    grp = x.shape[1]
    for g in range(x.shape[0]):
        _slab_store(h_ref, g * grp, h[g])
        logits = _dot_f32(h[g], b3) + br_ref[...]
        r_ref[g] = _route(logits)


def _mod_spec(idx, gb, d):
    return pl.BlockSpec((None, gb, 1, d), lambda i: (idx, i, 0, 0))


def _norm_mod(x3, g, modg, l, sc_i, sh_i):
    ng, grp, d = x3.shape
    gb = _row_tile(ng, 4)
    return pl.pallas_call(
        _norm_mod_kernel,
        grid=(ng // gb,),
        in_specs=[pl.BlockSpec((gb, grp, d), lambda i: (i, 0, 0)),
                  pl.BlockSpec((None, 1, d), lambda i: (l, 0, 0)),
                  _mod_spec(sc_i, gb, d), _mod_spec(sh_i, gb, d)],
        out_specs=pl.BlockSpec((gb, grp, d), lambda i: (i, 0, 0)),
        out_shape=jax.ShapeDtypeStruct(x3.shape, BF16),
        compiler_params=_cparams(1),
    )(x3, g, modg, modg)


def _norm_route(x3, g, modg, l, sc_i, sh_i, wr3, br):
    ng, grp, d = x3.shape
    gb = _row_tile(ng, 4)
    return pl.pallas_call(
        _norm_route_kernel,
        grid=(ng // gb,),
        in_specs=[pl.BlockSpec((gb, grp, d), lambda i: (i, 0, 0)),
                  pl.BlockSpec((None, 1, d), lambda i: (l, 0, 0)),
                  _mod_spec(sc_i, gb, d), _mod_spec(sh_i, gb, d),
                  pl.BlockSpec((None, 3, d, LANES), lambda i: (l, 0, 0, 0)),
                  pl.BlockSpec((None, 1, LANES), lambda i: (l, 0, 0))],
        out_specs=[pl.BlockSpec((gb * grp * _slab_pitch(d // LANES), LANES), lambda i: (i, 0)),
                   pl.BlockSpec((gb, grp, LANES), lambda i: (i, 0, 0))],
        out_shape=[jax.ShapeDtypeStruct((ng * grp * _slab_pitch(d // LANES), LANES), F32),
                   jax.ShapeDtypeStruct((ng, grp, LANES), F32)],
        compiler_params=_cparams(1),
    )(x3, g, modg, modg, wr3, br)


def _final_norm_kernel(x_ref, g_ref, op_ref, os_ref, *, n_prompt_blocks):
    i = pl.program_id(0)
    y = _rms(x_ref[...], g_ref[...])

    @pl.when(i < n_prompt_blocks)
    def _():
        op_ref[...] = y

    @pl.when(i >= n_prompt_blocks)
    def _():
        os_ref[...] = y


def _final_norm(x3, g, ng_prompt):
    ng, grp, d = x3.shape
    gb = _row_tile(np.gcd(ng_prompt, ng - ng_prompt), 4)
    npb = ng_prompt // gb
    blk = lambda f: pl.BlockSpec((gb, grp, d), f)
    return pl.pallas_call(
        functools.partial(_final_norm_kernel, n_prompt_blocks=npb),
        grid=(ng // gb,),
        in_specs=[blk(lambda i: (i, 0, 0)), pl.BlockSpec((1, d), lambda i: (0, 0))],
        out_specs=[blk(lambda i: (jnp.minimum(i, npb - 1), 0, 0)),
                   blk(lambda i: (jnp.maximum(i - npb, 0), 0, 0))],
        out_shape=[jax.ShapeDtypeStruct((ng_prompt, grp, d), F32),
                   jax.ShapeDtypeStruct((ng - ng_prompt, grp, d), F32)],
        compiler_params=_cparams(1),
    )(x3, g)


def _lanes(c, w):
    if w % LANES == 0:
        return c if w == LANES else jnp.tile(c, (1, w // LANES))
    return c[:, :w]


def _dot_nt(a, b):
    return lax.dot_general(a, b, (((1,), (1,)), ((), ())), preferred_element_type=F32)


def _sb_block(qh, kh, vh, c, u, masked):
    w = kh.shape[0]
    z = _dot_nt(qh, kh)
    sp = jnp.maximum(z, 0.0) + jnp.log2(1.0 + jnp.exp2(-jnp.abs(z)))
    if masked:
        row = lax.broadcasted_iota(jnp.int32, z.shape, 0)
        col = lax.broadcasted_iota(jnp.int32, z.shape, 1)
        valid = col < row
        sp = jnp.where(valid, sp, 0.0)
    hi = sp.astype(BF16)
    lo = (sp - hi.astype(F32)).astype(BF16)
    cs = jnp.dot(hi, u, preferred_element_type=F32) + jnp.dot(lo, u, preferred_element_type=F32)
    wgt = jnp.exp2(z - sp - cs - _lanes(c, w))
    if masked:
        wgt = jnp.where(valid, wgt, 0.0)
    o = jnp.dot(wgt.astype(BF16), vh, preferred_element_type=F32)
    c_new = c + jnp.sum(sp, axis=1, keepdims=True)
    return o, c_new


def _suffix_matrix(w):
    j = np.arange(w)[:, None]
    s = np.arange(w)[None, :]
    return jnp.asarray((j > s).astype(np.float32), dtype=BF16)


def _sb_stream_kernel(q_ref, kn_ref, vn_ref, k_hbm, v_hbm, ud_ref, up_ref, o_ref,
                      kbuf, vbuf, acc, carry, done, alive_ref, sem, *, heads, tk, cache_tiles,
                      cache_base):
    i = pl.program_id(0)
    n_past = i if cache_tiles is None else cache_tiles

    def fetch(j, slot):
        if cache_tiles is None:
            rows = pl.ds(pl.multiple_of(j * tk, tk), tk)
            src_k, src_v = k_hbm.at[:, rows, :], v_hbm.at[:, rows, :]
        else:
            blk = tk * heads
            rows = pl.ds(pl.multiple_of((cache_base + i * cache_tiles + j) * blk, blk), blk)
            src_k, src_v = k_hbm.at[rows], v_hbm.at[rows]
        return (pltpu.make_async_copy(src_k, kbuf.at[slot], sem.at[0, slot]),
                pltpu.make_async_copy(src_v, vbuf.at[slot], sem.at[1, slot]))

    def tile_of(buf, slot, h):
        if cache_tiles is None:
            return buf[slot, h]
        return buf.at[slot][pl.ds(h, tk, stride=heads), :].astype(BF16)

    @pl.when(n_past > 0)
    def _():
        for c in fetch(n_past - 1, 0):
            c.start()

    def diag(h, _):
        o, c = _sb_block(q_ref[h], kn_ref[h], vn_ref[h], jnp.zeros(carry.shape[1:], F32),
                         ud_ref[...], True)
        acc[h] = o
        carry[h] = c
        done[h] = 0
        return 0
    lax.fori_loop(0, heads, diag, 0, unroll=min(heads, HEAD_UNROLL))

    alive_ref[0] = heads

    @pl.when(n_past > 0)
    def _():
        for c in fetch(n_past - 1, 0):
            c.wait()

        @pl.when(n_past > 1)
        def _():
            for c in fetch(n_past - 2, 1):
                c.start()

        def first(h, alive):
            o, c = _sb_block(q_ref[h], tile_of(kbuf, 0, h), tile_of(vbuf, 0, h), carry[h],
                             up_ref[...], False)
            acc[h] = acc[h] + o
            carry[h] = c
            dead = (jnp.min(c) >= SB_DEAD).astype(jnp.int32)
            done[h] = dead
            return alive + 1 - dead
        alive_ref[0] = lax.fori_loop(0, heads, first, 0, unroll=min(heads, HEAD_UNROLL))

    def cond(state):
        j, alive = state
        return (j >= 0) & (alive > 0)

    def body(state):
        j, _ = state
        slot = lax.rem(n_past - 1 - j, 2)
        for c in fetch(j, slot):
            c.wait()

        @pl.when(j > 0)
        def _():
            for c in fetch(j - 1, 1 - slot):
                c.start()

        def head(h, alive):
            @pl.when(done[h] == 0)
            def _():
                o, c = _sb_block(q_ref[h], tile_of(kbuf, slot, h), tile_of(vbuf, slot, h),
                                 carry[h], up_ref[...], False)
                acc[h] = acc[h] + o
                carry[h] = c
                done[h] = (jnp.min(c) >= SB_DEAD).astype(jnp.int32)
            return alive + 1 - done[h]
        return j - 1, lax.fori_loop(0, heads, head, 0)

    j_end, _ = lax.while_loop(cond, body, (jnp.int32(n_past - 2), alive_ref[0]))

    @pl.when(j_end >= 0)
    def _():
        for c in fetch(j_end, lax.rem(n_past - 1 - j_end, 2)):
            c.wait()

    for h in range(heads):
        o_ref[:, h * DH_SB:(h + 1) * DH_SB] = acc[h].astype(o_ref.dtype)


def _sb_stream_attention(q, kn, vn, k_src, v_src, *, tq, tk, q_block0, n_tiles, cache_tiles=None,
                         cache_base=0):
    heads = q.shape[0]
    blk = pl.BlockSpec((heads, tq, DH_SB), lambda i: (0, q_block0 + i, 0))
    anyspec = pl.BlockSpec(memory_space=pl.ANY)
    if cache_tiles is None:
        buf = pltpu.VMEM((2, heads, tk, DH_SB), BF16)
    else:
        buf = pltpu.VMEM((2, tk * heads, DH_SB), F32)
    kern = functools.partial(_sb_stream_kernel, heads=heads, tk=tk, cache_tiles=cache_tiles,
                             cache_base=cache_base)
    return pl.pallas_call(
        kern, grid=(n_tiles,),
        in_specs=[blk, blk, blk, anyspec, anyspec,
                  pl.BlockSpec((tq, tq), lambda i: (0, 0)), pl.BlockSpec((tk, tk), lambda i: (0, 0))],
        out_specs=pl.BlockSpec((tq, heads * DH_SB), lambda i: (i, 0)),
        out_shape=jax.ShapeDtypeStruct((n_tiles * tq, heads * DH_SB), BF16),
        scratch_shapes=[buf, buf,
                        pltpu.VMEM((heads, tq, DH_SB), F32),
                        pltpu.VMEM((heads, tq, LANES), F32),
                        pltpu.SMEM((heads,), jnp.int32),
                        pltpu.SMEM((1,), jnp.int32),
                        pltpu.SemaphoreType.DMA((2, 2))],
        compiler_params=_cparams(1),
    )(q, kn, vn, k_src, v_src, _suffix_matrix(tq), _suffix_matrix(tk))


def _mla_block(qc, kc, va, m, acc, mask):
    s = _dot_nt(qc, kc)
    if mask is not None:
        s = jnp.where(mask, s, -jnp.inf)
    m_new = jnp.maximum(m, jnp.max(s, axis=1, keepdims=True))
    alpha = jnp.exp2(m - m_new)
    p = jnp.exp2(s - _lanes(m_new, s.shape[1]))
    acc_new = _lanes(alpha, acc.shape[1]) * acc + jnp.dot(p.astype(BF16), va,
                                                          preferred_element_type=F32)
    return m_new, acc_new


def _mla_kernel(qb_ref, kb_ref, fl_ref, q_ref, kn_ref, vn_ref, kp_ref, vp_ref, o_ref,
                acc, m_sc, *, heads, pos0):
    s = pl.program_id(0)
    fl = fl_ref[s]
    is_first = (fl & 1) != 0
    is_last = (fl & 2) != 0
    tq = q_ref.shape[1]

    @pl.when(is_first)
    def _():
        row = lax.broadcasted_iota(jnp.int32, (tq, tq), 0) + pos0
        col = lax.broadcasted_iota(jnp.int32, (tq, tq), 1) + pos0
        mask = (col // CHUNK) <= (row // CHUNK)

        def body(h, _):
            m0 = jnp.full((tq, LANES), -jnp.inf, F32)
            a0 = jnp.zeros((tq, 2 * V_DIM), F32)
            m, a = _mla_block(q_ref[h], kn_ref[h], vn_ref[h], m0, a0, mask)
            m_sc[h] = m
            acc[h] = a
            return 0
        lax.fori_loop(0, heads, body, 0, unroll=min(heads, MLA_UNROLL))

    @pl.when(jnp.logical_not(is_first))
    def _():
        def body(h, _):
            m, a = _mla_block(q_ref[h], kp_ref[h], vp_ref[h], m_sc[h], acc[h], None)
            m_sc[h] = m
            acc[h] = a
            return 0
        lax.fori_loop(0, heads, body, 0, unroll=min(heads, MLA_UNROLL))

    @pl.when(is_last)
    def _():
        for h in range(heads):
            a = acc[h]
            o_ref[:, h * V_DIM:(h + 1) * V_DIM] = (a[:, :V_DIM] / a[:, V_DIM:]).astype(o_ref.dtype)


def _mla_attention(qc, kcn, vn, kcp, vp, tables, *, tq, tk, pos0, out_rows, out_block_of):
    qb, kb, fl = tables
    n_steps = qb.shape[0]
    h = H_MLA
    wqk = 2 * LANES
    new = pl.BlockSpec((h, tq, wqk), lambda s, qb, kb, fl: (0, qb[s], 0))
    past = pl.BlockSpec((h, tk, wqk), lambda s, qb, kb, fl: (0, kb[s], 0))
    grid_spec = pltpu.PrefetchScalarGridSpec(
        num_scalar_prefetch=3,
        grid=(n_steps,),
        in_specs=[new, new, new, past, past],
        out_specs=pl.BlockSpec((tq, h * V_DIM), out_block_of),
        scratch_shapes=[pltpu.VMEM((h, tq, 2 * V_DIM), F32),
                        pltpu.VMEM((h, tq, LANES), F32)],
    )
    kern = functools.partial(_mla_kernel, heads=h, pos0=pos0)
    return pl.pallas_call(
        kern, grid_spec=grid_spec,
        out_shape=jax.ShapeDtypeStruct((out_rows, h * V_DIM), BF16),
        compiler_params=_cparams(1),
    )(qb, kb, fl, qc, kcn, vn, kcp, vp)


def _mla_latent_kernel(qb_ref, kb_ref, fl_ref, q_ref, cn_ref, rn_ref, cp_ref, rp_ref, w_ref,
                       o_ref, qa, qr, acc, m_sc, l_sc, *, heads, pos0):
    s_id = pl.program_id(0)
    fl = fl_ref[s_id]
    is_first = (fl & 1) != 0
    is_last = (fl & 2) != 0
    tq = q_ref.shape[1]
    hw = QK_NOPE + V_DIM

    def step(ck, kr, mask):
        s = _dot_nt(qa[...], ck) + _dot_nt(qr[...], kr)
        if mask is not None:
            s = jnp.where(mask, s, -jnp.inf)
        m_old = m_sc[...]
        m_new = jnp.maximum(m_old, jnp.max(s, axis=1, keepdims=True))
        alpha = jnp.exp2(m_old - m_new)
        p = jnp.exp2(s - m_new[:, :1])
        l_sc[...] = alpha * l_sc[...] + jnp.sum(p, axis=1, keepdims=True)
        acc[...] = alpha[:, :1] * acc[...] + jnp.dot(p.astype(BF16), ck,
                                                     preferred_element_type=F32)
        m_sc[...] = m_new

    @pl.when(is_first)
    def _():
        for h in range(heads):
            qh = q_ref[h]
            w_uk = w_ref[:, h * hw:h * hw + QK_NOPE]
            qa[h * tq:(h + 1) * tq, :] = _dot_nt(qh[:, :QK_NOPE], w_uk).astype(BF16)
            qr[h * tq:(h + 1) * tq, :] = qh[:, QK_NOPE:]
        m_sc[...] = jnp.full(m_sc.shape, -jnp.inf, F32)
        l_sc[...] = jnp.zeros(l_sc.shape, F32)
        acc[...] = jnp.zeros(acc.shape, F32)
        row = lax.broadcasted_iota(jnp.int32, (heads * tq, tq), 0) % tq + pos0
        col = lax.broadcasted_iota(jnp.int32, (heads * tq, tq), 1) + pos0
        step(cn_ref[...], rn_ref[...], (col // CHUNK) <= (row // CHUNK))

    @pl.when(jnp.logical_not(is_first))
    def _():
        step(cp_ref[...].astype(BF16), rp_ref[...], None)

    @pl.when(is_last)
    def _():
        o_lat = (acc[...] / l_sc[...][:, :1]).astype(BF16)
        for h in range(heads):
            w_uv = w_ref[:, h * hw + QK_NOPE:(h + 1) * hw]
            o_ref[:, h * V_DIM:(h + 1) * V_DIM] = jnp.dot(
                o_lat[h * tq:(h + 1) * tq, :], w_uv, preferred_element_type=F32).astype(o_ref.dtype)


def _mla_latent_attention(qc, c_new, r_new, c_past, r_past, w_ukv_b, tables, l, *, tq, tk, pos0,
                          out_rows, out_block_of):
    qb, kb, fl = tables
    h = H_MLA
    kvl = c_new.shape[1]
    wqk = 2 * LANES
    grid_spec = pltpu.PrefetchScalarGridSpec(
        num_scalar_prefetch=3,
        grid=(qb.shape[0],),
        in_specs=[pl.BlockSpec((h, tq, wqk), lambda s, qb, kb, fl: (0, qb[s], 0)),
                  pl.BlockSpec((tq, kvl), lambda s, qb, kb, fl: (qb[s], 0)),
                  pl.BlockSpec((tq, LANES), lambda s, qb, kb, fl: (qb[s], 0)),
                  pl.BlockSpec((tk, kvl), lambda s, qb, kb, fl: (kb[s], 0)),
                  pl.BlockSpec((tk, LANES), lambda s, qb, kb, fl: (kb[s], 0)),
                  pl.BlockSpec((None, kvl, w_ukv_b.shape[2]), lambda s, qb, kb, fl: (l, 0, 0))],
        out_specs=pl.BlockSpec((tq, h * V_DIM), out_block_of),
        scratch_shapes=[pltpu.VMEM((h * tq, kvl), BF16),
                        pltpu.VMEM((h * tq, LANES), BF16),
                        pltpu.VMEM((h * tq, kvl), F32),
                        pltpu.VMEM((h * tq, LANES), F32),
                        pltpu.VMEM((h * tq, LANES), F32)],
    )
    kern = functools.partial(_mla_latent_kernel, heads=h, pos0=pos0)
    return pl.pallas_call(
        kern, grid_spec=grid_spec,
        out_shape=jax.ShapeDtypeStruct((out_rows, h * V_DIM), BF16),
        compiler_params=_cparams(1),
    )(qb, kb, fl, qc, c_new, r_new, c_past, r_past, w_ukv_b)


def _causal_tables(nq, q_off, per_head=False, heads=1, n_batch=1, past_tiles=None,
                   past_stride=0):
    qb, kb, hb, fl = [], [], [], []
    if past_tiles is None:
        for i in range(nq):
            n = i + 1
            for j in range(n):
                qb.append(q_off + i)
                kb.append(max(i - j, 1) - 1 if j == 0 else i - j)
                hb.append(0)
                fl.append((1 if j == 0 else 0) | (2 if j == n - 1 else 0))
    else:
        for b in range(n_batch):
            for h in range(heads if per_head else 1):
                n = 1 + past_tiles
                for j in range(n):
                    qb.append(q_off + b)
                    jj = past_tiles - 1 if j == 0 else past_tiles - j
                    kb.append(past_stride + b * past_tiles + jj)
                    hb.append(h)
                    fl.append((1 if j == 0 else 0) | (2 if j == n - 1 else 0))
    arr = lambda v: jnp.asarray(np.asarray(v, dtype=np.int32))
    return arr(qb), arr(kb), arr(hb), arr(fl)


ROW_DMA_UNROLL = 8


def _moe_kernel(te_ref, nv_ref, tok_ref, dst_ref, h_hbm, w_ref, wg_ref, wu_ref, wd_ref, y_hbm,
                xbuf, obuf, wg_s, wu_s, wd_s, sem_in, sem_out, *, s):
    t = pl.program_id(0)
    tm = w_ref.shape[0]
    nv = nv_ref[0]
    live = t < nv
    slot = lax.rem(t, 2)

    p = _slab_pitch(s)

    def in_copy(tile, r, sl):
        src0 = pl.multiple_of(tok_ref[tile * tm + r] * p, 8)
        return pltpu.make_async_copy(h_hbm.at[pl.ds(src0, s)],
                                     xbuf.at[sl, pl.ds(pl.multiple_of(r * p, 8), s)],
                                     sem_in.at[sl])

    def gather_start(tile, sl):
        def body(r, _):
            in_copy(tile, r, sl).start()
            return 0
        lax.fori_loop(0, tm, body, 0, unroll=ROW_DMA_UNROLL)

    def gather_wait(tile, sl):
        def body(r, _):
            in_copy(tile, r, sl).wait()
            return 0
        lax.fori_loop(0, tm, body, 0, unroll=ROW_DMA_UNROLL)

    @pl.when(live & (t == 0))
    def _():
        gather_start(0, 0)

    @pl.when(t + 1 < nv)
    def _():
        gather_start(t + 1, 1 - slot)

    prev = te_ref[jnp.maximum(t - 1, 0)]
    fresh = (t == 0) | (te_ref[t] != prev)

    @pl.when(live & fresh)
    def _():
        wg_s[...] = wg_ref[...].astype(BF16)
        wu_s[...] = wu_ref[...].astype(BF16)
        wd_s[...] = wd_ref[...].astype(BF16)

    @pl.when(live)
    def _():
        gather_wait(t, slot)
        x = _slab_load(xbuf, 0, tm, s, lead=slot).astype(BF16)
        a = jnp.dot(x, wg_s[...], preferred_element_type=F32)
        u = jnp.dot(x, wu_s[...], preferred_element_type=F32)
        hid = (a * jax.nn.sigmoid(a)) * u * w_ref[...]
        y = jnp.dot(hid.astype(BF16), wd_s[...], preferred_element_type=F32)

        def out_copy(tile, r):
            dst0 = pl.multiple_of(dst_ref[tile * tm + r] * p, 8)
            return pltpu.make_async_copy(obuf.at[pl.ds(pl.multiple_of(r * p, 8), s)],
                                         y_hbm.at[pl.ds(dst0, s)], sem_out)

        def scatter_wait(tile):
            def wait(r, _):
                out_copy(tile, r).wait()
                return 0
            lax.fori_loop(0, tm, wait, 0, unroll=ROW_DMA_UNROLL)

        @pl.when(t > 0)
        def _():
            scatter_wait(t - 1)

        _slab_store(obuf, 0, y)

        def start(r, _):
            out_copy(t, r).start()
            return 0
        lax.fori_loop(0, tm, start, 0, unroll=ROW_DMA_UNROLL)

        @pl.when(t == nv - 1)
        def _():
            scatter_wait(t)


def _moe_experts(h_slab, row_tok, row_dst, row_w, tile_expert, n_valid, w_gate, w_up, w_down, l,
                 n_dest):
    d, f = w_gate.shape[-2:]
    s = d // LANES
    p = _slab_pitch(s)
    tm = MOE_TILE
    n_tiles = row_tok.shape[0] // tm
    wmap = lambda t, te, nv, tok, dst: (l, te[t], 0, 0)
    grid_spec = pltpu.PrefetchScalarGridSpec(
        num_scalar_prefetch=4, grid=(n_tiles,),
        in_specs=[pl.BlockSpec(memory_space=pl.ANY),
                  pl.BlockSpec((tm, 1), lambda t, te, nv, tok, dst: (t, 0)),
                  pl.BlockSpec((None, None, d, f), wmap),
                  pl.BlockSpec((None, None, d, f), wmap),
                  pl.BlockSpec((None, None, f, d), wmap)],
        out_specs=pl.BlockSpec(memory_space=pl.ANY),
        scratch_shapes=[pltpu.VMEM((2, tm * p, LANES), F32), pltpu.VMEM((tm * p, LANES), F32),
                        pltpu.VMEM((d, f), BF16), pltpu.VMEM((d, f), BF16),
                        pltpu.VMEM((f, d), BF16),
                        pltpu.SemaphoreType.DMA((2,)), pltpu.SemaphoreType.DMA(())],
    )
    return pl.pallas_call(
        functools.partial(_moe_kernel, s=s), grid_spec=grid_spec,
        out_shape=jax.ShapeDtypeStruct(((n_dest + tm) * p, LANES), F32),
        compiler_params=_cparams(1),
    )(tile_expert, n_valid, row_tok, row_dst, h_slab, row_w, w_gate, w_up, w_down)


def _combine_kernel(x_ref, y0_ref, y1_ref, g_ref, o_ref):
    gb, grp, d = x_ref.shape
    s = d // LANES
    for g in range(gb):
        y = _slab_load(y0_ref, g * grp, grp, s) + _slab_load(y1_ref, g * grp, grp, s)
        o_ref[g] = x_ref[g] + g_ref[g] * y


def _combine(x3, yg, modg, l, g_i):
    ng, grp, d = x3.shape
    s = d // LANES
    gb = _row_tile(ng, 2)
    blk = pl.BlockSpec((gb, grp, d), lambda i: (i, 0, 0))
    yblk = lambda off: pl.BlockSpec((gb * grp * _slab_pitch(s), LANES), lambda i: (i + off, 0))
    return pl.pallas_call(
        _combine_kernel,
        grid=(ng // gb,),
        in_specs=[blk, yblk(0), yblk(ng // gb), _mod_spec(g_i, gb, d)],
        out_specs=blk,
        out_shape=jax.ShapeDtypeStruct(x3.shape, F32),
        compiler_params=_cparams(1),
    )(x3, yg, yg, modg)


def _dispatch(route, n):
    tm = MOE_TILE
    r_cap = (2 * n + N_EXPERTS * (tm - 1) + tm - 1) // tm * tm
    e = route[:, :2].astype(jnp.int32)
    w = route[:, 2:4]
    flat_e = e.T.reshape(-1)
    flat_w = w.T.reshape(-1)
    order = jnp.argsort(flat_e, stable=True).astype(jnp.int32)
    bounds = jnp.searchsorted(flat_e[order], jnp.arange(N_EXPERTS + 1, dtype=jnp.int32),
                              side="left").astype(jnp.int32)
    counts = bounds[1:] - bounds[:-1]
    padded = (counts + tm - 1) // tm * tm
    ends_p = jnp.cumsum(padded)
    starts_p = ends_p - padded
    starts = jnp.cumsum(counts) - counts
    tile_start = jnp.arange(r_cap // tm, dtype=jnp.int32) * tm
    tile_expert = jnp.minimum(jnp.searchsorted(ends_p, tile_start, side="right"),
                              N_EXPERTS - 1).astype(jnp.int32)
    n_valid = (ends_p[-1:] // tm).astype(jnp.int32)
    row_e = jnp.repeat(tile_expert, tm)
    local = jnp.arange(r_cap, dtype=jnp.int32) - starts_p[row_e]
    valid = local < counts[row_e]
    slot = order[jnp.clip(starts[row_e] + local, 0, 2 * n - 1)]
    row_dst = jnp.where(valid, slot, 2 * n + jnp.arange(r_cap, dtype=jnp.int32) % tm)
    row_tok = jnp.where(valid, jnp.where(slot >= n, slot - n, slot), 0)
    row_w = jnp.where(valid, flat_w[slot], 0.0)
    return row_tok, row_dst, row_w.reshape(r_cap, 1), tile_expert, n_valid


def _gate_weights_kernel(a_ref, b_ref, o_ref):
    r = b_ref.shape[0]
    w = a_ref.shape[0]
    o_ref[:w - r, :] = a_ref[r:, :].astype(BF16)
    o_ref[w - r:, :] = b_ref[...].astype(BF16)


def _rope_weights_kernel(a_ref, o_ref):
    a = a_ref[...].astype(BF16)
    q = a.shape[0] // 2
    o_ref[...] = jnp.concatenate([a, a[q:], a[:q]], axis=0)


def _split_gate_rope_weights(w_nk, off_kr, n_gate, tn):
    depth, n_in, k = w_nk.shape
    assert off_kr % tn == 0 and n_gate % tn == 0 and QK_ROPE * 2 == LANES
    rb = off_kr // tn
    gates = pl.pallas_call(
        _gate_weights_kernel,
        grid=(depth, n_gate // tn),
        in_specs=[pl.BlockSpec((None, tn, k), lambda l, j: (l, rb + j, 0)),
                  pl.BlockSpec((None, QK_ROPE, k),
                               lambda l, j: (l, (rb + j + 1) * (tn // QK_ROPE), 0))],
        out_specs=pl.BlockSpec((None, tn, k), lambda l, j: (l, j, 0)),
        out_shape=jax.ShapeDtypeStruct((depth, n_gate, k), BF16),
        compiler_params=_cparams(2),
    )(w_nk, w_nk)
    rope = pl.pallas_call(
        _rope_weights_kernel,
        grid=(depth,),
        in_specs=[pl.BlockSpec((None, QK_ROPE, k), lambda l: (l, off_kr // QK_ROPE, 0))],
        out_specs=pl.BlockSpec((None, 2 * QK_ROPE, k), lambda l: (l, 0, 0)),
        out_shape=jax.ShapeDtypeStruct((depth, 2 * QK_ROPE, k), BF16),
        compiler_params=_cparams(1),
    )(w_nk)
    return gates, rope


def _rope_tables(pos):
    inv = ROPE_THETA ** (-jnp.arange(0, QK_ROPE, 2, dtype=F32) / QK_ROPE)
    ang = pos.astype(F32)[:, None] * inv[None, :]
    cos, sin = jnp.cos(ang), jnp.sin(ang)
    pad = jnp.zeros((pos.shape[0], LANES - QK_ROPE), F32)
    return (jnp.concatenate([cos, cos, pad], axis=1),
            jnp.concatenate([-sin, sin, pad], axis=1))


def _rope_lanes(acc, c, s):
    return acc * c + pltpu.roll(acc, LANES - QK_ROPE, axis=1) * s


def _swap_halves(w):
    half = w.shape[-1] // 2
    return jnp.concatenate([w[..., half:], w[..., :half]], axis=-1)


def kernel(x_prompt, x_sample, c_prompt, c_sample, cache_sb_k, cache_sb_v, cache_mla_ckv,
           cache_mla_krope, w_ada, b_ada, g_norm_mix, g_norm_ffn, w_in, g_q_lat, g_kv_lat,
           w_uq, w_ukv, w_branch_sb, w_branch_mla, w_out, w_router_group, b_router_group,
           w_router_expert, b_router_expert, w_exp_gate, w_exp_up, w_exp_down, g_final):
    bp, t_p, d = x_prompt.shape
    bs, t_s, _ = x_sample.shape
    depth = w_in.shape[0]
    past = cache_sb_k.shape[2]
    grp = t_s
    n_p, n_s = bp * t_p, bs * t_s
    n = n_p + n_s
    ng = n // grp
    sb_w = H_SB * DH_SB
    q_lora = g_q_lat.shape[1]
    kv_lora = g_kv_lat.shape[1]
    tm = _token_tile(n, grp)
    gpt = tm // grp
    tn = 512
    assert bp == 1 and t_p % ATTN_TILE == 0 and t_p % grp == 0 and past % 512 == 0

    x3 = jnp.concatenate([x_prompt.reshape(n_p // grp, grp, d), x_sample], axis=0)

    n_c = bp + bs
    c_rows = 16
    c_all = jnp.zeros((c_rows, d), F32).at[:n_c].set(jnp.concatenate([c_prompt, c_sample], 0))
    n_modc = N_MOD * d

    def ada_epi(accs, ex, outs):
        outs[0][...] = accs[0] + ex[0][...]

    mods = []
    for l in range(depth):
        mods.append(_matmul(
            [(c_all, w_ada, _wspec(l, d, tn, 0), True)], m=c_rows, n_out=n_modc, tm=c_rows, tn=tn,
            prologue=lambda a, ex: a * jax.nn.sigmoid(a),
            epilogue=ada_epi,
            extras=[b_ada.reshape(depth, 1, n_modc)],
            extra_specs=[pl.BlockSpec((None, 1, tn), lambda j, i, l=l: (l, 0, j))],
            out_shape=[jax.ShapeDtypeStruct((c_rows, n_modc), F32)],
            out_specs=[_spec2(c_rows, tn)])[0])
    mod = jnp.stack(mods)
    modg = jnp.concatenate(
        [jnp.broadcast_to(mod[:, :bp], (depth, n_p // grp, n_modc)), mod[:, bp:n_c]], axis=1)
    modg = modg.reshape(depth, ng, N_MOD, d).transpose(0, 2, 1, 3)
    modg = modg.reshape(depth * N_MOD, ng, 1, d)

    def mod_idx(l, k):
        return l * N_MOD + k

    def mspec_rows(l, k):
        return pl.BlockSpec((None, gpt, 1, tn), lambda j, i: (mod_idx(l, k), i, 0, j))

    pos = jnp.concatenate([jnp.arange(t_p, dtype=jnp.int32),
                           jnp.tile(past + jnp.arange(t_s, dtype=jnp.int32), bs)])
    rope_c, rope_s = _rope_tables(pos)
    tq = ATTN_TILE
    nq_p = t_p // tq
    tab_p = _causal_tables(nq_p, 0)
    tk_s = 512
    pt = past // tk_s
    tab_sb_s = lambda l: _causal_tables(0, n_p // t_s, n_batch=bs, past_tiles=pt,
                                        past_stride=l * bs * pt)
    tab_mla_s = _causal_tables(0, n_p // t_s, n_batch=bs, past_tiles=pt)

    off_q, off_k, off_v = 0, sb_w, 2 * sb_w
    off_cq = 3 * sb_w
    off_ckv = off_cq + q_lora
    off_kr = off_ckv + kv_lora
    off_g = off_kr + QK_ROPE
    assert off_g == off_kr + QK_ROPE
    w_nk = jnp.swapaxes(w_in, 1, 2)
    w_gates, w_kr_aug = _split_gate_rope_weights(w_nk, off_kr, 2 * d, tn)
    uq = w_uq.reshape(depth, q_lora, H_MLA, QK_NOPE + QK_ROPE)
    uq_r = uq[..., QK_NOPE:]
    w_uq_cat = jnp.concatenate([uq[..., :QK_NOPE], uq_r, _swap_halves(uq_r)], axis=-1)
    w_uq_cat = w_uq_cat.reshape(depth, q_lora, H_MLA * 2 * LANES)
    w_router = jnp.concatenate(
        [w_router_expert, w_router_group,
         jnp.zeros((depth, d, LANES - N_EXPERTS - N_GROUPS), F32)], axis=-1)
    r_hi = w_router.astype(BF16)
    r_res = w_router - r_hi.astype(F32)
    r_mid = r_res.astype(BF16)
    r_lo = (r_res - r_mid.astype(F32)).astype(BF16)
    w_router3 = jnp.stack([r_hi, r_mid, r_lo], axis=1)
    b_router = jnp.concatenate(
        [b_router_expert, b_router_group,
         jnp.zeros((depth, LANES - N_EXPERTS - N_GROUPS), F32)], axis=-1).reshape(depth, 1, LANES)

    rows_past = bs * past
    kr_past = jnp.pad(cache_mla_krope.reshape(depth * rows_past, QK_ROPE),
                      ((0, 0), (0, LANES - QK_ROPE))).astype(BF16)
    w_ukv_b = w_ukv.astype(BF16)
    hm = lambda rows: jax.ShapeDtypeStruct((H_SB, rows, LANES), BF16)
    new_k, new_v, new_c, new_r = [], [], [], []

    for l in range(depth):
        h = _norm_mod(x3, g_norm_mix.reshape(depth, 1, d), modg, l, mod_idx(l, 1), mod_idx(l, 0))
        h = h.reshape(n, d)

        sbq_scale = DH_SB ** -0.5 * float(np.log2(np.e))

        def plain_hm(accs, ex, outs):
            _store_heads(outs[0], accs[0] * sbq_scale)

        def f32_and_hm(accs, ex, outs):
            outs[0][...] = accs[0]
            _store_heads(outs[1], accs[0])

        sb_q = _matmul([(h, w_nk, _wspec_nk(l, d, tn, off_q), True, True)], m=n, n_out=sb_w, tm=tm, tn=tn,
                       epilogue=plain_hm, out_shape=[hm(n)], out_specs=[_hm_spec(tm, tn)])[0]
        k_f32, sb_k = _matmul([(h, w_nk, _wspec_nk(l, d, tn, off_k), True, True)], m=n, n_out=sb_w, tm=tm,
                              tn=tn, epilogue=f32_and_hm,
                              out_shape=[jax.ShapeDtypeStruct((n, sb_w), F32), hm(n)],
                              out_specs=[_spec2(tm, tn), _hm_spec(tm, tn)])
        v_f32, sb_v = _matmul([(h, w_nk, _wspec_nk(l, d, tn, off_v), True, True)], m=n, n_out=sb_w, tm=tm,
                              tn=tn, epilogue=f32_and_hm,
                              out_shape=[jax.ShapeDtypeStruct((n, sb_w), F32), hm(n)],
                              out_specs=[_spec2(tm, tn), _hm_spec(tm, tn)])

        def plain_f32(accs, ex, outs):
            outs[0][...] = accs[0]

        c_q = _matmul([(h, w_nk, _wspec_nk(l, d, tn, off_cq), True, True)], m=n, n_out=q_lora, tm=tm, tn=tn,
                      epilogue=plain_f32, out_shape=[jax.ShapeDtypeStruct((n, q_lora), F32)],
                      out_specs=[_spec2(tm, tn)])[0]

        def ckv_epi(accs, ex, outs):
            y = _rms(accs[0], ex[0][...])
            outs[0][...] = y
            outs[1][...] = y.astype(BF16)

        c_kv, c_kv_b = _matmul(
            [(h, w_nk, _wspec_nk(l, d, kv_lora, off_ckv), True, True)], m=n, n_out=kv_lora, tm=tm, tn=kv_lora,
            epilogue=ckv_epi, extras=[g_kv_lat.reshape(depth, 1, kv_lora)],
            extra_specs=[pl.BlockSpec((None, 1, kv_lora), lambda j, i: (l, 0, 0))],
            out_shape=[jax.ShapeDtypeStruct((n, kv_lora), F32),
                       jax.ShapeDtypeStruct((n, kv_lora), BF16)],
            out_specs=[_spec2(tm, kv_lora), _spec2(tm, kv_lora)])

        def kr_epi(accs, ex, outs):
            r = _rope_lanes(accs[0], ex[0][...], ex[1][...])
            outs[0][...] = r
            outs[1][...] = r.astype(BF16)

        rope_specs = [pl.BlockSpec((tm, LANES), lambda j, i: (i, 0))] * 2
        k_r, k_r_b = _matmul(
            [(h, w_kr_aug, pl.BlockSpec((None, LANES, d), lambda j, i: (l, 0, 0)), False, True)],
            m=n, n_out=LANES, tm=tm, tn=LANES, epilogue=kr_epi,
            extras=[rope_c, rope_s], extra_specs=rope_specs,
            out_shape=[jax.ShapeDtypeStruct((n, LANES), F32), jax.ShapeDtypeStruct((n, LANES), BF16)],
            out_specs=[_spec2(tm, LANES), _spec2(tm, LANES)])

        def gate_epi(accs, ex, outs):
            outs[0][...] = jax.nn.sigmoid(accs[0])

        gates = _matmul(
            [(h, w_gates, pl.BlockSpec((None, tn, d), lambda j, i: (l, j, 0)), False, True)],
            m=n, n_out=2 * d, tm=tm, tn=tn, epilogue=gate_epi,
            out_shape=[jax.ShapeDtypeStruct((n, 2 * d), F32)], out_specs=[_spec2(tm, tn)])[0]

        def cq_prologue(a, ex):
            return _rms(a, ex[0][...])

        gq_spec = pl.BlockSpec((None, 1, q_lora), lambda j, i: (l, 0, 0))
        gq = g_q_lat.reshape(depth, 1, q_lora)
        wqk = 2 * LANES
        hm_qk = lambda rows: jax.ShapeDtypeStruct((H_MLA, rows, wqk), BF16)

        q_scale = (QK_NOPE + QK_ROPE) ** -0.5 * float(np.log2(np.e))

        def qcat_epi(accs, ex, outs):
            c, s = ex[1][...], ex[2][...]
            for hh in range(tn // wqk):
                blk = accs[0][:, hh * wqk:(hh + 1) * wqk]
                outs[0][hh, :, :LANES] = (blk[:, :LANES] * q_scale).astype(BF16)
                outs[0][hh, :, LANES:] = (_rope_lanes(blk[:, LANES:], c, s) * q_scale).astype(BF16)

        q_cat = _matmul(
            [(c_q, w_uq_cat, pl.BlockSpec((None, q_lora, tn), lambda j, i: (l, 0, j)), True)],
            m=n, n_out=H_MLA * wqk, tm=tm, tn=tn, prologue=cq_prologue, epilogue=qcat_epi,
            extras=[gq, rope_c, rope_s], extra_specs=[gq_spec] + rope_specs,
            out_shape=[hm_qk(n)],
            out_specs=[pl.BlockSpec((tn // wqk, tm, wqk), lambda j, i: (j, i, 0))])[0]

        tn_kv = min(2048, H_MLA * wqk)
        hpt = tn_kv // wqk

        def kv_epi(accs, ex, outs):
            kr = ex[0][...]
            ones = jnp.ones(kr.shape, BF16)
            for hh in range(hpt):
                outs[0][hh, :, :LANES] = accs[0][:, hh * wqk:hh * wqk + LANES].astype(BF16)
                outs[0][hh, :, LANES:] = kr
                outs[1][hh, :, :LANES] = accs[0][:, hh * wqk + LANES:(hh + 1) * wqk].astype(BF16)
                outs[1][hh, :, LANES:] = ones

        def up_kv(a, kr, rows, tmr):
            hspec = pl.BlockSpec((hpt, tmr, wqk), lambda j, i: (j, i, 0))
            return _matmul(
                [(a, w_ukv, pl.BlockSpec((None, kv_lora, tn_kv), lambda j, i: (l, 0, j)), True)],
                m=rows, n_out=H_MLA * wqk, tm=tmr, tn=tn_kv, epilogue=kv_epi, extras=[kr],
                extra_specs=[pl.BlockSpec((tmr, LANES), lambda j, i: (i, 0))],
                out_shape=[hm_qk(rows), hm_qk(rows)], out_specs=[hspec, hspec])

        kc_new, v_new = up_kv(c_kv_b, k_r_b, n, tm)

        o_sb_p = _sb_stream_attention(sb_q, sb_k, sb_v, sb_k, sb_v, tq=tq, tk=tq, q_block0=0,
                                      n_tiles=n_p // tq)
        cache_k2 = cache_sb_k.reshape(depth * bs * past * H_SB, DH_SB)
        cache_v2 = cache_sb_v.reshape(depth * bs * past * H_SB, DH_SB)
        q_off_s = n_p // t_s
        o_sb_s = _sb_stream_attention(sb_q, sb_k, sb_v, cache_k2, cache_v2, tq=t_s, tk=ATTN_TILE,
                                      q_block0=q_off_s, n_tiles=bs,
                                      cache_tiles=past // ATTN_TILE,
                                      cache_base=l * bs * (past // ATTN_TILE))
        o_sb = jnp.concatenate([o_sb_p, o_sb_s], axis=0)

        tp3 = (tab_p[0], tab_p[1], tab_p[3])
        o_mla_p = _mla_attention(q_cat, kc_new, v_new, kc_new, v_new, tp3,
                                 tq=tq, tk=tq, pos0=0, out_rows=n_p,
                                 out_block_of=lambda s, qb, kb, fl: (qb[s], 0))
        tab_s = tab_sb_s(l)
        o_mla_s = _mla_latent_attention(
            q_cat, c_kv_b, k_r_b, cache_mla_ckv.reshape(depth * rows_past, kv_lora), kr_past,
            w_ukv_b, (tab_s[0], tab_s[1], tab_s[3]), l, tq=t_s, tk=tk_s, pos0=past, out_rows=n_s,
            out_block_of=lambda s, qb, kb, fl: (qb[s] - q_off_s, 0))
        o_mla = jnp.concatenate([o_mla_p, o_mla_s], axis=0)

        def merge_epi(accs, ex, outs):
            outs[0][...] = (ex[0][...] * accs[0] + ex[1][...] * accs[1]).astype(BF16)

        merged = _matmul(
            [(o_sb, w_branch_sb, _wspec(l, sb_w, tn, 0), True),
             (o_mla, w_branch_mla, _wspec(l, H_MLA * V_DIM, tn, 0), True)],
            m=n, n_out=d, tm=tm, tn=tn, epilogue=merge_epi,
            extras=[gates, gates],
            extra_specs=[pl.BlockSpec((tm, tn), lambda j, i: (i, j)),
                         pl.BlockSpec((tm, tn), lambda j, i: (i, j + d // tn))],
            out_shape=[jax.ShapeDtypeStruct((n, d), BF16)], out_specs=[_spec2(tm, tn)])[0]

        def resid_epi(accs, ex, outs):
            for g in range(gpt):
                outs[0][g] = ex[0][g] + ex[1][g] * accs[0][g * grp:(g + 1) * grp, :]

        x_spec = pl.BlockSpec((gpt, grp, tn), lambda j, i: (i, 0, j))
        x3 = _matmul(
            [(merged, w_out, _wspec(l, d, tn, 0), True)], m=n, n_out=d, tm=tm, tn=tn,
            epilogue=resid_epi, extras=[x3, modg], extra_specs=[x_spec, mspec_rows(l, 2)],
            out_shape=[jax.ShapeDtypeStruct((ng, grp, d), F32)], out_specs=[x_spec])[0]

        h2, route = _norm_route(x3, g_norm_ffn.reshape(depth, 1, d), modg, l, mod_idx(l, 4),
                                mod_idx(l, 3), w_router3, b_router)
        row_tok, row_dst, row_w, tile_expert, n_valid = _dispatch(route.reshape(n, LANES), n)
        yg = _moe_experts(h2, row_tok, row_dst, row_w, tile_expert, n_valid,
                          w_exp_gate, w_exp_up, w_exp_down, l, 2 * n)
        x3 = _combine(x3, yg, modg, l, mod_idx(l, 5))

        new_k.append(k_f32)
        new_v.append(v_f32)
        new_c.append(c_kv)
        new_r.append(k_r[:, :QK_ROPE])

    y_p, y_s = _final_norm(x3, g_final.reshape(1, d), n_p // grp)

    def split(parts, tail):
        a = jnp.stack(parts)
        return (a[:, :n_p].reshape((depth, bp, t_p) + tail),
                a[:, n_p:].reshape((depth, bs, t_s) + tail))

    pk, sk = split(new_k, (H_SB, DH_SB))
    pv, sv = split(new_v, (H_SB, DH_SB))
    pc, sc = split(new_c, (kv_lora,))
    pr, sr = split(new_r, (QK_ROPE,))
    return (y_p.reshape(bp, t_p, d), y_s.reshape(bs, t_s, d), pk, pv, pc, pr, sk, sv, sc, sr)
```

```python
import functools

import numpy as np
import jax
import jax.numpy as jnp
from jax import lax
from jax.experimental import pallas as pl
from jax.experimental.pallas import tpu as pltpu

F32 = jnp.float32
BF16 = jnp.bfloat16

CHUNK = 64
H_SB = 16
DH_SB = 128
H_MLA = 16
QK_NOPE = 128
QK_ROPE = 64
V_DIM = 128
ROPE_THETA = 10000.0
N_GROUPS = 4
EXPERTS_PER_GROUP = 8
N_EXPERTS = N_GROUPS * EXPERTS_PER_GROUP
N_MOD = 6
EPS = 1e-6

LANES = 128
ATTN_TILE = 256
MOE_TILE = 256
HEAD_UNROLL = 4
MLA_UNROLL = 16
SLAB_PAD = 4
SB_DEAD = 152.0
VMEM_LIMIT = 56 * 1024 * 1024


def _cparams(n_axes, vmem=VMEM_LIMIT):
    return pltpu.CompilerParams(dimension_semantics=("arbitrary",) * n_axes,
                                vmem_limit_bytes=vmem)


def _row_tile(n, cap=512):
    t = cap
    while n % t:
        t //= 2
    return t


def _token_tile(n, grp, cap=1152):
    return max(t for t in range(grp, cap + 1, grp) if n % t == 0)


def _matmul(pairs, *, m, n_out, tm, tn, epilogue, out_shape, out_specs,
            extras=(), extra_specs=(), prologue=None):
    n_pairs = len(pairs)
    n_ex = len(extras)
    n_outs = len(out_shape)
    pairs = [tuple(p) + (False,) * (5 - len(p)) for p in pairs]
    cast = [p[3] for p in pairs]
    b_nk = [p[4] for p in pairs]

    def kern(*refs):
        a_refs = refs[0:2 * n_pairs:2]
        b_refs = refs[1:2 * n_pairs:2]
        ex = refs[2 * n_pairs:2 * n_pairs + n_ex]
        outs = refs[2 * n_pairs + n_ex:2 * n_pairs + n_ex + n_outs]
        scr = refs[2 * n_pairs + n_ex + n_outs:]
        i = pl.program_id(1)
        accs = []
        si = 0
        for p in range(n_pairs):
            if cast[p]:
                bsc = scr[si]
                si += 1

                @pl.when(i == 0)
                def _(bsc=bsc, b_ref=b_refs[p]):
                    bsc[...] = b_ref[...].astype(BF16)

                bv = bsc[...]
            else:
                bv = b_refs[p][...]
            a = a_refs[p][...]
            if prologue is not None:
                a = prologue(a, ex)
            if b_nk[p]:
                accs.append(_dot_nt(a.astype(BF16), bv))
            else:
                accs.append(jnp.dot(a.astype(BF16), bv, preferred_element_type=F32))
        epilogue(accs, ex, outs)

    in_specs, args, scratch = [], [], []
    for (a, b, b_spec, cb, nk) in pairs:
        k = a.shape[1]
        in_specs += [pl.BlockSpec((tm, k), lambda j, i: (i, 0)), b_spec]
        args += [a, b]
        if cb:
            scratch.append(pltpu.VMEM((tn, k) if nk else (k, tn), BF16))
    in_specs += list(extra_specs)
    args += list(extras)
    return pl.pallas_call(
        kern,
        grid=(n_out // tn, m // tm),
        in_specs=in_specs,
        out_specs=out_specs,
        out_shape=out_shape,
        scratch_shapes=scratch,
        compiler_params=_cparams(2),
    )(*args)


def _wspec(l, k, tn, col_off):
    cb = col_off // tn
    assert cb * tn == col_off
    return pl.BlockSpec((None, k, tn), lambda j, i: (l, 0, cb + j))


def _wspec_nk(l, k, tn, row_off):
    rb = row_off // tn
    assert rb * tn == row_off
    return pl.BlockSpec((None, tn, k), lambda j, i: (l, rb + j, 0))


def _spec2(tm, tn):
    return pl.BlockSpec((tm, tn), lambda j, i: (i, j))


def _hm_spec(tm, tn):
    return pl.BlockSpec((tn // LANES, tm, LANES), lambda j, i: (j, i, 0))


def _store_heads(o_ref, val):
    for c in range(val.shape[1] // LANES):
        o_ref[c] = val[:, c * LANES:(c + 1) * LANES].astype(o_ref.dtype)


def _rms(x, g):
    return x * lax.rsqrt(jnp.mean(x * x, axis=-1, keepdims=True) + EPS) * g


def _norm_mod_kernel(x_ref, g_ref, sc_ref, sh_ref, o_ref):
    x = x_ref[...]
    y = _rms(x, g_ref[...])
    o_ref[...] = (y * (1.0 + sc_ref[...]) + sh_ref[...]).astype(o_ref.dtype)


def _split3(x):
    hi = x.astype(BF16)
    r = x - hi.astype(F32)
    mid = r.astype(BF16)
    lo = (r - mid.astype(F32)).astype(BF16)
    return hi, mid, lo


def _dot_f32(a, b3):
    a_hi, a_mid, a_lo = _split3(a)
    b_hi, b_mid, b_lo = b3
    d = functools.partial(jnp.dot, preferred_element_type=F32)
    small = d(a_hi, b_lo) + d(a_lo, b_hi) + d(a_mid, b_mid)
    return (d(a_hi, b_hi) + (d(a_hi, b_mid) + d(a_mid, b_hi))) + small


def _route(logits):
    lane = lax.broadcasted_iota(jnp.int32, logits.shape, 1)
    lanef = lane.astype(F32)
    big = jnp.float32(1e9)
    ninf = jnp.float32(-jnp.inf)
    is_g = (lane >= N_EXPERTS) & (lane < N_EXPERTS + N_GROUPS)
    gl = jnp.where(is_g, logits, ninf)
    gmax = jnp.max(gl, axis=1, keepdims=True)
    g_idx = jnp.min(jnp.where(gl == gmax, lanef - N_EXPERTS, big), axis=1, keepdims=True)
    p_group = 1.0 / jnp.sum(jnp.where(is_g, jnp.exp(gl - gmax), 0.0), axis=1, keepdims=True)
    grp = jnp.floor(lanef * (1.0 / EXPERTS_PER_GROUP))
    in_g = (lane < N_EXPERTS) & (grp == g_idx)
    el = jnp.where(in_g, logits, ninf)
    e1 = jnp.max(el, axis=1, keepdims=True)
    i1 = jnp.min(jnp.where(el == e1, lanef, big), axis=1, keepdims=True)
    el2 = jnp.where(lanef == i1, ninf, el)
    e2 = jnp.max(el2, axis=1, keepdims=True)
    i2 = jnp.min(jnp.where(el2 == e2, lanef, big), axis=1, keepdims=True)
    t = jnp.exp(e2 - e1)
    den = 1.0 + t
    w1 = (1.0 / den) * p_group
    w2 = (t / den) * p_group
    out = jnp.where(lane == 0, i1, jnp.where(lane == 1, i2,
          jnp.where(lane == 2, w1, jnp.where(lane == 3, w2, 0.0))))
    return out


def _slab_pitch(s):
    return s + SLAB_PAD


def _slab_store(ref, row0, val):
    rows, d = val.shape
    s = d // LANES
    p = _slab_pitch(s)
    for c in range(s):
        ref[pl.ds(row0 * p + c, rows, stride=p), :] = val[:, c * LANES:(c + 1) * LANES]


def _slab_load(ref, row0, rows, s, lead=None):
    pieces = []
    p = _slab_pitch(s)
    for c in range(s):
        rs = pl.ds(row0 * p + c, rows, stride=p)
        pieces.append(ref[rs, :] if lead is None else ref[lead, rs, :])
    return jnp.concatenate(pieces, axis=1)


def _norm_route_kernel(x_ref, g_ref, sc_ref, sh_ref, wr_ref, br_ref, h_ref, r_ref):
    x = x_ref[...]
    y = _rms(x, g_ref[...])
    h = y * (1.0 + sc_ref[...]) + sh_ref[...]
    b3 = (wr_ref[0], wr_ref[1], wr_ref[2])
    grp = x.shape[1]
    for g in range(x.shape[0]):
        _slab_store(h_ref, g * grp, h[g])
        logits = _dot_f32(h[g], b3) + br_ref[...]
        r_ref[g] = _route(logits)


def _mod_spec(idx, gb, d):
    return pl.BlockSpec((None, gb, 1, d), lambda i: (idx, i, 0, 0))


def _norm_mod(x3, g, modg, l, sc_i, sh_i):
    ng, grp, d = x3.shape
    gb = _row_tile(ng, 4)
    return pl.pallas_call(
        _norm_mod_kernel,
        grid=(ng // gb,),
        in_specs=[pl.BlockSpec((gb, grp, d), lambda i: (i, 0, 0)),
                  pl.BlockSpec((None, 1, d), lambda i: (l, 0, 0)),
                  _mod_spec(sc_i, gb, d), _mod_spec(sh_i, gb, d)],
        out_specs=pl.BlockSpec((gb, grp, d), lambda i: (i, 0, 0)),
        out_shape=jax.ShapeDtypeStruct(x3.shape, BF16),
        compiler_params=_cparams(1),
    )(x3, g, modg, modg)


def _norm_route(x3, g, modg, l, sc_i, sh_i, wr3, br):
    ng, grp, d = x3.shape
    gb = _row_tile(ng, 4)
    return pl.pallas_call(
        _norm_route_kernel,
        grid=(ng // gb,),
        in_specs=[pl.BlockSpec((gb, grp, d), lambda i: (i, 0, 0)),
                  pl.BlockSpec((None, 1, d), lambda i: (l, 0, 0)),
                  _mod_spec(sc_i, gb, d), _mod_spec(sh_i, gb, d),
                  pl.BlockSpec((None, 3, d, LANES), lambda i: (l, 0, 0, 0)),
                  pl.BlockSpec((None, 1, LANES), lambda i: (l, 0, 0))],
        out_specs=[pl.BlockSpec((gb * grp * _slab_pitch(d // LANES), LANES), lambda i: (i, 0)),
                   pl.BlockSpec((gb, grp, LANES), lambda i: (i, 0, 0))],
        out_shape=[jax.ShapeDtypeStruct((ng * grp * _slab_pitch(d // LANES), LANES), F32),
                   jax.ShapeDtypeStruct((ng, grp, LANES), F32)],
        compiler_params=_cparams(1),
    )(x3, g, modg, modg, wr3, br)


def _final_norm_kernel(x_ref, g_ref, op_ref, os_ref, *, n_prompt_blocks):
    i = pl.program_id(0)
    y = _rms(x_ref[...], g_ref[...])

    @pl.when(i < n_prompt_blocks)
    def _():
        op_ref[...] = y

    @pl.when(i >= n_prompt_blocks)
    def _():
        os_ref[...] = y


def _final_norm(x3, g, ng_prompt):
    ng, grp, d = x3.shape
    gb = _row_tile(np.gcd(ng_prompt, ng - ng_prompt), 4)
    npb = ng_prompt // gb
    blk = lambda f: pl.BlockSpec((gb, grp, d), f)
    return pl.pallas_call(
        functools.partial(_final_norm_kernel, n_prompt_blocks=npb),
        grid=(ng // gb,),
        in_specs=[blk(lambda i: (i, 0, 0)), pl.BlockSpec((1, d), lambda i: (0, 0))],
        out_specs=[blk(lambda i: (jnp.minimum(i, npb - 1), 0, 0)),
                   blk(lambda i: (jnp.maximum(i - npb, 0), 0, 0))],
        out_shape=[jax.ShapeDtypeStruct((ng_prompt, grp, d), F32),
                   jax.ShapeDtypeStruct((ng - ng_prompt, grp, d), F32)],
        compiler_params=_cparams(1),
    )(x3, g)


def _lanes(c, w):
    if w % LANES == 0:
        return c if w == LANES else jnp.tile(c, (1, w // LANES))
    return c[:, :w]


def _dot_nt(a, b):
    return lax.dot_general(a, b, (((1,), (1,)), ((), ())), preferred_element_type=F32)


def _sb_block(qh, kh, vh, c, u, masked):
    w = kh.shape[0]
    z = _dot_nt(qh, kh)
    sp = jnp.maximum(z, 0.0) + jnp.log2(1.0 + jnp.exp2(-jnp.abs(z)))
    if masked:
        row = lax.broadcasted_iota(jnp.int32, z.shape, 0)
        col = lax.broadcasted_iota(jnp.int32, z.shape, 1)
        valid = col < row
        sp = jnp.where(valid, sp, 0.0)
    hi = sp.astype(BF16)
    lo = (sp - hi.astype(F32)).astype(BF16)
    cs = jnp.dot(hi, u, preferred_element_type=F32) + jnp.dot(lo, u, preferred_element_type=F32)
    wgt = jnp.exp2(z - sp - cs - _lanes(c, w))
    if masked:
        wgt = jnp.where(valid, wgt, 0.0)
    o = jnp.dot(wgt.astype(BF16), vh, preferred_element_type=F32)
    c_new = c + jnp.sum(sp, axis=1, keepdims=True)
    return o, c_new


def _suffix_matrix(w):
    j = np.arange(w)[:, None]
    s = np.arange(w)[None, :]
    return jnp.asarray((j > s).astype(np.float32), dtype=BF16)


def _sb_stream_kernel(q_ref, kn_ref, vn_ref, k_hbm, v_hbm, ud_ref, up_ref, o_ref,
                      kbuf, vbuf, acc, carry, done, alive_ref, sem, *, heads, tk, cache_tiles,
                      cache_base):
    i = pl.program_id(0)
    n_past = i if cache_tiles is None else cache_tiles

    def fetch(j, slot):
        if cache_tiles is None:
            rows = pl.ds(pl.multiple_of(j * tk, tk), tk)
            src_k, src_v = k_hbm.at[:, rows, :], v_hbm.at[:, rows, :]
        else:
            blk = tk * heads
            rows = pl.ds(pl.multiple_of((cache_base + i * cache_tiles + j) * blk, blk), blk)
            src_k, src_v = k_hbm.at[rows], v_hbm.at[rows]
        return (pltpu.make_async_copy(src_k, kbuf.at[slot], sem.at[0, slot]),
                pltpu.make_async_copy(src_v, vbuf.at[slot], sem.at[1, slot]))

    def tile_of(buf, slot, h):
        if cache_tiles is None:
            return buf[slot, h]
        return buf.at[slot][pl.ds(h, tk, stride=heads), :].astype(BF16)

    @pl.when(n_past > 0)
    def _():
        for c in fetch(n_past - 1, 0):
            c.start()

    def diag(h, _):
        o, c = _sb_block(q_ref[h], kn_ref[h], vn_ref[h], jnp.zeros(carry.shape[1:], F32),
                         ud_ref[...], True)
        acc[h] = o
        carry[h] = c
        done[h] = 0
        return 0
    lax.fori_loop(0, heads, diag, 0, unroll=min(heads, HEAD_UNROLL))

    alive_ref[0] = heads

    @pl.when(n_past > 0)
    def _():
        for c in fetch(n_past - 1, 0):
            c.wait()

        @pl.when(n_past > 1)
        def _():
            for c in fetch(n_past - 2, 1):
                c.start()

        def first(h, alive):
            o, c = _sb_block(q_ref[h], tile_of(kbuf, 0, h), tile_of(vbuf, 0, h), carry[h],
                             up_ref[...], False)
            acc[h] = acc[h] + o
            carry[h] = c
            dead = (jnp.min(c) >= SB_DEAD).astype(jnp.int32)
            done[h] = dead
            return alive + 1 - dead
        alive_ref[0] = lax.fori_loop(0, heads, first, 0, unroll=min(heads, HEAD_UNROLL))

    def cond(state):
        j, alive = state
        return (j >= 0) & (alive > 0)

    def body(state):
        j, _ = state
        slot = lax.rem(n_past - 1 - j, 2)
        for c in fetch(j, slot):
            c.wait()

        @pl.when(j > 0)
        def _():
            for c in fetch(j - 1, 1 - slot):
                c.start()

        def head(h, alive):
            @pl.when(done[h] == 0)
            def _():
                o, c = _sb_block(q_ref[h], tile_of(kbuf, slot, h), tile_of(vbuf, slot, h),
                                 carry[h], up_ref[...], False)
                acc[h] = acc[h] + o
                carry[h] = c
                done[h] = (jnp.min(c) >= SB_DEAD).astype(jnp.int32)
            return alive + 1 - done[h]
        return j - 1, lax.fori_loop(0, heads, head, 0)

    j_end, _ = lax.while_loop(cond, body, (jnp.int32(n_past - 2), alive_ref[0]))

    @pl.when(j_end >= 0)
    def _():
        for c in fetch(j_end, lax.rem(n_past - 1 - j_end, 2)):
            c.wait()

    for h in range(heads):
        o_ref[:, h * DH_SB:(h + 1) * DH_SB] = acc[h].astype(o_ref.dtype)


def _sb_stream_attention(q, kn, vn, k_src, v_src, *, tq, tk, q_block0, n_tiles, cache_tiles=None,
                         cache_base=0):
    heads = q.shape[0]
    blk = pl.BlockSpec((heads, tq, DH_SB), lambda i: (0, q_block0 + i, 0))
    anyspec = pl.BlockSpec(memory_space=pl.ANY)
    if cache_tiles is None:
        buf = pltpu.VMEM((2, heads, tk, DH_SB), BF16)
    else:
        buf = pltpu.VMEM((2, tk * heads, DH_SB), F32)
    kern = functools.partial(_sb_stream_kernel, heads=heads, tk=tk, cache_tiles=cache_tiles,
                             cache_base=cache_base)
    return pl.pallas_call(
        kern, grid=(n_tiles,),
        in_specs=[blk, blk, blk, anyspec, anyspec,
                  pl.BlockSpec((tq, tq), lambda i: (0, 0)), pl.BlockSpec((tk, tk), lambda i: (0, 0))],
        out_specs=pl.BlockSpec((tq, heads * DH_SB), lambda i: (i, 0)),
        out_shape=jax.ShapeDtypeStruct((n_tiles * tq, heads * DH_SB), BF16),
        scratch_shapes=[buf, buf,
                        pltpu.VMEM((heads, tq, DH_SB), F32),
                        pltpu.VMEM((heads, tq, LANES), F32),
                        pltpu.SMEM((heads,), jnp.int32),
                        pltpu.SMEM((1,), jnp.int32),
                        pltpu.SemaphoreType.DMA((2, 2))],
        compiler_params=_cparams(1),
    )(q, kn, vn, k_src, v_src, _suffix_matrix(tq), _suffix_matrix(tk))


def _mla_block(qc, kc, va, m, acc, mask):
    s = _dot_nt(qc, kc)
    if mask is not None:
        s = jnp.where(mask, s, -jnp.inf)
    m_new = jnp.maximum(m, jnp.max(s, axis=1, keepdims=True))
    alpha = jnp.exp2(m - m_new)
    p = jnp.exp2(s - _lanes(m_new, s.shape[1]))
    acc_new = _lanes(alpha, acc.shape[1]) * acc + jnp.dot(p.astype(BF16), va,
                                                          preferred_element_type=F32)
    return m_new, acc_new


def _mla_kernel(qb_ref, kb_ref, fl_ref, q_ref, kn_ref, vn_ref, kp_ref, vp_ref, o_ref,
                acc, m_sc, *, heads, pos0):
    s = pl.program_id(0)
    fl = fl_ref[s]
    is_first = (fl & 1) != 0
    is_last = (fl & 2) != 0
    tq = q_ref.shape[1]

    @pl.when(is_first)
    def _():
        row = lax.broadcasted_iota(jnp.int32, (tq, tq), 0) + pos0
        col = lax.broadcasted_iota(jnp.int32, (tq, tq), 1) + pos0
        mask = (col // CHUNK) <= (row // CHUNK)

        def body(h, _):
            m0 = jnp.full((tq, LANES), -jnp.inf, F32)
            a0 = jnp.zeros((tq, 2 * V_DIM), F32)
            m, a = _mla_block(q_ref[h], kn_ref[h], vn_ref[h], m0, a0, mask)
            m_sc[h] = m
            acc[h] = a
            return 0
        lax.fori_loop(0, heads, body, 0, unroll=min(heads, MLA_UNROLL))

    @pl.when(jnp.logical_not(is_first))
    def _():
        def body(h, _):
            m, a = _mla_block(q_ref[h], kp_ref[h], vp_ref[h], m_sc[h], acc[h], None)
            m_sc[h] = m
            acc[h] = a
            return 0
        lax.fori_loop(0, heads, body, 0, unroll=min(heads, MLA_UNROLL))

    @pl.when(is_last)
    def _():
        for h in range(heads):
            a = acc[h]
            o_ref[:, h * V_DIM:(h + 1) * V_DIM] = (a[:, :V_DIM] / a[:, V_DIM:]).astype(o_ref.dtype)


def _mla_attention(qc, kcn, vn, kcp, vp, tables, *, tq, tk, pos0, out_rows, out_block_of):
    qb, kb, fl = tables
    n_steps = qb.shape[0]
    h = H_MLA
    wqk = 2 * LANES
    new = pl.BlockSpec((h, tq, wqk), lambda s, qb, kb, fl: (0, qb[s], 0))
    past = pl.BlockSpec((h, tk, wqk), lambda s, qb, kb, fl: (0, kb[s], 0))
    grid_spec = pltpu.PrefetchScalarGridSpec(
        num_scalar_prefetch=3,
        grid=(n_steps,),
        in_specs=[new, new, new, past, past],
        out_specs=pl.BlockSpec((tq, h * V_DIM), out_block_of),
        scratch_shapes=[pltpu.VMEM((h, tq, 2 * V_DIM), F32),
                        pltpu.VMEM((h, tq, LANES), F32)],
    )
    kern = functools.partial(_mla_kernel, heads=h, pos0=pos0)
    return pl.pallas_call(
        kern, grid_spec=grid_spec,
        out_shape=jax.ShapeDtypeStruct((out_rows, h * V_DIM), BF16),
        compiler_params=_cparams(1),
    )(qb, kb, fl, qc, kcn, vn, kcp, vp)


def _mla_latent_kernel(qb_ref, kb_ref, fl_ref, q_ref, cn_ref, rn_ref, cp_ref, rp_ref, w_ref,
                       o_ref, qa, qr, acc, m_sc, l_sc, *, heads, pos0):
    s_id = pl.program_id(0)
    fl = fl_ref[s_id]
    is_first = (fl & 1) != 0
    is_last = (fl & 2) != 0
    tq = q_ref.shape[1]
    hw = QK_NOPE + V_DIM

    def step(ck, kr, mask):
        s = _dot_nt(qa[...], ck) + _dot_nt(qr[...], kr)
        if mask is not None:
            s = jnp.where(mask, s, -jnp.inf)
        m_old = m_sc[...]
        m_new = jnp.maximum(m_old, jnp.max(s, axis=1, keepdims=True))
        alpha = jnp.exp2(m_old - m_new)
        p = jnp.exp2(s - m_new[:, :1])
        l_sc[...] = alpha * l_sc[...] + jnp.sum(p, axis=1, keepdims=True)
        acc[...] = alpha[:, :1] * acc[...] + jnp.dot(p.astype(BF16), ck,
                                                     preferred_element_type=F32)
        m_sc[...] = m_new

    @pl.when(is_first)
    def _():
        for h in range(heads):
            qh = q_ref[h]
            w_uk = w_ref[:, h * hw:h * hw + QK_NOPE]
            qa[h * tq:(h + 1) * tq, :] = _dot_nt(qh[:, :QK_NOPE], w_uk).astype(BF16)
            qr[h * tq:(h + 1) * tq, :] = qh[:, QK_NOPE:]
        m_sc[...] = jnp.full(m_sc.shape, -jnp.inf, F32)
        l_sc[...] = jnp.zeros(l_sc.shape, F32)
        acc[...] = jnp.zeros(acc.shape, F32)
        row = lax.broadcasted_iota(jnp.int32, (heads * tq, tq), 0) % tq + pos0
        col = lax.broadcasted_iota(jnp.int32, (heads * tq, tq), 1) + pos0
        step(cn_ref[...], rn_ref[...], (col // CHUNK) <= (row // CHUNK))

    @pl.when(jnp.logical_not(is_first))
    def _():
        step(cp_ref[...].astype(BF16), rp_ref[...], None)

    @pl.when(is_last)
    def _():
        o_lat = (acc[...] / l_sc[...][:, :1]).astype(BF16)
        for h in range(heads):
            w_uv = w_ref[:, h * hw + QK_NOPE:(h + 1) * hw]
            o_ref[:, h * V_DIM:(h + 1) * V_DIM] = jnp.dot(
                o_lat[h * tq:(h + 1) * tq, :], w_uv, preferred_element_type=F32).astype(o_ref.dtype)


def _mla_latent_attention(qc, c_new, r_new, c_past, r_past, w_ukv_b, tables, l, *, tq, tk, pos0,
                          out_rows, out_block_of):
    qb, kb, fl = tables
    h = H_MLA
    kvl = c_new.shape[1]
    wqk = 2 * LANES
    grid_spec = pltpu.PrefetchScalarGridSpec(
        num_scalar_prefetch=3,
        grid=(qb.shape[0],),
        in_specs=[pl.BlockSpec((h, tq, wqk), lambda s, qb, kb, fl: (0, qb[s], 0)),
                  pl.BlockSpec((tq, kvl), lambda s, qb, kb, fl: (qb[s], 0)),
                  pl.BlockSpec((tq, LANES), lambda s, qb, kb, fl: (qb[s], 0)),
                  pl.BlockSpec((tk, kvl), lambda s, qb, kb, fl: (kb[s], 0)),
                  pl.BlockSpec((tk, LANES), lambda s, qb, kb, fl: (kb[s], 0)),
                  pl.BlockSpec((None, kvl, w_ukv_b.shape[2]), lambda s, qb, kb, fl: (l, 0, 0))],
        out_specs=pl.BlockSpec((tq, h * V_DIM), out_block_of),
        scratch_shapes=[pltpu.VMEM((h * tq, kvl), BF16),
                        pltpu.VMEM((h * tq, LANES), BF16),
                        pltpu.VMEM((h * tq, kvl), F32),
                        pltpu.VMEM((h * tq, LANES), F32),
                        pltpu.VMEM((h * tq, LANES), F32)],
    )
    kern = functools.partial(_mla_latent_kernel, heads=h, pos0=pos0)
    return pl.pallas_call(
        kern, grid_spec=grid_spec,
        out_shape=jax.ShapeDtypeStruct((out_rows, h * V_DIM), BF16),
        compiler_params=_cparams(1),
    )(qb, kb, fl, qc, c_new, r_new, c_past, r_past, w_ukv_b)


def _causal_tables(nq, q_off, per_head=False, heads=1, n_batch=1, past_tiles=None,
                   past_stride=0):
    qb, kb, hb, fl = [], [], [], []
    if past_tiles is None:
        for i in range(nq):
            n = i + 1
            for j in range(n):
                qb.append(q_off + i)
                kb.append(max(i - j, 1) - 1 if j == 0 else i - j)
                hb.append(0)
                fl.append((1 if j == 0 else 0) | (2 if j == n - 1 else 0))
    else:
        for b in range(n_batch):
            for h in range(heads if per_head else 1):
                n = 1 + past_tiles
                for j in range(n):
                    qb.append(q_off + b)
                    jj = past_tiles - 1 if j == 0 else past_tiles - j
                    kb.append(past_stride + b * past_tiles + jj)
                    hb.append(h)
                    fl.append((1 if j == 0 else 0) | (2 if j == n - 1 else 0))
    arr = lambda v: jnp.asarray(np.asarray(v, dtype=np.int32))
    return arr(qb), arr(kb), arr(hb), arr(fl)


ROW_DMA_UNROLL = 8


def _moe_kernel(te_ref, nv_ref, tok_ref, dst_ref, h_hbm, w_ref, wg_ref, wu_ref, wd_ref, y_hbm,
                xbuf, obuf, wg_s, wu_s, wd_s, sem_in, sem_out, *, s):
    t = pl.program_id(0)
    tm = w_ref.shape[0]
    nv = nv_ref[0]
    live = t < nv
    slot = lax.rem(t, 2)

    p = _slab_pitch(s)

    def in_copy(tile, r, sl):
        src0 = pl.multiple_of(tok_ref[tile * tm + r] * p, SLAB_PAD)
        return pltpu.make_async_copy(h_hbm.at[pl.ds(src0, s)],
                                     xbuf.at[sl, pl.ds(pl.multiple_of(r * p, SLAB_PAD), s)],
                                     sem_in.at[sl])

    def gather_start(tile, sl):
        def body(r, _):
            in_copy(tile, r, sl).start()
            return 0
        lax.fori_loop(0, tm, body, 0, unroll=ROW_DMA_UNROLL)

    def gather_wait(tile, sl):
        def body(r, _):
            in_copy(tile, r, sl).wait()
            return 0
        lax.fori_loop(0, tm, body, 0, unroll=ROW_DMA_UNROLL)

    @pl.when(live & (t == 0))
    def _():
        gather_start(0, 0)

    @pl.when(t + 1 < nv)
    def _():
        gather_start(t + 1, 1 - slot)

    prev = te_ref[jnp.maximum(t - 1, 0)]
    fresh = (t == 0) | (te_ref[t] != prev)

    @pl.when(live & fresh)
    def _():
        wg_s[...] = wg_ref[...].astype(BF16)
        wu_s[...] = wu_ref[...].astype(BF16)
        wd_s[...] = wd_ref[...].astype(BF16)

    @pl.when(live)
    def _():
        gather_wait(t, slot)
        x = _slab_load(xbuf, 0, tm, s, lead=slot).astype(BF16)
        a = jnp.dot(x, wg_s[...], preferred_element_type=F32)
        u = jnp.dot(x, wu_s[...], preferred_element_type=F32)
        hid = (a * jax.nn.sigmoid(a)) * u * w_ref[...]
        y = jnp.dot(hid.astype(BF16), wd_s[...], preferred_element_type=F32)

        def out_copy(tile, r):
            dst0 = pl.multiple_of(dst_ref[tile * tm + r] * p, SLAB_PAD)
            return pltpu.make_async_copy(obuf.at[pl.ds(pl.multiple_of(r * p, SLAB_PAD), s)],
                                         y_hbm.at[pl.ds(dst0, s)], sem_out)

        def scatter_wait(tile):
            def wait(r, _):
                out_copy(tile, r).wait()
                return 0
            lax.fori_loop(0, tm, wait, 0, unroll=ROW_DMA_UNROLL)

        @pl.when(t > 0)
        def _():
            scatter_wait(t - 1)

        _slab_store(obuf, 0, y)

        def start(r, _):
            out_copy(t, r).start()
            return 0
        lax.fori_loop(0, tm, start, 0, unroll=ROW_DMA_UNROLL)

        @pl.when(t == nv - 1)
        def _():
            scatter_wait(t)


def _moe_experts(h_slab, row_tok, row_dst, row_w, tile_expert, n_valid, w_gate, w_up, w_down, l,
                 n_dest):
    d, f = w_gate.shape[-2:]
    s = d // LANES
    p = _slab_pitch(s)
    tm = MOE_TILE
    n_tiles = row_tok.shape[0] // tm
    wmap = lambda t, te, nv, tok, dst: (l, te[t], 0, 0)
    grid_spec = pltpu.PrefetchScalarGridSpec(
        num_scalar_prefetch=4, grid=(n_tiles,),
        in_specs=[pl.BlockSpec(memory_space=pl.ANY),
                  pl.BlockSpec((tm, 1), lambda t, te, nv, tok, dst: (t, 0)),
                  pl.BlockSpec((None, None, d, f), wmap),
                  pl.BlockSpec((None, None, d, f), wmap),
                  pl.BlockSpec((None, None, f, d), wmap)],
        out_specs=pl.BlockSpec(memory_space=pl.ANY),
        scratch_shapes=[pltpu.VMEM((2, tm * p, LANES), F32), pltpu.VMEM((tm * p, LANES), F32),
                        pltpu.VMEM((d, f), BF16), pltpu.VMEM((d, f), BF16),
                        pltpu.VMEM((f, d), BF16),
                        pltpu.SemaphoreType.DMA((2,)), pltpu.SemaphoreType.DMA(())],
    )
    return pl.pallas_call(
        functools.partial(_moe_kernel, s=s), grid_spec=grid_spec,
        out_shape=jax.ShapeDtypeStruct(((n_dest + tm) * p, LANES), F32),
        compiler_params=_cparams(1),
    )(tile_expert, n_valid, row_tok, row_dst, h_slab, row_w, w_gate, w_up, w_down)


def _combine_kernel(x_ref, y0_ref, y1_ref, g_ref, o_ref):
    gb, grp, d = x_ref.shape
    s = d // LANES
    for g in range(gb):
        y = _slab_load(y0_ref, g * grp, grp, s) + _slab_load(y1_ref, g * grp, grp, s)
        o_ref[g] = x_ref[g] + g_ref[g] * y


def _combine(x3, yg, modg, l, g_i):
    ng, grp, d = x3.shape
    s = d // LANES
    gb = _row_tile(ng, 2)
    blk = pl.BlockSpec((gb, grp, d), lambda i: (i, 0, 0))
    yblk = lambda off: pl.BlockSpec((gb * grp * _slab_pitch(s), LANES), lambda i: (i + off, 0))
    return pl.pallas_call(
        _combine_kernel,
        grid=(ng // gb,),
        in_specs=[blk, yblk(0), yblk(ng // gb), _mod_spec(g_i, gb, d)],
        out_specs=blk,
        out_shape=jax.ShapeDtypeStruct(x3.shape, F32),
        compiler_params=_cparams(1),
    )(x3, yg, yg, modg)


def _dispatch(route, n):
    tm = MOE_TILE
    r_cap = (2 * n + N_EXPERTS * (tm - 1) + tm - 1) // tm * tm
    e = route[:, :2].astype(jnp.int32)
    w = route[:, 2:4]
    flat_e = e.T.reshape(-1)
    flat_w = w.T.reshape(-1)
    order = jnp.argsort(flat_e, stable=True).astype(jnp.int32)
    bounds = jnp.searchsorted(flat_e[order], jnp.arange(N_EXPERTS + 1, dtype=jnp.int32),
                              side="left").astype(jnp.int32)
    counts = bounds[1:] - bounds[:-1]
    padded = (counts + tm - 1) // tm * tm
    ends_p = jnp.cumsum(padded)
    starts_p = ends_p - padded
    starts = jnp.cumsum(counts) - counts
    tile_start = jnp.arange(r_cap // tm, dtype=jnp.int32) * tm
    tile_expert = jnp.minimum(jnp.searchsorted(ends_p, tile_start, side="right"),
                              N_EXPERTS - 1).astype(jnp.int32)
    n_valid = (ends_p[-1:] // tm).astype(jnp.int32)
    per_row = lambda v: jnp.broadcast_to(v[:, None], (r_cap // tm, tm)).reshape(r_cap)
    local = jnp.arange(r_cap, dtype=jnp.int32) - per_row(starts_p[tile_expert])
    valid = local < per_row(counts[tile_expert])
    slot = order[jnp.clip(per_row(starts[tile_expert]) + local, 0, 2 * n - 1)]
    row_dst = jnp.where(valid, slot, 2 * n + jnp.arange(r_cap, dtype=jnp.int32) % tm)
    row_tok = jnp.where(valid, jnp.where(slot >= n, slot - n, slot), 0)
    row_w = jnp.where(valid, flat_w[slot], 0.0)
    return row_tok, row_dst, row_w.reshape(r_cap, 1), tile_expert, n_valid


def _gate_weights_kernel(a_ref, b_ref, o_ref):
    r = b_ref.shape[0]
    w = a_ref.shape[0]
    o_ref[:w - r, :] = a_ref[r:, :].astype(BF16)
    o_ref[w - r:, :] = b_ref[...].astype(BF16)


def _rope_weights_kernel(a_ref, o_ref):
    a = a_ref[...].astype(BF16)
    q = a.shape[0] // 2
    o_ref[...] = jnp.concatenate([a, a[q:], a[:q]], axis=0)


def _split_gate_rope_weights(w_nk, off_kr, n_gate, tn):
    depth, n_in, k = w_nk.shape
    assert off_kr % tn == 0 and n_gate % tn == 0 and QK_ROPE * 2 == LANES
    rb = off_kr // tn
    gates = pl.pallas_call(
        _gate_weights_kernel,
        grid=(depth, n_gate // tn),
        in_specs=[pl.BlockSpec((None, tn, k), lambda l, j: (l, rb + j, 0)),
                  pl.BlockSpec((None, QK_ROPE, k),
                               lambda l, j: (l, (rb + j + 1) * (tn // QK_ROPE), 0))],
        out_specs=pl.BlockSpec((None, tn, k), lambda l, j: (l, j, 0)),
        out_shape=jax.ShapeDtypeStruct((depth, n_gate, k), BF16),
        compiler_params=_cparams(2),
    )(w_nk, w_nk)
    rope = pl.pallas_call(
        _rope_weights_kernel,
        grid=(depth,),
        in_specs=[pl.BlockSpec((None, QK_ROPE, k), lambda l: (l, off_kr // QK_ROPE, 0))],
        out_specs=pl.BlockSpec((None, 2 * QK_ROPE, k), lambda l: (l, 0, 0)),
        out_shape=jax.ShapeDtypeStruct((depth, 2 * QK_ROPE, k), BF16),
        compiler_params=_cparams(1),
    )(w_nk)
    return gates, rope


def _rope_tables(pos):
    inv = ROPE_THETA ** (-jnp.arange(0, QK_ROPE, 2, dtype=F32) / QK_ROPE)
    ang = pos.astype(F32)[:, None] * inv[None, :]
    cos, sin = jnp.cos(ang), jnp.sin(ang)
    pad = jnp.zeros((pos.shape[0], LANES - QK_ROPE), F32)
    return (jnp.concatenate([cos, cos, pad], axis=1),
            jnp.concatenate([-sin, sin, pad], axis=1))


def _rope_lanes(acc, c, s):
    return acc * c + pltpu.roll(acc, LANES - QK_ROPE, axis=1) * s


def _swap_halves(w):
    half = w.shape[-1] // 2
    return jnp.concatenate([w[..., half:], w[..., :half]], axis=-1)


def kernel(x_prompt, x_sample, c_prompt, c_sample, cache_sb_k, cache_sb_v, cache_mla_ckv,
           cache_mla_krope, w_ada, b_ada, g_norm_mix, g_norm_ffn, w_in, g_q_lat, g_kv_lat,
           w_uq, w_ukv, w_branch_sb, w_branch_mla, w_out, w_router_group, b_router_group,
           w_router_expert, b_router_expert, w_exp_gate, w_exp_up, w_exp_down, g_final):
    bp, t_p, d = x_prompt.shape
    bs, t_s, _ = x_sample.shape
    depth = w_in.shape[0]
    past = cache_sb_k.shape[2]
    grp = t_s
    n_p, n_s = bp * t_p, bs * t_s
    n = n_p + n_s
    ng = n // grp
    sb_w = H_SB * DH_SB
    q_lora = g_q_lat.shape[1]
    kv_lora = g_kv_lat.shape[1]
    tm = _token_tile(n, grp)
    gpt = tm // grp
    tn = 512
    assert bp == 1 and t_p % ATTN_TILE == 0 and t_p % grp == 0 and past % 512 == 0

    x3 = jnp.concatenate([x_prompt.reshape(n_p // grp, grp, d), x_sample], axis=0)

    n_c = bp + bs
    c_rows = 16
    c_all = jnp.zeros((c_rows, d), F32).at[:n_c].set(jnp.concatenate([c_prompt, c_sample], 0))
    n_modc = N_MOD * d

    def ada_epi(accs, ex, outs):
        outs[0][...] = accs[0] + ex[0][...]

    mods = []
    for l in range(depth):
        mods.append(_matmul(
            [(c_all, w_ada, _wspec(l, d, tn, 0), True)], m=c_rows, n_out=n_modc, tm=c_rows, tn=tn,
            prologue=lambda a, ex: a * jax.nn.sigmoid(a),
            epilogue=ada_epi,
            extras=[b_ada.reshape(depth, 1, n_modc)],
            extra_specs=[pl.BlockSpec((None, 1, tn), lambda j, i, l=l: (l, 0, j))],
            out_shape=[jax.ShapeDtypeStruct((c_rows, n_modc), F32)],
            out_specs=[_spec2(c_rows, tn)])[0])
    mod = jnp.stack(mods)
    modg = jnp.concatenate(
        [jnp.broadcast_to(mod[:, :bp], (depth, n_p // grp, n_modc)), mod[:, bp:n_c]], axis=1)
    modg = modg.reshape(depth, ng, N_MOD, d).transpose(0, 2, 1, 3)
    modg = modg.reshape(depth * N_MOD, ng, 1, d)

    def mod_idx(l, k):
        return l * N_MOD + k

    def mspec_rows(l, k):
        return pl.BlockSpec((None, gpt, 1, tn), lambda j, i: (mod_idx(l, k), i, 0, j))

    pos = jnp.concatenate([jnp.arange(t_p, dtype=jnp.int32),
                           jnp.tile(past + jnp.arange(t_s, dtype=jnp.int32), bs)])
    rope_c, rope_s = _rope_tables(pos)
    tq = ATTN_TILE
    nq_p = t_p // tq
    tab_p = _causal_tables(nq_p, 0)
    tk_s = 512
    pt = past // tk_s
    tab_sb_s = lambda l: _causal_tables(0, n_p // t_s, n_batch=bs, past_tiles=pt,
                                        past_stride=l * bs * pt)
    tab_mla_s = _causal_tables(0, n_p // t_s, n_batch=bs, past_tiles=pt)

    off_q, off_k, off_v = 0, sb_w, 2 * sb_w
    off_cq = 3 * sb_w
    off_ckv = off_cq + q_lora
    off_kr = off_ckv + kv_lora
    off_g = off_kr + QK_ROPE
    assert off_g == off_kr + QK_ROPE
    w_nk = jnp.swapaxes(w_in, 1, 2)
    w_gates, w_kr_aug = _split_gate_rope_weights(w_nk, off_kr, 2 * d, tn)
    uq = w_uq.reshape(depth, q_lora, H_MLA, QK_NOPE + QK_ROPE)
    uq_r = uq[..., QK_NOPE:]
    w_uq_cat = jnp.concatenate([uq[..., :QK_NOPE], uq_r, _swap_halves(uq_r)], axis=-1)
    w_uq_cat = w_uq_cat.reshape(depth, q_lora, H_MLA * 2 * LANES)
    w_router = jnp.concatenate(
        [w_router_expert, w_router_group,
         jnp.zeros((depth, d, LANES - N_EXPERTS - N_GROUPS), F32)], axis=-1)
    r_hi = w_router.astype(BF16)
    r_res = w_router - r_hi.astype(F32)
    r_mid = r_res.astype(BF16)
    r_lo = (r_res - r_mid.astype(F32)).astype(BF16)
    w_router3 = jnp.stack([r_hi, r_mid, r_lo], axis=1)
    b_router = jnp.concatenate(
        [b_router_expert, b_router_group,
         jnp.zeros((depth, LANES - N_EXPERTS - N_GROUPS), F32)], axis=-1).reshape(depth, 1, LANES)

    rows_past = bs * past
    kr_past = jnp.pad(cache_mla_krope.reshape(depth * rows_past, QK_ROPE),
                      ((0, 0), (0, LANES - QK_ROPE))).astype(BF16)
    w_ukv_b = w_ukv.astype(BF16)
    hm = lambda rows: jax.ShapeDtypeStruct((H_SB, rows, LANES), BF16)
    new_k, new_v, new_c, new_r = [], [], [], []

    for l in range(depth):
        h = _norm_mod(x3, g_norm_mix.reshape(depth, 1, d), modg, l, mod_idx(l, 1), mod_idx(l, 0))
        h = h.reshape(n, d)

        sbq_scale = DH_SB ** -0.5 * float(np.log2(np.e))

        def plain_hm(accs, ex, outs):
            _store_heads(outs[0], accs[0] * sbq_scale)

        def f32_and_hm(accs, ex, outs):
            outs[0][...] = accs[0]
            _store_heads(outs[1], accs[0])

        sb_q = _matmul([(h, w_nk, _wspec_nk(l, d, tn, off_q), True, True)], m=n, n_out=sb_w, tm=tm, tn=tn,
                       epilogue=plain_hm, out_shape=[hm(n)], out_specs=[_hm_spec(tm, tn)])[0]
        k_f32, sb_k = _matmul([(h, w_nk, _wspec_nk(l, d, tn, off_k), True, True)], m=n, n_out=sb_w, tm=tm,
                              tn=tn, epilogue=f32_and_hm,
                              out_shape=[jax.ShapeDtypeStruct((n, sb_w), F32), hm(n)],
                              out_specs=[_spec2(tm, tn), _hm_spec(tm, tn)])
        v_f32, sb_v = _matmul([(h, w_nk, _wspec_nk(l, d, tn, off_v), True, True)], m=n, n_out=sb_w, tm=tm,
                              tn=tn, epilogue=f32_and_hm,
                              out_shape=[jax.ShapeDtypeStruct((n, sb_w), F32), hm(n)],
                              out_specs=[_spec2(tm, tn), _hm_spec(tm, tn)])

        def plain_f32(accs, ex, outs):
            outs[0][...] = accs[0]

        c_q = _matmul([(h, w_nk, _wspec_nk(l, d, tn, off_cq), True, True)], m=n, n_out=q_lora, tm=tm, tn=tn,
                      epilogue=plain_f32, out_shape=[jax.ShapeDtypeStruct((n, q_lora), F32)],
                      out_specs=[_spec2(tm, tn)])[0]

        def ckv_epi(accs, ex, outs):
            y = _rms(accs[0], ex[0][...])
            outs[0][...] = y
            outs[1][...] = y.astype(BF16)

        c_kv, c_kv_b = _matmul(
            [(h, w_nk, _wspec_nk(l, d, kv_lora, off_ckv), True, True)], m=n, n_out=kv_lora, tm=tm, tn=kv_lora,
            epilogue=ckv_epi, extras=[g_kv_lat.reshape(depth, 1, kv_lora)],
            extra_specs=[pl.BlockSpec((None, 1, kv_lora), lambda j, i: (l, 0, 0))],
            out_shape=[jax.ShapeDtypeStruct((n, kv_lora), F32),
                       jax.ShapeDtypeStruct((n, kv_lora), BF16)],
            out_specs=[_spec2(tm, kv_lora), _spec2(tm, kv_lora)])

        def kr_epi(accs, ex, outs):
            r = _rope_lanes(accs[0], ex[0][...], ex[1][...])
            outs[0][...] = r
            outs[1][...] = r.astype(BF16)

        rope_specs = [pl.BlockSpec((tm, LANES), lambda j, i: (i, 0))] * 2
        k_r, k_r_b = _matmul(
            [(h, w_kr_aug, pl.BlockSpec((None, LANES, d), lambda j, i: (l, 0, 0)), False, True)],
            m=n, n_out=LANES, tm=tm, tn=LANES, epilogue=kr_epi,
            extras=[rope_c, rope_s], extra_specs=rope_specs,
            out_shape=[jax.ShapeDtypeStruct((n, LANES), F32), jax.ShapeDtypeStruct((n, LANES), BF16)],
            out_specs=[_spec2(tm, LANES), _spec2(tm, LANES)])

        def gate_epi(accs, ex, outs):
            outs[0][...] = jax.nn.sigmoid(accs[0])

        gates = _matmul(
            [(h, w_gates, pl.BlockSpec((None, tn, d), lambda j, i: (l, j, 0)), False, True)],
            m=n, n_out=2 * d, tm=tm, tn=tn, epilogue=gate_epi,
            out_shape=[jax.ShapeDtypeStruct((n, 2 * d), F32)], out_specs=[_spec2(tm, tn)])[0]

        def cq_prologue(a, ex):
            return _rms(a, ex[0][...])

        gq_spec = pl.BlockSpec((None, 1, q_lora), lambda j, i: (l, 0, 0))
        gq = g_q_lat.reshape(depth, 1, q_lora)
        wqk = 2 * LANES
        hm_qk = lambda rows: jax.ShapeDtypeStruct((H_MLA, rows, wqk), BF16)

        q_scale = (QK_NOPE + QK_ROPE) ** -0.5 * float(np.log2(np.e))

        def qcat_epi(accs, ex, outs):
            c, s = ex[1][...], ex[2][...]
            for hh in range(tn // wqk):
                blk = accs[0][:, hh * wqk:(hh + 1) * wqk]
                outs[0][hh, :, :LANES] = (blk[:, :LANES] * q_scale).astype(BF16)
                outs[0][hh, :, LANES:] = (_rope_lanes(blk[:, LANES:], c, s) * q_scale).astype(BF16)

        q_cat = _matmul(
            [(c_q, w_uq_cat, pl.BlockSpec((None, q_lora, tn), lambda j, i: (l, 0, j)), True)],
            m=n, n_out=H_MLA * wqk, tm=tm, tn=tn, prologue=cq_prologue, epilogue=qcat_epi,
            extras=[gq, rope_c, rope_s], extra_specs=[gq_spec] + rope_specs,
            out_shape=[hm_qk(n)],
            out_specs=[pl.BlockSpec((tn // wqk, tm, wqk), lambda j, i: (j, i, 0))])[0]

        tn_kv = min(2048, H_MLA * wqk)
        hpt = tn_kv // wqk

        def kv_epi(accs, ex, outs):
            kr = ex[0][...]
            ones = jnp.ones(kr.shape, BF16)
            for hh in range(hpt):
                outs[0][hh, :, :LANES] = accs[0][:, hh * wqk:hh * wqk + LANES].astype(BF16)
                outs[0][hh, :, LANES:] = kr
                outs[1][hh, :, :LANES] = accs[0][:, hh * wqk + LANES:(hh + 1) * wqk].astype(BF16)
                outs[1][hh, :, LANES:] = ones

        def up_kv(a, kr, rows, tmr):
            hspec = pl.BlockSpec((hpt, tmr, wqk), lambda j, i: (j, i, 0))
            return _matmul(
                [(a, w_ukv, pl.BlockSpec((None, kv_lora, tn_kv), lambda j, i: (l, 0, j)), True)],
                m=rows, n_out=H_MLA * wqk, tm=tmr, tn=tn_kv, epilogue=kv_epi, extras=[kr],
                extra_specs=[pl.BlockSpec((tmr, LANES), lambda j, i: (i, 0))],
                out_shape=[hm_qk(rows), hm_qk(rows)], out_specs=[hspec, hspec])

        kc_new, v_new = up_kv(c_kv_b, k_r_b, n, tm)

        o_sb_p = _sb_stream_attention(sb_q, sb_k, sb_v, sb_k, sb_v, tq=tq, tk=tq, q_block0=0,
                                      n_tiles=n_p // tq)
        cache_k2 = cache_sb_k.reshape(depth * bs * past * H_SB, DH_SB)
        cache_v2 = cache_sb_v.reshape(depth * bs * past * H_SB, DH_SB)
        q_off_s = n_p // t_s
        o_sb_s = _sb_stream_attention(sb_q, sb_k, sb_v, cache_k2, cache_v2, tq=t_s, tk=ATTN_TILE,
                                      q_block0=q_off_s, n_tiles=bs,
                                      cache_tiles=past // ATTN_TILE,
                                      cache_base=l * bs * (past // ATTN_TILE))
        o_sb = jnp.concatenate([o_sb_p, o_sb_s], axis=0)

        tp3 = (tab_p[0], tab_p[1], tab_p[3])
        o_mla_p = _mla_attention(q_cat, kc_new, v_new, kc_new, v_new, tp3,
                                 tq=tq, tk=tq, pos0=0, out_rows=n_p,
                                 out_block_of=lambda s, qb, kb, fl: (qb[s], 0))
        tab_s = tab_sb_s(l)
        o_mla_s = _mla_latent_attention(
            q_cat, c_kv_b, k_r_b, cache_mla_ckv.reshape(depth * rows_past, kv_lora), kr_past,
            w_ukv_b, (tab_s[0], tab_s[1], tab_s[3]), l, tq=t_s, tk=tk_s, pos0=past, out_rows=n_s,
            out_block_of=lambda s, qb, kb, fl: (qb[s] - q_off_s, 0))
        o_mla = jnp.concatenate([o_mla_p, o_mla_s], axis=0)

        def merge_epi(accs, ex, outs):
            outs[0][...] = (ex[0][...] * accs[0] + ex[1][...] * accs[1]).astype(BF16)

        merged = _matmul(
            [(o_sb, w_branch_sb, _wspec(l, sb_w, tn, 0), True),
             (o_mla, w_branch_mla, _wspec(l, H_MLA * V_DIM, tn, 0), True)],
            m=n, n_out=d, tm=tm, tn=tn, epilogue=merge_epi,
            extras=[gates, gates],
            extra_specs=[pl.BlockSpec((tm, tn), lambda j, i: (i, j)),
                         pl.BlockSpec((tm, tn), lambda j, i: (i, j + d // tn))],
            out_shape=[jax.ShapeDtypeStruct((n, d), BF16)], out_specs=[_spec2(tm, tn)])[0]

        def resid_epi(accs, ex, outs):
            for g in range(gpt):
                outs[0][g] = ex[0][g] + ex[1][g] * accs[0][g * grp:(g + 1) * grp, :]

        x_spec = pl.BlockSpec((gpt, grp, tn), lambda j, i: (i, 0, j))
        x3 = _matmul(
            [(merged, w_out, _wspec(l, d, tn, 0), True)], m=n, n_out=d, tm=tm, tn=tn,
            epilogue=resid_epi, extras=[x3, modg], extra_specs=[x_spec, mspec_rows(l, 2)],
            out_shape=[jax.ShapeDtypeStruct((ng, grp, d), F32)], out_specs=[x_spec])[0]

        h2, route = _norm_route(x3, g_norm_ffn.reshape(depth, 1, d), modg, l, mod_idx(l, 4),
                                mod_idx(l, 3), w_router3, b_router)
        row_tok, row_dst, row_w, tile_expert, n_valid = _dispatch(route.reshape(n, LANES), n)
        yg = _moe_experts(h2, row_tok, row_dst, row_w, tile_expert, n_valid,
                          w_exp_gate, w_exp_up, w_exp_down, l, 2 * n)
        x3 = _combine(x3, yg, modg, l, mod_idx(l, 5))

        new_k.append(k_f32)
        new_v.append(v_f32)
        new_c.append(c_kv)
        new_r.append(k_r[:, :QK_ROPE])

    y_p, y_s = _final_norm(x3, g_final.reshape(1, d), n_p // grp)

    def split(parts, tail):
        a = jnp.stack(parts)
        return (a[:, :n_p].reshape((depth, bp, t_p) + tail),
                a[:, n_p:].reshape((depth, bs, t_s) + tail))

    pk, sk = split(new_k, (H_SB, DH_SB))
    pv, sv = split(new_v, (H_SB, DH_SB))
    pc, sc = split(new_c, (kv_lora,))
    pr, sr = split(new_r, (QK_ROPE,))
    return (y_p.reshape(bp, t_p, d), y_s.reshape(bs, t_s, d), pk, pv, pc, pr, sk, sv, sc, sr)
```

```python
import functools

import numpy as np
import jax
import jax.numpy as jnp
from jax import lax
from jax.experimental import pallas as pl
from jax.experimental.pallas import tpu as pltpu

F32 = jnp.float32
BF16 = jnp.bfloat16

CHUNK = 64
H_SB = 16
DH_SB = 128
H_MLA = 16
QK_NOPE = 128
QK_ROPE = 64
V_DIM = 128
ROPE_THETA = 10000.0
N_GROUPS = 4
EXPERTS_PER_GROUP = 8
N_EXPERTS = N_GROUPS * EXPERTS_PER_GROUP
N_MOD = 6
EPS = 1e-6

LANES = 128
ATTN_TILE = 256
MOE_TILE = 256
HEAD_UNROLL = 4
MLA_UNROLL = 16
SLAB_PAD = 4
SB_DEAD = 152.0
VMEM_LIMIT = 56 * 1024 * 1024


def _cparams(n_axes, vmem=VMEM_LIMIT):
    return pltpu.CompilerParams(dimension_semantics=("arbitrary",) * n_axes,
                                vmem_limit_bytes=vmem)


def _row_tile(n, cap=512):
    t = cap
    while n % t:
        t //= 2
    return t


def _token_tile(n, grp, cap=1152):
    return max(t for t in range(grp, cap + 1, grp) if n % t == 0)


def _matmul(pairs, *, m, n_out, tm, tn, epilogue, out_shape, out_specs,
            extras=(), extra_specs=(), prologue=None):
    n_pairs = len(pairs)
    n_ex = len(extras)
    n_outs = len(out_shape)
    pairs = [tuple(p) + (False,) * (5 - len(p)) for p in pairs]
    cast = [p[3] for p in pairs]
    b_nk = [p[4] for p in pairs]

    def kern(*refs):
        a_refs = refs[0:2 * n_pairs:2]
        b_refs = refs[1:2 * n_pairs:2]
        ex = refs[2 * n_pairs:2 * n_pairs + n_ex]
        outs = refs[2 * n_pairs + n_ex:2 * n_pairs + n_ex + n_outs]
        scr = refs[2 * n_pairs + n_ex + n_outs:]
        i = pl.program_id(1)
        accs = []
        si = 0
        for p in range(n_pairs):
            if cast[p]:
                bsc = scr[si]
                si += 1

                @pl.when(i == 0)
                def _(bsc=bsc, b_ref=b_refs[p]):
                    bsc[...] = b_ref[...].astype(BF16)

                bv = bsc[...]
            else:
                bv = b_refs[p][...]
            a = a_refs[p][...]
            if prologue is not None:
                a = prologue(a, ex)
            if b_nk[p]:
                accs.append(_dot_nt(a.astype(BF16), bv))
            else:
                accs.append(jnp.dot(a.astype(BF16), bv, preferred_element_type=F32))
        epilogue(accs, ex, outs)

    in_specs, args, scratch = [], [], []
    for (a, b, b_spec, cb, nk) in pairs:
        k = a.shape[1]
        in_specs += [pl.BlockSpec((tm, k), lambda j, i: (i, 0)), b_spec]
        args += [a, b]
        if cb:
            scratch.append(pltpu.VMEM((tn, k) if nk else (k, tn), BF16))
    in_specs += list(extra_specs)
    args += list(extras)
    return pl.pallas_call(
        kern,
        grid=(n_out // tn, m // tm),
        in_specs=in_specs,
        out_specs=out_specs,
        out_shape=out_shape,
        scratch_shapes=scratch,
        compiler_params=_cparams(2),
    )(*args)


def _wspec(l, k, tn, col_off):
    cb = col_off // tn
    assert cb * tn == col_off
    return pl.BlockSpec((None, k, tn), lambda j, i: (l, 0, cb + j))


def _wspec_nk(l, k, tn, row_off):
    rb = row_off // tn
    assert rb * tn == row_off
    return pl.BlockSpec((None, tn, k), lambda j, i: (l, rb + j, 0))


def _spec2(tm, tn):
    return pl.BlockSpec((tm, tn), lambda j, i: (i, j))


def _hm_spec(tm, tn):
    return pl.BlockSpec((tn // LANES, tm, LANES), lambda j, i: (j, i, 0))


def _store_heads(o_ref, val):
    for c in range(val.shape[1] // LANES):
        o_ref[c] = val[:, c * LANES:(c + 1) * LANES].astype(o_ref.dtype)


def _rms(x, g):
    return x * lax.rsqrt(jnp.mean(x * x, axis=-1, keepdims=True) + EPS) * g


def _norm_mod_kernel(x_ref, g_ref, sc_ref, sh_ref, o_ref):
    x = x_ref[...]
    y = _rms(x, g_ref[...])
    o_ref[...] = (y * (1.0 + sc_ref[...]) + sh_ref[...]).astype(o_ref.dtype)


def _split3(x):
    hi = x.astype(BF16)
    r = x - hi.astype(F32)
    mid = r.astype(BF16)
    lo = (r - mid.astype(F32)).astype(BF16)
    return hi, mid, lo


def _dot_f32(a, b3):
    a_hi, a_mid, a_lo = _split3(a)
    b_hi, b_mid, b_lo = b3
    d = functools.partial(jnp.dot, preferred_element_type=F32)
    small = d(a_hi, b_lo) + d(a_lo, b_hi) + d(a_mid, b_mid)
    return (d(a_hi, b_hi) + (d(a_hi, b_mid) + d(a_mid, b_hi))) + small


def _route(logits):
    lane = lax.broadcasted_iota(jnp.int32, logits.shape, 1)
    lanef = lane.astype(F32)
    big = jnp.float32(1e9)
    ninf = jnp.float32(-jnp.inf)
    is_g = (lane >= N_EXPERTS) & (lane < N_EXPERTS + N_GROUPS)
    gl = jnp.where(is_g, logits, ninf)
    gmax = jnp.max(gl, axis=1, keepdims=True)
    g_idx = jnp.min(jnp.where(gl == gmax, lanef - N_EXPERTS, big), axis=1, keepdims=True)
    p_group = 1.0 / jnp.sum(jnp.where(is_g, jnp.exp(gl - gmax), 0.0), axis=1, keepdims=True)
    grp = jnp.floor(lanef * (1.0 / EXPERTS_PER_GROUP))
    in_g = (lane < N_EXPERTS) & (grp == g_idx)
    el = jnp.where(in_g, logits, ninf)
    e1 = jnp.max(el, axis=1, keepdims=True)
    i1 = jnp.min(jnp.where(el == e1, lanef, big), axis=1, keepdims=True)
    el2 = jnp.where(lanef == i1, ninf, el)
    e2 = jnp.max(el2, axis=1, keepdims=True)
    i2 = jnp.min(jnp.where(el2 == e2, lanef, big), axis=1, keepdims=True)
    t = jnp.exp(e2 - e1)
    den = 1.0 + t
    w1 = (1.0 / den) * p_group
    w2 = (t / den) * p_group
    out = jnp.where(lane == 0, i1, jnp.where(lane == 1, i2,
          jnp.where(lane == 2, w1, jnp.where(lane == 3, w2, 0.0))))
    return out


def _slab_pitch(s):
    return s + SLAB_PAD


def _slab_store(ref, row0, val):
    rows, d = val.shape
    s = d // LANES
    p = _slab_pitch(s)
    for c in range(s):
        ref[pl.ds(row0 * p + c, rows, stride=p), :] = val[:, c * LANES:(c + 1) * LANES]


def _slab_load(ref, row0, rows, s, lead=None):
    pieces = []
    p = _slab_pitch(s)
    for c in range(s):
        rs = pl.ds(row0 * p + c, rows, stride=p)
        pieces.append(ref[rs, :] if lead is None else ref[lead, rs, :])
    return jnp.concatenate(pieces, axis=1)


def _norm_route_kernel(x_ref, g_ref, sc_ref, sh_ref, wr_ref, br_ref, h_ref, r_ref):
    x = x_ref[...]
    y = _rms(x, g_ref[...])
    h = y * (1.0 + sc_ref[...]) + sh_ref[...]
    b3 = (wr_ref[0], wr_ref[1], wr_ref[2])
    grp = x.shape[1]
    for g in range(x.shape[0]):
        _slab_store(h_ref, g * grp, h[g])
        logits = _dot_f32(h[g], b3) + br_ref[...]
        r_ref[g] = _route(logits)


def _mod_spec(idx, gb, d):
    return pl.BlockSpec((None, gb, 1, d), lambda i: (idx, i, 0, 0))


def _norm_mod(x3, g, modg, l, sc_i, sh_i):
    ng, grp, d = x3.shape
    gb = _row_tile(ng, 4)
    return pl.pallas_call(
        _norm_mod_kernel,
        grid=(ng // gb,),
        in_specs=[pl.BlockSpec((gb, grp, d), lambda i: (i, 0, 0)),
                  pl.BlockSpec((None, 1, d), lambda i: (l, 0, 0)),
                  _mod_spec(sc_i, gb, d), _mod_spec(sh_i, gb, d)],
        out_specs=pl.BlockSpec((gb, grp, d), lambda i: (i, 0, 0)),
        out_shape=jax.ShapeDtypeStruct(x3.shape, BF16),
        compiler_params=_cparams(1),
    )(x3, g, modg, modg)


def _norm_route(x3, g, modg, l, sc_i, sh_i, wr3, br):
    ng, grp, d = x3.shape
    gb = _row_tile(ng, 4)
    return pl.pallas_call(
        _norm_route_kernel,
        grid=(ng // gb,),
        in_specs=[pl.BlockSpec((gb, grp, d), lambda i: (i, 0, 0)),
                  pl.BlockSpec((None, 1, d), lambda i: (l, 0, 0)),
                  _mod_spec(sc_i, gb, d), _mod_spec(sh_i, gb, d),
                  pl.BlockSpec((None, 3, d, LANES), lambda i: (l, 0, 0, 0)),
                  pl.BlockSpec((None, 1, LANES), lambda i: (l, 0, 0))],
        out_specs=[pl.BlockSpec((gb * grp * _slab_pitch(d // LANES), LANES), lambda i: (i, 0)),
                   pl.BlockSpec((gb, grp, LANES), lambda i: (i, 0, 0))],
        out_shape=[jax.ShapeDtypeStruct((ng * grp * _slab_pitch(d // LANES), LANES), F32),
                   jax.ShapeDtypeStruct((ng, grp, LANES), F32)],
        compiler_params=_cparams(1),
    )(x3, g, modg, modg, wr3, br)


def _final_norm_kernel(x_ref, g_ref, op_ref, os_ref, *, n_prompt_blocks):
    i = pl.program_id(0)
    y = _rms(x_ref[...], g_ref[...])

    @pl.when(i < n_prompt_blocks)
    def _():
        op_ref[...] = y

    @pl.when(i >= n_prompt_blocks)
    def _():
        os_ref[...] = y


def _final_norm(x3, g, ng_prompt):
    ng, grp, d = x3.shape
    gb = _row_tile(np.gcd(ng_prompt, ng - ng_prompt), 4)
    npb = ng_prompt // gb
    blk = lambda f: pl.BlockSpec((gb, grp, d), f)
    return pl.pallas_call(
        functools.partial(_final_norm_kernel, n_prompt_blocks=npb),
        grid=(ng // gb,),
        in_specs=[blk(lambda i: (i, 0, 0)), pl.BlockSpec((1, d), lambda i: (0, 0))],
        out_specs=[blk(lambda i: (jnp.minimum(i, npb - 1), 0, 0)),
                   blk(lambda i: (jnp.maximum(i - npb, 0), 0, 0))],
        out_shape=[jax.ShapeDtypeStruct((ng_prompt, grp, d), F32),
                   jax.ShapeDtypeStruct((ng - ng_prompt, grp, d), F32)],
        compiler_params=_cparams(1),
    )(x3, g)


def _lanes(c, w):
    if w % LANES == 0:
        return c if w == LANES else jnp.tile(c, (1, w // LANES))
    return c[:, :w]


def _dot_nt(a, b):
    return lax.dot_general(a, b, (((1,), (1,)), ((), ())), preferred_element_type=F32)


def _sb_block(qh, kh, vh, c, u, masked):
    w = kh.shape[0]
    z = _dot_nt(qh, kh)
    sp = jnp.maximum(z, 0.0) + jnp.log2(1.0 + jnp.exp2(-jnp.abs(z)))
    if masked:
        row = lax.broadcasted_iota(jnp.int32, z.shape, 0)
        col = lax.broadcasted_iota(jnp.int32, z.shape, 1)
        valid = col < row
        sp = jnp.where(valid, sp, 0.0)
    hi = sp.astype(BF16)
    lo = (sp - hi.astype(F32)).astype(BF16)
    cs = jnp.dot(hi, u, preferred_element_type=F32) + jnp.dot(lo, u, preferred_element_type=F32)
    wgt = jnp.exp2(z - sp - cs - _lanes(c, w))
    if masked:
        wgt = jnp.where(valid, wgt, 0.0)
    o = jnp.dot(wgt.astype(BF16), vh, preferred_element_type=F32)
    c_new = c + jnp.sum(sp, axis=1, keepdims=True)
    return o, c_new


def _suffix_matrix(w):
    j = np.arange(w)[:, None]
    s = np.arange(w)[None, :]
    return jnp.asarray((j > s).astype(np.float32), dtype=BF16)


def _sb_stream_kernel(q_ref, kn_ref, vn_ref, k_hbm, v_hbm, ud_ref, up_ref, o_ref,
                      kbuf, vbuf, acc, carry, done, alive_ref, sem, *, heads, tk, cache_tiles,
                      cache_base):
    i = pl.program_id(0)
    n_past = i if cache_tiles is None else cache_tiles

    def fetch(j, slot):
        if cache_tiles is None:
            rows = pl.ds(pl.multiple_of(j * tk, tk), tk)
            src_k, src_v = k_hbm.at[:, rows, :], v_hbm.at[:, rows, :]
        else:
            blk = tk * heads
            rows = pl.ds(pl.multiple_of((cache_base + i * cache_tiles + j) * blk, blk), blk)
            src_k, src_v = k_hbm.at[rows], v_hbm.at[rows]
        return (pltpu.make_async_copy(src_k, kbuf.at[slot], sem.at[0, slot]),
                pltpu.make_async_copy(src_v, vbuf.at[slot], sem.at[1, slot]))

    def tile_of(buf, slot, h):
        if cache_tiles is None:
            return buf[slot, h]
        return buf.at[slot][pl.ds(h, tk, stride=heads), :].astype(BF16)

    @pl.when(n_past > 0)
    def _():
        for c in fetch(n_past - 1, 0):
            c.start()

    def diag(h, _):
        o, c = _sb_block(q_ref[h], kn_ref[h], vn_ref[h], jnp.zeros(carry.shape[1:], F32),
                         ud_ref[...], True)
        acc[h] = o
        carry[h] = c
        done[h] = 0
        return 0
    lax.fori_loop(0, heads, diag, 0, unroll=min(heads, HEAD_UNROLL))

    alive_ref[0] = heads

    @pl.when(n_past > 0)
    def _():
        for c in fetch(n_past - 1, 0):
            c.wait()

        @pl.when(n_past > 1)
        def _():
            for c in fetch(n_past - 2, 1):
                c.start()

        def first(h, alive):
            o, c = _sb_block(q_ref[h], tile_of(kbuf, 0, h), tile_of(vbuf, 0, h), carry[h],
                             up_ref[...], False)
            acc[h] = acc[h] + o
            carry[h] = c
            dead = (jnp.min(c) >= SB_DEAD).astype(jnp.int32)
            done[h] = dead
            return alive + 1 - dead
        alive_ref[0] = lax.fori_loop(0, heads, first, 0, unroll=min(heads, HEAD_UNROLL))

    def cond(state):
        j, alive = state
        return (j >= 0) & (alive > 0)

    def body(state):
        j, _ = state
        slot = lax.rem(n_past - 1 - j, 2)
        for c in fetch(j, slot):
            c.wait()

        @pl.when(j > 0)
        def _():
            for c in fetch(j - 1, 1 - slot):
                c.start()

        def head(h, alive):
            @pl.when(done[h] == 0)
            def _():
                o, c = _sb_block(q_ref[h], tile_of(kbuf, slot, h), tile_of(vbuf, slot, h),
                                 carry[h], up_ref[...], False)
                acc[h] = acc[h] + o
                carry[h] = c
                done[h] = (jnp.min(c) >= SB_DEAD).astype(jnp.int32)
            return alive + 1 - done[h]
        return j - 1, lax.fori_loop(0, heads, head, 0)

    j_end, _ = lax.while_loop(cond, body, (jnp.int32(n_past - 2), alive_ref[0]))

    @pl.when(j_end >= 0)
    def _():
        for c in fetch(j_end, lax.rem(n_past - 1 - j_end, 2)):
            c.wait()

    for h in range(heads):
        o_ref[:, h * DH_SB:(h + 1) * DH_SB] = acc[h].astype(o_ref.dtype)


def _sb_stream_attention(q, kn, vn, k_src, v_src, *, tq, tk, q_block0, n_tiles, cache_tiles=None,
                         cache_base=0):
    heads = q.shape[0]
    blk = pl.BlockSpec((heads, tq, DH_SB), lambda i: (0, q_block0 + i, 0))
    anyspec = pl.BlockSpec(memory_space=pl.ANY)
    if cache_tiles is None:
        buf = pltpu.VMEM((2, heads, tk, DH_SB), BF16)
    else:
        buf = pltpu.VMEM((2, tk * heads, DH_SB), F32)
    kern = functools.partial(_sb_stream_kernel, heads=heads, tk=tk, cache_tiles=cache_tiles,
                             cache_base=cache_base)
    return pl.pallas_call(
        kern, grid=(n_tiles,),
        in_specs=[blk, blk, blk, anyspec, anyspec,
                  pl.BlockSpec((tq, tq), lambda i: (0, 0)), pl.BlockSpec((tk, tk), lambda i: (0, 0))],
        out_specs=pl.BlockSpec((tq, heads * DH_SB), lambda i: (i, 0)),
        out_shape=jax.ShapeDtypeStruct((n_tiles * tq, heads * DH_SB), BF16),
        scratch_shapes=[buf, buf,
                        pltpu.VMEM((heads, tq, DH_SB), F32),
                        pltpu.VMEM((heads, tq, LANES), F32),
                        pltpu.SMEM((heads,), jnp.int32),
                        pltpu.SMEM((1,), jnp.int32),
                        pltpu.SemaphoreType.DMA((2, 2))],
        compiler_params=_cparams(1),
    )(q, kn, vn, k_src, v_src, _suffix_matrix(tq), _suffix_matrix(tk))


def _mla_block(qc, kc, va, m, acc, mask):
    s = _dot_nt(qc, kc)
    if mask is not None:
        s = jnp.where(mask, s, -jnp.inf)
    m_new = jnp.maximum(m, jnp.max(s, axis=1, keepdims=True))
    alpha = jnp.exp2(m - m_new)
    p = jnp.exp2(s - _lanes(m_new, s.shape[1]))
    acc_new = _lanes(alpha, acc.shape[1]) * acc + jnp.dot(p.astype(BF16), va,
                                                          preferred_element_type=F32)
    return m_new, acc_new


def _mla_kernel(qb_ref, kb_ref, fl_ref, q_ref, kn_ref, vn_ref, kp_ref, vp_ref, o_ref,
                acc, m_sc, *, heads, pos0):
    s = pl.program_id(0)
    fl = fl_ref[s]
    is_first = (fl & 1) != 0
    is_last = (fl & 2) != 0
    tq = q_ref.shape[1]

    @pl.when(is_first)
    def _():
        row = lax.broadcasted_iota(jnp.int32, (tq, tq), 0) + pos0
        col = lax.broadcasted_iota(jnp.int32, (tq, tq), 1) + pos0
        mask = (col // CHUNK) <= (row // CHUNK)

        def body(h, _):
            m0 = jnp.full((tq, LANES), -jnp.inf, F32)
            a0 = jnp.zeros((tq, 2 * V_DIM), F32)
            m, a = _mla_block(q_ref[h], kn_ref[h], vn_ref[h], m0, a0, mask)
            m_sc[h] = m
            acc[h] = a
            return 0
        lax.fori_loop(0, heads, body, 0, unroll=min(heads, MLA_UNROLL))

    @pl.when(jnp.logical_not(is_first))
    def _():
        def body(h, _):
            m, a = _mla_block(q_ref[h], kp_ref[h], vp_ref[h], m_sc[h], acc[h], None)
            m_sc[h] = m
            acc[h] = a
            return 0
        lax.fori_loop(0, heads, body, 0, unroll=min(heads, MLA_UNROLL))

    @pl.when(is_last)
    def _():
        for h in range(heads):
            a = acc[h]
            o_ref[:, h * V_DIM:(h + 1) * V_DIM] = (a[:, :V_DIM] / a[:, V_DIM:]).astype(o_ref.dtype)


def _mla_attention(qc, kcn, vn, kcp, vp, tables, *, tq, tk, pos0, out_rows, out_block_of):
    qb, kb, fl = tables
    n_steps = qb.shape[0]
    h = H_MLA
    wqk = 2 * LANES
    new = pl.BlockSpec((h, tq, wqk), lambda s, qb, kb, fl: (0, qb[s], 0))
    past = pl.BlockSpec((h, tk, wqk), lambda s, qb, kb, fl: (0, kb[s], 0))
    grid_spec = pltpu.PrefetchScalarGridSpec(
        num_scalar_prefetch=3,
        grid=(n_steps,),
        in_specs=[new, new, new, past, past],
        out_specs=pl.BlockSpec((tq, h * V_DIM), out_block_of),
        scratch_shapes=[pltpu.VMEM((h, tq, 2 * V_DIM), F32),
                        pltpu.VMEM((h, tq, LANES), F32)],
    )
    kern = functools.partial(_mla_kernel, heads=h, pos0=pos0)
    return pl.pallas_call(
        kern, grid_spec=grid_spec,
        out_shape=jax.ShapeDtypeStruct((out_rows, h * V_DIM), BF16),
        compiler_params=_cparams(1),
    )(qb, kb, fl, qc, kcn, vn, kcp, vp)


def _mla_latent_kernel(qb_ref, kb_ref, fl_ref, q_ref, cn_ref, rn_ref, cp_ref, rp_ref, w_ref,
                       o_ref, qa, qr, acc, m_sc, l_sc, *, heads, pos0):
    s_id = pl.program_id(0)
    fl = fl_ref[s_id]
    is_first = (fl & 1) != 0
    is_last = (fl & 2) != 0
    tq = q_ref.shape[1]
    hw = QK_NOPE + V_DIM

    def step(ck, kr, mask):
        s = _dot_nt(qa[...], ck) + _dot_nt(qr[...], kr)
        if mask is not None:
            s = jnp.where(mask, s, -jnp.inf)
        m_old = m_sc[...]
        m_new = jnp.maximum(m_old, jnp.max(s, axis=1, keepdims=True))
        alpha = jnp.exp2(m_old - m_new)
        p = jnp.exp2(s - m_new[:, :1])
        l_sc[...] = alpha * l_sc[...] + jnp.sum(p, axis=1, keepdims=True)
        acc[...] = alpha[:, :1] * acc[...] + jnp.dot(p.astype(BF16), ck,
                                                     preferred_element_type=F32)
        m_sc[...] = m_new

    @pl.when(is_first)
    def _():
        for h in range(heads):
            qh = q_ref[h]
            w_uk = w_ref[:, h * hw:h * hw + QK_NOPE]
            qa[h * tq:(h + 1) * tq, :] = _dot_nt(qh[:, :QK_NOPE], w_uk).astype(BF16)
            qr[h * tq:(h + 1) * tq, :] = qh[:, QK_NOPE:]
        m_sc[...] = jnp.full(m_sc.shape, -jnp.inf, F32)
        l_sc[...] = jnp.zeros(l_sc.shape, F32)
        acc[...] = jnp.zeros(acc.shape, F32)
        row = lax.broadcasted_iota(jnp.int32, (heads * tq, tq), 0) % tq + pos0
        col = lax.broadcasted_iota(jnp.int32, (heads * tq, tq), 1) + pos0
        step(cn_ref[...], rn_ref[...], (col // CHUNK) <= (row // CHUNK))

    @pl.when(jnp.logical_not(is_first))
    def _():
        step(cp_ref[...].astype(BF16), rp_ref[...], None)

    @pl.when(is_last)
    def _():
        o_lat = (acc[...] / l_sc[...][:, :1]).astype(BF16)
        for h in range(heads):
            w_uv = w_ref[:, h * hw + QK_NOPE:(h + 1) * hw]
            o_ref[:, h * V_DIM:(h + 1) * V_DIM] = jnp.dot(
                o_lat[h * tq:(h + 1) * tq, :], w_uv, preferred_element_type=F32).astype(o_ref.dtype)


def _mla_latent_attention(qc, c_new, r_new, c_past, r_past, w_ukv_b, tables, l, *, tq, tk, pos0,
                          out_rows, out_block_of):
    qb, kb, fl = tables
    h = H_MLA
    kvl = c_new.shape[1]
    wqk = 2 * LANES
    grid_spec = pltpu.PrefetchScalarGridSpec(
        num_scalar_prefetch=3,
        grid=(qb.shape[0],),
        in_specs=[pl.BlockSpec((h, tq, wqk), lambda s, qb, kb, fl: (0, qb[s], 0)),
                  pl.BlockSpec((tq, kvl), lambda s, qb, kb, fl: (qb[s], 0)),
                  pl.BlockSpec((tq, LANES), lambda s, qb, kb, fl: (qb[s], 0)),
                  pl.BlockSpec((tk, kvl), lambda s, qb, kb, fl: (kb[s], 0)),
                  pl.BlockSpec((tk, LANES), lambda s, qb, kb, fl: (kb[s], 0)),
                  pl.BlockSpec((None, kvl, w_ukv_b.shape[2]), lambda s, qb, kb, fl: (l, 0, 0))],
        out_specs=pl.BlockSpec((tq, h * V_DIM), out_block_of),
        scratch_shapes=[pltpu.VMEM((h * tq, kvl), BF16),
                        pltpu.VMEM((h * tq, LANES), BF16),
                        pltpu.VMEM((h * tq, kvl), F32),
                        pltpu.VMEM((h * tq, LANES), F32),
                        pltpu.VMEM((h * tq, LANES), F32)],
    )
    kern = functools.partial(_mla_latent_kernel, heads=h, pos0=pos0)
    return pl.pallas_call(
        kern, grid_spec=grid_spec,
        out_shape=jax.ShapeDtypeStruct((out_rows, h * V_DIM), BF16),
        compiler_params=_cparams(1),
    )(qb, kb, fl, qc, c_new, r_new, c_past, r_past, w_ukv_b)


def _causal_tables(nq, q_off, per_head=False, heads=1, n_batch=1, past_tiles=None,
                   past_stride=0):
    qb, kb, hb, fl = [], [], [], []
    if past_tiles is None:
        for i in range(nq):
            n = i + 1
            for j in range(n):
                qb.append(q_off + i)
                kb.append(max(i - j, 1) - 1 if j == 0 else i - j)
                hb.append(0)
                fl.append((1 if j == 0 else 0) | (2 if j == n - 1 else 0))
    else:
        for b in range(n_batch):
            for h in range(heads if per_head else 1):
                n = 1 + past_tiles
                for j in range(n):
                    qb.append(q_off + b)
                    jj = past_tiles - 1 if j == 0 else past_tiles - j
                    kb.append(past_stride + b * past_tiles + jj)
                    hb.append(h)
                    fl.append((1 if j == 0 else 0) | (2 if j == n - 1 else 0))
    arr = lambda v: jnp.asarray(np.asarray(v, dtype=np.int32))
    return arr(qb), arr(kb), arr(hb), arr(fl)


ROW_DMA_UNROLL = 8


def _moe_kernel(te_ref, nv_ref, tok_ref, dst_ref, h_hbm, w_ref, wg_ref, wu_ref, wd_ref, y_hbm,
                xbuf, obuf, wg_s, wu_s, wd_s, sem_in, sem_out, *, s):
    t = pl.program_id(0)
    tm = w_ref.shape[0]
    nv = nv_ref[0]
    live = t < nv
    slot = lax.rem(t, 2)

    p = _slab_pitch(s)

    def in_copy(tile, r, sl):
        src0 = pl.multiple_of(tok_ref[tile * tm + r] * p, SLAB_PAD)
        return pltpu.make_async_copy(h_hbm.at[pl.ds(src0, s)],
                                     xbuf.at[sl, pl.ds(pl.multiple_of(r * p, SLAB_PAD), s)],
                                     sem_in.at[sl])

    def gather_start(tile, sl):
        def body(r, _):
            in_copy(tile, r, sl).start()
            return 0
        lax.fori_loop(0, tm, body, 0, unroll=ROW_DMA_UNROLL)

    def gather_wait(tile, sl):
        def body(r, _):
            in_copy(tile, r, sl).wait()
            return 0
        lax.fori_loop(0, tm, body, 0, unroll=ROW_DMA_UNROLL)

    @pl.when(live & (t == 0))
    def _():
        gather_start(0, 0)

    @pl.when(t == nv)
    def _():
        gather_wait(t, slot)

    prev = te_ref[jnp.maximum(t - 1, 0)]
    fresh = (t == 0) | (te_ref[t] != prev)

    @pl.when(live & fresh)
    def _():
        wg_s[...] = wg_ref[...].astype(BF16)
        wu_s[...] = wu_ref[...].astype(BF16)
        wd_s[...] = wd_ref[...].astype(BF16)

    @pl.when(live)
    def _():
        gather_wait(t, slot)
        for r in range(tm):
            in_copy(t + 1, r, 1 - slot).start()
        x = _slab_load(xbuf, 0, tm, s, lead=slot).astype(BF16)
        a = jnp.dot(x, wg_s[...], preferred_element_type=F32)
        u = jnp.dot(x, wu_s[...], preferred_element_type=F32)
        hid = (a * jax.nn.sigmoid(a)) * u * w_ref[...]
        y = jnp.dot(hid.astype(BF16), wd_s[...], preferred_element_type=F32)

        def out_copy(tile, r):
            dst0 = pl.multiple_of(dst_ref[tile * tm + r] * p, SLAB_PAD)
            return pltpu.make_async_copy(obuf.at[pl.ds(pl.multiple_of(r * p, SLAB_PAD), s)],
                                         y_hbm.at[pl.ds(dst0, s)], sem_out)

        def scatter_wait(tile):
            def wait(r, _):
                out_copy(tile, r).wait()
                return 0
            lax.fori_loop(0, tm, wait, 0, unroll=ROW_DMA_UNROLL)

        @pl.when(t > 0)
        def _():
            scatter_wait(t - 1)

        _slab_store(obuf, 0, y)

        def start(r, _):
            out_copy(t, r).start()
            return 0
        lax.fori_loop(0, tm, start, 0, unroll=ROW_DMA_UNROLL)

        @pl.when(t == nv - 1)
        def _():
            scatter_wait(t)

        @pl.when(t == pl.num_programs(0) - 1)
        def _():
            gather_wait(t + 1, 1 - slot)


def _moe_experts(h_slab, row_tok, row_dst, row_w, tile_expert, n_valid, w_gate, w_up, w_down, l,
                 n_dest):
    d, f = w_gate.shape[-2:]
    s = d // LANES
    p = _slab_pitch(s)
    tm = MOE_TILE
    n_tiles = row_dst.shape[0] // tm
    wmap = lambda t, te, nv, tok, dst: (l, te[t], 0, 0)
    grid_spec = pltpu.PrefetchScalarGridSpec(
        num_scalar_prefetch=4, grid=(n_tiles,),
        in_specs=[pl.BlockSpec(memory_space=pl.ANY),
                  pl.BlockSpec((tm, 1), lambda t, te, nv, tok, dst: (t, 0)),
                  pl.BlockSpec((None, None, d, f), wmap),
                  pl.BlockSpec((None, None, d, f), wmap),
                  pl.BlockSpec((None, None, f, d), wmap)],
        out_specs=pl.BlockSpec(memory_space=pl.ANY),
        scratch_shapes=[pltpu.VMEM((2, tm * p, LANES), F32), pltpu.VMEM((tm * p, LANES), F32),
                        pltpu.VMEM((d, f), BF16), pltpu.VMEM((d, f), BF16),
                        pltpu.VMEM((f, d), BF16),
                        pltpu.SemaphoreType.DMA((2,)), pltpu.SemaphoreType.DMA(())],
    )
    return pl.pallas_call(
        functools.partial(_moe_kernel, s=s), grid_spec=grid_spec,
        out_shape=jax.ShapeDtypeStruct(((n_dest + tm) * p, LANES), F32),
        compiler_params=_cparams(1),
    )(tile_expert, n_valid, row_tok, row_dst, h_slab, row_w, w_gate, w_up, w_down)


def _combine_kernel(x_ref, y0_ref, y1_ref, g_ref, o_ref):
    gb, grp, d = x_ref.shape
    s = d // LANES
    for g in range(gb):
        y = _slab_load(y0_ref, g * grp, grp, s) + _slab_load(y1_ref, g * grp, grp, s)
        o_ref[g] = x_ref[g] + g_ref[g] * y


def _combine(x3, yg, modg, l, g_i):
    ng, grp, d = x3.shape
    s = d // LANES
    gb = _row_tile(ng, 2)
    blk = pl.BlockSpec((gb, grp, d), lambda i: (i, 0, 0))
    yblk = lambda off: pl.BlockSpec((gb * grp * _slab_pitch(s), LANES), lambda i: (i + off, 0))
    return pl.pallas_call(
        _combine_kernel,
        grid=(ng // gb,),
        in_specs=[blk, yblk(0), yblk(ng // gb), _mod_spec(g_i, gb, d)],
        out_specs=blk,
        out_shape=jax.ShapeDtypeStruct(x3.shape, F32),
        compiler_params=_cparams(1),
    )(x3, yg, yg, modg)


def _dispatch(route, n):
    tm = MOE_TILE
    r_cap = (2 * n + N_EXPERTS * (tm - 1) + tm - 1) // tm * tm
    e = route[:, :2].astype(jnp.int32)
    w = route[:, 2:4]
    flat_e = e.T.reshape(-1)
    flat_w = w.T.reshape(-1)
    order = jnp.argsort(flat_e, stable=True).astype(jnp.int32)
    bounds = jnp.searchsorted(flat_e[order], jnp.arange(N_EXPERTS + 1, dtype=jnp.int32),
                              side="left").astype(jnp.int32)
    counts = bounds[1:] - bounds[:-1]
    padded = (counts + tm - 1) // tm * tm
    ends_p = jnp.cumsum(padded)
    starts_p = ends_p - padded
    starts = jnp.cumsum(counts) - counts
    tile_start = jnp.arange(r_cap // tm, dtype=jnp.int32) * tm
    tile_expert = jnp.minimum(jnp.searchsorted(ends_p, tile_start, side="right"),
                              N_EXPERTS - 1).astype(jnp.int32)
    n_valid = (ends_p[-1:] // tm).astype(jnp.int32)
    per_row = lambda v: jnp.broadcast_to(v[:, None], (r_cap // tm, tm)).reshape(r_cap)
    local = jnp.arange(r_cap, dtype=jnp.int32) - per_row(starts_p[tile_expert])
    valid = local < per_row(counts[tile_expert])
    slot = order[jnp.clip(per_row(starts[tile_expert]) + local, 0, 2 * n - 1)]
    row_dst = jnp.where(valid, slot, 2 * n + jnp.arange(r_cap, dtype=jnp.int32) % tm)
    row_tok = jnp.where(valid, jnp.where(slot >= n, slot - n, slot), 0)
    row_tok = jnp.concatenate([row_tok, jnp.zeros((tm,), jnp.int32)])
    row_w = jnp.where(valid, flat_w[slot], 0.0)
    return row_tok, row_dst, row_w.reshape(r_cap, 1), tile_expert, n_valid


def _gate_weights_kernel(a_ref, b_ref, o_ref):
    r = b_ref.shape[0]
    w = a_ref.shape[0]
    o_ref[:w - r, :] = a_ref[r:, :].astype(BF16)
    o_ref[w - r:, :] = b_ref[...].astype(BF16)


def _rope_weights_kernel(a_ref, o_ref):
    a = a_ref[...].astype(BF16)
    q = a.shape[0] // 2
    o_ref[...] = jnp.concatenate([a, a[q:], a[:q]], axis=0)


def _split_gate_rope_weights(w_nk, off_kr, n_gate, tn):
    depth, n_in, k = w_nk.shape
    assert off_kr % tn == 0 and n_gate % tn == 0 and QK_ROPE * 2 == LANES
    rb = off_kr // tn
    gates = pl.pallas_call(
        _gate_weights_kernel,
        grid=(depth, n_gate // tn),
        in_specs=[pl.BlockSpec((None, tn, k), lambda l, j: (l, rb + j, 0)),
                  pl.BlockSpec((None, QK_ROPE, k),
                               lambda l, j: (l, (rb + j + 1) * (tn // QK_ROPE), 0))],
        out_specs=pl.BlockSpec((None, tn, k), lambda l, j: (l, j, 0)),
        out_shape=jax.ShapeDtypeStruct((depth, n_gate, k), BF16),
        compiler_params=_cparams(2),
    )(w_nk, w_nk)
    rope = pl.pallas_call(
        _rope_weights_kernel,
        grid=(depth,),
        in_specs=[pl.BlockSpec((None, QK_ROPE, k), lambda l: (l, off_kr // QK_ROPE, 0))],
        out_specs=pl.BlockSpec((None, 2 * QK_ROPE, k), lambda l: (l, 0, 0)),
        out_shape=jax.ShapeDtypeStruct((depth, 2 * QK_ROPE, k), BF16),
        compiler_params=_cparams(1),
    )(w_nk)
    return gates, rope


def _rope_tables(pos):
    inv = ROPE_THETA ** (-jnp.arange(0, QK_ROPE, 2, dtype=F32) / QK_ROPE)
    ang = pos.astype(F32)[:, None] * inv[None, :]
    cos, sin = jnp.cos(ang), jnp.sin(ang)
    pad = jnp.zeros((pos.shape[0], LANES - QK_ROPE), F32)
    return (jnp.concatenate([cos, cos, pad], axis=1),
            jnp.concatenate([-sin, sin, pad], axis=1))


def _rope_lanes(acc, c, s):
    return acc * c + pltpu.roll(acc, LANES - QK_ROPE, axis=1) * s


def _swap_halves(w):
    half = w.shape[-1] // 2
    return jnp.concatenate([w[..., half:], w[..., :half]], axis=-1)


def kernel(x_prompt, x_sample, c_prompt, c_sample, cache_sb_k, cache_sb_v, cache_mla_ckv,
           cache_mla_krope, w_ada, b_ada, g_norm_mix, g_norm_ffn, w_in, g_q_lat, g_kv_lat,
           w_uq, w_ukv, w_branch_sb, w_branch_mla, w_out, w_router_group, b_router_group,
           w_router_expert, b_router_expert, w_exp_gate, w_exp_up, w_exp_down, g_final):
    bp, t_p, d = x_prompt.shape
    bs, t_s, _ = x_sample.shape
    depth = w_in.shape[0]
    past = cache_sb_k.shape[2]
    grp = t_s
    n_p, n_s = bp * t_p, bs * t_s
    n = n_p + n_s
    ng = n // grp
    sb_w = H_SB * DH_SB
    q_lora = g_q_lat.shape[1]
    kv_lora = g_kv_lat.shape[1]
    tm = _token_tile(n, grp)
    gpt = tm // grp
    tn = 512
    assert bp == 1 and t_p % ATTN_TILE == 0 and t_p % grp == 0 and past % 512 == 0

    x3 = jnp.concatenate([x_prompt.reshape(n_p // grp, grp, d), x_sample], axis=0)

    n_c = bp + bs
    c_rows = 16
    c_all = jnp.zeros((c_rows, d), F32).at[:n_c].set(jnp.concatenate([c_prompt, c_sample], 0))
    n_modc = N_MOD * d

    def ada_epi(accs, ex, outs):
        outs[0][...] = accs[0] + ex[0][...]

    mods = []
    for l in range(depth):
        mods.append(_matmul(
            [(c_all, w_ada, _wspec(l, d, tn, 0), True)], m=c_rows, n_out=n_modc, tm=c_rows, tn=tn,
            prologue=lambda a, ex: a * jax.nn.sigmoid(a),
            epilogue=ada_epi,
            extras=[b_ada.reshape(depth, 1, n_modc)],
            extra_specs=[pl.BlockSpec((None, 1, tn), lambda j, i, l=l: (l, 0, j))],
            out_shape=[jax.ShapeDtypeStruct((c_rows, n_modc), F32)],
            out_specs=[_spec2(c_rows, tn)])[0])
    mod = jnp.stack(mods)
    modg = jnp.concatenate(
        [jnp.broadcast_to(mod[:, :bp], (depth, n_p // grp, n_modc)), mod[:, bp:n_c]], axis=1)
    modg = modg.reshape(depth, ng, N_MOD, d).transpose(0, 2, 1, 3)
    modg = modg.reshape(depth * N_MOD, ng, 1, d)

    def mod_idx(l, k):
        return l * N_MOD + k

    def mspec_rows(l, k):
        return pl.BlockSpec((None, gpt, 1, tn), lambda j, i: (mod_idx(l, k), i, 0, j))

    pos = jnp.concatenate([jnp.arange(t_p, dtype=jnp.int32),
                           jnp.tile(past + jnp.arange(t_s, dtype=jnp.int32), bs)])
    rope_c, rope_s = _rope_tables(pos)
    tq = ATTN_TILE
    nq_p = t_p // tq
    tab_p = _causal_tables(nq_p, 0)
    tk_s = 512
    pt = past // tk_s
    tab_sb_s = lambda l: _causal_tables(0, n_p // t_s, n_batch=bs, past_tiles=pt,
                                        past_stride=l * bs * pt)
    tab_mla_s = _causal_tables(0, n_p // t_s, n_batch=bs, past_tiles=pt)

    off_q, off_k, off_v = 0, sb_w, 2 * sb_w
    off_cq = 3 * sb_w
    off_ckv = off_cq + q_lora
    off_kr = off_ckv + kv_lora
    off_g = off_kr + QK_ROPE
    assert off_g == off_kr + QK_ROPE
    w_nk = jnp.swapaxes(w_in, 1, 2)
    w_gates, w_kr_aug = _split_gate_rope_weights(w_nk, off_kr, 2 * d, tn)
    uq = w_uq.reshape(depth, q_lora, H_MLA, QK_NOPE + QK_ROPE)
    uq_r = uq[..., QK_NOPE:]
    w_uq_cat = jnp.concatenate([uq[..., :QK_NOPE], uq_r, _swap_halves(uq_r)], axis=-1)
    w_uq_cat = w_uq_cat.reshape(depth, q_lora, H_MLA * 2 * LANES)
    w_router = jnp.concatenate(
        [w_router_expert, w_router_group,
         jnp.zeros((depth, d, LANES - N_EXPERTS - N_GROUPS), F32)], axis=-1)
    r_hi = w_router.astype(BF16)
    r_res = w_router - r_hi.astype(F32)
    r_mid = r_res.astype(BF16)
    r_lo = (r_res - r_mid.astype(F32)).astype(BF16)
    w_router3 = jnp.stack([r_hi, r_mid, r_lo], axis=1)
    b_router = jnp.concatenate(
        [b_router_expert, b_router_group,
         jnp.zeros((depth, LANES - N_EXPERTS - N_GROUPS), F32)], axis=-1).reshape(depth, 1, LANES)

    rows_past = bs * past
    kr_past = jnp.pad(cache_mla_krope.reshape(depth * rows_past, QK_ROPE),
                      ((0, 0), (0, LANES - QK_ROPE))).astype(BF16)
    w_ukv_b = w_ukv.astype(BF16)
    hm = lambda rows: jax.ShapeDtypeStruct((H_SB, rows, LANES), BF16)
    new_k, new_v, new_c, new_r = [], [], [], []

    for l in range(depth):
        h = _norm_mod(x3, g_norm_mix.reshape(depth, 1, d), modg, l, mod_idx(l, 1), mod_idx(l, 0))
        h = h.reshape(n, d)

        sbq_scale = DH_SB ** -0.5 * float(np.log2(np.e))

        def plain_hm(accs, ex, outs):
            _store_heads(outs[0], accs[0] * sbq_scale)

        def f32_and_hm(accs, ex, outs):
            outs[0][...] = accs[0]
            _store_heads(outs[1], accs[0])

        sb_q = _matmul([(h, w_nk, _wspec_nk(l, d, tn, off_q), True, True)], m=n, n_out=sb_w, tm=tm, tn=tn,
                       epilogue=plain_hm, out_shape=[hm(n)], out_specs=[_hm_spec(tm, tn)])[0]
        k_f32, sb_k = _matmul([(h, w_nk, _wspec_nk(l, d, tn, off_k), True, True)], m=n, n_out=sb_w, tm=tm,
                              tn=tn, epilogue=f32_and_hm,
                              out_shape=[jax.ShapeDtypeStruct((n, sb_w), F32), hm(n)],
                              out_specs=[_spec2(tm, tn), _hm_spec(tm, tn)])
        v_f32, sb_v = _matmul([(h, w_nk, _wspec_nk(l, d, tn, off_v), True, True)], m=n, n_out=sb_w, tm=tm,
                              tn=tn, epilogue=f32_and_hm,
                              out_shape=[jax.ShapeDtypeStruct((n, sb_w), F32), hm(n)],
                              out_specs=[_spec2(tm, tn), _hm_spec(tm, tn)])

        def plain_f32(accs, ex, outs):
            outs[0][...] = accs[0]

        c_q = _matmul([(h, w_nk, _wspec_nk(l, d, tn, off_cq), True, True)], m=n, n_out=q_lora, tm=tm, tn=tn,
                      epilogue=plain_f32, out_shape=[jax.ShapeDtypeStruct((n, q_lora), F32)],
                      out_specs=[_spec2(tm, tn)])[0]

        def ckv_epi(accs, ex, outs):
            y = _rms(accs[0], ex[0][...])
            outs[0][...] = y
            outs[1][...] = y.astype(BF16)

        c_kv, c_kv_b = _matmul(
            [(h, w_nk, _wspec_nk(l, d, kv_lora, off_ckv), True, True)], m=n, n_out=kv_lora, tm=tm, tn=kv_lora,
            epilogue=ckv_epi, extras=[g_kv_lat.reshape(depth, 1, kv_lora)],
            extra_specs=[pl.BlockSpec((None, 1, kv_lora), lambda j, i: (l, 0, 0))],
            out_shape=[jax.ShapeDtypeStruct((n, kv_lora), F32),
                       jax.ShapeDtypeStruct((n, kv_lora), BF16)],
            out_specs=[_spec2(tm, kv_lora), _spec2(tm, kv_lora)])

        def kr_epi(accs, ex, outs):
            r = _rope_lanes(accs[0], ex[0][...], ex[1][...])
            outs[0][...] = r
            outs[1][...] = r.astype(BF16)

        rope_specs = [pl.BlockSpec((tm, LANES), lambda j, i: (i, 0))] * 2
        k_r, k_r_b = _matmul(
            [(h, w_kr_aug, pl.BlockSpec((None, LANES, d), lambda j, i: (l, 0, 0)), False, True)],
            m=n, n_out=LANES, tm=tm, tn=LANES, epilogue=kr_epi,
            extras=[rope_c, rope_s], extra_specs=rope_specs,
            out_shape=[jax.ShapeDtypeStruct((n, LANES), F32), jax.ShapeDtypeStruct((n, LANES), BF16)],
            out_specs=[_spec2(tm, LANES), _spec2(tm, LANES)])

        def gate_epi(accs, ex, outs):
            outs[0][...] = jax.nn.sigmoid(accs[0])

        gates = _matmul(
            [(h, w_gates, pl.BlockSpec((None, tn, d), lambda j, i: (l, j, 0)), False, True)],
            m=n, n_out=2 * d, tm=tm, tn=tn, epilogue=gate_epi,
            out_shape=[jax.ShapeDtypeStruct((n, 2 * d), F32)], out_specs=[_spec2(tm, tn)])[0]

        def cq_prologue(a, ex):
            return _rms(a, ex[0][...])

        gq_spec = pl.BlockSpec((None, 1, q_lora), lambda j, i: (l, 0, 0))
        gq = g_q_lat.reshape(depth, 1, q_lora)
        wqk = 2 * LANES
        hm_qk = lambda rows: jax.ShapeDtypeStruct((H_MLA, rows, wqk), BF16)

        q_scale = (QK_NOPE + QK_ROPE) ** -0.5 * float(np.log2(np.e))

        def qcat_epi(accs, ex, outs):
            c, s = ex[1][...], ex[2][...]
            for hh in range(tn // wqk):
                blk = accs[0][:, hh * wqk:(hh + 1) * wqk]
                outs[0][hh, :, :LANES] = (blk[:, :LANES] * q_scale).astype(BF16)
                outs[0][hh, :, LANES:] = (_rope_lanes(blk[:, LANES:], c, s) * q_scale).astype(BF16)

        q_cat = _matmul(
            [(c_q, w_uq_cat, pl.BlockSpec((None, q_lora, tn), lambda j, i: (l, 0, j)), True)],
            m=n, n_out=H_MLA * wqk, tm=tm, tn=tn, prologue=cq_prologue, epilogue=qcat_epi,
            extras=[gq, rope_c, rope_s], extra_specs=[gq_spec] + rope_specs,
            out_shape=[hm_qk(n)],
            out_specs=[pl.BlockSpec((tn // wqk, tm, wqk), lambda j, i: (j, i, 0))])[0]

        tn_kv = min(2048, H_MLA * wqk)
        hpt = tn_kv // wqk

        def kv_epi(accs, ex, outs):
            kr = ex[0][...]
            ones = jnp.ones(kr.shape, BF16)
            for hh in range(hpt):
                outs[0][hh, :, :LANES] = accs[0][:, hh * wqk:hh * wqk + LANES].astype(BF16)
                outs[0][hh, :, LANES:] = kr
                outs[1][hh, :, :LANES] = accs[0][:, hh * wqk + LANES:(hh + 1) * wqk].astype(BF16)
                outs[1][hh, :, LANES:] = ones

        def up_kv(a, kr, rows, tmr):
            hspec = pl.BlockSpec((hpt, tmr, wqk), lambda j, i: (j, i, 0))
            return _matmul(
                [(a, w_ukv, pl.BlockSpec((None, kv_lora, tn_kv), lambda j, i: (l, 0, j)), True)],
                m=rows, n_out=H_MLA * wqk, tm=tmr, tn=tn_kv, epilogue=kv_epi, extras=[kr],
                extra_specs=[pl.BlockSpec((tmr, LANES), lambda j, i: (i, 0))],
                out_shape=[hm_qk(rows), hm_qk(rows)], out_specs=[hspec, hspec])

        kc_new, v_new = up_kv(c_kv_b, k_r_b, n, tm)

        o_sb_p = _sb_stream_attention(sb_q, sb_k, sb_v, sb_k, sb_v, tq=tq, tk=tq, q_block0=0,
                                      n_tiles=n_p // tq)
        cache_k2 = cache_sb_k.reshape(depth * bs * past * H_SB, DH_SB)
        cache_v2 = cache_sb_v.reshape(depth * bs * past * H_SB, DH_SB)
        q_off_s = n_p // t_s
        o_sb_s = _sb_stream_attention(sb_q, sb_k, sb_v, cache_k2, cache_v2, tq=t_s, tk=ATTN_TILE,
                                      q_block0=q_off_s, n_tiles=bs,
                                      cache_tiles=past // ATTN_TILE,
                                      cache_base=l * bs * (past // ATTN_TILE))
        o_sb = jnp.concatenate([o_sb_p, o_sb_s], axis=0)

        tp3 = (tab_p[0], tab_p[1], tab_p[3])
        o_mla_p = _mla_attention(q_cat, kc_new, v_new, kc_new, v_new, tp3,
                                 tq=tq, tk=tq, pos0=0, out_rows=n_p,
                                 out_block_of=lambda s, qb, kb, fl: (qb[s], 0))
        tab_s = tab_sb_s(l)
        o_mla_s = _mla_latent_attention(
            q_cat, c_kv_b, k_r_b, cache_mla_ckv.reshape(depth * rows_past, kv_lora), kr_past,
            w_ukv_b, (tab_s[0], tab_s[1], tab_s[3]), l, tq=t_s, tk=tk_s, pos0=past, out_rows=n_s,
            out_block_of=lambda s, qb, kb, fl: (qb[s] - q_off_s, 0))
        o_mla = jnp.concatenate([o_mla_p, o_mla_s], axis=0)

        def merge_epi(accs, ex, outs):
            outs[0][...] = (ex[0][...] * accs[0] + ex[1][...] * accs[1]).astype(BF16)

        merged = _matmul(
            [(o_sb, w_branch_sb, _wspec(l, sb_w, tn, 0), True),
             (o_mla, w_branch_mla, _wspec(l, H_MLA * V_DIM, tn, 0), True)],
            m=n, n_out=d, tm=tm, tn=tn, epilogue=merge_epi,
            extras=[gates, gates],
            extra_specs=[pl.BlockSpec((tm, tn), lambda j, i: (i, j)),
                         pl.BlockSpec((tm, tn), lambda j, i: (i, j + d // tn))],
            out_shape=[jax.ShapeDtypeStruct((n, d), BF16)], out_specs=[_spec2(tm, tn)])[0]

        def resid_epi(accs, ex, outs):
            for g in range(gpt):
                outs[0][g] = ex[0][g] + ex[1][g] * accs[0][g * grp:(g + 1) * grp, :]

        x_spec = pl.BlockSpec((gpt, grp, tn), lambda j, i: (i, 0, j))
        x3 = _matmul(
            [(merged, w_out, _wspec(l, d, tn, 0), True)], m=n, n_out=d, tm=tm, tn=tn,
            epilogue=resid_epi, extras=[x3, modg], extra_specs=[x_spec, mspec_rows(l, 2)],
            out_shape=[jax.ShapeDtypeStruct((ng, grp, d), F32)], out_specs=[x_spec])[0]

        h2, route = _norm_route(x3, g_norm_ffn.reshape(depth, 1, d), modg, l, mod_idx(l, 4),
                                mod_idx(l, 3), w_router3, b_router)
        row_tok, row_dst, row_w, tile_expert, n_valid = _dispatch(route.reshape(n, LANES), n)
        yg = _moe_experts(h2, row_tok, row_dst, row_w, tile_expert, n_valid,
                          w_exp_gate, w_exp_up, w_exp_down, l, 2 * n)
        x3 = _combine(x3, yg, modg, l, mod_idx(l, 5))

        new_k.append(k_f32)
        new_v.append(v_f32)
        new_c.append(c_kv)
        new_r.append(k_r[:, :QK_ROPE])

    y_p, y_s = _final_norm(x3, g_final.reshape(1, d), n_p // grp)

    def split(parts, tail):
        a = jnp.stack(parts)
        return (a[:, :n_p].reshape((depth, bp, t_p) + tail),
                a[:, n_p:].reshape((depth, bs, t_s) + tail))

    pk, sk = split(new_k, (H_SB, DH_SB))
    pv, sv = split(new_v, (H_SB, DH_SB))
    pc, sc = split(new_c, (kv_lora,))
    pr, sr = split(new_r, (QK_ROPE,))
    return (y_p.reshape(bp, t_p, d), y_s.reshape(bs, t_s, d), pk, pv, pc, pr, sk, sv, sc, sr)
```

```python
import functools

import numpy as np
import jax
import jax.numpy as jnp
from jax import lax
from jax.experimental import pallas as pl
from jax.experimental.pallas import tpu as pltpu

F32 = jnp.float32
BF16 = jnp.bfloat16

CHUNK = 64
H_SB = 16
DH_SB = 128
H_MLA = 16
QK_NOPE = 128
QK_ROPE = 64
V_DIM = 128
ROPE_THETA = 10000.0
N_GROUPS = 4
EXPERTS_PER_GROUP = 8
N_EXPERTS = N_GROUPS * EXPERTS_PER_GROUP
N_MOD = 6
EPS = 1e-6

LANES = 128
ATTN_TILE = 256
MOE_TILE = 256
HEAD_UNROLL = 4
MLA_UNROLL = 8
MLA_Q_TILE = 512
SLAB_PAD = 4
SB_DEAD = 152.0
VMEM_LIMIT = 56 * 1024 * 1024


def _cparams(n_axes, vmem=VMEM_LIMIT):
    return pltpu.CompilerParams(dimension_semantics=("arbitrary",) * n_axes,
                                vmem_limit_bytes=vmem)


def _row_tile(n, cap=512):
    t = cap
    while n % t:
        t //= 2
    return t


def _token_tile(n, grp, cap=1152):
    return max(t for t in range(grp, cap + 1, grp) if n % t == 0)


def _matmul(pairs, *, m, n_out, tm, tn, epilogue, out_shape, out_specs,
            extras=(), extra_specs=(), prologue=None):
    n_pairs = len(pairs)
    n_ex = len(extras)
    n_outs = len(out_shape)
    pairs = [tuple(p) + (False,) * (5 - len(p)) for p in pairs]
    cast = [p[3] for p in pairs]
    b_nk = [p[4] for p in pairs]

    def kern(*refs):
        a_refs = refs[0:2 * n_pairs:2]
        b_refs = refs[1:2 * n_pairs:2]
        ex = refs[2 * n_pairs:2 * n_pairs + n_ex]
        outs = refs[2 * n_pairs + n_ex:2 * n_pairs + n_ex + n_outs]
        scr = refs[2 * n_pairs + n_ex + n_outs:]
        i = pl.program_id(1)
        accs = []
        si = 0
        for p in range(n_pairs):
            if cast[p]:
                bsc = scr[si]
                si += 1

                @pl.when(i == 0)
                def _(bsc=bsc, b_ref=b_refs[p]):
                    bsc[...] = b_ref[...].astype(BF16)

                bv = bsc[...]
            else:
                bv = b_refs[p][...]
            a = a_refs[p][...]
            if prologue is not None:
                a = prologue(a, ex)
            if b_nk[p]:
                accs.append(_dot_nt(a.astype(BF16), bv))
            else:
                accs.append(jnp.dot(a.astype(BF16), bv, preferred_element_type=F32))
        epilogue(accs, ex, outs)

    in_specs, args, scratch = [], [], []
    for (a, b, b_spec, cb, nk) in pairs:
        k = a.shape[1]
        in_specs += [pl.BlockSpec((tm, k), lambda j, i: (i, 0)), b_spec]
        args += [a, b]
        if cb:
            scratch.append(pltpu.VMEM((tn, k) if nk else (k, tn), BF16))
    in_specs += list(extra_specs)
    args += list(extras)
    return pl.pallas_call(
        kern,
        grid=(n_out // tn, m // tm),
        in_specs=in_specs,
        out_specs=out_specs,
        out_shape=out_shape,
        scratch_shapes=scratch,
        compiler_params=_cparams(2),
    )(*args)


def _wspec(l, k, tn, col_off):
    cb = col_off // tn
    assert cb * tn == col_off
    return pl.BlockSpec((None, k, tn), lambda j, i: (l, 0, cb + j))


def _wspec_nk(l, k, tn, row_off):
    rb = row_off // tn
    assert rb * tn == row_off
    return pl.BlockSpec((None, tn, k), lambda j, i: (l, rb + j, 0))


def _spec2(tm, tn):
    return pl.BlockSpec((tm, tn), lambda j, i: (i, j))


def _hm_spec(tm, tn):
    return pl.BlockSpec((tn // LANES, tm, LANES), lambda j, i: (j, i, 0))


def _store_heads(o_ref, val):
    for c in range(val.shape[1] // LANES):
        o_ref[c] = val[:, c * LANES:(c + 1) * LANES].astype(o_ref.dtype)


def _rms(x, g):
    return x * lax.rsqrt(jnp.mean(x * x, axis=-1, keepdims=True) + EPS) * g


def _norm_mod_kernel(x_ref, g_ref, sc_ref, sh_ref, o_ref):
    x = x_ref[...]
    y = _rms(x, g_ref[...])
    o_ref[...] = (y * (1.0 + sc_ref[...]) + sh_ref[...]).astype(o_ref.dtype)


def _split3(x):
    hi = x.astype(BF16)
    r = x - hi.astype(F32)
    mid = r.astype(BF16)
    lo = (r - mid.astype(F32)).astype(BF16)
    return hi, mid, lo


def _dot_f32(a, b3):
    a_hi, a_mid, a_lo = _split3(a)
    b_hi, b_mid, b_lo = b3
    d = functools.partial(jnp.dot, preferred_element_type=F32)
    small = d(a_hi, b_lo) + d(a_lo, b_hi) + d(a_mid, b_mid)
    return (d(a_hi, b_hi) + (d(a_hi, b_mid) + d(a_mid, b_hi))) + small


def _route(logits):
    lane = lax.broadcasted_iota(jnp.int32, logits.shape, 1)
    lanef = lane.astype(F32)
    big = jnp.float32(1e9)
    ninf = jnp.float32(-jnp.inf)
    is_g = (lane >= N_EXPERTS) & (lane < N_EXPERTS + N_GROUPS)
    gl = jnp.where(is_g, logits, ninf)
    gmax = jnp.max(gl, axis=1, keepdims=True)
    g_idx = jnp.min(jnp.where(gl == gmax, lanef - N_EXPERTS, big), axis=1, keepdims=True)
    p_group = 1.0 / jnp.sum(jnp.where(is_g, jnp.exp(gl - gmax), 0.0), axis=1, keepdims=True)
    grp = jnp.floor(lanef * (1.0 / EXPERTS_PER_GROUP))
    in_g = (lane < N_EXPERTS) & (grp == g_idx)
    el = jnp.where(in_g, logits, ninf)
    e1 = jnp.max(el, axis=1, keepdims=True)
    i1 = jnp.min(jnp.where(el == e1, lanef, big), axis=1, keepdims=True)
    el2 = jnp.where(lanef == i1, ninf, el)
    e2 = jnp.max(el2, axis=1, keepdims=True)
    i2 = jnp.min(jnp.where(el2 == e2, lanef, big), axis=1, keepdims=True)
    t = jnp.exp(e2 - e1)
    den = 1.0 + t
    w1 = (1.0 / den) * p_group
    w2 = (t / den) * p_group
    out = jnp.where(lane == 0, i1, jnp.where(lane == 1, i2,
          jnp.where(lane == 2, w1, jnp.where(lane == 3, w2, 0.0))))
    return out


def _slab_pitch(s):
    return s + SLAB_PAD


def _slab_store(ref, row0, val):
    rows, d = val.shape
    s = d // LANES
    p = _slab_pitch(s)
    for c in range(s):
        ref[pl.ds(row0 * p + c, rows, stride=p), :] = val[:, c * LANES:(c + 1) * LANES]


def _slab_load(ref, row0, rows, s, lead=None):
    pieces = []
    p = _slab_pitch(s)
    for c in range(s):
        rs = pl.ds(row0 * p + c, rows, stride=p)
        pieces.append(ref[rs, :] if lead is None else ref[lead, rs, :])
    return jnp.concatenate(pieces, axis=1)


def _norm_route_kernel(x_ref, g_ref, sc_ref, sh_ref, wr_ref, br_ref, h_ref, r_ref):
    x = x_ref[...]
    y = _rms(x, g_ref[...])
    h = y * (1.0 + sc_ref[...]) + sh_ref[...]
    b3 = (wr_ref[0], wr_ref[1], wr_ref[2])
    grp = x.shape[1]
    for g in range(x.shape[0]):
        _slab_store(h_ref, g * grp, h[g])
        logits = _dot_f32(h[g], b3) + br_ref[...]
        r_ref[g] = _route(logits)


def _mod_spec(idx, gb, d):
    return pl.BlockSpec((None, gb, 1, d), lambda i: (idx, i, 0, 0))


def _norm_mod(x3, g, modg, l, sc_i, sh_i):
    ng, grp, d = x3.shape
    gb = _row_tile(ng, 4)
    return pl.pallas_call(
        _norm_mod_kernel,
        grid=(ng // gb,),
        in_specs=[pl.BlockSpec((gb, grp, d), lambda i: (i, 0, 0)),
                  pl.BlockSpec((None, 1, d), lambda i: (l, 0, 0)),
                  _mod_spec(sc_i, gb, d), _mod_spec(sh_i, gb, d)],
        out_specs=pl.BlockSpec((gb, grp, d), lambda i: (i, 0, 0)),
        out_shape=jax.ShapeDtypeStruct(x3.shape, BF16),
        compiler_params=_cparams(1),
    )(x3, g, modg, modg)


def _norm_route(x3, g, modg, l, sc_i, sh_i, wr3, br):
    ng, grp, d = x3.shape
    gb = _row_tile(ng, 4)
    return pl.pallas_call(
        _norm_route_kernel,
        grid=(ng // gb,),
        in_specs=[pl.BlockSpec((gb, grp, d), lambda i: (i, 0, 0)),
                  pl.BlockSpec((None, 1, d), lambda i: (l, 0, 0)),
                  _mod_spec(sc_i, gb, d), _mod_spec(sh_i, gb, d),
                  pl.BlockSpec((None, 3, d, LANES), lambda i: (l, 0, 0, 0)),
                  pl.BlockSpec((None, 1, LANES), lambda i: (l, 0, 0))],
        out_specs=[pl.BlockSpec((gb * grp * _slab_pitch(d // LANES), LANES), lambda i: (i, 0)),
                   pl.BlockSpec((gb, grp, LANES), lambda i: (i, 0, 0))],
        out_shape=[jax.ShapeDtypeStruct((ng * grp * _slab_pitch(d // LANES), LANES), F32),
                   jax.ShapeDtypeStruct((ng, grp, LANES), F32)],
        compiler_params=_cparams(1),
    )(x3, g, modg, modg, wr3, br)


def _final_norm_kernel(x_ref, g_ref, op_ref, os_ref, *, n_prompt_blocks):
    i = pl.program_id(0)
    y = _rms(x_ref[...], g_ref[...])

    @pl.when(i < n_prompt_blocks)
    def _():
        op_ref[...] = y

    @pl.when(i >= n_prompt_blocks)
    def _():
        os_ref[...] = y


def _final_norm(x3, g, ng_prompt):
    ng, grp, d = x3.shape
    gb = _row_tile(np.gcd(ng_prompt, ng - ng_prompt), 4)
    npb = ng_prompt // gb
    blk = lambda f: pl.BlockSpec((gb, grp, d), f)
    return pl.pallas_call(
        functools.partial(_final_norm_kernel, n_prompt_blocks=npb),
        grid=(ng // gb,),
        in_specs=[blk(lambda i: (i, 0, 0)), pl.BlockSpec((1, d), lambda i: (0, 0))],
        out_specs=[blk(lambda i: (jnp.minimum(i, npb - 1), 0, 0)),
                   blk(lambda i: (jnp.maximum(i - npb, 0), 0, 0))],
        out_shape=[jax.ShapeDtypeStruct((ng_prompt, grp, d), F32),
                   jax.ShapeDtypeStruct((ng - ng_prompt, grp, d), F32)],
        compiler_params=_cparams(1),
    )(x3, g)


def _lanes(c, w):
    if w % LANES == 0:
        return c if w == LANES else jnp.tile(c, (1, w // LANES))
    return c[:, :w]


def _dot_nt(a, b):
    return lax.dot_general(a, b, (((1,), (1,)), ((), ())), preferred_element_type=F32)


def _sb_block(qh, kh, vh, c, u, masked):
    w = kh.shape[0]
    z = _dot_nt(qh, kh)
    sp = jnp.maximum(z, 0.0) + jnp.log2(1.0 + jnp.exp2(-jnp.abs(z)))
    if masked:
        row = lax.broadcasted_iota(jnp.int32, z.shape, 0)
        col = lax.broadcasted_iota(jnp.int32, z.shape, 1)
        valid = col < row
        sp = jnp.where(valid, sp, 0.0)
    hi = sp.astype(BF16)
    lo = (sp - hi.astype(F32)).astype(BF16)
    cs = jnp.dot(hi, u, preferred_element_type=F32) + jnp.dot(lo, u, preferred_element_type=F32)
    wgt = jnp.exp2(z - sp - cs - _lanes(c, w))
    if masked:
        wgt = jnp.where(valid, wgt, 0.0)
    o = jnp.dot(wgt.astype(BF16), vh, preferred_element_type=F32)
    c_new = c + jnp.sum(sp, axis=1, keepdims=True)
    return o, c_new


def _suffix_matrix(w):
    j = np.arange(w)[:, None]
    s = np.arange(w)[None, :]
    return jnp.asarray((j > s).astype(np.float32), dtype=BF16)


def _sb_stream_kernel(q_ref, kn_ref, vn_ref, k_hbm, v_hbm, ud_ref, up_ref, o_ref,
                      kbuf, vbuf, acc, carry, done, alive_ref, sem, *, heads, tk, cache_tiles,
                      cache_base):
    i = pl.program_id(0)
    n_past = i if cache_tiles is None else cache_tiles

    def fetch(j, slot):
        if cache_tiles is None:
            rows = pl.ds(pl.multiple_of(j * tk, tk), tk)
            src_k, src_v = k_hbm.at[:, rows, :], v_hbm.at[:, rows, :]
        else:
            blk = tk * heads
            rows = pl.ds(pl.multiple_of((cache_base + i * cache_tiles + j) * blk, blk), blk)
            src_k, src_v = k_hbm.at[rows], v_hbm.at[rows]
        return (pltpu.make_async_copy(src_k, kbuf.at[slot], sem.at[0, slot]),
                pltpu.make_async_copy(src_v, vbuf.at[slot], sem.at[1, slot]))

    def tile_of(buf, slot, h):
        if cache_tiles is None:
            return buf[slot, h]
        return buf.at[slot][pl.ds(h, tk, stride=heads), :].astype(BF16)

    @pl.when(n_past > 0)
    def _():
        for c in fetch(n_past - 1, 0):
            c.start()

    def diag(h, _):
        o, c = _sb_block(q_ref[h], kn_ref[h], vn_ref[h], jnp.zeros(carry.shape[1:], F32),
                         ud_ref[...], True)
        acc[h] = o
        carry[h] = c
        done[h] = 0
        return 0
    lax.fori_loop(0, heads, diag, 0, unroll=min(heads, HEAD_UNROLL))

    alive_ref[0] = heads

    @pl.when(n_past > 0)
    def _():
        for c in fetch(n_past - 1, 0):
            c.wait()

        @pl.when(n_past > 1)
        def _():
            for c in fetch(n_past - 2, 1):
                c.start()

        def first(h, alive):
            o, c = _sb_block(q_ref[h], tile_of(kbuf, 0, h), tile_of(vbuf, 0, h), carry[h],
                             up_ref[...], False)
            acc[h] = acc[h] + o
            carry[h] = c
            dead = (jnp.min(c) >= SB_DEAD).astype(jnp.int32)
            done[h] = dead
            return alive + 1 - dead
        alive_ref[0] = lax.fori_loop(0, heads, first, 0, unroll=min(heads, HEAD_UNROLL))

    def cond(state):
        j, alive = state
        return (j >= 0) & (alive > 0)

    def body(state):
        j, _ = state
        slot = lax.rem(n_past - 1 - j, 2)
        for c in fetch(j, slot):
            c.wait()

        @pl.when(j > 0)
        def _():
            for c in fetch(j - 1, 1 - slot):
                c.start()

        def head(h, alive):
            @pl.when(done[h] == 0)
            def _():
                o, c = _sb_block(q_ref[h], tile_of(kbuf, slot, h), tile_of(vbuf, slot, h),
                                 carry[h], up_ref[...], False)
                acc[h] = acc[h] + o
                carry[h] = c
                done[h] = (jnp.min(c) >= SB_DEAD).astype(jnp.int32)
            return alive + 1 - done[h]
        return j - 1, lax.fori_loop(0, heads, head, 0)

    j_end, _ = lax.while_loop(cond, body, (jnp.int32(n_past - 2), alive_ref[0]))

    @pl.when(j_end >= 0)
    def _():
        for c in fetch(j_end, lax.rem(n_past - 1 - j_end, 2)):
            c.wait()

    for h in range(heads):
        o_ref[:, h * DH_SB:(h + 1) * DH_SB] = acc[h].astype(o_ref.dtype)


def _sb_stream_attention(q, kn, vn, k_src, v_src, *, tq, tk, q_block0, n_tiles, cache_tiles=None,
                         cache_base=0):
    heads = q.shape[0]
    blk = pl.BlockSpec((heads, tq, DH_SB), lambda i: (0, q_block0 + i, 0))
    anyspec = pl.BlockSpec(memory_space=pl.ANY)
    if cache_tiles is None:
        buf = pltpu.VMEM((2, heads, tk, DH_SB), BF16)
    else:
        buf = pltpu.VMEM((2, tk * heads, DH_SB), F32)
    kern = functools.partial(_sb_stream_kernel, heads=heads, tk=tk, cache_tiles=cache_tiles,
                             cache_base=cache_base)
    return pl.pallas_call(
        kern, grid=(n_tiles,),
        in_specs=[blk, blk, blk, anyspec, anyspec,
                  pl.BlockSpec((tq, tq), lambda i: (0, 0)), pl.BlockSpec((tk, tk), lambda i: (0, 0))],
        out_specs=pl.BlockSpec((tq, heads * DH_SB), lambda i: (i, 0)),
        out_shape=jax.ShapeDtypeStruct((n_tiles * tq, heads * DH_SB), BF16),
        scratch_shapes=[buf, buf,
                        pltpu.VMEM((heads, tq, DH_SB), F32),
                        pltpu.VMEM((heads, tq, LANES), F32),
                        pltpu.SMEM((heads,), jnp.int32),
                        pltpu.SMEM((1,), jnp.int32),
                        pltpu.SemaphoreType.DMA((2, 2))],
        compiler_params=_cparams(1),
    )(q, kn, vn, k_src, v_src, _suffix_matrix(tq), _suffix_matrix(tk))


def _mla_block(qc, kc, va, m, acc, mask):
    s = _dot_nt(qc, kc)
    if mask is not None:
        s = jnp.where(mask, s, -jnp.inf)
    m_new = jnp.maximum(m, jnp.max(s, axis=1, keepdims=True))
    alpha = jnp.exp2(m - m_new)
    p = jnp.exp2(s - _lanes(m_new, s.shape[1]))
    acc_new = _lanes(alpha, acc.shape[1]) * acc + jnp.dot(p.astype(BF16), va,
                                                          preferred_element_type=F32)
    return m_new, acc_new


def _mla_kernel(qb_ref, kb_ref, fl_ref, q_ref, k_ref, v_ref, o_ref, acc, m_sc, *, heads):
    s = pl.program_id(0)
    fl = fl_ref[s]
    tq, tk = q_ref.shape[1], k_ref.shape[1]

    @pl.when((fl & 1) != 0)
    def _():
        m_sc[...] = jnp.full(m_sc.shape, -jnp.inf, F32)
        acc[...] = jnp.zeros(acc.shape, F32)

    def run(mask):
        def body(h, _):
            m, a = _mla_block(q_ref[h], k_ref[h], v_ref[h], m_sc[h], acc[h], mask)
            m_sc[h] = m
            acc[h] = a
            return 0
        lax.fori_loop(0, heads, body, 0, unroll=min(heads, MLA_UNROLL))

    @pl.when((fl & 4) != 0)
    def _():
        row = lax.broadcasted_iota(jnp.int32, (tq, tk), 0) + qb_ref[s] * tq
        col = lax.broadcasted_iota(jnp.int32, (tq, tk), 1) + kb_ref[s] * tk
        run((col // CHUNK) <= (row // CHUNK))

    @pl.when((fl & 4) == 0)
    def _():
        run(None)

    @pl.when((fl & 2) != 0)
    def _():
        for h in range(heads):
            a = acc[h]
            o_ref[:, h * V_DIM:(h + 1) * V_DIM] = (a[:, :V_DIM] / a[:, V_DIM:]).astype(o_ref.dtype)


def _mla_tables(n_rows, tq, tk):
    r = tq // tk
    qb, kb, fl = [], [], []
    for i in range(n_rows // tq):
        tiles = [(i * r + j, 4) for j in range(r)] + [(j, 0) for j in reversed(range(i * r))]
        for idx, (j, f) in enumerate(tiles):
            qb.append(i)
            kb.append(j)
            fl.append(f | (1 if idx == 0 else 0) | (2 if idx == len(tiles) - 1 else 0))
    arr = lambda v: jnp.asarray(np.asarray(v, dtype=np.int32))
    return arr(qb), arr(kb), arr(fl)


def _mla_attention(qc, kc, va, *, n_rows, tq, tk):
    qb, kb, fl = _mla_tables(n_rows, tq, tk)
    h = H_MLA
    wqk = 2 * LANES
    kv_spec = pl.BlockSpec((h, tk, wqk), lambda s, qb, kb, fl: (0, kb[s], 0))
    grid_spec = pltpu.PrefetchScalarGridSpec(
        num_scalar_prefetch=3,
        grid=(qb.shape[0],),
        in_specs=[pl.BlockSpec((h, tq, wqk), lambda s, qb, kb, fl: (0, qb[s], 0)),
                  kv_spec, kv_spec],
        out_specs=pl.BlockSpec((tq, h * V_DIM), lambda s, qb, kb, fl: (qb[s], 0)),
        scratch_shapes=[pltpu.VMEM((h, tq, 2 * V_DIM), F32),
                        pltpu.VMEM((h, tq, LANES), F32)],
    )
    return pl.pallas_call(
        functools.partial(_mla_kernel, heads=h), grid_spec=grid_spec,
        out_shape=jax.ShapeDtypeStruct((n_rows, h * V_DIM), BF16),
        compiler_params=_cparams(1),
    )(qb, kb, fl, qc, kc, va)


def _mla_latent_kernel(qb_ref, kb_ref, fl_ref, q_ref, cn_ref, rn_ref, cp_ref, rp_ref, w_ref,
                       o_ref, qa, qr, acc, m_sc, l_sc, *, heads, pos0):
    s_id = pl.program_id(0)
    fl = fl_ref[s_id]
    is_first = (fl & 1) != 0
    is_last = (fl & 2) != 0
    tq = q_ref.shape[1]
    hw = QK_NOPE + V_DIM

    def step(ck, kr, mask):
        s = _dot_nt(qa[...], ck) + _dot_nt(qr[...], kr)
        if mask is not None:
            s = jnp.where(mask, s, -jnp.inf)
        m_old = m_sc[...]
        m_new = jnp.maximum(m_old, jnp.max(s, axis=1, keepdims=True))
        alpha = jnp.exp2(m_old - m_new)
        p = jnp.exp2(s - m_new[:, :1])
        l_sc[...] = alpha * l_sc[...] + jnp.sum(p, axis=1, keepdims=True)
        acc[...] = alpha[:, :1] * acc[...] + jnp.dot(p.astype(BF16), ck,
                                                     preferred_element_type=F32)
        m_sc[...] = m_new

    @pl.when(is_first)
    def _():
        for h in range(heads):
            qh = q_ref[h]
            w_uk = w_ref[:, h * hw:h * hw + QK_NOPE]
            qa[h * tq:(h + 1) * tq, :] = _dot_nt(qh[:, :QK_NOPE], w_uk).astype(BF16)
            qr[h * tq:(h + 1) * tq, :] = qh[:, QK_NOPE:]
        m_sc[...] = jnp.full(m_sc.shape, -jnp.inf, F32)
        l_sc[...] = jnp.zeros(l_sc.shape, F32)
        acc[...] = jnp.zeros(acc.shape, F32)
        row = lax.broadcasted_iota(jnp.int32, (heads * tq, tq), 0) % tq + pos0
        col = lax.broadcasted_iota(jnp.int32, (heads * tq, tq), 1) + pos0
        step(cn_ref[...], rn_ref[...], (col // CHUNK) <= (row // CHUNK))

    @pl.when(jnp.logical_not(is_first))
    def _():
        step(cp_ref[...].astype(BF16), rp_ref[...], None)

    @pl.when(is_last)
    def _():
        o_lat = (acc[...] / l_sc[...][:, :1]).astype(BF16)
        for h in range(heads):
            w_uv = w_ref[:, h * hw + QK_NOPE:(h + 1) * hw]
            o_ref[:, h * V_DIM:(h + 1) * V_DIM] = jnp.dot(
                o_lat[h * tq:(h + 1) * tq, :], w_uv, preferred_element_type=F32).astype(o_ref.dtype)


def _mla_latent_attention(qc, c_new, r_new, c_past, r_past, w_ukv_b, tables, l, *, tq, tk, pos0,
                          out_rows, out_block_of):
    qb, kb, fl = tables
    h = H_MLA
    kvl = c_new.shape[1]
    wqk = 2 * LANES
    grid_spec = pltpu.PrefetchScalarGridSpec(
        num_scalar_prefetch=3,
        grid=(qb.shape[0],),
        in_specs=[pl.BlockSpec((h, tq, wqk), lambda s, qb, kb, fl: (0, qb[s], 0)),
                  pl.BlockSpec((tq, kvl), lambda s, qb, kb, fl: (qb[s], 0)),
                  pl.BlockSpec((tq, LANES), lambda s, qb, kb, fl: (qb[s], 0)),
                  pl.BlockSpec((tk, kvl), lambda s, qb, kb, fl: (kb[s], 0)),
                  pl.BlockSpec((tk, LANES), lambda s, qb, kb, fl: (kb[s], 0)),
                  pl.BlockSpec((None, kvl, w_ukv_b.shape[2]), lambda s, qb, kb, fl: (l, 0, 0))],
        out_specs=pl.BlockSpec((tq, h * V_DIM), out_block_of),
        scratch_shapes=[pltpu.VMEM((h * tq, kvl), BF16),
                        pltpu.VMEM((h * tq, LANES), BF16),
                        pltpu.VMEM((h * tq, kvl), F32),
                        pltpu.VMEM((h * tq, LANES), F32),
                        pltpu.VMEM((h * tq, LANES), F32)],
    )
    kern = functools.partial(_mla_latent_kernel, heads=h, pos0=pos0)
    return pl.pallas_call(
        kern, grid_spec=grid_spec,
        out_shape=jax.ShapeDtypeStruct((out_rows, h * V_DIM), BF16),
        compiler_params=_cparams(1),
    )(qb, kb, fl, qc, c_new, r_new, c_past, r_past, w_ukv_b)


def _causal_tables(nq, q_off, per_head=False, heads=1, n_batch=1, past_tiles=None,
                   past_stride=0):
    qb, kb, hb, fl = [], [], [], []
    if past_tiles is None:
        for i in range(nq):
            n = i + 1
            for j in range(n):
                qb.append(q_off + i)
                kb.append(max(i - j, 1) - 1 if j == 0 else i - j)
                hb.append(0)
                fl.append((1 if j == 0 else 0) | (2 if j == n - 1 else 0))
    else:
        for b in range(n_batch):
            for h in range(heads if per_head else 1):
                n = 1 + past_tiles
                for j in range(n):
                    qb.append(q_off + b)
                    jj = past_tiles - 1 if j == 0 else past_tiles - j
                    kb.append(past_stride + b * past_tiles + jj)
                    hb.append(h)
                    fl.append((1 if j == 0 else 0) | (2 if j == n - 1 else 0))
    arr = lambda v: jnp.asarray(np.asarray(v, dtype=np.int32))
    return arr(qb), arr(kb), arr(hb), arr(fl)


ROW_DMA_UNROLL = 8


def _moe_kernel(te_ref, nv_ref, tok_ref, dst_ref, h_hbm, w_ref, wg_ref, wu_ref, wd_ref, y_hbm,
                xbuf, obuf, wg_s, wu_s, wd_s, sem_in, sem_out, *, s):
    t = pl.program_id(0)
    tm = w_ref.shape[0]
    nv = nv_ref[0]
    live = t < nv
    slot = lax.rem(t, 2)

    p = _slab_pitch(s)

    def in_copy(tile, r, sl):
        src0 = pl.multiple_of(tok_ref[tile * tm + r] * p, SLAB_PAD)
        return pltpu.make_async_copy(h_hbm.at[pl.ds(src0, s)],
                                     xbuf.at[sl, pl.ds(pl.multiple_of(r * p, SLAB_PAD), s)],
                                     sem_in.at[sl])

    def gather_start(tile, sl):
        def body(r, _):
            in_copy(tile, r, sl).start()
            return 0
        lax.fori_loop(0, tm, body, 0, unroll=ROW_DMA_UNROLL)

    def gather_wait(tile, sl):
        def body(r, _):
            in_copy(tile, r, sl).wait()
            return 0
        lax.fori_loop(0, tm, body, 0, unroll=ROW_DMA_UNROLL)

    @pl.when(live & (t == 0))
    def _():
        gather_start(0, 0)

    @pl.when(t + 1 < nv)
    def _():
        gather_start(t + 1, 1 - slot)

    prev = te_ref[jnp.maximum(t - 1, 0)]
    fresh = (t == 0) | (te_ref[t] != prev)

    @pl.when(live & fresh)
    def _():
        wg_s[...] = wg_ref[...].astype(BF16)
        wu_s[...] = wu_ref[...].astype(BF16)
        wd_s[...] = wd_ref[...].astype(BF16)

    @pl.when(live)
    def _():
        gather_wait(t, slot)
        x = _slab_load(xbuf, 0, tm, s, lead=slot).astype(BF16)
        a = jnp.dot(x, wg_s[...], preferred_element_type=F32)
        u = jnp.dot(x, wu_s[...], preferred_element_type=F32)
        hid = (a * jax.nn.sigmoid(a)) * u * w_ref[...]
        y = jnp.dot(hid.astype(BF16), wd_s[...], preferred_element_type=F32)

        def out_copy(tile, r):
            dst0 = pl.multiple_of(dst_ref[tile * tm + r] * p, SLAB_PAD)
            return pltpu.make_async_copy(obuf.at[pl.ds(pl.multiple_of(r * p, SLAB_PAD), s)],
                                         y_hbm.at[pl.ds(dst0, s)], sem_out)

        def scatter_wait(tile):
            def wait(r, _):
                out_copy(tile, r).wait()
                return 0
            lax.fori_loop(0, tm, wait, 0, unroll=ROW_DMA_UNROLL)

        @pl.when(t > 0)
        def _():
            scatter_wait(t - 1)

        _slab_store(obuf, 0, y)

        def start(r, _):
            out_copy(t, r).start()
            return 0
        lax.fori_loop(0, tm, start, 0, unroll=ROW_DMA_UNROLL)

        @pl.when(t == nv - 1)
        def _():
            scatter_wait(t)


def _moe_experts(h_slab, row_tok, row_dst, row_w, tile_expert, n_valid, w_gate, w_up, w_down, l,
                 n_dest):
    d, f = w_gate.shape[-2:]
    s = d // LANES
    p = _slab_pitch(s)
    tm = MOE_TILE
    n_tiles = row_tok.shape[0] // tm
    wmap = lambda t, te, nv, tok, dst: (l, te[t], 0, 0)
    grid_spec = pltpu.PrefetchScalarGridSpec(
        num_scalar_prefetch=4, grid=(n_tiles,),
        in_specs=[pl.BlockSpec(memory_space=pl.ANY),
                  pl.BlockSpec((tm, 1), lambda t, te, nv, tok, dst: (t, 0)),
                  pl.BlockSpec((None, None, d, f), wmap),
                  pl.BlockSpec((None, None, d, f), wmap),
                  pl.BlockSpec((None, None, f, d), wmap)],
        out_specs=pl.BlockSpec(memory_space=pl.ANY),
        scratch_shapes=[pltpu.VMEM((2, tm * p, LANES), F32), pltpu.VMEM((tm * p, LANES), F32),
                        pltpu.VMEM((d, f), BF16), pltpu.VMEM((d, f), BF16),
                        pltpu.VMEM((f, d), BF16),
                        pltpu.SemaphoreType.DMA((2,)), pltpu.SemaphoreType.DMA(())],
    )
    return pl.pallas_call(
        functools.partial(_moe_kernel, s=s), grid_spec=grid_spec,
        out_shape=jax.ShapeDtypeStruct(((n_dest + tm) * p, LANES), F32),
        compiler_params=_cparams(1),
    )(tile_expert, n_valid, row_tok, row_dst, h_slab, row_w, w_gate, w_up, w_down)


def _combine_kernel(x_ref, y0_ref, y1_ref, g_ref, o_ref):
    gb, grp, d = x_ref.shape
    s = d // LANES
    for g in range(gb):
        y = _slab_load(y0_ref, g * grp, grp, s) + _slab_load(y1_ref, g * grp, grp, s)
        o_ref[g] = x_ref[g] + g_ref[g] * y


def _combine(x3, yg, modg, l, g_i):
    ng, grp, d = x3.shape
    s = d // LANES
    gb = _row_tile(ng, 2)
    blk = pl.BlockSpec((gb, grp, d), lambda i: (i, 0, 0))
    yblk = lambda off: pl.BlockSpec((gb * grp * _slab_pitch(s), LANES), lambda i: (i + off, 0))
    return pl.pallas_call(
        _combine_kernel,
        grid=(ng // gb,),
        in_specs=[blk, yblk(0), yblk(ng // gb), _mod_spec(g_i, gb, d)],
        out_specs=blk,
        out_shape=jax.ShapeDtypeStruct(x3.shape, F32),
        compiler_params=_cparams(1),
    )(x3, yg, yg, modg)


def _dispatch(route, n):
    tm = MOE_TILE
    r_cap = (2 * n + N_EXPERTS * (tm - 1) + tm - 1) // tm * tm
    e = route[:, :2].astype(jnp.int32)
    w = route[:, 2:4]
    flat_e = e.T.reshape(-1)
    flat_w = w.T.reshape(-1)
    order = jnp.argsort(flat_e, stable=True).astype(jnp.int32)
    bounds = jnp.searchsorted(flat_e[order], jnp.arange(N_EXPERTS + 1, dtype=jnp.int32),
                              side="left").astype(jnp.int32)
    counts = bounds[1:] - bounds[:-1]
    padded = (counts + tm - 1) // tm * tm
    ends_p = jnp.cumsum(padded)
    starts_p = ends_p - padded
    starts = jnp.cumsum(counts) - counts
    tile_start = jnp.arange(r_cap // tm, dtype=jnp.int32) * tm
    tile_expert = jnp.minimum(jnp.searchsorted(ends_p, tile_start, side="right"),
                              N_EXPERTS - 1).astype(jnp.int32)
    n_valid = (ends_p[-1:] // tm).astype(jnp.int32)
    per_row = lambda v: jnp.broadcast_to(v[:, None], (r_cap // tm, tm)).reshape(r_cap)
    local = jnp.arange(r_cap, dtype=jnp.int32) - per_row(starts_p[tile_expert])
    valid = local < per_row(counts[tile_expert])
    slot = order[jnp.clip(per_row(starts[tile_expert]) + local, 0, 2 * n - 1)]
    row_dst = jnp.where(valid, slot, 2 * n + jnp.arange(r_cap, dtype=jnp.int32) % tm)
    row_tok = jnp.where(valid, jnp.where(slot >= n, slot - n, slot), 0)
    row_w = jnp.where(valid, flat_w[slot], 0.0)
    return row_tok, row_dst, row_w.reshape(r_cap, 1), tile_expert, n_valid


def _gate_weights_kernel(a_ref, b_ref, o_ref):
    r = b_ref.shape[0]
    w = a_ref.shape[0]
    o_ref[:w - r, :] = a_ref[r:, :].astype(BF16)
    o_ref[w - r:, :] = b_ref[...].astype(BF16)


def _rope_weights_kernel(a_ref, o_ref):
    a = a_ref[...].astype(BF16)
    q = a.shape[0] // 2
    o_ref[...] = jnp.concatenate([a, a[q:], a[:q]], axis=0)


def _split_gate_rope_weights(w_nk, off_kr, n_gate, tn):
    depth, n_in, k = w_nk.shape
    assert off_kr % tn == 0 and n_gate % tn == 0 and QK_ROPE * 2 == LANES
    rb = off_kr // tn
    gates = pl.pallas_call(
        _gate_weights_kernel,
        grid=(depth, n_gate // tn),
        in_specs=[pl.BlockSpec((None, tn, k), lambda l, j: (l, rb + j, 0)),
                  pl.BlockSpec((None, QK_ROPE, k),
                               lambda l, j: (l, (rb + j + 1) * (tn // QK_ROPE), 0))],
        out_specs=pl.BlockSpec((None, tn, k), lambda l, j: (l, j, 0)),
        out_shape=jax.ShapeDtypeStruct((depth, n_gate, k), BF16),
        compiler_params=_cparams(2),
    )(w_nk, w_nk)
    rope = pl.pallas_call(
        _rope_weights_kernel,
        grid=(depth,),
        in_specs=[pl.BlockSpec((None, QK_ROPE, k), lambda l: (l, off_kr // QK_ROPE, 0))],
        out_specs=pl.BlockSpec((None, 2 * QK_ROPE, k), lambda l: (l, 0, 0)),
        out_shape=jax.ShapeDtypeStruct((depth, 2 * QK_ROPE, k), BF16),
        compiler_params=_cparams(1),
    )(w_nk)
    return gates, rope


def _rope_tables(pos):
    inv = ROPE_THETA ** (-jnp.arange(0, QK_ROPE, 2, dtype=F32) / QK_ROPE)
    ang = pos.astype(F32)[:, None] * inv[None, :]
    cos, sin = jnp.cos(ang), jnp.sin(ang)
    pad = jnp.zeros((pos.shape[0], LANES - QK_ROPE), F32)
    return (jnp.concatenate([cos, cos, pad], axis=1),
            jnp.concatenate([-sin, sin, pad], axis=1))


def _rope_lanes(acc, c, s):
    return acc * c + pltpu.roll(acc, LANES - QK_ROPE, axis=1) * s


def _swap_halves(w):
    half = w.shape[-1] // 2
    return jnp.concatenate([w[..., half:], w[..., :half]], axis=-1)


def kernel(x_prompt, x_sample, c_prompt, c_sample, cache_sb_k, cache_sb_v, cache_mla_ckv,
           cache_mla_krope, w_ada, b_ada, g_norm_mix, g_norm_ffn, w_in, g_q_lat, g_kv_lat,
           w_uq, w_ukv, w_branch_sb, w_branch_mla, w_out, w_router_group, b_router_group,
           w_router_expert, b_router_expert, w_exp_gate, w_exp_up, w_exp_down, g_final):
    bp, t_p, d = x_prompt.shape
    bs, t_s, _ = x_sample.shape
    depth = w_in.shape[0]
    past = cache_sb_k.shape[2]
    grp = t_s
    n_p, n_s = bp * t_p, bs * t_s
    n = n_p + n_s
    ng = n // grp
    sb_w = H_SB * DH_SB
    q_lora = g_q_lat.shape[1]
    kv_lora = g_kv_lat.shape[1]
    tm = _token_tile(n, grp)
    gpt = tm // grp
    tn = 512
    assert bp == 1 and t_p % ATTN_TILE == 0 and t_p % grp == 0 and past % 512 == 0

    x3 = jnp.concatenate([x_prompt.reshape(n_p // grp, grp, d), x_sample], axis=0)

    n_c = bp + bs
    c_rows = 16
    c_all = jnp.zeros((c_rows, d), F32).at[:n_c].set(jnp.concatenate([c_prompt, c_sample], 0))
    n_modc = N_MOD * d

    def ada_epi(accs, ex, outs):
        outs[0][...] = accs[0] + ex[0][...]

    mods = []
    for l in range(depth):
        mods.append(_matmul(
            [(c_all, w_ada, _wspec(l, d, tn, 0), True)], m=c_rows, n_out=n_modc, tm=c_rows, tn=tn,
            prologue=lambda a, ex: a * jax.nn.sigmoid(a),
            epilogue=ada_epi,
            extras=[b_ada.reshape(depth, 1, n_modc)],
            extra_specs=[pl.BlockSpec((None, 1, tn), lambda j, i, l=l: (l, 0, j))],
            out_shape=[jax.ShapeDtypeStruct((c_rows, n_modc), F32)],
            out_specs=[_spec2(c_rows, tn)])[0])
    mod = jnp.stack(mods)
    modg = jnp.concatenate(
        [jnp.broadcast_to(mod[:, :bp], (depth, n_p // grp, n_modc)), mod[:, bp:n_c]], axis=1)
    modg = modg.reshape(depth, ng, N_MOD, d).transpose(0, 2, 1, 3)
    modg = modg.reshape(depth * N_MOD, ng, 1, d)

    def mod_idx(l, k):
        return l * N_MOD + k

    def mspec_rows(l, k):
        return pl.BlockSpec((None, gpt, 1, tn), lambda j, i: (mod_idx(l, k), i, 0, j))

    pos = jnp.concatenate([jnp.arange(t_p, dtype=jnp.int32),
                           jnp.tile(past + jnp.arange(t_s, dtype=jnp.int32), bs)])
    rope_c, rope_s = _rope_tables(pos)
    tq = ATTN_TILE
    nq_p = t_p // tq
    tab_p = _causal_tables(nq_p, 0)
    tk_s = 512
    pt = past // tk_s
    tab_sb_s = lambda l: _causal_tables(0, n_p // t_s, n_batch=bs, past_tiles=pt,
                                        past_stride=l * bs * pt)
    tab_mla_s = _causal_tables(0, n_p // t_s, n_batch=bs, past_tiles=pt)

    off_q, off_k, off_v = 0, sb_w, 2 * sb_w
    off_cq = 3 * sb_w
    off_ckv = off_cq + q_lora
    off_kr = off_ckv + kv_lora
    off_g = off_kr + QK_ROPE
    assert off_g == off_kr + QK_ROPE
    w_nk = jnp.swapaxes(w_in, 1, 2)
    w_gates, w_kr_aug = _split_gate_rope_weights(w_nk, off_kr, 2 * d, tn)
    uq = w_uq.reshape(depth, q_lora, H_MLA, QK_NOPE + QK_ROPE)
    uq_r = uq[..., QK_NOPE:]
    w_uq_cat = jnp.concatenate([uq[..., :QK_NOPE], uq_r, _swap_halves(uq_r)], axis=-1)
    w_uq_cat = w_uq_cat.reshape(depth, q_lora, H_MLA * 2 * LANES)
    w_router = jnp.concatenate(
        [w_router_expert, w_router_group,
         jnp.zeros((depth, d, LANES - N_EXPERTS - N_GROUPS), F32)], axis=-1)
    r_hi = w_router.astype(BF16)
    r_res = w_router - r_hi.astype(F32)
    r_mid = r_res.astype(BF16)
    r_lo = (r_res - r_mid.astype(F32)).astype(BF16)
    w_router3 = jnp.stack([r_hi, r_mid, r_lo], axis=1)
    b_router = jnp.concatenate(
        [b_router_expert, b_router_group,
         jnp.zeros((depth, LANES - N_EXPERTS - N_GROUPS), F32)], axis=-1).reshape(depth, 1, LANES)

    rows_past = bs * past
    kr_past = jnp.pad(cache_mla_krope.reshape(depth * rows_past, QK_ROPE),
                      ((0, 0), (0, LANES - QK_ROPE))).astype(BF16)
    w_ukv_b = w_ukv.astype(BF16)
    hm = lambda rows: jax.ShapeDtypeStruct((H_SB, rows, LANES), BF16)
    new_k, new_v, new_c, new_r = [], [], [], []

    for l in range(depth):
        h = _norm_mod(x3, g_norm_mix.reshape(depth, 1, d), modg, l, mod_idx(l, 1), mod_idx(l, 0))
        h = h.reshape(n, d)

        sbq_scale = DH_SB ** -0.5 * float(np.log2(np.e))

        def plain_hm(accs, ex, outs):
            _store_heads(outs[0], accs[0] * sbq_scale)

        def f32_and_hm(accs, ex, outs):
            outs[0][...] = accs[0]
            _store_heads(outs[1], accs[0])

        sb_q = _matmul([(h, w_nk, _wspec_nk(l, d, tn, off_q), True, True)], m=n, n_out=sb_w, tm=tm, tn=tn,
                       epilogue=plain_hm, out_shape=[hm(n)], out_specs=[_hm_spec(tm, tn)])[0]
        k_f32, sb_k = _matmul([(h, w_nk, _wspec_nk(l, d, tn, off_k), True, True)], m=n, n_out=sb_w, tm=tm,
                              tn=tn, epilogue=f32_and_hm,
                              out_shape=[jax.ShapeDtypeStruct((n, sb_w), F32), hm(n)],
                              out_specs=[_spec2(tm, tn), _hm_spec(tm, tn)])
        v_f32, sb_v = _matmul([(h, w_nk, _wspec_nk(l, d, tn, off_v), True, True)], m=n, n_out=sb_w, tm=tm,
                              tn=tn, epilogue=f32_and_hm,
                              out_shape=[jax.ShapeDtypeStruct((n, sb_w), F32), hm(n)],
                              out_specs=[_spec2(tm, tn), _hm_spec(tm, tn)])

        def plain_f32(accs, ex, outs):
            outs[0][...] = accs[0]

        c_q = _matmul([(h, w_nk, _wspec_nk(l, d, tn, off_cq), True, True)], m=n, n_out=q_lora, tm=tm, tn=tn,
                      epilogue=plain_f32, out_shape=[jax.ShapeDtypeStruct((n, q_lora), F32)],
                      out_specs=[_spec2(tm, tn)])[0]

        def ckv_epi(accs, ex, outs):
            y = _rms(accs[0], ex[0][...])
            outs[0][...] = y
            outs[1][...] = y.astype(BF16)

        c_kv, c_kv_b = _matmul(
            [(h, w_nk, _wspec_nk(l, d, kv_lora, off_ckv), True, True)], m=n, n_out=kv_lora, tm=tm, tn=kv_lora,
            epilogue=ckv_epi, extras=[g_kv_lat.reshape(depth, 1, kv_lora)],
            extra_specs=[pl.BlockSpec((None, 1, kv_lora), lambda j, i: (l, 0, 0))],
            out_shape=[jax.ShapeDtypeStruct((n, kv_lora), F32),
                       jax.ShapeDtypeStruct((n, kv_lora), BF16)],
            out_specs=[_spec2(tm, kv_lora), _spec2(tm, kv_lora)])

        def kr_epi(accs, ex, outs):
            r = _rope_lanes(accs[0], ex[0][...], ex[1][...])
            outs[0][...] = r
            outs[1][...] = r.astype(BF16)

        rope_specs = [pl.BlockSpec((tm, LANES), lambda j, i: (i, 0))] * 2
        k_r, k_r_b = _matmul(
            [(h, w_kr_aug, pl.BlockSpec((None, LANES, d), lambda j, i: (l, 0, 0)), False, True)],
            m=n, n_out=LANES, tm=tm, tn=LANES, epilogue=kr_epi,
            extras=[rope_c, rope_s], extra_specs=rope_specs,
            out_shape=[jax.ShapeDtypeStruct((n, LANES), F32), jax.ShapeDtypeStruct((n, LANES), BF16)],
            out_specs=[_spec2(tm, LANES), _spec2(tm, LANES)])

        def gate_epi(accs, ex, outs):
            outs[0][...] = jax.nn.sigmoid(accs[0])

        gates = _matmul(
            [(h, w_gates, pl.BlockSpec((None, tn, d), lambda j, i: (l, j, 0)), False, True)],
            m=n, n_out=2 * d, tm=tm, tn=tn, epilogue=gate_epi,
            out_shape=[jax.ShapeDtypeStruct((n, 2 * d), F32)], out_specs=[_spec2(tm, tn)])[0]

        def cq_prologue(a, ex):
            return _rms(a, ex[0][...])

        gq_spec = pl.BlockSpec((None, 1, q_lora), lambda j, i: (l, 0, 0))
        gq = g_q_lat.reshape(depth, 1, q_lora)
        wqk = 2 * LANES
        hm_qk = lambda rows: jax.ShapeDtypeStruct((H_MLA, rows, wqk), BF16)

        q_scale = (QK_NOPE + QK_ROPE) ** -0.5 * float(np.log2(np.e))

        def qcat_epi(accs, ex, outs):
            c, s = ex[1][...], ex[2][...]
            for hh in range(tn // wqk):
                blk = accs[0][:, hh * wqk:(hh + 1) * wqk]
                outs[0][hh, :, :LANES] = (blk[:, :LANES] * q_scale).astype(BF16)
                outs[0][hh, :, LANES:] = (_rope_lanes(blk[:, LANES:], c, s) * q_scale).astype(BF16)

        q_cat = _matmul(
            [(c_q, w_uq_cat, pl.BlockSpec((None, q_lora, tn), lambda j, i: (l, 0, j)), True)],
            m=n, n_out=H_MLA * wqk, tm=tm, tn=tn, prologue=cq_prologue, epilogue=qcat_epi,
            extras=[gq, rope_c, rope_s], extra_specs=[gq_spec] + rope_specs,
            out_shape=[hm_qk(n)],
            out_specs=[pl.BlockSpec((tn // wqk, tm, wqk), lambda j, i: (j, i, 0))])[0]

        tn_kv = min(2048, H_MLA * wqk)
        hpt = tn_kv // wqk

        def kv_epi(accs, ex, outs):
            kr = ex[0][...]
            ones = jnp.ones(kr.shape, BF16)
            for hh in range(hpt):
                outs[0][hh, :, :LANES] = accs[0][:, hh * wqk:hh * wqk + LANES].astype(BF16)
                outs[0][hh, :, LANES:] = kr
                outs[1][hh, :, :LANES] = accs[0][:, hh * wqk + LANES:(hh + 1) * wqk].astype(BF16)
                outs[1][hh, :, LANES:] = ones

        def up_kv(a, kr, rows, tmr):
            hspec = pl.BlockSpec((hpt, tmr, wqk), lambda j, i: (j, i, 0))
            return _matmul(
                [(a, w_ukv, pl.BlockSpec((None, kv_lora, tn_kv), lambda j, i: (l, 0, j)), True)],
                m=rows, n_out=H_MLA * wqk, tm=tmr, tn=tn_kv, epilogue=kv_epi, extras=[kr],
                extra_specs=[pl.BlockSpec((tmr, LANES), lambda j, i: (i, 0))],
                out_shape=[hm_qk(rows), hm_qk(rows)], out_specs=[hspec, hspec])

        kc_new, v_new = up_kv(c_kv_b, k_r_b, n, tm)

        o_sb_p = _sb_stream_attention(sb_q, sb_k, sb_v, sb_k, sb_v, tq=tq, tk=tq, q_block0=0,
                                      n_tiles=n_p // tq)
        cache_k2 = cache_sb_k.reshape(depth * bs * past * H_SB, DH_SB)
        cache_v2 = cache_sb_v.reshape(depth * bs * past * H_SB, DH_SB)
        q_off_s = n_p // t_s
        o_sb_s = _sb_stream_attention(sb_q, sb_k, sb_v, cache_k2, cache_v2, tq=t_s, tk=ATTN_TILE,
                                      q_block0=q_off_s, n_tiles=bs,
                                      cache_tiles=past // ATTN_TILE,
                                      cache_base=l * bs * (past // ATTN_TILE))
        o_sb = jnp.concatenate([o_sb_p, o_sb_s], axis=0)

        o_mla_p = _mla_attention(q_cat, kc_new, v_new, n_rows=n_p,
                                 tq=min(MLA_Q_TILE, n_p), tk=ATTN_TILE)
        tab_s = tab_sb_s(l)
        o_mla_s = _mla_latent_attention(
            q_cat, c_kv_b, k_r_b, cache_mla_ckv.reshape(depth * rows_past, kv_lora), kr_past,
            w_ukv_b, (tab_s[0], tab_s[1], tab_s[3]), l, tq=t_s, tk=tk_s, pos0=past, out_rows=n_s,
            out_block_of=lambda s, qb, kb, fl: (qb[s] - q_off_s, 0))
        o_mla = jnp.concatenate([o_mla_p, o_mla_s], axis=0)

        def merge_epi(accs, ex, outs):
            outs[0][...] = (ex[0][...] * accs[0] + ex[1][...] * accs[1]).astype(BF16)

        merged = _matmul(
            [(o_sb, w_branch_sb, _wspec(l, sb_w, tn, 0), True),
             (o_mla, w_branch_mla, _wspec(l, H_MLA * V_DIM, tn, 0), True)],
            m=n, n_out=d, tm=tm, tn=tn, epilogue=merge_epi,
            extras=[gates, gates],
            extra_specs=[pl.BlockSpec((tm, tn), lambda j, i: (i, j)),
                         pl.BlockSpec((tm, tn), lambda j, i: (i, j + d // tn))],
            out_shape=[jax.ShapeDtypeStruct((n, d), BF16)], out_specs=[_spec2(tm, tn)])[0]

        def resid_epi(accs, ex, outs):
            for g in range(gpt):
                outs[0][g] = ex[0][g] + ex[1][g] * accs[0][g * grp:(g + 1) * grp, :]

        x_spec = pl.BlockSpec((gpt, grp, tn), lambda j, i: (i, 0, j))
        x3 = _matmul(
            [(merged, w_out, _wspec(l, d, tn, 0), True)], m=n, n_out=d, tm=tm, tn=tn,
            epilogue=resid_epi, extras=[x3, modg], extra_specs=[x_spec, mspec_rows(l, 2)],
            out_shape=[jax.ShapeDtypeStruct((ng, grp, d), F32)], out_specs=[x_spec])[0]

        h2, route = _norm_route(x3, g_norm_ffn.reshape(depth, 1, d), modg, l, mod_idx(l, 4),
                                mod_idx(l, 3), w_router3, b_router)
        row_tok, row_dst, row_w, tile_expert, n_valid = _dispatch(route.reshape(n, LANES), n)
        yg = _moe_experts(h2, row_tok, row_dst, row_w, tile_expert, n_valid,
                          w_exp_gate, w_exp_up, w_exp_down, l, 2 * n)
        x3 = _combine(x3, yg, modg, l, mod_idx(l, 5))

        new_k.append(k_f32)
        new_v.append(v_f32)
        new_c.append(c_kv)
        new_r.append(k_r[:, :QK_ROPE])

    y_p, y_s = _final_norm(x3, g_final.reshape(1, d), n_p // grp)

    def split(parts, tail):
        a = jnp.stack(parts)
        return (a[:, :n_p].reshape((depth, bp, t_p) + tail),
                a[:, n_p:].reshape((depth, bs, t_s) + tail))

    pk, sk = split(new_k, (H_SB, DH_SB))
    pv, sv = split(new_v, (H_SB, DH_SB))
    pc, sc = split(new_c, (kv_lora,))
    pr, sr = split(new_r, (QK_ROPE,))
    return (y_p.reshape(bp, t_p, d), y_s.reshape(bs, t_s, d), pk, pv, pc, pr, sk, sv, sc, sr)
```

```python
import functools

import numpy as np
import jax
import jax.numpy as jnp
from jax import lax
from jax.experimental import pallas as pl
from jax.experimental.pallas import tpu as pltpu

F32 = jnp.float32
BF16 = jnp.bfloat16

CHUNK = 64
H_SB = 16
DH_SB = 128
H_MLA = 16
QK_NOPE = 128
QK_ROPE = 64
V_DIM = 128
ROPE_THETA = 10000.0
N_GROUPS = 4
EXPERTS_PER_GROUP = 8
N_EXPERTS = N_GROUPS * EXPERTS_PER_GROUP
N_MOD = 6
EPS = 1e-6

LANES = 128
ATTN_TILE = 256
MOE_TILE = 256
HEAD_UNROLL = 4
MLA_UNROLL = 8
MLA_Q_TILE = 512
SLAB_PAD = 4
SB_DEAD = 152.0
VMEM_LIMIT = 56 * 1024 * 1024


def _cparams(n_axes, vmem=VMEM_LIMIT):
    return pltpu.CompilerParams(dimension_semantics=("arbitrary",) * n_axes,
                                vmem_limit_bytes=vmem)


def _row_tile(n, cap=512):
    t = cap
    while n % t:
        t //= 2
    return t


def _token_tile(n, grp, cap=1152):
    return max(t for t in range(grp, cap + 1, grp) if n % t == 0)


def _matmul(pairs, *, m, n_out, tm, tn, epilogue, out_shape, out_specs,
            extras=(), extra_specs=(), prologue=None):
    n_pairs = len(pairs)
    n_ex = len(extras)
    n_outs = len(out_shape)
    pairs = [tuple(p) + (False,) * (5 - len(p)) for p in pairs]
    cast = [p[3] for p in pairs]
    b_nk = [p[4] for p in pairs]

    def kern(*refs):
        a_refs = refs[0:2 * n_pairs:2]
        b_refs = refs[1:2 * n_pairs:2]
        ex = refs[2 * n_pairs:2 * n_pairs + n_ex]
        outs = refs[2 * n_pairs + n_ex:2 * n_pairs + n_ex + n_outs]
        scr = refs[2 * n_pairs + n_ex + n_outs:]
        i = pl.program_id(1)
        accs = []
        si = 0
        for p in range(n_pairs):
            if cast[p]:
                bsc = scr[si]
                si += 1

                @pl.when(i == 0)
                def _(bsc=bsc, b_ref=b_refs[p]):
                    bsc[...] = b_ref[...].astype(BF16)

                bv = bsc[...]
            else:
                bv = b_refs[p][...]
            a = a_refs[p][...]
            if prologue is not None:
                a = prologue(a, ex)
            if b_nk[p]:
                accs.append(_dot_nt(a.astype(BF16), bv))
            else:
                accs.append(jnp.dot(a.astype(BF16), bv, preferred_element_type=F32))
        epilogue(accs, ex, outs)

    in_specs, args, scratch = [], [], []
    for (a, b, b_spec, cb, nk) in pairs:
        k = a.shape[1]
        in_specs += [pl.BlockSpec((tm, k), lambda j, i: (i, 0)), b_spec]
        args += [a, b]
        if cb:
            scratch.append(pltpu.VMEM((tn, k) if nk else (k, tn), BF16))
    in_specs += list(extra_specs)
    args += list(extras)
    return pl.pallas_call(
        kern,
        grid=(n_out // tn, m // tm),
        in_specs=in_specs,
        out_specs=out_specs,
        out_shape=out_shape,
        scratch_shapes=scratch,
        compiler_params=_cparams(2),
    )(*args)


def _wspec(l, k, tn, col_off):
    cb = col_off // tn
    assert cb * tn == col_off
    return pl.BlockSpec((None, k, tn), lambda j, i: (l, 0, cb + j))


def _wspec_nk(l, k, tn, row_off):
    rb = row_off // tn
    assert rb * tn == row_off
    return pl.BlockSpec((None, tn, k), lambda j, i: (l, rb + j, 0))


def _spec2(tm, tn):
    return pl.BlockSpec((tm, tn), lambda j, i: (i, j))


def _hm_spec(tm, tn):
    return pl.BlockSpec((tn // LANES, tm, LANES), lambda j, i: (j, i, 0))


def _store_heads(o_ref, val):
    for c in range(val.shape[1] // LANES):
        o_ref[c] = val[:, c * LANES:(c + 1) * LANES].astype(o_ref.dtype)


def _rms(x, g):
    return x * lax.rsqrt(jnp.mean(x * x, axis=-1, keepdims=True) + EPS) * g


def _mod_row(ref, g):
    return ref[pl.ds(g, 1), :]


def _norm_mod_kernel(x_ref, g_ref, sc_ref, sh_ref, o_ref):
    gb = x_ref.shape[0]
    g0 = pl.program_id(0) * gb
    for g in range(gb):
        y = _rms(x_ref[g], g_ref[...])
        o_ref[g] = (y * (1.0 + _mod_row(sc_ref, g0 + g)) + _mod_row(sh_ref, g0 + g)).astype(o_ref.dtype)


def _split3(x):
    hi = x.astype(BF16)
    r = x - hi.astype(F32)
    mid = r.astype(BF16)
    lo = (r - mid.astype(F32)).astype(BF16)
    return hi, mid, lo


def _dot_f32(a, b3):
    a_hi, a_mid, a_lo = _split3(a)
    b_hi, b_mid, b_lo = b3
    d = functools.partial(jnp.dot, preferred_element_type=F32)
    small = d(a_hi, b_lo) + d(a_lo, b_hi) + d(a_mid, b_mid)
    return (d(a_hi, b_hi) + (d(a_hi, b_mid) + d(a_mid, b_hi))) + small


def _route(logits):
    lane = lax.broadcasted_iota(jnp.int32, logits.shape, 1)
    lanef = lane.astype(F32)
    big = jnp.float32(1e9)
    ninf = jnp.float32(-jnp.inf)
    is_g = (lane >= N_EXPERTS) & (lane < N_EXPERTS + N_GROUPS)
    gl = jnp.where(is_g, logits, ninf)
    gmax = jnp.max(gl, axis=1, keepdims=True)
    g_idx = jnp.min(jnp.where(gl == gmax, lanef - N_EXPERTS, big), axis=1, keepdims=True)
    p_group = 1.0 / jnp.sum(jnp.where(is_g, jnp.exp(gl - gmax), 0.0), axis=1, keepdims=True)
    grp = jnp.floor(lanef * (1.0 / EXPERTS_PER_GROUP))
    in_g = (lane < N_EXPERTS) & (grp == g_idx)
    el = jnp.where(in_g, logits, ninf)
    e1 = jnp.max(el, axis=1, keepdims=True)
    i1 = jnp.min(jnp.where(el == e1, lanef, big), axis=1, keepdims=True)
    el2 = jnp.where(lanef == i1, ninf, el)
    e2 = jnp.max(el2, axis=1, keepdims=True)
    i2 = jnp.min(jnp.where(el2 == e2, lanef, big), axis=1, keepdims=True)
    t = jnp.exp(e2 - e1)
    den = 1.0 + t
    w1 = (1.0 / den) * p_group
    w2 = (t / den) * p_group
    out = jnp.where(lane == 0, i1, jnp.where(lane == 1, i2,
          jnp.where(lane == 2, w1, jnp.where(lane == 3, w2, 0.0))))
    return out


def _slab_pitch(s):
    return s + SLAB_PAD


def _slab_store(ref, row0, val):
    rows, d = val.shape
    s = d // LANES
    p = _slab_pitch(s)
    for c in range(s):
        ref[pl.ds(row0 * p + c, rows, stride=p), :] = val[:, c * LANES:(c + 1) * LANES]


def _slab_load(ref, row0, rows, s, lead=None):
    pieces = []
    p = _slab_pitch(s)
    for c in range(s):
        rs = pl.ds(row0 * p + c, rows, stride=p)
        pieces.append(ref[rs, :] if lead is None else ref[lead, rs, :])
    return jnp.concatenate(pieces, axis=1)


def _norm_route_kernel(x_ref, g_ref, sc_ref, sh_ref, wr_ref, br_ref, h_ref, r_ref):
    gb, grp, _ = x_ref.shape
    g0 = pl.program_id(0) * gb
    b3 = (wr_ref[0], wr_ref[1], wr_ref[2])
    for g in range(gb):
        y = _rms(x_ref[g], g_ref[...])
        h = y * (1.0 + _mod_row(sc_ref, g0 + g)) + _mod_row(sh_ref, g0 + g)
        _slab_store(h_ref, g * grp, h)
        logits = _dot_f32(h, b3) + br_ref[...]
        r_ref[g] = _route(logits)


def _mod_spec(idx, ng, d):
    return pl.BlockSpec((None, ng, d), lambda i: (idx, 0, 0))


def _norm_mod(x3, g, modg, l, sc_i, sh_i):
    ng, grp, d = x3.shape
    gb = _row_tile(ng, 4)
    return pl.pallas_call(
        _norm_mod_kernel,
        grid=(ng // gb,),
        in_specs=[pl.BlockSpec((gb, grp, d), lambda i: (i, 0, 0)),
                  pl.BlockSpec((None, 1, d), lambda i: (l, 0, 0)),
                  _mod_spec(sc_i, ng, d), _mod_spec(sh_i, ng, d)],
        out_specs=pl.BlockSpec((gb, grp, d), lambda i: (i, 0, 0)),
        out_shape=jax.ShapeDtypeStruct(x3.shape, BF16),
        compiler_params=_cparams(1),
    )(x3, g, modg, modg)


def _norm_route(x3, g, modg, l, sc_i, sh_i, wr3, br):
    ng, grp, d = x3.shape
    gb = _row_tile(ng, 4)
    return pl.pallas_call(
        _norm_route_kernel,
        grid=(ng // gb,),
        in_specs=[pl.BlockSpec((gb, grp, d), lambda i: (i, 0, 0)),
                  pl.BlockSpec((None, 1, d), lambda i: (l, 0, 0)),
                  _mod_spec(sc_i, ng, d), _mod_spec(sh_i, ng, d),
                  pl.BlockSpec((None, 3, d, LANES), lambda i: (l, 0, 0, 0)),
                  pl.BlockSpec((None, 1, LANES), lambda i: (l, 0, 0))],
        out_specs=[pl.BlockSpec((gb * grp * _slab_pitch(d // LANES), LANES), lambda i: (i, 0)),
                   pl.BlockSpec((gb, grp, LANES), lambda i: (i, 0, 0))],
        out_shape=[jax.ShapeDtypeStruct((ng * grp * _slab_pitch(d // LANES), LANES), F32),
                   jax.ShapeDtypeStruct((ng, grp, LANES), F32)],
        compiler_params=_cparams(1),
    )(x3, g, modg, modg, wr3, br)


def _final_norm_kernel(x_ref, g_ref, op_ref, os_ref, *, n_prompt_blocks):
    i = pl.program_id(0)
    y = _rms(x_ref[...], g_ref[...])

    @pl.when(i < n_prompt_blocks)
    def _():
        op_ref[...] = y

    @pl.when(i >= n_prompt_blocks)
    def _():
        os_ref[...] = y


def _final_norm(x3, g, ng_prompt):
    ng, grp, d = x3.shape
    gb = _row_tile(np.gcd(ng_prompt, ng - ng_prompt), 4)
    npb = ng_prompt // gb
    blk = lambda f: pl.BlockSpec((gb, grp, d), f)
    return pl.pallas_call(
        functools.partial(_final_norm_kernel, n_prompt_blocks=npb),
        grid=(ng // gb,),
        in_specs=[blk(lambda i: (i, 0, 0)), pl.BlockSpec((1, d), lambda i: (0, 0))],
        out_specs=[blk(lambda i: (jnp.minimum(i, npb - 1), 0, 0)),
                   blk(lambda i: (jnp.maximum(i - npb, 0), 0, 0))],
        out_shape=[jax.ShapeDtypeStruct((ng_prompt, grp, d), F32),
                   jax.ShapeDtypeStruct((ng - ng_prompt, grp, d), F32)],
        compiler_params=_cparams(1),
    )(x3, g)


def _lanes(c, w):
    if w % LANES == 0:
        return c if w == LANES else jnp.tile(c, (1, w // LANES))
    return c[:, :w]


def _dot_nt(a, b):
    return lax.dot_general(a, b, (((1,), (1,)), ((), ())), preferred_element_type=F32)


def _sb_block(qh, kh, vh, c, u, masked):
    w = kh.shape[0]
    z = _dot_nt(qh, kh)
    sp = jnp.maximum(z, 0.0) + jnp.log2(1.0 + jnp.exp2(-jnp.abs(z)))
    if masked:
        row = lax.broadcasted_iota(jnp.int32, z.shape, 0)
        col = lax.broadcasted_iota(jnp.int32, z.shape, 1)
        valid = col < row
        sp = jnp.where(valid, sp, 0.0)
    hi = sp.astype(BF16)
    lo = (sp - hi.astype(F32)).astype(BF16)
    if w % LANES == 0:
        cs = jnp.dot(jnp.concatenate([hi, lo], axis=1), u, preferred_element_type=F32)
    else:
        cs = (jnp.dot(hi, u[:w], preferred_element_type=F32)
              + jnp.dot(lo, u[w:], preferred_element_type=F32))
    wgt = jnp.exp2(z - sp - cs - _lanes(c, w))
    if masked:
        wgt = jnp.where(valid, wgt, 0.0)
    o = jnp.dot(wgt.astype(BF16), vh, preferred_element_type=F32)
    c_new = c + jnp.sum(sp, axis=1, keepdims=True)
    return o, c_new


def _suffix_matrix(w):
    j = np.arange(w)[:, None]
    s = np.arange(w)[None, :]
    u = (j > s).astype(np.float32)
    return jnp.asarray(np.concatenate([u, u], axis=0), dtype=BF16)


def _sb_stream_kernel(q_ref, kn_ref, vn_ref, k_hbm, v_hbm, ud_ref, up_ref, o_ref,
                      kbuf, vbuf, acc, carry, done, alive_ref, sem, *, heads, tk, cache_tiles,
                      cache_base):
    i = pl.program_id(0)
    n_past = i if cache_tiles is None else cache_tiles

    def fetch(j, slot):
        if cache_tiles is None:
            rows = pl.ds(pl.multiple_of(j * tk, tk), tk)
            src_k, src_v = k_hbm.at[:, rows, :], v_hbm.at[:, rows, :]
        else:
            blk = tk * heads
            rows = pl.ds(pl.multiple_of((cache_base + i * cache_tiles + j) * blk, blk), blk)
            src_k, src_v = k_hbm.at[rows], v_hbm.at[rows]
        return (pltpu.make_async_copy(src_k, kbuf.at[slot], sem.at[0, slot]),
                pltpu.make_async_copy(src_v, vbuf.at[slot], sem.at[1, slot]))

    def tile_of(buf, slot, h):
        if cache_tiles is None:
            return buf[slot, h]
        return buf.at[slot][pl.ds(h, tk, stride=heads), :].astype(BF16)

    @pl.when(n_past > 0)
    def _():
        for c in fetch(n_past - 1, 0):
            c.start()

    def diag(h, _):
        o, c = _sb_block(q_ref[h], kn_ref[h], vn_ref[h], jnp.zeros(carry.shape[1:], F32),
                         ud_ref[...], True)
        acc[h] = o
        carry[h] = c
        done[h] = 0
        return 0
    lax.fori_loop(0, heads, diag, 0, unroll=min(heads, HEAD_UNROLL))

    alive_ref[0] = heads

    @pl.when(n_past > 0)
    def _():
        for c in fetch(n_past - 1, 0):
            c.wait()

        @pl.when(n_past > 1)
        def _():
            for c in fetch(n_past - 2, 1):
                c.start()

        def first(h, alive):
            o, c = _sb_block(q_ref[h], tile_of(kbuf, 0, h), tile_of(vbuf, 0, h), carry[h],
                             up_ref[...], False)
            acc[h] = acc[h] + o
            carry[h] = c
            dead = (jnp.min(c) >= SB_DEAD).astype(jnp.int32)
            done[h] = dead
            return alive + 1 - dead
        alive_ref[0] = lax.fori_loop(0, heads, first, 0, unroll=min(heads, HEAD_UNROLL))

    def cond(state):
        j, alive = state
        return (j >= 0) & (alive > 0)

    def body(state):
        j, _ = state
        slot = lax.rem(n_past - 1 - j, 2)
        for c in fetch(j, slot):
            c.wait()

        @pl.when(j > 0)
        def _():
            for c in fetch(j - 1, 1 - slot):
                c.start()

        def head(h, alive):
            @pl.when(done[h] == 0)
            def _():
                o, c = _sb_block(q_ref[h], tile_of(kbuf, slot, h), tile_of(vbuf, slot, h),
                                 carry[h], up_ref[...], False)
                acc[h] = acc[h] + o
                carry[h] = c
                done[h] = (jnp.min(c) >= SB_DEAD).astype(jnp.int32)
            return alive + 1 - done[h]
        return j - 1, lax.fori_loop(0, heads, head, 0)

    j_end, _ = lax.while_loop(cond, body, (jnp.int32(n_past - 2), alive_ref[0]))

    @pl.when(j_end >= 0)
    def _():
        for c in fetch(j_end, lax.rem(n_past - 1 - j_end, 2)):
            c.wait()

    for h in range(heads):
        o_ref[:, h * DH_SB:(h + 1) * DH_SB] = acc[h].astype(o_ref.dtype)


def _sb_stream_attention(q, kn, vn, k_src, v_src, *, tq, tk, q_block0, n_tiles, cache_tiles=None,
                         cache_base=0):
    heads = q.shape[0]
    blk = pl.BlockSpec((heads, tq, DH_SB), lambda i: (0, q_block0 + i, 0))
    anyspec = pl.BlockSpec(memory_space=pl.ANY)
    if cache_tiles is None:
        buf = pltpu.VMEM((2, heads, tk, DH_SB), BF16)
    else:
        buf = pltpu.VMEM((2, tk * heads, DH_SB), F32)
    kern = functools.partial(_sb_stream_kernel, heads=heads, tk=tk, cache_tiles=cache_tiles,
                             cache_base=cache_base)
    return pl.pallas_call(
        kern, grid=(n_tiles,),
        in_specs=[blk, blk, blk, anyspec, anyspec,
                  pl.BlockSpec((2 * tq, tq), lambda i: (0, 0)),
                  pl.BlockSpec((2 * tk, tk), lambda i: (0, 0))],
        out_specs=pl.BlockSpec((tq, heads * DH_SB), lambda i: (i, 0)),
        out_shape=jax.ShapeDtypeStruct((n_tiles * tq, heads * DH_SB), BF16),
        scratch_shapes=[buf, buf,
                        pltpu.VMEM((heads, tq, DH_SB), F32),
                        pltpu.VMEM((heads, tq, LANES), F32),
                        pltpu.SMEM((heads,), jnp.int32),
                        pltpu.SMEM((1,), jnp.int32),
                        pltpu.SemaphoreType.DMA((2, 2))],
        compiler_params=_cparams(1),
    )(q, kn, vn, k_src, v_src, _suffix_matrix(tq), _suffix_matrix(tk))


def _mla_block(qc, kc, va, m, acc, mask):
    s = _dot_nt(qc, kc)
    if mask is not None:
        s = jnp.where(mask, s, -jnp.inf)
    m_new = jnp.maximum(m, jnp.max(s, axis=1, keepdims=True))
    alpha = jnp.exp2(m - m_new)
    p = jnp.exp2(s - _lanes(m_new, s.shape[1]))
    acc_new = _lanes(alpha, acc.shape[1]) * acc + jnp.dot(p.astype(BF16), va,
                                                          preferred_element_type=F32)
    return m_new, acc_new


def _mla_kernel(qb_ref, kb_ref, fl_ref, q_ref, k_ref, v_ref, o_ref, acc, m_sc, *, heads):
    s = pl.program_id(0)
    fl = fl_ref[s]
    tq, tk = q_ref.shape[1], k_ref.shape[1]

    @pl.when((fl & 1) != 0)
    def _():
        m_sc[...] = jnp.full(m_sc.shape, -jnp.inf, F32)
        acc[...] = jnp.zeros(acc.shape, F32)

    def run(mask):
        def body(h, _):
            m, a = _mla_block(q_ref[h], k_ref[h], v_ref[h], m_sc[h], acc[h], mask)
            m_sc[h] = m
            acc[h] = a
            return 0
        lax.fori_loop(0, heads, body, 0, unroll=min(heads, MLA_UNROLL))

    @pl.when((fl & 4) != 0)
    def _():
        row = lax.broadcasted_iota(jnp.int32, (tq, tk), 0) + qb_ref[s] * tq
        col = lax.broadcasted_iota(jnp.int32, (tq, tk), 1) + kb_ref[s] * tk
        run((col // CHUNK) <= (row // CHUNK))

    @pl.when((fl & 4) == 0)
    def _():
        run(None)

    @pl.when((fl & 2) != 0)
    def _():
        for h in range(heads):
            a = acc[h]
            o_ref[:, h * V_DIM:(h + 1) * V_DIM] = (a[:, :V_DIM] / a[:, V_DIM:]).astype(o_ref.dtype)


def _mla_tables(n_rows, tq, tk):
    r = tq // tk
    qb, kb, fl = [], [], []
    for i in range(n_rows // tq):
        tiles = [(i * r + j, 4) for j in range(r)] + [(j, 0) for j in reversed(range(i * r))]
        for idx, (j, f) in enumerate(tiles):
            qb.append(i)
            kb.append(j)
            fl.append(f | (1 if idx == 0 else 0) | (2 if idx == len(tiles) - 1 else 0))
    arr = lambda v: jnp.asarray(np.asarray(v, dtype=np.int32))
    return arr(qb), arr(kb), arr(fl)


def _mla_attention(qc, kc, va, *, n_rows, tq, tk):
    qb, kb, fl = _mla_tables(n_rows, tq, tk)
    h = H_MLA
    wqk = 2 * LANES
    kv_spec = pl.BlockSpec((h, tk, wqk), lambda s, qb, kb, fl: (0, kb[s], 0))
    grid_spec = pltpu.PrefetchScalarGridSpec(
        num_scalar_prefetch=3,
        grid=(qb.shape[0],),
        in_specs=[pl.BlockSpec((h, tq, wqk), lambda s, qb, kb, fl: (0, qb[s], 0)),
                  kv_spec, kv_spec],
        out_specs=pl.BlockSpec((tq, h * V_DIM), lambda s, qb, kb, fl: (qb[s], 0)),
        scratch_shapes=[pltpu.VMEM((h, tq, 2 * V_DIM), F32),
                        pltpu.VMEM((h, tq, LANES), F32)],
    )
    return pl.pallas_call(
        functools.partial(_mla_kernel, heads=h), grid_spec=grid_spec,
        out_shape=jax.ShapeDtypeStruct((n_rows, h * V_DIM), BF16),
        compiler_params=_cparams(1),
    )(qb, kb, fl, qc, kc, va)


def _mla_latent_kernel(qb_ref, kb_ref, fl_ref, q_ref, cn_ref, rn_ref, cp_ref, rp_ref, w_ref,
                       o_ref, qa, qr, acc, m_sc, l_sc, *, heads, pos0):
    s_id = pl.program_id(0)
    fl = fl_ref[s_id]
    is_first = (fl & 1) != 0
    is_last = (fl & 2) != 0
    tq = q_ref.shape[1]
    hw = QK_NOPE + V_DIM

    def step(ck, kr, mask):
        s = _dot_nt(qa[...], ck) + _dot_nt(qr[...], kr)
        if mask is not None:
            s = jnp.where(mask, s, -jnp.inf)
        m_old = m_sc[...]
        m_new = jnp.maximum(m_old, jnp.max(s, axis=1, keepdims=True))
        alpha = jnp.exp2(m_old - m_new)
        p = jnp.exp2(s - m_new[:, :1])
        l_sc[...] = alpha * l_sc[...] + jnp.sum(p, axis=1, keepdims=True)
        acc[...] = alpha[:, :1] * acc[...] + jnp.dot(p.astype(BF16), ck,
                                                     preferred_element_type=F32)
        m_sc[...] = m_new

    @pl.when(is_first)
    def _():
        for h in range(heads):
            qh = q_ref[h]
            w_uk = w_ref[:, h * hw:h * hw + QK_NOPE]
            qa[h * tq:(h + 1) * tq, :] = _dot_nt(qh[:, :QK_NOPE], w_uk).astype(BF16)
            qr[h * tq:(h + 1) * tq, :] = qh[:, QK_NOPE:]
        m_sc[...] = jnp.full(m_sc.shape, -jnp.inf, F32)
        l_sc[...] = jnp.zeros(l_sc.shape, F32)
        acc[...] = jnp.zeros(acc.shape, F32)
        row = lax.broadcasted_iota(jnp.int32, (heads * tq, tq), 0) % tq + pos0
        col = lax.broadcasted_iota(jnp.int32, (heads * tq, tq), 1) + pos0
        step(cn_ref[...], rn_ref[...], (col // CHUNK) <= (row // CHUNK))

    @pl.when(jnp.logical_not(is_first))
    def _():
        step(cp_ref[...].astype(BF16), rp_ref[...], None)

    @pl.when(is_last)
    def _():
        o_lat = (acc[...] / l_sc[...][:, :1]).astype(BF16)
        for h in range(heads):
            w_uv = w_ref[:, h * hw + QK_NOPE:(h + 1) * hw]
            o_ref[:, h * V_DIM:(h + 1) * V_DIM] = jnp.dot(
                o_lat[h * tq:(h + 1) * tq, :], w_uv, preferred_element_type=F32).astype(o_ref.dtype)


def _mla_latent_attention(qc, c_new, r_new, c_past, r_past, w_ukv_b, tables, l, *, tq, tk, pos0,
                          out_rows, out_block_of):
    qb, kb, fl = tables
    h = H_MLA
    kvl = c_new.shape[1]
    wqk = 2 * LANES
    grid_spec = pltpu.PrefetchScalarGridSpec(
        num_scalar_prefetch=3,
        grid=(qb.shape[0],),
        in_specs=[pl.BlockSpec((h, tq, wqk), lambda s, qb, kb, fl: (0, qb[s], 0)),
                  pl.BlockSpec((tq, kvl), lambda s, qb, kb, fl: (qb[s], 0)),
                  pl.BlockSpec((tq, LANES), lambda s, qb, kb, fl: (qb[s], 0)),
                  pl.BlockSpec((tk, kvl), lambda s, qb, kb, fl: (kb[s], 0)),
                  pl.BlockSpec((tk, LANES), lambda s, qb, kb, fl: (kb[s], 0)),
                  pl.BlockSpec((None, kvl, w_ukv_b.shape[2]), lambda s, qb, kb, fl: (l, 0, 0))],
        out_specs=pl.BlockSpec((tq, h * V_DIM), out_block_of),
        scratch_shapes=[pltpu.VMEM((h * tq, kvl), BF16),
                        pltpu.VMEM((h * tq, LANES), BF16),
                        pltpu.VMEM((h * tq, kvl), F32),
                        pltpu.VMEM((h * tq, LANES), F32),
                        pltpu.VMEM((h * tq, LANES), F32)],
    )
    kern = functools.partial(_mla_latent_kernel, heads=h, pos0=pos0)
    return pl.pallas_call(
        kern, grid_spec=grid_spec,
        out_shape=jax.ShapeDtypeStruct((out_rows, h * V_DIM), BF16),
        compiler_params=_cparams(1),
    )(qb, kb, fl, qc, c_new, r_new, c_past, r_past, w_ukv_b)


def _causal_tables(nq, q_off, per_head=False, heads=1, n_batch=1, past_tiles=None,
                   past_stride=0):
    qb, kb, hb, fl = [], [], [], []
    if past_tiles is None:
        for i in range(nq):
            n = i + 1
            for j in range(n):
                qb.append(q_off + i)
                kb.append(max(i - j, 1) - 1 if j == 0 else i - j)
                hb.append(0)
                fl.append((1 if j == 0 else 0) | (2 if j == n - 1 else 0))
    else:
        for b in range(n_batch):
            for h in range(heads if per_head else 1):
                n = 1 + past_tiles
                for j in range(n):
                    qb.append(q_off + b)
                    jj = past_tiles - 1 if j == 0 else past_tiles - j
                    kb.append(past_stride + b * past_tiles + jj)
                    hb.append(h)
                    fl.append((1 if j == 0 else 0) | (2 if j == n - 1 else 0))
    arr = lambda v: jnp.asarray(np.asarray(v, dtype=np.int32))
    return arr(qb), arr(kb), arr(hb), arr(fl)


ROW_DMA_UNROLL = 8


def _moe_kernel(te_ref, nv_ref, tok_ref, dst_ref, h_hbm, w_ref, wg_ref, wu_ref, wd_ref, y_hbm,
                xbuf, obuf, wg_s, wu_s, wd_s, sem_in, sem_out, *, s):
    t = pl.program_id(0)
    tm = w_ref.shape[0]
    nv = nv_ref[0]
    live = t < nv
    slot = lax.rem(t, 2)

    p = _slab_pitch(s)

    def in_copy(tile, r, sl):
        src0 = pl.multiple_of(tok_ref[tile * tm + r] * p, SLAB_PAD)
        return pltpu.make_async_copy(h_hbm.at[pl.ds(src0, s)],
                                     xbuf.at[sl, pl.ds(pl.multiple_of(r * p, SLAB_PAD), s)],
                                     sem_in.at[sl])

    def gather_start(tile, sl):
        def body(r, _):
            in_copy(tile, r, sl).start()
            return 0
        lax.fori_loop(0, tm, body, 0, unroll=ROW_DMA_UNROLL)

    def gather_wait(tile, sl):
        def body(r, _):
            in_copy(tile, r, sl).wait()
            return 0
        lax.fori_loop(0, tm, body, 0, unroll=ROW_DMA_UNROLL)

    @pl.when(live & (t == 0))
    def _():
        gather_start(0, 0)

    @pl.when(t + 1 < nv)
    def _():
        gather_start(t + 1, 1 - slot)

    prev = te_ref[jnp.maximum(t - 1, 0)]
    fresh = (t == 0) | (te_ref[t] != prev)

    @pl.when(live & fresh)
    def _():
        wg_s[...] = wg_ref[...].astype(BF16)
        wu_s[...] = wu_ref[...].astype(BF16)
        wd_s[...] = wd_ref[...].astype(BF16)

    @pl.when(live)
    def _():
        gather_wait(t, slot)
        x = _slab_load(xbuf, 0, tm, s, lead=slot).astype(BF16)
        a = jnp.dot(x, wg_s[...], preferred_element_type=F32)
        u = jnp.dot(x, wu_s[...], preferred_element_type=F32)
        hid = (a * jax.nn.sigmoid(a)) * u * w_ref[...]
        y = jnp.dot(hid.astype(BF16), wd_s[...], preferred_element_type=F32)

        def out_copy(tile, r):
            dst0 = pl.multiple_of(dst_ref[tile * tm + r] * p, SLAB_PAD)
            return pltpu.make_async_copy(obuf.at[pl.ds(pl.multiple_of(r * p, SLAB_PAD), s)],
                                         y_hbm.at[pl.ds(dst0, s)], sem_out)

        def scatter_wait(tile):
            def wait(r, _):
                out_copy(tile, r).wait()
                return 0
            lax.fori_loop(0, tm, wait, 0, unroll=ROW_DMA_UNROLL)

        @pl.when(t > 0)
        def _():
            scatter_wait(t - 1)

        _slab_store(obuf, 0, y)

        def start(r, _):
            out_copy(t, r).start()
            return 0
        lax.fori_loop(0, tm, start, 0, unroll=ROW_DMA_UNROLL)

        @pl.when(t == nv - 1)
        def _():
            scatter_wait(t)


def _moe_experts(h_slab, row_tok, row_dst, row_w, tile_expert, n_valid, w_gate, w_up, w_down, l,
                 n_dest):
    d, f = w_gate.shape[-2:]
    s = d // LANES
    p = _slab_pitch(s)
    tm = MOE_TILE
    n_tiles = row_tok.shape[0] // tm
    wmap = lambda t, te, nv, tok, dst: (l, te[t], 0, 0)
    grid_spec = pltpu.PrefetchScalarGridSpec(
        num_scalar_prefetch=4, grid=(n_tiles,),
        in_specs=[pl.BlockSpec(memory_space=pl.ANY),
                  pl.BlockSpec((tm, 1), lambda t, te, nv, tok, dst: (t, 0)),
                  pl.BlockSpec((None, None, d, f), wmap),
                  pl.BlockSpec((None, None, d, f), wmap),
                  pl.BlockSpec((None, None, f, d), wmap)],
        out_specs=pl.BlockSpec(memory_space=pl.ANY),
        scratch_shapes=[pltpu.VMEM((2, tm * p, LANES), F32), pltpu.VMEM((tm * p, LANES), F32),
                        pltpu.VMEM((d, f), BF16), pltpu.VMEM((d, f), BF16),
                        pltpu.VMEM((f, d), BF16),
                        pltpu.SemaphoreType.DMA((2,)), pltpu.SemaphoreType.DMA(())],
    )
    return pl.pallas_call(
        functools.partial(_moe_kernel, s=s), grid_spec=grid_spec,
        out_shape=jax.ShapeDtypeStruct(((n_dest + tm) * p, LANES), F32),
        compiler_params=_cparams(1),
    )(tile_expert, n_valid, row_tok, row_dst, h_slab, row_w, w_gate, w_up, w_down)


def _combine_kernel(x_ref, y0_ref, y1_ref, g_ref, o_ref):
    gb, grp, d = x_ref.shape
    s = d // LANES
    g0 = pl.program_id(0) * gb
    for g in range(gb):
        y = _slab_load(y0_ref, g * grp, grp, s) + _slab_load(y1_ref, g * grp, grp, s)
        o_ref[g] = x_ref[g] + _mod_row(g_ref, g0 + g) * y


def _combine(x3, yg, modg, l, g_i):
    ng, grp, d = x3.shape
    s = d // LANES
    gb = _row_tile(ng, 2)
    blk = pl.BlockSpec((gb, grp, d), lambda i: (i, 0, 0))
    yblk = lambda off: pl.BlockSpec((gb * grp * _slab_pitch(s), LANES), lambda i: (i + off, 0))
    return pl.pallas_call(
        _combine_kernel,
        grid=(ng // gb,),
        in_specs=[blk, yblk(0), yblk(ng // gb), _mod_spec(g_i, ng, d)],
        out_specs=blk,
        out_shape=jax.ShapeDtypeStruct(x3.shape, F32),
        compiler_params=_cparams(1),
    )(x3, yg, yg, modg)


def _dispatch(route, n):
    tm = MOE_TILE
    r_cap = (2 * n + N_EXPERTS * (tm - 1) + tm - 1) // tm * tm
    e = route[:, :2].astype(jnp.int32)
    w = route[:, 2:4]
    flat_e = e.T.reshape(-1)
    flat_w = w.T.reshape(-1)
    order = jnp.argsort(flat_e, stable=True).astype(jnp.int32)
    bounds = jnp.searchsorted(flat_e[order], jnp.arange(N_EXPERTS + 1, dtype=jnp.int32),
                              side="left").astype(jnp.int32)
    counts = bounds[1:] - bounds[:-1]
    padded = (counts + tm - 1) // tm * tm
    ends_p = jnp.cumsum(padded)
    starts_p = ends_p - padded
    starts = jnp.cumsum(counts) - counts
    tile_start = jnp.arange(r_cap // tm, dtype=jnp.int32) * tm
    tile_expert = jnp.minimum(jnp.searchsorted(ends_p, tile_start, side="right"),
                              N_EXPERTS - 1).astype(jnp.int32)
    n_valid = (ends_p[-1:] // tm).astype(jnp.int32)
    per_row = lambda v: jnp.broadcast_to(v[:, None], (r_cap // tm, tm)).reshape(r_cap)
    local = jnp.arange(r_cap, dtype=jnp.int32) - per_row(starts_p[tile_expert])
    valid = local < per_row(counts[tile_expert])
    slot = order[jnp.clip(per_row(starts[tile_expert]) + local, 0, 2 * n - 1)]
    row_dst = jnp.where(valid, slot, 2 * n + jnp.arange(r_cap, dtype=jnp.int32) % tm)
    row_tok = jnp.where(valid, jnp.where(slot >= n, slot - n, slot), 0)
    row_w = jnp.where(valid, flat_w[slot], 0.0)
    return row_tok, row_dst, row_w.reshape(r_cap, 1), tile_expert, n_valid


def _gate_weights_kernel(a_ref, b_ref, o_ref):
    r = b_ref.shape[0]
    w = a_ref.shape[0]
    o_ref[:w - r, :] = a_ref[r:, :].astype(BF16)
    o_ref[w - r:, :] = b_ref[...].astype(BF16)


def _rope_weights_kernel(a_ref, o_ref):
    a = a_ref[...].astype(BF16)
    q = a.shape[0] // 2
    o_ref[...] = jnp.concatenate([a, a[q:], a[:q]], axis=0)


def _split_gate_rope_weights(w_nk, off_kr, n_gate, tn):
    depth, n_in, k = w_nk.shape
    assert off_kr % tn == 0 and n_gate % tn == 0 and QK_ROPE * 2 == LANES
    rb = off_kr // tn
    gates = pl.pallas_call(
        _gate_weights_kernel,
        grid=(depth, n_gate // tn),
        in_specs=[pl.BlockSpec((None, tn, k), lambda l, j: (l, rb + j, 0)),
                  pl.BlockSpec((None, QK_ROPE, k),
                               lambda l, j: (l, (rb + j + 1) * (tn // QK_ROPE), 0))],
        out_specs=pl.BlockSpec((None, tn, k), lambda l, j: (l, j, 0)),
        out_shape=jax.ShapeDtypeStruct((depth, n_gate, k), BF16),
        compiler_params=_cparams(2),
    )(w_nk, w_nk)
    rope = pl.pallas_call(
        _rope_weights_kernel,
        grid=(depth,),
        in_specs=[pl.BlockSpec((None, QK_ROPE, k), lambda l: (l, off_kr // QK_ROPE, 0))],
        out_specs=pl.BlockSpec((None, 2 * QK_ROPE, k), lambda l: (l, 0, 0)),
        out_shape=jax.ShapeDtypeStruct((depth, 2 * QK_ROPE, k), BF16),
        compiler_params=_cparams(1),
    )(w_nk)
    return gates, rope


def _rope_tables(pos):
    inv = ROPE_THETA ** (-jnp.arange(0, QK_ROPE, 2, dtype=F32) / QK_ROPE)
    ang = pos.astype(F32)[:, None] * inv[None, :]
    cos, sin = jnp.cos(ang), jnp.sin(ang)
    pad = jnp.zeros((pos.shape[0], LANES - QK_ROPE), F32)
    return (jnp.concatenate([cos, cos, pad], axis=1),
            jnp.concatenate([-sin, sin, pad], axis=1))


def _rope_lanes(acc, c, s):
    return acc * c + pltpu.roll(acc, LANES - QK_ROPE, axis=1) * s


def _swap_halves(w):
    half = w.shape[-1] // 2
    return jnp.concatenate([w[..., half:], w[..., :half]], axis=-1)


def kernel(x_prompt, x_sample, c_prompt, c_sample, cache_sb_k, cache_sb_v, cache_mla_ckv,
           cache_mla_krope, w_ada, b_ada, g_norm_mix, g_norm_ffn, w_in, g_q_lat, g_kv_lat,
           w_uq, w_ukv, w_branch_sb, w_branch_mla, w_out, w_router_group, b_router_group,
           w_router_expert, b_router_expert, w_exp_gate, w_exp_up, w_exp_down, g_final):
    bp, t_p, d = x_prompt.shape
    bs, t_s, _ = x_sample.shape
    depth = w_in.shape[0]
    past = cache_sb_k.shape[2]
    grp = t_s
    n_p, n_s = bp * t_p, bs * t_s
    n = n_p + n_s
    ng = n // grp
    sb_w = H_SB * DH_SB
    q_lora = g_q_lat.shape[1]
    kv_lora = g_kv_lat.shape[1]
    tm = _token_tile(n, grp)
    gpt = tm // grp
    tn = 512
    assert bp == 1 and t_p % ATTN_TILE == 0 and t_p % grp == 0 and past % 512 == 0

    x3 = jnp.concatenate([x_prompt.reshape(n_p // grp, grp, d), x_sample], axis=0)

    n_c = bp + bs
    c_rows = 16
    c_all = jnp.zeros((c_rows, d), F32).at[:n_c].set(jnp.concatenate([c_prompt, c_sample], 0))
    n_modc = N_MOD * d

    def ada_epi(accs, ex, outs):
        outs[0][...] = accs[0] + ex[0][...]

    mods = []
    for l in range(depth):
        mods.append(_matmul(
            [(c_all, w_ada, _wspec(l, d, tn, 0), True)], m=c_rows, n_out=n_modc, tm=c_rows, tn=tn,
            prologue=lambda a, ex: a * jax.nn.sigmoid(a),
            epilogue=ada_epi,
            extras=[b_ada.reshape(depth, 1, n_modc)],
            extra_specs=[pl.BlockSpec((None, 1, tn), lambda j, i, l=l: (l, 0, j))],
            out_shape=[jax.ShapeDtypeStruct((c_rows, n_modc), F32)],
            out_specs=[_spec2(c_rows, tn)])[0])
    mod = jnp.stack(mods)
    modg = jnp.concatenate(
        [jnp.broadcast_to(mod[:, :bp], (depth, n_p // grp, n_modc)), mod[:, bp:n_c]], axis=1)
    modg = modg.reshape(depth, ng, N_MOD, d).transpose(0, 2, 1, 3)
    modg = modg.reshape(depth * N_MOD, ng, d)

    def mod_idx(l, k):
        return l * N_MOD + k

    def mspec_rows(l, k):
        return pl.BlockSpec((None, ng, tn), lambda j, i: (mod_idx(l, k), 0, j))

    pos = jnp.concatenate([jnp.arange(t_p, dtype=jnp.int32),
                           jnp.tile(past + jnp.arange(t_s, dtype=jnp.int32), bs)])
    rope_c, rope_s = _rope_tables(pos)
    tq = ATTN_TILE
    nq_p = t_p // tq
    tab_p = _causal_tables(nq_p, 0)
    tk_s = 512
    pt = past // tk_s
    tab_sb_s = lambda l: _causal_tables(0, n_p // t_s, n_batch=bs, past_tiles=pt,
                                        past_stride=l * bs * pt)
    tab_mla_s = _causal_tables(0, n_p // t_s, n_batch=bs, past_tiles=pt)

    off_q, off_k, off_v = 0, sb_w, 2 * sb_w
    off_cq = 3 * sb_w
    off_ckv = off_cq + q_lora
    off_kr = off_ckv + kv_lora
    off_g = off_kr + QK_ROPE
    assert off_g == off_kr + QK_ROPE
    w_nk = jnp.swapaxes(w_in, 1, 2)
    w_gates, w_kr_aug = _split_gate_rope_weights(w_nk, off_kr, 2 * d, tn)
    uq = w_uq.reshape(depth, q_lora, H_MLA, QK_NOPE + QK_ROPE)
    uq_r = uq[..., QK_NOPE:]
    w_uq_cat = jnp.concatenate([uq[..., :QK_NOPE], uq_r, _swap_halves(uq_r)], axis=-1)
    w_uq_cat = w_uq_cat.reshape(depth, q_lora, H_MLA * 2 * LANES)
    w_router = jnp.concatenate(
        [w_router_expert, w_router_group,
         jnp.zeros((depth, d, LANES - N_EXPERTS - N_GROUPS), F32)], axis=-1)
    r_hi = w_router.astype(BF16)
    r_res = w_router - r_hi.astype(F32)
    r_mid = r_res.astype(BF16)
    r_lo = (r_res - r_mid.astype(F32)).astype(BF16)
    w_router3 = jnp.stack([r_hi, r_mid, r_lo], axis=1)
    b_router = jnp.concatenate(
        [b_router_expert, b_router_group,
         jnp.zeros((depth, LANES - N_EXPERTS - N_GROUPS), F32)], axis=-1).reshape(depth, 1, LANES)

    rows_past = bs * past
    kr_past = jnp.pad(cache_mla_krope.reshape(depth * rows_past, QK_ROPE),
                      ((0, 0), (0, LANES - QK_ROPE))).astype(BF16)
    w_ukv_b = w_ukv.astype(BF16)
    hm = lambda rows: jax.ShapeDtypeStruct((H_SB, rows, LANES), BF16)
    new_k, new_v, new_c, new_r = [], [], [], []

    for l in range(depth):
        h = _norm_mod(x3, g_norm_mix.reshape(depth, 1, d), modg, l, mod_idx(l, 1), mod_idx(l, 0))
        h = h.reshape(n, d)

        sbq_scale = DH_SB ** -0.5 * float(np.log2(np.e))

        def plain_hm(accs, ex, outs):
            _store_heads(outs[0], accs[0] * sbq_scale)

        def f32_and_hm(accs, ex, outs):
            outs[0][...] = accs[0]
            _store_heads(outs[1], accs[0])

        sb_q = _matmul([(h, w_nk, _wspec_nk(l, d, tn, off_q), True, True)], m=n, n_out=sb_w, tm=tm, tn=tn,
                       epilogue=plain_hm, out_shape=[hm(n)], out_specs=[_hm_spec(tm, tn)])[0]
        k_f32, sb_k = _matmul([(h, w_nk, _wspec_nk(l, d, tn, off_k), True, True)], m=n, n_out=sb_w, tm=tm,
                              tn=tn, epilogue=f32_and_hm,
                              out_shape=[jax.ShapeDtypeStruct((n, sb_w), F32), hm(n)],
                              out_specs=[_spec2(tm, tn), _hm_spec(tm, tn)])
        v_f32, sb_v = _matmul([(h, w_nk, _wspec_nk(l, d, tn, off_v), True, True)], m=n, n_out=sb_w, tm=tm,
                              tn=tn, epilogue=f32_and_hm,
                              out_shape=[jax.ShapeDtypeStruct((n, sb_w), F32), hm(n)],
                              out_specs=[_spec2(tm, tn), _hm_spec(tm, tn)])

        def plain_f32(accs, ex, outs):
            outs[0][...] = accs[0]

        c_q = _matmul([(h, w_nk, _wspec_nk(l, d, tn, off_cq), True, True)], m=n, n_out=q_lora, tm=tm, tn=tn,
                      epilogue=plain_f32, out_shape=[jax.ShapeDtypeStruct((n, q_lora), F32)],
                      out_specs=[_spec2(tm, tn)])[0]

        def ckv_epi(accs, ex, outs):
            y = _rms(accs[0], ex[0][...])
            outs[0][...] = y
            outs[1][...] = y.astype(BF16)

        c_kv, c_kv_b = _matmul(
            [(h, w_nk, _wspec_nk(l, d, kv_lora, off_ckv), True, True)], m=n, n_out=kv_lora, tm=tm, tn=kv_lora,
            epilogue=ckv_epi, extras=[g_kv_lat.reshape(depth, 1, kv_lora)],
            extra_specs=[pl.BlockSpec((None, 1, kv_lora), lambda j, i: (l, 0, 0))],
            out_shape=[jax.ShapeDtypeStruct((n, kv_lora), F32),
                       jax.ShapeDtypeStruct((n, kv_lora), BF16)],
            out_specs=[_spec2(tm, kv_lora), _spec2(tm, kv_lora)])

        def kr_epi(accs, ex, outs):
            r = _rope_lanes(accs[0], ex[0][...], ex[1][...])
            outs[0][...] = r
            outs[1][...] = r.astype(BF16)

        rope_specs = [pl.BlockSpec((tm, LANES), lambda j, i: (i, 0))] * 2
        k_r, k_r_b = _matmul(
            [(h, w_kr_aug, pl.BlockSpec((None, LANES, d), lambda j, i: (l, 0, 0)), False, True)],
            m=n, n_out=LANES, tm=tm, tn=LANES, epilogue=kr_epi,
            extras=[rope_c, rope_s], extra_specs=rope_specs,
            out_shape=[jax.ShapeDtypeStruct((n, LANES), F32), jax.ShapeDtypeStruct((n, LANES), BF16)],
            out_specs=[_spec2(tm, LANES), _spec2(tm, LANES)])

        def gate_epi(accs, ex, outs):
            outs[0][...] = jax.nn.sigmoid(accs[0])

        gates = _matmul(
            [(h, w_gates, pl.BlockSpec((None, tn, d), lambda j, i: (l, j, 0)), False, True)],
            m=n, n_out=2 * d, tm=tm, tn=tn, epilogue=gate_epi,
            out_shape=[jax.ShapeDtypeStruct((n, 2 * d), F32)], out_specs=[_spec2(tm, tn)])[0]

        def cq_prologue(a, ex):
            return _rms(a, ex[0][...])

        gq_spec = pl.BlockSpec((None, 1, q_lora), lambda j, i: (l, 0, 0))
        gq = g_q_lat.reshape(depth, 1, q_lora)
        wqk = 2 * LANES
        hm_qk = lambda rows: jax.ShapeDtypeStruct((H_MLA, rows, wqk), BF16)

        q_scale = (QK_NOPE + QK_ROPE) ** -0.5 * float(np.log2(np.e))

        def qcat_epi(accs, ex, outs):
            c, s = ex[1][...], ex[2][...]
            for hh in range(tn // wqk):
                blk = accs[0][:, hh * wqk:(hh + 1) * wqk]
                outs[0][hh, :, :LANES] = (blk[:, :LANES] * q_scale).astype(BF16)
                outs[0][hh, :, LANES:] = (_rope_lanes(blk[:, LANES:], c, s) * q_scale).astype(BF16)

        q_cat = _matmul(
            [(c_q, w_uq_cat, pl.BlockSpec((None, q_lora, tn), lambda j, i: (l, 0, j)), True)],
            m=n, n_out=H_MLA * wqk, tm=tm, tn=tn, prologue=cq_prologue, epilogue=qcat_epi,
            extras=[gq, rope_c, rope_s], extra_specs=[gq_spec] + rope_specs,
            out_shape=[hm_qk(n)],
            out_specs=[pl.BlockSpec((tn // wqk, tm, wqk), lambda j, i: (j, i, 0))])[0]

        tn_kv = min(2048, H_MLA * wqk)
        hpt = tn_kv // wqk

        def kv_epi(accs, ex, outs):
            kr = ex[0][...]
            ones = jnp.ones(kr.shape, BF16)
            for hh in range(hpt):
                outs[0][hh, :, :LANES] = accs[0][:, hh * wqk:hh * wqk + LANES].astype(BF16)
                outs[0][hh, :, LANES:] = kr
                outs[1][hh, :, :LANES] = accs[0][:, hh * wqk + LANES:(hh + 1) * wqk].astype(BF16)
                outs[1][hh, :, LANES:] = ones

        def up_kv(a, kr, rows, tmr):
            hspec = pl.BlockSpec((hpt, tmr, wqk), lambda j, i: (j, i, 0))
            return _matmul(
                [(a, w_ukv, pl.BlockSpec((None, kv_lora, tn_kv), lambda j, i: (l, 0, j)), True)],
                m=rows, n_out=H_MLA * wqk, tm=tmr, tn=tn_kv, epilogue=kv_epi, extras=[kr],
                extra_specs=[pl.BlockSpec((tmr, LANES), lambda j, i: (i, 0))],
                out_shape=[hm_qk(rows), hm_qk(rows)], out_specs=[hspec, hspec])

        kc_new, v_new = up_kv(c_kv_b, k_r_b, n, tm)

        o_sb_p = _sb_stream_attention(sb_q, sb_k, sb_v, sb_k, sb_v, tq=tq, tk=tq, q_block0=0,
                                      n_tiles=n_p // tq)
        cache_k2 = cache_sb_k.reshape(depth * bs * past * H_SB, DH_SB)
        cache_v2 = cache_sb_v.reshape(depth * bs * past * H_SB, DH_SB)
        q_off_s = n_p // t_s
        o_sb_s = _sb_stream_attention(sb_q, sb_k, sb_v, cache_k2, cache_v2, tq=t_s, tk=ATTN_TILE,
                                      q_block0=q_off_s, n_tiles=bs,
                                      cache_tiles=past // ATTN_TILE,
                                      cache_base=l * bs * (past // ATTN_TILE))
        o_sb = jnp.concatenate([o_sb_p, o_sb_s], axis=0)

        o_mla_p = _mla_attention(q_cat, kc_new, v_new, n_rows=n_p,
                                 tq=min(MLA_Q_TILE, n_p), tk=ATTN_TILE)
        tab_s = tab_sb_s(l)
        o_mla_s = _mla_latent_attention(
            q_cat, c_kv_b, k_r_b, cache_mla_ckv.reshape(depth * rows_past, kv_lora), kr_past,
            w_ukv_b, (tab_s[0], tab_s[1], tab_s[3]), l, tq=t_s, tk=tk_s, pos0=past, out_rows=n_s,
            out_block_of=lambda s, qb, kb, fl: (qb[s] - q_off_s, 0))
        o_mla = jnp.concatenate([o_mla_p, o_mla_s], axis=0)

        def merge_epi(accs, ex, outs):
            outs[0][...] = (ex[0][...] * accs[0] + ex[1][...] * accs[1]).astype(BF16)

        merged = _matmul(
            [(o_sb, w_branch_sb, _wspec(l, sb_w, tn, 0), True),
             (o_mla, w_branch_mla, _wspec(l, H_MLA * V_DIM, tn, 0), True)],
            m=n, n_out=d, tm=tm, tn=tn, epilogue=merge_epi,
            extras=[gates, gates],
            extra_specs=[pl.BlockSpec((tm, tn), lambda j, i: (i, j)),
                         pl.BlockSpec((tm, tn), lambda j, i: (i, j + d // tn))],
            out_shape=[jax.ShapeDtypeStruct((n, d), BF16)], out_specs=[_spec2(tm, tn)])[0]

        def resid_epi(accs, ex, outs):
            g0 = pl.program_id(1) * gpt
            for g in range(gpt):
                outs[0][g] = ex[0][g] + _mod_row(ex[1], g0 + g) * accs[0][g * grp:(g + 1) * grp, :]

        x_spec = pl.BlockSpec((gpt, grp, tn), lambda j, i: (i, 0, j))
        x3 = _matmul(
            [(merged, w_out, _wspec(l, d, tn, 0), True)], m=n, n_out=d, tm=tm, tn=tn,
            epilogue=resid_epi, extras=[x3, modg], extra_specs=[x_spec, mspec_rows(l, 2)],
            out_shape=[jax.ShapeDtypeStruct((ng, grp, d), F32)], out_specs=[x_spec])[0]

        h2, route = _norm_route(x3, g_norm_ffn.reshape(depth, 1, d), modg, l, mod_idx(l, 4),
                                mod_idx(l, 3), w_router3, b_router)
        row_tok, row_dst, row_w, tile_expert, n_valid = _dispatch(route.reshape(n, LANES), n)
        yg = _moe_experts(h2, row_tok, row_dst, row_w, tile_expert, n_valid,
                          w_exp_gate, w_exp_up, w_exp_down, l, 2 * n)
        x3 = _combine(x3, yg, modg, l, mod_idx(l, 5))

        new_k.append(k_f32)
        new_v.append(v_f32)
        new_c.append(c_kv)
        new_r.append(k_r[:, :QK_ROPE])

    y_p, y_s = _final_norm(x3, g_final.reshape(1, d), n_p // grp)

    def split(parts, tail):
        a = jnp.stack(parts)
        return (a[:, :n_p].reshape((depth, bp, t_p) + tail),
                a[:, n_p:].reshape((depth, bs, t_s) + tail))

    pk, sk = split(new_k, (H_SB, DH_SB))
    pv, sv = split(new_v, (H_SB, DH_SB))
    pc, sc = split(new_c, (kv_lora,))
    pr, sr = split(new_r, (QK_ROPE,))
    return (y_p.reshape(bp, t_p, d), y_s.reshape(bs, t_s, d), pk, pv, pc, pr, sk, sv, sc, sr)
```

```python
import functools

import numpy as np
import jax
import jax.numpy as jnp
from jax import lax
from jax.experimental import pallas as pl
from jax.experimental.pallas import tpu as pltpu

F32 = jnp.float32
BF16 = jnp.bfloat16

CHUNK = 64
H_SB = 16
DH_SB = 128
H_MLA = 16
QK_NOPE = 128
QK_ROPE = 64
V_DIM = 128
ROPE_THETA = 10000.0
N_GROUPS = 4
EXPERTS_PER_GROUP = 8
N_EXPERTS = N_GROUPS * EXPERTS_PER_GROUP
N_MOD = 6
EPS = 1e-6

LANES = 128
ATTN_TILE = 256
MOE_TILE = 256
HEAD_UNROLL = 4
MLA_UNROLL = 8
MLA_Q_TILE = 512
SLAB_PAD = 4
SB_DEAD = 152.0
VMEM_LIMIT = 56 * 1024 * 1024


def _cparams(n_axes, vmem=VMEM_LIMIT):
    return pltpu.CompilerParams(dimension_semantics=("arbitrary",) * n_axes,
                                vmem_limit_bytes=vmem)


def _row_tile(n, cap=512):
    t = cap
    while n % t:
        t //= 2
    return t


def _token_tile(n, grp, cap=1152):
    return max(t for t in range(grp, cap + 1, grp) if n % t == 0)


def _matmul(pairs, *, m, n_out, tm, tn, epilogue, out_shape, out_specs,
            extras=(), extra_specs=(), prologue=None):
    n_pairs = len(pairs)
    n_ex = len(extras)
    n_outs = len(out_shape)
    pairs = [tuple(p) + (False,) * (5 - len(p)) for p in pairs]
    cast = [p[3] for p in pairs]
    b_nk = [p[4] for p in pairs]

    def kern(*refs):
        a_refs = refs[0:2 * n_pairs:2]
        b_refs = refs[1:2 * n_pairs:2]
        ex = refs[2 * n_pairs:2 * n_pairs + n_ex]
        outs = refs[2 * n_pairs + n_ex:2 * n_pairs + n_ex + n_outs]
        scr = refs[2 * n_pairs + n_ex + n_outs:]
        i = pl.program_id(1)
        accs = []
        si = 0
        for p in range(n_pairs):
            if cast[p]:
                bsc = scr[si]
                si += 1

                @pl.when(i == 0)
                def _(bsc=bsc, b_ref=b_refs[p]):
                    bsc[...] = b_ref[...].astype(BF16)

                bv = bsc[...]
            else:
                bv = b_refs[p][...]
            a = a_refs[p][...]
            if prologue is not None:
                a = prologue(a, ex)
            if b_nk[p]:
                accs.append(_dot_nt(a.astype(BF16), bv))
            else:
                accs.append(jnp.dot(a.astype(BF16), bv, preferred_element_type=F32))
        epilogue(accs, ex, outs)

    in_specs, args, scratch = [], [], []
    for (a, b, b_spec, cb, nk) in pairs:
        k = a.shape[1]
        in_specs += [pl.BlockSpec((tm, k), lambda j, i: (i, 0)), b_spec]
        args += [a, b]
        if cb:
            scratch.append(pltpu.VMEM((tn, k) if nk else (k, tn), BF16))
    in_specs += list(extra_specs)
    args += list(extras)
    return pl.pallas_call(
        kern,
        grid=(n_out // tn, m // tm),
        in_specs=in_specs,
        out_specs=out_specs,
        out_shape=out_shape,
        scratch_shapes=scratch,
        compiler_params=_cparams(2),
    )(*args)


def _wspec(l, k, tn, col_off):
    cb = col_off // tn
    assert cb * tn == col_off
    return pl.BlockSpec((None, k, tn), lambda j, i: (l, 0, cb + j))


def _wspec_nk(l, k, tn, row_off):
    rb = row_off // tn
    assert rb * tn == row_off
    return pl.BlockSpec((None, tn, k), lambda j, i: (l, rb + j, 0))


def _spec2(tm, tn):
    return pl.BlockSpec((tm, tn), lambda j, i: (i, j))


def _hm_spec(tm, tn):
    return pl.BlockSpec((tn // LANES, tm, LANES), lambda j, i: (j, i, 0))


def _store_heads(o_ref, val):
    for c in range(val.shape[1] // LANES):
        o_ref[c] = val[:, c * LANES:(c + 1) * LANES].astype(o_ref.dtype)


def _rms(x, g):
    return x * lax.rsqrt(jnp.mean(x * x, axis=-1, keepdims=True) + EPS) * g


def _mod_row(ref, g):
    return ref[pl.ds(g, 1), :]


def _norm_mod_kernel(x_ref, g_ref, sc_ref, sh_ref, o_ref):
    gb = x_ref.shape[0]
    g0 = pl.program_id(0) * gb
    for g in range(gb):
        y = _rms(x_ref[g], g_ref[...])
        o_ref[g] = (y * (1.0 + _mod_row(sc_ref, g0 + g)) + _mod_row(sh_ref, g0 + g)).astype(o_ref.dtype)


def _split3(x):
    hi = x.astype(BF16)
    r = x - hi.astype(F32)
    mid = r.astype(BF16)
    lo = (r - mid.astype(F32)).astype(BF16)
    return hi, mid, lo


def _dot_f32(a, b3):
    a_hi, a_mid, a_lo = _split3(a)
    b_hi, b_mid, b_lo = b3
    d = functools.partial(jnp.dot, preferred_element_type=F32)
    small = d(a_hi, b_lo) + d(a_lo, b_hi) + d(a_mid, b_mid)
    return (d(a_hi, b_hi) + (d(a_hi, b_mid) + d(a_mid, b_hi))) + small


def _route(logits):
    lane = lax.broadcasted_iota(jnp.int32, logits.shape, 1)
    lanef = lane.astype(F32)
    big = jnp.float32(1e9)
    ninf = jnp.float32(-jnp.inf)
    is_g = (lane >= N_EXPERTS) & (lane < N_EXPERTS + N_GROUPS)
    gl = jnp.where(is_g, logits, ninf)
    gmax = jnp.max(gl, axis=1, keepdims=True)
    g_idx = jnp.min(jnp.where(gl == gmax, lanef - N_EXPERTS, big), axis=1, keepdims=True)
    p_group = 1.0 / jnp.sum(jnp.where(is_g, jnp.exp(gl - gmax), 0.0), axis=1, keepdims=True)
    grp = jnp.floor(lanef * (1.0 / EXPERTS_PER_GROUP))
    in_g = (lane < N_EXPERTS) & (grp == g_idx)
    el = jnp.where(in_g, logits, ninf)
    e1 = jnp.max(el, axis=1, keepdims=True)
    i1 = jnp.min(jnp.where(el == e1, lanef, big), axis=1, keepdims=True)
    el2 = jnp.where(lanef == i1, ninf, el)
    e2 = jnp.max(el2, axis=1, keepdims=True)
    i2 = jnp.min(jnp.where(el2 == e2, lanef, big), axis=1, keepdims=True)
    t = jnp.exp(e2 - e1)
    den = 1.0 + t
    w1 = (1.0 / den) * p_group
    w2 = (t / den) * p_group
    out = jnp.where(lane == 0, i1, jnp.where(lane == 1, i2,
          jnp.where(lane == 2, w1, jnp.where(lane == 3, w2, 0.0))))
    return out


def _slab_pitch(s):
    return s + SLAB_PAD


def _slab_store(ref, row0, val):
    rows, d = val.shape
    s = d // LANES
    p = _slab_pitch(s)
    for c in range(s):
        ref[pl.ds(row0 * p + c, rows, stride=p), :] = val[:, c * LANES:(c + 1) * LANES]


def _slab_load(ref, row0, rows, s, lead=None):
    pieces = []
    p = _slab_pitch(s)
    for c in range(s):
        rs = pl.ds(row0 * p + c, rows, stride=p)
        pieces.append(ref[rs, :] if lead is None else ref[lead, rs, :])
    return jnp.concatenate(pieces, axis=1)


def _norm_route_kernel(x_ref, g_ref, sc_ref, sh_ref, wr_ref, br_ref, h_ref, r_ref):
    gb, grp, _ = x_ref.shape
    g0 = pl.program_id(0) * gb
    b3 = (wr_ref[0], wr_ref[1], wr_ref[2])
    for g in range(gb):
        y = _rms(x_ref[g], g_ref[...])
        h = y * (1.0 + _mod_row(sc_ref, g0 + g)) + _mod_row(sh_ref, g0 + g)
        _slab_store(h_ref, g * grp, h)
        logits = _dot_f32(h, b3) + br_ref[...]
        r_ref[g] = _route(logits)


def _mod_spec(idx, ng, d):
    return pl.BlockSpec((None, ng, d), lambda i: (idx, 0, 0))


def _norm_mod(x3, g, modg, l, sc_i, sh_i):
    ng, grp, d = x3.shape
    gb = _row_tile(ng, 4)
    return pl.pallas_call(
        _norm_mod_kernel,
        grid=(ng // gb,),
        in_specs=[pl.BlockSpec((gb, grp, d), lambda i: (i, 0, 0)),
                  pl.BlockSpec((None, 1, d), lambda i: (l, 0, 0)),
                  _mod_spec(sc_i, ng, d), _mod_spec(sh_i, ng, d)],
        out_specs=pl.BlockSpec((gb, grp, d), lambda i: (i, 0, 0)),
        out_shape=jax.ShapeDtypeStruct(x3.shape, BF16),
        compiler_params=_cparams(1),
    )(x3, g, modg, modg)


def _norm_route(x3, g, modg, l, sc_i, sh_i, wr3, br):
    ng, grp, d = x3.shape
    gb = _row_tile(ng, 4)
    return pl.pallas_call(
        _norm_route_kernel,
        grid=(ng // gb,),
        in_specs=[pl.BlockSpec((gb, grp, d), lambda i: (i, 0, 0)),
                  pl.BlockSpec((None, 1, d), lambda i: (l, 0, 0)),
                  _mod_spec(sc_i, ng, d), _mod_spec(sh_i, ng, d),
                  pl.BlockSpec((None, 3, d, LANES), lambda i: (l, 0, 0, 0)),
                  pl.BlockSpec((None, 1, LANES), lambda i: (l, 0, 0))],
        out_specs=[pl.BlockSpec((gb * grp * _slab_pitch(d // LANES), LANES), lambda i: (i, 0)),
                   pl.BlockSpec((gb, grp, LANES), lambda i: (i, 0, 0))],
        out_shape=[jax.ShapeDtypeStruct((ng * grp * _slab_pitch(d // LANES), LANES), F32),
                   jax.ShapeDtypeStruct((ng, grp, LANES), F32)],
        compiler_params=_cparams(1),
    )(x3, g, modg, modg, wr3, br)


def _final_norm_kernel(x_ref, g_ref, op_ref, os_ref, *, n_prompt_blocks):
    i = pl.program_id(0)
    y = _rms(x_ref[...], g_ref[...])

    @pl.when(i < n_prompt_blocks)
    def _():
        op_ref[...] = y

    @pl.when(i >= n_prompt_blocks)
    def _():
        os_ref[...] = y


def _final_norm(x3, g, ng_prompt):
    ng, grp, d = x3.shape
    gb = _row_tile(np.gcd(ng_prompt, ng - ng_prompt), 4)
    npb = ng_prompt // gb
    blk = lambda f: pl.BlockSpec((gb, grp, d), f)
    return pl.pallas_call(
        functools.partial(_final_norm_kernel, n_prompt_blocks=npb),
        grid=(ng // gb,),
        in_specs=[blk(lambda i: (i, 0, 0)), pl.BlockSpec((1, d), lambda i: (0, 0))],
        out_specs=[blk(lambda i: (jnp.minimum(i, npb - 1), 0, 0)),
                   blk(lambda i: (jnp.maximum(i - npb, 0), 0, 0))],
        out_shape=[jax.ShapeDtypeStruct((ng_prompt, grp, d), F32),
                   jax.ShapeDtypeStruct((ng - ng_prompt, grp, d), F32)],
        compiler_params=_cparams(1),
    )(x3, g)


def _lanes(c, w):
    if w % LANES == 0:
        return c if w == LANES else jnp.tile(c, (1, w // LANES))
    return c[:, :w]


def _dot_nt(a, b):
    return lax.dot_general(a, b, (((1,), (1,)), ((), ())), preferred_element_type=F32)


def _sb_block(qh, kh, vh, c, u, masked):
    w = kh.shape[0]
    z = _dot_nt(qh, kh)
    sp = jnp.maximum(z, 0.0) + jnp.log2(1.0 + jnp.exp2(-jnp.abs(z)))
    if masked:
        row = lax.broadcasted_iota(jnp.int32, z.shape, 0)
        col = lax.broadcasted_iota(jnp.int32, z.shape, 1)
        valid = col < row
        sp = jnp.where(valid, sp, 0.0)
    hi = sp.astype(BF16)
    lo = (sp - hi.astype(F32)).astype(BF16)
    if w % LANES == 0:
        cs = jnp.dot(jnp.concatenate([hi, lo], axis=1), u, preferred_element_type=F32)
    else:
        cs = (jnp.dot(hi, u[:w], preferred_element_type=F32)
              + jnp.dot(lo, u[w:], preferred_element_type=F32))
    wgt = jnp.exp2(z - sp - cs - _lanes(c, w))
    if masked:
        wgt = jnp.where(valid, wgt, 0.0)
    o = jnp.dot(wgt.astype(BF16), vh, preferred_element_type=F32)
    c_new = c + jnp.sum(sp, axis=1, keepdims=True)
    return o, c_new


def _suffix_matrix(w):
    j = np.arange(w)[:, None]
    s = np.arange(w)[None, :]
    u = (j > s).astype(np.float32)
    return jnp.asarray(np.concatenate([u, u], axis=0), dtype=BF16)


def _sb_stream_kernel(q_ref, kn_ref, vn_ref, k_hbm, v_hbm, ud_ref, up_ref, o_ref,
                      kbuf, vbuf, acc, carry, done, alive_ref, sem, *, heads, tk, cache_tiles,
                      cache_base):
    i = pl.program_id(0)
    n_past = i if cache_tiles is None else cache_tiles

    def fetch(j, slot):
        if cache_tiles is None:
            rows = pl.ds(pl.multiple_of(j * tk, tk), tk)
            src_k, src_v = k_hbm.at[:, rows, :], v_hbm.at[:, rows, :]
        else:
            blk = tk * heads
            rows = pl.ds(pl.multiple_of((cache_base + i * cache_tiles + j) * blk, blk), blk)
            src_k, src_v = k_hbm.at[rows], v_hbm.at[rows]
        return (pltpu.make_async_copy(src_k, kbuf.at[slot], sem.at[0, slot]),
                pltpu.make_async_copy(src_v, vbuf.at[slot], sem.at[1, slot]))

    def tile_of(buf, slot, h):
        if cache_tiles is None:
            return buf[slot, h]
        return buf.at[slot][pl.ds(h, tk, stride=heads), :].astype(BF16)

    @pl.when(n_past > 0)
    def _():
        for c in fetch(n_past - 1, 0):
            c.start()

    def diag(h, _):
        o, c = _sb_block(q_ref[h], kn_ref[h], vn_ref[h], jnp.zeros(carry.shape[1:], F32),
                         ud_ref[...], True)
        acc[h] = o
        carry[h] = c
        done[h] = 0
        return 0
    lax.fori_loop(0, heads, diag, 0, unroll=min(heads, HEAD_UNROLL))

    alive_ref[0] = heads

    @pl.when(n_past > 0)
    def _():
        for c in fetch(n_past - 1, 0):
            c.wait()

        @pl.when(n_past > 1)
        def _():
            for c in fetch(n_past - 2, 1):
                c.start()

        def first(h, alive):
            o, c = _sb_block(q_ref[h], tile_of(kbuf, 0, h), tile_of(vbuf, 0, h), carry[h],
                             up_ref[...], False)
            acc[h] = acc[h] + o
            carry[h] = c
            dead = (jnp.min(c) >= SB_DEAD).astype(jnp.int32)
            done[h] = dead
            return alive + 1 - dead
        alive_ref[0] = lax.fori_loop(0, heads, first, 0, unroll=min(heads, HEAD_UNROLL))

    def cond(state):
        j, alive = state
        return (j >= 0) & (alive > 0)

    def body(state):
        j, _ = state
        slot = lax.rem(n_past - 1 - j, 2)
        for c in fetch(j, slot):
            c.wait()

        @pl.when(j > 0)
        def _():
            for c in fetch(j - 1, 1 - slot):
                c.start()

        def head(h, alive):
            @pl.when(done[h] == 0)
            def _():
                o, c = _sb_block(q_ref[h], tile_of(kbuf, slot, h), tile_of(vbuf, slot, h),
                                 carry[h], up_ref[...], False)
                acc[h] = acc[h] + o
                carry[h] = c
                done[h] = (jnp.min(c) >= SB_DEAD).astype(jnp.int32)
            return alive + 1 - done[h]
        return j - 1, lax.fori_loop(0, heads, head, 0)

    j_end, _ = lax.while_loop(cond, body, (jnp.int32(n_past - 2), alive_ref[0]))

    @pl.when(j_end >= 0)
    def _():
        for c in fetch(j_end, lax.rem(n_past - 1 - j_end, 2)):
            c.wait()

    for h in range(heads):
        o_ref[:, h * DH_SB:(h + 1) * DH_SB] = acc[h].astype(o_ref.dtype)


def _sb_stream_attention(q, kn, vn, k_src, v_src, *, tq, tk, q_block0, n_tiles, cache_tiles=None,
                         cache_base=0):
    heads = q.shape[0]
    blk = pl.BlockSpec((heads, tq, DH_SB), lambda i: (0, q_block0 + i, 0))
    anyspec = pl.BlockSpec(memory_space=pl.ANY)
    if cache_tiles is None:
        buf = pltpu.VMEM((2, heads, tk, DH_SB), BF16)
    else:
        buf = pltpu.VMEM((2, tk * heads, DH_SB), F32)
    kern = functools.partial(_sb_stream_kernel, heads=heads, tk=tk, cache_tiles=cache_tiles,
                             cache_base=cache_base)
    return pl.pallas_call(
        kern, grid=(n_tiles,),
        in_specs=[blk, blk, blk, anyspec, anyspec,
                  pl.BlockSpec((2 * tq, tq), lambda i: (0, 0)),
                  pl.BlockSpec((2 * tk, tk), lambda i: (0, 0))],
        out_specs=pl.BlockSpec((tq, heads * DH_SB), lambda i: (i, 0)),
        out_shape=jax.ShapeDtypeStruct((n_tiles * tq, heads * DH_SB), BF16),
        scratch_shapes=[buf, buf,
                        pltpu.VMEM((heads, tq, DH_SB), F32),
                        pltpu.VMEM((heads, tq, LANES), F32),
                        pltpu.SMEM((heads,), jnp.int32),
                        pltpu.SMEM((1,), jnp.int32),
                        pltpu.SemaphoreType.DMA((2, 2))],
        compiler_params=_cparams(1),
    )(q, kn, vn, k_src, v_src, _suffix_matrix(tq), _suffix_matrix(tk))


def _mla_block(qc, kc, va, m, acc, mask):
    s = _dot_nt(qc, kc)
    if mask is not None:
        s = jnp.where(mask, s, -jnp.inf)
    m_new = jnp.maximum(m, jnp.max(s, axis=1, keepdims=True))
    alpha = jnp.exp2(m - m_new)
    p = jnp.exp2(s - _lanes(m_new, s.shape[1]))
    acc_new = _lanes(alpha, acc.shape[1]) * acc + jnp.dot(p.astype(BF16), va,
                                                          preferred_element_type=F32)
    return m_new, acc_new


def _mla_kernel(qb_ref, kb_ref, fl_ref, q_ref, k_ref, v_ref, o_ref, acc, m_sc, *, heads):
    s = pl.program_id(0)
    fl = fl_ref[s]
    tq, tk = q_ref.shape[1], k_ref.shape[1]

    @pl.when((fl & 1) != 0)
    def _():
        m_sc[...] = jnp.full(m_sc.shape, -jnp.inf, F32)
        acc[...] = jnp.zeros(acc.shape, F32)

    def run(mask):
        def body(h, _):
            m, a = _mla_block(q_ref[h], k_ref[h], v_ref[h], m_sc[h], acc[h], mask)
            m_sc[h] = m
            acc[h] = a
            return 0
        lax.fori_loop(0, heads, body, 0, unroll=min(heads, MLA_UNROLL))

    @pl.when((fl & 4) != 0)
    def _():
        row = lax.broadcasted_iota(jnp.int32, (tq, tk), 0) + qb_ref[s] * tq
        col = lax.broadcasted_iota(jnp.int32, (tq, tk), 1) + kb_ref[s] * tk
        run((col // CHUNK) <= (row // CHUNK))

    @pl.when((fl & 4) == 0)
    def _():
        run(None)

    @pl.when((fl & 2) != 0)
    def _():
        for h in range(heads):
            a = acc[h]
            o_ref[:, h * V_DIM:(h + 1) * V_DIM] = (a[:, :V_DIM] / a[:, V_DIM:]).astype(o_ref.dtype)


def _mla_tables(n_rows, tq, tk):
    r = tq // tk
    qb, kb, fl = [], [], []
    for i in range(n_rows // tq):
        tiles = [(i * r + j, 4) for j in range(r)] + [(j, 0) for j in reversed(range(i * r))]
        for idx, (j, f) in enumerate(tiles):
            qb.append(i)
            kb.append(j)
            fl.append(f | (1 if idx == 0 else 0) | (2 if idx == len(tiles) - 1 else 0))
    arr = lambda v: jnp.asarray(np.asarray(v, dtype=np.int32))
    return arr(qb), arr(kb), arr(fl)


def _mla_attention(qc, kc, va, *, n_rows, tq, tk):
    qb, kb, fl = _mla_tables(n_rows, tq, tk)
    h = H_MLA
    wqk = 2 * LANES
    kv_spec = pl.BlockSpec((h, tk, wqk), lambda s, qb, kb, fl: (0, kb[s], 0))
    grid_spec = pltpu.PrefetchScalarGridSpec(
        num_scalar_prefetch=3,
        grid=(qb.shape[0],),
        in_specs=[pl.BlockSpec((h, tq, wqk), lambda s, qb, kb, fl: (0, qb[s], 0)),
                  kv_spec, kv_spec],
        out_specs=pl.BlockSpec((tq, h * V_DIM), lambda s, qb, kb, fl: (qb[s], 0)),
        scratch_shapes=[pltpu.VMEM((h, tq, 2 * V_DIM), F32),
                        pltpu.VMEM((h, tq, LANES), F32)],
    )
    return pl.pallas_call(
        functools.partial(_mla_kernel, heads=h), grid_spec=grid_spec,
        out_shape=jax.ShapeDtypeStruct((n_rows, h * V_DIM), BF16),
        compiler_params=_cparams(1),
    )(qb, kb, fl, qc, kc, va)


def _mla_latent_kernel(qb_ref, kb_ref, fl_ref, q_ref, cn_ref, rn_ref, cp_ref, rp_ref, w_ref,
                       o_ref, qa, qr, acc, m_sc, l_sc, *, heads, pos0):
    s_id = pl.program_id(0)
    fl = fl_ref[s_id]
    is_first = (fl & 1) != 0
    is_last = (fl & 2) != 0
    tq = q_ref.shape[1]
    hw = QK_NOPE + V_DIM

    def step(ck, kr, mask):
        s = _dot_nt(qa[...], ck) + _dot_nt(qr[...], kr)
        if mask is not None:
            s = jnp.where(mask, s, -jnp.inf)
        m_old = m_sc[...]
        m_new = jnp.maximum(m_old, jnp.max(s, axis=1, keepdims=True))
        alpha = jnp.exp2(m_old - m_new)
        p = jnp.exp2(s - m_new[:, :1])
        l_sc[...] = alpha * l_sc[...] + jnp.sum(p, axis=1, keepdims=True)
        acc[...] = alpha[:, :1] * acc[...] + jnp.dot(p.astype(BF16), ck,
                                                     preferred_element_type=F32)
        m_sc[...] = m_new

    @pl.when(is_first)
    def _():
        for h in range(heads):
            qh = q_ref[h]
            w_uk = w_ref[:, h * hw:h * hw + QK_NOPE]
            qa[h * tq:(h + 1) * tq, :] = _dot_nt(qh[:, :QK_NOPE], w_uk).astype(BF16)
            qr[h * tq:(h + 1) * tq, :] = qh[:, QK_NOPE:]
        m_sc[...] = jnp.full(m_sc.shape, -jnp.inf, F32)
        l_sc[...] = jnp.zeros(l_sc.shape, F32)
        acc[...] = jnp.zeros(acc.shape, F32)
        row = lax.broadcasted_iota(jnp.int32, (heads * tq, tq), 0) % tq + pos0
        col = lax.broadcasted_iota(jnp.int32, (heads * tq, tq), 1) + pos0
        step(cn_ref[...], rn_ref[...], (col // CHUNK) <= (row // CHUNK))

    @pl.when(jnp.logical_not(is_first))
    def _():
        step(cp_ref[...].astype(BF16), rp_ref[...], None)

    @pl.when(is_last)
    def _():
        o_lat = (acc[...] / l_sc[...][:, :1]).astype(BF16)
        for h in range(heads):
            w_uv = w_ref[:, h * hw + QK_NOPE:(h + 1) * hw]
            o_ref[:, h * V_DIM:(h + 1) * V_DIM] = jnp.dot(
                o_lat[h * tq:(h + 1) * tq, :], w_uv, preferred_element_type=F32).astype(o_ref.dtype)


def _mla_latent_attention(qc, c_new, r_new, c_past, r_past, w_ukv_b, tables, l, *, tq, tk, pos0,
                          out_rows, out_block_of):
    qb, kb, fl = tables
    h = H_MLA
    kvl = c_new.shape[1]
    wqk = 2 * LANES
    grid_spec = pltpu.PrefetchScalarGridSpec(
        num_scalar_prefetch=3,
        grid=(qb.shape[0],),
        in_specs=[pl.BlockSpec((h, tq, wqk), lambda s, qb, kb, fl: (0, qb[s], 0)),
                  pl.BlockSpec((tq, kvl), lambda s, qb, kb, fl: (qb[s], 0)),
                  pl.BlockSpec((tq, LANES), lambda s, qb, kb, fl: (qb[s], 0)),
                  pl.BlockSpec((tk, kvl), lambda s, qb, kb, fl: (kb[s], 0)),
                  pl.BlockSpec((tk, LANES), lambda s, qb, kb, fl: (kb[s], 0)),
                  pl.BlockSpec((None, kvl, w_ukv_b.shape[2]), lambda s, qb, kb, fl: (l, 0, 0))],
        out_specs=pl.BlockSpec((tq, h * V_DIM), out_block_of),
        scratch_shapes=[pltpu.VMEM((h * tq, kvl), BF16),
                        pltpu.VMEM((h * tq, LANES), BF16),
                        pltpu.VMEM((h * tq, kvl), F32),
                        pltpu.VMEM((h * tq, LANES), F32),
                        pltpu.VMEM((h * tq, LANES), F32)],
    )
    kern = functools.partial(_mla_latent_kernel, heads=h, pos0=pos0)
    return pl.pallas_call(
        kern, grid_spec=grid_spec,
        out_shape=jax.ShapeDtypeStruct((out_rows, h * V_DIM), BF16),
        compiler_params=_cparams(1),
    )(qb, kb, fl, qc, c_new, r_new, c_past, r_past, w_ukv_b)


def _causal_tables(nq, q_off, per_head=False, heads=1, n_batch=1, past_tiles=None,
                   past_stride=0):
    qb, kb, hb, fl = [], [], [], []
    if past_tiles is None:
        for i in range(nq):
            n = i + 1
            for j in range(n):
                qb.append(q_off + i)
                kb.append(max(i - j, 1) - 1 if j == 0 else i - j)
                hb.append(0)
                fl.append((1 if j == 0 else 0) | (2 if j == n - 1 else 0))
    else:
        for b in range(n_batch):
            for h in range(heads if per_head else 1):
                n = 1 + past_tiles
                for j in range(n):
                    qb.append(q_off + b)
                    jj = past_tiles - 1 if j == 0 else past_tiles - j
                    kb.append(past_stride + b * past_tiles + jj)
                    hb.append(h)
                    fl.append((1 if j == 0 else 0) | (2 if j == n - 1 else 0))
    arr = lambda v: jnp.asarray(np.asarray(v, dtype=np.int32))
    return arr(qb), arr(kb), arr(hb), arr(fl)


ROW_DMA_UNROLL = 8


def _moe_kernel(te_ref, nv_ref, tok_ref, dst_ref, h_hbm, w_ref, wg_ref, wu_ref, wd_ref, y_hbm,
                xbuf, obuf, wg_s, wu_s, wd_s, sem_in, sem_out, *, s):
    t = pl.program_id(0)
    tm = w_ref.shape[0]
    nv = nv_ref[0]
    live = t < nv
    slot = lax.rem(t, 2)

    p = _slab_pitch(s)

    def in_copy(tile, r, sl):
        src0 = pl.multiple_of(tok_ref[tile * tm + r] * p, SLAB_PAD)
        return pltpu.make_async_copy(h_hbm.at[pl.ds(src0, s)],
                                     xbuf.at[sl, pl.ds(pl.multiple_of(r * p, SLAB_PAD), s)],
                                     sem_in.at[sl])

    def gather_start(tile, sl):
        def body(r, _):
            in_copy(tile, r, sl).start()
            return 0
        lax.fori_loop(0, tm, body, 0, unroll=ROW_DMA_UNROLL)

    def gather_wait(tile, sl):
        def body(r, _):
            in_copy(tile, r, sl).wait()
            return 0
        lax.fori_loop(0, tm, body, 0, unroll=ROW_DMA_UNROLL)

    @pl.when(live & (t == 0))
    def _():
        gather_start(0, 0)

    @pl.when(t + 1 < nv)
    def _():
        gather_start(t + 1, 1 - slot)

    prev = te_ref[jnp.maximum(t - 1, 0)]
    fresh = (t == 0) | (te_ref[t] != prev)

    @pl.when(live & fresh)
    def _():
        wg_s[...] = wg_ref[...].astype(BF16)
        wu_s[...] = wu_ref[...].astype(BF16)
        wd_s[...] = wd_ref[...].astype(BF16)

    @pl.when(live)
    def _():
        gather_wait(t, slot)
        x = _slab_load(xbuf, 0, tm, s, lead=slot).astype(BF16)
        a = jnp.dot(x, wg_s[...], preferred_element_type=F32)
        u = jnp.dot(x, wu_s[...], preferred_element_type=F32)
        hid = (a * jax.nn.sigmoid(a)) * u * w_ref[...]
        y = jnp.dot(hid.astype(BF16), wd_s[...], preferred_element_type=F32)

        def out_copy(tile, r):
            dst0 = pl.multiple_of(dst_ref[tile * tm + r] * p, SLAB_PAD)
            return pltpu.make_async_copy(obuf.at[pl.ds(pl.multiple_of(r * p, SLAB_PAD), s)],
                                         y_hbm.at[pl.ds(dst0, s)], sem_out)

        def scatter_wait(tile):
            def wait(r, _):
                out_copy(tile, r).wait()
                return 0
            lax.fori_loop(0, tm, wait, 0, unroll=ROW_DMA_UNROLL)

        @pl.when(t > 0)
        def _():
            scatter_wait(t - 1)

        _slab_store(obuf, 0, y)

        def start(r, _):
            out_copy(t, r).start()
            return 0
        lax.fori_loop(0, tm, start, 0, unroll=ROW_DMA_UNROLL)

        @pl.when(t == nv - 1)
        def _():
            scatter_wait(t)


def _moe_experts(h_slab, row_tok, row_dst, row_w, tile_expert, n_valid, w_gate, w_up, w_down, l,
                 n_dest):
    d, f = w_gate.shape[-2:]
    s = d // LANES
    p = _slab_pitch(s)
    tm = MOE_TILE
    n_tiles = row_tok.shape[0] // tm
    wmap = lambda t, te, nv, tok, dst: (l, te[t], 0, 0)
    grid_spec = pltpu.PrefetchScalarGridSpec(
        num_scalar_prefetch=4, grid=(n_tiles,),
        in_specs=[pl.BlockSpec(memory_space=pl.ANY),
                  pl.BlockSpec((tm, 1), lambda t, te, nv, tok, dst: (t, 0)),
                  pl.BlockSpec((None, None, d, f), wmap),
                  pl.BlockSpec((None, None, d, f), wmap),
                  pl.BlockSpec((None, None, f, d), wmap)],
        out_specs=pl.BlockSpec(memory_space=pl.ANY),
        scratch_shapes=[pltpu.VMEM((2, tm * p, LANES), F32), pltpu.VMEM((tm * p, LANES), F32),
                        pltpu.VMEM((d, f), BF16), pltpu.VMEM((d, f), BF16),
                        pltpu.VMEM((f, d), BF16),
                        pltpu.SemaphoreType.DMA((2,)), pltpu.SemaphoreType.DMA(())],
    )
    return pl.pallas_call(
        functools.partial(_moe_kernel, s=s), grid_spec=grid_spec,
        out_shape=jax.ShapeDtypeStruct(((n_dest + tm) * p, LANES), F32),
        compiler_params=_cparams(1),
    )(tile_expert, n_valid, row_tok, row_dst, h_slab, row_w, w_gate, w_up, w_down)


def _combine_kernel(x_ref, y0_ref, y1_ref, g_ref, o_ref):
    gb, grp, d = x_ref.shape
    s = d // LANES
    g0 = pl.program_id(0) * gb
    for g in range(gb):
        y = _slab_load(y0_ref, g * grp, grp, s) + _slab_load(y1_ref, g * grp, grp, s)
        o_ref[g] = x_ref[g] + _mod_row(g_ref, g0 + g) * y


def _combine(x3, yg, modg, l, g_i):
    ng, grp, d = x3.shape
    s = d // LANES
    gb = _row_tile(ng, 2)
    blk = pl.BlockSpec((gb, grp, d), lambda i: (i, 0, 0))
    yblk = lambda off: pl.BlockSpec((gb * grp * _slab_pitch(s), LANES), lambda i: (i + off, 0))
    return pl.pallas_call(
        _combine_kernel,
        grid=(ng // gb,),
        in_specs=[blk, yblk(0), yblk(ng // gb), _mod_spec(g_i, ng, d)],
        out_specs=blk,
        out_shape=jax.ShapeDtypeStruct(x3.shape, F32),
        compiler_params=_cparams(1),
    )(x3, yg, yg, modg)


def _dispatch(route, n):
    tm = MOE_TILE
    r_cap = (2 * n + N_EXPERTS * (tm - 1) + tm - 1) // tm * tm
    e = route[:, :2].astype(jnp.int32)
    w = route[:, 2:4]
    flat_e = e.T.reshape(-1)
    flat_w = w.T.reshape(-1)
    order = jnp.argsort(flat_e, stable=True).astype(jnp.int32)
    bounds = jnp.searchsorted(flat_e[order], jnp.arange(N_EXPERTS + 1, dtype=jnp.int32),
                              side="left").astype(jnp.int32)
    counts = bounds[1:] - bounds[:-1]
    padded = (counts + tm - 1) // tm * tm
    ends_p = jnp.cumsum(padded)
    starts_p = ends_p - padded
    starts = jnp.cumsum(counts) - counts
    tile_start = jnp.arange(r_cap // tm, dtype=jnp.int32) * tm
    tile_expert = jnp.minimum(jnp.searchsorted(ends_p, tile_start, side="right"),
                              N_EXPERTS - 1).astype(jnp.int32)
    n_valid = (ends_p[-1:] // tm).astype(jnp.int32)
    per_row = lambda v: jnp.broadcast_to(v[:, None], (r_cap // tm, tm)).reshape(r_cap)
    local = jnp.arange(r_cap, dtype=jnp.int32) - per_row(starts_p[tile_expert])
    valid = local < per_row(counts[tile_expert])
    slot = order[jnp.clip(per_row(starts[tile_expert]) + local, 0, 2 * n - 1)]
    row_dst = jnp.where(valid, slot, 2 * n + jnp.arange(r_cap, dtype=jnp.int32) % tm)
    row_tok = jnp.where(valid, jnp.where(slot >= n, slot - n, slot), 0)
    row_w = jnp.where(valid, flat_w[slot], 0.0)
    return row_tok, row_dst, row_w.reshape(r_cap, 1), tile_expert, n_valid


def _gates_kernel(h_ref, wa_ref, wb_ref, o_ref, bsc):
    @pl.when(pl.program_id(1) == 0)
    def _():
        r = wb_ref.shape[0]
        w = wa_ref.shape[0]
        bsc[:w - r, :] = wa_ref[r:, :].astype(BF16)
        bsc[w - r:, :] = wb_ref[...].astype(BF16)

    o_ref[...] = jax.nn.sigmoid(_dot_nt(h_ref[...], bsc[...]))


def _gates(h, w_nk, l, off_kr, n_gate, tm, tn):
    n, k = h.shape
    assert off_kr % tn == 0 and n_gate % tn == 0 and tn % QK_ROPE == 0
    rb = off_kr // tn
    return pl.pallas_call(
        _gates_kernel,
        grid=(n_gate // tn, n // tm),
        in_specs=[pl.BlockSpec((tm, k), lambda j, i: (i, 0)),
                  pl.BlockSpec((None, tn, k), lambda j, i: (l, rb + j, 0)),
                  pl.BlockSpec((None, QK_ROPE, k),
                               lambda j, i: (l, (rb + j + 1) * (tn // QK_ROPE), 0))],
        out_specs=pl.BlockSpec((tm, tn), lambda j, i: (i, j)),
        out_shape=jax.ShapeDtypeStruct((n, n_gate), F32),
        scratch_shapes=[pltpu.VMEM((tn, k), BF16)],
        compiler_params=_cparams(2),
    )(h, w_nk, w_nk)


def _rope_weights_kernel(a_ref, o_ref):
    a = a_ref[...].astype(BF16)
    q = a.shape[0] // 2
    o_ref[...] = jnp.concatenate([a, a[q:], a[:q]], axis=0)


def _rope_key_weights(w_nk, off_kr):
    depth, n_in, k = w_nk.shape
    assert QK_ROPE * 2 == LANES and off_kr % QK_ROPE == 0
    return pl.pallas_call(
        _rope_weights_kernel,
        grid=(depth,),
        in_specs=[pl.BlockSpec((None, QK_ROPE, k), lambda l: (l, off_kr // QK_ROPE, 0))],
        out_specs=pl.BlockSpec((None, 2 * QK_ROPE, k), lambda l: (l, 0, 0)),
        out_shape=jax.ShapeDtypeStruct((depth, 2 * QK_ROPE, k), BF16),
        compiler_params=_cparams(1),
    )(w_nk)


def _rope_tables(pos):
    inv = ROPE_THETA ** (-jnp.arange(0, QK_ROPE, 2, dtype=F32) / QK_ROPE)
    ang = pos.astype(F32)[:, None] * inv[None, :]
    cos, sin = jnp.cos(ang), jnp.sin(ang)
    pad = jnp.zeros((pos.shape[0], LANES - QK_ROPE), F32)
    return (jnp.concatenate([cos, cos, pad], axis=1),
            jnp.concatenate([-sin, sin, pad], axis=1))


def _rope_lanes(acc, c, s):
    return acc * c + pltpu.roll(acc, LANES - QK_ROPE, axis=1) * s


def _swap_halves(w):
    half = w.shape[-1] // 2
    return jnp.concatenate([w[..., half:], w[..., :half]], axis=-1)


def kernel(x_prompt, x_sample, c_prompt, c_sample, cache_sb_k, cache_sb_v, cache_mla_ckv,
           cache_mla_krope, w_ada, b_ada, g_norm_mix, g_norm_ffn, w_in, g_q_lat, g_kv_lat,
           w_uq, w_ukv, w_branch_sb, w_branch_mla, w_out, w_router_group, b_router_group,
           w_router_expert, b_router_expert, w_exp_gate, w_exp_up, w_exp_down, g_final):
    bp, t_p, d = x_prompt.shape
    bs, t_s, _ = x_sample.shape
    depth = w_in.shape[0]
    past = cache_sb_k.shape[2]
    grp = t_s
    n_p, n_s = bp * t_p, bs * t_s
    n = n_p + n_s
    ng = n // grp
    sb_w = H_SB * DH_SB
    q_lora = g_q_lat.shape[1]
    kv_lora = g_kv_lat.shape[1]
    tm = _token_tile(n, grp)
    gpt = tm // grp
    tn = 512
    assert bp == 1 and t_p % ATTN_TILE == 0 and t_p % grp == 0 and past % 512 == 0

    x3 = jnp.concatenate([x_prompt.reshape(n_p // grp, grp, d), x_sample], axis=0)

    n_c = bp + bs
    c_rows = 16
    c_all = jnp.zeros((c_rows, d), F32).at[:n_c].set(jnp.concatenate([c_prompt, c_sample], 0))
    n_modc = N_MOD * d

    def ada_epi(accs, ex, outs):
        outs[0][...] = accs[0] + ex[0][...]

    mods = []
    for l in range(depth):
        mods.append(_matmul(
            [(c_all, w_ada, _wspec(l, d, tn, 0), True)], m=c_rows, n_out=n_modc, tm=c_rows, tn=tn,
            prologue=lambda a, ex: a * jax.nn.sigmoid(a),
            epilogue=ada_epi,
            extras=[b_ada.reshape(depth, 1, n_modc)],
            extra_specs=[pl.BlockSpec((None, 1, tn), lambda j, i, l=l: (l, 0, j))],
            out_shape=[jax.ShapeDtypeStruct((c_rows, n_modc), F32)],
            out_specs=[_spec2(c_rows, tn)])[0])
    mod = jnp.stack(mods)
    modg = jnp.concatenate(
        [jnp.broadcast_to(mod[:, :bp], (depth, n_p // grp, n_modc)), mod[:, bp:n_c]], axis=1)
    modg = modg.reshape(depth, ng, N_MOD, d).transpose(0, 2, 1, 3)
    modg = modg.reshape(depth * N_MOD, ng, d)

    def mod_idx(l, k):
        return l * N_MOD + k

    def mspec_rows(l, k):
        return pl.BlockSpec((None, ng, tn), lambda j, i: (mod_idx(l, k), 0, j))

    pos = jnp.concatenate([jnp.arange(t_p, dtype=jnp.int32),
                           jnp.tile(past + jnp.arange(t_s, dtype=jnp.int32), bs)])
    rope_c, rope_s = _rope_tables(pos)
    tq = ATTN_TILE
    tk_s = 512
    pt = past // tk_s
    tab_sb_s = lambda l: _causal_tables(0, n_p // t_s, n_batch=bs, past_tiles=pt,
                                        past_stride=l * bs * pt)

    off_q, off_k, off_v = 0, sb_w, 2 * sb_w
    off_cq = 3 * sb_w
    off_ckv = off_cq + q_lora
    off_kr = off_ckv + kv_lora
    off_g = off_kr + QK_ROPE
    assert off_g == off_kr + QK_ROPE
    w_nk = jnp.swapaxes(w_in, 1, 2)
    w_kr_aug = _rope_key_weights(w_nk, off_kr)
    uq = w_uq.reshape(depth, q_lora, H_MLA, QK_NOPE + QK_ROPE)
    uq_r = uq[..., QK_NOPE:]
    w_uq_cat = jnp.concatenate([uq[..., :QK_NOPE], uq_r, _swap_halves(uq_r)], axis=-1)
    w_uq_cat = w_uq_cat.reshape(depth, q_lora, H_MLA * 2 * LANES)
    w_router = jnp.concatenate(
        [w_router_expert, w_router_group,
         jnp.zeros((depth, d, LANES - N_EXPERTS - N_GROUPS), F32)], axis=-1)
    r_hi = w_router.astype(BF16)
    r_res = w_router - r_hi.astype(F32)
    r_mid = r_res.astype(BF16)
    r_lo = (r_res - r_mid.astype(F32)).astype(BF16)
    w_router3 = jnp.stack([r_hi, r_mid, r_lo], axis=1)
    b_router = jnp.concatenate(
        [b_router_expert, b_router_group,
         jnp.zeros((depth, LANES - N_EXPERTS - N_GROUPS), F32)], axis=-1).reshape(depth, 1, LANES)

    rows_past = bs * past
    kr_past = jnp.pad(cache_mla_krope.reshape(depth * rows_past, QK_ROPE),
                      ((0, 0), (0, LANES - QK_ROPE))).astype(BF16)
    w_ukv_b = w_ukv.astype(BF16)
    hm = lambda rows: jax.ShapeDtypeStruct((H_SB, rows, LANES), BF16)
    new_k, new_v, new_c, new_r = [], [], [], []

    for l in range(depth):
        h = _norm_mod(x3, g_norm_mix.reshape(depth, 1, d), modg, l, mod_idx(l, 1), mod_idx(l, 0))
        h = h.reshape(n, d)

        sbq_scale = DH_SB ** -0.5 * float(np.log2(np.e))

        def plain_hm(accs, ex, outs):
            _store_heads(outs[0], accs[0] * sbq_scale)

        def f32_and_hm(accs, ex, outs):
            outs[0][...] = accs[0]
            _store_heads(outs[1], accs[0])

        sb_q = _matmul([(h, w_nk, _wspec_nk(l, d, tn, off_q), True, True)], m=n, n_out=sb_w, tm=tm, tn=tn,
                       epilogue=plain_hm, out_shape=[hm(n)], out_specs=[_hm_spec(tm, tn)])[0]
        k_f32, sb_k = _matmul([(h, w_nk, _wspec_nk(l, d, tn, off_k), True, True)], m=n, n_out=sb_w, tm=tm,
                              tn=tn, epilogue=f32_and_hm,
                              out_shape=[jax.ShapeDtypeStruct((n, sb_w), F32), hm(n)],
                              out_specs=[_spec2(tm, tn), _hm_spec(tm, tn)])
        v_f32, sb_v = _matmul([(h, w_nk, _wspec_nk(l, d, tn, off_v), True, True)], m=n, n_out=sb_w, tm=tm,
                              tn=tn, epilogue=f32_and_hm,
                              out_shape=[jax.ShapeDtypeStruct((n, sb_w), F32), hm(n)],
                              out_specs=[_spec2(tm, tn), _hm_spec(tm, tn)])

        def plain_f32(accs, ex, outs):
            outs[0][...] = accs[0]

        c_q = _matmul([(h, w_nk, _wspec_nk(l, d, tn, off_cq), True, True)], m=n, n_out=q_lora, tm=tm, tn=tn,
                      epilogue=plain_f32, out_shape=[jax.ShapeDtypeStruct((n, q_lora), F32)],
                      out_specs=[_spec2(tm, tn)])[0]

        def ckv_epi(accs, ex, outs):
            y = _rms(accs[0], ex[0][...])
            outs[0][...] = y
            outs[1][...] = y.astype(BF16)

        c_kv, c_kv_b = _matmul(
            [(h, w_nk, _wspec_nk(l, d, kv_lora, off_ckv), True, True)], m=n, n_out=kv_lora, tm=tm, tn=kv_lora,
            epilogue=ckv_epi, extras=[g_kv_lat.reshape(depth, 1, kv_lora)],
            extra_specs=[pl.BlockSpec((None, 1, kv_lora), lambda j, i: (l, 0, 0))],
            out_shape=[jax.ShapeDtypeStruct((n, kv_lora), F32),
                       jax.ShapeDtypeStruct((n, kv_lora), BF16)],
            out_specs=[_spec2(tm, kv_lora), _spec2(tm, kv_lora)])

        def kr_epi(accs, ex, outs):
            r = _rope_lanes(accs[0], ex[0][...], ex[1][...])
            outs[0][...] = r
            outs[1][...] = r.astype(BF16)

        rope_specs = [pl.BlockSpec((tm, LANES), lambda j, i: (i, 0))] * 2
        k_r, k_r_b = _matmul(
            [(h, w_kr_aug, pl.BlockSpec((None, LANES, d), lambda j, i: (l, 0, 0)), False, True)],
            m=n, n_out=LANES, tm=tm, tn=LANES, epilogue=kr_epi,
            extras=[rope_c, rope_s], extra_specs=rope_specs,
            out_shape=[jax.ShapeDtypeStruct((n, LANES), F32), jax.ShapeDtypeStruct((n, LANES), BF16)],
            out_specs=[_spec2(tm, LANES), _spec2(tm, LANES)])

        gates = _gates(h, w_nk, l, off_kr, 2 * d, tm, tn)

        def cq_prologue(a, ex):
            return _rms(a, ex[0][...])

        gq_spec = pl.BlockSpec((None, 1, q_lora), lambda j, i: (l, 0, 0))
        gq = g_q_lat.reshape(depth, 1, q_lora)
        wqk = 2 * LANES
        hm_qk = lambda rows: jax.ShapeDtypeStruct((H_MLA, rows, wqk), BF16)

        q_scale = (QK_NOPE + QK_ROPE) ** -0.5 * float(np.log2(np.e))

        def qcat_epi(accs, ex, outs):
            c, s = ex[1][...], ex[2][...]
            for hh in range(tn // wqk):
                blk = accs[0][:, hh * wqk:(hh + 1) * wqk]
                outs[0][hh, :, :LANES] = (blk[:, :LANES] * q_scale).astype(BF16)
                outs[0][hh, :, LANES:] = (_rope_lanes(blk[:, LANES:], c, s) * q_scale).astype(BF16)

        q_cat = _matmul(
            [(c_q, w_uq_cat, pl.BlockSpec((None, q_lora, tn), lambda j, i: (l, 0, j)), True)],
            m=n, n_out=H_MLA * wqk, tm=tm, tn=tn, prologue=cq_prologue, epilogue=qcat_epi,
            extras=[gq, rope_c, rope_s], extra_specs=[gq_spec] + rope_specs,
            out_shape=[hm_qk(n)],
            out_specs=[pl.BlockSpec((tn // wqk, tm, wqk), lambda j, i: (j, i, 0))])[0]

        tn_kv = min(2048, H_MLA * wqk)
        hpt = tn_kv // wqk

        def kv_epi(accs, ex, outs):
            kr = ex[0][...]
            ones = jnp.ones(kr.shape, BF16)
            for hh in range(hpt):
                outs[0][hh, :, :LANES] = accs[0][:, hh * wqk:hh * wqk + LANES].astype(BF16)
                outs[0][hh, :, LANES:] = kr
                outs[1][hh, :, :LANES] = accs[0][:, hh * wqk + LANES:(hh + 1) * wqk].astype(BF16)
                outs[1][hh, :, LANES:] = ones

        def up_kv(a, kr, rows, tmr):
            hspec = pl.BlockSpec((hpt, tmr, wqk), lambda j, i: (j, i, 0))
            return _matmul(
                [(a, w_ukv, pl.BlockSpec((None, kv_lora, tn_kv), lambda j, i: (l, 0, j)), True)],
                m=rows, n_out=H_MLA * wqk, tm=tmr, tn=tn_kv, epilogue=kv_epi, extras=[kr],
                extra_specs=[pl.BlockSpec((tmr, LANES), lambda j, i: (i, 0))],
                out_shape=[hm_qk(rows), hm_qk(rows)], out_specs=[hspec, hspec])

        kc_new, v_new = up_kv(c_kv_b, k_r_b, n, tm)

        o_sb_p = _sb_stream_attention(sb_q, sb_k, sb_v, sb_k, sb_v, tq=tq, tk=tq, q_block0=0,
                                      n_tiles=n_p // tq)
        cache_k2 = cache_sb_k.reshape(depth * bs * past * H_SB, DH_SB)
        cache_v2 = cache_sb_v.reshape(depth * bs * past * H_SB, DH_SB)
        q_off_s = n_p // t_s
        o_sb_s = _sb_stream_attention(sb_q, sb_k, sb_v, cache_k2, cache_v2, tq=t_s, tk=ATTN_TILE,
                                      q_block0=q_off_s, n_tiles=bs,
                                      cache_tiles=past // ATTN_TILE,
                                      cache_base=l * bs * (past // ATTN_TILE))
        o_sb = jnp.concatenate([o_sb_p, o_sb_s], axis=0)

        o_mla_p = _mla_attention(q_cat, kc_new, v_new, n_rows=n_p,
                                 tq=min(MLA_Q_TILE, n_p), tk=ATTN_TILE)
        tab_s = tab_sb_s(l)
        o_mla_s = _mla_latent_attention(
            q_cat, c_kv_b, k_r_b, cache_mla_ckv.reshape(depth * rows_past, kv_lora), kr_past,
            w_ukv_b, (tab_s[0], tab_s[1], tab_s[3]), l, tq=t_s, tk=tk_s, pos0=past, out_rows=n_s,
            out_block_of=lambda s, qb, kb, fl: (qb[s] - q_off_s, 0))
        o_mla = jnp.concatenate([o_mla_p, o_mla_s], axis=0)

        def merge_epi(accs, ex, outs):
            outs[0][...] = (ex[0][...] * accs[0] + ex[1][...] * accs[1]).astype(BF16)

        merged = _matmul(
            [(o_sb, w_branch_sb, _wspec(l, sb_w, tn, 0), True),
             (o_mla, w_branch_mla, _wspec(l, H_MLA * V_DIM, tn, 0), True)],
            m=n, n_out=d, tm=tm, tn=tn, epilogue=merge_epi,
            extras=[gates, gates],
            extra_specs=[pl.BlockSpec((tm, tn), lambda j, i: (i, j)),
                         pl.BlockSpec((tm, tn), lambda j, i: (i, j + d // tn))],
            out_shape=[jax.ShapeDtypeStruct((n, d), BF16)], out_specs=[_spec2(tm, tn)])[0]

        def resid_epi(accs, ex, outs):
            g0 = pl.program_id(1) * gpt
            for g in range(gpt):
                outs[0][g] = ex[0][g] + _mod_row(ex[1], g0 + g) * accs[0][g * grp:(g + 1) * grp, :]

        x_spec = pl.BlockSpec((gpt, grp, tn), lambda j, i: (i, 0, j))
        x3 = _matmul(
            [(merged, w_out, _wspec(l, d, tn, 0), True)], m=n, n_out=d, tm=tm, tn=tn,
            epilogue=resid_epi, extras=[x3, modg], extra_specs=[x_spec, mspec_rows(l, 2)],
            out_shape=[jax.ShapeDtypeStruct((ng, grp, d), F32)], out_specs=[x_spec])[0]

        h2, route = _norm_route(x3, g_norm_ffn.reshape(depth, 1, d), modg, l, mod_idx(l, 4),
                                mod_idx(l, 3), w_router3, b_router)
        row_tok, row_dst, row_w, tile_expert, n_valid = _dispatch(route.reshape(n, LANES), n)
        yg = _moe_experts(h2, row_tok, row_dst, row_w, tile_expert, n_valid,
                          w_exp_gate, w_exp_up, w_exp_down, l, 2 * n)
        x3 = _combine(x3, yg, modg, l, mod_idx(l, 5))

        new_k.append(k_f32)
        new_v.append(v_f32)
        new_c.append(c_kv)
        new_r.append(k_r[:, :QK_ROPE])

    y_p, y_s = _final_norm(x3, g_final.reshape(1, d), n_p // grp)

    def split(parts, tail):
        a = jnp.stack(parts)
        return (a[:, :n_p].reshape((depth, bp, t_p) + tail),
                a[:, n_p:].reshape((depth, bs, t_s) + tail))

    pk, sk = split(new_k, (H_SB, DH_SB))
    pv, sv = split(new_v, (H_SB, DH_SB))
    pc, sc = split(new_c, (kv_lora,))
    pr, sr = split(new_r, (QK_ROPE,))
    return (y_p.reshape(bp, t_p, d), y_s.reshape(bs, t_s, d), pk, pv, pc, pr, sk, sv, sc, sr)
```

```python
import functools

import numpy as np
import jax
import jax.numpy as jnp
from jax import lax
from jax.experimental import pallas as pl
from jax.experimental.pallas import tpu as pltpu

F32 = jnp.float32
BF16 = jnp.bfloat16

CHUNK = 64
H_SB = 16
DH_SB = 128
H_MLA = 16
QK_NOPE = 128
QK_ROPE = 64
V_DIM = 128
ROPE_THETA = 10000.0
N_GROUPS = 4
EXPERTS_PER_GROUP = 8
N_EXPERTS = N_GROUPS * EXPERTS_PER_GROUP
N_MOD = 6
EPS = 1e-6

LANES = 128
ATTN_TILE = 256
MOE_TILE = 256
HEAD_UNROLL = 4
MLA_UNROLL = 8
MLA_Q_TILE = 512
SLAB_PAD = 4
SB_DEAD = 152.0
VMEM_LIMIT = 56 * 1024 * 1024


def _cparams(n_axes, vmem=VMEM_LIMIT):
    return pltpu.CompilerParams(dimension_semantics=("arbitrary",) * n_axes,
                                vmem_limit_bytes=vmem)


def _row_tile(n, cap=512):
    t = cap
    while n % t:
        t //= 2
    return t


def _token_tile(n, grp, cap=1152):
    return max(t for t in range(grp, cap + 1, grp) if n % t == 0)


def _matmul(pairs, *, m, n_out, tm, tn, epilogue, out_shape, out_specs,
            extras=(), extra_specs=(), prologue=None):
    n_pairs = len(pairs)
    n_ex = len(extras)
    n_outs = len(out_shape)
    pairs = [tuple(p) + (False,) * (5 - len(p)) for p in pairs]
    cast = [p[3] for p in pairs]
    b_nk = [p[4] for p in pairs]

    def kern(*refs):
        a_refs = refs[0:2 * n_pairs:2]
        b_refs = refs[1:2 * n_pairs:2]
        ex = refs[2 * n_pairs:2 * n_pairs + n_ex]
        outs = refs[2 * n_pairs + n_ex:2 * n_pairs + n_ex + n_outs]
        scr = refs[2 * n_pairs + n_ex + n_outs:]
        i = pl.program_id(1)
        accs = []
        si = 0
        for p in range(n_pairs):
            if cast[p]:
                bsc = scr[si]
                si += 1

                @pl.when(i == 0)
                def _(bsc=bsc, b_ref=b_refs[p]):
                    bsc[...] = b_ref[...].astype(BF16)

                bv = bsc[...]
            else:
                bv = b_refs[p][...]
            a = a_refs[p][...]
            if prologue is not None:
                a = prologue(a, ex)
            if b_nk[p]:
                accs.append(_dot_nt(a.astype(BF16), bv))
            else:
                accs.append(jnp.dot(a.astype(BF16), bv, preferred_element_type=F32))
        epilogue(accs, ex, outs)

    in_specs, args, scratch = [], [], []
    for (a, b, b_spec, cb, nk) in pairs:
        k = a.shape[1]
        in_specs += [pl.BlockSpec((tm, k), lambda j, i: (i, 0)), b_spec]
        args += [a, b]
        if cb:
            scratch.append(pltpu.VMEM((tn, k) if nk else (k, tn), BF16))
    in_specs += list(extra_specs)
    args += list(extras)
    return pl.pallas_call(
        kern,
        grid=(n_out // tn, m // tm),
        in_specs=in_specs,
        out_specs=out_specs,
        out_shape=out_shape,
        scratch_shapes=scratch,
        compiler_params=_cparams(2),
    )(*args)


def _wspec(l, k, tn, col_off):
    cb = col_off // tn
    assert cb * tn == col_off
    return pl.BlockSpec((None, k, tn), lambda j, i: (l, 0, cb + j))


def _wspec_nk(l, k, tn, row_off):
    rb = row_off // tn
    assert rb * tn == row_off
    return pl.BlockSpec((None, tn, k), lambda j, i: (l, rb + j, 0))


def _spec2(tm, tn):
    return pl.BlockSpec((tm, tn), lambda j, i: (i, j))


def _hm_spec(tm, tn):
    return pl.BlockSpec((tn // LANES, tm, LANES), lambda j, i: (j, i, 0))


def _store_heads(o_ref, val):
    for c in range(val.shape[1] // LANES):
        o_ref[c] = val[:, c * LANES:(c + 1) * LANES].astype(o_ref.dtype)


def _rms(x, g):
    return x * lax.rsqrt(jnp.mean(x * x, axis=-1, keepdims=True) + EPS) * g


def _mod_row(ref, g):
    return ref[pl.ds(g, 1), :]


def _norm_mod_kernel(x_ref, g_ref, sc_ref, sh_ref, o_ref):
    gb = x_ref.shape[0]
    g0 = pl.program_id(0) * gb
    for g in range(gb):
        y = _rms(x_ref[g], g_ref[...])
        o_ref[g] = (y * (1.0 + _mod_row(sc_ref, g0 + g)) + _mod_row(sh_ref, g0 + g)).astype(o_ref.dtype)


def _split3(x):
    hi = x.astype(BF16)
    r = x - hi.astype(F32)
    mid = r.astype(BF16)
    lo = (r - mid.astype(F32)).astype(BF16)
    return hi, mid, lo


def _dot_f32(a, b3):
    a_hi, a_mid, a_lo = _split3(a)
    b_hi, b_mid, b_lo = b3
    d = functools.partial(jnp.dot, preferred_element_type=F32)
    small = d(a_hi, b_lo) + d(a_lo, b_hi) + d(a_mid, b_mid)
    return (d(a_hi, b_hi) + (d(a_hi, b_mid) + d(a_mid, b_hi))) + small


def _route(logits):
    lane = lax.broadcasted_iota(jnp.int32, logits.shape, 1)
    lanef = lane.astype(F32)
    big = jnp.float32(1e9)
    ninf = jnp.float32(-jnp.inf)
    is_g = (lane >= N_EXPERTS) & (lane < N_EXPERTS + N_GROUPS)
    gl = jnp.where(is_g, logits, ninf)
    gmax = jnp.max(gl, axis=1, keepdims=True)
    g_idx = jnp.min(jnp.where(gl == gmax, lanef - N_EXPERTS, big), axis=1, keepdims=True)
    p_group = 1.0 / jnp.sum(jnp.where(is_g, jnp.exp(gl - gmax), 0.0), axis=1, keepdims=True)
    grp = jnp.floor(lanef * (1.0 / EXPERTS_PER_GROUP))
    in_g = (lane < N_EXPERTS) & (grp == g_idx)
    el = jnp.where(in_g, logits, ninf)
    e1 = jnp.max(el, axis=1, keepdims=True)
    i1 = jnp.min(jnp.where(el == e1, lanef, big), axis=1, keepdims=True)
    el2 = jnp.where(lanef == i1, ninf, el)
    e2 = jnp.max(el2, axis=1, keepdims=True)
    i2 = jnp.min(jnp.where(el2 == e2, lanef, big), axis=1, keepdims=True)
    t = jnp.exp(e2 - e1)
    den = 1.0 + t
    w1 = (1.0 / den) * p_group
    w2 = (t / den) * p_group
    out = jnp.where(lane == 0, i1, jnp.where(lane == 1, i2,
          jnp.where(lane == 2, w1, jnp.where(lane == 3, w2, 0.0))))
    return out


def _slab_pitch(s):
    return s + SLAB_PAD


def _slab_store(ref, row0, val):
    rows, d = val.shape
    s = d // LANES
    p = _slab_pitch(s)
    for c in range(s):
        ref[pl.ds(row0 * p + c, rows, stride=p), :] = val[:, c * LANES:(c + 1) * LANES]


def _slab_load(ref, row0, rows, s, lead=None):
    pieces = []
    p = _slab_pitch(s)
    for c in range(s):
        rs = pl.ds(row0 * p + c, rows, stride=p)
        pieces.append(ref[rs, :] if lead is None else ref[lead, rs, :])
    return jnp.concatenate(pieces, axis=1)


def _norm_route_kernel(x_ref, g_ref, sc_ref, sh_ref, wr_ref, br_ref, h_ref, r_ref):
    gb, grp, _ = x_ref.shape
    g0 = pl.program_id(0) * gb
    b3 = (wr_ref[0], wr_ref[1], wr_ref[2])
    for g in range(gb):
        y = _rms(x_ref[g], g_ref[...])
        h = y * (1.0 + _mod_row(sc_ref, g0 + g)) + _mod_row(sh_ref, g0 + g)
        _slab_store(h_ref, g * grp, h)
        logits = _dot_f32(h, b3) + br_ref[...]
        r_ref[g] = _route(logits)


def _mod_spec(idx, ng, d):
    return pl.BlockSpec((None, ng, d), lambda i: (idx, 0, 0))


def _norm_mod(x3, g, modg, l, sc_i, sh_i):
    ng, grp, d = x3.shape
    gb = _row_tile(ng, 4)
    return pl.pallas_call(
        _norm_mod_kernel,
        grid=(ng // gb,),
        in_specs=[pl.BlockSpec((gb, grp, d), lambda i: (i, 0, 0)),
                  pl.BlockSpec((None, 1, d), lambda i: (l, 0, 0)),
                  _mod_spec(sc_i, ng, d), _mod_spec(sh_i, ng, d)],
        out_specs=pl.BlockSpec((gb, grp, d), lambda i: (i, 0, 0)),
        out_shape=jax.ShapeDtypeStruct(x3.shape, BF16),
        compiler_params=_cparams(1),
    )(x3, g, modg, modg)


def _norm_route(x3, g, modg, l, sc_i, sh_i, wr3, br):
    ng, grp, d = x3.shape
    gb = _row_tile(ng, 4)
    return pl.pallas_call(
        _norm_route_kernel,
        grid=(ng // gb,),
        in_specs=[pl.BlockSpec((gb, grp, d), lambda i: (i, 0, 0)),
                  pl.BlockSpec((None, 1, d), lambda i: (l, 0, 0)),
                  _mod_spec(sc_i, ng, d), _mod_spec(sh_i, ng, d),
                  pl.BlockSpec((None, 3, d, LANES), lambda i: (l, 0, 0, 0)),
                  pl.BlockSpec((None, 1, LANES), lambda i: (l, 0, 0))],
        out_specs=[pl.BlockSpec((gb * grp * _slab_pitch(d // LANES), LANES), lambda i: (i, 0)),
                   pl.BlockSpec((gb, grp, LANES), lambda i: (i, 0, 0))],
        out_shape=[jax.ShapeDtypeStruct((ng * grp * _slab_pitch(d // LANES), LANES), F32),
                   jax.ShapeDtypeStruct((ng, grp, LANES), F32)],
        compiler_params=_cparams(1),
    )(x3, g, modg, modg, wr3, br)


def _final_norm_kernel(x_ref, g_ref, op_ref, os_ref, *, n_prompt_blocks):
    i = pl.program_id(0)
    y = _rms(x_ref[...], g_ref[...])

    @pl.when(i < n_prompt_blocks)
    def _():
        op_ref[...] = y

    @pl.when(i >= n_prompt_blocks)
    def _():
        os_ref[...] = y


def _final_norm(x3, g, ng_prompt):
    ng, grp, d = x3.shape
    gb = _row_tile(np.gcd(ng_prompt, ng - ng_prompt), 4)
    npb = ng_prompt // gb
    blk = lambda f: pl.BlockSpec((gb, grp, d), f)
    return pl.pallas_call(
        functools.partial(_final_norm_kernel, n_prompt_blocks=npb),
        grid=(ng // gb,),
        in_specs=[blk(lambda i: (i, 0, 0)), pl.BlockSpec((1, d), lambda i: (0, 0))],
        out_specs=[blk(lambda i: (jnp.minimum(i, npb - 1), 0, 0)),
                   blk(lambda i: (jnp.maximum(i - npb, 0), 0, 0))],
        out_shape=[jax.ShapeDtypeStruct((ng_prompt, grp, d), F32),
                   jax.ShapeDtypeStruct((ng - ng_prompt, grp, d), F32)],
        compiler_params=_cparams(1),
    )(x3, g)


def _lanes(c, w):
    if w % LANES == 0:
        return c if w == LANES else jnp.tile(c, (1, w // LANES))
    return c[:, :w]


def _dot_nt(a, b):
    return lax.dot_general(a, b, (((1,), (1,)), ((), ())), preferred_element_type=F32)


def _sb_block(qh, kh, vh, c, u, masked):
    w = kh.shape[0]
    z = _dot_nt(qh, kh)
    sp = jnp.maximum(z, 0.0) + jnp.log2(1.0 + jnp.exp2(-jnp.abs(z)))
    if masked:
        row = lax.broadcasted_iota(jnp.int32, z.shape, 0)
        col = lax.broadcasted_iota(jnp.int32, z.shape, 1)
        valid = col < row
        sp = jnp.where(valid, sp, 0.0)
    hi = sp.astype(BF16)
    lo = (sp - hi.astype(F32)).astype(BF16)
    if w % LANES == 0:
        cs = jnp.dot(jnp.concatenate([hi, lo], axis=1), u, preferred_element_type=F32)
    else:
        cs = (jnp.dot(hi, u[:w], preferred_element_type=F32)
              + jnp.dot(lo, u[w:], preferred_element_type=F32))
    wgt = jnp.exp2(z - sp - cs - _lanes(c, w))
    if masked:
        wgt = jnp.where(valid, wgt, 0.0)
    o = jnp.dot(wgt.astype(BF16), vh, preferred_element_type=F32)
    c_new = c + jnp.sum(sp, axis=1, keepdims=True)
    return o, c_new


def _suffix_matrix(w):
    j = np.arange(w)[:, None]
    s = np.arange(w)[None, :]
    u = (j > s).astype(np.float32)
    return jnp.asarray(np.concatenate([u, u], axis=0), dtype=BF16)


def _sb_stream_kernel(q_ref, kn_ref, vn_ref, k_hbm, v_hbm, ud_ref, up_ref, o_ref,
                      kbuf, vbuf, acc, carry, done, alive_ref, sem, *, heads, tk, cache_tiles,
                      cache_base):
    i = pl.program_id(0)
    n_past = i if cache_tiles is None else cache_tiles

    def fetch(j, slot):
        if cache_tiles is None:
            rows = pl.ds(pl.multiple_of(j * tk, tk), tk)
            src_k, src_v = k_hbm.at[:, rows, :], v_hbm.at[:, rows, :]
        else:
            blk = tk * heads
            rows = pl.ds(pl.multiple_of((cache_base + i * cache_tiles + j) * blk, blk), blk)
            src_k, src_v = k_hbm.at[rows], v_hbm.at[rows]
        return (pltpu.make_async_copy(src_k, kbuf.at[slot], sem.at[0, slot]),
                pltpu.make_async_copy(src_v, vbuf.at[slot], sem.at[1, slot]))

    def tile_of(buf, slot, h):
        if cache_tiles is None:
            return buf[slot, h]
        return buf.at[slot][pl.ds(h, tk, stride=heads), :].astype(BF16)

    @pl.when(n_past > 0)
    def _():
        for c in fetch(n_past - 1, 0):
            c.start()

    def diag(h, _):
        o, c = _sb_block(q_ref[h], kn_ref[h], vn_ref[h], jnp.zeros(carry.shape[1:], F32),
                         ud_ref[...], True)
        acc[h] = o
        carry[h] = c
        done[h] = 0
        return 0
    lax.fori_loop(0, heads, diag, 0, unroll=min(heads, HEAD_UNROLL))

    alive_ref[0] = heads

    @pl.when(n_past > 0)
    def _():
        for c in fetch(n_past - 1, 0):
            c.wait()

        @pl.when(n_past > 1)
        def _():
            for c in fetch(n_past - 2, 1):
                c.start()

        def first(h, alive):
            o, c = _sb_block(q_ref[h], tile_of(kbuf, 0, h), tile_of(vbuf, 0, h), carry[h],
                             up_ref[...], False)
            acc[h] = acc[h] + o
            carry[h] = c
            dead = (jnp.min(c) >= SB_DEAD).astype(jnp.int32)
            done[h] = dead
            return alive + 1 - dead
        alive_ref[0] = lax.fori_loop(0, heads, first, 0, unroll=min(heads, HEAD_UNROLL))

    def cond(state):
        j, alive = state
        return (j >= 0) & (alive > 0)

    def body(state):
        j, _ = state
        slot = lax.rem(n_past - 1 - j, 2)
        for c in fetch(j, slot):
            c.wait()

        @pl.when(j > 0)
        def _():
            for c in fetch(j - 1, 1 - slot):
                c.start()

        def head(h, alive):
            @pl.when(done[h] == 0)
            def _():
                o, c = _sb_block(q_ref[h], tile_of(kbuf, slot, h), tile_of(vbuf, slot, h),
                                 carry[h], up_ref[...], False)
                acc[h] = acc[h] + o
                carry[h] = c
                done[h] = (jnp.min(c) >= SB_DEAD).astype(jnp.int32)
            return alive + 1 - done[h]
        return j - 1, lax.fori_loop(0, heads, head, 0)

    j_end, _ = lax.while_loop(cond, body, (jnp.int32(n_past - 2), alive_ref[0]))

    @pl.when(j_end >= 0)
    def _():
        for c in fetch(j_end, lax.rem(n_past - 1 - j_end, 2)):
            c.wait()

    for h in range(heads):
        o_ref[:, h * DH_SB:(h + 1) * DH_SB] = acc[h].astype(o_ref.dtype)


def _sb_stream_attention(q, kn, vn, k_src, v_src, *, tq, tk, q_block0, n_tiles, cache_tiles=None,
                         cache_base=0):
    heads = q.shape[0]
    blk = pl.BlockSpec((heads, tq, DH_SB), lambda i: (0, q_block0 + i, 0))
    anyspec = pl.BlockSpec(memory_space=pl.ANY)
    if cache_tiles is None:
        buf = pltpu.VMEM((2, heads, tk, DH_SB), BF16)
    else:
        buf = pltpu.VMEM((2, tk * heads, DH_SB), F32)
    kern = functools.partial(_sb_stream_kernel, heads=heads, tk=tk, cache_tiles=cache_tiles,
                             cache_base=cache_base)
    return pl.pallas_call(
        kern, grid=(n_tiles,),
        in_specs=[blk, blk, blk, anyspec, anyspec,
                  pl.BlockSpec((2 * tq, tq), lambda i: (0, 0)),
                  pl.BlockSpec((2 * tk, tk), lambda i: (0, 0))],
        out_specs=pl.BlockSpec((tq, heads * DH_SB), lambda i: (i, 0)),
        out_shape=jax.ShapeDtypeStruct((n_tiles * tq, heads * DH_SB), BF16),
        scratch_shapes=[buf, buf,
                        pltpu.VMEM((heads, tq, DH_SB), F32),
                        pltpu.VMEM((heads, tq, LANES), F32),
                        pltpu.SMEM((heads,), jnp.int32),
                        pltpu.SMEM((1,), jnp.int32),
                        pltpu.SemaphoreType.DMA((2, 2))],
        compiler_params=_cparams(1),
    )(q, kn, vn, k_src, v_src, _suffix_matrix(tq), _suffix_matrix(tk))


def _mla_block(qc, kc, va, m, acc, mask):
    s = _dot_nt(qc, kc)
    if mask is not None:
        s = jnp.where(mask, s, -jnp.inf)
    m_new = jnp.maximum(m, jnp.max(s, axis=1, keepdims=True))
    alpha = jnp.exp2(m - m_new)
    p = jnp.exp2(s - _lanes(m_new, s.shape[1]))
    acc_new = _lanes(alpha, acc.shape[1]) * acc + jnp.dot(p.astype(BF16), va,
                                                          preferred_element_type=F32)
    return m_new, acc_new


def _mla_kernel(qb_ref, kb_ref, fl_ref, q_ref, k_ref, v_ref, o_ref, acc, m_sc, *, heads):
    s = pl.program_id(0)
    fl = fl_ref[s]
    tq, tk = q_ref.shape[1], k_ref.shape[1]

    @pl.when((fl & 1) != 0)
    def _():
        m_sc[...] = jnp.full(m_sc.shape, -jnp.inf, F32)
        acc[...] = jnp.zeros(acc.shape, F32)

    def run(mask):
        def body(h, _):
            m, a = _mla_block(q_ref[h], k_ref[h], v_ref[h], m_sc[h], acc[h], mask)
            m_sc[h] = m
            acc[h] = a
            return 0
        lax.fori_loop(0, heads, body, 0, unroll=min(heads, MLA_UNROLL))

    @pl.when((fl & 4) != 0)
    def _():
        row = lax.broadcasted_iota(jnp.int32, (tq, tk), 0) + qb_ref[s] * tq
        col = lax.broadcasted_iota(jnp.int32, (tq, tk), 1) + kb_ref[s] * tk
        run((col // CHUNK) <= (row // CHUNK))

    @pl.when((fl & 4) == 0)
    def _():
        run(None)

    @pl.when((fl & 2) != 0)
    def _():
        for h in range(heads):
            a = acc[h]
            o_ref[:, h * V_DIM:(h + 1) * V_DIM] = (a[:, :V_DIM] / a[:, V_DIM:]).astype(o_ref.dtype)


def _mla_tables(n_rows, tq, tk):
    r = tq // tk
    qb, kb, fl = [], [], []
    for i in range(n_rows // tq):
        tiles = [(i * r + j, 4) for j in range(r)] + [(j, 0) for j in reversed(range(i * r))]
        for idx, (j, f) in enumerate(tiles):
            qb.append(i)
            kb.append(j)
            fl.append(f | (1 if idx == 0 else 0) | (2 if idx == len(tiles) - 1 else 0))
    arr = lambda v: jnp.asarray(np.asarray(v, dtype=np.int32))
    return arr(qb), arr(kb), arr(fl)


def _mla_attention(qc, kc, va, *, n_rows, tq, tk):
    qb, kb, fl = _mla_tables(n_rows, tq, tk)
    h = H_MLA
    wqk = 2 * LANES
    kv_spec = pl.BlockSpec((h, tk, wqk), lambda s, qb, kb, fl: (0, kb[s], 0))
    grid_spec = pltpu.PrefetchScalarGridSpec(
        num_scalar_prefetch=3,
        grid=(qb.shape[0],),
        in_specs=[pl.BlockSpec((h, tq, wqk), lambda s, qb, kb, fl: (0, qb[s], 0)),
                  kv_spec, kv_spec],
        out_specs=pl.BlockSpec((tq, h * V_DIM), lambda s, qb, kb, fl: (qb[s], 0)),
        scratch_shapes=[pltpu.VMEM((h, tq, 2 * V_DIM), F32),
                        pltpu.VMEM((h, tq, LANES), F32)],
    )
    return pl.pallas_call(
        functools.partial(_mla_kernel, heads=h), grid_spec=grid_spec,
        out_shape=jax.ShapeDtypeStruct((n_rows, h * V_DIM), BF16),
        compiler_params=_cparams(1),
    )(qb, kb, fl, qc, kc, va)


def _mla_latent_kernel(qb_ref, kb_ref, fl_ref, q_ref, cn_ref, rn_ref, cp_ref, rp_ref, w_ref,
                       o_ref, qa, qr, acc, m_sc, l_sc, *, heads, pos0):
    s_id = pl.program_id(0)
    fl = fl_ref[s_id]
    is_first = (fl & 1) != 0
    is_last = (fl & 2) != 0
    tq = q_ref.shape[1]
    hw = QK_NOPE + V_DIM

    def step(ck, kr, mask):
        s = _dot_nt(qa[...], ck) + _dot_nt(qr[...], kr)
        if mask is not None:
            s = jnp.where(mask, s, -jnp.inf)
        m_old = m_sc[...]
        m_new = jnp.maximum(m_old, jnp.max(s, axis=1, keepdims=True))
        alpha = jnp.exp2(m_old - m_new)
        p = jnp.exp2(s - m_new[:, :1])
        l_sc[...] = alpha * l_sc[...] + jnp.sum(p, axis=1, keepdims=True)
        acc[...] = alpha[:, :1] * acc[...] + jnp.dot(p.astype(BF16), ck,
                                                     preferred_element_type=F32)
        m_sc[...] = m_new

    @pl.when(is_first)
    def _():
        for h in range(heads):
            qh = q_ref[h]
            w_uk = w_ref[:, h * hw:h * hw + QK_NOPE]
            qa[h * tq:(h + 1) * tq, :] = _dot_nt(qh[:, :QK_NOPE], w_uk).astype(BF16)
            qr[h * tq:(h + 1) * tq, :] = qh[:, QK_NOPE:]
        m_sc[...] = jnp.full(m_sc.shape, -jnp.inf, F32)
        l_sc[...] = jnp.zeros(l_sc.shape, F32)
        acc[...] = jnp.zeros(acc.shape, F32)
        row = lax.broadcasted_iota(jnp.int32, (heads * tq, tq), 0) % tq + pos0
        col = lax.broadcasted_iota(jnp.int32, (heads * tq, tq), 1) + pos0
        step(cn_ref[...], rn_ref[...], (col // CHUNK) <= (row // CHUNK))

    @pl.when(jnp.logical_not(is_first))
    def _():
        step(cp_ref[...].astype(BF16), rp_ref[...], None)

    @pl.when(is_last)
    def _():
        o_lat = (acc[...] / l_sc[...][:, :1]).astype(BF16)
        for h in range(heads):
            w_uv = w_ref[:, h * hw + QK_NOPE:(h + 1) * hw]
            o_ref[:, h * V_DIM:(h + 1) * V_DIM] = jnp.dot(
                o_lat[h * tq:(h + 1) * tq, :], w_uv, preferred_element_type=F32).astype(o_ref.dtype)


def _mla_latent_attention(qc, c_new, r_new, c_past, r_past, w_ukv_b, tables, l, *, tq, tk, pos0,
                          out_rows, out_block_of):
    qb, kb, fl = tables
    h = H_MLA
    kvl = c_new.shape[1]
    wqk = 2 * LANES
    grid_spec = pltpu.PrefetchScalarGridSpec(
        num_scalar_prefetch=3,
        grid=(qb.shape[0],),
        in_specs=[pl.BlockSpec((h, tq, wqk), lambda s, qb, kb, fl: (0, qb[s], 0)),
                  pl.BlockSpec((tq, kvl), lambda s, qb, kb, fl: (qb[s], 0)),
                  pl.BlockSpec((tq, LANES), lambda s, qb, kb, fl: (qb[s], 0)),
                  pl.BlockSpec((tk, kvl), lambda s, qb, kb, fl: (kb[s], 0)),
                  pl.BlockSpec((tk, LANES), lambda s, qb, kb, fl: (kb[s], 0)),
                  pl.BlockSpec((None, kvl, w_ukv_b.shape[2]), lambda s, qb, kb, fl: (l, 0, 0))],
        out_specs=pl.BlockSpec((tq, h * V_DIM), out_block_of),
        scratch_shapes=[pltpu.VMEM((h * tq, kvl), BF16),
                        pltpu.VMEM((h * tq, LANES), BF16),
                        pltpu.VMEM((h * tq, kvl), F32),
                        pltpu.VMEM((h * tq, LANES), F32),
                        pltpu.VMEM((h * tq, LANES), F32)],
    )
    kern = functools.partial(_mla_latent_kernel, heads=h, pos0=pos0)
    return pl.pallas_call(
        kern, grid_spec=grid_spec,
        out_shape=jax.ShapeDtypeStruct((out_rows, h * V_DIM), BF16),
        compiler_params=_cparams(1),
    )(qb, kb, fl, qc, c_new, r_new, c_past, r_past, w_ukv_b)


def _causal_tables(nq, q_off, per_head=False, heads=1, n_batch=1, past_tiles=None,
                   past_stride=0):
    qb, kb, hb, fl = [], [], [], []
    if past_tiles is None:
        for i in range(nq):
            n = i + 1
            for j in range(n):
                qb.append(q_off + i)
                kb.append(max(i - j, 1) - 1 if j == 0 else i - j)
                hb.append(0)
                fl.append((1 if j == 0 else 0) | (2 if j == n - 1 else 0))
    else:
        for b in range(n_batch):
            for h in range(heads if per_head else 1):
                n = 1 + past_tiles
                for j in range(n):
                    qb.append(q_off + b)
                    jj = past_tiles - 1 if j == 0 else past_tiles - j
                    kb.append(past_stride + b * past_tiles + jj)
                    hb.append(h)
                    fl.append((1 if j == 0 else 0) | (2 if j == n - 1 else 0))
    arr = lambda v: jnp.asarray(np.asarray(v, dtype=np.int32))
    return arr(qb), arr(kb), arr(hb), arr(fl)


ROW_DMA_UNROLL = 8


def _moe_kernel(te_ref, nv_ref, tok_ref, dst_ref, h_hbm, w_ref, wg_ref, wu_ref, wd_ref, y_hbm,
                xbuf, obuf, wg_s, wu_s, wd_s, sem_in, sem_out, *, s):
    t = pl.program_id(0)
    tm = w_ref.shape[0]
    nv = nv_ref[0]
    live = t < nv
    slot = lax.rem(t, 2)

    p = _slab_pitch(s)

    def in_copy(tile, r, sl):
        src0 = pl.multiple_of(tok_ref[tile * tm + r] * p, SLAB_PAD)
        return pltpu.make_async_copy(h_hbm.at[pl.ds(src0, s)],
                                     xbuf.at[sl, pl.ds(pl.multiple_of(r * p, SLAB_PAD), s)],
                                     sem_in.at[sl])

    def gather_start(tile, sl):
        def body(r, _):
            in_copy(tile, r, sl).start()
            return 0
        lax.fori_loop(0, tm, body, 0, unroll=ROW_DMA_UNROLL)

    def gather_wait(tile, sl):
        def body(r, _):
            in_copy(tile, r, sl).wait()
            return 0
        lax.fori_loop(0, tm, body, 0, unroll=ROW_DMA_UNROLL)

    @pl.when(live & (t == 0))
    def _():
        gather_start(0, 0)

    @pl.when(t + 1 < nv)
    def _():
        gather_start(t + 1, 1 - slot)

    prev = te_ref[jnp.maximum(t - 1, 0)]
    fresh = (t == 0) | (te_ref[t] != prev)

    @pl.when(live & fresh)
    def _():
        wg_s[...] = wg_ref[...].astype(BF16)
        wu_s[...] = wu_ref[...].astype(BF16)
        wd_s[...] = wd_ref[...].astype(BF16)

    @pl.when(live)
    def _():
        gather_wait(t, slot)
        x = _slab_load(xbuf, 0, tm, s, lead=slot).astype(BF16)
        a = jnp.dot(x, wg_s[...], preferred_element_type=F32)
        u = jnp.dot(x, wu_s[...], preferred_element_type=F32)
        hid = (a * jax.nn.sigmoid(a)) * u * w_ref[...]
        y = jnp.dot(hid.astype(BF16), wd_s[...], preferred_element_type=F32)

        def out_copy(tile, r):
            dst0 = pl.multiple_of(dst_ref[tile * tm + r] * p, SLAB_PAD)
            return pltpu.make_async_copy(obuf.at[pl.ds(pl.multiple_of(r * p, SLAB_PAD), s)],
                                         y_hbm.at[pl.ds(dst0, s)], sem_out)

        def scatter_wait(tile):
            def wait(r, _):
                out_copy(tile, r).wait()
                return 0
            lax.fori_loop(0, tm, wait, 0, unroll=ROW_DMA_UNROLL)

        @pl.when(t > 0)
        def _():
            scatter_wait(t - 1)

        _slab_store(obuf, 0, y)

        def start(r, _):
            out_copy(t, r).start()
            return 0
        lax.fori_loop(0, tm, start, 0, unroll=ROW_DMA_UNROLL)

        @pl.when(t == nv - 1)
        def _():
            scatter_wait(t)


def _moe_experts(h_slab, row_tok, row_dst, row_w, tile_expert, n_valid, w_gate, w_up, w_down, l,
                 n_dest):
    d, f = w_gate.shape[-2:]
    s = d // LANES
    p = _slab_pitch(s)
    tm = MOE_TILE
    n_tiles = row_tok.shape[0] // tm
    wmap = lambda t, te, nv, tok, dst: (l, te[t], 0, 0)
    grid_spec = pltpu.PrefetchScalarGridSpec(
        num_scalar_prefetch=4, grid=(n_tiles,),
        in_specs=[pl.BlockSpec(memory_space=pl.ANY),
                  pl.BlockSpec((tm, 1), lambda t, te, nv, tok, dst: (t, 0)),
                  pl.BlockSpec((None, None, d, f), wmap),
                  pl.BlockSpec((None, None, d, f), wmap),
                  pl.BlockSpec((None, None, f, d), wmap)],
        out_specs=pl.BlockSpec(memory_space=pl.ANY),
        scratch_shapes=[pltpu.VMEM((2, tm * p, LANES), F32), pltpu.VMEM((tm * p, LANES), F32),
                        pltpu.VMEM((d, f), BF16), pltpu.VMEM((d, f), BF16),
                        pltpu.VMEM((f, d), BF16),
                        pltpu.SemaphoreType.DMA((2,)), pltpu.SemaphoreType.DMA(())],
    )
    return pl.pallas_call(
        functools.partial(_moe_kernel, s=s), grid_spec=grid_spec,
        out_shape=jax.ShapeDtypeStruct(((n_dest + tm) * p, LANES), F32),
        compiler_params=_cparams(1),
    )(tile_expert, n_valid, row_tok, row_dst, h_slab, row_w, w_gate, w_up, w_down)


def _combine_kernel(x_ref, y0_ref, y1_ref, g_ref, o_ref):
    gb, grp, d = x_ref.shape
    s = d // LANES
    g0 = pl.program_id(0) * gb
    for g in range(gb):
        y = _slab_load(y0_ref, g * grp, grp, s) + _slab_load(y1_ref, g * grp, grp, s)
        o_ref[g] = x_ref[g] + _mod_row(g_ref, g0 + g) * y


def _combine(x3, yg, modg, l, g_i):
    ng, grp, d = x3.shape
    s = d // LANES
    gb = _row_tile(ng, 2)
    blk = pl.BlockSpec((gb, grp, d), lambda i: (i, 0, 0))
    yblk = lambda off: pl.BlockSpec((gb * grp * _slab_pitch(s), LANES), lambda i: (i + off, 0))
    return pl.pallas_call(
        _combine_kernel,
        grid=(ng // gb,),
        in_specs=[blk, yblk(0), yblk(ng // gb), _mod_spec(g_i, ng, d)],
        out_specs=blk,
        out_shape=jax.ShapeDtypeStruct(x3.shape, F32),
        compiler_params=_cparams(1),
    )(x3, yg, yg, modg)


def _dispatch(route, n):
    tm = MOE_TILE
    r_cap = (2 * n + N_EXPERTS * (tm - 1) + tm - 1) // tm * tm
    e = route[:, :2].astype(jnp.int32)
    w = route[:, 2:4]
    flat_e = e.T.reshape(-1)
    flat_w = w.T.reshape(-1)
    order = jnp.argsort(flat_e, stable=True).astype(jnp.int32)
    bounds = jnp.searchsorted(flat_e[order], jnp.arange(N_EXPERTS + 1, dtype=jnp.int32),
                              side="left").astype(jnp.int32)
    counts = bounds[1:] - bounds[:-1]
    padded = (counts + tm - 1) // tm * tm
    ends_p = jnp.cumsum(padded)
    starts_p = ends_p - padded
    starts = jnp.cumsum(counts) - counts
    tile_start = jnp.arange(r_cap // tm, dtype=jnp.int32) * tm
    tile_expert = jnp.minimum(jnp.searchsorted(ends_p, tile_start, side="right"),
                              N_EXPERTS - 1).astype(jnp.int32)
    n_valid = (ends_p[-1:] // tm).astype(jnp.int32)
    per_row = lambda v: jnp.broadcast_to(v[:, None], (r_cap // tm, tm)).reshape(r_cap)
    local = jnp.arange(r_cap, dtype=jnp.int32) - per_row(starts_p[tile_expert])
    valid = local < per_row(counts[tile_expert])
    slot = order[jnp.clip(per_row(starts[tile_expert]) + local, 0, 2 * n - 1)]
    row_dst = jnp.where(valid, slot, 2 * n + jnp.arange(r_cap, dtype=jnp.int32) % tm)
    row_tok = jnp.where(valid, jnp.where(slot >= n, slot - n, slot), 0)
    row_w = jnp.where(valid, flat_w[slot], 0.0)
    return row_tok, row_dst, row_w.reshape(r_cap, 1), tile_expert, n_valid


def _gates_kernel(h_ref, wa_ref, wb_ref, o_ref, bsc):
    @pl.when(pl.program_id(1) == 0)
    def _():
        r = wb_ref.shape[0]
        w = wa_ref.shape[0]
        bsc[:w - r, :] = wa_ref[r:, :].astype(BF16)
        bsc[w - r:, :] = wb_ref[...].astype(BF16)

    o_ref[...] = jax.nn.sigmoid(_dot_nt(h_ref[...], bsc[...]))


def _gates(h, w_nk, l, off_kr, n_gate, tm, tn):
    n, k = h.shape
    assert off_kr % tn == 0 and n_gate % tn == 0 and tn % QK_ROPE == 0
    rb = off_kr // tn
    return pl.pallas_call(
        _gates_kernel,
        grid=(n_gate // tn, n // tm),
        in_specs=[pl.BlockSpec((tm, k), lambda j, i: (i, 0)),
                  pl.BlockSpec((None, tn, k), lambda j, i: (l, rb + j, 0)),
                  pl.BlockSpec((None, QK_ROPE, k),
                               lambda j, i: (l, (rb + j + 1) * (tn // QK_ROPE), 0))],
        out_specs=pl.BlockSpec((tm, tn), lambda j, i: (i, j)),
        out_shape=jax.ShapeDtypeStruct((n, n_gate), F32),
        scratch_shapes=[pltpu.VMEM((tn, k), BF16)],
        compiler_params=_cparams(2),
    )(h, w_nk, w_nk)


def _rope_weights_kernel(a_ref, o_ref):
    a = a_ref[...].astype(BF16)
    q = a.shape[0] // 2
    o_ref[...] = jnp.concatenate([a, a[q:], a[:q]], axis=0)


def _rope_key_weights(w_nk, off_kr):
    depth, n_in, k = w_nk.shape
    assert QK_ROPE * 2 == LANES and off_kr % QK_ROPE == 0
    return pl.pallas_call(
        _rope_weights_kernel,
        grid=(depth,),
        in_specs=[pl.BlockSpec((None, QK_ROPE, k), lambda l: (l, off_kr // QK_ROPE, 0))],
        out_specs=pl.BlockSpec((None, 2 * QK_ROPE, k), lambda l: (l, 0, 0)),
        out_shape=jax.ShapeDtypeStruct((depth, 2 * QK_ROPE, k), BF16),
        compiler_params=_cparams(1),
    )(w_nk)


def _rope_tables(pos):
    inv = ROPE_THETA ** (-jnp.arange(0, QK_ROPE, 2, dtype=F32) / QK_ROPE)
    ang = pos.astype(F32)[:, None] * inv[None, :]
    cos, sin = jnp.cos(ang), jnp.sin(ang)
    pad = jnp.zeros((pos.shape[0], LANES - QK_ROPE), F32)
    return (jnp.concatenate([cos, cos, pad], axis=1),
            jnp.concatenate([-sin, sin, pad], axis=1))


def _rope_lanes(acc, c, s):
    return acc * c + pltpu.roll(acc, LANES - QK_ROPE, axis=1) * s


def _swap_halves(w):
    half = w.shape[-1] // 2
    return jnp.concatenate([w[..., half:], w[..., :half]], axis=-1)


def kernel(x_prompt, x_sample, c_prompt, c_sample, cache_sb_k, cache_sb_v, cache_mla_ckv,
           cache_mla_krope, w_ada, b_ada, g_norm_mix, g_norm_ffn, w_in, g_q_lat, g_kv_lat,
           w_uq, w_ukv, w_branch_sb, w_branch_mla, w_out, w_router_group, b_router_group,
           w_router_expert, b_router_expert, w_exp_gate, w_exp_up, w_exp_down, g_final):
    bp, t_p, d = x_prompt.shape
    bs, t_s, _ = x_sample.shape
    depth = w_in.shape[0]
    past = cache_sb_k.shape[2]
    grp = t_s
    n_p, n_s = bp * t_p, bs * t_s
    n = n_p + n_s
    ng = n // grp
    sb_w = H_SB * DH_SB
    q_lora = g_q_lat.shape[1]
    kv_lora = g_kv_lat.shape[1]
    tm = _token_tile(n, grp)
    gpt = tm // grp
    tn = 512
    assert bp == 1 and t_p % ATTN_TILE == 0 and t_p % grp == 0 and past % 512 == 0

    x3 = jnp.concatenate([x_prompt.reshape(n_p // grp, grp, d), x_sample], axis=0)

    n_c = bp + bs
    c_rows = 16
    c_all = jnp.zeros((c_rows, d), F32).at[:n_c].set(jnp.concatenate([c_prompt, c_sample], 0))
    n_modc = N_MOD * d

    def ada_epi(accs, ex, outs):
        outs[0][...] = accs[0] + ex[0][...]

    mods = []
    for l in range(depth):
        mods.append(_matmul(
            [(c_all, w_ada, _wspec(l, d, tn, 0), True)], m=c_rows, n_out=n_modc, tm=c_rows, tn=tn,
            prologue=lambda a, ex: a * jax.nn.sigmoid(a),
            epilogue=ada_epi,
            extras=[b_ada.reshape(depth, 1, n_modc)],
            extra_specs=[pl.BlockSpec((None, 1, tn), lambda j, i, l=l: (l, 0, j))],
            out_shape=[jax.ShapeDtypeStruct((c_rows, n_modc), F32)],
            out_specs=[_spec2(c_rows, tn)])[0])
    mod = jnp.stack(mods)
    modg = jnp.concatenate(
        [jnp.broadcast_to(mod[:, :bp], (depth, n_p // grp, n_modc)), mod[:, bp:n_c]], axis=1)
    modg = modg.reshape(depth, ng, N_MOD, d).transpose(0, 2, 1, 3)
    modg = modg.reshape(depth * N_MOD, ng, d)

    def mod_idx(l, k):
        return l * N_MOD + k

    def mspec_rows(l, k):
        return pl.BlockSpec((None, ng, tn), lambda j, i: (mod_idx(l, k), 0, j))

    pos = jnp.concatenate([jnp.arange(t_p, dtype=jnp.int32),
                           jnp.tile(past + jnp.arange(t_s, dtype=jnp.int32), bs)])
    rope_c, rope_s = _rope_tables(pos)
    tq = ATTN_TILE
    tk_s = 512
    pt = past // tk_s
    tab_sb_s = lambda l: _causal_tables(0, n_p // t_s, n_batch=bs, past_tiles=pt,
                                        past_stride=l * bs * pt)

    off_q, off_k, off_v = 0, sb_w, 2 * sb_w
    off_cq = 3 * sb_w
    off_ckv = off_cq + q_lora
    off_kr = off_ckv + kv_lora
    off_g = off_kr + QK_ROPE
    assert off_g == off_kr + QK_ROPE
    w_nk = jnp.swapaxes(w_in, 1, 2)
    w_kr_aug = _rope_key_weights(w_nk, off_kr)
    uq = w_uq.reshape(depth, q_lora, H_MLA, QK_NOPE + QK_ROPE)
    uq_r = uq[..., QK_NOPE:]
    w_uq_cat = jnp.concatenate([uq[..., :QK_NOPE], uq_r, _swap_halves(uq_r)], axis=-1)
    w_uq_cat = w_uq_cat.reshape(depth, q_lora, H_MLA * 2 * LANES)
    w_router = jnp.concatenate(
        [w_router_expert, w_router_group,
         jnp.zeros((depth, d, LANES - N_EXPERTS - N_GROUPS), F32)], axis=-1)
    r_hi = w_router.astype(BF16)
    r_res = w_router - r_hi.astype(F32)
    r_mid = r_res.astype(BF16)
    r_lo = (r_res - r_mid.astype(F32)).astype(BF16)
    w_router3 = jnp.stack([r_hi, r_mid, r_lo], axis=1)
    b_router = jnp.concatenate(
        [b_router_expert, b_router_group,
         jnp.zeros((depth, LANES - N_EXPERTS - N_GROUPS), F32)], axis=-1).reshape(depth, 1, LANES)

    rows_past = bs * past
    kr_past = jnp.pad(cache_mla_krope.reshape(depth * rows_past, QK_ROPE),
                      ((0, 0), (0, LANES - QK_ROPE))).astype(BF16)
    w_ukv_b = w_ukv.astype(BF16)
    hm = lambda rows: jax.ShapeDtypeStruct((H_SB, rows, LANES), BF16)
    new_k, new_v, new_c, new_r = [], [], [], []

    for l in range(depth):
        h = _norm_mod(x3, g_norm_mix.reshape(depth, 1, d), modg, l, mod_idx(l, 1), mod_idx(l, 0))
        h = h.reshape(n, d)

        sbq_scale = DH_SB ** -0.5 * float(np.log2(np.e))

        def plain_hm(accs, ex, outs):
            _store_heads(outs[0], accs[0] * sbq_scale)

        def f32_and_hm(accs, ex, outs):
            outs[0][...] = accs[0]
            _store_heads(outs[1], accs[0])

        sb_q = _matmul([(h, w_nk, _wspec_nk(l, d, tn, off_q), True, True)], m=n, n_out=sb_w, tm=tm, tn=tn,
                       epilogue=plain_hm, out_shape=[hm(n)], out_specs=[_hm_spec(tm, tn)])[0]
        k_f32, sb_k = _matmul([(h, w_nk, _wspec_nk(l, d, tn, off_k), True, True)], m=n, n_out=sb_w, tm=tm,
                              tn=tn, epilogue=f32_and_hm,
                              out_shape=[jax.ShapeDtypeStruct((n, sb_w), F32), hm(n)],
                              out_specs=[_spec2(tm, tn), _hm_spec(tm, tn)])
        v_f32, sb_v = _matmul([(h, w_nk, _wspec_nk(l, d, tn, off_v), True, True)], m=n, n_out=sb_w, tm=tm,
                              tn=tn, epilogue=f32_and_hm,
                              out_shape=[jax.ShapeDtypeStruct((n, sb_w), F32), hm(n)],
                              out_specs=[_spec2(tm, tn), _hm_spec(tm, tn)])

        def plain_f32(accs, ex, outs):
            outs[0][...] = accs[0]

        c_q = _matmul([(h, w_nk, _wspec_nk(l, d, tn, off_cq), True, True)], m=n, n_out=q_lora, tm=tm, tn=tn,
                      epilogue=plain_f32, out_shape=[jax.ShapeDtypeStruct((n, q_lora), F32)],
                      out_specs=[_spec2(tm, tn)])[0]

        def ckv_epi(accs, ex, outs):
            y = _rms(accs[0], ex[0][...])
            outs[0][...] = y
            outs[1][...] = y.astype(BF16)

        c_kv, c_kv_b = _matmul(
            [(h, w_nk, _wspec_nk(l, d, kv_lora, off_ckv), True, True)], m=n, n_out=kv_lora, tm=tm, tn=kv_lora,
            epilogue=ckv_epi, extras=[g_kv_lat.reshape(depth, 1, kv_lora)],
            extra_specs=[pl.BlockSpec((None, 1, kv_lora), lambda j, i: (l, 0, 0))],
            out_shape=[jax.ShapeDtypeStruct((n, kv_lora), F32),
                       jax.ShapeDtypeStruct((n, kv_lora), BF16)],
            out_specs=[_spec2(tm, kv_lora), _spec2(tm, kv_lora)])

        def kr_epi(accs, ex, outs):
            r = _rope_lanes(accs[0], ex[0][...], ex[1][...])
            outs[0][...] = r
            outs[1][...] = r.astype(BF16)

        rope_specs = [pl.BlockSpec((tm, LANES), lambda j, i: (i, 0))] * 2
        k_r, k_r_b = _matmul(
            [(h, w_kr_aug, pl.BlockSpec((None, LANES, d), lambda j, i: (l, 0, 0)), False, True)],
            m=n, n_out=LANES, tm=tm, tn=LANES, epilogue=kr_epi,
            extras=[rope_c, rope_s], extra_specs=rope_specs,
            out_shape=[jax.ShapeDtypeStruct((n, LANES), F32), jax.ShapeDtypeStruct((n, LANES), BF16)],
            out_specs=[_spec2(tm, LANES), _spec2(tm, LANES)])

        gates = _gates(h, w_nk, l, off_kr, 2 * d, tm, tn)

        def cq_prologue(a, ex):
            return _rms(a, ex[0][...])

        gq_spec = pl.BlockSpec((None, 1, q_lora), lambda j, i: (l, 0, 0))
        gq = g_q_lat.reshape(depth, 1, q_lora)
        wqk = 2 * LANES
        hm_qk = lambda rows: jax.ShapeDtypeStruct((H_MLA, rows, wqk), BF16)

        q_scale = (QK_NOPE + QK_ROPE) ** -0.5 * float(np.log2(np.e))

        def qcat_epi(accs, ex, outs):
            c, s = ex[1][...], ex[2][...]
            for hh in range(tn // wqk):
                blk = accs[0][:, hh * wqk:(hh + 1) * wqk]
                outs[0][hh, :, :LANES] = (blk[:, :LANES] * q_scale).astype(BF16)
                outs[0][hh, :, LANES:] = (_rope_lanes(blk[:, LANES:], c, s) * q_scale).astype(BF16)

        q_cat = _matmul(
            [(c_q, w_uq_cat, pl.BlockSpec((None, q_lora, tn), lambda j, i: (l, 0, j)), True)],
            m=n, n_out=H_MLA * wqk, tm=tm, tn=tn, prologue=cq_prologue, epilogue=qcat_epi,
            extras=[gq, rope_c, rope_s], extra_specs=[gq_spec] + rope_specs,
            out_shape=[hm_qk(n)],
            out_specs=[pl.BlockSpec((tn // wqk, tm, wqk), lambda j, i: (j, i, 0))])[0]

        tn_kv = min(2048, H_MLA * wqk)
        hpt = tn_kv // wqk

        def kv_epi(accs, ex, outs):
            kr = ex[0][...]
            ones = jnp.ones(kr.shape, BF16)
            for hh in range(hpt):
                outs[0][hh, :, :LANES] = accs[0][:, hh * wqk:hh * wqk + LANES].astype(BF16)
                outs[0][hh, :, LANES:] = kr
                outs[1][hh, :, :LANES] = accs[0][:, hh * wqk + LANES:(hh + 1) * wqk].astype(BF16)
                outs[1][hh, :, LANES:] = ones

        def up_kv(a, kr, rows, tmr):
            hspec = pl.BlockSpec((hpt, tmr, wqk), lambda j, i: (j, i, 0))
            return _matmul(
                [(a, w_ukv, pl.BlockSpec((None, kv_lora, tn_kv), lambda j, i: (l, 0, j)), True)],
                m=rows, n_out=H_MLA * wqk, tm=tmr, tn=tn_kv, epilogue=kv_epi, extras=[kr],
                extra_specs=[pl.BlockSpec((tmr, LANES), lambda j, i: (i, 0))],
                out_shape=[hm_qk(rows), hm_qk(rows)], out_specs=[hspec, hspec])

        kc_new, v_new = up_kv(c_kv_b, k_r_b, n, tm)

        o_sb_p = _sb_stream_attention(sb_q, sb_k, sb_v, sb_k, sb_v, tq=tq, tk=tq, q_block0=0,
                                      n_tiles=n_p // tq)
        cache_k2 = cache_sb_k.reshape(depth * bs * past * H_SB, DH_SB)
        cache_v2 = cache_sb_v.reshape(depth * bs * past * H_SB, DH_SB)
        q_off_s = n_p // t_s
        o_sb_s = _sb_stream_attention(sb_q, sb_k, sb_v, cache_k2, cache_v2, tq=t_s, tk=ATTN_TILE,
                                      q_block0=q_off_s, n_tiles=bs,
                                      cache_tiles=past // ATTN_TILE,
                                      cache_base=l * bs * (past // ATTN_TILE))
        o_sb = jnp.concatenate([o_sb_p, o_sb_s], axis=0)

        o_mla_p = _mla_attention(q_cat, kc_new, v_new, n_rows=n_p,
                                 tq=min(MLA_Q_TILE, n_p), tk=min(MLA_Q_TILE, n_p))
        tab_s = tab_sb_s(l)
        o_mla_s = _mla_latent_attention(
            q_cat, c_kv_b, k_r_b, cache_mla_ckv.reshape(depth * rows_past, kv_lora), kr_past,
            w_ukv_b, (tab_s[0], tab_s[1], tab_s[3]), l, tq=t_s, tk=tk_s, pos0=past, out_rows=n_s,
            out_block_of=lambda s, qb, kb, fl: (qb[s] - q_off_s, 0))
        o_mla = jnp.concatenate([o_mla_p, o_mla_s], axis=0)

        def merge_epi(accs, ex, outs):
            outs[0][...] = (ex[0][...] * accs[0] + ex[1][...] * accs[1]).astype(BF16)

        merged = _matmul(
            [(o_sb, w_branch_sb, _wspec(l, sb_w, tn, 0), True),
             (o_mla, w_branch_mla, _wspec(l, H_MLA * V_DIM, tn, 0), True)],
            m=n, n_out=d, tm=tm, tn=tn, epilogue=merge_epi,
            extras=[gates, gates],
            extra_specs=[pl.BlockSpec((tm, tn), lambda j, i: (i, j)),
                         pl.BlockSpec((tm, tn), lambda j, i: (i, j + d // tn))],
            out_shape=[jax.ShapeDtypeStruct((n, d), BF16)], out_specs=[_spec2(tm, tn)])[0]

        def resid_epi(accs, ex, outs):
            g0 = pl.program_id(1) * gpt
            for g in range(gpt):
                outs[0][g] = ex[0][g] + _mod_row(ex[1], g0 + g) * accs[0][g * grp:(g + 1) * grp, :]

        x_spec = pl.BlockSpec((gpt, grp, tn), lambda j, i: (i, 0, j))
        x3 = _matmul(
            [(merged, w_out, _wspec(l, d, tn, 0), True)], m=n, n_out=d, tm=tm, tn=tn,
            epilogue=resid_epi, extras=[x3, modg], extra_specs=[x_spec, mspec_rows(l, 2)],
            out_shape=[jax.ShapeDtypeStruct((ng, grp, d), F32)], out_specs=[x_spec])[0]

        h2, route = _norm_route(x3, g_norm_ffn.reshape(depth, 1, d), modg, l, mod_idx(l, 4),
                                mod_idx(l, 3), w_router3, b_router)
        row_tok, row_dst, row_w, tile_expert, n_valid = _dispatch(route.reshape(n, LANES), n)
        yg = _moe_experts(h2, row_tok, row_dst, row_w, tile_expert, n_valid,
                          w_exp_gate, w_exp_up, w_exp_down, l, 2 * n)
        x3 = _combine(x3, yg, modg, l, mod_idx(l, 5))

        new_k.append(k_f32)
        new_v.append(v_f32)
        new_c.append(c_kv)
        new_r.append(k_r[:, :QK_ROPE])

    y_p, y_s = _final_norm(x3, g_final.reshape(1, d), n_p // grp)

    def split(parts, tail):
        a = jnp.stack(parts)
        return (a[:, :n_p].reshape((depth, bp, t_p) + tail),
                a[:, n_p:].reshape((depth, bs, t_s) + tail))

    pk, sk = split(new_k, (H_SB, DH_SB))
    pv, sv = split(new_v, (H_SB, DH_SB))
    pc, sc = split(new_c, (kv_lora,))
    pr, sr = split(new_r, (QK_ROPE,))
    return (y_p.reshape(bp, t_p, d), y_s.reshape(bs, t_s, d), pk, pv, pc, pr, sk, sv, sc, sr)
```

```python
import functools

import numpy as np
import jax
import jax.numpy as jnp
from jax import lax
from jax.experimental import pallas as pl
from jax.experimental.pallas import tpu as pltpu

F32 = jnp.float32
BF16 = jnp.bfloat16

CHUNK = 64
H_SB = 16
DH_SB = 128
H_MLA = 16
QK_NOPE = 128
QK_ROPE = 64
V_DIM = 128
ROPE_THETA = 10000.0
N_GROUPS = 4
EXPERTS_PER_GROUP = 8
N_EXPERTS = N_GROUPS * EXPERTS_PER_GROUP
N_MOD = 6
EPS = 1e-6

LANES = 128
ATTN_TILE = 256
MOE_TILE = 256
HEAD_UNROLL = 4
MLA_UNROLL = 8
MLA_Q_TILE = 512
SLAB_PAD = 4
SB_DEAD = 152.0
VMEM_LIMIT = 56 * 1024 * 1024


def _cparams(n_axes, vmem=VMEM_LIMIT):
    return pltpu.CompilerParams(dimension_semantics=("arbitrary",) * n_axes,
                                vmem_limit_bytes=vmem)


def _row_tile(n, cap=512):
    t = cap
    while n % t:
        t //= 2
    return t


def _token_tile(n, grp, cap=1152):
    return max(t for t in range(grp, cap + 1, grp) if n % t == 0)


def _matmul(pairs, *, m, n_out, tm, tn, epilogue, out_shape, out_specs,
            extras=(), extra_specs=(), prologue=None):
    n_pairs = len(pairs)
    n_ex = len(extras)
    n_outs = len(out_shape)
    pairs = [tuple(p) + (False,) * (5 - len(p)) for p in pairs]
    cast = [p[3] for p in pairs]
    b_nk = [p[4] for p in pairs]

    def kern(*refs):
        a_refs = refs[0:2 * n_pairs:2]
        b_refs = refs[1:2 * n_pairs:2]
        ex = refs[2 * n_pairs:2 * n_pairs + n_ex]
        outs = refs[2 * n_pairs + n_ex:2 * n_pairs + n_ex + n_outs]
        scr = refs[2 * n_pairs + n_ex + n_outs:]
        i = pl.program_id(1)
        accs = []
        si = 0
        for p in range(n_pairs):
            if cast[p]:
                bsc = scr[si]
                si += 1

                @pl.when(i == 0)
                def _(bsc=bsc, b_ref=b_refs[p]):
                    bsc[...] = b_ref[...].astype(BF16)

                bv = bsc[...]
            else:
                bv = b_refs[p][...]
            a = a_refs[p][...]
            if prologue is not None:
                a = prologue(a, ex)
            if b_nk[p]:
                accs.append(_dot_nt(a.astype(BF16), bv))
            else:
                accs.append(jnp.dot(a.astype(BF16), bv, preferred_element_type=F32))
        epilogue(accs, ex, outs)

    in_specs, args, scratch = [], [], []
    for (a, b, b_spec, cb, nk) in pairs:
        k = a.shape[1]
        in_specs += [pl.BlockSpec((tm, k), lambda j, i: (i, 0)), b_spec]
        args += [a, b]
        if cb:
            scratch.append(pltpu.VMEM((tn, k) if nk else (k, tn), BF16))
    in_specs += list(extra_specs)
    args += list(extras)
    return pl.pallas_call(
        kern,
        grid=(n_out // tn, m // tm),
        in_specs=in_specs,
        out_specs=out_specs,
        out_shape=out_shape,
        scratch_shapes=scratch,
        compiler_params=_cparams(2),
    )(*args)


def _wspec(l, k, tn, col_off):
    cb = col_off // tn
    assert cb * tn == col_off
    return pl.BlockSpec((None, k, tn), lambda j, i: (l, 0, cb + j))


def _wspec_nk(l, k, tn, row_off):
    rb = row_off // tn
    assert rb * tn == row_off
    return pl.BlockSpec((None, tn, k), lambda j, i: (l, rb + j, 0))


def _spec2(tm, tn):
    return pl.BlockSpec((tm, tn), lambda j, i: (i, j))


def _hm_spec(tm, tn):
    return pl.BlockSpec((tn // LANES, tm, LANES), lambda j, i: (j, i, 0))


def _store_heads(o_ref, val):
    for c in range(val.shape[1] // LANES):
        o_ref[c] = val[:, c * LANES:(c + 1) * LANES].astype(o_ref.dtype)


def _rms(x, g):
    return x * lax.rsqrt(jnp.mean(x * x, axis=-1, keepdims=True) + EPS) * g


def _mod_row(ref, g):
    return ref[pl.ds(g, 1), :]


def _norm_mod_kernel(x_ref, g_ref, sc_ref, sh_ref, o_ref):
    gb = x_ref.shape[0]
    g0 = pl.program_id(0) * gb
    for g in range(gb):
        y = _rms(x_ref[g], g_ref[...])
        o_ref[g] = (y * (1.0 + _mod_row(sc_ref, g0 + g)) + _mod_row(sh_ref, g0 + g)).astype(o_ref.dtype)


def _split3(x):
    hi = x.astype(BF16)
    r = x - hi.astype(F32)
    mid = r.astype(BF16)
    lo = (r - mid.astype(F32)).astype(BF16)
    return hi, mid, lo


def _dot_f32(a, b3):
    a_hi, a_mid, a_lo = _split3(a)
    b_hi, b_mid, b_lo = b3
    d = functools.partial(jnp.dot, preferred_element_type=F32)
    small = d(a_hi, b_lo) + d(a_lo, b_hi) + d(a_mid, b_mid)
    return (d(a_hi, b_hi) + (d(a_hi, b_mid) + d(a_mid, b_hi))) + small


def _route(logits):
    lane = lax.broadcasted_iota(jnp.int32, logits.shape, 1)
    lanef = lane.astype(F32)
    big = jnp.float32(1e9)
    ninf = jnp.float32(-jnp.inf)
    is_g = (lane >= N_EXPERTS) & (lane < N_EXPERTS + N_GROUPS)
    gl = jnp.where(is_g, logits, ninf)
    gmax = jnp.max(gl, axis=1, keepdims=True)
    g_idx = jnp.min(jnp.where(gl == gmax, lanef - N_EXPERTS, big), axis=1, keepdims=True)
    p_group = 1.0 / jnp.sum(jnp.where(is_g, jnp.exp(gl - gmax), 0.0), axis=1, keepdims=True)
    grp = jnp.floor(lanef * (1.0 / EXPERTS_PER_GROUP))
    in_g = (lane < N_EXPERTS) & (grp == g_idx)
    el = jnp.where(in_g, logits, ninf)
    e1 = jnp.max(el, axis=1, keepdims=True)
    i1 = jnp.min(jnp.where(el == e1, lanef, big), axis=1, keepdims=True)
    el2 = jnp.where(lanef == i1, ninf, el)
    e2 = jnp.max(el2, axis=1, keepdims=True)
    i2 = jnp.min(jnp.where(el2 == e2, lanef, big), axis=1, keepdims=True)
    t = jnp.exp(e2 - e1)
    den = 1.0 + t
    w1 = (1.0 / den) * p_group
    w2 = (t / den) * p_group
    out = jnp.where(lane == 0, i1, jnp.where(lane == 1, i2,
          jnp.where(lane == 2, w1, jnp.where(lane == 3, w2, 0.0))))
    return out


def _slab_pitch(s):
    return s + SLAB_PAD


def _slab_store(ref, row0, val):
    rows, d = val.shape
    s = d // LANES
    p = _slab_pitch(s)
    for c in range(s):
        ref[pl.ds(row0 * p + c, rows, stride=p), :] = val[:, c * LANES:(c + 1) * LANES]


def _slab_load(ref, row0, rows, s, lead=None):
    pieces = []
    p = _slab_pitch(s)
    for c in range(s):
        rs = pl.ds(row0 * p + c, rows, stride=p)
        pieces.append(ref[rs, :] if lead is None else ref[lead, rs, :])
    return jnp.concatenate(pieces, axis=1)


def _norm_route_kernel(x_ref, g_ref, sc_ref, sh_ref, wr_ref, br_ref, h_ref, r_ref):
    gb, grp, _ = x_ref.shape
    g0 = pl.program_id(0) * gb
    b3 = (wr_ref[0], wr_ref[1], wr_ref[2])
    for g in range(gb):
        y = _rms(x_ref[g], g_ref[...])
        h = y * (1.0 + _mod_row(sc_ref, g0 + g)) + _mod_row(sh_ref, g0 + g)
        _slab_store(h_ref, g * grp, h)
        logits = _dot_f32(h, b3) + br_ref[...]
        r_ref[g] = _route(logits)


def _mod_spec(idx, ng, d):
    return pl.BlockSpec((None, ng, d), lambda i: (idx, 0, 0))


def _norm_mod(x3, g, modg, l, sc_i, sh_i):
    ng, grp, d = x3.shape
    gb = _row_tile(ng, 4)
    return pl.pallas_call(
        _norm_mod_kernel,
        grid=(ng // gb,),
        in_specs=[pl.BlockSpec((gb, grp, d), lambda i: (i, 0, 0)),
                  pl.BlockSpec((None, 1, d), lambda i: (l, 0, 0)),
                  _mod_spec(sc_i, ng, d), _mod_spec(sh_i, ng, d)],
        out_specs=pl.BlockSpec((gb, grp, d), lambda i: (i, 0, 0)),
        out_shape=jax.ShapeDtypeStruct(x3.shape, BF16),
        compiler_params=_cparams(1),
    )(x3, g, modg, modg)


def _norm_route(x3, g, modg, l, sc_i, sh_i, wr3, br):
    ng, grp, d = x3.shape
    gb = _row_tile(ng, 4)
    return pl.pallas_call(
        _norm_route_kernel,
        grid=(ng // gb,),
        in_specs=[pl.BlockSpec((gb, grp, d), lambda i: (i, 0, 0)),
                  pl.BlockSpec((None, 1, d), lambda i: (l, 0, 0)),
                  _mod_spec(sc_i, ng, d), _mod_spec(sh_i, ng, d),
                  pl.BlockSpec((None, 3, d, LANES), lambda i: (l, 0, 0, 0)),
                  pl.BlockSpec((None, 1, LANES), lambda i: (l, 0, 0))],
        out_specs=[pl.BlockSpec((gb * grp * _slab_pitch(d // LANES), LANES), lambda i: (i, 0)),
                   pl.BlockSpec((gb, grp, LANES), lambda i: (i, 0, 0))],
        out_shape=[jax.ShapeDtypeStruct((ng * grp * _slab_pitch(d // LANES), LANES), F32),
                   jax.ShapeDtypeStruct((ng, grp, LANES), F32)],
        compiler_params=_cparams(1),
    )(x3, g, modg, modg, wr3, br)


def _final_norm_kernel(x_ref, g_ref, op_ref, os_ref, *, n_prompt_blocks):
    i = pl.program_id(0)
    y = _rms(x_ref[...], g_ref[...])

    @pl.when(i < n_prompt_blocks)
    def _():
        op_ref[...] = y

    @pl.when(i >= n_prompt_blocks)
    def _():
        os_ref[...] = y


def _final_norm(x3, g, ng_prompt):
    ng, grp, d = x3.shape
    gb = _row_tile(np.gcd(ng_prompt, ng - ng_prompt), 4)
    npb = ng_prompt // gb
    blk = lambda f: pl.BlockSpec((gb, grp, d), f)
    return pl.pallas_call(
        functools.partial(_final_norm_kernel, n_prompt_blocks=npb),
        grid=(ng // gb,),
        in_specs=[blk(lambda i: (i, 0, 0)), pl.BlockSpec((1, d), lambda i: (0, 0))],
        out_specs=[blk(lambda i: (jnp.minimum(i, npb - 1), 0, 0)),
                   blk(lambda i: (jnp.maximum(i - npb, 0), 0, 0))],
        out_shape=[jax.ShapeDtypeStruct((ng_prompt, grp, d), F32),
                   jax.ShapeDtypeStruct((ng - ng_prompt, grp, d), F32)],
        compiler_params=_cparams(1),
    )(x3, g)


def _lanes(c, w):
    if w % LANES == 0:
        return c if w == LANES else jnp.tile(c, (1, w // LANES))
    return c[:, :w]


def _dot_nt(a, b):
    return lax.dot_general(a, b, (((1,), (1,)), ((), ())), preferred_element_type=F32)


def _sb_block(qh, kh, vh, c, u, masked):
    w = kh.shape[0]
    z = _dot_nt(qh, kh)
    sp = jnp.maximum(z, 0.0) + jnp.log2(1.0 + jnp.exp2(-jnp.abs(z)))
    if masked:
        row = lax.broadcasted_iota(jnp.int32, z.shape, 0)
        col = lax.broadcasted_iota(jnp.int32, z.shape, 1)
        valid = col < row
        sp = jnp.where(valid, sp, 0.0)
    hi = sp.astype(BF16)
    lo = (sp - hi.astype(F32)).astype(BF16)
    if w % LANES == 0:
        cs = jnp.dot(jnp.concatenate([hi, lo], axis=1), u, preferred_element_type=F32)
    else:
        cs = (jnp.dot(hi, u[:w], preferred_element_type=F32)
              + jnp.dot(lo, u[w:], preferred_element_type=F32))
    wgt = jnp.exp2(z - sp - cs - _lanes(c, w))
    if masked:
        wgt = jnp.where(valid, wgt, 0.0)
    o = jnp.dot(wgt.astype(BF16), vh, preferred_element_type=F32)
    c_new = c + jnp.sum(sp, axis=1, keepdims=True)
    return o, c_new


def _suffix_matrix(w):
    j = np.arange(w)[:, None]
    s = np.arange(w)[None, :]
    u = (j > s).astype(np.float32)
    return jnp.asarray(np.concatenate([u, u], axis=0), dtype=BF16)


def _sb_stream_kernel(q_ref, kn_ref, vn_ref, k_hbm, v_hbm, ud_ref, up_ref, o_ref,
                      kbuf, vbuf, acc, carry, done, alive_ref, sem, *, heads, tk, cache_tiles,
                      cache_base):
    i = pl.program_id(0)
    n_past = i if cache_tiles is None else cache_tiles

    def fetch(j, slot):
        if cache_tiles is None:
            rows = pl.ds(pl.multiple_of(j * tk, tk), tk)
            src_k, src_v = k_hbm.at[:, rows, :], v_hbm.at[:, rows, :]
        else:
            blk = tk * heads
            rows = pl.ds(pl.multiple_of((cache_base + i * cache_tiles + j) * blk, blk), blk)
            src_k, src_v = k_hbm.at[rows], v_hbm.at[rows]
        return (pltpu.make_async_copy(src_k, kbuf.at[slot], sem.at[0, slot]),
                pltpu.make_async_copy(src_v, vbuf.at[slot], sem.at[1, slot]))

    def tile_of(buf, slot, h):
        if cache_tiles is None:
            return buf[slot, h]
        return buf.at[slot][pl.ds(h, tk, stride=heads), :].astype(BF16)

    @pl.when(n_past > 0)
    def _():
        for c in fetch(n_past - 1, 0):
            c.start()

    def diag(h, _):
        o, c = _sb_block(q_ref[h], kn_ref[h], vn_ref[h], jnp.zeros(carry.shape[1:], F32),
                         ud_ref[...], True)
        acc[h] = o
        carry[h] = c
        done[h] = 0
        return 0
    lax.fori_loop(0, heads, diag, 0, unroll=min(heads, HEAD_UNROLL))

    alive_ref[0] = heads

    @pl.when(n_past > 0)
    def _():
        for c in fetch(n_past - 1, 0):
            c.wait()

        @pl.when(n_past > 1)
        def _():
            for c in fetch(n_past - 2, 1):
                c.start()

        def first(h, alive):
            o, c = _sb_block(q_ref[h], tile_of(kbuf, 0, h), tile_of(vbuf, 0, h), carry[h],
                             up_ref[...], False)
            acc[h] = acc[h] + o
            carry[h] = c
            dead = (jnp.min(c) >= SB_DEAD).astype(jnp.int32)
            done[h] = dead
            return alive + 1 - dead
        alive_ref[0] = lax.fori_loop(0, heads, first, 0, unroll=min(heads, HEAD_UNROLL))

    def cond(state):
        j, alive = state
        return (j >= 0) & (alive > 0)

    def body(state):
        j, _ = state
        slot = lax.rem(n_past - 1 - j, 2)
        for c in fetch(j, slot):
            c.wait()

        @pl.when(j > 0)
        def _():
            for c in fetch(j - 1, 1 - slot):
                c.start()

        def head(h, alive):
            @pl.when(done[h] == 0)
            def _():
                o, c = _sb_block(q_ref[h], tile_of(kbuf, slot, h), tile_of(vbuf, slot, h),
                                 carry[h], up_ref[...], False)
                acc[h] = acc[h] + o
                carry[h] = c
                done[h] = (jnp.min(c) >= SB_DEAD).astype(jnp.int32)
            return alive + 1 - done[h]
        return j - 1, lax.fori_loop(0, heads, head, 0)

    j_end, _ = lax.while_loop(cond, body, (jnp.int32(n_past - 2), alive_ref[0]))

    @pl.when(j_end >= 0)
    def _():
        for c in fetch(j_end, lax.rem(n_past - 1 - j_end, 2)):
            c.wait()

    for h in range(heads):
        o_ref[:, h * DH_SB:(h + 1) * DH_SB] = acc[h].astype(o_ref.dtype)


def _sb_stream_attention(q, kn, vn, k_src, v_src, *, tq, tk, q_block0, n_tiles, cache_tiles=None,
                         cache_base=0):
    heads = q.shape[0]
    blk = pl.BlockSpec((heads, tq, DH_SB), lambda i: (0, q_block0 + i, 0))
    anyspec = pl.BlockSpec(memory_space=pl.ANY)
    if cache_tiles is None:
        buf = pltpu.VMEM((2, heads, tk, DH_SB), BF16)
    else:
        buf = pltpu.VMEM((2, tk * heads, DH_SB), F32)
    kern = functools.partial(_sb_stream_kernel, heads=heads, tk=tk, cache_tiles=cache_tiles,
                             cache_base=cache_base)
    return pl.pallas_call(
        kern, grid=(n_tiles,),
        in_specs=[blk, blk, blk, anyspec, anyspec,
                  pl.BlockSpec((2 * tq, tq), lambda i: (0, 0)),
                  pl.BlockSpec((2 * tk, tk), lambda i: (0, 0))],
        out_specs=pl.BlockSpec((tq, heads * DH_SB), lambda i: (i, 0)),
        out_shape=jax.ShapeDtypeStruct((n_tiles * tq, heads * DH_SB), BF16),
        scratch_shapes=[buf, buf,
                        pltpu.VMEM((heads, tq, DH_SB), F32),
                        pltpu.VMEM((heads, tq, LANES), F32),
                        pltpu.SMEM((heads,), jnp.int32),
                        pltpu.SMEM((1,), jnp.int32),
                        pltpu.SemaphoreType.DMA((2, 2))],
        compiler_params=_cparams(1),
    )(q, kn, vn, k_src, v_src, _suffix_matrix(tq), _suffix_matrix(tk))


def _mla_block(qc, kc, va, m, acc, mask):
    s = _dot_nt(qc, kc)
    if mask is not None:
        s = jnp.where(mask, s, -jnp.inf)
    m_new = jnp.maximum(m, jnp.max(s, axis=1, keepdims=True))
    alpha = jnp.exp2(m - m_new)
    p = jnp.exp2(s - _lanes(m_new, s.shape[1]))
    acc_new = _lanes(alpha, acc.shape[1]) * acc + jnp.dot(p.astype(BF16), va,
                                                          preferred_element_type=F32)
    return m_new, acc_new


def _mla_kernel(qb_ref, kb_ref, fl_ref, q_ref, k_ref, v_ref, o_ref, acc, m_sc, *, heads):
    s = pl.program_id(0)
    fl = fl_ref[s]
    tq, tk = q_ref.shape[1], k_ref.shape[1]

    @pl.when((fl & 1) != 0)
    def _():
        m_sc[...] = jnp.full(m_sc.shape, -jnp.inf, F32)
        acc[...] = jnp.zeros(acc.shape, F32)

    def run(mask):
        def body(h, _):
            m, a = _mla_block(q_ref[h], k_ref[h], v_ref[h], m_sc[h], acc[h], mask)
            m_sc[h] = m
            acc[h] = a
            return 0
        lax.fori_loop(0, heads, body, 0, unroll=min(heads, MLA_UNROLL))

    @pl.when((fl & 4) != 0)
    def _():
        row = lax.broadcasted_iota(jnp.int32, (tq, tk), 0) + qb_ref[s] * tq
        col = lax.broadcasted_iota(jnp.int32, (tq, tk), 1) + kb_ref[s] * tk
        run((col // CHUNK) <= (row // CHUNK))

    @pl.when((fl & 4) == 0)
    def _():
        run(None)

    @pl.when((fl & 2) != 0)
    def _():
        for h in range(heads):
            a = acc[h]
            o_ref[:, h * V_DIM:(h + 1) * V_DIM] = (a[:, :V_DIM] / a[:, V_DIM:]).astype(o_ref.dtype)


def _mla_tables(n_rows, tq, tk):
    r = tq // tk
    qb, kb, fl = [], [], []
    for i in range(n_rows // tq):
        tiles = [(i * r + j, 4) for j in range(r)] + [(j, 0) for j in reversed(range(i * r))]
        for idx, (j, f) in enumerate(tiles):
            qb.append(i)
            kb.append(j)
            fl.append(f | (1 if idx == 0 else 0) | (2 if idx == len(tiles) - 1 else 0))
    arr = lambda v: jnp.asarray(np.asarray(v, dtype=np.int32))
    return arr(qb), arr(kb), arr(fl)


def _mla_attention(qc, kc, va, *, n_rows, tq, tk):
    qb, kb, fl = _mla_tables(n_rows, tq, tk)
    h = H_MLA
    wqk = 2 * LANES
    kv_spec = pl.BlockSpec((h, tk, wqk), lambda s, qb, kb, fl: (0, kb[s], 0))
    grid_spec = pltpu.PrefetchScalarGridSpec(
        num_scalar_prefetch=3,
        grid=(qb.shape[0],),
        in_specs=[pl.BlockSpec((h, tq, wqk), lambda s, qb, kb, fl: (0, qb[s], 0)),
                  kv_spec, kv_spec],
        out_specs=pl.BlockSpec((tq, h * V_DIM), lambda s, qb, kb, fl: (qb[s], 0)),
        scratch_shapes=[pltpu.VMEM((h, tq, 2 * V_DIM), F32),
                        pltpu.VMEM((h, tq, LANES), F32)],
    )
    return pl.pallas_call(
        functools.partial(_mla_kernel, heads=h), grid_spec=grid_spec,
        out_shape=jax.ShapeDtypeStruct((n_rows, h * V_DIM), BF16),
        compiler_params=_cparams(1),
    )(qb, kb, fl, qc, kc, va)


def _mla_latent_kernel(qb_ref, kb_ref, fl_ref, q_ref, cn_ref, rn_ref, cp_ref, rp_ref, w_ref,
                       o_ref, qa, qr, acc, m_sc, l_sc, *, heads, pos0):
    s_id = pl.program_id(0)
    fl = fl_ref[s_id]
    is_first = (fl & 1) != 0
    is_last = (fl & 2) != 0
    tq = q_ref.shape[1]
    hw = QK_NOPE + V_DIM

    def step(ck, kr, mask):
        s = _dot_nt(qa[...], ck) + _dot_nt(qr[...], kr)
        if mask is not None:
            s = jnp.where(mask, s, -jnp.inf)
        m_old = m_sc[...]
        m_new = jnp.maximum(m_old, jnp.max(s, axis=1, keepdims=True))
        alpha = jnp.exp2(m_old - m_new)
        p = jnp.exp2(s - m_new[:, :1])
        l_sc[...] = alpha * l_sc[...] + jnp.sum(p, axis=1, keepdims=True)
        acc[...] = alpha[:, :1] * acc[...] + jnp.dot(p.astype(BF16), ck,
                                                     preferred_element_type=F32)
        m_sc[...] = m_new

    @pl.when(is_first)
    def _():
        for h in range(heads):
            qh = q_ref[h]
            w_uk = w_ref[:, h * hw:h * hw + QK_NOPE]
            qa[h * tq:(h + 1) * tq, :] = _dot_nt(qh[:, :QK_NOPE], w_uk).astype(BF16)
            qr[h * tq:(h + 1) * tq, :] = qh[:, QK_NOPE:]
        m_sc[...] = jnp.full(m_sc.shape, -jnp.inf, F32)
        l_sc[...] = jnp.zeros(l_sc.shape, F32)
        acc[...] = jnp.zeros(acc.shape, F32)
        row = lax.broadcasted_iota(jnp.int32, (heads * tq, tq), 0) % tq + pos0
        col = lax.broadcasted_iota(jnp.int32, (heads * tq, tq), 1) + pos0
        step(cn_ref[...], rn_ref[...], (col // CHUNK) <= (row // CHUNK))

    @pl.when(jnp.logical_not(is_first))
    def _():
        step(cp_ref[...].astype(BF16), rp_ref[...], None)

    @pl.when(is_last)
    def _():
        o_lat = (acc[...] / l_sc[...][:, :1]).astype(BF16)
        for h in range(heads):
            w_uv = w_ref[:, h * hw + QK_NOPE:(h + 1) * hw]
            o_ref[:, h * V_DIM:(h + 1) * V_DIM] = jnp.dot(
                o_lat[h * tq:(h + 1) * tq, :], w_uv, preferred_element_type=F32).astype(o_ref.dtype)


def _mla_latent_attention(qc, c_new, r_new, c_past, r_past, w_ukv_b, tables, l, *, tq, tk, pos0,
                          out_rows, out_block_of):
    qb, kb, fl = tables
    h = H_MLA
    kvl = c_new.shape[1]
    wqk = 2 * LANES
    grid_spec = pltpu.PrefetchScalarGridSpec(
        num_scalar_prefetch=3,
        grid=(qb.shape[0],),
        in_specs=[pl.BlockSpec((h, tq, wqk), lambda s, qb, kb, fl: (0, qb[s], 0)),
                  pl.BlockSpec((tq, kvl), lambda s, qb, kb, fl: (qb[s], 0)),
                  pl.BlockSpec((tq, LANES), lambda s, qb, kb, fl: (qb[s], 0)),
                  pl.BlockSpec((tk, kvl), lambda s, qb, kb, fl: (kb[s], 0)),
                  pl.BlockSpec((tk, LANES), lambda s, qb, kb, fl: (kb[s], 0)),
                  pl.BlockSpec((None, kvl, w_ukv_b.shape[2]), lambda s, qb, kb, fl: (l, 0, 0))],
        out_specs=pl.BlockSpec((tq, h * V_DIM), out_block_of),
        scratch_shapes=[pltpu.VMEM((h * tq, kvl), BF16),
                        pltpu.VMEM((h * tq, LANES), BF16),
                        pltpu.VMEM((h * tq, kvl), F32),
                        pltpu.VMEM((h * tq, LANES), F32),
                        pltpu.VMEM((h * tq, LANES), F32)],
    )
    kern = functools.partial(_mla_latent_kernel, heads=h, pos0=pos0)
    return pl.pallas_call(
        kern, grid_spec=grid_spec,
        out_shape=jax.ShapeDtypeStruct((out_rows, h * V_DIM), BF16),
        compiler_params=_cparams(1),
    )(qb, kb, fl, qc, c_new, r_new, c_past, r_past, w_ukv_b)


def _cache_walk_tables(n_batch, q_off, past_tiles, past_stride):
    qb, kb, fl = [], [], []
    for b in range(n_batch):
        for j in range(1 + past_tiles):
            qb.append(q_off + b)
            kb.append(past_stride + b * past_tiles + past_tiles - max(j, 1))
            fl.append((1 if j == 0 else 0) | (2 if j == past_tiles else 0))
    arr = lambda v: jnp.asarray(np.asarray(v, dtype=np.int32))
    return arr(qb), arr(kb), arr(fl)


ROW_DMA_UNROLL = 8


def _moe_kernel(te_ref, nv_ref, tok_ref, dst_ref, h_hbm, w_ref, wg_ref, wu_ref, wd_ref, y_hbm,
                xbuf, obuf, wg_s, wu_s, wd_s, sem_in, sem_out, *, s):
    t = pl.program_id(0)
    tm = w_ref.shape[0]
    nv = nv_ref[0]
    live = t < nv
    slot = lax.rem(t, 2)

    p = _slab_pitch(s)

    def in_copy(tile, r, sl):
        src0 = pl.multiple_of(tok_ref[tile * tm + r] * p, SLAB_PAD)
        return pltpu.make_async_copy(h_hbm.at[pl.ds(src0, s)],
                                     xbuf.at[sl, pl.ds(pl.multiple_of(r * p, SLAB_PAD), s)],
                                     sem_in.at[sl])

    def gather_start(tile, sl):
        def body(r, _):
            in_copy(tile, r, sl).start()
            return 0
        lax.fori_loop(0, tm, body, 0, unroll=ROW_DMA_UNROLL)

    def gather_wait(tile, sl):
        def body(r, _):
            in_copy(tile, r, sl).wait()
            return 0
        lax.fori_loop(0, tm, body, 0, unroll=ROW_DMA_UNROLL)

    @pl.when(live & (t == 0))
    def _():
        gather_start(0, 0)

    @pl.when(t + 1 < nv)
    def _():
        gather_start(t + 1, 1 - slot)

    prev = te_ref[jnp.maximum(t - 1, 0)]
    fresh = (t == 0) | (te_ref[t] != prev)

    @pl.when(live & fresh)
    def _():
        wg_s[...] = wg_ref[...].astype(BF16)
        wu_s[...] = wu_ref[...].astype(BF16)
        wd_s[...] = wd_ref[...].astype(BF16)

    @pl.when(live)
    def _():
        gather_wait(t, slot)
        x = _slab_load(xbuf, 0, tm, s, lead=slot).astype(BF16)
        a = jnp.dot(x, wg_s[...], preferred_element_type=F32)
        u = jnp.dot(x, wu_s[...], preferred_element_type=F32)
        hid = (a * jax.nn.sigmoid(a)) * u * w_ref[...]
        y = jnp.dot(hid.astype(BF16), wd_s[...], preferred_element_type=F32)

        def out_copy(tile, r):
            dst0 = pl.multiple_of(dst_ref[tile * tm + r] * p, SLAB_PAD)
            return pltpu.make_async_copy(obuf.at[pl.ds(pl.multiple_of(r * p, SLAB_PAD), s)],
                                         y_hbm.at[pl.ds(dst0, s)], sem_out)

        def scatter_wait(tile):
            def wait(r, _):
                out_copy(tile, r).wait()
                return 0
            lax.fori_loop(0, tm, wait, 0, unroll=ROW_DMA_UNROLL)

        @pl.when(t > 0)
        def _():
            scatter_wait(t - 1)

        _slab_store(obuf, 0, y)

        def start(r, _):
            out_copy(t, r).start()
            return 0
        lax.fori_loop(0, tm, start, 0, unroll=ROW_DMA_UNROLL)

        @pl.when(t == nv - 1)
        def _():
            scatter_wait(t)


def _moe_experts(h_slab, row_tok, row_dst, row_w, tile_expert, n_valid, w_gate, w_up, w_down, l,
                 n_dest):
    d, f = w_gate.shape[-2:]
    s = d // LANES
    p = _slab_pitch(s)
    tm = MOE_TILE
    n_tiles = row_tok.shape[0] // tm
    wmap = lambda t, te, nv, tok, dst: (l, te[t], 0, 0)
    grid_spec = pltpu.PrefetchScalarGridSpec(
        num_scalar_prefetch=4, grid=(n_tiles,),
        in_specs=[pl.BlockSpec(memory_space=pl.ANY),
                  pl.BlockSpec((tm, 1), lambda t, te, nv, tok, dst: (t, 0)),
                  pl.BlockSpec((None, None, d, f), wmap),
                  pl.BlockSpec((None, None, d, f), wmap),
                  pl.BlockSpec((None, None, f, d), wmap)],
        out_specs=pl.BlockSpec(memory_space=pl.ANY),
        scratch_shapes=[pltpu.VMEM((2, tm * p, LANES), F32), pltpu.VMEM((tm * p, LANES), F32),
                        pltpu.VMEM((d, f), BF16), pltpu.VMEM((d, f), BF16),
                        pltpu.VMEM((f, d), BF16),
                        pltpu.SemaphoreType.DMA((2,)), pltpu.SemaphoreType.DMA(())],
    )
    return pl.pallas_call(
        functools.partial(_moe_kernel, s=s), grid_spec=grid_spec,
        out_shape=jax.ShapeDtypeStruct(((n_dest + tm) * p, LANES), F32),
        compiler_params=_cparams(1),
    )(tile_expert, n_valid, row_tok, row_dst, h_slab, row_w, w_gate, w_up, w_down)


def _combine_kernel(x_ref, y0_ref, y1_ref, g_ref, o_ref):
    gb, grp, d = x_ref.shape
    s = d // LANES
    g0 = pl.program_id(0) * gb
    for g in range(gb):
        y = _slab_load(y0_ref, g * grp, grp, s) + _slab_load(y1_ref, g * grp, grp, s)
        o_ref[g] = x_ref[g] + _mod_row(g_ref, g0 + g) * y


def _combine(x3, yg, modg, l, g_i):
    ng, grp, d = x3.shape
    s = d // LANES
    gb = _row_tile(ng, 2)
    blk = pl.BlockSpec((gb, grp, d), lambda i: (i, 0, 0))
    yblk = lambda off: pl.BlockSpec((gb * grp * _slab_pitch(s), LANES), lambda i: (i + off, 0))
    return pl.pallas_call(
        _combine_kernel,
        grid=(ng // gb,),
        in_specs=[blk, yblk(0), yblk(ng // gb), _mod_spec(g_i, ng, d)],
        out_specs=blk,
        out_shape=jax.ShapeDtypeStruct(x3.shape, F32),
        compiler_params=_cparams(1),
    )(x3, yg, yg, modg)


def _dispatch(route, n):
    tm = MOE_TILE
    r_cap = (2 * n + N_EXPERTS * (tm - 1) + tm - 1) // tm * tm
    e = route[:, :2].astype(jnp.int32)
    w = route[:, 2:4]
    flat_e = e.T.reshape(-1)
    flat_w = w.T.reshape(-1)
    order = jnp.argsort(flat_e, stable=True).astype(jnp.int32)
    bounds = jnp.searchsorted(flat_e[order], jnp.arange(N_EXPERTS + 1, dtype=jnp.int32),
                              side="left").astype(jnp.int32)
    counts = bounds[1:] - bounds[:-1]
    padded = (counts + tm - 1) // tm * tm
    ends_p = jnp.cumsum(padded)
    starts_p = ends_p - padded
    starts = jnp.cumsum(counts) - counts
    tile_start = jnp.arange(r_cap // tm, dtype=jnp.int32) * tm
    tile_expert = jnp.minimum(jnp.searchsorted(ends_p, tile_start, side="right"),
                              N_EXPERTS - 1).astype(jnp.int32)
    n_valid = (ends_p[-1:] // tm).astype(jnp.int32)
    per_row = lambda v: jnp.broadcast_to(v[:, None], (r_cap // tm, tm)).reshape(r_cap)
    local = jnp.arange(r_cap, dtype=jnp.int32) - per_row(starts_p[tile_expert])
    valid = local < per_row(counts[tile_expert])
    slot = order[jnp.clip(per_row(starts[tile_expert]) + local, 0, 2 * n - 1)]
    row_dst = jnp.where(valid, slot, 2 * n + jnp.arange(r_cap, dtype=jnp.int32) % tm)
    row_tok = jnp.where(valid, jnp.where(slot >= n, slot - n, slot), 0)
    row_w = jnp.where(valid, flat_w[slot], 0.0)
    return row_tok, row_dst, row_w.reshape(r_cap, 1), tile_expert, n_valid


def _gates_kernel(h_ref, wa_ref, wb_ref, o_ref, bsc):
    @pl.when(pl.program_id(1) == 0)
    def _():
        r = wb_ref.shape[0]
        w = wa_ref.shape[0]
        bsc[:w - r, :] = wa_ref[r:, :].astype(BF16)
        bsc[w - r:, :] = wb_ref[...].astype(BF16)

    o_ref[...] = jax.nn.sigmoid(_dot_nt(h_ref[...], bsc[...]))


def _gates(h, w_nk, l, off_kr, n_gate, tm, tn):
    n, k = h.shape
    assert off_kr % tn == 0 and n_gate % tn == 0 and tn % QK_ROPE == 0
    rb = off_kr // tn
    return pl.pallas_call(
        _gates_kernel,
        grid=(n_gate // tn, n // tm),
        in_specs=[pl.BlockSpec((tm, k), lambda j, i: (i, 0)),
                  pl.BlockSpec((None, tn, k), lambda j, i: (l, rb + j, 0)),
                  pl.BlockSpec((None, QK_ROPE, k),
                               lambda j, i: (l, (rb + j + 1) * (tn // QK_ROPE), 0))],
        out_specs=pl.BlockSpec((tm, tn), lambda j, i: (i, j)),
        out_shape=jax.ShapeDtypeStruct((n, n_gate), F32),
        scratch_shapes=[pltpu.VMEM((tn, k), BF16)],
        compiler_params=_cparams(2),
    )(h, w_nk, w_nk)


def _rope_weights_kernel(a_ref, o_ref):
    a = a_ref[...].astype(BF16)
    q = a.shape[0] // 2
    o_ref[...] = jnp.concatenate([a, a[q:], a[:q]], axis=0)


def _rope_key_weights(w_nk, off_kr):
    depth, n_in, k = w_nk.shape
    assert QK_ROPE * 2 == LANES and off_kr % QK_ROPE == 0
    return pl.pallas_call(
        _rope_weights_kernel,
        grid=(depth,),
        in_specs=[pl.BlockSpec((None, QK_ROPE, k), lambda l: (l, off_kr // QK_ROPE, 0))],
        out_specs=pl.BlockSpec((None, 2 * QK_ROPE, k), lambda l: (l, 0, 0)),
        out_shape=jax.ShapeDtypeStruct((depth, 2 * QK_ROPE, k), BF16),
        compiler_params=_cparams(1),
    )(w_nk)


def _rope_tables(pos):
    inv = ROPE_THETA ** (-jnp.arange(0, QK_ROPE, 2, dtype=F32) / QK_ROPE)
    ang = pos.astype(F32)[:, None] * inv[None, :]
    cos, sin = jnp.cos(ang), jnp.sin(ang)
    pad = jnp.zeros((pos.shape[0], LANES - QK_ROPE), F32)
    return (jnp.concatenate([cos, cos, pad], axis=1),
            jnp.concatenate([-sin, sin, pad], axis=1))


def _rope_lanes(acc, c, s):
    return acc * c + pltpu.roll(acc, LANES - QK_ROPE, axis=1) * s


def _swap_halves(w):
    half = w.shape[-1] // 2
    return jnp.concatenate([w[..., half:], w[..., :half]], axis=-1)


def kernel(x_prompt, x_sample, c_prompt, c_sample, cache_sb_k, cache_sb_v, cache_mla_ckv,
           cache_mla_krope, w_ada, b_ada, g_norm_mix, g_norm_ffn, w_in, g_q_lat, g_kv_lat,
           w_uq, w_ukv, w_branch_sb, w_branch_mla, w_out, w_router_group, b_router_group,
           w_router_expert, b_router_expert, w_exp_gate, w_exp_up, w_exp_down, g_final):
    bp, t_p, d = x_prompt.shape
    bs, t_s, _ = x_sample.shape
    depth = w_in.shape[0]
    past = cache_sb_k.shape[2]
    grp = t_s
    n_p, n_s = bp * t_p, bs * t_s
    n = n_p + n_s
    ng = n // grp
    sb_w = H_SB * DH_SB
    q_lora = g_q_lat.shape[1]
    kv_lora = g_kv_lat.shape[1]
    tm = _token_tile(n, grp)
    gpt = tm // grp
    tn = 512
    assert bp == 1 and t_p % ATTN_TILE == 0 and t_p % grp == 0 and past % 512 == 0

    x3 = jnp.concatenate([x_prompt.reshape(n_p // grp, grp, d), x_sample], axis=0)

    n_c = bp + bs
    c_rows = 16
    c_all = jnp.zeros((c_rows, d), F32).at[:n_c].set(jnp.concatenate([c_prompt, c_sample], 0))
    n_modc = N_MOD * d

    def ada_epi(accs, ex, outs):
        outs[0][...] = accs[0] + ex[0][...]

    mods = []
    for l in range(depth):
        mods.append(_matmul(
            [(c_all, w_ada, _wspec(l, d, tn, 0), True)], m=c_rows, n_out=n_modc, tm=c_rows, tn=tn,
            prologue=lambda a, ex: a * jax.nn.sigmoid(a),
            epilogue=ada_epi,
            extras=[b_ada.reshape(depth, 1, n_modc)],
            extra_specs=[pl.BlockSpec((None, 1, tn), lambda j, i, l=l: (l, 0, j))],
            out_shape=[jax.ShapeDtypeStruct((c_rows, n_modc), F32)],
            out_specs=[_spec2(c_rows, tn)])[0])
    mod = jnp.stack(mods)
    modg = jnp.concatenate(
        [jnp.broadcast_to(mod[:, :bp], (depth, n_p // grp, n_modc)), mod[:, bp:n_c]], axis=1)
    modg = modg.reshape(depth, ng, N_MOD, d).transpose(0, 2, 1, 3)
    modg = modg.reshape(depth * N_MOD, ng, d)

    def mod_idx(l, k):
        return l * N_MOD + k

    def mspec_rows(l, k):
        return pl.BlockSpec((None, ng, tn), lambda j, i: (mod_idx(l, k), 0, j))

    pos = jnp.concatenate([jnp.arange(t_p, dtype=jnp.int32),
                           jnp.tile(past + jnp.arange(t_s, dtype=jnp.int32), bs)])
    rope_c, rope_s = _rope_tables(pos)
    tq = ATTN_TILE
    tk_s = 512
    pt = past // tk_s

    off_q, off_k, off_v = 0, sb_w, 2 * sb_w
    off_cq = 3 * sb_w
    off_ckv = off_cq + q_lora
    off_kr = off_ckv + kv_lora
    off_g = off_kr + QK_ROPE
    assert off_g == off_kr + QK_ROPE
    w_nk = jnp.swapaxes(w_in, 1, 2)
    w_kr_aug = _rope_key_weights(w_nk, off_kr)
    uq = w_uq.reshape(depth, q_lora, H_MLA, QK_NOPE + QK_ROPE)
    uq_r = uq[..., QK_NOPE:]
    w_uq_cat = jnp.concatenate([uq[..., :QK_NOPE], uq_r, _swap_halves(uq_r)], axis=-1)
    w_uq_cat = w_uq_cat.reshape(depth, q_lora, H_MLA * 2 * LANES)
    w_router = jnp.concatenate(
        [w_router_expert, w_router_group,
         jnp.zeros((depth, d, LANES - N_EXPERTS - N_GROUPS), F32)], axis=-1)
    r_hi = w_router.astype(BF16)
    r_res = w_router - r_hi.astype(F32)
    r_mid = r_res.astype(BF16)
    r_lo = (r_res - r_mid.astype(F32)).astype(BF16)
    w_router3 = jnp.stack([r_hi, r_mid, r_lo], axis=1)
    b_router = jnp.concatenate(
        [b_router_expert, b_router_group,
         jnp.zeros((depth, LANES - N_EXPERTS - N_GROUPS), F32)], axis=-1).reshape(depth, 1, LANES)

    rows_past = bs * past
    kr_past = jnp.pad(cache_mla_krope.reshape(depth * rows_past, QK_ROPE),
                      ((0, 0), (0, LANES - QK_ROPE))).astype(BF16)
    w_ukv_b = w_ukv.astype(BF16)
    hm = lambda rows: jax.ShapeDtypeStruct((H_SB, rows, LANES), BF16)
    new_k, new_v, new_c, new_r = [], [], [], []

    for l in range(depth):
        h = _norm_mod(x3, g_norm_mix.reshape(depth, 1, d), modg, l, mod_idx(l, 1), mod_idx(l, 0))
        h = h.reshape(n, d)

        sbq_scale = DH_SB ** -0.5 * float(np.log2(np.e))

        def plain_hm(accs, ex, outs):
            _store_heads(outs[0], accs[0] * sbq_scale)

        def f32_and_hm(accs, ex, outs):
            outs[0][...] = accs[0]
            _store_heads(outs[1], accs[0])

        sb_q = _matmul([(h, w_nk, _wspec_nk(l, d, tn, off_q), True, True)], m=n, n_out=sb_w, tm=tm, tn=tn,
                       epilogue=plain_hm, out_shape=[hm(n)], out_specs=[_hm_spec(tm, tn)])[0]
        k_f32, sb_k = _matmul([(h, w_nk, _wspec_nk(l, d, tn, off_k), True, True)], m=n, n_out=sb_w, tm=tm,
                              tn=tn, epilogue=f32_and_hm,
                              out_shape=[jax.ShapeDtypeStruct((n, sb_w), F32), hm(n)],
                              out_specs=[_spec2(tm, tn), _hm_spec(tm, tn)])
        v_f32, sb_v = _matmul([(h, w_nk, _wspec_nk(l, d, tn, off_v), True, True)], m=n, n_out=sb_w, tm=tm,
                              tn=tn, epilogue=f32_and_hm,
                              out_shape=[jax.ShapeDtypeStruct((n, sb_w), F32), hm(n)],
                              out_specs=[_spec2(tm, tn), _hm_spec(tm, tn)])

        def plain_f32(accs, ex, outs):
            outs[0][...] = accs[0]

        c_q = _matmul([(h, w_nk, _wspec_nk(l, d, tn, off_cq), True, True)], m=n, n_out=q_lora, tm=tm, tn=tn,
                      epilogue=plain_f32, out_shape=[jax.ShapeDtypeStruct((n, q_lora), F32)],
                      out_specs=[_spec2(tm, tn)])[0]

        def ckv_epi(accs, ex, outs):
            y = _rms(accs[0], ex[0][...])
            outs[0][...] = y
            outs[1][...] = y.astype(BF16)

        c_kv, c_kv_b = _matmul(
            [(h, w_nk, _wspec_nk(l, d, kv_lora, off_ckv), True, True)], m=n, n_out=kv_lora, tm=tm, tn=kv_lora,
            epilogue=ckv_epi, extras=[g_kv_lat.reshape(depth, 1, kv_lora)],
            extra_specs=[pl.BlockSpec((None, 1, kv_lora), lambda j, i: (l, 0, 0))],
            out_shape=[jax.ShapeDtypeStruct((n, kv_lora), F32),
                       jax.ShapeDtypeStruct((n, kv_lora), BF16)],
            out_specs=[_spec2(tm, kv_lora), _spec2(tm, kv_lora)])

        def kr_epi(accs, ex, outs):
            r = _rope_lanes(accs[0], ex[0][...], ex[1][...])
            outs[0][...] = r
            outs[1][...] = r.astype(BF16)

        rope_specs = [pl.BlockSpec((tm, LANES), lambda j, i: (i, 0))] * 2
        k_r, k_r_b = _matmul(
            [(h, w_kr_aug, pl.BlockSpec((None, LANES, d), lambda j, i: (l, 0, 0)), False, True)],
            m=n, n_out=LANES, tm=tm, tn=LANES, epilogue=kr_epi,
            extras=[rope_c, rope_s], extra_specs=rope_specs,
            out_shape=[jax.ShapeDtypeStruct((n, LANES), F32), jax.ShapeDtypeStruct((n, LANES), BF16)],
            out_specs=[_spec2(tm, LANES), _spec2(tm, LANES)])

        gates = _gates(h, w_nk, l, off_kr, 2 * d, tm, tn)

        def cq_prologue(a, ex):
            return _rms(a, ex[0][...])

        gq_spec = pl.BlockSpec((None, 1, q_lora), lambda j, i: (l, 0, 0))
        gq = g_q_lat.reshape(depth, 1, q_lora)
        wqk = 2 * LANES
        hm_qk = lambda rows: jax.ShapeDtypeStruct((H_MLA, rows, wqk), BF16)

        q_scale = (QK_NOPE + QK_ROPE) ** -0.5 * float(np.log2(np.e))

        def qcat_epi(accs, ex, outs):
            c, s = ex[1][...], ex[2][...]
            for hh in range(tn // wqk):
                blk = accs[0][:, hh * wqk:(hh + 1) * wqk]
                outs[0][hh, :, :LANES] = (blk[:, :LANES] * q_scale).astype(BF16)
                outs[0][hh, :, LANES:] = (_rope_lanes(blk[:, LANES:], c, s) * q_scale).astype(BF16)

        q_cat = _matmul(
            [(c_q, w_uq_cat, pl.BlockSpec((None, q_lora, tn), lambda j, i: (l, 0, j)), True)],
            m=n, n_out=H_MLA * wqk, tm=tm, tn=tn, prologue=cq_prologue, epilogue=qcat_epi,
            extras=[gq, rope_c, rope_s], extra_specs=[gq_spec] + rope_specs,
            out_shape=[hm_qk(n)],
            out_specs=[pl.BlockSpec((tn // wqk, tm, wqk), lambda j, i: (j, i, 0))])[0]

        tn_kv = min(2048, H_MLA * wqk)
        hpt = tn_kv // wqk

        def kv_epi(accs, ex, outs):
            kr = ex[0][...]
            ones = jnp.ones(kr.shape, BF16)
            for hh in range(hpt):
                outs[0][hh, :, :LANES] = accs[0][:, hh * wqk:hh * wqk + LANES].astype(BF16)
                outs[0][hh, :, LANES:] = kr
                outs[1][hh, :, :LANES] = accs[0][:, hh * wqk + LANES:(hh + 1) * wqk].astype(BF16)
                outs[1][hh, :, LANES:] = ones

        def up_kv(a, kr, rows, tmr):
            hspec = pl.BlockSpec((hpt, tmr, wqk), lambda j, i: (j, i, 0))
            return _matmul(
                [(a, w_ukv, pl.BlockSpec((None, kv_lora, tn_kv), lambda j, i: (l, 0, j)), True)],
                m=rows, n_out=H_MLA * wqk, tm=tmr, tn=tn_kv, epilogue=kv_epi, extras=[kr],
                extra_specs=[pl.BlockSpec((tmr, LANES), lambda j, i: (i, 0))],
                out_shape=[hm_qk(rows), hm_qk(rows)], out_specs=[hspec, hspec])

        kc_new, v_new = up_kv(c_kv_b, k_r_b, n, tm)

        o_sb_p = _sb_stream_attention(sb_q, sb_k, sb_v, sb_k, sb_v, tq=tq, tk=tq, q_block0=0,
                                      n_tiles=n_p // tq)
        cache_k2 = cache_sb_k.reshape(depth * bs * past * H_SB, DH_SB)
        cache_v2 = cache_sb_v.reshape(depth * bs * past * H_SB, DH_SB)
        q_off_s = n_p // t_s
        o_sb_s = _sb_stream_attention(sb_q, sb_k, sb_v, cache_k2, cache_v2, tq=t_s, tk=ATTN_TILE,
                                      q_block0=q_off_s, n_tiles=bs,
                                      cache_tiles=past // ATTN_TILE,
                                      cache_base=l * bs * (past // ATTN_TILE))
        o_sb = jnp.concatenate([o_sb_p, o_sb_s], axis=0)

        o_mla_p = _mla_attention(q_cat, kc_new, v_new, n_rows=n_p,
                                 tq=min(MLA_Q_TILE, n_p), tk=min(MLA_Q_TILE, n_p))
        o_mla_s = _mla_latent_attention(
            q_cat, c_kv_b, k_r_b, cache_mla_ckv.reshape(depth * rows_past, kv_lora), kr_past,
            w_ukv_b, _cache_walk_tables(bs, q_off_s, pt, l * bs * pt), l,
            tq=t_s, tk=tk_s, pos0=past, out_rows=n_s,
            out_block_of=lambda s, qb, kb, fl: (qb[s] - q_off_s, 0))
        o_mla = jnp.concatenate([o_mla_p, o_mla_s], axis=0)

        def merge_epi(accs, ex, outs):
            outs[0][...] = (ex[0][...] * accs[0] + ex[1][...] * accs[1]).astype(BF16)

        merged = _matmul(
            [(o_sb, w_branch_sb, _wspec(l, sb_w, tn, 0), True),
             (o_mla, w_branch_mla, _wspec(l, H_MLA * V_DIM, tn, 0), True)],
            m=n, n_out=d, tm=tm, tn=tn, epilogue=merge_epi,
            extras=[gates, gates],
            extra_specs=[pl.BlockSpec((tm, tn), lambda j, i: (i, j)),
                         pl.BlockSpec((tm, tn), lambda j, i: (i, j + d // tn))],
            out_shape=[jax.ShapeDtypeStruct((n, d), BF16)], out_specs=[_spec2(tm, tn)])[0]

        def resid_epi(accs, ex, outs):
            g0 = pl.program_id(1) * gpt
            for g in range(gpt):
                outs[0][g] = ex[0][g] + _mod_row(ex[1], g0 + g) * accs[0][g * grp:(g + 1) * grp, :]

        x_spec = pl.BlockSpec((gpt, grp, tn), lambda j, i: (i, 0, j))
        x3 = _matmul(
            [(merged, w_out, _wspec(l, d, tn, 0), True)], m=n, n_out=d, tm=tm, tn=tn,
            epilogue=resid_epi, extras=[x3, modg], extra_specs=[x_spec, mspec_rows(l, 2)],
            out_shape=[jax.ShapeDtypeStruct((ng, grp, d), F32)], out_specs=[x_spec])[0]

        h2, route = _norm_route(x3, g_norm_ffn.reshape(depth, 1, d), modg, l, mod_idx(l, 4),
                                mod_idx(l, 3), w_router3, b_router)
        row_tok, row_dst, row_w, tile_expert, n_valid = _dispatch(route.reshape(n, LANES), n)
        yg = _moe_experts(h2, row_tok, row_dst, row_w, tile_expert, n_valid,
                          w_exp_gate, w_exp_up, w_exp_down, l, 2 * n)
        x3 = _combine(x3, yg, modg, l, mod_idx(l, 5))

        new_k.append(k_f32)
        new_v.append(v_f32)
        new_c.append(c_kv)
        new_r.append(k_r[:, :QK_ROPE])

    y_p, y_s = _final_norm(x3, g_final.reshape(1, d), n_p // grp)

    def split(parts, tail):
        a = jnp.stack(parts)
        return (a[:, :n_p].reshape((depth, bp, t_p) + tail),
                a[:, n_p:].reshape((depth, bs, t_s) + tail))

    pk, sk = split(new_k, (H_SB, DH_SB))
    pv, sv = split(new_v, (H_SB, DH_SB))
    pc, sc = split(new_c, (kv_lora,))
    pr, sr = split(new_r, (QK_ROPE,))
    return (y_p.reshape(bp, t_p, d), y_s.reshape(bs, t_s, d), pk, pv, pc, pr, sk, sv, sc, sr)
```

```python
import functools

import numpy as np
import jax
import jax.numpy as jnp
from jax import lax
from jax.experimental import pallas as pl
from jax.experimental.pallas import tpu as pltpu

F32 = jnp.float32
BF16 = jnp.bfloat16

CHUNK = 64
H_SB = 16
DH_SB = 128
H_MLA = 16
QK_NOPE = 128
QK_ROPE = 64
V_DIM = 128
ROPE_THETA = 10000.0
N_GROUPS = 4
EXPERTS_PER_GROUP = 8
N_EXPERTS = N_GROUPS * EXPERTS_PER_GROUP
N_MOD = 6
EPS = 1e-6

LANES = 128
ATTN_TILE = 256
MOE_TILE = 256
HEAD_UNROLL = 4
MLA_UNROLL = 8
MLA_Q_TILE = 512
SLAB_PAD = 4
SB_DEAD = 152.0
VMEM_LIMIT = 56 * 1024 * 1024


def _cparams(n_axes, vmem=VMEM_LIMIT):
    return pltpu.CompilerParams(dimension_semantics=("arbitrary",) * n_axes,
                                vmem_limit_bytes=vmem)


def _row_tile(n, cap=512):
    t = cap
    while n % t:
        t //= 2
    return t


def _token_tile(n, grp, cap=1152):
    return max(t for t in range(grp, cap + 1, grp) if n % t == 0)


def _matmul(pairs, *, m, n_out, tm, tn, epilogue, out_shape, out_specs,
            extras=(), extra_specs=(), prologue=None):
    n_pairs = len(pairs)
    n_ex = len(extras)
    n_outs = len(out_shape)
    pairs = [tuple(p) + (False,) * (5 - len(p)) for p in pairs]
    cast = [p[3] for p in pairs]
    b_nk = [p[4] for p in pairs]

    def kern(*refs):
        a_refs = refs[0:2 * n_pairs:2]
        b_refs = refs[1:2 * n_pairs:2]
        ex = refs[2 * n_pairs:2 * n_pairs + n_ex]
        outs = refs[2 * n_pairs + n_ex:2 * n_pairs + n_ex + n_outs]
        scr = refs[2 * n_pairs + n_ex + n_outs:]
        i = pl.program_id(1)
        accs = []
        si = 0
        for p in range(n_pairs):
            if cast[p]:
                bsc = scr[si]
                si += 1

                @pl.when(i == 0)
                def _(bsc=bsc, b_ref=b_refs[p]):
                    bsc[...] = b_ref[...].astype(BF16)

                bv = bsc[...]
            else:
                bv = b_refs[p][...]
            a = a_refs[p][...]
            if prologue is not None:
                a = prologue(a, ex)
            if b_nk[p]:
                accs.append(_dot_nt(a.astype(BF16), bv))
            else:
                accs.append(jnp.dot(a.astype(BF16), bv, preferred_element_type=F32))
        epilogue(accs, ex, outs)

    in_specs, args, scratch = [], [], []
    for (a, b, b_spec, cb, nk) in pairs:
        k = a.shape[1]
        in_specs += [pl.BlockSpec((tm, k), lambda j, i: (i, 0)), b_spec]
        args += [a, b]
        if cb:
            scratch.append(pltpu.VMEM((tn, k) if nk else (k, tn), BF16))
    in_specs += list(extra_specs)
    args += list(extras)
    return pl.pallas_call(
        kern,
        grid=(n_out // tn, m // tm),
        in_specs=in_specs,
        out_specs=out_specs,
        out_shape=out_shape,
        scratch_shapes=scratch,
        compiler_params=_cparams(2),
    )(*args)


def _wspec(l, k, tn, col_off):
    cb = col_off // tn
    assert cb * tn == col_off
    return pl.BlockSpec((None, k, tn), lambda j, i: (l, 0, cb + j))


def _wspec_nk(l, k, tn, row_off):
    rb = row_off // tn
    assert rb * tn == row_off
    return pl.BlockSpec((None, tn, k), lambda j, i: (l, rb + j, 0))


def _spec2(tm, tn):
    return pl.BlockSpec((tm, tn), lambda j, i: (i, j))


def _hm_spec(tm, tn):
    return pl.BlockSpec((tn // LANES, tm, LANES), lambda j, i: (j, i, 0))


def _store_heads(o_ref, val):
    for c in range(val.shape[1] // LANES):
        o_ref[c] = val[:, c * LANES:(c + 1) * LANES].astype(o_ref.dtype)


def _rms(x, g):
    return x * lax.rsqrt(jnp.mean(x * x, axis=-1, keepdims=True) + EPS) * g


def _mod_row(ref, g):
    return ref[pl.ds(g, 1), :]


def _norm_mod_kernel(x_ref, g_ref, sc_ref, sh_ref, o_ref):
    gb = x_ref.shape[0]
    g0 = pl.program_id(0) * gb
    for g in range(gb):
        y = _rms(x_ref[g], g_ref[...])
        o_ref[g] = (y * (1.0 + _mod_row(sc_ref, g0 + g)) + _mod_row(sh_ref, g0 + g)).astype(o_ref.dtype)


def _split3(x):
    hi = x.astype(BF16)
    r = x - hi.astype(F32)
    mid = r.astype(BF16)
    lo = (r - mid.astype(F32)).astype(BF16)
    return hi, mid, lo


def _dot_f32(a, b3):
    a_hi, a_mid, a_lo = _split3(a)
    b_hi, b_mid, b_lo = b3
    d = functools.partial(jnp.dot, preferred_element_type=F32)
    small = d(a_hi, b_lo) + d(a_lo, b_hi) + d(a_mid, b_mid)
    return (d(a_hi, b_hi) + (d(a_hi, b_mid) + d(a_mid, b_hi))) + small


def _route(logits):
    lane = lax.broadcasted_iota(jnp.int32, logits.shape, 1)
    lanef = lane.astype(F32)
    big = jnp.float32(1e9)
    ninf = jnp.float32(-jnp.inf)
    is_g = (lane >= N_EXPERTS) & (lane < N_EXPERTS + N_GROUPS)
    gl = jnp.where(is_g, logits, ninf)
    gmax = jnp.max(gl, axis=1, keepdims=True)
    g_idx = jnp.min(jnp.where(gl == gmax, lanef - N_EXPERTS, big), axis=1, keepdims=True)
    p_group = 1.0 / jnp.sum(jnp.where(is_g, jnp.exp(gl - gmax), 0.0), axis=1, keepdims=True)
    grp = jnp.floor(lanef * (1.0 / EXPERTS_PER_GROUP))
    in_g = (lane < N_EXPERTS) & (grp == g_idx)
    el = jnp.where(in_g, logits, ninf)
    e1 = jnp.max(el, axis=1, keepdims=True)
    i1 = jnp.min(jnp.where(el == e1, lanef, big), axis=1, keepdims=True)
    el2 = jnp.where(lanef == i1, ninf, el)
    e2 = jnp.max(el2, axis=1, keepdims=True)
    i2 = jnp.min(jnp.where(el2 == e2, lanef, big), axis=1, keepdims=True)
    t = jnp.exp(e2 - e1)
    den = 1.0 + t
    w1 = (1.0 / den) * p_group
    w2 = (t / den) * p_group
    out = jnp.where(lane == 0, i1, jnp.where(lane == 1, i2,
          jnp.where(lane == 2, w1, jnp.where(lane == 3, w2, 0.0))))
    return out


def _slab_pitch(s):
    return s + SLAB_PAD


def _slab_store(ref, row0, val):
    rows, d = val.shape
    s = d // LANES
    p = _slab_pitch(s)
    for c in range(s):
        ref[pl.ds(row0 * p + c, rows, stride=p), :] = val[:, c * LANES:(c + 1) * LANES]


def _slab_load(ref, row0, rows, s, lead=None):
    pieces = []
    p = _slab_pitch(s)
    for c in range(s):
        rs = pl.ds(row0 * p + c, rows, stride=p)
        pieces.append(ref[rs, :] if lead is None else ref[lead, rs, :])
    return jnp.concatenate(pieces, axis=1)


def _norm_route_kernel(x_ref, g_ref, sc_ref, sh_ref, wr_ref, br_ref, h_ref, r_ref):
    gb, grp, _ = x_ref.shape
    g0 = pl.program_id(0) * gb
    b3 = (wr_ref[0], wr_ref[1], wr_ref[2])
    for g in range(gb):
        y = _rms(x_ref[g], g_ref[...])
        h = y * (1.0 + _mod_row(sc_ref, g0 + g)) + _mod_row(sh_ref, g0 + g)
        _slab_store(h_ref, g * grp, h)
        logits = _dot_f32(h, b3) + br_ref[...]
        r_ref[g] = _route(logits)


def _mod_spec(idx, ng, d):
    return pl.BlockSpec((None, ng, d), lambda i: (idx, 0, 0))


def _norm_mod(x3, g, modg, l, sc_i, sh_i):
    ng, grp, d = x3.shape
    gb = _row_tile(ng, 4)
    return pl.pallas_call(
        _norm_mod_kernel,
        grid=(ng // gb,),
        in_specs=[pl.BlockSpec((gb, grp, d), lambda i: (i, 0, 0)),
                  pl.BlockSpec((None, 1, d), lambda i: (l, 0, 0)),
                  _mod_spec(sc_i, ng, d), _mod_spec(sh_i, ng, d)],
        out_specs=pl.BlockSpec((gb, grp, d), lambda i: (i, 0, 0)),
        out_shape=jax.ShapeDtypeStruct(x3.shape, BF16),
        compiler_params=_cparams(1),
    )(x3, g, modg, modg)


def _norm_route(x3, g, modg, l, sc_i, sh_i, wr3, br):
    ng, grp, d = x3.shape
    gb = _row_tile(ng, 4)
    return pl.pallas_call(
        _norm_route_kernel,
        grid=(ng // gb,),
        in_specs=[pl.BlockSpec((gb, grp, d), lambda i: (i, 0, 0)),
                  pl.BlockSpec((None, 1, d), lambda i: (l, 0, 0)),
                  _mod_spec(sc_i, ng, d), _mod_spec(sh_i, ng, d),
                  pl.BlockSpec((None, 3, d, LANES), lambda i: (l, 0, 0, 0)),
                  pl.BlockSpec((None, 1, LANES), lambda i: (l, 0, 0))],
        out_specs=[pl.BlockSpec((gb * grp * _slab_pitch(d // LANES), LANES), lambda i: (i, 0)),
                   pl.BlockSpec((gb, grp, LANES), lambda i: (i, 0, 0))],
        out_shape=[jax.ShapeDtypeStruct((ng * grp * _slab_pitch(d // LANES), LANES), F32),
                   jax.ShapeDtypeStruct((ng, grp, LANES), F32)],
        compiler_params=_cparams(1),
    )(x3, g, modg, modg, wr3, br)


def _final_norm_kernel(x_ref, g_ref, op_ref, os_ref, *, n_prompt_blocks):
    i = pl.program_id(0)
    y = _rms(x_ref[...], g_ref[...])

    @pl.when(i < n_prompt_blocks)
    def _():
        op_ref[...] = y

    @pl.when(i >= n_prompt_blocks)
    def _():
        os_ref[...] = y


def _final_norm(x3, g, ng_prompt):
    ng, grp, d = x3.shape
    gb = _row_tile(np.gcd(ng_prompt, ng - ng_prompt), 4)
    npb = ng_prompt // gb
    blk = lambda f: pl.BlockSpec((gb, grp, d), f)
    return pl.pallas_call(
        functools.partial(_final_norm_kernel, n_prompt_blocks=npb),
        grid=(ng // gb,),
        in_specs=[blk(lambda i: (i, 0, 0)), pl.BlockSpec((1, d), lambda i: (0, 0))],
        out_specs=[blk(lambda i: (jnp.minimum(i, npb - 1), 0, 0)),
                   blk(lambda i: (jnp.maximum(i - npb, 0), 0, 0))],
        out_shape=[jax.ShapeDtypeStruct((ng_prompt, grp, d), F32),
                   jax.ShapeDtypeStruct((ng - ng_prompt, grp, d), F32)],
        compiler_params=_cparams(1),
    )(x3, g)


def _lanes(c, w):
    if w % LANES == 0:
        return c if w == LANES else jnp.tile(c, (1, w // LANES))
    return c[:, :w]


def _dot_nt(a, b):
    return lax.dot_general(a, b, (((1,), (1,)), ((), ())), preferred_element_type=F32)


def _sb_block(qh, kh, vh, c, u, masked):
    w = kh.shape[0]
    z = _dot_nt(qh, kh)
    sp = jnp.maximum(z, 0.0) + jnp.log2(1.0 + jnp.exp2(-jnp.abs(z)))
    if masked:
        row = lax.broadcasted_iota(jnp.int32, z.shape, 0)
        col = lax.broadcasted_iota(jnp.int32, z.shape, 1)
        valid = col < row
        sp = jnp.where(valid, sp, 0.0)
    hi = sp.astype(BF16)
    lo = (sp - hi.astype(F32)).astype(BF16)
    if w % LANES == 0:
        cs = jnp.dot(jnp.concatenate([hi, lo], axis=1), u, preferred_element_type=F32)
    else:
        cs = (jnp.dot(hi, u[:w], preferred_element_type=F32)
              + jnp.dot(lo, u[w:], preferred_element_type=F32))
    wgt = jnp.exp2(z - sp - cs - _lanes(c, w))
    if masked:
        wgt = jnp.where(valid, wgt, 0.0)
    o = jnp.dot(wgt.astype(BF16), vh, preferred_element_type=F32)
    c_new = c + jnp.sum(sp, axis=1, keepdims=True)
    return o, c_new


def _suffix_matrix(w):
    j = np.arange(w)[:, None]
    s = np.arange(w)[None, :]
    u = (j > s).astype(np.float32)
    return jnp.asarray(np.concatenate([u, u], axis=0), dtype=BF16)


def _sb_stream_kernel(q_ref, kn_ref, vn_ref, k_hbm, v_hbm, ud_ref, up_ref, o_ref,
                      kbuf, vbuf, acc, carry, done, alive_ref, sem, *, heads, tk, cache_tiles,
                      cache_base):
    i = pl.program_id(0)
    n_past = i if cache_tiles is None else cache_tiles

    def fetch(j, slot):
        if cache_tiles is None:
            rows = pl.ds(pl.multiple_of(j * tk, tk), tk)
            src_k, src_v = k_hbm.at[:, rows, :], v_hbm.at[:, rows, :]
        else:
            blk = tk * heads
            rows = pl.ds(pl.multiple_of((cache_base + i * cache_tiles + j) * blk, blk), blk)
            src_k, src_v = k_hbm.at[rows], v_hbm.at[rows]
        return (pltpu.make_async_copy(src_k, kbuf.at[slot], sem.at[0, slot]),
                pltpu.make_async_copy(src_v, vbuf.at[slot], sem.at[1, slot]))

    def tile_of(buf, slot, h):
        if cache_tiles is None:
            return buf[slot, h]
        return buf.at[slot][pl.ds(h, tk, stride=heads), :].astype(BF16)

    @pl.when(n_past > 0)
    def _():
        for c in fetch(n_past - 1, 0):
            c.start()

    def diag(h, _):
        o, c = _sb_block(q_ref[h], kn_ref[h], vn_ref[h], jnp.zeros(carry.shape[1:], F32),
                         ud_ref[...], True)
        acc[h] = o
        carry[h] = c
        done[h] = 0
        return 0
    lax.fori_loop(0, heads, diag, 0, unroll=min(heads, HEAD_UNROLL))

    alive_ref[0] = heads

    @pl.when(n_past > 0)
    def _():
        for c in fetch(n_past - 1, 0):
            c.wait()

        @pl.when(n_past > 1)
        def _():
            for c in fetch(n_past - 2, 1):
                c.start()

        def first(h, alive):
            o, c = _sb_block(q_ref[h], tile_of(kbuf, 0, h), tile_of(vbuf, 0, h), carry[h],
                             up_ref[...], False)
            acc[h] = acc[h] + o
            carry[h] = c
            dead = (jnp.min(c) >= SB_DEAD).astype(jnp.int32)
            done[h] = dead
            return alive + 1 - dead
        alive_ref[0] = lax.fori_loop(0, heads, first, 0, unroll=min(heads, HEAD_UNROLL))

    def cond(state):
        j, alive = state
        return (j >= 0) & (alive > 0)

    def body(state):
        j, _ = state
        slot = lax.rem(n_past - 1 - j, 2)
        for c in fetch(j, slot):
            c.wait()

        @pl.when(j > 0)
        def _():
            for c in fetch(j - 1, 1 - slot):
                c.start()

        def head(h, alive):
            @pl.when(done[h] == 0)
            def _():
                o, c = _sb_block(q_ref[h], tile_of(kbuf, slot, h), tile_of(vbuf, slot, h),
                                 carry[h], up_ref[...], False)
                acc[h] = acc[h] + o
                carry[h] = c
                done[h] = (jnp.min(c) >= SB_DEAD).astype(jnp.int32)
            return alive + 1 - done[h]
        return j - 1, lax.fori_loop(0, heads, head, 0)

    j_end, _ = lax.while_loop(cond, body, (jnp.int32(n_past - 2), alive_ref[0]))

    @pl.when(j_end >= 0)
    def _():
        for c in fetch(j_end, lax.rem(n_past - 1 - j_end, 2)):
            c.wait()

    for h in range(heads):
        o_ref[:, h * DH_SB:(h + 1) * DH_SB] = acc[h].astype(o_ref.dtype)


def _sb_stream_attention(q, kn, vn, k_src, v_src, *, tq, tk, q_block0, n_tiles, cache_tiles=None,
                         cache_base=0):
    heads = q.shape[0]
    blk = pl.BlockSpec((heads, tq, DH_SB), lambda i: (0, q_block0 + i, 0))
    anyspec = pl.BlockSpec(memory_space=pl.ANY)
    if cache_tiles is None:
        buf = pltpu.VMEM((2, heads, tk, DH_SB), BF16)
    else:
        buf = pltpu.VMEM((2, tk * heads, DH_SB), F32)
    kern = functools.partial(_sb_stream_kernel, heads=heads, tk=tk, cache_tiles=cache_tiles,
                             cache_base=cache_base)
    return pl.pallas_call(
        kern, grid=(n_tiles,),
        in_specs=[blk, blk, blk, anyspec, anyspec,
                  pl.BlockSpec((2 * tq, tq), lambda i: (0, 0)),
                  pl.BlockSpec((2 * tk, tk), lambda i: (0, 0))],
        out_specs=pl.BlockSpec((tq, heads * DH_SB), lambda i: (i, 0)),
        out_shape=jax.ShapeDtypeStruct((n_tiles * tq, heads * DH_SB), BF16),
        scratch_shapes=[buf, buf,
                        pltpu.VMEM((heads, tq, DH_SB), F32),
                        pltpu.VMEM((heads, tq, LANES), F32),
                        pltpu.SMEM((heads,), jnp.int32),
                        pltpu.SMEM((1,), jnp.int32),
                        pltpu.SemaphoreType.DMA((2, 2))],
        compiler_params=_cparams(1),
    )(q, kn, vn, k_src, v_src, _suffix_matrix(tq), _suffix_matrix(tk))


def _mla_block(qc, kc, va, m, acc, mask):
    s = _dot_nt(qc, kc)
    if mask is not None:
        s = jnp.where(mask, s, -jnp.inf)
    m_new = jnp.maximum(m, jnp.max(s, axis=1, keepdims=True))
    alpha = jnp.exp2(m - m_new)
    p = jnp.exp2(s - _lanes(m_new, s.shape[1]))
    acc_new = _lanes(alpha, acc.shape[1]) * acc + jnp.dot(p.astype(BF16), va,
                                                          preferred_element_type=F32)
    return m_new, acc_new


def _mla_kernel(qb_ref, kb_ref, fl_ref, q_ref, k_ref, v_ref, o_ref, acc, m_sc, *, heads):
    s = pl.program_id(0)
    fl = fl_ref[s]
    tq, tk = q_ref.shape[1], k_ref.shape[1]

    @pl.when((fl & 1) != 0)
    def _():
        m_sc[...] = jnp.full(m_sc.shape, -jnp.inf, F32)
        acc[...] = jnp.zeros(acc.shape, F32)

    def run(mask):
        def body(h, _):
            m, a = _mla_block(q_ref[h], k_ref[h], v_ref[h], m_sc[h], acc[h], mask)
            m_sc[h] = m
            acc[h] = a
            return 0
        lax.fori_loop(0, heads, body, 0, unroll=min(heads, MLA_UNROLL))

    @pl.when((fl & 4) != 0)
    def _():
        row = lax.broadcasted_iota(jnp.int32, (tq, tk), 0) + qb_ref[s] * tq
        col = lax.broadcasted_iota(jnp.int32, (tq, tk), 1) + kb_ref[s] * tk
        run((col // CHUNK) <= (row // CHUNK))

    @pl.when((fl & 4) == 0)
    def _():
        run(None)

    @pl.when((fl & 2) != 0)
    def _():
        for h in range(heads):
            a = acc[h]
            o_ref[:, h * V_DIM:(h + 1) * V_DIM] = (a[:, :V_DIM] / a[:, V_DIM:]).astype(o_ref.dtype)


def _mla_tables(n_rows, tq, tk):
    r = tq // tk
    qb, kb, fl = [], [], []
    for i in range(n_rows // tq):
        tiles = [(i * r + j, 4) for j in range(r)] + [(j, 0) for j in reversed(range(i * r))]
        for idx, (j, f) in enumerate(tiles):
            qb.append(i)
            kb.append(j)
            fl.append(f | (1 if idx == 0 else 0) | (2 if idx == len(tiles) - 1 else 0))
    arr = lambda v: jnp.asarray(np.asarray(v, dtype=np.int32))
    return arr(qb), arr(kb), arr(fl)


def _mla_attention(qc, kc, va, *, n_rows, tq, tk):
    qb, kb, fl = _mla_tables(n_rows, tq, tk)
    h = H_MLA
    wqk = 2 * LANES
    kv_spec = pl.BlockSpec((h, tk, wqk), lambda s, qb, kb, fl: (0, kb[s], 0))
    grid_spec = pltpu.PrefetchScalarGridSpec(
        num_scalar_prefetch=3,
        grid=(qb.shape[0],),
        in_specs=[pl.BlockSpec((h, tq, wqk), lambda s, qb, kb, fl: (0, qb[s], 0)),
                  kv_spec, kv_spec],
        out_specs=pl.BlockSpec((tq, h * V_DIM), lambda s, qb, kb, fl: (qb[s], 0)),
        scratch_shapes=[pltpu.VMEM((h, tq, 2 * V_DIM), F32),
                        pltpu.VMEM((h, tq, LANES), F32)],
    )
    return pl.pallas_call(
        functools.partial(_mla_kernel, heads=h), grid_spec=grid_spec,
        out_shape=jax.ShapeDtypeStruct((n_rows, h * V_DIM), BF16),
        compiler_params=_cparams(1),
    )(qb, kb, fl, qc, kc, va)


def _mla_latent_kernel(qb_ref, kb_ref, fl_ref, q_ref, cn_ref, rn_ref, cp_ref, rp_ref, w_ref,
                       o_ref, qa, qr, acc, m_sc, l_sc, *, heads, pos0):
    s_id = pl.program_id(0)
    fl = fl_ref[s_id]
    is_first = (fl & 1) != 0
    is_last = (fl & 2) != 0
    tq = q_ref.shape[1]
    hw = QK_NOPE + V_DIM

    def step(ck, kr, mask):
        s = _dot_nt(qa[...], ck) + _dot_nt(qr[...], kr)
        if mask is not None:
            s = jnp.where(mask, s, -jnp.inf)
        m_old = m_sc[...]
        m_new = jnp.maximum(m_old, jnp.max(s, axis=1, keepdims=True))
        alpha = jnp.exp2(m_old - m_new)
        p = jnp.exp2(s - m_new[:, :1])
        l_sc[...] = alpha * l_sc[...] + jnp.sum(p, axis=1, keepdims=True)
        acc[...] = alpha[:, :1] * acc[...] + jnp.dot(p.astype(BF16), ck,
                                                     preferred_element_type=F32)
        m_sc[...] = m_new

    @pl.when(is_first)
    def _():
        for h in range(heads):
            qh = q_ref[h]
            w_uk = w_ref[:, h * hw:h * hw + QK_NOPE]
            qa[h * tq:(h + 1) * tq, :] = _dot_nt(qh[:, :QK_NOPE], w_uk).astype(BF16)
            qr[h * tq:(h + 1) * tq, :] = qh[:, QK_NOPE:]
        m_sc[...] = jnp.full(m_sc.shape, -jnp.inf, F32)
        l_sc[...] = jnp.zeros(l_sc.shape, F32)
        acc[...] = jnp.zeros(acc.shape, F32)
        row = lax.broadcasted_iota(jnp.int32, (heads * tq, tq), 0) % tq + pos0
        col = lax.broadcasted_iota(jnp.int32, (heads * tq, tq), 1) + pos0
        step(cn_ref[...], rn_ref[...], (col // CHUNK) <= (row // CHUNK))

    @pl.when(jnp.logical_not(is_first))
    def _():
        step(cp_ref[...].astype(BF16), rp_ref[...], None)

    @pl.when(is_last)
    def _():
        o_lat = (acc[...] / l_sc[...][:, :1]).astype(BF16)
        for h in range(heads):
            w_uv = w_ref[:, h * hw + QK_NOPE:(h + 1) * hw]
            o_ref[:, h * V_DIM:(h + 1) * V_DIM] = jnp.dot(
                o_lat[h * tq:(h + 1) * tq, :], w_uv, preferred_element_type=F32).astype(o_ref.dtype)


def _mla_latent_attention(qc, c_new, r_new, c_past, r_past, w_ukv_b, tables, l, *, tq, tk, pos0,
                          out_rows, out_block_of):
    qb, kb, fl = tables
    h = H_MLA
    kvl = c_new.shape[1]
    wqk = 2 * LANES
    grid_spec = pltpu.PrefetchScalarGridSpec(
        num_scalar_prefetch=3,
        grid=(qb.shape[0],),
        in_specs=[pl.BlockSpec((h, tq, wqk), lambda s, qb, kb, fl: (0, qb[s], 0)),
                  pl.BlockSpec((tq, kvl), lambda s, qb, kb, fl: (qb[s], 0)),
                  pl.BlockSpec((tq, LANES), lambda s, qb, kb, fl: (qb[s], 0)),
                  pl.BlockSpec((tk, kvl), lambda s, qb, kb, fl: (kb[s], 0)),
                  pl.BlockSpec((tk, LANES), lambda s, qb, kb, fl: (kb[s], 0)),
                  pl.BlockSpec((None, kvl, w_ukv_b.shape[2]), lambda s, qb, kb, fl: (l, 0, 0))],
        out_specs=pl.BlockSpec((tq, h * V_DIM), out_block_of),
        scratch_shapes=[pltpu.VMEM((h * tq, kvl), BF16),
                        pltpu.VMEM((h * tq, LANES), BF16),
                        pltpu.VMEM((h * tq, kvl), F32),
                        pltpu.VMEM((h * tq, LANES), F32),
                        pltpu.VMEM((h * tq, LANES), F32)],
    )
    kern = functools.partial(_mla_latent_kernel, heads=h, pos0=pos0)
    return pl.pallas_call(
        kern, grid_spec=grid_spec,
        out_shape=jax.ShapeDtypeStruct((out_rows, h * V_DIM), BF16),
        compiler_params=_cparams(1),
    )(qb, kb, fl, qc, c_new, r_new, c_past, r_past, w_ukv_b)


def _cache_walk_tables(n_batch, q_off, past_tiles, past_stride):
    qb, kb, fl = [], [], []
    for b in range(n_batch):
        for j in range(1 + past_tiles):
            qb.append(q_off + b)
            kb.append(past_stride + b * past_tiles + past_tiles - max(j, 1))
            fl.append((1 if j == 0 else 0) | (2 if j == past_tiles else 0))
    arr = lambda v: jnp.asarray(np.asarray(v, dtype=np.int32))
    return arr(qb), arr(kb), arr(fl)


ROW_DMA_UNROLL = 8


def _moe_kernel(te_ref, nv_ref, tok_ref, dst_ref, h_hbm, w_ref, wg_ref, wu_ref, wd_ref, y_hbm,
                xbuf, obuf, wg_s, wu_s, wd_s, sem_in, sem_out, *, s):
    t = pl.program_id(0)
    tm = w_ref.shape[0]
    nv = nv_ref[0]
    live = t < nv
    slot = lax.rem(t, 2)

    p = _slab_pitch(s)

    def in_copy(tile, r, sl):
        src0 = pl.multiple_of(tok_ref[tile * tm + r] * p, SLAB_PAD)
        return pltpu.make_async_copy(h_hbm.at[pl.ds(src0, s)],
                                     xbuf.at[sl, pl.ds(pl.multiple_of(r * p, SLAB_PAD), s)],
                                     sem_in.at[sl])

    def gather_start(tile, sl):
        def body(r, _):
            in_copy(tile, r, sl).start()
            return 0
        lax.fori_loop(0, tm, body, 0, unroll=ROW_DMA_UNROLL)

    def gather_wait(tile, sl):
        def body(r, _):
            in_copy(tile, r, sl).wait()
            return 0
        lax.fori_loop(0, tm, body, 0, unroll=ROW_DMA_UNROLL)

    @pl.when(live & (t == 0))
    def _():
        gather_start(0, 0)

    @pl.when(t + 1 < nv)
    def _():
        gather_start(t + 1, 1 - slot)

    prev = te_ref[jnp.maximum(t - 1, 0)]
    fresh = (t == 0) | (te_ref[t] != prev)

    @pl.when(live & fresh)
    def _():
        wg_s[...] = wg_ref[...].astype(BF16)
        wu_s[...] = wu_ref[...].astype(BF16)
        wd_s[...] = wd_ref[...].astype(BF16)

    @pl.when(live)
    def _():
        gather_wait(t, slot)
        x = _slab_load(xbuf, 0, tm, s, lead=slot).astype(BF16)
        a = jnp.dot(x, wg_s[...], preferred_element_type=F32)
        u = jnp.dot(x, wu_s[...], preferred_element_type=F32)
        hid = (a * jax.nn.sigmoid(a)) * u * w_ref[...]
        y = jnp.dot(hid.astype(BF16), wd_s[...], preferred_element_type=F32)

        def out_copy(tile, r):
            dst0 = pl.multiple_of(dst_ref[tile * tm + r] * p, SLAB_PAD)
            return pltpu.make_async_copy(obuf.at[pl.ds(pl.multiple_of(r * p, SLAB_PAD), s)],
                                         y_hbm.at[pl.ds(dst0, s)], sem_out)

        def scatter_wait(tile):
            def wait(r, _):
                out_copy(tile, r).wait()
                return 0
            lax.fori_loop(0, tm, wait, 0, unroll=ROW_DMA_UNROLL)

        @pl.when(t > 0)
        def _():
            scatter_wait(t - 1)

        _slab_store(obuf, 0, y)

        def start(r, _):
            out_copy(t, r).start()
            return 0
        lax.fori_loop(0, tm, start, 0, unroll=ROW_DMA_UNROLL)

        @pl.when(t == nv - 1)
        def _():
            scatter_wait(t)


def _moe_experts(h_slab, row_tok, row_dst, row_w, tile_expert, n_valid, w_gate, w_up, w_down, l,
                 n_dest):
    d, f = w_gate.shape[-2:]
    s = d // LANES
    p = _slab_pitch(s)
    tm = MOE_TILE
    n_tiles = row_tok.shape[0] // tm
    wmap = lambda t, te, nv, tok, dst: (l, te[t], 0, 0)
    grid_spec = pltpu.PrefetchScalarGridSpec(
        num_scalar_prefetch=4, grid=(n_tiles,),
        in_specs=[pl.BlockSpec(memory_space=pl.ANY),
                  pl.BlockSpec((tm, 1), lambda t, te, nv, tok, dst: (t, 0)),
                  pl.BlockSpec((None, None, d, f), wmap),
                  pl.BlockSpec((None, None, d, f), wmap),
                  pl.BlockSpec((None, None, f, d), wmap)],
        out_specs=pl.BlockSpec(memory_space=pl.ANY),
        scratch_shapes=[pltpu.VMEM((2, tm * p, LANES), F32), pltpu.VMEM((tm * p, LANES), F32),
                        pltpu.VMEM((d, f), BF16), pltpu.VMEM((d, f), BF16),
                        pltpu.VMEM((f, d), BF16),
                        pltpu.SemaphoreType.DMA((2,)), pltpu.SemaphoreType.DMA(())],
    )
    return pl.pallas_call(
        functools.partial(_moe_kernel, s=s), grid_spec=grid_spec,
        out_shape=jax.ShapeDtypeStruct(((n_dest + tm) * p, LANES), F32),
        compiler_params=_cparams(1),
    )(tile_expert, n_valid, row_tok, row_dst, h_slab, row_w, w_gate, w_up, w_down)


def _combine_kernel(x_ref, y0_ref, y1_ref, g_ref, o_ref):
    gb, grp, d = x_ref.shape
    s = d // LANES
    g0 = pl.program_id(0) * gb
    for g in range(gb):
        y = _slab_load(y0_ref, g * grp, grp, s) + _slab_load(y1_ref, g * grp, grp, s)
        o_ref[g] = x_ref[g] + _mod_row(g_ref, g0 + g) * y


def _combine(x3, yg, modg, l, g_i):
    ng, grp, d = x3.shape
    s = d // LANES
    gb = _row_tile(ng, 2)
    blk = pl.BlockSpec((gb, grp, d), lambda i: (i, 0, 0))
    yblk = lambda off: pl.BlockSpec((gb * grp * _slab_pitch(s), LANES), lambda i: (i + off, 0))
    return pl.pallas_call(
        _combine_kernel,
        grid=(ng // gb,),
        in_specs=[blk, yblk(0), yblk(ng // gb), _mod_spec(g_i, ng, d)],
        out_specs=blk,
        out_shape=jax.ShapeDtypeStruct(x3.shape, F32),
        compiler_params=_cparams(1),
    )(x3, yg, yg, modg)


def _dispatch(route, n):
    tm = MOE_TILE
    r_cap = (2 * n + N_EXPERTS * (tm - 1) + tm - 1) // tm * tm
    e = route[:, :2].astype(jnp.int32)
    w = route[:, 2:4]
    flat_e = e.T.reshape(-1)
    flat_w = w.T.reshape(-1)
    order = jnp.argsort(flat_e, stable=True).astype(jnp.int32)
    bounds = jnp.searchsorted(flat_e[order], jnp.arange(N_EXPERTS + 1, dtype=jnp.int32),
                              side="left").astype(jnp.int32)
    counts = bounds[1:] - bounds[:-1]
    padded = (counts + tm - 1) // tm * tm
    ends_p = jnp.cumsum(padded)
    starts_p = ends_p - padded
    starts = jnp.cumsum(counts) - counts
    tile_start = jnp.arange(r_cap // tm, dtype=jnp.int32) * tm
    tile_expert = jnp.minimum(jnp.searchsorted(ends_p, tile_start, side="right"),
                              N_EXPERTS - 1).astype(jnp.int32)
    n_valid = (ends_p[-1:] // tm).astype(jnp.int32)
    per_row = lambda v: jnp.broadcast_to(v[:, None], (r_cap // tm, tm)).reshape(r_cap)
    local = jnp.arange(r_cap, dtype=jnp.int32) - per_row(starts_p[tile_expert])
    valid = local < per_row(counts[tile_expert])
    slot = order[jnp.clip(per_row(starts[tile_expert]) + local, 0, 2 * n - 1)]
    row_dst = jnp.where(valid, slot, 2 * n + jnp.arange(r_cap, dtype=jnp.int32) % tm)
    row_tok = jnp.where(valid, jnp.where(slot >= n, slot - n, slot), 0)
    row_w = jnp.where(valid, flat_w[slot], 0.0)
    return row_tok, row_dst, row_w.reshape(r_cap, 1), tile_expert, n_valid


def _gates_kernel(h_ref, wa_ref, wb_ref, o_ref, bsc):
    @pl.when(pl.program_id(1) == 0)
    def _():
        r = wb_ref.shape[0]
        w = wa_ref.shape[0]
        bsc[:w - r, :] = wa_ref[r:, :].astype(BF16)
        bsc[w - r:, :] = wb_ref[...].astype(BF16)

    o_ref[...] = jax.nn.sigmoid(_dot_nt(h_ref[...], bsc[...]))


def _gates(h, w_nk, l, off_kr, n_gate, tm, tn):
    n, k = h.shape
    assert off_kr % tn == 0 and n_gate % tn == 0 and tn % QK_ROPE == 0
    rb = off_kr // tn
    return pl.pallas_call(
        _gates_kernel,
        grid=(n_gate // tn, n // tm),
        in_specs=[pl.BlockSpec((tm, k), lambda j, i: (i, 0)),
                  pl.BlockSpec((None, tn, k), lambda j, i: (l, rb + j, 0)),
                  pl.BlockSpec((None, QK_ROPE, k),
                               lambda j, i: (l, (rb + j + 1) * (tn // QK_ROPE), 0))],
        out_specs=pl.BlockSpec((tm, tn), lambda j, i: (i, j)),
        out_shape=jax.ShapeDtypeStruct((n, n_gate), F32),
        scratch_shapes=[pltpu.VMEM((tn, k), BF16)],
        compiler_params=_cparams(2),
    )(h, w_nk, w_nk)


def _rope_weights_kernel(a_ref, o_ref):
    a = a_ref[...].astype(BF16)
    q = a.shape[0] // 2
    o_ref[...] = jnp.concatenate([a, a[q:], a[:q]], axis=0)


def _rope_key_weights(w_nk, off_kr):
    depth, n_in, k = w_nk.shape
    assert QK_ROPE * 2 == LANES and off_kr % QK_ROPE == 0
    return pl.pallas_call(
        _rope_weights_kernel,
        grid=(depth,),
        in_specs=[pl.BlockSpec((None, QK_ROPE, k), lambda l: (l, off_kr // QK_ROPE, 0))],
        out_specs=pl.BlockSpec((None, 2 * QK_ROPE, k), lambda l: (l, 0, 0)),
        out_shape=jax.ShapeDtypeStruct((depth, 2 * QK_ROPE, k), BF16),
        compiler_params=_cparams(1),
    )(w_nk)


def _rope_tables(pos):
    inv = ROPE_THETA ** (-jnp.arange(0, QK_ROPE, 2, dtype=F32) / QK_ROPE)
    ang = pos.astype(F32)[:, None] * inv[None, :]
    cos, sin = jnp.cos(ang), jnp.sin(ang)
    pad = jnp.zeros((pos.shape[0], LANES - QK_ROPE), F32)
    return (jnp.concatenate([cos, cos, pad], axis=1),
            jnp.concatenate([-sin, sin, pad], axis=1))


def _rope_lanes(acc, c, s):
    return acc * c + pltpu.roll(acc, LANES - QK_ROPE, axis=1) * s


def _swap_halves(w):
    half = w.shape[-1] // 2
    return jnp.concatenate([w[..., half:], w[..., :half]], axis=-1)


def kernel(x_prompt, x_sample, c_prompt, c_sample, cache_sb_k, cache_sb_v, cache_mla_ckv,
           cache_mla_krope, w_ada, b_ada, g_norm_mix, g_norm_ffn, w_in, g_q_lat, g_kv_lat,
           w_uq, w_ukv, w_branch_sb, w_branch_mla, w_out, w_router_group, b_router_group,
           w_router_expert, b_router_expert, w_exp_gate, w_exp_up, w_exp_down, g_final):
    bp, t_p, d = x_prompt.shape
    bs, t_s, _ = x_sample.shape
    depth = w_in.shape[0]
    past = cache_sb_k.shape[2]
    grp = t_s
    n_p, n_s = bp * t_p, bs * t_s
    n = n_p + n_s
    ng = n // grp
    sb_w = H_SB * DH_SB
    q_lora = g_q_lat.shape[1]
    kv_lora = g_kv_lat.shape[1]
    tm = _token_tile(n, grp)
    gpt = tm // grp
    tn = 512
    assert bp == 1 and t_p % ATTN_TILE == 0 and t_p % grp == 0 and past % 512 == 0

    x3 = jnp.concatenate([x_prompt.reshape(n_p // grp, grp, d), x_sample], axis=0)

    n_c = bp + bs
    c_rows = 16
    c_all = jnp.zeros((c_rows, d), F32).at[:n_c].set(jnp.concatenate([c_prompt, c_sample], 0))
    n_modc = N_MOD * d

    def ada_epi(accs, ex, outs):
        outs[0][...] = accs[0] + ex[0][...]

    mods = []
    for l in range(depth):
        mods.append(_matmul(
            [(c_all, w_ada, _wspec(l, d, tn, 0), True)], m=c_rows, n_out=n_modc, tm=c_rows, tn=tn,
            prologue=lambda a, ex: a * jax.nn.sigmoid(a),
            epilogue=ada_epi,
            extras=[b_ada.reshape(depth, 1, n_modc)],
            extra_specs=[pl.BlockSpec((None, 1, tn), lambda j, i, l=l: (l, 0, j))],
            out_shape=[jax.ShapeDtypeStruct((c_rows, n_modc), F32)],
            out_specs=[_spec2(c_rows, tn)])[0])
    mod = jnp.stack(mods)
    modg = jnp.concatenate(
        [jnp.broadcast_to(mod[:, :bp], (depth, n_p // grp, n_modc)), mod[:, bp:n_c]], axis=1)
    modg = modg.reshape(depth, ng, N_MOD, d).transpose(0, 2, 1, 3)
    modg = modg.reshape(depth * N_MOD, ng, d)

    def mod_idx(l, k):
        return l * N_MOD + k

    def mspec_rows(l, k):
        return pl.BlockSpec((None, ng, tn), lambda j, i: (mod_idx(l, k), 0, j))

    pos = jnp.concatenate([jnp.arange(t_p, dtype=jnp.int32),
                           jnp.tile(past + jnp.arange(t_s, dtype=jnp.int32), bs)])
    rope_c, rope_s = _rope_tables(pos)
    tq = ATTN_TILE
    tk_s = 512
    pt = past // tk_s

    off_q, off_k, off_v = 0, sb_w, 2 * sb_w
    off_cq = 3 * sb_w
    off_ckv = off_cq + q_lora
    off_kr = off_ckv + kv_lora
    off_g = off_kr + QK_ROPE
    assert off_g == off_kr + QK_ROPE
    w_nk = jnp.swapaxes(w_in, 1, 2)
    w_kr_aug = _rope_key_weights(w_nk, off_kr)
    uq = w_uq.reshape(depth, q_lora, H_MLA, QK_NOPE + QK_ROPE)
    uq_r = uq[..., QK_NOPE:]
    w_uq_cat = jnp.concatenate([uq[..., :QK_NOPE], uq_r, _swap_halves(uq_r)], axis=-1)
    w_uq_cat = w_uq_cat.reshape(depth, q_lora, H_MLA * 2 * LANES)
    w_router = jnp.concatenate(
        [w_router_expert, w_router_group,
         jnp.zeros((depth, d, LANES - N_EXPERTS - N_GROUPS), F32)], axis=-1)
    r_hi = w_router.astype(BF16)
    r_res = w_router - r_hi.astype(F32)
    r_mid = r_res.astype(BF16)
    r_lo = (r_res - r_mid.astype(F32)).astype(BF16)
    w_router3 = jnp.stack([r_hi, r_mid, r_lo], axis=1)
    b_router = jnp.concatenate(
        [b_router_expert, b_router_group,
         jnp.zeros((depth, LANES - N_EXPERTS - N_GROUPS), F32)], axis=-1).reshape(depth, 1, LANES)

    rows_past = bs * past
    kr_past = jnp.pad(cache_mla_krope.reshape(depth * rows_past, QK_ROPE),
                      ((0, 0), (0, LANES - QK_ROPE))).astype(BF16)
    w_ukv_b = w_ukv.astype(BF16)
    hm = lambda rows: jax.ShapeDtypeStruct((H_SB, rows, LANES), BF16)
    new_k, new_v, new_c, new_r = [], [], [], []

    for l in range(depth):
        h = _norm_mod(x3, g_norm_mix.reshape(depth, 1, d), modg, l, mod_idx(l, 1), mod_idx(l, 0))
        h = h.reshape(n, d)

        sbq_scale = DH_SB ** -0.5 * float(np.log2(np.e))

        def plain_hm(accs, ex, outs):
            _store_heads(outs[0], accs[0] * sbq_scale)

        def f32_and_hm(accs, ex, outs):
            outs[0][...] = accs[0]
            _store_heads(outs[1], accs[0])

        sb_q = _matmul([(h, w_nk, _wspec_nk(l, d, tn, off_q), True, True)], m=n, n_out=sb_w, tm=tm, tn=tn,
                       epilogue=plain_hm, out_shape=[hm(n)], out_specs=[_hm_spec(tm, tn)])[0]
        k_f32, sb_k = _matmul([(h, w_nk, _wspec_nk(l, d, tn, off_k), True, True)], m=n, n_out=sb_w, tm=tm,
                              tn=tn, epilogue=f32_and_hm,
                              out_shape=[jax.ShapeDtypeStruct((n, sb_w), F32), hm(n)],
                              out_specs=[_spec2(tm, tn), _hm_spec(tm, tn)])
        v_f32, sb_v = _matmul([(h, w_nk, _wspec_nk(l, d, tn, off_v), True, True)], m=n, n_out=sb_w, tm=tm,
                              tn=tn, epilogue=f32_and_hm,
                              out_shape=[jax.ShapeDtypeStruct((n, sb_w), F32), hm(n)],
                              out_specs=[_spec2(tm, tn), _hm_spec(tm, tn)])

        def plain_f32(accs, ex, outs):
            outs[0][...] = accs[0]

        c_q = _matmul([(h, w_nk, _wspec_nk(l, d, tn, off_cq), True, True)], m=n, n_out=q_lora, tm=tm, tn=tn,
                      epilogue=plain_f32, out_shape=[jax.ShapeDtypeStruct((n, q_lora), F32)],
                      out_specs=[_spec2(tm, tn)])[0]

        def ckv_epi(accs, ex, outs):
            y = _rms(accs[0], ex[0][...])
            outs[0][...] = y
            outs[1][...] = y.astype(BF16)

        c_kv, c_kv_b = _matmul(
            [(h, w_nk, _wspec_nk(l, d, kv_lora, off_ckv), True, True)], m=n, n_out=kv_lora, tm=tm, tn=kv_lora,
            epilogue=ckv_epi, extras=[g_kv_lat.reshape(depth, 1, kv_lora)],
            extra_specs=[pl.BlockSpec((None, 1, kv_lora), lambda j, i: (l, 0, 0))],
            out_shape=[jax.ShapeDtypeStruct((n, kv_lora), F32),
                       jax.ShapeDtypeStruct((n, kv_lora), BF16)],
            out_specs=[_spec2(tm, kv_lora), _spec2(tm, kv_lora)])

        def kr_epi(accs, ex, outs):
            r = _rope_lanes(accs[0], ex[0][...], ex[1][...])
            outs[0][...] = r
            outs[1][...] = r.astype(BF16)

        rope_specs = [pl.BlockSpec((tm, LANES), lambda j, i: (i, 0))] * 2
        k_r, k_r_b = _matmul(
            [(h, w_kr_aug, pl.BlockSpec((None, LANES, d), lambda j, i: (l, 0, 0)), False, True)],
            m=n, n_out=LANES, tm=tm, tn=LANES, epilogue=kr_epi,
            extras=[rope_c, rope_s], extra_specs=rope_specs,
            out_shape=[jax.ShapeDtypeStruct((n, LANES), F32), jax.ShapeDtypeStruct((n, LANES), BF16)],
            out_specs=[_spec2(tm, LANES), _spec2(tm, LANES)])

        gates = _gates(h, w_nk, l, off_kr, 2 * d, tm, tn)

        def cq_prologue(a, ex):
            return _rms(a, ex[0][...])

        gq_spec = pl.BlockSpec((None, 1, q_lora), lambda j, i: (l, 0, 0))
        gq = g_q_lat.reshape(depth, 1, q_lora)
        wqk = 2 * LANES
        hm_qk = lambda rows: jax.ShapeDtypeStruct((H_MLA, rows, wqk), BF16)

        q_scale = (QK_NOPE + QK_ROPE) ** -0.5 * float(np.log2(np.e))

        tn_kv = min(2048, H_MLA * wqk)

        def qcat_epi(accs, ex, outs):
            c, s = ex[1][...], ex[2][...]
            for hh in range(tn_kv // wqk):
                blk = accs[0][:, hh * wqk:(hh + 1) * wqk]
                outs[0][hh, :, :LANES] = (blk[:, :LANES] * q_scale).astype(BF16)
                outs[0][hh, :, LANES:] = (_rope_lanes(blk[:, LANES:], c, s) * q_scale).astype(BF16)

        q_cat = _matmul(
            [(c_q, w_uq_cat, pl.BlockSpec((None, q_lora, tn_kv), lambda j, i: (l, 0, j)), True)],
            m=n, n_out=H_MLA * wqk, tm=tm, tn=tn_kv, prologue=cq_prologue, epilogue=qcat_epi,
            extras=[gq, rope_c, rope_s], extra_specs=[gq_spec] + rope_specs,
            out_shape=[hm_qk(n)],
            out_specs=[pl.BlockSpec((tn_kv // wqk, tm, wqk), lambda j, i: (j, i, 0))])[0]

        hpt = tn_kv // wqk

        def kv_epi(accs, ex, outs):
            kr = ex[0][...]
            ones = jnp.ones(kr.shape, BF16)
            for hh in range(hpt):
                outs[0][hh, :, :LANES] = accs[0][:, hh * wqk:hh * wqk + LANES].astype(BF16)
                outs[0][hh, :, LANES:] = kr
                outs[1][hh, :, :LANES] = accs[0][:, hh * wqk + LANES:(hh + 1) * wqk].astype(BF16)
                outs[1][hh, :, LANES:] = ones

        def up_kv(a, kr, rows, tmr):
            hspec = pl.BlockSpec((hpt, tmr, wqk), lambda j, i: (j, i, 0))
            return _matmul(
                [(a, w_ukv, pl.BlockSpec((None, kv_lora, tn_kv), lambda j, i: (l, 0, j)), True)],
                m=rows, n_out=H_MLA * wqk, tm=tmr, tn=tn_kv, epilogue=kv_epi, extras=[kr],
                extra_specs=[pl.BlockSpec((tmr, LANES), lambda j, i: (i, 0))],
                out_shape=[hm_qk(rows), hm_qk(rows)], out_specs=[hspec, hspec])

        kc_new, v_new = up_kv(c_kv_b, k_r_b, n, tm)

        o_sb_p = _sb_stream_attention(sb_q, sb_k, sb_v, sb_k, sb_v, tq=tq, tk=tq, q_block0=0,
                                      n_tiles=n_p // tq)
        cache_k2 = cache_sb_k.reshape(depth * bs * past * H_SB, DH_SB)
        cache_v2 = cache_sb_v.reshape(depth * bs * past * H_SB, DH_SB)
        q_off_s = n_p // t_s
        o_sb_s = _sb_stream_attention(sb_q, sb_k, sb_v, cache_k2, cache_v2, tq=t_s, tk=ATTN_TILE,
                                      q_block0=q_off_s, n_tiles=bs,
                                      cache_tiles=past // ATTN_TILE,
                                      cache_base=l * bs * (past // ATTN_TILE))
        o_sb = jnp.concatenate([o_sb_p, o_sb_s], axis=0)

        o_mla_p = _mla_attention(q_cat, kc_new, v_new, n_rows=n_p,
                                 tq=min(MLA_Q_TILE, n_p), tk=min(MLA_Q_TILE, n_p))
        o_mla_s = _mla_latent_attention(
            q_cat, c_kv_b, k_r_b, cache_mla_ckv.reshape(depth * rows_past, kv_lora), kr_past,
            w_ukv_b, _cache_walk_tables(bs, q_off_s, pt, l * bs * pt), l,
            tq=t_s, tk=tk_s, pos0=past, out_rows=n_s,
            out_block_of=lambda s, qb, kb, fl: (qb[s] - q_off_s, 0))
        o_mla = jnp.concatenate([o_mla_p, o_mla_s], axis=0)

        def merge_epi(accs, ex, outs):
            outs[0][...] = (ex[0][...] * accs[0] + ex[1][...] * accs[1]).astype(BF16)

        merged = _matmul(
            [(o_sb, w_branch_sb, _wspec(l, sb_w, tn, 0), True),
             (o_mla, w_branch_mla, _wspec(l, H_MLA * V_DIM, tn, 0), True)],
            m=n, n_out=d, tm=tm, tn=tn, epilogue=merge_epi,
            extras=[gates, gates],
            extra_specs=[pl.BlockSpec((tm, tn), lambda j, i: (i, j)),
                         pl.BlockSpec((tm, tn), lambda j, i: (i, j + d // tn))],
            out_shape=[jax.ShapeDtypeStruct((n, d), BF16)], out_specs=[_spec2(tm, tn)])[0]

        def resid_epi(accs, ex, outs):
            g0 = pl.program_id(1) * gpt
            for g in range(gpt):
                outs[0][g] = ex[0][g] + _mod_row(ex[1], g0 + g) * accs[0][g * grp:(g + 1) * grp, :]

        x_spec = pl.BlockSpec((gpt, grp, tn), lambda j, i: (i, 0, j))
        x3 = _matmul(
            [(merged, w_out, _wspec(l, d, tn, 0), True)], m=n, n_out=d, tm=tm, tn=tn,
            epilogue=resid_epi, extras=[x3, modg], extra_specs=[x_spec, mspec_rows(l, 2)],
            out_shape=[jax.ShapeDtypeStruct((ng, grp, d), F32)], out_specs=[x_spec])[0]

        h2, route = _norm_route(x3, g_norm_ffn.reshape(depth, 1, d), modg, l, mod_idx(l, 4),
                                mod_idx(l, 3), w_router3, b_router)
        row_tok, row_dst, row_w, tile_expert, n_valid = _dispatch(route.reshape(n, LANES), n)
        yg = _moe_experts(h2, row_tok, row_dst, row_w, tile_expert, n_valid,
                          w_exp_gate, w_exp_up, w_exp_down, l, 2 * n)
        x3 = _combine(x3, yg, modg, l, mod_idx(l, 5))

        new_k.append(k_f32)
        new_v.append(v_f32)
        new_c.append(c_kv)
        new_r.append(k_r[:, :QK_ROPE])

    y_p, y_s = _final_norm(x3, g_final.reshape(1, d), n_p // grp)

    def split(parts, tail):
        a = jnp.stack(parts)
        return (a[:, :n_p].reshape((depth, bp, t_p) + tail),
                a[:, n_p:].reshape((depth, bs, t_s) + tail))

    pk, sk = split(new_k, (H_SB, DH_SB))
    pv, sv = split(new_v, (H_SB, DH_SB))
    pc, sc = split(new_c, (kv_lora,))
    pr, sr = split(new_r, (QK_ROPE,))
    return (y_p.reshape(bp, t_p, d), y_s.reshape(bs, t_s, d), pk, pv, pc, pr, sk, sv, sc, sr)
```

```python
import functools

import numpy as np
import jax
import jax.numpy as jnp
from jax import lax
from jax.experimental import pallas as pl
from jax.experimental.pallas import tpu as pltpu

F32 = jnp.float32
BF16 = jnp.bfloat16

CHUNK = 64
H_SB = 16
DH_SB = 128
H_MLA = 16
QK_NOPE = 128
QK_ROPE = 64
V_DIM = 128
ROPE_THETA = 10000.0
N_GROUPS = 4
EXPERTS_PER_GROUP = 8
N_EXPERTS = N_GROUPS * EXPERTS_PER_GROUP
N_MOD = 6
EPS = 1e-6

LANES = 128
ATTN_TILE = 256
MOE_TILE = 256
HEAD_UNROLL = 4
MLA_UNROLL = 8
MLA_Q_TILE = 512
SLAB_PAD = 4
SB_DEAD = 152.0
VMEM_LIMIT = 56 * 1024 * 1024


def _cparams(n_axes, vmem=VMEM_LIMIT):
    return pltpu.CompilerParams(dimension_semantics=("arbitrary",) * n_axes,
                                vmem_limit_bytes=vmem)


def _row_tile(n, cap=512):
    t = cap
    while n % t:
        t //= 2
    return t


def _token_tile(n, grp, cap=1152):
    return max(t for t in range(grp, cap + 1, grp) if n % t == 0)


def _matmul(pairs, *, m, n_out, tm, tn, epilogue, out_shape, out_specs,
            extras=(), extra_specs=(), prologue=None):
    n_pairs = len(pairs)
    n_ex = len(extras)
    n_outs = len(out_shape)
    pairs = [tuple(p) + (False,) * (5 - len(p)) for p in pairs]
    cast = [p[3] for p in pairs]
    b_nk = [p[4] for p in pairs]

    def kern(*refs):
        a_refs = refs[0:2 * n_pairs:2]
        b_refs = refs[1:2 * n_pairs:2]
        ex = refs[2 * n_pairs:2 * n_pairs + n_ex]
        outs = refs[2 * n_pairs + n_ex:2 * n_pairs + n_ex + n_outs]
        scr = refs[2 * n_pairs + n_ex + n_outs:]
        i = pl.program_id(1)
        accs = []
        si = 0
        for p in range(n_pairs):
            if cast[p]:
                bsc = scr[si]
                si += 1

                @pl.when(i == 0)
                def _(bsc=bsc, b_ref=b_refs[p]):
                    bsc[...] = b_ref[...].astype(BF16)

                bv = bsc[...]
            else:
                bv = b_refs[p][...]
            a = a_refs[p][...]
            if prologue is not None:
                a = prologue(a, ex)
            if b_nk[p]:
                accs.append(_dot_nt(a.astype(BF16), bv))
            else:
                accs.append(jnp.dot(a.astype(BF16), bv, preferred_element_type=F32))
        epilogue(accs, ex, outs)

    in_specs, args, scratch = [], [], []
    for (a, b, b_spec, cb, nk) in pairs:
        k = a.shape[1]
        in_specs += [pl.BlockSpec((tm, k), lambda j, i: (i, 0)), b_spec]
        args += [a, b]
        if cb:
            scratch.append(pltpu.VMEM((tn, k) if nk else (k, tn), BF16))
    in_specs += list(extra_specs)
    args += list(extras)
    return pl.pallas_call(
        kern,
        grid=(n_out // tn, m // tm),
        in_specs=in_specs,
        out_specs=out_specs,
        out_shape=out_shape,
        scratch_shapes=scratch,
        compiler_params=_cparams(2),
    )(*args)


def _wspec(l, k, tn, col_off):
    cb = col_off // tn
    assert cb * tn == col_off
    return pl.BlockSpec((None, k, tn), lambda j, i: (l, 0, cb + j))


def _wspec_nk(l, k, tn, row_off):
    rb = row_off // tn
    assert rb * tn == row_off
    return pl.BlockSpec((None, tn, k), lambda j, i: (l, rb + j, 0))


def _spec2(tm, tn):
    return pl.BlockSpec((tm, tn), lambda j, i: (i, j))


def _hm_spec(tm, tn):
    return pl.BlockSpec((tn // LANES, tm, LANES), lambda j, i: (j, i, 0))


def _store_heads(o_ref, val):
    for c in range(val.shape[1] // LANES):
        o_ref[c] = val[:, c * LANES:(c + 1) * LANES].astype(o_ref.dtype)


def _rms(x, g):
    return x * lax.rsqrt(jnp.mean(x * x, axis=-1, keepdims=True) + EPS) * g


def _mod_row(ref, g):
    return ref[pl.ds(g, 1), :]


def _norm_mod_kernel(x_ref, g_ref, sc_ref, sh_ref, o_ref):
    gb = x_ref.shape[0]
    g0 = pl.program_id(0) * gb
    for g in range(gb):
        y = _rms(x_ref[g], g_ref[...])
        o_ref[g] = (y * (1.0 + _mod_row(sc_ref, g0 + g)) + _mod_row(sh_ref, g0 + g)).astype(o_ref.dtype)


def _split2(x):
    hi = x.astype(BF16)
    return hi, (x - hi.astype(F32)).astype(BF16)


def _dot_f32(a, b2):
    a_hi, a_lo = _split2(a)
    b_hi, b_lo = b2
    d = functools.partial(jnp.dot, preferred_element_type=F32)
    return d(a_hi, b_hi) + (d(a_hi, b_lo) + d(a_lo, b_hi))


def _route(logits):
    lane = lax.broadcasted_iota(jnp.int32, logits.shape, 1)
    lanef = lane.astype(F32)
    big = jnp.float32(1e9)
    ninf = jnp.float32(-jnp.inf)
    is_g = (lane >= N_EXPERTS) & (lane < N_EXPERTS + N_GROUPS)
    gl = jnp.where(is_g, logits, ninf)
    gmax = jnp.max(gl, axis=1, keepdims=True)
    g_idx = jnp.min(jnp.where(gl == gmax, lanef - N_EXPERTS, big), axis=1, keepdims=True)
    p_group = 1.0 / jnp.sum(jnp.where(is_g, jnp.exp(gl - gmax), 0.0), axis=1, keepdims=True)
    grp = jnp.floor(lanef * (1.0 / EXPERTS_PER_GROUP))
    in_g = (lane < N_EXPERTS) & (grp == g_idx)
    el = jnp.where(in_g, logits, ninf)
    e1 = jnp.max(el, axis=1, keepdims=True)
    i1 = jnp.min(jnp.where(el == e1, lanef, big), axis=1, keepdims=True)
    el2 = jnp.where(lanef == i1, ninf, el)
    e2 = jnp.max(el2, axis=1, keepdims=True)
    i2 = jnp.min(jnp.where(el2 == e2, lanef, big), axis=1, keepdims=True)
    t = jnp.exp(e2 - e1)
    den = 1.0 + t
    w1 = (1.0 / den) * p_group
    w2 = (t / den) * p_group
    out = jnp.where(lane == 0, i1, jnp.where(lane == 1, i2,
          jnp.where(lane == 2, w1, jnp.where(lane == 3, w2, 0.0))))
    return out


def _slab_pitch(s):
    return s + SLAB_PAD


def _slab_store(ref, row0, val):
    rows, d = val.shape
    s = d // LANES
    p = _slab_pitch(s)
    for c in range(s):
        ref[pl.ds(row0 * p + c, rows, stride=p), :] = val[:, c * LANES:(c + 1) * LANES]


def _slab_load(ref, row0, rows, s, lead=None):
    pieces = []
    p = _slab_pitch(s)
    for c in range(s):
        rs = pl.ds(row0 * p + c, rows, stride=p)
        pieces.append(ref[rs, :] if lead is None else ref[lead, rs, :])
    return jnp.concatenate(pieces, axis=1)


def _norm_route_kernel(x_ref, g_ref, sc_ref, sh_ref, wr_ref, br_ref, h_ref, r_ref):
    gb, grp, _ = x_ref.shape
    g0 = pl.program_id(0) * gb
    b3 = (wr_ref[0], wr_ref[1])
    for g in range(gb):
        y = _rms(x_ref[g], g_ref[...])
        h = y * (1.0 + _mod_row(sc_ref, g0 + g)) + _mod_row(sh_ref, g0 + g)
        _slab_store(h_ref, g * grp, h)
        logits = _dot_f32(h, b3) + br_ref[...]
        r_ref[g] = _route(logits)


def _mod_spec(idx, ng, d):
    return pl.BlockSpec((None, ng, d), lambda i: (idx, 0, 0))


def _norm_mod(x3, g, modg, l, sc_i, sh_i):
    ng, grp, d = x3.shape
    gb = _row_tile(ng, 4)
    return pl.pallas_call(
        _norm_mod_kernel,
        grid=(ng // gb,),
        in_specs=[pl.BlockSpec((gb, grp, d), lambda i: (i, 0, 0)),
                  pl.BlockSpec((None, 1, d), lambda i: (l, 0, 0)),
                  _mod_spec(sc_i, ng, d), _mod_spec(sh_i, ng, d)],
        out_specs=pl.BlockSpec((gb, grp, d), lambda i: (i, 0, 0)),
        out_shape=jax.ShapeDtypeStruct(x3.shape, BF16),
        compiler_params=_cparams(1),
    )(x3, g, modg, modg)


def _norm_route(x3, g, modg, l, sc_i, sh_i, wr3, br):
    ng, grp, d = x3.shape
    gb = _row_tile(ng, 4)
    return pl.pallas_call(
        _norm_route_kernel,
        grid=(ng // gb,),
        in_specs=[pl.BlockSpec((gb, grp, d), lambda i: (i, 0, 0)),
                  pl.BlockSpec((None, 1, d), lambda i: (l, 0, 0)),
                  _mod_spec(sc_i, ng, d), _mod_spec(sh_i, ng, d),
                  pl.BlockSpec((None, 2, d, LANES), lambda i: (l, 0, 0, 0)),
                  pl.BlockSpec((None, 1, LANES), lambda i: (l, 0, 0))],
        out_specs=[pl.BlockSpec((gb * grp * _slab_pitch(d // LANES), LANES), lambda i: (i, 0)),
                   pl.BlockSpec((gb, grp, LANES), lambda i: (i, 0, 0))],
        out_shape=[jax.ShapeDtypeStruct((ng * grp * _slab_pitch(d // LANES), LANES), F32),
                   jax.ShapeDtypeStruct((ng, grp, LANES), F32)],
        compiler_params=_cparams(1),
    )(x3, g, modg, modg, wr3, br)


def _final_norm_kernel(x_ref, g_ref, op_ref, os_ref, *, n_prompt_blocks):
    i = pl.program_id(0)
    y = _rms(x_ref[...], g_ref[...])

    @pl.when(i < n_prompt_blocks)
    def _():
        op_ref[...] = y

    @pl.when(i >= n_prompt_blocks)
    def _():
        os_ref[...] = y


def _final_norm(x3, g, ng_prompt):
    ng, grp, d = x3.shape
    gb = _row_tile(np.gcd(ng_prompt, ng - ng_prompt), 4)
    npb = ng_prompt // gb
    blk = lambda f: pl.BlockSpec((gb, grp, d), f)
    return pl.pallas_call(
        functools.partial(_final_norm_kernel, n_prompt_blocks=npb),
        grid=(ng // gb,),
        in_specs=[blk(lambda i: (i, 0, 0)), pl.BlockSpec((1, d), lambda i: (0, 0))],
        out_specs=[blk(lambda i: (jnp.minimum(i, npb - 1), 0, 0)),
                   blk(lambda i: (jnp.maximum(i - npb, 0), 0, 0))],
        out_shape=[jax.ShapeDtypeStruct((ng_prompt, grp, d), F32),
                   jax.ShapeDtypeStruct((ng - ng_prompt, grp, d), F32)],
        compiler_params=_cparams(1),
    )(x3, g)


def _lanes(c, w):
    if w % LANES == 0:
        return c if w == LANES else jnp.tile(c, (1, w // LANES))
    return c[:, :w]


def _dot_nt(a, b):
    return lax.dot_general(a, b, (((1,), (1,)), ((), ())), preferred_element_type=F32)


def _sb_block(qh, kh, vh, c, u, masked):
    w = kh.shape[0]
    z = _dot_nt(qh, kh)
    sp = jnp.maximum(z, 0.0) + jnp.log2(1.0 + jnp.exp2(-jnp.abs(z)))
    if masked:
        row = lax.broadcasted_iota(jnp.int32, z.shape, 0)
        col = lax.broadcasted_iota(jnp.int32, z.shape, 1)
        valid = col < row
        sp = jnp.where(valid, sp, 0.0)
    hi = sp.astype(BF16)
    lo = (sp - hi.astype(F32)).astype(BF16)
    if w % LANES == 0:
        cs = jnp.dot(jnp.concatenate([hi, lo], axis=1), u, preferred_element_type=F32)
    else:
        cs = (jnp.dot(hi, u[:w], preferred_element_type=F32)
              + jnp.dot(lo, u[w:], preferred_element_type=F32))
    wgt = jnp.exp2(z - sp - cs - _lanes(c, w))
    if masked:
        wgt = jnp.where(valid, wgt, 0.0)
    o = jnp.dot(wgt.astype(BF16), vh, preferred_element_type=F32)
    c_new = c + jnp.sum(sp, axis=1, keepdims=True)
    return o, c_new


def _suffix_matrix(w):
    j = np.arange(w)[:, None]
    s = np.arange(w)[None, :]
    u = (j > s).astype(np.float32)
    return jnp.asarray(np.concatenate([u, u], axis=0), dtype=BF16)


def _sb_stream_kernel(q_ref, kn_ref, vn_ref, k_hbm, v_hbm, ud_ref, up_ref, o_ref,
                      kbuf, vbuf, acc, carry, done, alive_ref, sem, *, heads, tk, cache_tiles,
                      cache_base):
    i = pl.program_id(0)
    n_past = i if cache_tiles is None else cache_tiles

    def fetch(j, slot):
        if cache_tiles is None:
            rows = pl.ds(pl.multiple_of(j * tk, tk), tk)
            src_k, src_v = k_hbm.at[:, rows, :], v_hbm.at[:, rows, :]
        else:
            blk = tk * heads
            rows = pl.ds(pl.multiple_of((cache_base + i * cache_tiles + j) * blk, blk), blk)
            src_k, src_v = k_hbm.at[rows], v_hbm.at[rows]
        return (pltpu.make_async_copy(src_k, kbuf.at[slot], sem.at[0, slot]),
                pltpu.make_async_copy(src_v, vbuf.at[slot], sem.at[1, slot]))

    def tile_of(buf, slot, h):
        if cache_tiles is None:
            return buf[slot, h]
        return buf.at[slot][pl.ds(h, tk, stride=heads), :].astype(BF16)

    @pl.when(n_past > 0)
    def _():
        for c in fetch(n_past - 1, 0):
            c.start()

    def diag(h, _):
        o, c = _sb_block(q_ref[h], kn_ref[h], vn_ref[h], jnp.zeros(carry.shape[1:], F32),
                         ud_ref[...], True)
        acc[h] = o
        carry[h] = c
        done[h] = 0
        return 0
    lax.fori_loop(0, heads, diag, 0, unroll=min(heads, HEAD_UNROLL))

    alive_ref[0] = heads

    @pl.when(n_past > 0)
    def _():
        for c in fetch(n_past - 1, 0):
            c.wait()

        @pl.when(n_past > 1)
        def _():
            for c in fetch(n_past - 2, 1):
                c.start()

        def first(h, alive):
            o, c = _sb_block(q_ref[h], tile_of(kbuf, 0, h), tile_of(vbuf, 0, h), carry[h],
                             up_ref[...], False)
            acc[h] = acc[h] + o
            carry[h] = c
            dead = (jnp.min(c) >= SB_DEAD).astype(jnp.int32)
            done[h] = dead
            return alive + 1 - dead
        alive_ref[0] = lax.fori_loop(0, heads, first, 0, unroll=min(heads, HEAD_UNROLL))

    def cond(state):
        j, alive = state
        return (j >= 0) & (alive > 0)

    def body(state):
        j, _ = state
        slot = lax.rem(n_past - 1 - j, 2)
        for c in fetch(j, slot):
            c.wait()

        @pl.when(j > 0)
        def _():
            for c in fetch(j - 1, 1 - slot):
                c.start()

        def head(h, alive):
            @pl.when(done[h] == 0)
            def _():
                o, c = _sb_block(q_ref[h], tile_of(kbuf, slot, h), tile_of(vbuf, slot, h),
                                 carry[h], up_ref[...], False)
                acc[h] = acc[h] + o
                carry[h] = c
                done[h] = (jnp.min(c) >= SB_DEAD).astype(jnp.int32)
            return alive + 1 - done[h]
        return j - 1, lax.fori_loop(0, heads, head, 0)

    j_end, _ = lax.while_loop(cond, body, (jnp.int32(n_past - 2), alive_ref[0]))

    @pl.when(j_end >= 0)
    def _():
        for c in fetch(j_end, lax.rem(n_past - 1 - j_end, 2)):
            c.wait()

    for h in range(heads):
        o_ref[:, h * DH_SB:(h + 1) * DH_SB] = acc[h].astype(o_ref.dtype)


def _sb_stream_attention(q, kn, vn, k_src, v_src, *, tq, tk, q_block0, n_tiles, cache_tiles=None,
                         cache_base=0):
    heads = q.shape[0]
    blk = pl.BlockSpec((heads, tq, DH_SB), lambda i: (0, q_block0 + i, 0))
    anyspec = pl.BlockSpec(memory_space=pl.ANY)
    if cache_tiles is None:
        buf = pltpu.VMEM((2, heads, tk, DH_SB), BF16)
    else:
        buf = pltpu.VMEM((2, tk * heads, DH_SB), F32)
    kern = functools.partial(_sb_stream_kernel, heads=heads, tk=tk, cache_tiles=cache_tiles,
                             cache_base=cache_base)
    return pl.pallas_call(
        kern, grid=(n_tiles,),
        in_specs=[blk, blk, blk, anyspec, anyspec,
                  pl.BlockSpec((2 * tq, tq), lambda i: (0, 0)),
                  pl.BlockSpec((2 * tk, tk), lambda i: (0, 0))],
        out_specs=pl.BlockSpec((tq, heads * DH_SB), lambda i: (i, 0)),
        out_shape=jax.ShapeDtypeStruct((n_tiles * tq, heads * DH_SB), BF16),
        scratch_shapes=[buf, buf,
                        pltpu.VMEM((heads, tq, DH_SB), F32),
                        pltpu.VMEM((heads, tq, LANES), F32),
                        pltpu.SMEM((heads,), jnp.int32),
                        pltpu.SMEM((1,), jnp.int32),
                        pltpu.SemaphoreType.DMA((2, 2))],
        compiler_params=_cparams(1),
    )(q, kn, vn, k_src, v_src, _suffix_matrix(tq), _suffix_matrix(tk))


def _mla_block(qc, kc, va, m, acc, mask):
    s = _dot_nt(qc, kc)
    if mask is not None:
        s = jnp.where(mask, s, -jnp.inf)
    m_new = jnp.maximum(m, jnp.max(s, axis=1, keepdims=True))
    alpha = jnp.exp2(m - m_new)
    p = jnp.exp2(s - _lanes(m_new, s.shape[1]))
    acc_new = _lanes(alpha, acc.shape[1]) * acc + jnp.dot(p.astype(BF16), va,
                                                          preferred_element_type=F32)
    return m_new, acc_new


def _mla_kernel(qb_ref, kb_ref, fl_ref, q_ref, k_ref, v_ref, o_ref, acc, m_sc, *, heads):
    s = pl.program_id(0)
    fl = fl_ref[s]
    tq, tk = q_ref.shape[1], k_ref.shape[1]

    @pl.when((fl & 1) != 0)
    def _():
        m_sc[...] = jnp.full(m_sc.shape, -jnp.inf, F32)
        acc[...] = jnp.zeros(acc.shape, F32)

    def run(mask):
        def body(h, _):
            m, a = _mla_block(q_ref[h], k_ref[h], v_ref[h], m_sc[h], acc[h], mask)
            m_sc[h] = m
            acc[h] = a
            return 0
        lax.fori_loop(0, heads, body, 0, unroll=min(heads, MLA_UNROLL))

    @pl.when((fl & 4) != 0)
    def _():
        row = lax.broadcasted_iota(jnp.int32, (tq, tk), 0) + qb_ref[s] * tq
        col = lax.broadcasted_iota(jnp.int32, (tq, tk), 1) + kb_ref[s] * tk
        run((col // CHUNK) <= (row // CHUNK))

    @pl.when((fl & 4) == 0)
    def _():
        run(None)

    @pl.when((fl & 2) != 0)
    def _():
        for h in range(heads):
            a = acc[h]
            o_ref[:, h * V_DIM:(h + 1) * V_DIM] = (a[:, :V_DIM] / a[:, V_DIM:]).astype(o_ref.dtype)


def _mla_tables(n_rows, tq, tk):
    r = tq // tk
    qb, kb, fl = [], [], []
    for i in range(n_rows // tq):
        tiles = [(i * r + j, 4) for j in range(r)] + [(j, 0) for j in reversed(range(i * r))]
        for idx, (j, f) in enumerate(tiles):
            qb.append(i)
            kb.append(j)
            fl.append(f | (1 if idx == 0 else 0) | (2 if idx == len(tiles) - 1 else 0))
    arr = lambda v: jnp.asarray(np.asarray(v, dtype=np.int32))
    return arr(qb), arr(kb), arr(fl)


def _mla_attention(qc, kc, va, *, n_rows, tq, tk):
    qb, kb, fl = _mla_tables(n_rows, tq, tk)
    h = H_MLA
    wqk = 2 * LANES
    kv_spec = pl.BlockSpec((h, tk, wqk), lambda s, qb, kb, fl: (0, kb[s], 0))
    grid_spec = pltpu.PrefetchScalarGridSpec(
        num_scalar_prefetch=3,
        grid=(qb.shape[0],),
        in_specs=[pl.BlockSpec((h, tq, wqk), lambda s, qb, kb, fl: (0, qb[s], 0)),
                  kv_spec, kv_spec],
        out_specs=pl.BlockSpec((tq, h * V_DIM), lambda s, qb, kb, fl: (qb[s], 0)),
        scratch_shapes=[pltpu.VMEM((h, tq, 2 * V_DIM), F32),
                        pltpu.VMEM((h, tq, LANES), F32)],
    )
    return pl.pallas_call(
        functools.partial(_mla_kernel, heads=h), grid_spec=grid_spec,
        out_shape=jax.ShapeDtypeStruct((n_rows, h * V_DIM), BF16),
        compiler_params=_cparams(1),
    )(qb, kb, fl, qc, kc, va)


def _mla_latent_kernel(qb_ref, kb_ref, fl_ref, q_ref, cn_ref, rn_ref, cp_ref, rp_ref, w_ref,
                       o_ref, qa, qr, acc, m_sc, l_sc, *, heads, pos0):
    s_id = pl.program_id(0)
    fl = fl_ref[s_id]
    is_first = (fl & 1) != 0
    is_last = (fl & 2) != 0
    tq = q_ref.shape[1]
    hw = QK_NOPE + V_DIM

    def step(ck, kr, mask):
        s = _dot_nt(qa[...], ck) + _dot_nt(qr[...], kr)
        if mask is not None:
            s = jnp.where(mask, s, -jnp.inf)
        m_old = m_sc[...]
        m_new = jnp.maximum(m_old, jnp.max(s, axis=1, keepdims=True))
        alpha = jnp.exp2(m_old - m_new)
        p = jnp.exp2(s - m_new[:, :1])
        l_sc[...] = alpha * l_sc[...] + jnp.sum(p, axis=1, keepdims=True)
        acc[...] = alpha[:, :1] * acc[...] + jnp.dot(p.astype(BF16), ck,
                                                     preferred_element_type=F32)
        m_sc[...] = m_new

    @pl.when(is_first)
    def _():
        for h in range(heads):
            qh = q_ref[h]
            w_uk = w_ref[:, h * hw:h * hw + QK_NOPE]
            qa[h * tq:(h + 1) * tq, :] = _dot_nt(qh[:, :QK_NOPE], w_uk).astype(BF16)
            qr[h * tq:(h + 1) * tq, :] = qh[:, QK_NOPE:]
        m_sc[...] = jnp.full(m_sc.shape, -jnp.inf, F32)
        l_sc[...] = jnp.zeros(l_sc.shape, F32)
        acc[...] = jnp.zeros(acc.shape, F32)
        row = lax.broadcasted_iota(jnp.int32, (heads * tq, tq), 0) % tq + pos0
        col = lax.broadcasted_iota(jnp.int32, (heads * tq, tq), 1) + pos0
        step(cn_ref[...], rn_ref[...], (col // CHUNK) <= (row // CHUNK))

    @pl.when(jnp.logical_not(is_first))
    def _():
        step(cp_ref[...].astype(BF16), rp_ref[...], None)

    @pl.when(is_last)
    def _():
        o_lat = (acc[...] / l_sc[...][:, :1]).astype(BF16)
        for h in range(heads):
            w_uv = w_ref[:, h * hw + QK_NOPE:(h + 1) * hw]
            o_ref[:, h * V_DIM:(h + 1) * V_DIM] = jnp.dot(
                o_lat[h * tq:(h + 1) * tq, :], w_uv, preferred_element_type=F32).astype(o_ref.dtype)


def _mla_latent_attention(qc, c_new, r_new, c_past, r_past, w_ukv_b, tables, l, *, tq, tk, pos0,
                          out_rows, out_block_of):
    qb, kb, fl = tables
    h = H_MLA
    kvl = c_new.shape[1]
    wqk = 2 * LANES
    grid_spec = pltpu.PrefetchScalarGridSpec(
        num_scalar_prefetch=3,
        grid=(qb.shape[0],),
        in_specs=[pl.BlockSpec((h, tq, wqk), lambda s, qb, kb, fl: (0, qb[s], 0)),
                  pl.BlockSpec((tq, kvl), lambda s, qb, kb, fl: (qb[s], 0)),
                  pl.BlockSpec((tq, LANES), lambda s, qb, kb, fl: (qb[s], 0)),
                  pl.BlockSpec((tk, kvl), lambda s, qb, kb, fl: (kb[s], 0)),
                  pl.BlockSpec((tk, LANES), lambda s, qb, kb, fl: (kb[s], 0)),
                  pl.BlockSpec((None, kvl, w_ukv_b.shape[2]), lambda s, qb, kb, fl: (l, 0, 0))],
        out_specs=pl.BlockSpec((tq, h * V_DIM), out_block_of),
        scratch_shapes=[pltpu.VMEM((h * tq, kvl), BF16),
                        pltpu.VMEM((h * tq, LANES), BF16),
                        pltpu.VMEM((h * tq, kvl), F32),
                        pltpu.VMEM((h * tq, LANES), F32),
                        pltpu.VMEM((h * tq, LANES), F32)],
    )
    kern = functools.partial(_mla_latent_kernel, heads=h, pos0=pos0)
    return pl.pallas_call(
        kern, grid_spec=grid_spec,
        out_shape=jax.ShapeDtypeStruct((out_rows, h * V_DIM), BF16),
        compiler_params=_cparams(1),
    )(qb, kb, fl, qc, c_new, r_new, c_past, r_past, w_ukv_b)


def _cache_walk_tables(n_batch, q_off, past_tiles, past_stride):
    qb, kb, fl = [], [], []
    for b in range(n_batch):
        for j in range(1 + past_tiles):
            qb.append(q_off + b)
            kb.append(past_stride + b * past_tiles + past_tiles - max(j, 1))
            fl.append((1 if j == 0 else 0) | (2 if j == past_tiles else 0))
    arr = lambda v: jnp.asarray(np.asarray(v, dtype=np.int32))
    return arr(qb), arr(kb), arr(fl)


ROW_DMA_UNROLL = 8


def _moe_kernel(te_ref, nv_ref, tok_ref, dst_ref, h_hbm, w_ref, wg_ref, wu_ref, wd_ref, y_hbm,
                xbuf, obuf, wg_s, wu_s, wd_s, sem_in, sem_out, *, s):
    t = pl.program_id(0)
    tm = w_ref.shape[0]
    nv = nv_ref[0]
    live = t < nv
    slot = lax.rem(t, 2)

    p = _slab_pitch(s)

    def in_copy(tile, r, sl):
        src0 = pl.multiple_of(tok_ref[tile * tm + r] * p, SLAB_PAD)
        return pltpu.make_async_copy(h_hbm.at[pl.ds(src0, s)],
                                     xbuf.at[sl, pl.ds(pl.multiple_of(r * p, SLAB_PAD), s)],
                                     sem_in.at[sl])

    def gather_start(tile, sl):
        def body(r, _):
            in_copy(tile, r, sl).start()
            return 0
        lax.fori_loop(0, tm, body, 0, unroll=ROW_DMA_UNROLL)

    def gather_wait(tile, sl):
        def body(r, _):
            in_copy(tile, r, sl).wait()
            return 0
        lax.fori_loop(0, tm, body, 0, unroll=ROW_DMA_UNROLL)

    @pl.when(live & (t == 0))
    def _():
        gather_start(0, 0)

    @pl.when(t + 1 < nv)
    def _():
        gather_start(t + 1, 1 - slot)

    prev = te_ref[jnp.maximum(t - 1, 0)]
    fresh = (t == 0) | (te_ref[t] != prev)

    @pl.when(live & fresh)
    def _():
        wg_s[...] = wg_ref[...].astype(BF16)
        wu_s[...] = wu_ref[...].astype(BF16)
        wd_s[...] = wd_ref[...].astype(BF16)

    @pl.when(live)
    def _():
        gather_wait(t, slot)
        x = _slab_load(xbuf, 0, tm, s, lead=slot).astype(BF16)
        a = jnp.dot(x, wg_s[...], preferred_element_type=F32)
        u = jnp.dot(x, wu_s[...], preferred_element_type=F32)
        hid = (a * jax.nn.sigmoid(a)) * u * w_ref[...]
        y = jnp.dot(hid.astype(BF16), wd_s[...], preferred_element_type=F32)

        def out_copy(tile, r):
            dst0 = pl.multiple_of(dst_ref[tile * tm + r] * p, SLAB_PAD)
            return pltpu.make_async_copy(obuf.at[pl.ds(pl.multiple_of(r * p, SLAB_PAD), s)],
                                         y_hbm.at[pl.ds(dst0, s)], sem_out)

        def scatter_wait(tile):
            def wait(r, _):
                out_copy(tile, r).wait()
                return 0
            lax.fori_loop(0, tm, wait, 0, unroll=ROW_DMA_UNROLL)

        @pl.when(t > 0)
        def _():
            scatter_wait(t - 1)

        _slab_store(obuf, 0, y)

        def start(r, _):
            out_copy(t, r).start()
            return 0
        lax.fori_loop(0, tm, start, 0, unroll=ROW_DMA_UNROLL)

        @pl.when(t == nv - 1)
        def _():
            scatter_wait(t)


def _moe_experts(h_slab, row_tok, row_dst, row_w, tile_expert, n_valid, w_gate, w_up, w_down, l,
                 n_dest):
    d, f = w_gate.shape[-2:]
    s = d // LANES
    p = _slab_pitch(s)
    tm = MOE_TILE
    n_tiles = row_tok.shape[0] // tm
    wmap = lambda t, te, nv, tok, dst: (l, te[t], 0, 0)
    grid_spec = pltpu.PrefetchScalarGridSpec(
        num_scalar_prefetch=4, grid=(n_tiles,),
        in_specs=[pl.BlockSpec(memory_space=pl.ANY),
                  pl.BlockSpec((tm, 1), lambda t, te, nv, tok, dst: (t, 0)),
                  pl.BlockSpec((None, None, d, f), wmap),
                  pl.BlockSpec((None, None, d, f), wmap),
                  pl.BlockSpec((None, None, f, d), wmap)],
        out_specs=pl.BlockSpec(memory_space=pl.ANY),
        scratch_shapes=[pltpu.VMEM((2, tm * p, LANES), F32), pltpu.VMEM((tm * p, LANES), F32),
                        pltpu.VMEM((d, f), BF16), pltpu.VMEM((d, f), BF16),
                        pltpu.VMEM((f, d), BF16),
                        pltpu.SemaphoreType.DMA((2,)), pltpu.SemaphoreType.DMA(())],
    )
    return pl.pallas_call(
        functools.partial(_moe_kernel, s=s), grid_spec=grid_spec,
        out_shape=jax.ShapeDtypeStruct(((n_dest + tm) * p, LANES), F32),
        compiler_params=_cparams(1),
    )(tile_expert, n_valid, row_tok, row_dst, h_slab, row_w, w_gate, w_up, w_down)


def _combine_kernel(x_ref, y0_ref, y1_ref, g_ref, o_ref):
    gb, grp, d = x_ref.shape
    s = d // LANES
    g0 = pl.program_id(0) * gb
    for g in range(gb):
        y = _slab_load(y0_ref, g * grp, grp, s) + _slab_load(y1_ref, g * grp, grp, s)
        o_ref[g] = x_ref[g] + _mod_row(g_ref, g0 + g) * y


def _combine(x3, yg, modg, l, g_i):
    ng, grp, d = x3.shape
    s = d // LANES
    gb = _row_tile(ng, 2)
    blk = pl.BlockSpec((gb, grp, d), lambda i: (i, 0, 0))
    yblk = lambda off: pl.BlockSpec((gb * grp * _slab_pitch(s), LANES), lambda i: (i + off, 0))
    return pl.pallas_call(
        _combine_kernel,
        grid=(ng // gb,),
        in_specs=[blk, yblk(0), yblk(ng // gb), _mod_spec(g_i, ng, d)],
        out_specs=blk,
        out_shape=jax.ShapeDtypeStruct(x3.shape, F32),
        compiler_params=_cparams(1),
    )(x3, yg, yg, modg)


def _dispatch(route, n):
    tm = MOE_TILE
    r_cap = (2 * n + N_EXPERTS * (tm - 1) + tm - 1) // tm * tm
    e = route[:, :2].astype(jnp.int32)
    w = route[:, 2:4]
    flat_e = e.T.reshape(-1)
    flat_w = w.T.reshape(-1)
    order = jnp.argsort(flat_e, stable=True).astype(jnp.int32)
    bounds = jnp.searchsorted(flat_e[order], jnp.arange(N_EXPERTS + 1, dtype=jnp.int32),
                              side="left").astype(jnp.int32)
    counts = bounds[1:] - bounds[:-1]
    padded = (counts + tm - 1) // tm * tm
    ends_p = jnp.cumsum(padded)
    starts_p = ends_p - padded
    starts = jnp.cumsum(counts) - counts
    tile_start = jnp.arange(r_cap // tm, dtype=jnp.int32) * tm
    tile_expert = jnp.minimum(jnp.searchsorted(ends_p, tile_start, side="right"),
                              N_EXPERTS - 1).astype(jnp.int32)
    n_valid = (ends_p[-1:] // tm).astype(jnp.int32)
    per_row = lambda v: jnp.broadcast_to(v[:, None], (r_cap // tm, tm)).reshape(r_cap)
    local = jnp.arange(r_cap, dtype=jnp.int32) - per_row(starts_p[tile_expert])
    valid = local < per_row(counts[tile_expert])
    slot = order[jnp.clip(per_row(starts[tile_expert]) + local, 0, 2 * n - 1)]
    row_dst = jnp.where(valid, slot, 2 * n + jnp.arange(r_cap, dtype=jnp.int32) % tm)
    row_tok = jnp.where(valid, jnp.where(slot >= n, slot - n, slot), 0)
    row_w = jnp.where(valid, flat_w[slot], 0.0)
    return row_tok, row_dst, row_w.reshape(r_cap, 1), tile_expert, n_valid


def _gates_kernel(h_ref, wa_ref, wb_ref, o_ref, bsc):
    @pl.when(pl.program_id(1) == 0)
    def _():
        r = wb_ref.shape[0]
        w = wa_ref.shape[0]
        bsc[:w - r, :] = wa_ref[r:, :].astype(BF16)
        bsc[w - r:, :] = wb_ref[...].astype(BF16)

    o_ref[...] = jax.nn.sigmoid(_dot_nt(h_ref[...], bsc[...]))


def _gates(h, w_nk, l, off_kr, n_gate, tm, tn):
    n, k = h.shape
    assert off_kr % tn == 0 and n_gate % tn == 0 and tn % QK_ROPE == 0
    rb = off_kr // tn
    return pl.pallas_call(
        _gates_kernel,
        grid=(n_gate // tn, n // tm),
        in_specs=[pl.BlockSpec((tm, k), lambda j, i: (i, 0)),
                  pl.BlockSpec((None, tn, k), lambda j, i: (l, rb + j, 0)),
                  pl.BlockSpec((None, QK_ROPE, k),
                               lambda j, i: (l, (rb + j + 1) * (tn // QK_ROPE), 0))],
        out_specs=pl.BlockSpec((tm, tn), lambda j, i: (i, j)),
        out_shape=jax.ShapeDtypeStruct((n, n_gate), F32),
        scratch_shapes=[pltpu.VMEM((tn, k), BF16)],
        compiler_params=_cparams(2),
    )(h, w_nk, w_nk)


def _rope_weights_kernel(a_ref, o_ref):
    a = a_ref[...].astype(BF16)
    q = a.shape[0] // 2
    o_ref[...] = jnp.concatenate([a, a[q:], a[:q]], axis=0)


def _rope_key_weights(w_nk, off_kr):
    depth, n_in, k = w_nk.shape
    assert QK_ROPE * 2 == LANES and off_kr % QK_ROPE == 0
    return pl.pallas_call(
        _rope_weights_kernel,
        grid=(depth,),
        in_specs=[pl.BlockSpec((None, QK_ROPE, k), lambda l: (l, off_kr // QK_ROPE, 0))],
        out_specs=pl.BlockSpec((None, 2 * QK_ROPE, k), lambda l: (l, 0, 0)),
        out_shape=jax.ShapeDtypeStruct((depth, 2 * QK_ROPE, k), BF16),
        compiler_params=_cparams(1),
    )(w_nk)


def _rope_tables(pos):
    inv = ROPE_THETA ** (-jnp.arange(0, QK_ROPE, 2, dtype=F32) / QK_ROPE)
    ang = pos.astype(F32)[:, None] * inv[None, :]
    cos, sin = jnp.cos(ang), jnp.sin(ang)
    pad = jnp.zeros((pos.shape[0], LANES - QK_ROPE), F32)
    return (jnp.concatenate([cos, cos, pad], axis=1),
            jnp.concatenate([-sin, sin, pad], axis=1))


def _rope_lanes(acc, c, s):
    return acc * c + pltpu.roll(acc, LANES - QK_ROPE, axis=1) * s


def _swap_halves(w):
    half = w.shape[-1] // 2
    return jnp.concatenate([w[..., half:], w[..., :half]], axis=-1)


def kernel(x_prompt, x_sample, c_prompt, c_sample, cache_sb_k, cache_sb_v, cache_mla_ckv,
           cache_mla_krope, w_ada, b_ada, g_norm_mix, g_norm_ffn, w_in, g_q_lat, g_kv_lat,
           w_uq, w_ukv, w_branch_sb, w_branch_mla, w_out, w_router_group, b_router_group,
           w_router_expert, b_router_expert, w_exp_gate, w_exp_up, w_exp_down, g_final):
    bp, t_p, d = x_prompt.shape
    bs, t_s, _ = x_sample.shape
    depth = w_in.shape[0]
    past = cache_sb_k.shape[2]
    grp = t_s
    n_p, n_s = bp * t_p, bs * t_s
    n = n_p + n_s
    ng = n // grp
    sb_w = H_SB * DH_SB
    q_lora = g_q_lat.shape[1]
    kv_lora = g_kv_lat.shape[1]
    tm = _token_tile(n, grp)
    gpt = tm // grp
    tn = 512
    assert bp == 1 and t_p % ATTN_TILE == 0 and t_p % grp == 0 and past % 512 == 0

    x3 = jnp.concatenate([x_prompt.reshape(n_p // grp, grp, d), x_sample], axis=0)

    n_c = bp + bs
    c_rows = 16
    c_all = jnp.zeros((c_rows, d), F32).at[:n_c].set(jnp.concatenate([c_prompt, c_sample], 0))
    n_modc = N_MOD * d

    def ada_epi(accs, ex, outs):
        outs[0][...] = accs[0] + ex[0][...]

    mods = []
    for l in range(depth):
        mods.append(_matmul(
            [(c_all, w_ada, _wspec(l, d, tn, 0), True)], m=c_rows, n_out=n_modc, tm=c_rows, tn=tn,
            prologue=lambda a, ex: a * jax.nn.sigmoid(a),
            epilogue=ada_epi,
            extras=[b_ada.reshape(depth, 1, n_modc)],
            extra_specs=[pl.BlockSpec((None, 1, tn), lambda j, i, l=l: (l, 0, j))],
            out_shape=[jax.ShapeDtypeStruct((c_rows, n_modc), F32)],
            out_specs=[_spec2(c_rows, tn)])[0])
    mod = jnp.stack(mods)
    modg = jnp.concatenate(
        [jnp.broadcast_to(mod[:, :bp], (depth, n_p // grp, n_modc)), mod[:, bp:n_c]], axis=1)
    modg = modg.reshape(depth, ng, N_MOD, d).transpose(0, 2, 1, 3)
    modg = modg.reshape(depth * N_MOD, ng, d)

    def mod_idx(l, k):
        return l * N_MOD + k

    def mspec_rows(l, k):
        return pl.BlockSpec((None, ng, tn), lambda j, i: (mod_idx(l, k), 0, j))

    pos = jnp.concatenate([jnp.arange(t_p, dtype=jnp.int32),
                           jnp.tile(past + jnp.arange(t_s, dtype=jnp.int32), bs)])
    rope_c, rope_s = _rope_tables(pos)
    tq = ATTN_TILE
    tk_s = _row_tile(past, 1024)
    pt = past // tk_s

    off_q, off_k, off_v = 0, sb_w, 2 * sb_w
    off_cq = 3 * sb_w
    off_ckv = off_cq + q_lora
    off_kr = off_ckv + kv_lora
    off_g = off_kr + QK_ROPE
    assert off_g == off_kr + QK_ROPE
    w_nk = jnp.swapaxes(w_in, 1, 2)
    w_kr_aug = _rope_key_weights(w_nk, off_kr)
    uq = w_uq.reshape(depth, q_lora, H_MLA, QK_NOPE + QK_ROPE)
    uq_r = uq[..., QK_NOPE:]
    w_uq_cat = jnp.concatenate([uq[..., :QK_NOPE], uq_r, _swap_halves(uq_r)], axis=-1)
    w_uq_cat = w_uq_cat.reshape(depth, q_lora, H_MLA * 2 * LANES)
    w_router = jnp.concatenate(
        [w_router_expert, w_router_group,
         jnp.zeros((depth, d, LANES - N_EXPERTS - N_GROUPS), F32)], axis=-1)
    r_hi = w_router.astype(BF16)
    r_lo = (w_router - r_hi.astype(F32)).astype(BF16)
    w_router3 = jnp.stack([r_hi, r_lo], axis=1)
    b_router = jnp.concatenate(
        [b_router_expert, b_router_group,
         jnp.zeros((depth, LANES - N_EXPERTS - N_GROUPS), F32)], axis=-1).reshape(depth, 1, LANES)

    rows_past = bs * past
    kr_past = jnp.pad(cache_mla_krope.reshape(depth * rows_past, QK_ROPE),
                      ((0, 0), (0, LANES - QK_ROPE))).astype(BF16)
    w_ukv_b = w_ukv.astype(BF16)
    hm = lambda rows: jax.ShapeDtypeStruct((H_SB, rows, LANES), BF16)
    new_k, new_v, new_c, new_r = [], [], [], []

    for l in range(depth):
        h = _norm_mod(x3, g_norm_mix.reshape(depth, 1, d), modg, l, mod_idx(l, 1), mod_idx(l, 0))
        h = h.reshape(n, d)

        sbq_scale = DH_SB ** -0.5 * float(np.log2(np.e))

        def plain_hm(accs, ex, outs):
            _store_heads(outs[0], accs[0] * sbq_scale)

        def f32_and_hm(accs, ex, outs):
            outs[0][...] = accs[0]
            _store_heads(outs[1], accs[0])

        sb_q = _matmul([(h, w_nk, _wspec_nk(l, d, tn, off_q), True, True)], m=n, n_out=sb_w, tm=tm, tn=tn,
                       epilogue=plain_hm, out_shape=[hm(n)], out_specs=[_hm_spec(tm, tn)])[0]
        k_f32, sb_k = _matmul([(h, w_nk, _wspec_nk(l, d, tn, off_k), True, True)], m=n, n_out=sb_w, tm=tm,
                              tn=tn, epilogue=f32_and_hm,
                              out_shape=[jax.ShapeDtypeStruct((n, sb_w), F32), hm(n)],
                              out_specs=[_spec2(tm, tn), _hm_spec(tm, tn)])
        v_f32, sb_v = _matmul([(h, w_nk, _wspec_nk(l, d, tn, off_v), True, True)], m=n, n_out=sb_w, tm=tm,
                              tn=tn, epilogue=f32_and_hm,
                              out_shape=[jax.ShapeDtypeStruct((n, sb_w), F32), hm(n)],
                              out_specs=[_spec2(tm, tn), _hm_spec(tm, tn)])

        def plain_f32(accs, ex, outs):
            outs[0][...] = accs[0]

        c_q = _matmul([(h, w_nk, _wspec_nk(l, d, tn, off_cq), True, True)], m=n, n_out=q_lora, tm=tm, tn=tn,
                      epilogue=plain_f32, out_shape=[jax.ShapeDtypeStruct((n, q_lora), F32)],
                      out_specs=[_spec2(tm, tn)])[0]

        def ckv_epi(accs, ex, outs):
            y = _rms(accs[0], ex[0][...])
            outs[0][...] = y
            outs[1][...] = y.astype(BF16)

        c_kv, c_kv_b = _matmul(
            [(h, w_nk, _wspec_nk(l, d, kv_lora, off_ckv), True, True)], m=n, n_out=kv_lora, tm=tm, tn=kv_lora,
            epilogue=ckv_epi, extras=[g_kv_lat.reshape(depth, 1, kv_lora)],
            extra_specs=[pl.BlockSpec((None, 1, kv_lora), lambda j, i: (l, 0, 0))],
            out_shape=[jax.ShapeDtypeStruct((n, kv_lora), F32),
                       jax.ShapeDtypeStruct((n, kv_lora), BF16)],
            out_specs=[_spec2(tm, kv_lora), _spec2(tm, kv_lora)])

        def kr_epi(accs, ex, outs):
            r = _rope_lanes(accs[0], ex[0][...], ex[1][...])
            outs[0][...] = r
            outs[1][...] = r.astype(BF16)

        rope_specs = [pl.BlockSpec((tm, LANES), lambda j, i: (i, 0))] * 2
        k_r, k_r_b = _matmul(
            [(h, w_kr_aug, pl.BlockSpec((None, LANES, d), lambda j, i: (l, 0, 0)), False, True)],
            m=n, n_out=LANES, tm=tm, tn=LANES, epilogue=kr_epi,
            extras=[rope_c, rope_s], extra_specs=rope_specs,
            out_shape=[jax.ShapeDtypeStruct((n, LANES), F32), jax.ShapeDtypeStruct((n, LANES), BF16)],
            out_specs=[_spec2(tm, LANES), _spec2(tm, LANES)])

        gates = _gates(h, w_nk, l, off_kr, 2 * d, tm, tn)

        def cq_prologue(a, ex):
            return _rms(a, ex[0][...])

        gq_spec = pl.BlockSpec((None, 1, q_lora), lambda j, i: (l, 0, 0))
        gq = g_q_lat.reshape(depth, 1, q_lora)
        wqk = 2 * LANES
        hm_qk = lambda rows: jax.ShapeDtypeStruct((H_MLA, rows, wqk), BF16)

        q_scale = (QK_NOPE + QK_ROPE) ** -0.5 * float(np.log2(np.e))

        tn_kv = min(2048, H_MLA * wqk)

        def qcat_epi(accs, ex, outs):
            c, s = ex[1][...], ex[2][...]
            for hh in range(tn_kv // wqk):
                blk = accs[0][:, hh * wqk:(hh + 1) * wqk]
                outs[0][hh, :, :LANES] = (blk[:, :LANES] * q_scale).astype(BF16)
                outs[0][hh, :, LANES:] = (_rope_lanes(blk[:, LANES:], c, s) * q_scale).astype(BF16)

        q_cat = _matmul(
            [(c_q, w_uq_cat, pl.BlockSpec((None, q_lora, tn_kv), lambda j, i: (l, 0, j)), True)],
            m=n, n_out=H_MLA * wqk, tm=tm, tn=tn_kv, prologue=cq_prologue, epilogue=qcat_epi,
            extras=[gq, rope_c, rope_s], extra_specs=[gq_spec] + rope_specs,
            out_shape=[hm_qk(n)],
            out_specs=[pl.BlockSpec((tn_kv // wqk, tm, wqk), lambda j, i: (j, i, 0))])[0]

        hpt = tn_kv // wqk

        def kv_epi(accs, ex, outs):
            kr = ex[0][...]
            ones = jnp.ones(kr.shape, BF16)
            for hh in range(hpt):
                outs[0][hh, :, :LANES] = accs[0][:, hh * wqk:hh * wqk + LANES].astype(BF16)
                outs[0][hh, :, LANES:] = kr
                outs[1][hh, :, :LANES] = accs[0][:, hh * wqk + LANES:(hh + 1) * wqk].astype(BF16)
                outs[1][hh, :, LANES:] = ones

        def up_kv(a, kr, rows, tmr):
            hspec = pl.BlockSpec((hpt, tmr, wqk), lambda j, i: (j, i, 0))
            return _matmul(
                [(a, w_ukv, pl.BlockSpec((None, kv_lora, tn_kv), lambda j, i: (l, 0, j)), True)],
                m=rows, n_out=H_MLA * wqk, tm=tmr, tn=tn_kv, epilogue=kv_epi, extras=[kr],
                extra_specs=[pl.BlockSpec((tmr, LANES), lambda j, i: (i, 0))],
                out_shape=[hm_qk(rows), hm_qk(rows)], out_specs=[hspec, hspec])

        kc_new, v_new = up_kv(c_kv_b, k_r_b, n, tm)

        o_sb_p = _sb_stream_attention(sb_q, sb_k, sb_v, sb_k, sb_v, tq=tq, tk=tq, q_block0=0,
                                      n_tiles=n_p // tq)
        cache_k2 = cache_sb_k.reshape(depth * bs * past * H_SB, DH_SB)
        cache_v2 = cache_sb_v.reshape(depth * bs * past * H_SB, DH_SB)
        q_off_s = n_p // t_s
        o_sb_s = _sb_stream_attention(sb_q, sb_k, sb_v, cache_k2, cache_v2, tq=t_s, tk=ATTN_TILE,
                                      q_block0=q_off_s, n_tiles=bs,
                                      cache_tiles=past // ATTN_TILE,
                                      cache_base=l * bs * (past // ATTN_TILE))
        o_sb = jnp.concatenate([o_sb_p, o_sb_s], axis=0)

        o_mla_p = _mla_attention(q_cat, kc_new, v_new, n_rows=n_p,
                                 tq=min(MLA_Q_TILE, n_p), tk=min(MLA_Q_TILE, n_p))
        o_mla_s = _mla_latent_attention(
            q_cat, c_kv_b, k_r_b, cache_mla_ckv.reshape(depth * rows_past, kv_lora), kr_past,
            w_ukv_b, _cache_walk_tables(bs, q_off_s, pt, l * bs * pt), l,
            tq=t_s, tk=tk_s, pos0=past, out_rows=n_s,
            out_block_of=lambda s, qb, kb, fl: (qb[s] - q_off_s, 0))
        o_mla = jnp.concatenate([o_mla_p, o_mla_s], axis=0)

        def merge_epi(accs, ex, outs):
            outs[0][...] = (ex[0][...] * accs[0] + ex[1][...] * accs[1]).astype(BF16)

        merged = _matmul(
            [(o_sb, w_branch_sb, _wspec(l, sb_w, tn, 0), True),
             (o_mla, w_branch_mla, _wspec(l, H_MLA * V_DIM, tn, 0), True)],
            m=n, n_out=d, tm=tm, tn=tn, epilogue=merge_epi,
            extras=[gates, gates],
            extra_specs=[pl.BlockSpec((tm, tn), lambda j, i: (i, j)),
                         pl.BlockSpec((tm, tn), lambda j, i: (i, j + d // tn))],
            out_shape=[jax.ShapeDtypeStruct((n, d), BF16)], out_specs=[_spec2(tm, tn)])[0]

        def resid_epi(accs, ex, outs):
            g0 = pl.program_id(1) * gpt
            for g in range(gpt):
                outs[0][g] = ex[0][g] + _mod_row(ex[1], g0 + g) * accs[0][g * grp:(g + 1) * grp, :]

        x_spec = pl.BlockSpec((gpt, grp, tn), lambda j, i: (i, 0, j))
        x3 = _matmul(
            [(merged, w_out, _wspec(l, d, tn, 0), True)], m=n, n_out=d, tm=tm, tn=tn,
            epilogue=resid_epi, extras=[x3, modg], extra_specs=[x_spec, mspec_rows(l, 2)],
            out_shape=[jax.ShapeDtypeStruct((ng, grp, d), F32)], out_specs=[x_spec])[0]

        h2, route = _norm_route(x3, g_norm_ffn.reshape(depth, 1, d), modg, l, mod_idx(l, 4),
                                mod_idx(l, 3), w_router3, b_router)
        row_tok, row_dst, row_w, tile_expert, n_valid = _dispatch(route.reshape(n, LANES), n)
        yg = _moe_experts(h2, row_tok, row_dst, row_w, tile_expert, n_valid,
                          w_exp_gate, w_exp_up, w_exp_down, l, 2 * n)
        x3 = _combine(x3, yg, modg, l, mod_idx(l, 5))

        new_k.append(k_f32)
        new_v.append(v_f32)
        new_c.append(c_kv)
        new_r.append(k_r[:, :QK_ROPE])

    y_p, y_s = _final_norm(x3, g_final.reshape(1, d), n_p // grp)

    def split(parts, tail):
        a = jnp.stack(parts)
        return (a[:, :n_p].reshape((depth, bp, t_p) + tail),
                a[:, n_p:].reshape((depth, bs, t_s) + tail))

    pk, sk = split(new_k, (H_SB, DH_SB))
    pv, sv = split(new_v, (H_SB, DH_SB))
    pc, sc = split(new_c, (kv_lora,))
    pr, sr = split(new_r, (QK_ROPE,))
    return (y_p.reshape(bp, t_p, d), y_s.reshape(bs, t_s, d), pk, pv, pc, pr, sk, sv, sc, sr)
```

```python
import functools

import numpy as np
import jax
import jax.numpy as jnp
from jax import lax
from jax.experimental import pallas as pl
from jax.experimental.pallas import tpu as pltpu

F32 = jnp.float32
BF16 = jnp.bfloat16

CHUNK = 64
H_SB = 16
DH_SB = 128
H_MLA = 16
QK_NOPE = 128
QK_ROPE = 64
V_DIM = 128
ROPE_THETA = 10000.0
N_GROUPS = 4
EXPERTS_PER_GROUP = 8
N_EXPERTS = N_GROUPS * EXPERTS_PER_GROUP
N_MOD = 6
EPS = 1e-6

LANES = 128
ATTN_TILE = 256
MOE_TILE = 256
HEAD_UNROLL = 4
MLA_UNROLL = 8
MLA_Q_TILE = 512
SLAB_PAD = 4
SB_DEAD = 152.0
VMEM_LIMIT = 56 * 1024 * 1024


def _cparams(n_axes, vmem=VMEM_LIMIT):
    return pltpu.CompilerParams(dimension_semantics=("arbitrary",) * n_axes,
                                vmem_limit_bytes=vmem)


def _row_tile(n, cap=512):
    t = cap
    while n % t:
        t //= 2
    return t


def _token_tile(n, grp, cap=1152):
    return max(t for t in range(grp, cap + 1, grp) if n % t == 0)


def _matmul(pairs, *, m, n_out, tm, tn, epilogue, out_shape, out_specs,
            extras=(), extra_specs=(), prologue=None):
    n_pairs = len(pairs)
    n_ex = len(extras)
    n_outs = len(out_shape)
    pairs = [tuple(p) + (False,) * (5 - len(p)) for p in pairs]
    cast = [p[3] for p in pairs]
    b_nk = [p[4] for p in pairs]

    def kern(*refs):
        a_refs = refs[0:2 * n_pairs:2]
        b_refs = refs[1:2 * n_pairs:2]
        ex = refs[2 * n_pairs:2 * n_pairs + n_ex]
        outs = refs[2 * n_pairs + n_ex:2 * n_pairs + n_ex + n_outs]
        scr = refs[2 * n_pairs + n_ex + n_outs:]
        i = pl.program_id(1)
        accs = []
        si = 0
        for p in range(n_pairs):
            if cast[p]:
                bsc = scr[si]
                si += 1

                @pl.when(i == 0)
                def _(bsc=bsc, b_ref=b_refs[p]):
                    bsc[...] = b_ref[...].astype(BF16)

                bv = bsc[...]
            else:
                bv = b_refs[p][...]
            a = a_refs[p][...]
            if prologue is not None:
                a = prologue(a, ex)
            if b_nk[p]:
                accs.append(_dot_nt(a.astype(BF16), bv))
            else:
                accs.append(jnp.dot(a.astype(BF16), bv, preferred_element_type=F32))
        epilogue(accs, ex, outs)

    in_specs, args, scratch = [], [], []
    for (a, b, b_spec, cb, nk) in pairs:
        k = a.shape[1]
        in_specs += [pl.BlockSpec((tm, k), lambda j, i: (i, 0)), b_spec]
        args += [a, b]
        if cb:
            scratch.append(pltpu.VMEM((tn, k) if nk else (k, tn), BF16))
    in_specs += list(extra_specs)
    args += list(extras)
    return pl.pallas_call(
        kern,
        grid=(n_out // tn, m // tm),
        in_specs=in_specs,
        out_specs=out_specs,
        out_shape=out_shape,
        scratch_shapes=scratch,
        compiler_params=_cparams(2),
    )(*args)


def _wspec(l, k, tn, col_off):
    cb = col_off // tn
    assert cb * tn == col_off
    return pl.BlockSpec((None, k, tn), lambda j, i: (l, 0, cb + j))


def _wspec_nk(l, k, tn, row_off):
    rb = row_off // tn
    assert rb * tn == row_off
    return pl.BlockSpec((None, tn, k), lambda j, i: (l, rb + j, 0))


def _spec2(tm, tn):
    return pl.BlockSpec((tm, tn), lambda j, i: (i, j))


def _hm_spec(tm, tn):
    return pl.BlockSpec((tn // LANES, tm, LANES), lambda j, i: (j, i, 0))


def _store_heads(o_ref, val):
    for c in range(val.shape[1] // LANES):
        o_ref[c] = val[:, c * LANES:(c + 1) * LANES].astype(o_ref.dtype)


def _rms(x, g):
    return x * lax.rsqrt(jnp.mean(x * x, axis=-1, keepdims=True) + EPS) * g


def _mod_row(ref, g):
    return ref[pl.ds(g, 1), :]


def _norm_mod_kernel(x_ref, g_ref, sc_ref, sh_ref, o_ref):
    gb = x_ref.shape[0]
    g0 = pl.program_id(0) * gb
    for g in range(gb):
        y = _rms(x_ref[g], g_ref[...])
        o_ref[g] = (y * (1.0 + _mod_row(sc_ref, g0 + g)) + _mod_row(sh_ref, g0 + g)).astype(o_ref.dtype)


def _split2(x):
    hi = x.astype(BF16)
    return hi, (x - hi.astype(F32)).astype(BF16)


def _dot_f32(a, b2):
    a_hi, a_lo = _split2(a)
    b_hi, b_lo = b2
    d = functools.partial(jnp.dot, preferred_element_type=F32)
    return d(a_hi, b_hi) + (d(a_hi, b_lo) + d(a_lo, b_hi))


def _route(logits):
    lane = lax.broadcasted_iota(jnp.int32, logits.shape, 1)
    lanef = lane.astype(F32)
    big = jnp.float32(1e9)
    ninf = jnp.float32(-jnp.inf)
    is_g = (lane >= N_EXPERTS) & (lane < N_EXPERTS + N_GROUPS)
    gl = jnp.where(is_g, logits, ninf)
    gmax = jnp.max(gl, axis=1, keepdims=True)
    g_idx = jnp.min(jnp.where(gl == gmax, lanef - N_EXPERTS, big), axis=1, keepdims=True)
    p_group = 1.0 / jnp.sum(jnp.where(is_g, jnp.exp(gl - gmax), 0.0), axis=1, keepdims=True)
    grp = jnp.floor(lanef * (1.0 / EXPERTS_PER_GROUP))
    in_g = (lane < N_EXPERTS) & (grp == g_idx)
    el = jnp.where(in_g, logits, ninf)
    e1 = jnp.max(el, axis=1, keepdims=True)
    i1 = jnp.min(jnp.where(el == e1, lanef, big), axis=1, keepdims=True)
    el2 = jnp.where(lanef == i1, ninf, el)
    e2 = jnp.max(el2, axis=1, keepdims=True)
    i2 = jnp.min(jnp.where(el2 == e2, lanef, big), axis=1, keepdims=True)
    t = jnp.exp(e2 - e1)
    den = 1.0 + t
    w1 = (1.0 / den) * p_group
    w2 = (t / den) * p_group
    out = jnp.where(lane == 0, i1, jnp.where(lane == 1, i2,
          jnp.where(lane == 2, w1, jnp.where(lane == 3, w2, 0.0))))
    return out


def _slab_pitch(s):
    return s + SLAB_PAD


def _slab_store(ref, row0, val):
    rows, d = val.shape
    s = d // LANES
    p = _slab_pitch(s)
    for c in range(s):
        ref[pl.ds(row0 * p + c, rows, stride=p), :] = val[:, c * LANES:(c + 1) * LANES]


def _slab_load(ref, row0, rows, s, lead=None):
    pieces = []
    p = _slab_pitch(s)
    for c in range(s):
        rs = pl.ds(row0 * p + c, rows, stride=p)
        pieces.append(ref[rs, :] if lead is None else ref[lead, rs, :])
    return jnp.concatenate(pieces, axis=1)


def _norm_route_kernel(x_ref, g_ref, sc_ref, sh_ref, wr_ref, br_ref, h_ref, r_ref):
    gb, grp, _ = x_ref.shape
    g0 = pl.program_id(0) * gb
    b3 = (wr_ref[0], wr_ref[1])
    for g in range(gb):
        y = _rms(x_ref[g], g_ref[...])
        h = y * (1.0 + _mod_row(sc_ref, g0 + g)) + _mod_row(sh_ref, g0 + g)
        _slab_store(h_ref, g * grp, h)
        logits = _dot_f32(h, b3) + br_ref[...]
        r_ref[g] = _route(logits)


def _mod_spec(idx, ng, d):
    return pl.BlockSpec((None, ng, d), lambda i: (idx, 0, 0))


def _norm_mod(x3, g, modg, l, sc_i, sh_i):
    ng, grp, d = x3.shape
    gb = _row_tile(ng, 4)
    return pl.pallas_call(
        _norm_mod_kernel,
        grid=(ng // gb,),
        in_specs=[pl.BlockSpec((gb, grp, d), lambda i: (i, 0, 0)),
                  pl.BlockSpec((None, 1, d), lambda i: (l, 0, 0)),
                  _mod_spec(sc_i, ng, d), _mod_spec(sh_i, ng, d)],
        out_specs=pl.BlockSpec((gb, grp, d), lambda i: (i, 0, 0)),
        out_shape=jax.ShapeDtypeStruct(x3.shape, BF16),
        compiler_params=_cparams(1),
    )(x3, g, modg, modg)


def _norm_route(x3, g, modg, l, sc_i, sh_i, wr3, br):
    ng, grp, d = x3.shape
    gb = _row_tile(ng, 4)
    return pl.pallas_call(
        _norm_route_kernel,
        grid=(ng // gb,),
        in_specs=[pl.BlockSpec((gb, grp, d), lambda i: (i, 0, 0)),
                  pl.BlockSpec((None, 1, d), lambda i: (l, 0, 0)),
                  _mod_spec(sc_i, ng, d), _mod_spec(sh_i, ng, d),
                  pl.BlockSpec((None, 2, d, LANES), lambda i: (l, 0, 0, 0)),
                  pl.BlockSpec((None, 1, LANES), lambda i: (l, 0, 0))],
        out_specs=[pl.BlockSpec((gb * grp * _slab_pitch(d // LANES), LANES), lambda i: (i, 0)),
                   pl.BlockSpec((gb, grp, LANES), lambda i: (i, 0, 0))],
        out_shape=[jax.ShapeDtypeStruct((ng * grp * _slab_pitch(d // LANES), LANES), F32),
                   jax.ShapeDtypeStruct((ng, grp, LANES), F32)],
        compiler_params=_cparams(1),
    )(x3, g, modg, modg, wr3, br)


def _final_norm_kernel(x_ref, g_ref, op_ref, os_ref, *, n_prompt_blocks):
    i = pl.program_id(0)
    y = _rms(x_ref[...], g_ref[...])

    @pl.when(i < n_prompt_blocks)
    def _():
        op_ref[...] = y

    @pl.when(i >= n_prompt_blocks)
    def _():
        os_ref[...] = y


def _final_norm(x3, g, ng_prompt):
    ng, grp, d = x3.shape
    gb = _row_tile(np.gcd(ng_prompt, ng - ng_prompt), 4)
    npb = ng_prompt // gb
    blk = lambda f: pl.BlockSpec((gb, grp, d), f)
    return pl.pallas_call(
        functools.partial(_final_norm_kernel, n_prompt_blocks=npb),
        grid=(ng // gb,),
        in_specs=[blk(lambda i: (i, 0, 0)), pl.BlockSpec((1, d), lambda i: (0, 0))],
        out_specs=[blk(lambda i: (jnp.minimum(i, npb - 1), 0, 0)),
                   blk(lambda i: (jnp.maximum(i - npb, 0), 0, 0))],
        out_shape=[jax.ShapeDtypeStruct((ng_prompt, grp, d), F32),
                   jax.ShapeDtypeStruct((ng - ng_prompt, grp, d), F32)],
        compiler_params=_cparams(1),
    )(x3, g)


def _lanes(c, w):
    if w % LANES == 0:
        return c if w == LANES else jnp.tile(c, (1, w // LANES))
    return c[:, :w]


def _dot_nt(a, b):
    return lax.dot_general(a, b, (((1,), (1,)), ((), ())), preferred_element_type=F32)


def _sb_block(qh, kh, vh, c, u, masked):
    w = kh.shape[0]
    z = _dot_nt(qh, kh)
    sp = jnp.maximum(z, 0.0) + jnp.log2(1.0 + jnp.exp2(-jnp.abs(z)))
    if masked:
        row = lax.broadcasted_iota(jnp.int32, z.shape, 0)
        col = lax.broadcasted_iota(jnp.int32, z.shape, 1)
        valid = col < row
        sp = jnp.where(valid, sp, 0.0)
    hi = sp.astype(BF16)
    lo = (sp - hi.astype(F32)).astype(BF16)
    if w % LANES == 0:
        cs = jnp.dot(jnp.concatenate([hi, lo], axis=1), u, preferred_element_type=F32)
    else:
        cs = (jnp.dot(hi, u[:w], preferred_element_type=F32)
              + jnp.dot(lo, u[w:], preferred_element_type=F32))
    wgt = jnp.exp2(z - sp - cs - _lanes(c, w))
    if masked:
        wgt = jnp.where(valid, wgt, 0.0)
    o = jnp.dot(wgt.astype(BF16), vh, preferred_element_type=F32)
    c_new = c + jnp.sum(sp, axis=1, keepdims=True)
    return o, c_new


def _suffix_matrix(w):
    j = np.arange(w)[:, None]
    s = np.arange(w)[None, :]
    u = (j > s).astype(np.float32)
    return jnp.asarray(np.concatenate([u, u], axis=0), dtype=BF16)


def _sb_stream_kernel(q_ref, kn_ref, vn_ref, k_hbm, v_hbm, ud_ref, up_ref, o_ref,
                      kbuf, vbuf, acc, carry, done, alive_ref, sem, *, heads, tk, cache_tiles,
                      cache_base):
    i = pl.program_id(0)
    n_past = i if cache_tiles is None else cache_tiles

    def fetch(j, slot):
        if cache_tiles is None:
            rows = pl.ds(pl.multiple_of(j * tk, tk), tk)
            src_k, src_v = k_hbm.at[:, rows, :], v_hbm.at[:, rows, :]
        else:
            blk = tk * heads
            rows = pl.ds(pl.multiple_of((cache_base + i * cache_tiles + j) * blk, blk), blk)
            src_k, src_v = k_hbm.at[rows], v_hbm.at[rows]
        return (pltpu.make_async_copy(src_k, kbuf.at[slot], sem.at[0, slot]),
                pltpu.make_async_copy(src_v, vbuf.at[slot], sem.at[1, slot]))

    def tile_of(buf, slot, h):
        if cache_tiles is None:
            return buf[slot, h]
        return buf.at[slot][pl.ds(h, tk, stride=heads), :].astype(BF16)

    @pl.when(n_past > 0)
    def _():
        for c in fetch(n_past - 1, 0):
            c.start()

    def diag(h, _):
        o, c = _sb_block(q_ref[h], kn_ref[h], vn_ref[h], jnp.zeros(carry.shape[1:], F32),
                         ud_ref[...], True)
        acc[h] = o
        carry[h] = c
        done[h] = 0
        return 0
    lax.fori_loop(0, heads, diag, 0, unroll=min(heads, HEAD_UNROLL))

    alive_ref[0] = heads

    @pl.when(n_past > 0)
    def _():
        for c in fetch(n_past - 1, 0):
            c.wait()

        @pl.when(n_past > 1)
        def _():
            for c in fetch(n_past - 2, 1):
                c.start()

        def first(h, alive):
            o, c = _sb_block(q_ref[h], tile_of(kbuf, 0, h), tile_of(vbuf, 0, h), carry[h],
                             up_ref[...], False)
            acc[h] = acc[h] + o
            carry[h] = c
            dead = (jnp.min(c) >= SB_DEAD).astype(jnp.int32)
            done[h] = dead
            return alive + 1 - dead
        alive_ref[0] = lax.fori_loop(0, heads, first, 0, unroll=min(heads, HEAD_UNROLL))

    def cond(state):
        j, alive = state
        return (j >= 0) & (alive > 0)

    def body(state):
        j, _ = state
        slot = lax.rem(n_past - 1 - j, 2)
        for c in fetch(j, slot):
            c.wait()

        @pl.when(j > 0)
        def _():
            for c in fetch(j - 1, 1 - slot):
                c.start()

        def head(h, alive):
            @pl.when(done[h] == 0)
            def _():
                o, c = _sb_block(q_ref[h], tile_of(kbuf, slot, h), tile_of(vbuf, slot, h),
                                 carry[h], up_ref[...], False)
                acc[h] = acc[h] + o
                carry[h] = c
                done[h] = (jnp.min(c) >= SB_DEAD).astype(jnp.int32)
            return alive + 1 - done[h]
        return j - 1, lax.fori_loop(0, heads, head, 0)

    j_end, _ = lax.while_loop(cond, body, (jnp.int32(n_past - 2), alive_ref[0]))

    @pl.when(j_end >= 0)
    def _():
        for c in fetch(j_end, lax.rem(n_past - 1 - j_end, 2)):
            c.wait()

    for h in range(heads):
        o_ref[:, h * DH_SB:(h + 1) * DH_SB] = acc[h].astype(o_ref.dtype)


def _sb_stream_attention(q, kn, vn, k_src, v_src, *, tq, tk, q_block0, n_tiles, cache_tiles=None,
                         cache_base=0):
    heads = q.shape[0]
    blk = pl.BlockSpec((heads, tq, DH_SB), lambda i: (0, q_block0 + i, 0))
    anyspec = pl.BlockSpec(memory_space=pl.ANY)
    if cache_tiles is None:
        buf = pltpu.VMEM((2, heads, tk, DH_SB), BF16)
    else:
        buf = pltpu.VMEM((2, tk * heads, DH_SB), F32)
    kern = functools.partial(_sb_stream_kernel, heads=heads, tk=tk, cache_tiles=cache_tiles,
                             cache_base=cache_base)
    return pl.pallas_call(
        kern, grid=(n_tiles,),
        in_specs=[blk, blk, blk, anyspec, anyspec,
                  pl.BlockSpec((2 * tq, tq), lambda i: (0, 0)),
                  pl.BlockSpec((2 * tk, tk), lambda i: (0, 0))],
        out_specs=pl.BlockSpec((tq, heads * DH_SB), lambda i: (i, 0)),
        out_shape=jax.ShapeDtypeStruct((n_tiles * tq, heads * DH_SB), BF16),
        scratch_shapes=[buf, buf,
                        pltpu.VMEM((heads, tq, DH_SB), F32),
                        pltpu.VMEM((heads, tq, LANES), F32),
                        pltpu.SMEM((heads,), jnp.int32),
                        pltpu.SMEM((1,), jnp.int32),
                        pltpu.SemaphoreType.DMA((2, 2))],
        compiler_params=_cparams(1),
    )(q, kn, vn, k_src, v_src, _suffix_matrix(tq), _suffix_matrix(tk))


def _mla_block(qc, kc, va, m, acc, mask):
    s = _dot_nt(qc, kc)
    if mask is not None:
        s = jnp.where(mask, s, -jnp.inf)
    m_new = jnp.maximum(m, jnp.max(s, axis=1, keepdims=True))
    alpha = jnp.exp2(m - m_new)
    p = jnp.exp2(s - _lanes(m_new, s.shape[1]))
    acc_new = _lanes(alpha, acc.shape[1]) * acc + jnp.dot(p.astype(BF16), va,
                                                          preferred_element_type=F32)
    return m_new, acc_new


def _mla_kernel(qb_ref, kb_ref, fl_ref, q_ref, k_ref, v_ref, o_ref, acc, m_sc, *, heads):
    s = pl.program_id(0)
    fl = fl_ref[s]
    tq, tk = q_ref.shape[1], k_ref.shape[1]

    @pl.when((fl & 1) != 0)
    def _():
        m_sc[...] = jnp.full(m_sc.shape, -jnp.inf, F32)
        acc[...] = jnp.zeros(acc.shape, F32)

    def run(mask):
        def body(h, _):
            m, a = _mla_block(q_ref[h], k_ref[h], v_ref[h], m_sc[h], acc[h], mask)
            m_sc[h] = m
            acc[h] = a
            return 0
        lax.fori_loop(0, heads, body, 0, unroll=min(heads, MLA_UNROLL))

    @pl.when((fl & 4) != 0)
    def _():
        row = lax.broadcasted_iota(jnp.int32, (tq, tk), 0) + qb_ref[s] * tq
        col = lax.broadcasted_iota(jnp.int32, (tq, tk), 1) + kb_ref[s] * tk
        run((col // CHUNK) <= (row // CHUNK))

    @pl.when((fl & 4) == 0)
    def _():
        run(None)

    @pl.when((fl & 2) != 0)
    def _():
        for h in range(heads):
            a = acc[h]
            o_ref[:, h * V_DIM:(h + 1) * V_DIM] = (a[:, :V_DIM] / a[:, V_DIM:]).astype(o_ref.dtype)


def _mla_tables(n_rows, tq, tk):
    r = tq // tk
    qb, kb, fl = [], [], []
    for i in range(n_rows // tq):
        tiles = [(i * r + j, 4) for j in range(r)] + [(j, 0) for j in reversed(range(i * r))]
        for idx, (j, f) in enumerate(tiles):
            qb.append(i)
            kb.append(j)
            fl.append(f | (1 if idx == 0 else 0) | (2 if idx == len(tiles) - 1 else 0))
    arr = lambda v: jnp.asarray(np.asarray(v, dtype=np.int32))
    return arr(qb), arr(kb), arr(fl)


def _mla_attention(qc, kc, va, *, n_rows, tq, tk):
    qb, kb, fl = _mla_tables(n_rows, tq, tk)
    h = H_MLA
    wqk = 2 * LANES
    kv_spec = pl.BlockSpec((h, tk, wqk), lambda s, qb, kb, fl: (0, kb[s], 0))
    grid_spec = pltpu.PrefetchScalarGridSpec(
        num_scalar_prefetch=3,
        grid=(qb.shape[0],),
        in_specs=[pl.BlockSpec((h, tq, wqk), lambda s, qb, kb, fl: (0, qb[s], 0)),
                  kv_spec, kv_spec],
        out_specs=pl.BlockSpec((tq, h * V_DIM), lambda s, qb, kb, fl: (qb[s], 0)),
        scratch_shapes=[pltpu.VMEM((h, tq, 2 * V_DIM), F32),
                        pltpu.VMEM((h, tq, LANES), F32)],
    )
    return pl.pallas_call(
        functools.partial(_mla_kernel, heads=h), grid_spec=grid_spec,
        out_shape=jax.ShapeDtypeStruct((n_rows, h * V_DIM), BF16),
        compiler_params=_cparams(1),
    )(qb, kb, fl, qc, kc, va)


def _mla_latent_kernel(qb_ref, kb_ref, fl_ref, q_ref, cn_ref, rn_ref, cp_ref, rp_ref, w_ref,
                       o_ref, qa, qr, acc, m_sc, l_sc, *, heads, pos0):
    s_id = pl.program_id(0)
    fl = fl_ref[s_id]
    is_first = (fl & 1) != 0
    is_last = (fl & 2) != 0
    tq = q_ref.shape[1]
    hw = QK_NOPE + V_DIM

    def step(ck, kr, mask):
        s = _dot_nt(qa[...], ck) + _dot_nt(qr[...], kr)
        if mask is not None:
            s = jnp.where(mask, s, -jnp.inf)
        m_old = m_sc[...]
        m_new = jnp.maximum(m_old, jnp.max(s, axis=1, keepdims=True))
        alpha = jnp.exp2(m_old - m_new)
        p = jnp.exp2(s - m_new[:, :1])
        l_sc[...] = alpha * l_sc[...] + jnp.sum(p, axis=1, keepdims=True)
        acc[...] = alpha[:, :1] * acc[...] + jnp.dot(p.astype(BF16), ck,
                                                     preferred_element_type=F32)
        m_sc[...] = m_new

    @pl.when(is_first)
    def _():
        for h in range(heads):
            qh = q_ref[h]
            w_uk = w_ref[:, h * hw:h * hw + QK_NOPE]
            qa[h * tq:(h + 1) * tq, :] = _dot_nt(qh[:, :QK_NOPE], w_uk).astype(BF16)
            qr[h * tq:(h + 1) * tq, :] = qh[:, QK_NOPE:]
        m_sc[...] = jnp.full(m_sc.shape, -jnp.inf, F32)
        l_sc[...] = jnp.zeros(l_sc.shape, F32)
        acc[...] = jnp.zeros(acc.shape, F32)
        row = lax.broadcasted_iota(jnp.int32, (heads * tq, tq), 0) % tq + pos0
        col = lax.broadcasted_iota(jnp.int32, (heads * tq, tq), 1) + pos0
        step(cn_ref[...], rn_ref[...], (col // CHUNK) <= (row // CHUNK))

    @pl.when(jnp.logical_not(is_first))
    def _():
        step(cp_ref[...].astype(BF16), rp_ref[...], None)

    @pl.when(is_last)
    def _():
        o_lat = (acc[...] / l_sc[...][:, :1]).astype(BF16)
        for h in range(heads):
            w_uv = w_ref[:, h * hw + QK_NOPE:(h + 1) * hw]
            o_ref[:, h * V_DIM:(h + 1) * V_DIM] = jnp.dot(
                o_lat[h * tq:(h + 1) * tq, :], w_uv, preferred_element_type=F32).astype(o_ref.dtype)


def _mla_latent_attention(qc, c_new, r_new, c_past, r_past, w_ukv_b, tables, l, *, tq, tk, pos0,
                          out_rows, out_block_of):
    qb, kb, fl = tables
    h = H_MLA
    kvl = c_new.shape[1]
    wqk = 2 * LANES
    grid_spec = pltpu.PrefetchScalarGridSpec(
        num_scalar_prefetch=3,
        grid=(qb.shape[0],),
        in_specs=[pl.BlockSpec((h, tq, wqk), lambda s, qb, kb, fl: (0, qb[s], 0)),
                  pl.BlockSpec((tq, kvl), lambda s, qb, kb, fl: (qb[s], 0)),
                  pl.BlockSpec((tq, LANES), lambda s, qb, kb, fl: (qb[s], 0)),
                  pl.BlockSpec((tk, kvl), lambda s, qb, kb, fl: (kb[s], 0)),
                  pl.BlockSpec((tk, LANES), lambda s, qb, kb, fl: (kb[s], 0)),
                  pl.BlockSpec((None, kvl, w_ukv_b.shape[2]), lambda s, qb, kb, fl: (l, 0, 0))],
        out_specs=pl.BlockSpec((tq, h * V_DIM), out_block_of),
        scratch_shapes=[pltpu.VMEM((h * tq, kvl), BF16),
                        pltpu.VMEM((h * tq, LANES), BF16),
                        pltpu.VMEM((h * tq, kvl), F32),
                        pltpu.VMEM((h * tq, LANES), F32),
                        pltpu.VMEM((h * tq, LANES), F32)],
    )
    kern = functools.partial(_mla_latent_kernel, heads=h, pos0=pos0)
    return pl.pallas_call(
        kern, grid_spec=grid_spec,
        out_shape=jax.ShapeDtypeStruct((out_rows, h * V_DIM), BF16),
        compiler_params=_cparams(1),
    )(qb, kb, fl, qc, c_new, r_new, c_past, r_past, w_ukv_b)


def _cache_walk_tables(n_batch, q_off, past_tiles, past_stride):
    qb, kb, fl = [], [], []
    for b in range(n_batch):
        for j in range(1 + past_tiles):
            qb.append(q_off + b)
            kb.append(past_stride + b * past_tiles + past_tiles - max(j, 1))
            fl.append((1 if j == 0 else 0) | (2 if j == past_tiles else 0))
    arr = lambda v: jnp.asarray(np.asarray(v, dtype=np.int32))
    return arr(qb), arr(kb), arr(fl)


ROW_DMA_UNROLL = 8


def _moe_kernel(te_ref, nv_ref, tok_ref, dst_ref, h_hbm, w_ref, wg_ref, wu_ref, wd_ref, y_hbm,
                xbuf, obuf, wg_s, wu_s, wd_s, sem_in, sem_out, *, s):
    t = pl.program_id(0)
    tm = w_ref.shape[0]
    nv = nv_ref[0]
    live = t < nv
    slot = lax.rem(t, 2)

    p = _slab_pitch(s)

    def in_copy(tile, r, sl):
        src0 = pl.multiple_of(tok_ref[tile * tm + r] * p, SLAB_PAD)
        return pltpu.make_async_copy(h_hbm.at[pl.ds(src0, s)],
                                     xbuf.at[sl, pl.ds(pl.multiple_of(r * p, SLAB_PAD), s)],
                                     sem_in.at[sl])

    def gather_start(tile, sl):
        def body(r, _):
            in_copy(tile, 2 * r, sl).start(priority=0)
            in_copy(tile, 2 * r + 1, sl).start(priority=1)
            return 0
        lax.fori_loop(0, tm // 2, body, 0, unroll=ROW_DMA_UNROLL // 2)

    def gather_wait(tile, sl):
        def body(r, _):
            in_copy(tile, r, sl).wait()
            return 0
        lax.fori_loop(0, tm, body, 0, unroll=ROW_DMA_UNROLL)

    @pl.when(live & (t == 0))
    def _():
        gather_start(0, 0)

    @pl.when(t + 1 < nv)
    def _():
        gather_start(t + 1, 1 - slot)

    prev = te_ref[jnp.maximum(t - 1, 0)]
    fresh = (t == 0) | (te_ref[t] != prev)

    @pl.when(live & fresh)
    def _():
        wg_s[...] = wg_ref[...].astype(BF16)
        wu_s[...] = wu_ref[...].astype(BF16)
        wd_s[...] = wd_ref[...].astype(BF16)

    @pl.when(live)
    def _():
        gather_wait(t, slot)
        x = _slab_load(xbuf, 0, tm, s, lead=slot).astype(BF16)
        a = jnp.dot(x, wg_s[...], preferred_element_type=F32)
        u = jnp.dot(x, wu_s[...], preferred_element_type=F32)
        hid = (a * jax.nn.sigmoid(a)) * u * w_ref[...]
        y = jnp.dot(hid.astype(BF16), wd_s[...], preferred_element_type=F32)

        def out_copy(tile, r):
            dst0 = pl.multiple_of(dst_ref[tile * tm + r] * p, SLAB_PAD)
            return pltpu.make_async_copy(obuf.at[pl.ds(pl.multiple_of(r * p, SLAB_PAD), s)],
                                         y_hbm.at[pl.ds(dst0, s)], sem_out)

        def scatter_wait(tile):
            def wait(r, _):
                out_copy(tile, r).wait()
                return 0
            lax.fori_loop(0, tm, wait, 0, unroll=ROW_DMA_UNROLL)

        @pl.when(t > 0)
        def _():
            scatter_wait(t - 1)

        _slab_store(obuf, 0, y)

        def start(r, _):
            out_copy(t, 2 * r).start(priority=0)
            out_copy(t, 2 * r + 1).start(priority=1)
            return 0
        lax.fori_loop(0, tm // 2, start, 0, unroll=ROW_DMA_UNROLL // 2)

        @pl.when(t == nv - 1)
        def _():
            scatter_wait(t)


def _moe_experts(h_slab, row_tok, row_dst, row_w, tile_expert, n_valid, w_gate, w_up, w_down, l,
                 n_dest):
    d, f = w_gate.shape[-2:]
    s = d // LANES
    p = _slab_pitch(s)
    tm = MOE_TILE
    n_tiles = row_tok.shape[0] // tm
    wmap = lambda t, te, nv, tok, dst: (l, te[t], 0, 0)
    grid_spec = pltpu.PrefetchScalarGridSpec(
        num_scalar_prefetch=4, grid=(n_tiles,),
        in_specs=[pl.BlockSpec(memory_space=pl.ANY),
                  pl.BlockSpec((tm, 1), lambda t, te, nv, tok, dst: (t, 0)),
                  pl.BlockSpec((None, None, d, f), wmap),
                  pl.BlockSpec((None, None, d, f), wmap),
                  pl.BlockSpec((None, None, f, d), wmap)],
        out_specs=pl.BlockSpec(memory_space=pl.ANY),
        scratch_shapes=[pltpu.VMEM((2, tm * p, LANES), F32), pltpu.VMEM((tm * p, LANES), F32),
                        pltpu.VMEM((d, f), BF16), pltpu.VMEM((d, f), BF16),
                        pltpu.VMEM((f, d), BF16),
                        pltpu.SemaphoreType.DMA((2,)), pltpu.SemaphoreType.DMA(())],
    )
    return pl.pallas_call(
        functools.partial(_moe_kernel, s=s), grid_spec=grid_spec,
        out_shape=jax.ShapeDtypeStruct(((n_dest + tm) * p, LANES), F32),
        compiler_params=_cparams(1),
    )(tile_expert, n_valid, row_tok, row_dst, h_slab, row_w, w_gate, w_up, w_down)


def _combine_kernel(x_ref, y0_ref, y1_ref, g_ref, o_ref):
    gb, grp, d = x_ref.shape
    s = d // LANES
    g0 = pl.program_id(0) * gb
    for g in range(gb):
        y = _slab_load(y0_ref, g * grp, grp, s) + _slab_load(y1_ref, g * grp, grp, s)
        o_ref[g] = x_ref[g] + _mod_row(g_ref, g0 + g) * y


def _combine(x3, yg, modg, l, g_i):
    ng, grp, d = x3.shape
    s = d // LANES
    gb = _row_tile(ng, 2)
    blk = pl.BlockSpec((gb, grp, d), lambda i: (i, 0, 0))
    yblk = lambda off: pl.BlockSpec((gb * grp * _slab_pitch(s), LANES), lambda i: (i + off, 0))
    return pl.pallas_call(
        _combine_kernel,
        grid=(ng // gb,),
        in_specs=[blk, yblk(0), yblk(ng // gb), _mod_spec(g_i, ng, d)],
        out_specs=blk,
        out_shape=jax.ShapeDtypeStruct(x3.shape, F32),
        compiler_params=_cparams(1),
    )(x3, yg, yg, modg)


def _dispatch(route, n):
    tm = MOE_TILE
    r_cap = (2 * n + N_EXPERTS * (tm - 1) + tm - 1) // tm * tm
    e = route[:, :2].astype(jnp.int32)
    w = route[:, 2:4]
    flat_e = e.T.reshape(-1)
    flat_w = w.T.reshape(-1)
    order = jnp.argsort(flat_e, stable=True).astype(jnp.int32)
    bounds = jnp.searchsorted(flat_e[order], jnp.arange(N_EXPERTS + 1, dtype=jnp.int32),
                              side="left").astype(jnp.int32)
    counts = bounds[1:] - bounds[:-1]
    padded = (counts + tm - 1) // tm * tm
    ends_p = jnp.cumsum(padded)
    starts_p = ends_p - padded
    starts = jnp.cumsum(counts) - counts
    tile_start = jnp.arange(r_cap // tm, dtype=jnp.int32) * tm
    tile_expert = jnp.minimum(jnp.searchsorted(ends_p, tile_start, side="right"),
                              N_EXPERTS - 1).astype(jnp.int32)
    n_valid = (ends_p[-1:] // tm).astype(jnp.int32)
    per_row = lambda v: jnp.broadcast_to(v[:, None], (r_cap // tm, tm)).reshape(r_cap)
    local = jnp.arange(r_cap, dtype=jnp.int32) - per_row(starts_p[tile_expert])
    valid = local < per_row(counts[tile_expert])
    slot = order[jnp.clip(per_row(starts[tile_expert]) + local, 0, 2 * n - 1)]
    row_dst = jnp.where(valid, slot, 2 * n + jnp.arange(r_cap, dtype=jnp.int32) % tm)
    row_tok = jnp.where(valid, jnp.where(slot >= n, slot - n, slot), 0)
    row_w = jnp.where(valid, flat_w[slot], 0.0)
    return row_tok, row_dst, row_w.reshape(r_cap, 1), tile_expert, n_valid


def _gates_kernel(h_ref, wa_ref, wb_ref, o_ref, bsc):
    @pl.when(pl.program_id(1) == 0)
    def _():
        r = wb_ref.shape[0]
        w = wa_ref.shape[0]
        bsc[:w - r, :] = wa_ref[r:, :].astype(BF16)
        bsc[w - r:, :] = wb_ref[...].astype(BF16)

    o_ref[...] = jax.nn.sigmoid(_dot_nt(h_ref[...], bsc[...]))


def _gates(h, w_nk, l, off_kr, n_gate, tm, tn):
    n, k = h.shape
    assert off_kr % tn == 0 and n_gate % tn == 0 and tn % QK_ROPE == 0
    rb = off_kr // tn
    return pl.pallas_call(
        _gates_kernel,
        grid=(n_gate // tn, n // tm),
        in_specs=[pl.BlockSpec((tm, k), lambda j, i: (i, 0)),
                  pl.BlockSpec((None, tn, k), lambda j, i: (l, rb + j, 0)),
                  pl.BlockSpec((None, QK_ROPE, k),
                               lambda j, i: (l, (rb + j + 1) * (tn // QK_ROPE), 0))],
        out_specs=pl.BlockSpec((tm, tn), lambda j, i: (i, j)),
        out_shape=jax.ShapeDtypeStruct((n, n_gate), F32),
        scratch_shapes=[pltpu.VMEM((tn, k), BF16)],
        compiler_params=_cparams(2),
    )(h, w_nk, w_nk)


def _rope_weights_kernel(a_ref, o_ref):
    a = a_ref[...].astype(BF16)
    q = a.shape[0] // 2
    o_ref[...] = jnp.concatenate([a, a[q:], a[:q]], axis=0)


def _rope_key_weights(w_nk, off_kr):
    depth, n_in, k = w_nk.shape
    assert QK_ROPE * 2 == LANES and off_kr % QK_ROPE == 0
    return pl.pallas_call(
        _rope_weights_kernel,
        grid=(depth,),
        in_specs=[pl.BlockSpec((None, QK_ROPE, k), lambda l: (l, off_kr // QK_ROPE, 0))],
        out_specs=pl.BlockSpec((None, 2 * QK_ROPE, k), lambda l: (l, 0, 0)),
        out_shape=jax.ShapeDtypeStruct((depth, 2 * QK_ROPE, k), BF16),
        compiler_params=_cparams(1),
    )(w_nk)


def _rope_tables(pos):
    inv = ROPE_THETA ** (-jnp.arange(0, QK_ROPE, 2, dtype=F32) / QK_ROPE)
    ang = pos.astype(F32)[:, None] * inv[None, :]
    cos, sin = jnp.cos(ang), jnp.sin(ang)
    pad = jnp.zeros((pos.shape[0], LANES - QK_ROPE), F32)
    return (jnp.concatenate([cos, cos, pad], axis=1),
            jnp.concatenate([-sin, sin, pad], axis=1))


def _rope_lanes(acc, c, s):
    return acc * c + pltpu.roll(acc, LANES - QK_ROPE, axis=1) * s


def _swap_halves(w):
    half = w.shape[-1] // 2
    return jnp.concatenate([w[..., half:], w[..., :half]], axis=-1)


def kernel(x_prompt, x_sample, c_prompt, c_sample, cache_sb_k, cache_sb_v, cache_mla_ckv,
           cache_mla_krope, w_ada, b_ada, g_norm_mix, g_norm_ffn, w_in, g_q_lat, g_kv_lat,
           w_uq, w_ukv, w_branch_sb, w_branch_mla, w_out, w_router_group, b_router_group,
           w_router_expert, b_router_expert, w_exp_gate, w_exp_up, w_exp_down, g_final):
    bp, t_p, d = x_prompt.shape
    bs, t_s, _ = x_sample.shape
    depth = w_in.shape[0]
    past = cache_sb_k.shape[2]
    grp = t_s
    n_p, n_s = bp * t_p, bs * t_s
    n = n_p + n_s
    ng = n // grp
    sb_w = H_SB * DH_SB
    q_lora = g_q_lat.shape[1]
    kv_lora = g_kv_lat.shape[1]
    tm = _token_tile(n, grp)
    gpt = tm // grp
    tn = 512
    assert bp == 1 and t_p % ATTN_TILE == 0 and t_p % grp == 0 and past % 512 == 0

    x3 = jnp.concatenate([x_prompt.reshape(n_p // grp, grp, d), x_sample], axis=0)

    n_c = bp + bs
    c_rows = 16
    c_all = jnp.zeros((c_rows, d), F32).at[:n_c].set(jnp.concatenate([c_prompt, c_sample], 0))
    n_modc = N_MOD * d

    def ada_epi(accs, ex, outs):
        outs[0][...] = accs[0] + ex[0][...]

    mods = []
    for l in range(depth):
        mods.append(_matmul(
            [(c_all, w_ada, _wspec(l, d, tn, 0), True)], m=c_rows, n_out=n_modc, tm=c_rows, tn=tn,
            prologue=lambda a, ex: a * jax.nn.sigmoid(a),
            epilogue=ada_epi,
            extras=[b_ada.reshape(depth, 1, n_modc)],
            extra_specs=[pl.BlockSpec((None, 1, tn), lambda j, i, l=l: (l, 0, j))],
            out_shape=[jax.ShapeDtypeStruct((c_rows, n_modc), F32)],
            out_specs=[_spec2(c_rows, tn)])[0])
    mod = jnp.stack(mods)
    modg = jnp.concatenate(
        [jnp.broadcast_to(mod[:, :bp], (depth, n_p // grp, n_modc)), mod[:, bp:n_c]], axis=1)
    modg = modg.reshape(depth, ng, N_MOD, d).transpose(0, 2, 1, 3)
    modg = modg.reshape(depth * N_MOD, ng, d)

    def mod_idx(l, k):
        return l * N_MOD + k

    def mspec_rows(l, k):
        return pl.BlockSpec((None, ng, tn), lambda j, i: (mod_idx(l, k), 0, j))

    pos = jnp.concatenate([jnp.arange(t_p, dtype=jnp.int32),
                           jnp.tile(past + jnp.arange(t_s, dtype=jnp.int32), bs)])
    rope_c, rope_s = _rope_tables(pos)
    tq = ATTN_TILE
    tk_s = _row_tile(past, 1024)
    pt = past // tk_s

    off_q, off_k, off_v = 0, sb_w, 2 * sb_w
    off_cq = 3 * sb_w
    off_ckv = off_cq + q_lora
    off_kr = off_ckv + kv_lora
    off_g = off_kr + QK_ROPE
    assert off_g == off_kr + QK_ROPE
    w_nk = jnp.swapaxes(w_in, 1, 2)
    w_kr_aug = _rope_key_weights(w_nk, off_kr)
    uq = w_uq.reshape(depth, q_lora, H_MLA, QK_NOPE + QK_ROPE)
    uq_r = uq[..., QK_NOPE:]
    w_uq_cat = jnp.concatenate([uq[..., :QK_NOPE], uq_r, _swap_halves(uq_r)], axis=-1)
    w_uq_cat = w_uq_cat.reshape(depth, q_lora, H_MLA * 2 * LANES)
    w_router = jnp.concatenate(
        [w_router_expert, w_router_group,
         jnp.zeros((depth, d, LANES - N_EXPERTS - N_GROUPS), F32)], axis=-1)
    r_hi = w_router.astype(BF16)
    r_lo = (w_router - r_hi.astype(F32)).astype(BF16)
    w_router3 = jnp.stack([r_hi, r_lo], axis=1)
    b_router = jnp.concatenate(
        [b_router_expert, b_router_group,
         jnp.zeros((depth, LANES - N_EXPERTS - N_GROUPS), F32)], axis=-1).reshape(depth, 1, LANES)

    rows_past = bs * past
    kr_past = jnp.pad(cache_mla_krope.reshape(depth * rows_past, QK_ROPE),
                      ((0, 0), (0, LANES - QK_ROPE))).astype(BF16)
    w_ukv_b = w_ukv.astype(BF16)
    hm = lambda rows: jax.ShapeDtypeStruct((H_SB, rows, LANES), BF16)
    new_k, new_v, new_c, new_r = [], [], [], []

    for l in range(depth):
        h = _norm_mod(x3, g_norm_mix.reshape(depth, 1, d), modg, l, mod_idx(l, 1), mod_idx(l, 0))
        h = h.reshape(n, d)

        sbq_scale = DH_SB ** -0.5 * float(np.log2(np.e))

        def plain_hm(accs, ex, outs):
            _store_heads(outs[0], accs[0] * sbq_scale)

        def f32_and_hm(accs, ex, outs):
            outs[0][...] = accs[0]
            _store_heads(outs[1], accs[0])

        sb_q = _matmul([(h, w_nk, _wspec_nk(l, d, tn, off_q), True, True)], m=n, n_out=sb_w, tm=tm, tn=tn,
                       epilogue=plain_hm, out_shape=[hm(n)], out_specs=[_hm_spec(tm, tn)])[0]
        k_f32, sb_k = _matmul([(h, w_nk, _wspec_nk(l, d, tn, off_k), True, True)], m=n, n_out=sb_w, tm=tm,
                              tn=tn, epilogue=f32_and_hm,
                              out_shape=[jax.ShapeDtypeStruct((n, sb_w), F32), hm(n)],
                              out_specs=[_spec2(tm, tn), _hm_spec(tm, tn)])
        v_f32, sb_v = _matmul([(h, w_nk, _wspec_nk(l, d, tn, off_v), True, True)], m=n, n_out=sb_w, tm=tm,
                              tn=tn, epilogue=f32_and_hm,
                              out_shape=[jax.ShapeDtypeStruct((n, sb_w), F32), hm(n)],
                              out_specs=[_spec2(tm, tn), _hm_spec(tm, tn)])

        def plain_f32(accs, ex, outs):
            outs[0][...] = accs[0]

        c_q = _matmul([(h, w_nk, _wspec_nk(l, d, tn, off_cq), True, True)], m=n, n_out=q_lora, tm=tm, tn=tn,
                      epilogue=plain_f32, out_shape=[jax.ShapeDtypeStruct((n, q_lora), F32)],
                      out_specs=[_spec2(tm, tn)])[0]

        def ckv_epi(accs, ex, outs):
            y = _rms(accs[0], ex[0][...])
            outs[0][...] = y
            outs[1][...] = y.astype(BF16)

        c_kv, c_kv_b = _matmul(
            [(h, w_nk, _wspec_nk(l, d, kv_lora, off_ckv), True, True)], m=n, n_out=kv_lora, tm=tm, tn=kv_lora,
            epilogue=ckv_epi, extras=[g_kv_lat.reshape(depth, 1, kv_lora)],
            extra_specs=[pl.BlockSpec((None, 1, kv_lora), lambda j, i: (l, 0, 0))],
            out_shape=[jax.ShapeDtypeStruct((n, kv_lora), F32),
                       jax.ShapeDtypeStruct((n, kv_lora), BF16)],
            out_specs=[_spec2(tm, kv_lora), _spec2(tm, kv_lora)])

        def kr_epi(accs, ex, outs):
            r = _rope_lanes(accs[0], ex[0][...], ex[1][...])
            outs[0][...] = r
            outs[1][...] = r.astype(BF16)

        rope_specs = [pl.BlockSpec((tm, LANES), lambda j, i: (i, 0))] * 2
        k_r, k_r_b = _matmul(
            [(h, w_kr_aug, pl.BlockSpec((None, LANES, d), lambda j, i: (l, 0, 0)), False, True)],
            m=n, n_out=LANES, tm=tm, tn=LANES, epilogue=kr_epi,
            extras=[rope_c, rope_s], extra_specs=rope_specs,
            out_shape=[jax.ShapeDtypeStruct((n, LANES), F32), jax.ShapeDtypeStruct((n, LANES), BF16)],
            out_specs=[_spec2(tm, LANES), _spec2(tm, LANES)])

        gates = _gates(h, w_nk, l, off_kr, 2 * d, tm, tn)

        def cq_prologue(a, ex):
            return _rms(a, ex[0][...])

        gq_spec = pl.BlockSpec((None, 1, q_lora), lambda j, i: (l, 0, 0))
        gq = g_q_lat.reshape(depth, 1, q_lora)
        wqk = 2 * LANES
        hm_qk = lambda rows: jax.ShapeDtypeStruct((H_MLA, rows, wqk), BF16)

        q_scale = (QK_NOPE + QK_ROPE) ** -0.5 * float(np.log2(np.e))

        tn_kv = min(2048, H_MLA * wqk)

        def qcat_epi(accs, ex, outs):
            c, s = ex[1][...], ex[2][...]
            for hh in range(tn_kv // wqk):
                blk = accs[0][:, hh * wqk:(hh + 1) * wqk]
                outs[0][hh, :, :LANES] = (blk[:, :LANES] * q_scale).astype(BF16)
                outs[0][hh, :, LANES:] = (_rope_lanes(blk[:, LANES:], c, s) * q_scale).astype(BF16)

        q_cat = _matmul(
            [(c_q, w_uq_cat, pl.BlockSpec((None, q_lora, tn_kv), lambda j, i: (l, 0, j)), True)],
            m=n, n_out=H_MLA * wqk, tm=tm, tn=tn_kv, prologue=cq_prologue, epilogue=qcat_epi,
            extras=[gq, rope_c, rope_s], extra_specs=[gq_spec] + rope_specs,
            out_shape=[hm_qk(n)],
            out_specs=[pl.BlockSpec((tn_kv // wqk, tm, wqk), lambda j, i: (j, i, 0))])[0]

        hpt = tn_kv // wqk

        def kv_epi(accs, ex, outs):
            kr = ex[0][...]
            ones = jnp.ones(kr.shape, BF16)
            for hh in range(hpt):
                outs[0][hh, :, :LANES] = accs[0][:, hh * wqk:hh * wqk + LANES].astype(BF16)
                outs[0][hh, :, LANES:] = kr
                outs[1][hh, :, :LANES] = accs[0][:, hh * wqk + LANES:(hh + 1) * wqk].astype(BF16)
                outs[1][hh, :, LANES:] = ones

        def up_kv(a, kr, rows, tmr):
            hspec = pl.BlockSpec((hpt, tmr, wqk), lambda j, i: (j, i, 0))
            return _matmul(
                [(a, w_ukv, pl.BlockSpec((None, kv_lora, tn_kv), lambda j, i: (l, 0, j)), True)],
                m=rows, n_out=H_MLA * wqk, tm=tmr, tn=tn_kv, epilogue=kv_epi, extras=[kr],
                extra_specs=[pl.BlockSpec((tmr, LANES), lambda j, i: (i, 0))],
                out_shape=[hm_qk(rows), hm_qk(rows)], out_specs=[hspec, hspec])

        kc_new, v_new = up_kv(c_kv_b, k_r_b, n, tm)

        o_sb_p = _sb_stream_attention(sb_q, sb_k, sb_v, sb_k, sb_v, tq=tq, tk=tq, q_block0=0,
                                      n_tiles=n_p // tq)
        cache_k2 = cache_sb_k.reshape(depth * bs * past * H_SB, DH_SB)
        cache_v2 = cache_sb_v.reshape(depth * bs * past * H_SB, DH_SB)
        q_off_s = n_p // t_s
        o_sb_s = _sb_stream_attention(sb_q, sb_k, sb_v, cache_k2, cache_v2, tq=t_s, tk=ATTN_TILE,
                                      q_block0=q_off_s, n_tiles=bs,
                                      cache_tiles=past // ATTN_TILE,
                                      cache_base=l * bs * (past // ATTN_TILE))
        o_sb = jnp.concatenate([o_sb_p, o_sb_s], axis=0)

        o_mla_p = _mla_attention(q_cat, kc_new, v_new, n_rows=n_p,
                                 tq=min(MLA_Q_TILE, n_p), tk=min(MLA_Q_TILE, n_p))
        o_mla_s = _mla_latent_attention(
            q_cat, c_kv_b, k_r_b, cache_mla_ckv.reshape(depth * rows_past, kv_lora), kr_past,
            w_ukv_b, _cache_walk_tables(bs, q_off_s, pt, l * bs * pt), l,
            tq=t_s, tk=tk_s, pos0=past, out_rows=n_s,
            out_block_of=lambda s, qb, kb, fl: (qb[s] - q_off_s, 0))
        o_mla = jnp.concatenate([o_mla_p, o_mla_s], axis=0)

        def merge_epi(accs, ex, outs):
            outs[0][...] = (ex[0][...] * accs[0] + ex[1][...] * accs[1]).astype(BF16)

        merged = _matmul(
            [(o_sb, w_branch_sb, _wspec(l, sb_w, tn, 0), True),
             (o_mla, w_branch_mla, _wspec(l, H_MLA * V_DIM, tn, 0), True)],
            m=n, n_out=d, tm=tm, tn=tn, epilogue=merge_epi,
            extras=[gates, gates],
            extra_specs=[pl.BlockSpec((tm, tn), lambda j, i: (i, j)),
                         pl.BlockSpec((tm, tn), lambda j, i: (i, j + d // tn))],
            out_shape=[jax.ShapeDtypeStruct((n, d), BF16)], out_specs=[_spec2(tm, tn)])[0]

        def resid_epi(accs, ex, outs):
            g0 = pl.program_id(1) * gpt
            for g in range(gpt):
                outs[0][g] = ex[0][g] + _mod_row(ex[1], g0 + g) * accs[0][g * grp:(g + 1) * grp, :]

        x_spec = pl.BlockSpec((gpt, grp, tn), lambda j, i: (i, 0, j))
        x3 = _matmul(
            [(merged, w_out, _wspec(l, d, tn, 0), True)], m=n, n_out=d, tm=tm, tn=tn,
            epilogue=resid_epi, extras=[x3, modg], extra_specs=[x_spec, mspec_rows(l, 2)],
            out_shape=[jax.ShapeDtypeStruct((ng, grp, d), F32)], out_specs=[x_spec])[0]

        h2, route = _norm_route(x3, g_norm_ffn.reshape(depth, 1, d), modg, l, mod_idx(l, 4),
                                mod_idx(l, 3), w_router3, b_router)
        row_tok, row_dst, row_w, tile_expert, n_valid = _dispatch(route.reshape(n, LANES), n)
        yg = _moe_experts(h2, row_tok, row_dst, row_w, tile_expert, n_valid,
                          w_exp_gate, w_exp_up, w_exp_down, l, 2 * n)
        x3 = _combine(x3, yg, modg, l, mod_idx(l, 5))

        new_k.append(k_f32)
        new_v.append(v_f32)
        new_c.append(c_kv)
        new_r.append(k_r[:, :QK_ROPE])

    y_p, y_s = _final_norm(x3, g_final.reshape(1, d), n_p // grp)

    def split(parts, tail):
        a = jnp.stack(parts)
        return (a[:, :n_p].reshape((depth, bp, t_p) + tail),
                a[:, n_p:].reshape((depth, bs, t_s) + tail))

    pk, sk = split(new_k, (H_SB, DH_SB))
    pv, sv = split(new_v, (H_SB, DH_SB))
    pc, sc = split(new_c, (kv_lora,))
    pr, sr = split(new_r, (QK_ROPE,))
    return (y_p.reshape(bp, t_p, d), y_s.reshape(bs, t_s, d), pk, pv, pc, pr, sk, sv, sc, sr)
```
